```python
import math
import jax
import jax.numpy as jnp
from jax import lax
import numpy as np

D_MODEL = 2048
BATCH = 2
SEQ = 4096
DEPTH = 1
DEC_BATCH = 8
DEC_SEQ = 32
PAST_LEN = 4096

CHUNK = 64
RMS_EPS = 1e-6
GDN_HEADS = 8
GDN_HEAD_DIM = 128
GDN_WIDTH = GDN_HEADS * GDN_HEAD_DIM
GDN_CONV = 4
GDN_CONV_DIM = 3 * GDN_WIDTH
DIFF_HEADS = 8
DIFF_HEAD_DIM = 64
DIFF_V_DIM = 2 * DIFF_HEAD_DIM
DIFF_QK_WIDTH = DIFF_HEADS * 2 * DIFF_HEAD_DIM
DIFF_WIDTH = DIFF_HEADS * DIFF_V_DIM
ATTN_BLOCK = 128
MEM_TOKENS = 256
MEM_HEADS = 4
MEM_HEAD_DIM = D_MODEL // MEM_HEADS
PEER_HEADS = 8
PEER_N_KEYS = 128
PEER_EXPERTS = PEER_N_KEYS * PEER_N_KEYS
PEER_TOPK = 16
PEER_QUERY_DIM = 256
PEER_HALF = PEER_QUERY_DIM // 2
PEER_BLOCK = 128
IN_SIZES = (GDN_CONV_DIM, GDN_WIDTH, GDN_HEADS, GDN_HEADS, DIFF_QK_WIDTH, DIFF_QK_WIDTH, DIFF_WIDTH, D_MODEL, D_MODEL)
IN_COLS = sum(IN_SIZES)

kernel_name = 'hybrid_stream_gdn_diffattn_peer_step'


def rmsnorm(x, g):
    xf = x.astype(jnp.float32)
    y = xf * lax.rsqrt(jnp.mean(xf * xf, axis=-1, keepdims=True) + RMS_EPS)
    return (y * g.astype(jnp.float32)).astype(x.dtype)


def l2norm(x):
    xf = x.astype(jnp.float32)
    return xf * lax.rsqrt(jnp.sum(xf * xf, axis=-1, keepdims=True) + 1e-6)


def alibi_slopes(n):
    return jnp.asarray(np.array([2.0 ** (-8.0 * (i + 1) / n) for i in range(n)], dtype=np.float32))


def causal_short_conv(x, prev, w):
    t = x.shape[1]
    xp = jnp.concatenate([prev.astype(x.dtype), x], axis=1)
    y = xp[:, 0:t] * w[0]
    for i in range(1, GDN_CONV):
        y = y + xp[:, i:i + t] * w[i]
    return jax.nn.silu(y), xp[:, t:]


def gated_delta_rule(q, k, v, g, beta, s0, chunk):
    bsz, t, nh, dk = q.shape
    dv = v.shape[-1]
    n = t // chunk

    def blocks(a):
        a = a.reshape((bsz, n, chunk) + a.shape[2:])
        return jnp.moveaxis(a, 2, 3)

    qc, kc, vc, gc, bc = (blocks(a) for a in (q, k, v, g, beta))
    G = jnp.cumsum(gc, axis=-1)
    i = jnp.arange(chunk)
    lower = i[:, None] >= i[None, :]
    strict = i[:, None] > i[None, :]
    gdiff = G[..., :, None] - G[..., None, :]
    decay = jnp.where(lower, jnp.exp(jnp.where(lower, gdiff, 0.0)), 0.0)
    kk = jnp.einsum('bnhid,bnhjd->bnhij', kc, kc)
    a_mat = jnp.where(strict, bc[..., :, None] * kk * decay, 0.0)
    rhs = jnp.concatenate([vc * bc[..., None], kc * (bc * jnp.exp(G))[..., None]], axis=-1)
    sol = lax.linalg.triangular_solve(a_mat, rhs, left_side=True, lower=True, unit_diagonal=True)
    u, w = sol[..., :dv], sol[..., dv:]
    qk = jnp.where(lower, jnp.einsum('bnhid,bnhjd->bnhij', qc, kc) * decay, 0.0)
    q_dec = qc * jnp.exp(G)[..., None]
    k_dec = kc * jnp.exp(G[..., -1:] - G)[..., None]
    g_end = jnp.exp(G[..., -1])

    def step(s, xs):
        u_c, w_c, qk_c, q_c, k_c, ge = xs
        v_new = u_c - jnp.einsum('bhld,bhdv->bhlv', w_c, s)
        o = jnp.einsum('bhld,bhdv->bhlv', q_c, s) + jnp.einsum('bhij,bhjv->bhiv', qk_c, v_new)
        s = s * ge[..., None, None] + jnp.einsum('bhld,bhlv->bhdv', k_c, v_new)
        return s, o

    xs = tuple(jnp.moveaxis(a, 1, 0) for a in (u, w, qk, q_dec, k_dec, g_end))
    s_fin, o = lax.scan(step, s0, xs)
    o = jnp.transpose(o, (1, 0, 3, 2, 4)).reshape(bsz, t, nh, dv)
    return o, s_fin


def gdn_branch(qkv, z, b_raw, a_raw, conv_prev, s0, conv_w, a_log, dt_bias, norm_g):
    bsz, t, _ = qkv.shape
    c, conv_new = causal_short_conv(qkv, conv_prev, conv_w)
    q, k, v = jnp.split(c, 3, axis=-1)
    hd = (bsz, t, GDN_HEADS, GDN_HEAD_DIM)
    q = l2norm(q.reshape(hd)) * (GDN_HEAD_DIM ** -0.5)
    k = l2norm(k.reshape(hd))
    v = v.reshape(hd).astype(jnp.float32)
    beta = jax.nn.sigmoid(b_raw.astype(jnp.float32))
    g = -jnp.exp(a_log.astype(jnp.float32)) * jax.nn.softplus(a_raw.astype(jnp.float32) + dt_bias.astype(jnp.float32))
    o, s_new = gated_delta_rule(q, k, v, g, beta, s0.astype(jnp.float32), min(CHUNK, t))
    o = rmsnorm(o, norm_g) * jax.nn.silu(z.reshape(hd).astype(jnp.float32))
    return o.reshape(bsz, t, GDN_WIDTH).astype(qkv.dtype), conv_new, s_new.astype(qkv.dtype)


def diff_attend(q, k, v, q_pos, k_pos, lam):
    slopes = alibi_slopes(DIFF_HEADS)
    s = jnp.einsum('bqhmd,bkhmd->bhmqk', q, k).astype(jnp.float32) * (DIFF_HEAD_DIM ** -0.5)
    dist = jnp.abs(q_pos[:, None] - k_pos[None, :]).astype(jnp.float32)
    visible = (k_pos[None, :] // CHUNK) <= (q_pos[:, None] // CHUNK)
    s = jnp.where(visible, s - slopes[:, None, None, None] * dist, -jnp.inf)
    p = jax.nn.softmax(s, axis=-1)
    a = p[:, :, 0] - lam * p[:, :, 1]
    return jnp.einsum('bhqk,bkhe->bqhe', a, v.astype(jnp.float32))


def diff_branch(q, k_rows, v_rows, k_past, v_past, lam):
    bsz, t = q.shape[:2]
    q = q.reshape(bsz, t, DIFF_HEADS, 2, DIFF_HEAD_DIM)
    if k_past is None:
        k = k_rows.reshape(bsz, t, DIFF_HEADS, 2, DIFF_HEAD_DIM)
        pos = jnp.arange(t, dtype=jnp.int32)
        nblk = t // ATTN_BLOCK
        qb = jnp.moveaxis(q.reshape(bsz, nblk, ATTN_BLOCK, DIFF_HEADS, 2, DIFF_HEAD_DIM), 1, 0)
        ob = lax.map(lambda a: diff_attend(a[0], k, v_rows, a[1], pos, lam), (qb, pos.reshape(nblk, ATTN_BLOCK)))
        return jnp.moveaxis(ob, 0, 1).reshape(bsz, t, DIFF_HEADS, DIFF_V_DIM)
    past = k_past.shape[1]
    k = jnp.concatenate([k_past.astype(k_rows.dtype), k_rows], axis=1).reshape(bsz, past + t, DIFF_HEADS, 2, DIFF_HEAD_DIM)
    v = jnp.concatenate([v_past.astype(v_rows.dtype), v_rows], axis=1)
    q_pos = past + jnp.arange(t, dtype=jnp.int32)
    k_pos = jnp.arange(past + t, dtype=jnp.int32)
    return diff_attend(q, k, v, q_pos, k_pos, lam)


def memory_kv(mem, norm_g, w_mk, w_mv):
    hm = rmsnorm(mem, norm_g)
    shp = mem.shape[:2] + (MEM_HEADS, MEM_HEAD_DIM)
    return (hm @ w_mk).reshape(shp), (hm @ w_mv).reshape(shp)


def cross_attend(h, mem_k, mem_v, w_mq, w_mo):
    bsz, t, _ = h.shape
    q = (h @ w_mq).reshape(bsz, t, MEM_HEADS, MEM_HEAD_DIM)
    s = jnp.einsum('bqhd,bkhd->bhqk', q, mem_k.astype(q.dtype)).astype(jnp.float32) * (MEM_HEAD_DIM ** -0.5)
    p = jax.nn.softmax(s, axis=-1)
    o = jnp.einsum('bhqk,bkhd->bqhd', p, mem_v.astype(jnp.float32))
    return o.reshape(bsz, t, D_MODEL).astype(h.dtype) @ w_mo


def peer_block(xb, w_q, sub_keys, u_tab, v_tab):
    n = xb.shape[0]
    q = (xb @ w_q).reshape(n, PEER_HEADS, 2, PEER_HALF)
    s = jnp.einsum('nhpc,hpkc->nhpk', q, sub_keys.astype(q.dtype)).astype(jnp.float32)
    v1, i1 = lax.top_k(s[:, :, 0], PEER_TOPK)
    v2, i2 = lax.top_k(s[:, :, 1], PEER_TOPK)
    cand = (v1[..., :, None] + v2[..., None, :]).reshape(n, PEER_HEADS, PEER_TOPK * PEER_TOPK)
    cidx = (i1[..., :, None] * PEER_N_KEYS + i2[..., None, :]).reshape(n, PEER_HEADS, PEER_TOPK * PEER_TOPK)
    best, sel = lax.top_k(cand, PEER_TOPK)
    eidx = jnp.take_along_axis(cidx, sel, axis=-1)
    gate = jax.nn.softmax(best, axis=-1)
    u = jnp.take(u_tab, eidx, axis=0)
    hid = jax.nn.gelu(jnp.einsum('nd,nhkd->nhk', xb, u.astype(xb.dtype)).astype(jnp.float32))
    coef = (gate * hid).astype(xb.dtype)
    return jnp.einsum('nhk,nhkd->nd', coef, jnp.take(v_tab, eidx, axis=0).astype(xb.dtype))


def peer_ffn(xf, w_q, sub_keys, u_tab, v_tab):
    n = xf.shape[0]
    nb = -(-n // PEER_BLOCK)
    xb = jnp.pad(xf, ((0, nb * PEER_BLOCK - n), (0, 0))).reshape(nb, PEER_BLOCK, D_MODEL)
    out = lax.map(lambda blk: peer_block(blk, w_q, sub_keys, u_tab, v_tab), xb)
    return out.reshape(nb * PEER_BLOCK, D_MODEL)[:n]


def trunk_layer(x, lp, layer, mem_k, mem_v, conv_prev, gdn_state, k_past, v_past):
    bsz, t, _ = x.shape
    h = rmsnorm(x, lp['norm_mix_g'])
    split_pts = [int(c) for c in np.cumsum(IN_SIZES)[:-1]]
    qkv, z, b_raw, a_raw, dq, dk, dv, gate_a, gate_b = jnp.split(h @ lp['w_in'], split_pts, axis=-1)
    o_a, conv_new, s_new = gdn_branch(qkv, z, b_raw, a_raw, conv_prev, gdn_state, lp['gdn_conv_w'],
                                      lp['gdn_a_log'], lp['gdn_dt_bias'], lp['gdn_norm_g'])
    lam_init = 0.8 - 0.6 * math.exp(-0.3 * layer)
    f32 = jnp.float32
    lam = (jnp.exp(jnp.sum(lp['diff_lambda_q1'].astype(f32) * lp['diff_lambda_k1'].astype(f32)))
           - jnp.exp(jnp.sum(lp['diff_lambda_q2'].astype(f32) * lp['diff_lambda_k2'].astype(f32))) + lam_init)
    k_rows = dk.reshape(bsz, t, DIFF_HEADS, 2 * DIFF_HEAD_DIM)
    v_rows = dv.reshape(bsz, t, DIFF_HEADS, DIFF_V_DIM)
    o_b = diff_branch(dq, k_rows, v_rows, k_past, v_past, lam)
    o_b = (rmsnorm(o_b, lp['diff_subln_g']) * (1.0 - lam_init)).reshape(bsz, t, DIFF_WIDTH).astype(x.dtype)
    merged = jax.nn.sigmoid(gate_a) * (o_a @ lp['w_branch_a']) + jax.nn.sigmoid(gate_b) * (o_b @ lp['w_branch_b'])
    x = x + merged @ lp['w_out']
    x = x + cross_attend(rmsnorm(x, lp['norm_cross_g']), mem_k, mem_v, lp['w_mq'], lp['w_mo'])
    hf = rmsnorm(x, lp['norm_ffn_g']).reshape(bsz * t, D_MODEL)
    x = x + peer_ffn(hf, lp['peer_w_q'], lp['peer_sub_keys'], lp['peer_u'], lp['peer_v']).reshape(bsz, t, D_MODEL)
    return x, k_rows, v_rows, s_new, conv_new


def setup_inputs(seed: int = 0) -> dict:
    key = jax.random.key(seed)
    ks = iter(jax.random.split(key, 48))
    f32 = jnp.float32

    def nrm(shape, scale):
        return jax.random.normal(next(ks), shape, f32) * scale

    def gain(shape):
        return 1.0 + 0.02 * jax.random.normal(next(ks), shape, f32)

    L = DEPTH
    inv_d = D_MODEL ** -0.5
    return {
        'x_prompt': nrm((BATCH, SEQ, D_MODEL), 1.0),
        'x_sample': nrm((DEC_BATCH, DEC_SEQ, D_MODEL), 1.0),
        'cache_diff_k': nrm((L, DEC_BATCH, PAST_LEN, DIFF_HEADS, 2 * DIFF_HEAD_DIM), 1.0),
        'cache_diff_v': nrm((L, DEC_BATCH, PAST_LEN, DIFF_HEADS, DIFF_V_DIM), 1.0),
        'state_gdn': nrm((L, DEC_BATCH, GDN_HEADS, GDN_HEAD_DIM, GDN_HEAD_DIM), 0.1),
        'state_conv': nrm((L, DEC_BATCH, GDN_CONV - 1, GDN_CONV_DIM), 1.0),
        'cache_mem_k': nrm((L, DEC_BATCH, MEM_TOKENS, MEM_HEADS, MEM_HEAD_DIM), 1.0),
        'cache_mem_v': nrm((L, DEC_BATCH, MEM_TOKENS, MEM_HEADS, MEM_HEAD_DIM), 1.0),
        'mem_prompt': nrm((BATCH, MEM_TOKENS, D_MODEL), 1.0),
        'norm_mix_g': gain((L, D_MODEL)),
        'w_in': nrm((L, D_MODEL, IN_COLS), inv_d),
        'gdn_conv_w': nrm((L, GDN_CONV, GDN_CONV_DIM), 0.5),
        'gdn_a_log': jnp.log(jax.random.uniform(next(ks), (L, GDN_HEADS), f32, 1.0, 16.0)),
        'gdn_dt_bias': nrm((L, GDN_HEADS), 0.1),
        'gdn_norm_g': gain((L, GDN_HEAD_DIM)),
        'diff_lambda_q1': nrm((L, DIFF_HEAD_DIM), 0.1),
        'diff_lambda_k1': nrm((L, DIFF_HEAD_DIM), 0.1),
        'diff_lambda_q2': nrm((L, DIFF_HEAD_DIM), 0.1),
        'diff_lambda_k2': nrm((L, DIFF_HEAD_DIM), 0.1),
        'diff_subln_g': gain((L, DIFF_V_DIM)),
        'w_branch_a': nrm((L, GDN_WIDTH, D_MODEL), GDN_WIDTH ** -0.5),
        'w_branch_b': nrm((L, DIFF_WIDTH, D_MODEL), DIFF_WIDTH ** -0.5),
        'w_out': nrm((L, D_MODEL, D_MODEL), inv_d),
        'norm_cross_g': gain((L, D_MODEL)),
        'norm_mem_g': gain((L, D_MODEL)),
        'w_mq': nrm((L, D_MODEL, D_MODEL), inv_d),
        'w_mk': nrm((L, D_MODEL, D_MODEL), inv_d),
        'w_mv': nrm((L, D_MODEL, D_MODEL), inv_d),
        'w_mo': nrm((L, D_MODEL, D_MODEL), inv_d),
        'norm_ffn_g': gain((L, D_MODEL)),
        'peer_w_q': nrm((L, D_MODEL, PEER_HEADS * PEER_QUERY_DIM), inv_d),
        'peer_sub_keys': nrm((L, PEER_HEADS, 2, PEER_N_KEYS, PEER_HALF), PEER_HALF ** -0.5),
        'peer_u': nrm((L, PEER_EXPERTS, D_MODEL), inv_d),
        'peer_v': nrm((L, PEER_EXPERTS, D_MODEL), PEER_HEADS ** -0.5),
        'final_norm_g': gain((D_MODEL,)),
    }


def reference(x_prompt, x_sample, cache_diff_k, cache_diff_v, state_gdn, state_conv, cache_mem_k, cache_mem_v,
              mem_prompt, norm_mix_g, w_in, gdn_conv_w, gdn_a_log, gdn_dt_bias, gdn_norm_g,
              diff_lambda_q1, diff_lambda_k1, diff_lambda_q2, diff_lambda_k2, diff_subln_g,
              w_branch_a, w_branch_b, w_out, norm_cross_g, norm_mem_g, w_mq, w_mk, w_mv, w_mo,
              norm_ffn_g, peer_w_q, peer_sub_keys, peer_u, peer_v, final_norm_g):
    yp, ys = x_prompt, x_sample
    pk, pv, ps, pc, pmk, pmv = [], [], [], [], [], []
    sk, sv, ss, sc = [], [], [], []
    for l in range(DEPTH):
        lp = {
            'norm_mix_g': norm_mix_g[l], 'w_in': w_in[l], 'gdn_conv_w': gdn_conv_w[l],
            'gdn_a_log': gdn_a_log[l], 'gdn_dt_bias': gdn_dt_bias[l], 'gdn_norm_g': gdn_norm_g[l],
            'diff_lambda_q1': diff_lambda_q1[l], 'diff_lambda_k1': diff_lambda_k1[l],
            'diff_lambda_q2': diff_lambda_q2[l], 'diff_lambda_k2': diff_lambda_k2[l],
            'diff_subln_g': diff_subln_g[l], 'w_branch_a': w_branch_a[l], 'w_branch_b': w_branch_b[l],
            'w_out': w_out[l], 'norm_cross_g': norm_cross_g[l], 'w_mq': w_mq[l], 'w_mo': w_mo[l],
            'norm_ffn_g': norm_ffn_g[l], 'peer_w_q': peer_w_q[l], 'peer_sub_keys': peer_sub_keys[l],
            'peer_u': peer_u[l], 'peer_v': peer_v[l],
        }
        mem_k, mem_v = memory_kv(mem_prompt, norm_mem_g[l], w_mk[l], w_mv[l])
        conv0 = jnp.zeros((yp.shape[0], GDN_CONV - 1, GDN_CONV_DIM), yp.dtype)
        s0 = jnp.zeros((yp.shape[0], GDN_HEADS, GDN_HEAD_DIM, GDN_HEAD_DIM), yp.dtype)
        yp, k_r, v_r, s_n, c_n = trunk_layer(yp, lp, l, mem_k, mem_v, conv0, s0, None, None)
        pk.append(k_r); pv.append(v_r); ps.append(s_n); pc.append(c_n); pmk.append(mem_k); pmv.append(mem_v)
        ys, k_r, v_r, s_n, c_n = trunk_layer(ys, lp, l, cache_mem_k[l], cache_mem_v[l], state_conv[l],
                                             state_gdn[l], cache_diff_k[l], cache_diff_v[l])
        sk.append(k_r); sv.append(v_r); ss.append(s_n); sc.append(c_n)
    y_prompt = rmsnorm(yp, final_norm_g)
    y_sample = rmsnorm(ys, final_norm_g)
    new_diff_k_prompt = jnp.stack(pk)
    new_diff_v_prompt = jnp.stack(pv)
    new_state_gdn_prompt = jnp.stack(ps)
    new_state_conv_prompt = jnp.stack(pc)
    new_mem_k_prompt = jnp.stack(pmk)
    new_mem_v_prompt = jnp.stack(pmv)
    new_diff_k_sample = jnp.stack(sk)
    new_diff_v_sample = jnp.stack(sv)
    new_state_gdn_sample = jnp.stack(ss)
    new_state_conv_sample = jnp.stack(sc)
    return (y_prompt, y_sample, new_diff_k_prompt, new_diff_v_prompt, new_state_gdn_prompt, new_state_conv_prompt,
            new_mem_k_prompt, new_mem_v_prompt, new_diff_k_sample, new_diff_v_sample, new_state_gdn_sample,
            new_state_conv_sample)
```

```python
import functools
import math

import jax
import jax.numpy as jnp
from jax import lax
from jax.experimental import pallas as pl
from jax.experimental.pallas import tpu as pltpu

F32 = jnp.float32
BF16 = jnp.bfloat16
HIGHEST = lax.Precision.HIGHEST

RMS_EPS = 1e-6
CHUNK = 64
CHUNK_SHIFT = 6
DIFF_TQ = 512
GDN_HEADS = 8
HEAD_DIM = 128
GDN_WIDTH = GDN_HEADS * HEAD_DIM
GDN_CONV = 4
DIFF_HEADS = 8
DIFF_HEAD_DIM = 64
MEM_HEADS = 4
PEER_HEADS = 8
PEER_N_KEYS = 128
PEER_TOPK = 16
LANES = 128
SUBLANES = 8
V7X_VMEM_LIMIT = 52 * 1024 * 1024

COL_QKV = 0
COL_Z = 3072
COL_DQ = 4096
COL_DK = 5120
COL_DV = 6144
COL_GA = 7168
COL_GB = 9216
COL_BA = 11264
IN_COLS_PADDED = 11520
IN_TN = 768


def _cparams(sem, vmem=V7X_VMEM_LIMIT):
    return pltpu.CompilerParams(dimension_semantics=sem, vmem_limit_bytes=vmem)


def _dot(a, b, precision=None):
    return jnp.dot(a, b, preferred_element_type=F32, precision=precision)


def _dot_nt(a, b, precision=None):
    return lax.dot_general(a, b, (((1,), (1,)), ((), ())), preferred_element_type=F32, precision=precision)


def _rms(xf, g):
    return xf * lax.rsqrt(jnp.mean(xf * xf, axis=-1, keepdims=True) + RMS_EPS) * g


def _mm_kernel(*refs, has_norm, has_res):
    it = iter(refs)
    x_ref = next(it)
    g_ref = next(it) if has_norm else None
    w_ref = next(it)
    r_ref = next(it) if has_res else None
    o_ref = next(it)
    h_scr = next(it)

    @pl.when(pl.program_id(1) == 0)
    def _():
        xf = x_ref[...].astype(F32)
        if has_norm:
            xf = _rms(xf, g_ref[...])
        h_scr[...] = xf.astype(BF16)

    acc = _dot(h_scr[...], w_ref[...])
    if has_res:
        acc = acc + r_ref[...]
    o_ref[...] = acc.astype(o_ref.dtype)


def _matmul(x, w, *, norm_g=None, residual=None, tm, tn, out_dtype=F32, name="matmul"):
    m, k = x.shape
    n = w.shape[1]
    assert m % tm == 0 and n % tn == 0, (m, n, tm, tn)
    in_specs = [pl.BlockSpec((tm, k), lambda i, j: (i, 0))]
    args = [x]
    if norm_g is not None:
        in_specs.append(pl.BlockSpec((1, k), lambda i, j: (0, 0)))
        args.append(norm_g.reshape(1, k).astype(F32))
    in_specs.append(pl.BlockSpec((k, tn), lambda i, j: (0, j)))
    args.append(w)
    if residual is not None:
        in_specs.append(pl.BlockSpec((tm, tn), lambda i, j: (i, j)))
        args.append(residual)
    return pl.pallas_call(
        functools.partial(_mm_kernel, has_norm=norm_g is not None, has_res=residual is not None),
        out_shape=jax.ShapeDtypeStruct((m, n), out_dtype),
        grid=(m // tm, n // tn),
        in_specs=in_specs,
        out_specs=pl.BlockSpec((tm, tn), lambda i, j: (i, j)),
        scratch_shapes=[pltpu.VMEM((tm, k), BF16)],
        compiler_params=_cparams(("parallel", "arbitrary")),
        name=name,
    )(*args)


def _gdn_kernel(qkv_ref, z_ref, ba_ref, prev_ref, s0_ref, convw_ref, alog_ref, dtb_ref, ng_ref,
                o_ref, sfin_ref, s_scr, xbuf, *, L):
    c = pl.program_id(1)

    @pl.when(c == 0)
    def _():
        s_scr[...] = s0_ref[0]
        xbuf[0:SUBLANES, :] = prev_ref[0]

    xbuf[SUBLANES:SUBLANES + L, :] = qkv_ref[...]
    base = SUBLANES - (GDN_CONV - 1)
    y = xbuf[base:base + L, :] * convw_ref[0:1, :]
    for i in range(1, GDN_CONV):
        y = y + xbuf[base + i:base + i + L, :] * convw_ref[i:i + 1, :]
    y = y * jax.nn.sigmoid(y)
    tail = xbuf[L:L + SUBLANES, :]
    xbuf[0:SUBLANES, :] = tail

    row = lax.broadcasted_iota(jnp.int32, (L, L), 0)
    col = lax.broadcasted_iota(jnp.int32, (L, L), 1)
    lower = row >= col
    strict = row > col
    tril = lower.astype(F32)
    triu = (row <= col).astype(F32)
    ones = jnp.ones((L, L), F32)
    eye = (row == col).astype(F32)
    ba = ba_ref[...]
    n_double = int(math.log2(L)) - 1
    assert 2 ** (n_double + 1) == L

    for h in range(GDN_HEADS):
        lo, hi = h * HEAD_DIM, (h + 1) * HEAD_DIM
        q = y[:, lo:hi]
        k = y[:, GDN_WIDTH + lo:GDN_WIDTH + hi]
        v = y[:, 2 * GDN_WIDTH + lo:2 * GDN_WIDTH + hi]
        q = q * lax.rsqrt(jnp.sum(q * q, axis=-1, keepdims=True) + 1e-6) * (HEAD_DIM ** -0.5)
        k = k * lax.rsqrt(jnp.sum(k * k, axis=-1, keepdims=True) + 1e-6)
        beta = jax.nn.sigmoid(ba[:, h:h + 1])
        a_in = ba[:, GDN_HEADS + h:GDN_HEADS + h + 1] + dtb_ref[0:1, h:h + 1]
        softplus = jnp.maximum(a_in, 0.0) + jnp.log(1.0 + jnp.exp(-jnp.abs(a_in)))
        g = -jnp.exp(alog_ref[0:1, h:h + 1]) * softplus
        gmat = jnp.broadcast_to(g, (L, L))
        g_i = _dot(tril, gmat, HIGHEST)
        g_j = _dot(ones, gmat * triu, HIGHEST)
        decay = jnp.where(lower, jnp.exp(jnp.where(lower, g_i - g_j, 0.0)), 0.0)
        g_col = g_i[:, 0:1]
        g_last = g_i[L - 1:L, 0:1]
        e_col = jnp.exp(g_col)
        kk = _dot_nt(k, k, HIGHEST)
        a_mat = jnp.where(strict, beta * kk * decay, 0.0)
        x = -a_mat
        t_inv = eye + x
        for _ in range(n_double):
            x = _dot(x, x, HIGHEST)
            t_inv = t_inv + _dot(t_inv, x, HIGHEST)
        rhs = jnp.concatenate([v * beta, k * (beta * e_col)], axis=-1)
        sol = _dot(t_inv, rhs, HIGHEST)
        u = sol[:, :HEAD_DIM]
        w = sol[:, HEAD_DIM:]
        qk = jnp.where(lower, _dot_nt(q, k, HIGHEST) * decay, 0.0)
        q_dec = q * e_col
        k_dec = k * jnp.exp(g_last - g_col)
        g_end = jnp.exp(g_last)
        s = s_scr[h]
        v_new = u - _dot(w, s, HIGHEST)
        o = _dot(q_dec, s, HIGHEST) + _dot(qk, v_new, HIGHEST)
        s_scr[h] = s * g_end + _dot(k_dec.T, v_new, HIGHEST)
        zf = z_ref[:, lo:hi]
        o = _rms(o, ng_ref[...]) * (zf * jax.nn.sigmoid(zf))
        o_ref[:, lo:hi] = o.astype(o_ref.dtype)

    @pl.when(c == pl.num_programs(1) - 1)
    def _():
        sfin_ref[0] = s_scr[...]


def _gdn(proj, conv_prev8, s0, conv_w, a_log, dt_bias, norm_g, *, bsz, t):
    L = min(CHUNK, t)
    nc = t // L
    conv_dim = 3 * GDN_WIDTH
    kern = functools.partial(_gdn_kernel, L=L)
    o, s_fin = pl.pallas_call(
        kern,
        out_shape=(jax.ShapeDtypeStruct((bsz * t, GDN_WIDTH), BF16),
                   jax.ShapeDtypeStruct((bsz, GDN_HEADS, HEAD_DIM, HEAD_DIM), F32)),
        grid=(bsz, nc),
        in_specs=[
            pl.BlockSpec((L, conv_dim), lambda b, c: (b * nc + c, COL_QKV // conv_dim)),
            pl.BlockSpec((L, GDN_WIDTH), lambda b, c: (b * nc + c, COL_Z // GDN_WIDTH)),
            pl.BlockSpec((L, LANES), lambda b, c: (b * nc + c, COL_BA // LANES)),
            pl.BlockSpec((1, SUBLANES, conv_dim), lambda b, c: (b, 0, 0)),
            pl.BlockSpec((1, GDN_HEADS, HEAD_DIM, HEAD_DIM), lambda b, c: (b, 0, 0, 0)),
            pl.BlockSpec((GDN_CONV, conv_dim), lambda b, c: (0, 0)),
            pl.BlockSpec((1, GDN_HEADS), lambda b, c: (0, 0)),
            pl.BlockSpec((1, GDN_HEADS), lambda b, c: (0, 0)),
            pl.BlockSpec((1, HEAD_DIM), lambda b, c: (0, 0)),
        ],
        out_specs=(pl.BlockSpec((L, GDN_WIDTH), lambda b, c: (b * nc + c, 0)),
                   pl.BlockSpec((1, GDN_HEADS, HEAD_DIM, HEAD_DIM), lambda b, c: (b, 0, 0, 0))),
        scratch_shapes=[pltpu.VMEM((GDN_HEADS, HEAD_DIM, HEAD_DIM), F32),
                        pltpu.VMEM((SUBLANES + L + SUBLANES, conv_dim), F32)],
        compiler_params=_cparams(("parallel", "arbitrary")),
        name="gdn",
    )(proj, proj, proj, conv_prev8, s0, conv_w, a_log.reshape(1, -1), dt_bias.reshape(1, -1),
      norm_g.reshape(1, -1))
    return o, s_fin


def _lambda_value(lam_refs, lam_init):
    lq1, lk1, lq2, lk2 = (r[...] for r in lam_refs)
    return (jnp.exp(jnp.sum(lq1 * lk1, axis=-1, keepdims=True))
            - jnp.exp(jnp.sum(lq2 * lk2, axis=-1, keepdims=True)) + lam_init)


def _softmax_update(m_scr, l_scr, acc_scr, mi, s, vb):
    m_old = m_scr[mi]
    m_new = jnp.maximum(m_old, jnp.max(s, axis=-1, keepdims=True))
    alpha = jnp.exp(m_old - m_new)
    p = jnp.exp(s - m_new)
    l_scr[mi] = alpha * l_scr[mi] + jnp.sum(p, axis=-1, keepdims=True)
    acc_scr[mi] = alpha * acc_scr[mi] + _dot(p.astype(BF16), vb)
    m_scr[mi] = m_new


def _diff_scores(qb, kb, bias):
    out = []
    for mi in range(2):
        sl = slice(mi * DIFF_HEAD_DIM, (mi + 1) * DIFF_HEAD_DIM)
        out.append(_dot_nt(qb[:, sl], kb[:, sl]) * (DIFF_HEAD_DIM ** -0.5) + bias)
    return out


def _diff_finalize(m_scr, l_scr, acc_scr, lam_refs, g_ref, o_ref, lam_init):
    lam = _lambda_value(lam_refs, lam_init)
    o = acc_scr[0] / l_scr[0] - lam * (acc_scr[1] / l_scr[1])
    o = _rms(o, g_ref[...]) * (1.0 - lam_init)
    o_ref[...] = o.astype(o_ref.dtype)


def _diff_init(m_scr, l_scr, acc_scr):
    m_scr[...] = jnp.full(m_scr.shape, -jnp.inf, F32)
    l_scr[...] = jnp.zeros(l_scr.shape, F32)
    acc_scr[...] = jnp.zeros(acc_scr.shape, F32)


def _alibi_bias(slope, q_pos, k_pos):
    dist = jnp.abs(q_pos - k_pos).astype(F32)
    visible = jnp.right_shift(k_pos, CHUNK_SHIFT) <= jnp.right_shift(q_pos, CHUNK_SHIFT)
    return jnp.where(visible, -slope * dist, -jnp.inf)


def _diff_prompt_kernel(slopes_ref, q_ref, k_ref, v_ref, lq1, lk1, lq2, lk2, g_ref, o_ref,
                        m_scr, l_scr, acc_scr, *, tq, lam_init):
    h = pl.program_id(1)
    i = pl.program_id(2)
    j = pl.program_id(3)

    @pl.when(j == 0)
    def _():
        _diff_init(m_scr, l_scr, acc_scr)

    @pl.when(j <= i)
    def _():
        q_pos = i * tq + lax.broadcasted_iota(jnp.int32, (tq, tq), 0)
        k_pos = j * tq + lax.broadcasted_iota(jnp.int32, (tq, tq), 1)
        bias = _alibi_bias(slopes_ref[h], q_pos, k_pos)
        qb = q_ref[...].astype(BF16)
        kb = k_ref[...].astype(BF16)
        vb = v_ref[...].astype(BF16)
        for mi, s in enumerate(_diff_scores(qb, kb, bias)):
            _softmax_update(m_scr, l_scr, acc_scr, mi, s, vb)

    @pl.when(j == pl.num_programs(3) - 1)
    def _():
        _diff_finalize(m_scr, l_scr, acc_scr, (lq1, lk1, lq2, lk2), g_ref, o_ref, lam_init)


def _alibi_slopes():
    return jnp.asarray([2.0 ** (-8.0 * (i + 1) / DIFF_HEADS) for i in range(DIFF_HEADS)], F32)


def _lam_specs(nidx):
    zero = {3: lambda a, b, c: (0, 0), 4: lambda a, b, c, d: (0, 0)}[nidx]
    return [pl.BlockSpec((1, DIFF_HEAD_DIM), zero)] * 4 + [pl.BlockSpec((1, HEAD_DIM), zero)]


def _diff_prompt(proj, lams, subln_g, *, bsz, t, lam_init):
    tq = min(DIFF_TQ, t)
    nq = t // tq
    kern = functools.partial(_diff_prompt_kernel, tq=tq, lam_init=lam_init)
    qc, kc, vc = COL_DQ // LANES, COL_DK // LANES, COL_DV // LANES
    return pl.pallas_call(
        kern,
        out_shape=jax.ShapeDtypeStruct((bsz * t, DIFF_HEADS * HEAD_DIM), BF16),
        grid=(bsz, DIFF_HEADS, nq, nq),
        in_specs=[
            pl.BlockSpec(memory_space=pltpu.SMEM),
            pl.BlockSpec((tq, LANES), lambda b, h, i, j: (b * nq + i, qc + h)),
            pl.BlockSpec((tq, LANES), lambda b, h, i, j: (b * nq + jnp.minimum(i, j), kc + h)),
            pl.BlockSpec((tq, LANES), lambda b, h, i, j: (b * nq + jnp.minimum(i, j), vc + h)),
        ] + _lam_specs(4),
        out_specs=pl.BlockSpec((tq, LANES), lambda b, h, i, j: (b * nq + i, h)),
        scratch_shapes=[pltpu.VMEM((2, tq, 1), F32), pltpu.VMEM((2, tq, 1), F32),
                        pltpu.VMEM((2, tq, HEAD_DIM), F32)],
        compiler_params=_cparams(("parallel", "parallel", "parallel", "arbitrary")),
        name="diff_attn_prompt",
    )(_alibi_slopes(), proj, proj, proj, *lams, subln_g.reshape(1, -1))


def _diff_sample_kernel(slopes_ref, q_ref, kn_ref, vn_ref, kp_ref, vp_ref, lq1, lk1, lq2, lk2, g_ref, o_ref,
                        m_scr, l_scr, acc_scr, *, t, tk, past, lam_init):
    h = pl.program_id(1)
    j = pl.program_id(2)
    slope = slopes_ref[h]
    qb = q_ref[...].astype(BF16)

    @pl.when(j == 0)
    def _():
        _diff_init(m_scr, l_scr, acc_scr)
        q_pos = past + lax.broadcasted_iota(jnp.int32, (t, t), 0)
        k_pos = past + lax.broadcasted_iota(jnp.int32, (t, t), 1)
        bias = _alibi_bias(slope, q_pos, k_pos)
        vb = vn_ref[...].astype(BF16)
        for mi, s in enumerate(_diff_scores(qb, kn_ref[...].astype(BF16), bias)):
            _softmax_update(m_scr, l_scr, acc_scr, mi, s, vb)

    q_pos = past + lax.broadcasted_iota(jnp.int32, (t, tk), 0)
    k_pos = j * tk + lax.broadcasted_iota(jnp.int32, (t, tk), 1)
    bias = _alibi_bias(slope, q_pos, k_pos)
    vb = vp_ref[...].astype(BF16)
    for mi, s in enumerate(_diff_scores(qb, kp_ref[...].astype(BF16), bias)):
        _softmax_update(m_scr, l_scr, acc_scr, mi, s, vb)

    @pl.when(j == pl.num_programs(2) - 1)
    def _():
        _diff_finalize(m_scr, l_scr, acc_scr, (lq1, lk1, lq2, lk2), g_ref, o_ref, lam_init)


def _diff_sample(proj, k_past, v_past, lams, subln_g, *, bsz, t, lam_init):
    past = k_past.shape[1]
    tk = min(1024, past)
    nk = past // tk
    kp = k_past.reshape(bsz * past, DIFF_HEADS * HEAD_DIM)
    vp = v_past.reshape(bsz * past, DIFF_HEADS * HEAD_DIM)
    kern = functools.partial(_diff_sample_kernel, t=t, tk=tk, past=past, lam_init=lam_init)
    qc, kc, vc = COL_DQ // LANES, COL_DK // LANES, COL_DV // LANES
    return pl.pallas_call(
        kern,
        out_shape=jax.ShapeDtypeStruct((bsz * t, DIFF_HEADS * HEAD_DIM), BF16),
        grid=(bsz, DIFF_HEADS, nk),
        in_specs=[
            pl.BlockSpec(memory_space=pltpu.SMEM),
            pl.BlockSpec((t, LANES), lambda b, h, j: (b, qc + h)),
            pl.BlockSpec((t, LANES), lambda b, h, j: (b, kc + h)),
            pl.BlockSpec((t, LANES), lambda b, h, j: (b, vc + h)),
            pl.BlockSpec((tk, LANES), lambda b, h, j: (b * nk + j, h)),
            pl.BlockSpec((tk, LANES), lambda b, h, j: (b * nk + j, h)),
        ] + _lam_specs(3),
        out_specs=pl.BlockSpec((t, LANES), lambda b, h, j: (b, h)),
        scratch_shapes=[pltpu.VMEM((2, t, 1), F32), pltpu.VMEM((2, t, 1), F32),
                        pltpu.VMEM((2, t, HEAD_DIM), F32)],
        compiler_params=_cparams(("parallel", "parallel", "arbitrary")),
        name="diff_attn_sample",
    )(_alibi_slopes(), proj, proj, proj, kp, vp, *lams, subln_g.reshape(1, -1))


def _merge_kernel(oa_ref, ob_ref, wa_ref, wb_ref, ga_ref, gb_ref, o_ref):
    ya = _dot(oa_ref[...], wa_ref[...])
    yb = _dot(ob_ref[...], wb_ref[...])
    o_ref[...] = (jax.nn.sigmoid(ga_ref[...]) * ya + jax.nn.sigmoid(gb_ref[...]) * yb).astype(o_ref.dtype)


def _merge(o_a, o_b, wa, wb, proj, *, tm):
    m = o_a.shape[0]
    d = wa.shape[1]
    tn = 512
    ga0, gb0 = COL_GA // tn, COL_GB // tn
    return pl.pallas_call(
        _merge_kernel,
        out_shape=jax.ShapeDtypeStruct((m, d), BF16),
        grid=(m // tm, d // tn),
        in_specs=[
            pl.BlockSpec((tm, o_a.shape[1]), lambda i, j: (i, 0)),
            pl.BlockSpec((tm, o_b.shape[1]), lambda i, j: (i, 0)),
            pl.BlockSpec((wa.shape[0], tn), lambda i, j: (0, j)),
            pl.BlockSpec((wb.shape[0], tn), lambda i, j: (0, j)),
            pl.BlockSpec((tm, tn), lambda i, j: (i, ga0 + j)),
            pl.BlockSpec((tm, tn), lambda i, j: (i, gb0 + j)),
        ],
        out_specs=pl.BlockSpec((tm, tn), lambda i, j: (i, j)),
        compiler_params=_cparams(("parallel", "arbitrary")),
        name="merge",
    )(o_a, o_b, wa, wb, proj, proj)


def _cross_kernel(q_ref, k_ref, v_ref, o_ref):
    dh = q_ref.shape[-1] // MEM_HEADS
    for h in range(MEM_HEADS):
        sl = slice(h * dh, (h + 1) * dh)
        s = _dot_nt(q_ref[:, sl], k_ref[0, :, sl].astype(BF16)) * (dh ** -0.5)
        s = s - jnp.max(s, axis=-1, keepdims=True)
        p = jnp.exp(s)
        p = p / jnp.sum(p, axis=-1, keepdims=True)
        o_ref[:, sl] = _dot(p.astype(BF16), v_ref[0, :, sl].astype(BF16)).astype(o_ref.dtype)


def _cross_attend(q, mem_k, mem_v, *, bsz, t, tm):
    d = q.shape[1]
    nt = t // tm
    mt = mem_k.shape[1]
    return pl.pallas_call(
        _cross_kernel,
        out_shape=jax.ShapeDtypeStruct(q.shape, BF16),
        grid=(bsz, nt),
        in_specs=[
            pl.BlockSpec((tm, d), lambda b, i: (b * nt + i, 0)),
            pl.BlockSpec((1, mt, d), lambda b, i: (b, 0, 0)),
            pl.BlockSpec((1, mt, d), lambda b, i: (b, 0, 0)),
        ],
        out_specs=pl.BlockSpec((tm, d), lambda b, i: (b * nt + i, 0)),
        compiler_params=_cparams(("parallel", "arbitrary")),
        name="cross_attn",
    )(q, mem_k, mem_v)


def _topk_rows(work, cidx, k):
    n, tn = work.shape
    rid = lax.broadcasted_iota(jnp.int32, (n, tn), 0)
    kid = lax.broadcasted_iota(jnp.int32, (k, tn), 0)
    vals = jnp.zeros((k, tn), F32)
    idxs = jnp.zeros((k, tn), jnp.int32)
    for t in range(k):
        m = jnp.max(work, axis=0, keepdims=True)
        pos = jnp.min(jnp.where(work == m, rid, n), axis=0, keepdims=True)
        hit = rid == pos
        if cidx is None:
            picked = pos
        else:
            picked = jnp.sum(jnp.where(hit, cidx, 0), axis=0, keepdims=True)
        vals = jnp.where(kid == t, m, vals)
        idxs = jnp.where(kid == t, picked, idxs)
        work = jnp.where(hit, -jnp.inf, work)
    return vals, idxs


def _peer_route_kernel(q_ref, keys_ref, e_ref, g_ref):
    tn = q_ref.shape[0]
    k = PEER_TOPK
    for h in range(PEER_HEADS):
        sub = []
        for p in range(2):
            c0 = (h * 2 + p) * PEER_N_KEYS
            s = _dot_nt(keys_ref[h, p], q_ref[:, c0:c0 + PEER_N_KEYS], HIGHEST)
            sub.append(_topk_rows(s, None, k))
        (v1, i1), (v2, i2) = sub
        cand = jnp.concatenate([v1[a:a + 1, :] + v2 for a in range(k)], axis=0)
        cidx = jnp.concatenate([i1[a:a + 1, :] * PEER_N_KEYS + i2 for a in range(k)], axis=0)
        best, eidx = _topk_rows(cand, cidx, k)
        ex = jnp.exp(best - jnp.max(best, axis=0, keepdims=True))
        gate = ex / jnp.sum(ex, axis=0, keepdims=True)
        e_ref[h * k:(h + 1) * k, :] = eidx
        g_ref[h * k:(h + 1) * k, :] = gate


def _peer_route(qp, sub_keys, *, tn):
    n = qp.shape[0]
    rows = PEER_HEADS * PEER_TOPK
    return pl.pallas_call(
        _peer_route_kernel,
        out_shape=(jax.ShapeDtypeStruct((rows, n), jnp.int32), jax.ShapeDtypeStruct((rows, n), F32)),
        grid=(n // tn,),
        in_specs=[
            pl.BlockSpec((tn, qp.shape[1]), lambda i: (i, 0)),
            pl.BlockSpec(sub_keys.shape, lambda i: (0, 0, 0, 0)),
        ],
        out_specs=(pl.BlockSpec((rows, tn), lambda i: (0, i)), pl.BlockSpec((rows, tn), lambda i: (0, i))),
        compiler_params=_cparams(("parallel",)),
        name="peer_route",
    )(qp, sub_keys)


def _gelu_tanh(x):
    return 0.5 * x * (1.0 + jnp.tanh(math.sqrt(2.0 / math.pi) * (x + 0.044715 * (x * x * x))))


def _peer_expert_kernel(x_ref, gn_ref, e_ref, g_ref, u_ref, v_ref, gf_ref, o_ref, h_scr, acc_scr):
    r = pl.program_id(1)
    tn = x_ref.shape[0]
    eb = u_ref.shape[0]
    pairs = e_ref.shape[0]

    @pl.when(r == 0)
    def _():
        h_scr[...] = _rms(x_ref[...], gn_ref[...]).astype(BF16)
        acc_scr[...] = jnp.zeros(acc_scr.shape, F32)

    hid = _gelu_tanh(_dot_nt(u_ref[...], h_scr[...]))
    expert_id = r * eb + lax.broadcasted_iota(jnp.int32, (eb, LANES), 0)
    chunks = []
    for c in range(tn // LANES):
        lanes = pl.ds(c * LANES, LANES)

        def body(p8, gt, lanes=lanes):
            rows = pl.ds(pl.multiple_of(p8 * SUBLANES, SUBLANES), SUBLANES)
            e8 = e_ref[rows, lanes]
            g8 = g_ref[rows, lanes]
            for s in range(SUBLANES):
                gt = gt + jnp.where(e8[s:s + 1, :] == expert_id, g8[s:s + 1, :], 0.0)
            return gt

        chunks.append(lax.fori_loop(0, pairs // SUBLANES, body, jnp.zeros((eb, LANES), F32)))
    gate_t = jnp.concatenate(chunks, axis=1) if len(chunks) > 1 else chunks[0]
    coef = (gate_t * hid).T.astype(BF16)
    acc_scr[...] += _dot(coef, v_ref[...])

    @pl.when(r == pl.num_programs(1) - 1)
    def _():
        o_ref[...] = _rms(x_ref[...] + acc_scr[...], gf_ref[...])


def _peer_experts(x, norm_g, e_t, g_t, u_tab, v_tab, final_g, *, tn):
    n, d = x.shape
    eb = PEER_N_KEYS
    nblk = u_tab.shape[0] // eb
    rows = e_t.shape[0]
    return pl.pallas_call(
        _peer_expert_kernel,
        out_shape=jax.ShapeDtypeStruct((n, d), F32),
        grid=(n // tn, nblk),
        in_specs=[
            pl.BlockSpec((tn, d), lambda i, r: (i, 0)),
            pl.BlockSpec((1, d), lambda i, r: (0, 0)),
            pl.BlockSpec((rows, tn), lambda i, r: (0, i)),
            pl.BlockSpec((rows, tn), lambda i, r: (0, i)),
            pl.BlockSpec((eb, d), lambda i, r: (r, 0)),
            pl.BlockSpec((eb, d), lambda i, r: (r, 0)),
            pl.BlockSpec((1, d), lambda i, r: (0, 0)),
        ],
        out_specs=pl.BlockSpec((tn, d), lambda i, r: (i, 0)),
        scratch_shapes=[pltpu.VMEM((tn, d), BF16), pltpu.VMEM((tn, d), F32)],
        compiler_params=_cparams(("parallel", "arbitrary")),
        name="peer_experts",
    )(x, norm_g.reshape(1, d), e_t, g_t, u_tab, v_tab, final_g.reshape(1, d))


def _reorder_w_in(w_in):
    c_qkv = 3 * GDN_WIDTH
    c_z = c_qkv + GDN_WIDTH
    c_ba = c_z + 2 * GDN_HEADS
    rest = w_in[:, c_ba:]
    pad = jnp.zeros((w_in.shape[0], IN_COLS_PADDED - COL_BA - 2 * GDN_HEADS), w_in.dtype)
    return jnp.concatenate([w_in[:, :c_z], rest, w_in[:, c_z:c_ba], pad], axis=1).astype(BF16)


def _layer(x, wts, layer, mem_k, mem_v, conv_prev, gdn_state, k_past, v_past, *, tm):
    bsz, t, d = x.shape
    n = bsz * t
    x2d = x.reshape(n, d)
    proj = _matmul(x2d, wts["w_in"], norm_g=wts["norm_mix_g"], tm=tm, tn=IN_TN, name="in_proj")
    conv_dim = 3 * GDN_WIDTH
    conv_prev8 = jnp.concatenate(
        [jnp.zeros((bsz, SUBLANES - (GDN_CONV - 1), conv_dim), F32), conv_prev.astype(F32)], axis=1)
    o_a, s_new = _gdn(proj, conv_prev8, gdn_state.astype(F32), wts["gdn_conv_w"], wts["gdn_a_log"],
                      wts["gdn_dt_bias"], wts["gdn_norm_g"], bsz=bsz, t=t)
    conv_new = proj.reshape(bsz, t, -1)[:, t - (GDN_CONV - 1):, COL_QKV:COL_QKV + conv_dim]
    lam_init = 0.8 - 0.6 * math.exp(-0.3 * layer)
    lams = tuple(wts[k].reshape(1, -1) for k in ("diff_lambda_q1", "diff_lambda_k1", "diff_lambda_q2", "diff_lambda_k2"))
    if k_past is None:
        o_b = _diff_prompt(proj, lams, wts["diff_subln_g"], bsz=bsz, t=t, lam_init=lam_init)
    else:
        o_b = _diff_sample(proj, k_past, v_past, lams, wts["diff_subln_g"], bsz=bsz, t=t, lam_init=lam_init)
    k_rows = proj[:, COL_DK:COL_DK + DIFF_HEADS * HEAD_DIM].reshape(bsz, t, DIFF_HEADS, HEAD_DIM)
    v_rows = proj[:, COL_DV:COL_DV + DIFF_HEADS * HEAD_DIM].reshape(bsz, t, DIFF_HEADS, HEAD_DIM)
    merged = _merge(o_a, o_b, wts["w_branch_a"], wts["w_branch_b"], proj, tm=tm)
    x1 = _matmul(merged, wts["w_out"], residual=x2d, tm=tm, tn=512, name="out_proj")
    qm = _matmul(x1, wts["w_mq"], norm_g=wts["norm_cross_g"], tm=tm, tn=512, out_dtype=BF16, name="mem_q")
    mt = mem_k.shape[1]
    oc = _cross_attend(qm, mem_k.reshape(bsz, mt, d), mem_v.reshape(bsz, mt, d), bsz=bsz, t=t, tm=min(tm, t))
    x2 = _matmul(oc, wts["w_mo"], residual=x1, tm=tm, tn=512, name="mem_o")
    qp = _matmul(x2, wts["peer_w_q"], norm_g=wts["norm_ffn_g"], tm=tm, tn=512, name="peer_q")
    tn = min(512, n)
    e_t, g_t = _peer_route(qp, wts["peer_sub_keys"], tn=LANES)
    y = _peer_experts(x2, wts["norm_ffn_g"], e_t, g_t, wts["peer_u"], wts["peer_v"], wts["final_norm_g"], tn=tn)
    return y.reshape(bsz, t, d), k_rows, v_rows, s_new, conv_new


def kernel(x_prompt, x_sample, cache_diff_k, cache_diff_v, state_gdn, state_conv, cache_mem_k, cache_mem_v,
           mem_prompt, norm_mix_g, w_in, gdn_conv_w, gdn_a_log, gdn_dt_bias, gdn_norm_g,
           diff_lambda_q1, diff_lambda_k1, diff_lambda_q2, diff_lambda_k2, diff_subln_g,
           w_branch_a, w_branch_b, w_out, norm_cross_g, norm_mem_g, w_mq, w_mk, w_mv, w_mo,
           norm_ffn_g, peer_w_q, peer_sub_keys, peer_u, peer_v, final_norm_g):
    depth = w_in.shape[0]
    assert depth == 1, "final norm is fused into the last layer's PEER kernel"
    l = 0
    wts = {
        "norm_mix_g": norm_mix_g[l], "w_in": _reorder_w_in(w_in[l]), "gdn_conv_w": gdn_conv_w[l],
        "gdn_a_log": gdn_a_log[l], "gdn_dt_bias": gdn_dt_bias[l], "gdn_norm_g": gdn_norm_g[l],
        "diff_lambda_q1": diff_lambda_q1[l], "diff_lambda_k1": diff_lambda_k1[l],
        "diff_lambda_q2": diff_lambda_q2[l], "diff_lambda_k2": diff_lambda_k2[l],
        "diff_subln_g": diff_subln_g[l], "w_branch_a": w_branch_a[l].astype(BF16),
        "w_branch_b": w_branch_b[l].astype(BF16), "w_out": w_out[l].astype(BF16),
        "norm_cross_g": norm_cross_g[l], "w_mq": w_mq[l].astype(BF16), "w_mo": w_mo[l].astype(BF16),
        "norm_ffn_g": norm_ffn_g[l], "peer_w_q": peer_w_q[l].astype(BF16), "peer_sub_keys": peer_sub_keys[l],
        "peer_u": peer_u[l].astype(BF16), "peer_v": peer_v[l].astype(BF16), "final_norm_g": final_norm_g,
    }
    bp, tp, d = x_prompt.shape
    bs, ts, _ = x_sample.shape
    mem2d = mem_prompt.reshape(-1, d)
    tmm = min(512, mem2d.shape[0])
    mem_k = _matmul(mem2d, w_mk[l].astype(BF16), norm_g=norm_mem_g[l], tm=tmm, tn=512, name="mem_k")
    mem_v = _matmul(mem2d, w_mv[l].astype(BF16), norm_g=norm_mem_g[l], tm=tmm, tn=512, name="mem_v")
    mshape = mem_prompt.shape[:2] + (MEM_HEADS, d // MEM_HEADS)
    mem_k = mem_k.reshape(mshape)
    mem_v = mem_v.reshape(mshape)
    conv0 = jnp.zeros((bp, GDN_CONV - 1, 3 * GDN_WIDTH), F32)
    s0 = jnp.zeros((bp, GDN_HEADS, HEAD_DIM, HEAD_DIM), F32)
    yp, pk, pv, ps, pc = _layer(x_prompt, wts, l, mem_k, mem_v, conv0, s0, None, None, tm=min(512, bp * tp))
    ys, sk, sv, ss, sc = _layer(x_sample, wts, l, cache_mem_k[l], cache_mem_v[l], state_conv[l], state_gdn[l],
                                cache_diff_k[l], cache_diff_v[l], tm=min(256, bs * ts))
    return (yp, ys, pk[None], pv[None], ps[None], pc[None], mem_k[None], mem_v[None],
            sk[None], sv[None], ss[None], sc[None])
```

```python
import functools
import math

import jax
import jax.numpy as jnp
from jax import lax
from jax.experimental import pallas as pl
from jax.experimental.pallas import tpu as pltpu

F32 = jnp.float32
BF16 = jnp.bfloat16
HIGHEST = lax.Precision.HIGHEST

RMS_EPS = 1e-6
CHUNK = 64
CHUNK_SHIFT = 6
DIFF_TQ = 512
GDN_HEADS = 8
HEAD_DIM = 128
GDN_WIDTH = GDN_HEADS * HEAD_DIM
GDN_CONV = 4
DIFF_HEADS = 8
DIFF_HEAD_DIM = 64
MEM_HEADS = 4
PEER_HEADS = 8
PEER_N_KEYS = 128
PEER_TOPK = 16
LANES = 128
SUBLANES = 8
V7X_VMEM_LIMIT = 52 * 1024 * 1024

COL_QKV = 0
COL_Z = 3072
COL_DQ = 4096
COL_DK = 5120
COL_DV = 6144
COL_GA = 7168
COL_GB = 9216
COL_BA = 11264
IN_COLS_PADDED = 11520
IN_TN = 768


def _cparams(sem, vmem=V7X_VMEM_LIMIT):
    return pltpu.CompilerParams(dimension_semantics=sem, vmem_limit_bytes=vmem)


def _dot(a, b, precision=None):
    return jnp.dot(a, b, preferred_element_type=F32, precision=precision)


def _dot_nt(a, b, precision=None):
    return lax.dot_general(a, b, (((1,), (1,)), ((), ())), preferred_element_type=F32, precision=precision)


def _rms(xf, g):
    return xf * lax.rsqrt(jnp.mean(xf * xf, axis=-1, keepdims=True) + RMS_EPS) * g


def _mm_kernel(*refs, has_norm, has_res):
    it = iter(refs)
    x_ref = next(it)
    g_ref = next(it) if has_norm else None
    w_ref = next(it)
    r_ref = next(it) if has_res else None
    o_ref = next(it)
    h_scr = next(it)

    @pl.when(pl.program_id(1) == 0)
    def _():
        xf = x_ref[...].astype(F32)
        if has_norm:
            xf = _rms(xf, g_ref[...])
        h_scr[...] = xf.astype(BF16)

    acc = _dot(h_scr[...], w_ref[...])
    if has_res:
        acc = acc + r_ref[...]
    o_ref[...] = acc.astype(o_ref.dtype)


def _matmul(x, w, *, norm_g=None, residual=None, tm, tn, out_dtype=F32, name="matmul"):
    m, k = x.shape
    n = w.shape[1]
    assert m % tm == 0 and n % tn == 0, (m, n, tm, tn)
    in_specs = [pl.BlockSpec((tm, k), lambda i, j: (i, 0))]
    args = [x]
    if norm_g is not None:
        in_specs.append(pl.BlockSpec((1, k), lambda i, j: (0, 0)))
        args.append(norm_g.reshape(1, k).astype(F32))
    in_specs.append(pl.BlockSpec((k, tn), lambda i, j: (0, j)))
    args.append(w)
    if residual is not None:
        in_specs.append(pl.BlockSpec((tm, tn), lambda i, j: (i, j)))
        args.append(residual)
    return pl.pallas_call(
        functools.partial(_mm_kernel, has_norm=norm_g is not None, has_res=residual is not None),
        out_shape=jax.ShapeDtypeStruct((m, n), out_dtype),
        grid=(m // tm, n // tn),
        in_specs=in_specs,
        out_specs=pl.BlockSpec((tm, tn), lambda i, j: (i, j)),
        scratch_shapes=[pltpu.VMEM((tm, k), BF16)],
        compiler_params=_cparams(("parallel", "arbitrary")),
        name=name,
    )(*args)


def _gdn_kernel(qkv_ref, z_ref, ba_ref, prev_ref, s0_ref, convw_ref, alog_ref, dtb_ref, ng_ref,
                o_ref, sfin_ref, s_scr, xbuf, *, L):
    c = pl.program_id(1)

    @pl.when(c == 0)
    def _():
        s_scr[...] = s0_ref[0]
        xbuf[0:SUBLANES, :] = prev_ref[0]

    xbuf[SUBLANES:SUBLANES + L, :] = qkv_ref[...]
    base = SUBLANES - (GDN_CONV - 1)
    y = xbuf[base:base + L, :] * convw_ref[0:1, :]
    for i in range(1, GDN_CONV):
        y = y + xbuf[base + i:base + i + L, :] * convw_ref[i:i + 1, :]
    y = y * jax.nn.sigmoid(y)
    tail = xbuf[L:L + SUBLANES, :]
    xbuf[0:SUBLANES, :] = tail

    row = lax.broadcasted_iota(jnp.int32, (L, L), 0)
    col = lax.broadcasted_iota(jnp.int32, (L, L), 1)
    lower = row >= col
    strict = row > col
    tril = lower.astype(F32)
    triu = (row <= col).astype(F32)
    ones = jnp.ones((L, L), F32)
    eye = (row == col).astype(F32)
    ba = ba_ref[...]
    n_double = int(math.log2(L)) - 1
    assert 2 ** (n_double + 1) == L

    for h in range(GDN_HEADS):
        lo, hi = h * HEAD_DIM, (h + 1) * HEAD_DIM
        q = y[:, lo:hi]
        k = y[:, GDN_WIDTH + lo:GDN_WIDTH + hi]
        v = y[:, 2 * GDN_WIDTH + lo:2 * GDN_WIDTH + hi]
        q = q * lax.rsqrt(jnp.sum(q * q, axis=-1, keepdims=True) + 1e-6) * (HEAD_DIM ** -0.5)
        k = k * lax.rsqrt(jnp.sum(k * k, axis=-1, keepdims=True) + 1e-6)
        beta = jax.nn.sigmoid(ba[:, h:h + 1])
        a_in = ba[:, GDN_HEADS + h:GDN_HEADS + h + 1] + dtb_ref[0:1, h:h + 1]
        softplus = jnp.maximum(a_in, 0.0) + jnp.log(1.0 + jnp.exp(-jnp.abs(a_in)))
        g = -jnp.exp(alog_ref[0:1, h:h + 1]) * softplus
        gmat = jnp.broadcast_to(g, (L, L))
        g_i = _dot(tril, gmat, HIGHEST)
        g_j = _dot(ones, gmat * triu, HIGHEST)
        decay = jnp.where(lower, jnp.exp(jnp.where(lower, g_i - g_j, 0.0)), 0.0)
        g_col = g_i[:, 0:1]
        g_last = g_i[L - 1:L, 0:1]
        e_col = jnp.exp(g_col)
        kk = _dot_nt(k, k, HIGHEST)
        a_mat = jnp.where(strict, beta * kk * decay, 0.0)
        x = -a_mat
        t_inv = eye + x
        for _ in range(n_double):
            x = _dot(x, x, HIGHEST)
            t_inv = t_inv + _dot(t_inv, x, HIGHEST)
        rhs = jnp.concatenate([v * beta, k * (beta * e_col)], axis=-1)
        sol = _dot(t_inv, rhs, HIGHEST)
        u = sol[:, :HEAD_DIM]
        w = sol[:, HEAD_DIM:]
        qk = jnp.where(lower, _dot_nt(q, k, HIGHEST) * decay, 0.0)
        q_dec = q * e_col
        k_dec = k * jnp.exp(g_last - g_col)
        g_end = jnp.exp(g_last)
        s = s_scr[h]
        v_new = u - _dot(w, s, HIGHEST)
        o = _dot(q_dec, s, HIGHEST) + _dot(qk, v_new, HIGHEST)
        s_scr[h] = s * g_end + _dot(k_dec.T, v_new, HIGHEST)
        zf = z_ref[:, lo:hi]
        o = _rms(o, ng_ref[...]) * (zf * jax.nn.sigmoid(zf))
        o_ref[:, lo:hi] = o.astype(o_ref.dtype)

    @pl.when(c == pl.num_programs(1) - 1)
    def _():
        sfin_ref[0] = s_scr[...]


def _gdn(proj, conv_prev8, s0, conv_w, a_log, dt_bias, norm_g, *, bsz, t):
    L = min(CHUNK, t)
    nc = t // L
    conv_dim = 3 * GDN_WIDTH
    kern = functools.partial(_gdn_kernel, L=L)
    o, s_fin = pl.pallas_call(
        kern,
        out_shape=(jax.ShapeDtypeStruct((bsz * t, GDN_WIDTH), BF16),
                   jax.ShapeDtypeStruct((bsz, GDN_HEADS, HEAD_DIM, HEAD_DIM), F32)),
        grid=(bsz, nc),
        in_specs=[
            pl.BlockSpec((L, conv_dim), lambda b, c: (b * nc + c, COL_QKV // conv_dim)),
            pl.BlockSpec((L, GDN_WIDTH), lambda b, c: (b * nc + c, COL_Z // GDN_WIDTH)),
            pl.BlockSpec((L, LANES), lambda b, c: (b * nc + c, COL_BA // LANES)),
            pl.BlockSpec((1, SUBLANES, conv_dim), lambda b, c: (b, 0, 0)),
            pl.BlockSpec((1, GDN_HEADS, HEAD_DIM, HEAD_DIM), lambda b, c: (b, 0, 0, 0)),
            pl.BlockSpec((GDN_CONV, conv_dim), lambda b, c: (0, 0)),
            pl.BlockSpec((1, GDN_HEADS), lambda b, c: (0, 0)),
            pl.BlockSpec((1, GDN_HEADS), lambda b, c: (0, 0)),
            pl.BlockSpec((1, HEAD_DIM), lambda b, c: (0, 0)),
        ],
        out_specs=(pl.BlockSpec((L, GDN_WIDTH), lambda b, c: (b * nc + c, 0)),
                   pl.BlockSpec((1, GDN_HEADS, HEAD_DIM, HEAD_DIM), lambda b, c: (b, 0, 0, 0))),
        scratch_shapes=[pltpu.VMEM((GDN_HEADS, HEAD_DIM, HEAD_DIM), F32),
                        pltpu.VMEM((SUBLANES + L + SUBLANES, conv_dim), F32)],
        compiler_params=_cparams(("parallel", "arbitrary")),
        name="gdn",
    )(proj, proj, proj, conv_prev8, s0, conv_w, a_log.reshape(1, -1), dt_bias.reshape(1, -1),
      norm_g.reshape(1, -1))
    return o, s_fin


def _lambda_value(lam_refs, lam_init):
    lq1, lk1, lq2, lk2 = (r[...] for r in lam_refs)
    return (jnp.exp(jnp.sum(lq1 * lk1, axis=-1, keepdims=True))
            - jnp.exp(jnp.sum(lq2 * lk2, axis=-1, keepdims=True)) + lam_init)


def _softmax_update(m_scr, l_scr, acc_scr, mi, s, vb):
    m_old = m_scr[mi]
    m_new = jnp.maximum(m_old, jnp.max(s, axis=-1, keepdims=True))
    alpha = jnp.exp(m_old - m_new)
    p = jnp.exp(s - m_new)
    l_scr[mi] = alpha * l_scr[mi] + jnp.sum(p, axis=-1, keepdims=True)
    acc_scr[mi] = alpha * acc_scr[mi] + _dot(p.astype(BF16), vb)
    m_scr[mi] = m_new


def _diff_scores(qb, kb, bias):
    out = []
    for mi in range(2):
        sl = slice(mi * DIFF_HEAD_DIM, (mi + 1) * DIFF_HEAD_DIM)
        out.append(_dot_nt(qb[:, sl], kb[:, sl]) * (DIFF_HEAD_DIM ** -0.5) + bias)
    return out


def _diff_finalize(m_scr, l_scr, acc_scr, lam_refs, g_ref, o_ref, lam_init):
    lam = _lambda_value(lam_refs, lam_init)
    o = acc_scr[0] / l_scr[0] - lam * (acc_scr[1] / l_scr[1])
    o = _rms(o, g_ref[...]) * (1.0 - lam_init)
    o_ref[...] = o.astype(o_ref.dtype)


def _diff_init(m_scr, l_scr, acc_scr):
    m_scr[...] = jnp.full(m_scr.shape, -jnp.inf, F32)
    l_scr[...] = jnp.zeros(l_scr.shape, F32)
    acc_scr[...] = jnp.zeros(acc_scr.shape, F32)


def _alibi_bias(slope, q_pos, k_pos):
    dist = jnp.abs(q_pos - k_pos).astype(F32)
    visible = jnp.right_shift(k_pos, CHUNK_SHIFT) <= jnp.right_shift(q_pos, CHUNK_SHIFT)
    return jnp.where(visible, -slope * dist, -jnp.inf)


def _diff_prompt_kernel(slopes_ref, q_ref, k_ref, v_ref, lq1, lk1, lq2, lk2, g_ref, o_ref,
                        m_scr, l_scr, acc_scr, *, tq, lam_init):
    h = pl.program_id(1)
    i = pl.program_id(2)
    j = pl.program_id(3)

    @pl.when(j == 0)
    def _():
        _diff_init(m_scr, l_scr, acc_scr)

    @pl.when(j <= i)
    def _():
        q_pos = i * tq + lax.broadcasted_iota(jnp.int32, (tq, tq), 0)
        k_pos = j * tq + lax.broadcasted_iota(jnp.int32, (tq, tq), 1)
        bias = _alibi_bias(slopes_ref[h], q_pos, k_pos)
        qb = q_ref[...].astype(BF16)
        kb = k_ref[...].astype(BF16)
        vb = v_ref[...].astype(BF16)
        for mi, s in enumerate(_diff_scores(qb, kb, bias)):
            _softmax_update(m_scr, l_scr, acc_scr, mi, s, vb)

    @pl.when(j == pl.num_programs(3) - 1)
    def _():
        _diff_finalize(m_scr, l_scr, acc_scr, (lq1, lk1, lq2, lk2), g_ref, o_ref, lam_init)


def _alibi_slopes():
    return jnp.asarray([2.0 ** (-8.0 * (i + 1) / DIFF_HEADS) for i in range(DIFF_HEADS)], F32)


def _lam_specs(nidx):
    zero = {3: lambda a, b, c: (0, 0), 4: lambda a, b, c, d: (0, 0)}[nidx]
    return [pl.BlockSpec((1, DIFF_HEAD_DIM), zero)] * 4 + [pl.BlockSpec((1, HEAD_DIM), zero)]


def _diff_prompt(proj, lams, subln_g, *, bsz, t, lam_init):
    tq = min(DIFF_TQ, t)
    nq = t // tq
    kern = functools.partial(_diff_prompt_kernel, tq=tq, lam_init=lam_init)
    qc, kc, vc = COL_DQ // LANES, COL_DK // LANES, COL_DV // LANES
    return pl.pallas_call(
        kern,
        out_shape=jax.ShapeDtypeStruct((bsz * t, DIFF_HEADS * HEAD_DIM), BF16),
        grid=(bsz, DIFF_HEADS, nq, nq),
        in_specs=[
            pl.BlockSpec(memory_space=pltpu.SMEM),
            pl.BlockSpec((tq, LANES), lambda b, h, i, j: (b * nq + i, qc + h)),
            pl.BlockSpec((tq, LANES), lambda b, h, i, j: (b * nq + jnp.minimum(i, j), kc + h)),
            pl.BlockSpec((tq, LANES), lambda b, h, i, j: (b * nq + jnp.minimum(i, j), vc + h)),
        ] + _lam_specs(4),
        out_specs=pl.BlockSpec((tq, LANES), lambda b, h, i, j: (b * nq + i, h)),
        scratch_shapes=[pltpu.VMEM((2, tq, 1), F32), pltpu.VMEM((2, tq, 1), F32),
                        pltpu.VMEM((2, tq, HEAD_DIM), F32)],
        compiler_params=_cparams(("parallel", "parallel", "parallel", "arbitrary")),
        name="diff_attn_prompt",
    )(_alibi_slopes(), proj, proj, proj, *lams, subln_g.reshape(1, -1))


def _diff_sample_kernel(slopes_ref, q_ref, kn_ref, vn_ref, kp_ref, vp_ref, lq1, lk1, lq2, lk2, g_ref, o_ref,
                        m_scr, l_scr, acc_scr, *, t, tk, past, lam_init):
    h = pl.program_id(1)
    j = pl.program_id(2)
    slope = slopes_ref[h]
    qb = q_ref[...].astype(BF16)

    @pl.when(j == 0)
    def _():
        _diff_init(m_scr, l_scr, acc_scr)
        q_pos = past + lax.broadcasted_iota(jnp.int32, (t, t), 0)
        k_pos = past + lax.broadcasted_iota(jnp.int32, (t, t), 1)
        bias = _alibi_bias(slope, q_pos, k_pos)
        vb = vn_ref[...].astype(BF16)
        for mi, s in enumerate(_diff_scores(qb, kn_ref[...].astype(BF16), bias)):
            _softmax_update(m_scr, l_scr, acc_scr, mi, s, vb)

    q_pos = past + lax.broadcasted_iota(jnp.int32, (t, tk), 0)
    k_pos = j * tk + lax.broadcasted_iota(jnp.int32, (t, tk), 1)
    bias = _alibi_bias(slope, q_pos, k_pos)
    vb = vp_ref[...].astype(BF16)
    for mi, s in enumerate(_diff_scores(qb, kp_ref[...].astype(BF16), bias)):
        _softmax_update(m_scr, l_scr, acc_scr, mi, s, vb)

    @pl.when(j == pl.num_programs(2) - 1)
    def _():
        _diff_finalize(m_scr, l_scr, acc_scr, (lq1, lk1, lq2, lk2), g_ref, o_ref, lam_init)


def _diff_sample(proj, k_past, v_past, lams, subln_g, *, bsz, t, lam_init):
    past = k_past.shape[1]
    tk = min(1024, past)
    nk = past // tk
    kp = k_past.reshape(bsz * past, DIFF_HEADS * HEAD_DIM)
    vp = v_past.reshape(bsz * past, DIFF_HEADS * HEAD_DIM)
    kern = functools.partial(_diff_sample_kernel, t=t, tk=tk, past=past, lam_init=lam_init)
    qc, kc, vc = COL_DQ // LANES, COL_DK // LANES, COL_DV // LANES
    return pl.pallas_call(
        kern,
        out_shape=jax.ShapeDtypeStruct((bsz * t, DIFF_HEADS * HEAD_DIM), BF16),
        grid=(bsz, DIFF_HEADS, nk),
        in_specs=[
            pl.BlockSpec(memory_space=pltpu.SMEM),
            pl.BlockSpec((t, LANES), lambda b, h, j: (b, qc + h)),
            pl.BlockSpec((t, LANES), lambda b, h, j: (b, kc + h)),
            pl.BlockSpec((t, LANES), lambda b, h, j: (b, vc + h)),
            pl.BlockSpec((tk, LANES), lambda b, h, j: (b * nk + j, h)),
            pl.BlockSpec((tk, LANES), lambda b, h, j: (b * nk + j, h)),
        ] + _lam_specs(3),
        out_specs=pl.BlockSpec((t, LANES), lambda b, h, j: (b, h)),
        scratch_shapes=[pltpu.VMEM((2, t, 1), F32), pltpu.VMEM((2, t, 1), F32),
                        pltpu.VMEM((2, t, HEAD_DIM), F32)],
        compiler_params=_cparams(("parallel", "parallel", "arbitrary")),
        name="diff_attn_sample",
    )(_alibi_slopes(), proj, proj, proj, kp, vp, *lams, subln_g.reshape(1, -1))


def _merge_kernel(oa_ref, ob_ref, wa_ref, wb_ref, ga_ref, gb_ref, o_ref):
    ya = _dot(oa_ref[...], wa_ref[...])
    yb = _dot(ob_ref[...], wb_ref[...])
    o_ref[...] = (jax.nn.sigmoid(ga_ref[...]) * ya + jax.nn.sigmoid(gb_ref[...]) * yb).astype(o_ref.dtype)


def _merge(o_a, o_b, wa, wb, proj, *, tm):
    m = o_a.shape[0]
    d = wa.shape[1]
    tn = 512
    ga0, gb0 = COL_GA // tn, COL_GB // tn
    return pl.pallas_call(
        _merge_kernel,
        out_shape=jax.ShapeDtypeStruct((m, d), BF16),
        grid=(m // tm, d // tn),
        in_specs=[
            pl.BlockSpec((tm, o_a.shape[1]), lambda i, j: (i, 0)),
            pl.BlockSpec((tm, o_b.shape[1]), lambda i, j: (i, 0)),
            pl.BlockSpec((wa.shape[0], tn), lambda i, j: (0, j)),
            pl.BlockSpec((wb.shape[0], tn), lambda i, j: (0, j)),
            pl.BlockSpec((tm, tn), lambda i, j: (i, ga0 + j)),
            pl.BlockSpec((tm, tn), lambda i, j: (i, gb0 + j)),
        ],
        out_specs=pl.BlockSpec((tm, tn), lambda i, j: (i, j)),
        compiler_params=_cparams(("parallel", "arbitrary")),
        name="merge",
    )(o_a, o_b, wa, wb, proj, proj)


def _cross_kernel(q_ref, k_ref, v_ref, o_ref):
    dh = q_ref.shape[-1] // MEM_HEADS
    for h in range(MEM_HEADS):
        sl = slice(h * dh, (h + 1) * dh)
        s = _dot_nt(q_ref[:, sl], k_ref[0, :, sl].astype(BF16)) * (dh ** -0.5)
        s = s - jnp.max(s, axis=-1, keepdims=True)
        p = jnp.exp(s)
        p = p / jnp.sum(p, axis=-1, keepdims=True)
        o_ref[:, sl] = _dot(p.astype(BF16), v_ref[0, :, sl].astype(BF16)).astype(o_ref.dtype)


def _cross_attend(q, mem_k, mem_v, *, bsz, t, tm):
    d = q.shape[1]
    nt = t // tm
    mt = mem_k.shape[1]
    return pl.pallas_call(
        _cross_kernel,
        out_shape=jax.ShapeDtypeStruct(q.shape, BF16),
        grid=(bsz, nt),
        in_specs=[
            pl.BlockSpec((tm, d), lambda b, i: (b * nt + i, 0)),
            pl.BlockSpec((1, mt, d), lambda b, i: (b, 0, 0)),
            pl.BlockSpec((1, mt, d), lambda b, i: (b, 0, 0)),
        ],
        out_specs=pl.BlockSpec((tm, d), lambda b, i: (b * nt + i, 0)),
        compiler_params=_cparams(("parallel", "arbitrary")),
        name="cross_attn",
    )(q, mem_k, mem_v)


def _topk_rows(work, cidx, k):
    n, tn = work.shape
    rid = lax.broadcasted_iota(jnp.int32, (n, tn), 0)
    kid = lax.broadcasted_iota(jnp.int32, (k, tn), 0)
    vals = jnp.zeros((k, tn), F32)
    idxs = jnp.zeros((k, tn), jnp.int32)
    for t in range(k):
        m = jnp.max(work, axis=0, keepdims=True)
        pos = jnp.min(jnp.where(work == m, rid, n), axis=0, keepdims=True)
        hit = rid == pos
        if cidx is None:
            picked = pos
        else:
            picked = jnp.sum(jnp.where(hit, cidx, 0), axis=0, keepdims=True)
        vals = jnp.where(kid == t, m, vals)
        idxs = jnp.where(kid == t, picked, idxs)
        work = jnp.where(hit, -jnp.inf, work)
    return vals, idxs


def _peer_route_kernel(q_ref, keys_ref, e_ref, g_ref):
    tn = q_ref.shape[0]
    k = PEER_TOPK
    for h in range(PEER_HEADS):
        sub = []
        for p in range(2):
            c0 = (h * 2 + p) * PEER_N_KEYS
            s = _dot_nt(keys_ref[h, p], q_ref[:, c0:c0 + PEER_N_KEYS], HIGHEST)
            sub.append(_topk_rows(s, None, k))
        (v1, i1), (v2, i2) = sub
        cand = jnp.concatenate([v1[a:a + 1, :] + v2 for a in range(k)], axis=0)
        cidx = jnp.concatenate([i1[a:a + 1, :] * PEER_N_KEYS + i2 for a in range(k)], axis=0)
        best, eidx = _topk_rows(cand, cidx, k)
        ex = jnp.exp(best - jnp.max(best, axis=0, keepdims=True))
        gate = ex / jnp.sum(ex, axis=0, keepdims=True)
        e_ref[h * k:(h + 1) * k, :] = eidx
        g_ref[h * k:(h + 1) * k, :] = gate


HALF_KEYS = PEER_N_KEYS // 2
BF16_HIGH_MASK = 0xFFFF0000


def _peer_gate_kernel(et_ref, gt_ref, o_ref, e_scr, g_scr):
    tg = et_ref.shape[1]
    e_scr[...] = et_ref[...].T
    g_scr[...] = gt_ref[...].T
    kid = lax.broadcasted_iota(jnp.int32, (PEER_N_KEYS, PEER_N_KEYS), 0)

    def body(n8, carry):
        rows = pl.ds(pl.multiple_of(n8 * SUBLANES, SUBLANES), SUBLANES)
        e8 = e_scr[rows, :]
        g8 = g_scr[rows, :]
        a8 = jnp.right_shift(e8, 7)
        b8 = jnp.bitwise_and(e8, PEER_N_KEYS - 1)
        for s in range(SUBLANES):
            at = jnp.where(a8[s:s + 1, :] == kid, g8[s:s + 1, :], 0.0).astype(BF16)
            bt = jnp.where(b8[s:s + 1, :] == kid, 1.0, 0.0).astype(BF16)
            grid = _dot_nt(at, bt)
            bits = pltpu.bitcast(grid.astype(BF16).astype(F32), jnp.uint32)
            word = jnp.bitwise_or(jnp.bitwise_and(bits[HALF_KEYS:, :], jnp.uint32(BF16_HIGH_MASK)),
                                  jnp.right_shift(bits[:HALF_KEYS, :], jnp.uint32(16)))
            o_ref[n8 * SUBLANES + s] = word
        return carry

    lax.fori_loop(0, tg // SUBLANES, body, 0)


def _peer_route(qp, sub_keys, *, tn):
    n = qp.shape[0]
    rows = PEER_HEADS * PEER_TOPK
    e_t, g_t = pl.pallas_call(
        _peer_route_kernel,
        out_shape=(jax.ShapeDtypeStruct((rows, n), jnp.int32), jax.ShapeDtypeStruct((rows, n), F32)),
        grid=(n // tn,),
        in_specs=[
            pl.BlockSpec((tn, qp.shape[1]), lambda i: (i, 0)),
            pl.BlockSpec(sub_keys.shape, lambda i: (0, 0, 0, 0)),
        ],
        out_specs=(pl.BlockSpec((rows, tn), lambda i: (0, i)), pl.BlockSpec((rows, tn), lambda i: (0, i))),
        compiler_params=_cparams(("parallel",)),
        name="peer_route",
    )(qp, sub_keys)
    return pl.pallas_call(
        _peer_gate_kernel,
        out_shape=jax.ShapeDtypeStruct((n, HALF_KEYS, PEER_N_KEYS), jnp.uint32),
        grid=(n // tn,),
        in_specs=[pl.BlockSpec((rows, tn), lambda i: (0, i)), pl.BlockSpec((rows, tn), lambda i: (0, i))],
        out_specs=pl.BlockSpec((tn, HALF_KEYS, PEER_N_KEYS), lambda i: (i, 0, 0)),
        scratch_shapes=[pltpu.VMEM((tn, rows), jnp.int32), pltpu.VMEM((tn, rows), F32)],
        compiler_params=_cparams(("parallel",)),
        name="peer_gates",
    )(e_t, g_t)


def _gelu_tanh(x):
    return 0.5 * x * (1.0 + jnp.tanh(math.sqrt(2.0 / math.pi) * (x + 0.044715 * (x * x * x))))


def _peer_expert_kernel(x_ref, gn_ref, gp_ref, u1_ref, u2_ref, v1_ref, v2_ref, gf_ref, o_ref, h_scr, acc_scr):
    r = pl.program_id(1)

    @pl.when(r == 0)
    def _():
        h_scr[...] = _rms(x_ref[...], gn_ref[...]).astype(BF16)
        acc_scr[...] = jnp.zeros(acc_scr.shape, F32)

    word = gp_ref[...]
    g_lo = pltpu.bitcast(jnp.left_shift(word, jnp.uint32(16)), F32)
    g_hi = pltpu.bitcast(jnp.bitwise_and(word, jnp.uint32(BF16_HIGH_MASK)), F32)
    u_cat = jnp.concatenate([u1_ref[...], u2_ref[...]], axis=0)
    v_cat = jnp.concatenate([v1_ref[...], v2_ref[...]], axis=0)
    hid = _gelu_tanh(_dot_nt(h_scr[...], u_cat))
    coef = (jnp.concatenate([g_lo, g_hi], axis=1) * hid).astype(BF16)
    acc_scr[...] += _dot(coef, v_cat)

    @pl.when(r == pl.num_programs(1) - 1)
    def _():
        o_ref[...] = _rms(x_ref[...] + acc_scr[...], gf_ref[...])


def _peer_experts(x, norm_g, gates, u_tab, v_tab, final_g, *, tn):
    n, d = x.shape
    eb = PEER_N_KEYS
    assert u_tab.shape[0] == eb * eb
    tab1 = pl.BlockSpec((eb, d), lambda i, r: (r, 0))
    tab2 = pl.BlockSpec((eb, d), lambda i, r: (r + HALF_KEYS, 0))
    return pl.pallas_call(
        _peer_expert_kernel,
        out_shape=jax.ShapeDtypeStruct((n, d), F32),
        grid=(n // tn, HALF_KEYS),
        in_specs=[
            pl.BlockSpec((tn, d), lambda i, r: (i, 0)),
            pl.BlockSpec((1, d), lambda i, r: (0, 0)),
            pl.BlockSpec((tn, eb), lambda i, r: (i, r)),
            tab1, tab2, tab1, tab2,
            pl.BlockSpec((1, d), lambda i, r: (0, 0)),
        ],
        out_specs=pl.BlockSpec((tn, d), lambda i, r: (i, 0)),
        scratch_shapes=[pltpu.VMEM((tn, d), BF16), pltpu.VMEM((tn, d), F32)],
        compiler_params=_cparams(("parallel", "arbitrary")),
        name="peer_experts",
    )(x, norm_g.reshape(1, d), gates.reshape(n, HALF_KEYS * eb), u_tab, u_tab, v_tab, v_tab, final_g.reshape(1, d))


def _reorder_w_in(w_in):
    c_qkv = 3 * GDN_WIDTH
    c_z = c_qkv + GDN_WIDTH
    c_ba = c_z + 2 * GDN_HEADS
    rest = w_in[:, c_ba:]
    pad = jnp.zeros((w_in.shape[0], IN_COLS_PADDED - COL_BA - 2 * GDN_HEADS), w_in.dtype)
    return jnp.concatenate([w_in[:, :c_z], rest, w_in[:, c_z:c_ba], pad], axis=1).astype(BF16)


def _layer(x, wts, layer, mem_k, mem_v, conv_prev, gdn_state, k_past, v_past, *, tm):
    bsz, t, d = x.shape
    n = bsz * t
    x2d = x.reshape(n, d)
    proj = _matmul(x2d, wts["w_in"], norm_g=wts["norm_mix_g"], tm=tm, tn=IN_TN, name="in_proj")
    conv_dim = 3 * GDN_WIDTH
    conv_prev8 = jnp.concatenate(
        [jnp.zeros((bsz, SUBLANES - (GDN_CONV - 1), conv_dim), F32), conv_prev.astype(F32)], axis=1)
    o_a, s_new = _gdn(proj, conv_prev8, gdn_state.astype(F32), wts["gdn_conv_w"], wts["gdn_a_log"],
                      wts["gdn_dt_bias"], wts["gdn_norm_g"], bsz=bsz, t=t)
    conv_new = proj.reshape(bsz, t, -1)[:, t - (GDN_CONV - 1):, COL_QKV:COL_QKV + conv_dim]
    lam_init = 0.8 - 0.6 * math.exp(-0.3 * layer)
    lams = tuple(wts[k].reshape(1, -1) for k in ("diff_lambda_q1", "diff_lambda_k1", "diff_lambda_q2", "diff_lambda_k2"))
    if k_past is None:
        o_b = _diff_prompt(proj, lams, wts["diff_subln_g"], bsz=bsz, t=t, lam_init=lam_init)
    else:
        o_b = _diff_sample(proj, k_past, v_past, lams, wts["diff_subln_g"], bsz=bsz, t=t, lam_init=lam_init)
    k_rows = proj[:, COL_DK:COL_DK + DIFF_HEADS * HEAD_DIM].reshape(bsz, t, DIFF_HEADS, HEAD_DIM)
    v_rows = proj[:, COL_DV:COL_DV + DIFF_HEADS * HEAD_DIM].reshape(bsz, t, DIFF_HEADS, HEAD_DIM)
    merged = _merge(o_a, o_b, wts["w_branch_a"], wts["w_branch_b"], proj, tm=tm)
    x1 = _matmul(merged, wts["w_out"], residual=x2d, tm=tm, tn=512, name="out_proj")
    qm = _matmul(x1, wts["w_mq"], norm_g=wts["norm_cross_g"], tm=tm, tn=512, out_dtype=BF16, name="mem_q")
    mt = mem_k.shape[1]
    oc = _cross_attend(qm, mem_k.reshape(bsz, mt, d), mem_v.reshape(bsz, mt, d), bsz=bsz, t=t, tm=min(tm, t))
    x2 = _matmul(oc, wts["w_mo"], residual=x1, tm=tm, tn=512, name="mem_o")
    qp = _matmul(x2, wts["peer_w_q"], norm_g=wts["norm_ffn_g"], tm=tm, tn=512, name="peer_q")
    tn = min(512, n)
    gates = _peer_route(qp, wts["peer_sub_keys"], tn=LANES)
    y = _peer_experts(x2, wts["norm_ffn_g"], gates, wts["peer_u"], wts["peer_v"], wts["final_norm_g"], tn=tn)
    return y.reshape(bsz, t, d), k_rows, v_rows, s_new, conv_new


def kernel(x_prompt, x_sample, cache_diff_k, cache_diff_v, state_gdn, state_conv, cache_mem_k, cache_mem_v,
           mem_prompt, norm_mix_g, w_in, gdn_conv_w, gdn_a_log, gdn_dt_bias, gdn_norm_g,
           diff_lambda_q1, diff_lambda_k1, diff_lambda_q2, diff_lambda_k2, diff_subln_g,
           w_branch_a, w_branch_b, w_out, norm_cross_g, norm_mem_g, w_mq, w_mk, w_mv, w_mo,
           norm_ffn_g, peer_w_q, peer_sub_keys, peer_u, peer_v, final_norm_g):
    depth = w_in.shape[0]
    assert depth == 1, "final norm is fused into the last layer's PEER kernel"
    l = 0
    wts = {
        "norm_mix_g": norm_mix_g[l], "w_in": _reorder_w_in(w_in[l]), "gdn_conv_w": gdn_conv_w[l],
        "gdn_a_log": gdn_a_log[l], "gdn_dt_bias": gdn_dt_bias[l], "gdn_norm_g": gdn_norm_g[l],
        "diff_lambda_q1": diff_lambda_q1[l], "diff_lambda_k1": diff_lambda_k1[l],
        "diff_lambda_q2": diff_lambda_q2[l], "diff_lambda_k2": diff_lambda_k2[l],
        "diff_subln_g": diff_subln_g[l], "w_branch_a": w_branch_a[l].astype(BF16),
        "w_branch_b": w_branch_b[l].astype(BF16), "w_out": w_out[l].astype(BF16),
        "norm_cross_g": norm_cross_g[l], "w_mq": w_mq[l].astype(BF16), "w_mo": w_mo[l].astype(BF16),
        "norm_ffn_g": norm_ffn_g[l], "peer_w_q": peer_w_q[l].astype(BF16), "peer_sub_keys": peer_sub_keys[l],
        "peer_u": peer_u[l].astype(BF16), "peer_v": peer_v[l].astype(BF16), "final_norm_g": final_norm_g,
    }
    bp, tp, d = x_prompt.shape
    bs, ts, _ = x_sample.shape
    mem2d = mem_prompt.reshape(-1, d)
    tmm = min(512, mem2d.shape[0])
    mem_k = _matmul(mem2d, w_mk[l].astype(BF16), norm_g=norm_mem_g[l], tm=tmm, tn=512, name="mem_k")
    mem_v = _matmul(mem2d, w_mv[l].astype(BF16), norm_g=norm_mem_g[l], tm=tmm, tn=512, name="mem_v")
    mshape = mem_prompt.shape[:2] + (MEM_HEADS, d // MEM_HEADS)
    mem_k = mem_k.reshape(mshape)
    mem_v = mem_v.reshape(mshape)
    conv0 = jnp.zeros((bp, GDN_CONV - 1, 3 * GDN_WIDTH), F32)
    s0 = jnp.zeros((bp, GDN_HEADS, HEAD_DIM, HEAD_DIM), F32)
    yp, pk, pv, ps, pc = _layer(x_prompt, wts, l, mem_k, mem_v, conv0, s0, None, None, tm=min(512, bp * tp))
    ys, sk, sv, ss, sc = _layer(x_sample, wts, l, cache_mem_k[l], cache_mem_v[l], state_conv[l], state_gdn[l],
                                cache_diff_k[l], cache_diff_v[l], tm=min(256, bs * ts))
    return (yp, ys, pk[None], pv[None], ps[None], pc[None], mem_k[None], mem_v[None],
            sk[None], sv[None], ss[None], sc[None])
```

```python
import functools
import math

import jax
import jax.numpy as jnp
from jax import lax
from jax.experimental import pallas as pl
from jax.experimental.pallas import tpu as pltpu

F32 = jnp.float32
BF16 = jnp.bfloat16
HIGHEST = lax.Precision.HIGHEST

RMS_EPS = 1e-6
CHUNK = 64
CHUNK_SHIFT = 6
DIFF_TQ = 512
DIFF_SUB = 256
GDN_HEADS = 8
HEAD_DIM = 128
GDN_WIDTH = GDN_HEADS * HEAD_DIM
GDN_CONV = 4
DIFF_HEADS = 8
DIFF_HEAD_DIM = 64
MEM_HEADS = 4
PEER_HEADS = 8
PEER_N_KEYS = 128
PEER_TOPK = 16
LANES = 128
SUBLANES = 8
V7X_VMEM_LIMIT = 52 * 1024 * 1024

COL_QKV = 0
COL_Z = 3072
COL_DQ = 4096
COL_DK = 5120
COL_DV = 6144
COL_GA = 7168
COL_GB = 9216
COL_BA = 11264
IN_COLS_PADDED = 11520
IN_TN = 768


def _cparams(sem, vmem=V7X_VMEM_LIMIT):
    return pltpu.CompilerParams(dimension_semantics=sem, vmem_limit_bytes=vmem)


def _dot(a, b, precision=None):
    return jnp.dot(a, b, preferred_element_type=F32, precision=precision)


def _dot_nt(a, b, precision=None):
    return lax.dot_general(a, b, (((1,), (1,)), ((), ())), preferred_element_type=F32, precision=precision)


def _split_bf16(x):
    hi = x.astype(BF16)
    return hi, (x - hi.astype(F32)).astype(BF16)


def _dot3_impl(dot, a, b):
    a_hi, a_lo = _split_bf16(a)
    b_hi, b_lo = _split_bf16(b)
    m = a.shape[0]
    both = dot(jnp.concatenate([a_hi, a_lo], axis=0), b_hi)
    return both[:m] + both[m:] + dot(a_hi, b_lo)


def _dot3(a, b):
    return _dot3_impl(_dot, a, b)


def _dot3_nt(a, b):
    return _dot3_impl(_dot_nt, a, b)


def _rms(xf, g):
    return xf * lax.rsqrt(jnp.mean(xf * xf, axis=-1, keepdims=True) + RMS_EPS) * g


def _mm_kernel(*refs, has_norm, has_res):
    it = iter(refs)
    x_ref = next(it)
    g_ref = next(it) if has_norm else None
    w_ref = next(it)
    r_ref = next(it) if has_res else None
    o_ref = next(it)
    h_scr = next(it)

    @pl.when(pl.program_id(1) == 0)
    def _():
        xf = x_ref[...].astype(F32)
        if has_norm:
            xf = _rms(xf, g_ref[...])
        h_scr[...] = xf.astype(BF16)

    acc = _dot(h_scr[...], w_ref[...])
    if has_res:
        acc = acc + r_ref[...]
    o_ref[...] = acc.astype(o_ref.dtype)


def _matmul(x, w, *, norm_g=None, residual=None, tm, tn, out_dtype=F32, name="matmul"):
    m, k = x.shape
    n = w.shape[1]
    assert m % tm == 0 and n % tn == 0, (m, n, tm, tn)
    in_specs = [pl.BlockSpec((tm, k), lambda i, j: (i, 0))]
    args = [x]
    if norm_g is not None:
        in_specs.append(pl.BlockSpec((1, k), lambda i, j: (0, 0)))
        args.append(norm_g.reshape(1, k).astype(F32))
    in_specs.append(pl.BlockSpec((k, tn), lambda i, j: (0, j)))
    args.append(w)
    if residual is not None:
        in_specs.append(pl.BlockSpec((tm, tn), lambda i, j: (i, j)))
        args.append(residual)
    return pl.pallas_call(
        functools.partial(_mm_kernel, has_norm=norm_g is not None, has_res=residual is not None),
        out_shape=jax.ShapeDtypeStruct((m, n), out_dtype),
        grid=(m // tm, n // tn),
        in_specs=in_specs,
        out_specs=pl.BlockSpec((tm, tn), lambda i, j: (i, j)),
        scratch_shapes=[pltpu.VMEM((tm, k), BF16)],
        compiler_params=_cparams(("parallel", "arbitrary")),
        name=name,
    )(*args)


def _gdn_kernel(qkv_ref, z_ref, ba_ref, prev_ref, s0_ref, convw_ref, alog_ref, dtb_ref, ng_ref,
                o_ref, sfin_ref, s_scr, xbuf, *, L):
    c = pl.program_id(1)

    @pl.when(c == 0)
    def _():
        s_scr[...] = s0_ref[0]
        xbuf[0:SUBLANES, :] = prev_ref[0]

    xbuf[SUBLANES:SUBLANES + L, :] = qkv_ref[...]
    base = SUBLANES - (GDN_CONV - 1)
    y = xbuf[base:base + L, :] * convw_ref[0:1, :]
    for i in range(1, GDN_CONV):
        y = y + xbuf[base + i:base + i + L, :] * convw_ref[i:i + 1, :]
    y = y * jax.nn.sigmoid(y)
    tail = xbuf[L:L + SUBLANES, :]
    xbuf[0:SUBLANES, :] = tail

    row = lax.broadcasted_iota(jnp.int32, (L, L), 0)
    col = lax.broadcasted_iota(jnp.int32, (L, L), 1)
    lower = row >= col
    strict = row > col
    eye = (row == col).astype(F32)
    n_double = int(math.log2(L)) - 1
    assert 2 ** (n_double + 1) == L

    ba = ba_ref[...]
    beta_all = jax.nn.sigmoid(ba)
    a_in = ba + dtb_ref[...]
    softplus = jnp.maximum(a_in, 0.0) + jnp.log(1.0 + jnp.exp(-jnp.abs(a_in)))
    g_all = -jnp.exp(alog_ref[...]) * softplus
    g_cum = _dot(lower.astype(F32), g_all, HIGHEST)
    g_cum_t = g_cum.T

    heads = range(GDN_HEADS)
    cat0 = functools.partial(jnp.concatenate, axis=0)

    def head_cols(part, h):
        return y[:, part * GDN_WIDTH + h * HEAD_DIM:part * GDN_WIDTH + (h + 1) * HEAD_DIM]

    q = [head_cols(0, h) for h in heads]
    k = [head_cols(1, h) for h in heads]
    v = [head_cols(2, h) for h in heads]
    q = [a * lax.rsqrt(jnp.sum(a * a, axis=-1, keepdims=True) + 1e-6) * (HEAD_DIM ** -0.5) for a in q]
    k = [a * lax.rsqrt(jnp.sum(a * a, axis=-1, keepdims=True) + 1e-6) for a in k]
    beta = [beta_all[:, h:h + 1] for h in heads]
    g_col = [g_cum[:, GDN_HEADS + h:GDN_HEADS + h + 1] for h in heads]
    g_row = [g_cum_t[GDN_HEADS + h:GDN_HEADS + h + 1, :] for h in heads]
    decay = [jnp.where(lower, jnp.exp(jnp.where(lower, g_col[h] - g_row[h], 0.0)), 0.0) for h in heads]
    g_last = [g_col[h][L - 1:L, :] for h in heads]
    e_col = [jnp.exp(g_col[h]) for h in heads]
    qk_kk = [_dot3_nt(cat0([q[h], k[h]]), k[h]) for h in heads]
    x = [-jnp.where(strict, beta[h] * qk_kk[h][L:] * decay[h], 0.0) for h in heads]
    t_inv = [eye + x[h] for h in heads]
    xp = [_dot3(x[h], x[h]) for h in heads]
    for _ in range(n_double - 1):
        both = [_dot3(cat0([t_inv[h], xp[h]]), xp[h]) for h in heads]
        t_inv = [t_inv[h] + both[h][:L] for h in heads]
        xp = [both[h][L:] for h in heads]
    t_inv = [t_inv[h] + _dot3(t_inv[h], xp[h]) for h in heads]
    sol = [_dot3(t_inv[h], jnp.concatenate([v[h] * beta[h], k[h] * (beta[h] * e_col[h])], axis=-1)) for h in heads]
    qk = [jnp.where(lower, qk_kk[h][:L] * decay[h], 0.0) for h in heads]
    k_dec_t = [(k[h] * jnp.exp(g_last[h] - g_col[h])).T for h in heads]
    s = [s_scr[h] for h in heads]
    ws_qs = [_dot3(cat0([sol[h][:, HEAD_DIM:], q[h] * e_col[h]]), s[h]) for h in heads]
    v_new = [sol[h][:, :HEAD_DIM] - ws_qs[h][:L] for h in heads]
    tail = [_dot3(cat0([qk[h], k_dec_t[h]]), v_new[h]) for h in heads]
    for h in heads:
        s_scr[h] = s[h] * jnp.exp(g_last[h]) + tail[h][L:]
        zf = z_ref[:, h * HEAD_DIM:(h + 1) * HEAD_DIM]
        o = _rms(ws_qs[h][L:] + tail[h][:L], ng_ref[...]) * (zf * jax.nn.sigmoid(zf))
        o_ref[:, h * HEAD_DIM:(h + 1) * HEAD_DIM] = o.astype(o_ref.dtype)

    @pl.when(c == pl.num_programs(1) - 1)
    def _():
        sfin_ref[0] = s_scr[...]


def _decay_lanes(p):
    return jnp.zeros((1, LANES), F32).at[0, GDN_HEADS:2 * GDN_HEADS].set(p.astype(F32))


def _gdn(proj, conv_prev8, s0, conv_w, a_log, dt_bias, norm_g, *, bsz, t):
    L = min(CHUNK, t)
    nc = t // L
    conv_dim = 3 * GDN_WIDTH
    kern = functools.partial(_gdn_kernel, L=L)
    o, s_fin = pl.pallas_call(
        kern,
        out_shape=(jax.ShapeDtypeStruct((bsz * t, GDN_WIDTH), BF16),
                   jax.ShapeDtypeStruct((bsz, GDN_HEADS, HEAD_DIM, HEAD_DIM), F32)),
        grid=(bsz, nc),
        in_specs=[
            pl.BlockSpec((L, conv_dim), lambda b, c: (b * nc + c, COL_QKV // conv_dim)),
            pl.BlockSpec((L, GDN_WIDTH), lambda b, c: (b * nc + c, COL_Z // GDN_WIDTH)),
            pl.BlockSpec((L, LANES), lambda b, c: (b * nc + c, COL_BA // LANES)),
            pl.BlockSpec((1, SUBLANES, conv_dim), lambda b, c: (b, 0, 0)),
            pl.BlockSpec((1, GDN_HEADS, HEAD_DIM, HEAD_DIM), lambda b, c: (b, 0, 0, 0)),
            pl.BlockSpec((GDN_CONV, conv_dim), lambda b, c: (0, 0)),
            pl.BlockSpec((1, LANES), lambda b, c: (0, 0)),
            pl.BlockSpec((1, LANES), lambda b, c: (0, 0)),
            pl.BlockSpec((1, HEAD_DIM), lambda b, c: (0, 0)),
        ],
        out_specs=(pl.BlockSpec((L, GDN_WIDTH), lambda b, c: (b * nc + c, 0)),
                   pl.BlockSpec((1, GDN_HEADS, HEAD_DIM, HEAD_DIM), lambda b, c: (b, 0, 0, 0))),
        scratch_shapes=[pltpu.VMEM((GDN_HEADS, HEAD_DIM, HEAD_DIM), F32),
                        pltpu.VMEM((SUBLANES + L + SUBLANES, conv_dim), F32)],
        compiler_params=_cparams(("parallel", "arbitrary")),
        name="gdn",
    )(proj, proj, proj, conv_prev8, s0, conv_w, _decay_lanes(a_log), _decay_lanes(dt_bias),
      norm_g.reshape(1, -1))
    return o, s_fin


def _lambda_value(lam_refs, lam_init):
    lq1, lk1, lq2, lk2 = (r[...] for r in lam_refs)
    return (jnp.exp(jnp.sum(lq1 * lk1, axis=-1, keepdims=True))
            - jnp.exp(jnp.sum(lq2 * lk2, axis=-1, keepdims=True)) + lam_init)


def _softmax_update(m_scr, l_scr, acc_scr, scores, vb):
    idx = range(len(scores))
    m_old = [m_scr[n] for n in idx]
    m_new = [jnp.maximum(m_old[n], jnp.max(scores[n], axis=-1, keepdims=True)) for n in idx]
    p = [jnp.exp(scores[n] - m_new[n]) for n in idx]
    alpha = [jnp.exp(m_old[n] - m_new[n]) for n in idx]
    pv = [_dot(p[n].astype(BF16), vb) for n in idx]
    for n in idx:
        l_scr[n] = alpha[n] * l_scr[n] + jnp.sum(p[n], axis=-1, keepdims=True)
        acc_scr[n] = alpha[n] * acc_scr[n] + pv[n]
        m_scr[n] = m_new[n]


def _diff_finalize(o1, o2, lam_refs, g_ref, o_ref, lam_init):
    lam = _lambda_value(lam_refs, lam_init)
    o = _rms(o1 - lam * o2, g_ref[...]) * (1.0 - lam_init)
    o_ref[...] = o.astype(o_ref.dtype)


def _diff_init(m_scr, l_scr, acc_scr):
    m_scr[...] = jnp.full(m_scr.shape, -jnp.inf, F32)
    l_scr[...] = jnp.zeros(l_scr.shape, F32)
    acc_scr[...] = jnp.zeros(acc_scr.shape, F32)


def _near_bias(slope, r, c, q_pos, k_pos):
    visible = jnp.right_shift(k_pos, CHUNK_SHIFT) <= jnp.right_shift(q_pos, CHUNK_SHIFT)
    return jnp.where(visible, slope * (r - jnp.abs(q_pos - k_pos)).astype(F32), -jnp.inf)


def _split_distance(slope, d):
    return (-(slope * CHUNK) * jnp.right_shift(d, CHUNK_SHIFT).astype(F32),
            -slope * jnp.bitwise_and(d, CHUNK - 1).astype(F32))


def _diff_prompt_kernel(slopes_ref, q_ref, k_ref, v_ref, lq1, lk1, lq2, lk2, g_ref, o_ref,
                        m_scr, l_scr, acc_scr, lhs_scr, *, tq, lam_init):
    h = pl.program_id(1)
    i = pl.program_id(2)
    j = pl.program_id(3)
    slope = slopes_ref[h]
    half = DIFF_HEAD_DIM
    lane = lax.broadcasted_iota(jnp.int32, (tq, LANES), 1)

    @pl.when(j == 0)
    def _():
        _diff_init(m_scr, l_scr, acc_scr)
        qs = q_ref[...] * (DIFF_HEAD_DIM ** -0.5)
        lhs_scr[0] = jnp.where(lane < half, qs, 1.0).astype(BF16)
        lhs_scr[1] = jnp.where(lane >= half, qs, 1.0).astype(BF16)

    sub = min(DIFF_SUB, tq)
    units = [(n, rb) for rb in range(tq // sub) for n in range(2)]

    def step(t_hi, t_lo, near):
        kb = k_ref[...]
        zero = jnp.zeros_like(kb)
        aug0 = zero if t_hi is None else jnp.where(lane == half, t_hi, jnp.where(lane == half + 1, t_lo, 0.0))
        aug1 = zero if t_hi is None else jnp.where(lane == 0, t_hi, jnp.where(lane == 1, t_lo, 0.0))
        rhs = [jnp.where(lane < half, kb, aug0).astype(BF16), jnp.where(lane >= half, kb, aug1).astype(BF16)]
        vb = v_ref[...].astype(BF16)

        def scores(u):
            n, rb = u
            return _dot_nt(lhs_scr[n, rb * sub:(rb + 1) * sub, :], rhs[n])

        nxt = scores(units[0])
        bias = None
        for idx, (n, rb) in enumerate(units):
            s = nxt
            if idx + 1 < len(units):
                nxt = scores(units[idx + 1])
            rows = slice(rb * sub, (rb + 1) * sub)
            if near:
                if n == 0:
                    r = rb * sub + lax.broadcasted_iota(jnp.int32, (sub, tq), 0)
                    c = lax.broadcasted_iota(jnp.int32, (sub, tq), 1)
                    bias = _near_bias(slope, r, c, r, c)
                s = s + bias
            m_old = m_scr[n, rows, :]
            m_new = jnp.maximum(m_old, jnp.max(s, axis=-1, keepdims=True))
            p = jnp.exp(s - m_new)
            alpha = jnp.exp(m_old - m_new)
            l_scr[n, rows, :] = alpha * l_scr[n, rows, :] + jnp.sum(p, axis=-1, keepdims=True)
            acc_scr[n, rows, :] = alpha * acc_scr[n, rows, :] + _dot(p.astype(BF16), vb)
            m_scr[n, rows, :] = m_new

    @pl.when(j < i)
    def _():
        d = (i - j) * tq - lax.broadcasted_iota(jnp.int32, (tq, LANES), 0)
        t_hi, t_lo = _split_distance(slope, d)
        step(t_hi, t_lo, False)

    @pl.when(j == i)
    def _():
        step(None, None, True)

    @pl.when(j == pl.num_programs(3) - 1)
    def _():
        _diff_finalize(acc_scr[0] / l_scr[0], acc_scr[1] / l_scr[1], (lq1, lk1, lq2, lk2), g_ref, o_ref, lam_init)


def _alibi_slopes():
    return jnp.asarray([2.0 ** (-8.0 * (i + 1) / DIFF_HEADS) for i in range(DIFF_HEADS)], F32)


def _lam_specs(nidx):
    zero = {3: lambda a, b, c: (0, 0), 4: lambda a, b, c, d: (0, 0)}[nidx]
    return [pl.BlockSpec((1, DIFF_HEAD_DIM), zero)] * 4 + [pl.BlockSpec((1, HEAD_DIM), zero)]


def _diff_prompt(proj, lams, subln_g, *, bsz, t, lam_init):
    tq = min(DIFF_TQ, t)
    nq = t // tq
    assert tq % CHUNK == 0 and t <= CHUNK * 255
    kern = functools.partial(_diff_prompt_kernel, tq=tq, lam_init=lam_init)
    qc, kc, vc = COL_DQ // LANES, COL_DK // LANES, COL_DV // LANES
    return pl.pallas_call(
        kern,
        out_shape=jax.ShapeDtypeStruct((bsz * t, DIFF_HEADS * HEAD_DIM), BF16),
        grid=(bsz, DIFF_HEADS, nq, nq),
        in_specs=[
            pl.BlockSpec(memory_space=pltpu.SMEM),
            pl.BlockSpec((tq, LANES), lambda b, h, i, j: (b * nq + i, qc + h)),
            pl.BlockSpec((tq, LANES), lambda b, h, i, j: (b * nq + jnp.minimum(i, j), kc + h)),
            pl.BlockSpec((tq, LANES), lambda b, h, i, j: (b * nq + jnp.minimum(i, j), vc + h)),
        ] + _lam_specs(4),
        out_specs=pl.BlockSpec((tq, LANES), lambda b, h, i, j: (b * nq + i, h)),
        scratch_shapes=[pltpu.VMEM((2, tq, 1), F32), pltpu.VMEM((2, tq, 1), F32),
                        pltpu.VMEM((2, tq, HEAD_DIM), F32), pltpu.VMEM((2, tq, LANES), BF16)],
        compiler_params=_cparams(("parallel", "parallel", "parallel", "arbitrary")),
        name="diff_attn_prompt",
    )(_alibi_slopes(), proj, proj, proj, *lams, subln_g.reshape(1, -1))


def _diff_sample_kernel(slopes_ref, q_ref, kn_ref, vn_ref, kp_ref, vp_ref, lq1, lk1, lq2, lk2, g_ref, o_ref,
                        m_scr, l_scr, acc_scr, lhs_scr, *, t, tk, past, lam_init):
    h = pl.program_id(1)
    j = pl.program_id(2)
    slope = slopes_ref[h]

    @pl.when(j == 0)
    def _():
        _diff_init(m_scr, l_scr, acc_scr)
        lane = lax.broadcasted_iota(jnp.int32, (t, LANES), 1)
        qs = q_ref[...] * (DIFF_HEAD_DIM ** -0.5)
        lhs_scr[0:t, :] = jnp.where(lane < DIFF_HEAD_DIM, qs, 0.0).astype(BF16)
        lhs_scr[t:2 * t, :] = jnp.where(lane >= DIFF_HEAD_DIM, qs, 0.0).astype(BF16)
        r = lax.broadcasted_iota(jnp.int32, (t, t), 0)
        c = lax.broadcasted_iota(jnp.int32, (t, t), 1)
        bias = _near_bias(slope, r, c, past + r, past + c)
        s = _dot_nt(lhs_scr[...], kn_ref[...].astype(BF16)) + jnp.concatenate([bias, bias], axis=0)
        _softmax_update(m_scr, l_scr, acc_scr, [s], vn_ref[...].astype(BF16))

    d = past - j * tk - lax.broadcasted_iota(jnp.int32, (1, tk), 1)
    s = _dot_nt(lhs_scr[...], kp_ref[...].astype(BF16)) - slope * d.astype(F32)
    _softmax_update(m_scr, l_scr, acc_scr, [s], vp_ref[...].astype(BF16))

    @pl.when(j == pl.num_programs(2) - 1)
    def _():
        o = acc_scr[0] / l_scr[0]
        _diff_finalize(o[:t], o[t:], (lq1, lk1, lq2, lk2), g_ref, o_ref, lam_init)


def _diff_sample(proj, k_past, v_past, lams, subln_g, *, bsz, t, lam_init):
    past = k_past.shape[1]
    tk = min(1024, past)
    nk = past // tk
    kp = k_past.reshape(bsz * past, DIFF_HEADS * HEAD_DIM)
    vp = v_past.reshape(bsz * past, DIFF_HEADS * HEAD_DIM)
    kern = functools.partial(_diff_sample_kernel, t=t, tk=tk, past=past, lam_init=lam_init)
    qc, kc, vc = COL_DQ // LANES, COL_DK // LANES, COL_DV // LANES
    return pl.pallas_call(
        kern,
        out_shape=jax.ShapeDtypeStruct((bsz * t, DIFF_HEADS * HEAD_DIM), BF16),
        grid=(bsz, DIFF_HEADS, nk),
        in_specs=[
            pl.BlockSpec(memory_space=pltpu.SMEM),
            pl.BlockSpec((t, LANES), lambda b, h, j: (b, qc + h)),
            pl.BlockSpec((t, LANES), lambda b, h, j: (b, kc + h)),
            pl.BlockSpec((t, LANES), lambda b, h, j: (b, vc + h)),
            pl.BlockSpec((tk, LANES), lambda b, h, j: (b * nk + j, h)),
            pl.BlockSpec((tk, LANES), lambda b, h, j: (b * nk + j, h)),
        ] + _lam_specs(3),
        out_specs=pl.BlockSpec((t, LANES), lambda b, h, j: (b, h)),
        scratch_shapes=[pltpu.VMEM((1, 2 * t, 1), F32), pltpu.VMEM((1, 2 * t, 1), F32),
                        pltpu.VMEM((1, 2 * t, HEAD_DIM), F32), pltpu.VMEM((2 * t, LANES), BF16)],
        compiler_params=_cparams(("parallel", "parallel", "arbitrary")),
        name="diff_attn_sample",
    )(_alibi_slopes(), proj, proj, proj, kp, vp, *lams, subln_g.reshape(1, -1))


def _merge_kernel(oa_ref, ob_ref, wa_ref, wb_ref, ga_ref, gb_ref, o_ref):
    ya = _dot(oa_ref[...], wa_ref[...])
    yb = _dot(ob_ref[...], wb_ref[...])
    o_ref[...] = (jax.nn.sigmoid(ga_ref[...]) * ya + jax.nn.sigmoid(gb_ref[...]) * yb).astype(o_ref.dtype)


def _merge(o_a, o_b, wa, wb, proj, *, tm):
    m = o_a.shape[0]
    d = wa.shape[1]
    tn = 512
    ga0, gb0 = COL_GA // tn, COL_GB // tn
    return pl.pallas_call(
        _merge_kernel,
        out_shape=jax.ShapeDtypeStruct((m, d), BF16),
        grid=(m // tm, d // tn),
        in_specs=[
            pl.BlockSpec((tm, o_a.shape[1]), lambda i, j: (i, 0)),
            pl.BlockSpec((tm, o_b.shape[1]), lambda i, j: (i, 0)),
            pl.BlockSpec((wa.shape[0], tn), lambda i, j: (0, j)),
            pl.BlockSpec((wb.shape[0], tn), lambda i, j: (0, j)),
            pl.BlockSpec((tm, tn), lambda i, j: (i, ga0 + j)),
            pl.BlockSpec((tm, tn), lambda i, j: (i, gb0 + j)),
        ],
        out_specs=pl.BlockSpec((tm, tn), lambda i, j: (i, j)),
        compiler_params=_cparams(("parallel", "arbitrary")),
        name="merge",
    )(o_a, o_b, wa, wb, proj, proj)


def _cross_kernel(q_ref, k_ref, v_ref, o_ref):
    dh = q_ref.shape[-1] // MEM_HEADS
    for h in range(MEM_HEADS):
        sl = slice(h * dh, (h + 1) * dh)
        s = _dot_nt(q_ref[:, sl], k_ref[0, :, sl].astype(BF16)) * (dh ** -0.5)
        s = s - jnp.max(s, axis=-1, keepdims=True)
        p = jnp.exp(s)
        p = p / jnp.sum(p, axis=-1, keepdims=True)
        o_ref[:, sl] = _dot(p.astype(BF16), v_ref[0, :, sl].astype(BF16)).astype(o_ref.dtype)


def _cross_attend(q, mem_k, mem_v, *, bsz, t, tm):
    d = q.shape[1]
    nt = t // tm
    mt = mem_k.shape[1]
    return pl.pallas_call(
        _cross_kernel,
        out_shape=jax.ShapeDtypeStruct(q.shape, BF16),
        grid=(bsz, nt),
        in_specs=[
            pl.BlockSpec((tm, d), lambda b, i: (b * nt + i, 0)),
            pl.BlockSpec((1, mt, d), lambda b, i: (b, 0, 0)),
            pl.BlockSpec((1, mt, d), lambda b, i: (b, 0, 0)),
        ],
        out_specs=pl.BlockSpec((tm, d), lambda b, i: (b * nt + i, 0)),
        compiler_params=_cparams(("parallel", "arbitrary")),
        name="cross_attn",
    )(q, mem_k, mem_v)


def _topk_rows(work, cidx, k):
    n, tn = work.shape
    rid = lax.broadcasted_iota(jnp.int32, (n, tn), 0)
    kid = lax.broadcasted_iota(jnp.int32, (k, tn), 0)
    vals = jnp.zeros((k, tn), F32)
    idxs = jnp.zeros((k, tn), jnp.int32)
    for t in range(k):
        m = jnp.max(work, axis=0, keepdims=True)
        pos = jnp.min(jnp.where(work == m, rid, n), axis=0, keepdims=True)
        hit = rid == pos
        if cidx is None:
            picked = pos
        else:
            picked = jnp.sum(jnp.where(hit, cidx, 0), axis=0, keepdims=True)
        vals = jnp.where(kid == t, m, vals)
        idxs = jnp.where(kid == t, picked, idxs)
        work = jnp.where(hit, -jnp.inf, work)
    return vals, idxs


def _product_candidates(v1, i1, v2, i2, k):
    assert k == 2 * SUBLANES
    sub = lax.broadcasted_iota(jnp.int32, (SUBLANES, v1.shape[1]), 0)
    vals, idxs = [], []
    for a in range(k // 2):
        nb = k // (a + 1)
        width = k if nb > SUBLANES else SUBLANES
        cv = v1[a:a + 1, :] + v2[:width, :]
        ci = i1[a:a + 1, :] * PEER_N_KEYS + i2[:width, :]
        if nb < SUBLANES:
            cv = jnp.where(sub < nb, cv, -jnp.inf)
        vals.append(cv)
        idxs.append(ci)
    vals.append(v1[k // 2:, :] + v2[0:1, :])
    idxs.append(i1[k // 2:, :] * PEER_N_KEYS + i2[0:1, :])
    return jnp.concatenate(vals, axis=0), jnp.concatenate(idxs, axis=0)


def _peer_route_kernel(q_ref, keys_ref, e_ref, g_ref):
    tn = q_ref.shape[0]
    k = PEER_TOPK
    for h in range(PEER_HEADS):
        sub = []
        for p in range(2):
            c0 = (h * 2 + p) * PEER_N_KEYS
            s = _dot_nt(keys_ref[h, p], q_ref[:, c0:c0 + PEER_N_KEYS], HIGHEST)
            sub.append(_topk_rows(s, None, k))
        (v1, i1), (v2, i2) = sub
        cand, cidx = _product_candidates(v1, i1, v2, i2, k)
        best, eidx = _topk_rows(cand, cidx, k)
        ex = jnp.exp(best - jnp.max(best, axis=0, keepdims=True))
        gate = ex / jnp.sum(ex, axis=0, keepdims=True)
        e_ref[h * k:(h + 1) * k, :] = eidx
        g_ref[h * k:(h + 1) * k, :] = gate


HALF_KEYS = PEER_N_KEYS // 2
BF16_HIGH_MASK = 0xFFFF0000


def _peer_gate_kernel(et_ref, gt_ref, o_ref, e_scr, g_scr):
    tg = et_ref.shape[1]
    e_scr[...] = et_ref[...].T
    g_scr[...] = gt_ref[...].T
    kid = lax.broadcasted_iota(jnp.int32, (PEER_N_KEYS, PEER_N_KEYS), 0)

    def body(n8, carry):
        rows = pl.ds(pl.multiple_of(n8 * SUBLANES, SUBLANES), SUBLANES)
        e8 = e_scr[rows, :]
        g8 = g_scr[rows, :]
        a8 = jnp.right_shift(e8, 7)
        b8 = jnp.bitwise_and(e8, PEER_N_KEYS - 1)
        words = []
        for s in range(SUBLANES):
            at = jnp.where(a8[s:s + 1, :] == kid, g8[s:s + 1, :], 0.0).astype(BF16)
            bt = jnp.where(b8[s:s + 1, :] == kid, 1.0, 0.0).astype(BF16)
            grid = _dot_nt(at, bt)
            bits = pltpu.bitcast(grid.astype(BF16).astype(F32), jnp.uint32)
            words.append(jnp.bitwise_or(jnp.bitwise_and(bits[HALF_KEYS:, :], jnp.uint32(BF16_HIGH_MASK)),
                                        jnp.right_shift(bits[:HALF_KEYS, :], jnp.uint32(16))))
        o_ref[:, rows, :] = jnp.swapaxes(jnp.stack(words, axis=0), 0, 1)
        return carry

    lax.fori_loop(0, tg // SUBLANES, body, 0)


def _peer_route(qp, sub_keys, *, tn):
    n = qp.shape[0]
    rows = PEER_HEADS * PEER_TOPK
    e_t, g_t = pl.pallas_call(
        _peer_route_kernel,
        out_shape=(jax.ShapeDtypeStruct((rows, n), jnp.int32), jax.ShapeDtypeStruct((rows, n), F32)),
        grid=(n // tn,),
        in_specs=[
            pl.BlockSpec((tn, qp.shape[1]), lambda i: (i, 0)),
            pl.BlockSpec(sub_keys.shape, lambda i: (0, 0, 0, 0)),
        ],
        out_specs=(pl.BlockSpec((rows, tn), lambda i: (0, i)), pl.BlockSpec((rows, tn), lambda i: (0, i))),
        compiler_params=_cparams(("parallel",)),
        name="peer_route",
    )(qp, sub_keys)
    return pl.pallas_call(
        _peer_gate_kernel,
        out_shape=jax.ShapeDtypeStruct((HALF_KEYS, n, PEER_N_KEYS), jnp.uint32),
        grid=(n // tn,),
        in_specs=[pl.BlockSpec((rows, tn), lambda i: (0, i)), pl.BlockSpec((rows, tn), lambda i: (0, i))],
        out_specs=pl.BlockSpec((HALF_KEYS, tn, PEER_N_KEYS), lambda i: (0, i, 0)),
        scratch_shapes=[pltpu.VMEM((tn, rows), jnp.int32), pltpu.VMEM((tn, rows), F32)],
        compiler_params=_cparams(("parallel",)),
        name="peer_gates",
    )(e_t, g_t)


def _gelu_tanh(x):
    return 0.5 * x * (1.0 + jnp.tanh(math.sqrt(2.0 / math.pi) * (x + 0.044715 * (x * x * x))))


def _peer_expert_kernel(x_ref, gn_ref, gp_ref, u1_ref, u2_ref, v1_ref, v2_ref, gf_ref, o_ref,
                        h_scr, acc_scr, coef_scr):
    r = pl.program_id(1)
    last = pl.num_programs(1) - 1

    @pl.when(r == 0)
    def _():
        h_scr[...] = _rms(x_ref[...], gn_ref[...]).astype(BF16)
        acc_scr[...] = jnp.zeros(acc_scr.shape, F32)
        coef_scr[...] = jnp.zeros(coef_scr.shape, BF16)

    v_cat = jnp.concatenate([v1_ref[...], v2_ref[...]], axis=0)
    acc_scr[...] += _dot(coef_scr[(r + 1) % 2], v_cat)

    word = gp_ref[...]
    g_lo = pltpu.bitcast(jnp.left_shift(word, jnp.uint32(16)), F32)
    g_hi = pltpu.bitcast(jnp.bitwise_and(word, jnp.uint32(BF16_HIGH_MASK)), F32)
    u_cat = jnp.concatenate([u1_ref[...], u2_ref[...]], axis=0)
    hid = _gelu_tanh(_dot_nt(h_scr[...], u_cat))
    coef_scr[r % 2] = (jnp.concatenate([g_lo, g_hi], axis=1) * hid).astype(BF16)

    @pl.when(r == last)
    def _():
        o_ref[...] = _rms(x_ref[...] + acc_scr[...], gf_ref[...])


def _peer_experts(x, norm_g, gates, u_tab, v_tab, final_g, *, tn):
    n, d = x.shape
    eb = PEER_N_KEYS
    assert u_tab.shape[0] == eb * eb
    nb = HALF_KEYS

    def cur(r):
        return jnp.minimum(r, nb - 1)

    def prev(r):
        return jnp.maximum(r - 1, 0)

    return pl.pallas_call(
        _peer_expert_kernel,
        out_shape=jax.ShapeDtypeStruct((n, d), F32),
        grid=(n // tn, nb + 1),
        in_specs=[
            pl.BlockSpec((tn, d), lambda i, r: (i, 0)),
            pl.BlockSpec((1, d), lambda i, r: (0, 0)),
            pl.BlockSpec((None, tn, eb), lambda i, r: (cur(r), i, 0)),
            pl.BlockSpec((eb, d), lambda i, r: (cur(r), 0)),
            pl.BlockSpec((eb, d), lambda i, r: (cur(r) + nb, 0)),
            pl.BlockSpec((eb, d), lambda i, r: (prev(r), 0)),
            pl.BlockSpec((eb, d), lambda i, r: (prev(r) + nb, 0)),
            pl.BlockSpec((1, d), lambda i, r: (0, 0)),
        ],
        out_specs=pl.BlockSpec((tn, d), lambda i, r: (i, 0)),
        scratch_shapes=[pltpu.VMEM((tn, d), BF16), pltpu.VMEM((tn, d), F32), pltpu.VMEM((2, tn, 2 * eb), BF16)],
        compiler_params=_cparams(("parallel", "arbitrary")),
        name="peer_experts",
    )(x, norm_g.reshape(1, d), gates, u_tab, u_tab, v_tab, v_tab, final_g.reshape(1, d))


def _reorder_w_in(w_in):
    c_qkv = 3 * GDN_WIDTH
    c_z = c_qkv + GDN_WIDTH
    c_ba = c_z + 2 * GDN_HEADS
    rest = w_in[:, c_ba:]
    pad = jnp.zeros((w_in.shape[0], IN_COLS_PADDED - COL_BA - 2 * GDN_HEADS), w_in.dtype)
    return jnp.concatenate([w_in[:, :c_z], rest, w_in[:, c_z:c_ba], pad], axis=1).astype(BF16)


def _layer(x, wts, layer, mem_k, mem_v, conv_prev, gdn_state, k_past, v_past, *, tm):
    bsz, t, d = x.shape
    n = bsz * t
    x2d = x.reshape(n, d)
    proj = _matmul(x2d, wts["w_in"], norm_g=wts["norm_mix_g"], tm=tm, tn=IN_TN, name="in_proj")
    conv_dim = 3 * GDN_WIDTH
    conv_prev8 = jnp.concatenate(
        [jnp.zeros((bsz, SUBLANES - (GDN_CONV - 1), conv_dim), F32), conv_prev.astype(F32)], axis=1)
    o_a, s_new = _gdn(proj, conv_prev8, gdn_state.astype(F32), wts["gdn_conv_w"], wts["gdn_a_log"],
                      wts["gdn_dt_bias"], wts["gdn_norm_g"], bsz=bsz, t=t)
    conv_new = proj.reshape(bsz, t, -1)[:, t - (GDN_CONV - 1):, COL_QKV:COL_QKV + conv_dim]
    lam_init = 0.8 - 0.6 * math.exp(-0.3 * layer)
    lams = tuple(wts[k].reshape(1, -1) for k in ("diff_lambda_q1", "diff_lambda_k1", "diff_lambda_q2", "diff_lambda_k2"))
    if k_past is None:
        o_b = _diff_prompt(proj, lams, wts["diff_subln_g"], bsz=bsz, t=t, lam_init=lam_init)
    else:
        o_b = _diff_sample(proj, k_past, v_past, lams, wts["diff_subln_g"], bsz=bsz, t=t, lam_init=lam_init)
    k_rows = proj[:, COL_DK:COL_DK + DIFF_HEADS * HEAD_DIM].reshape(bsz, t, DIFF_HEADS, HEAD_DIM)
    v_rows = proj[:, COL_DV:COL_DV + DIFF_HEADS * HEAD_DIM].reshape(bsz, t, DIFF_HEADS, HEAD_DIM)
    merged = _merge(o_a, o_b, wts["w_branch_a"], wts["w_branch_b"], proj, tm=tm)
    x1 = _matmul(merged, wts["w_out"], residual=x2d, tm=tm, tn=512, name="out_proj")
    qm = _matmul(x1, wts["w_mq"], norm_g=wts["norm_cross_g"], tm=tm, tn=512, out_dtype=BF16, name="mem_q")
    mt = mem_k.shape[1]
    oc = _cross_attend(qm, mem_k.reshape(bsz, mt, d), mem_v.reshape(bsz, mt, d), bsz=bsz, t=t, tm=min(tm, t))
    x2 = _matmul(oc, wts["w_mo"], residual=x1, tm=tm, tn=512, name="mem_o")
    qp = _matmul(x2, wts["peer_w_q"], norm_g=wts["norm_ffn_g"], tm=tm, tn=512, name="peer_q")
    tn = min(512, n)
    gates = _peer_route(qp, wts["peer_sub_keys"], tn=LANES)
    y = _peer_experts(x2, wts["norm_ffn_g"], gates, wts["peer_u"], wts["peer_v"], wts["final_norm_g"], tn=tn)
    return y.reshape(bsz, t, d), k_rows, v_rows, s_new, conv_new


def kernel(x_prompt, x_sample, cache_diff_k, cache_diff_v, state_gdn, state_conv, cache_mem_k, cache_mem_v,
           mem_prompt, norm_mix_g, w_in, gdn_conv_w, gdn_a_log, gdn_dt_bias, gdn_norm_g,
           diff_lambda_q1, diff_lambda_k1, diff_lambda_q2, diff_lambda_k2, diff_subln_g,
           w_branch_a, w_branch_b, w_out, norm_cross_g, norm_mem_g, w_mq, w_mk, w_mv, w_mo,
           norm_ffn_g, peer_w_q, peer_sub_keys, peer_u, peer_v, final_norm_g):
    depth = w_in.shape[0]
    assert depth == 1, "final norm is fused into the last layer's PEER kernel"
    l = 0
    wts = {
        "norm_mix_g": norm_mix_g[l], "w_in": _reorder_w_in(w_in[l]), "gdn_conv_w": gdn_conv_w[l],
        "gdn_a_log": gdn_a_log[l], "gdn_dt_bias": gdn_dt_bias[l], "gdn_norm_g": gdn_norm_g[l],
        "diff_lambda_q1": diff_lambda_q1[l], "diff_lambda_k1": diff_lambda_k1[l],
        "diff_lambda_q2": diff_lambda_q2[l], "diff_lambda_k2": diff_lambda_k2[l],
        "diff_subln_g": diff_subln_g[l], "w_branch_a": w_branch_a[l].astype(BF16),
        "w_branch_b": w_branch_b[l].astype(BF16), "w_out": w_out[l].astype(BF16),
        "norm_cross_g": norm_cross_g[l], "w_mq": w_mq[l].astype(BF16), "w_mo": w_mo[l].astype(BF16),
        "norm_ffn_g": norm_ffn_g[l], "peer_w_q": peer_w_q[l].astype(BF16), "peer_sub_keys": peer_sub_keys[l],
        "peer_u": peer_u[l].astype(BF16), "peer_v": peer_v[l].astype(BF16), "final_norm_g": final_norm_g,
    }
    bp, tp, d = x_prompt.shape
    bs, ts, _ = x_sample.shape
    mem2d = mem_prompt.reshape(-1, d)
    tmm = min(512, mem2d.shape[0])
    mem_k = _matmul(mem2d, w_mk[l].astype(BF16), norm_g=norm_mem_g[l], tm=tmm, tn=512, name="mem_k")
    mem_v = _matmul(mem2d, w_mv[l].astype(BF16), norm_g=norm_mem_g[l], tm=tmm, tn=512, name="mem_v")
    mshape = mem_prompt.shape[:2] + (MEM_HEADS, d // MEM_HEADS)
    mem_k = mem_k.reshape(mshape)
    mem_v = mem_v.reshape(mshape)
    conv0 = jnp.zeros((bp, GDN_CONV - 1, 3 * GDN_WIDTH), F32)
    s0 = jnp.zeros((bp, GDN_HEADS, HEAD_DIM, HEAD_DIM), F32)
    yp, pk, pv, ps, pc = _layer(x_prompt, wts, l, mem_k, mem_v, conv0, s0, None, None, tm=min(512, bp * tp))
    ys, sk, sv, ss, sc = _layer(x_sample, wts, l, cache_mem_k[l], cache_mem_v[l], state_conv[l], state_gdn[l],
                                cache_diff_k[l], cache_diff_v[l], tm=min(256, bs * ts))
    return (yp, ys, pk[None], pv[None], ps[None], pc[None], mem_k[None], mem_v[None],
            sk[None], sv[None], ss[None], sc[None])
```

```python
import functools
import math

import jax
import jax.numpy as jnp
from jax import lax
from jax.experimental import pallas as pl
from jax.experimental.pallas import tpu as pltpu

F32 = jnp.float32
BF16 = jnp.bfloat16
HIGHEST = lax.Precision.HIGHEST

RMS_EPS = 1e-6
CHUNK = 64
CHUNK_SHIFT = 6
DIFF_TQ = 512
DIFF_SUB = 256
GDN_HEADS = 8
HEAD_DIM = 128
GDN_WIDTH = GDN_HEADS * HEAD_DIM
GDN_CONV = 4
DIFF_HEADS = 8
DIFF_HEAD_DIM = 64
MEM_HEADS = 4
PEER_HEADS = 8
PEER_N_KEYS = 128
PEER_TOPK = 16
LANES = 128
SUBLANES = 8
V7X_VMEM_LIMIT = 52 * 1024 * 1024

COL_QKV = 0
COL_Z = 3072
COL_DQ = 4096
COL_DK = 5120
COL_DV = 6144
COL_GA = 7168
COL_GB = 9216
COL_BA = 11264
IN_COLS_PADDED = 11520
IN_TN = 768
IN_TM = 1024


def _cparams(sem, vmem=V7X_VMEM_LIMIT):
    return pltpu.CompilerParams(dimension_semantics=sem, vmem_limit_bytes=vmem)


def _dot(a, b, precision=None):
    return jnp.dot(a, b, preferred_element_type=F32, precision=precision)


def _dot_nt(a, b, precision=None):
    return lax.dot_general(a, b, (((1,), (1,)), ((), ())), preferred_element_type=F32, precision=precision)


def _split_bf16(x):
    hi = x.astype(BF16)
    return hi, (x - hi.astype(F32)).astype(BF16)


def _dot3_impl(dot, a, b):
    a_hi, a_lo = _split_bf16(a)
    b_hi, b_lo = _split_bf16(b)
    m = a.shape[0]
    both = dot(jnp.concatenate([a_hi, a_lo], axis=0), b_hi)
    return both[:m] + both[m:] + dot(a_hi, b_lo)


def _dot3(a, b):
    return _dot3_impl(_dot, a, b)


def _dot3_nt(a, b):
    return _dot3_impl(_dot_nt, a, b)


def _rms(xf, g):
    return xf * lax.rsqrt(jnp.mean(xf * xf, axis=-1, keepdims=True) + RMS_EPS) * g


def _mm_kernel(*refs, has_norm, has_res):
    it = iter(refs)
    x_ref = next(it)
    g_ref = next(it) if has_norm else None
    w_ref = next(it)
    r_ref = next(it) if has_res else None
    o_ref = next(it)
    h_scr = next(it)

    @pl.when(pl.program_id(1) == 0)
    def _():
        xf = x_ref[...].astype(F32)
        if has_norm:
            xf = _rms(xf, g_ref[...])
        h_scr[...] = xf.astype(BF16)

    acc = _dot(h_scr[...], w_ref[...])
    if has_res:
        acc = acc + r_ref[...]
    o_ref[...] = acc.astype(o_ref.dtype)


def _matmul(x, w, *, norm_g=None, residual=None, tm, tn, out_dtype=F32, name="matmul"):
    m, k = x.shape
    n = w.shape[1]
    assert m % tm == 0 and n % tn == 0, (m, n, tm, tn)
    in_specs = [pl.BlockSpec((tm, k), lambda i, j: (i, 0))]
    args = [x]
    if norm_g is not None:
        in_specs.append(pl.BlockSpec((1, k), lambda i, j: (0, 0)))
        args.append(norm_g.reshape(1, k).astype(F32))
    in_specs.append(pl.BlockSpec((k, tn), lambda i, j: (0, j)))
    args.append(w)
    if residual is not None:
        in_specs.append(pl.BlockSpec((tm, tn), lambda i, j: (i, j)))
        args.append(residual)
    return pl.pallas_call(
        functools.partial(_mm_kernel, has_norm=norm_g is not None, has_res=residual is not None),
        out_shape=jax.ShapeDtypeStruct((m, n), out_dtype),
        grid=(m // tm, n // tn),
        in_specs=in_specs,
        out_specs=pl.BlockSpec((tm, tn), lambda i, j: (i, j)),
        scratch_shapes=[pltpu.VMEM((tm, k), BF16)],
        compiler_params=_cparams(("parallel", "arbitrary")),
        name=name,
    )(*args)


def _gdn_kernel(qkv_ref, z_ref, ba_ref, prev_ref, s0_ref, convw_ref, alog_ref, dtb_ref, ng_ref,
                o_ref, sfin_ref, s_scr, xbuf, *, L):
    c = pl.program_id(1)

    @pl.when(c == 0)
    def _():
        s_scr[...] = s0_ref[0]
        xbuf[0:SUBLANES, :] = prev_ref[0]

    xbuf[SUBLANES:SUBLANES + L, :] = qkv_ref[...]
    base = SUBLANES - (GDN_CONV - 1)
    y = xbuf[base:base + L, :] * convw_ref[0:1, :]
    for i in range(1, GDN_CONV):
        y = y + xbuf[base + i:base + i + L, :] * convw_ref[i:i + 1, :]
    y = y * jax.nn.sigmoid(y)
    tail = xbuf[L:L + SUBLANES, :]
    xbuf[0:SUBLANES, :] = tail

    row = lax.broadcasted_iota(jnp.int32, (L, L), 0)
    col = lax.broadcasted_iota(jnp.int32, (L, L), 1)
    lower = row >= col
    strict = row > col
    eye = (row == col).astype(F32)
    n_double = int(math.log2(L)) - 1
    assert 2 ** (n_double + 1) == L

    ba = ba_ref[...]
    beta_all = jax.nn.sigmoid(ba)
    a_in = ba + dtb_ref[...]
    softplus = jnp.maximum(a_in, 0.0) + jnp.log(1.0 + jnp.exp(-jnp.abs(a_in)))
    g_all = -jnp.exp(alog_ref[...]) * softplus
    g_cum = _dot(lower.astype(F32), g_all, HIGHEST)
    g_cum_t = g_cum.T

    heads = range(GDN_HEADS)
    cat0 = functools.partial(jnp.concatenate, axis=0)

    def head_cols(part, h):
        return y[:, part * GDN_WIDTH + h * HEAD_DIM:part * GDN_WIDTH + (h + 1) * HEAD_DIM]

    q = [head_cols(0, h) for h in heads]
    k = [head_cols(1, h) for h in heads]
    v = [head_cols(2, h) for h in heads]
    q = [a * lax.rsqrt(jnp.sum(a * a, axis=-1, keepdims=True) + 1e-6) * (HEAD_DIM ** -0.5) for a in q]
    k = [a * lax.rsqrt(jnp.sum(a * a, axis=-1, keepdims=True) + 1e-6) for a in k]
    beta = [beta_all[:, h:h + 1] for h in heads]
    g_col = [g_cum[:, GDN_HEADS + h:GDN_HEADS + h + 1] for h in heads]
    g_row = [g_cum_t[GDN_HEADS + h:GDN_HEADS + h + 1, :] for h in heads]
    decay = [jnp.where(lower, jnp.exp(jnp.where(lower, g_col[h] - g_row[h], 0.0)), 0.0) for h in heads]
    g_last = [g_col[h][L - 1:L, :] for h in heads]
    e_col = [jnp.exp(g_col[h]) for h in heads]
    qk_kk = [_dot3_nt(cat0([q[h], k[h]]), k[h]) for h in heads]
    x = [-jnp.where(strict, beta[h] * qk_kk[h][L:] * decay[h], 0.0) for h in heads]
    t_inv = [eye + x[h] for h in heads]
    xp = [_dot3(x[h], x[h]) for h in heads]
    for _ in range(n_double - 1):
        both = [_dot3(cat0([t_inv[h], xp[h]]), xp[h]) for h in heads]
        t_inv = [t_inv[h] + both[h][:L] for h in heads]
        xp = [both[h][L:] for h in heads]
    t_inv = [t_inv[h] + _dot3(t_inv[h], xp[h]) for h in heads]
    sol = [_dot3(t_inv[h], jnp.concatenate([v[h] * beta[h], k[h] * (beta[h] * e_col[h])], axis=-1)) for h in heads]
    qk = [jnp.where(lower, qk_kk[h][:L] * decay[h], 0.0) for h in heads]
    k_dec_t = [(k[h] * jnp.exp(g_last[h] - g_col[h])).T for h in heads]
    s = [s_scr[h] for h in heads]
    ws_qs = [_dot3(cat0([sol[h][:, HEAD_DIM:], q[h] * e_col[h]]), s[h]) for h in heads]
    v_new = [sol[h][:, :HEAD_DIM] - ws_qs[h][:L] for h in heads]
    tail = [_dot3(cat0([qk[h], k_dec_t[h]]), v_new[h]) for h in heads]
    for h in heads:
        s_scr[h] = s[h] * jnp.exp(g_last[h]) + tail[h][L:]
        zf = z_ref[:, h * HEAD_DIM:(h + 1) * HEAD_DIM]
        o = _rms(ws_qs[h][L:] + tail[h][:L], ng_ref[...]) * (zf * jax.nn.sigmoid(zf))
        o_ref[:, h * HEAD_DIM:(h + 1) * HEAD_DIM] = o.astype(o_ref.dtype)

    @pl.when(c == pl.num_programs(1) - 1)
    def _():
        sfin_ref[0] = s_scr[...]


def _decay_lanes(p):
    return jnp.zeros((1, LANES), F32).at[0, GDN_HEADS:2 * GDN_HEADS].set(p.astype(F32))


def _gdn(proj, conv_prev8, s0, conv_w, a_log, dt_bias, norm_g, *, bsz, t):
    L = min(CHUNK, t)
    nc = t // L
    conv_dim = 3 * GDN_WIDTH
    kern = functools.partial(_gdn_kernel, L=L)
    o, s_fin = pl.pallas_call(
        kern,
        out_shape=(jax.ShapeDtypeStruct((bsz * t, GDN_WIDTH), BF16),
                   jax.ShapeDtypeStruct((bsz, GDN_HEADS, HEAD_DIM, HEAD_DIM), F32)),
        grid=(bsz, nc),
        in_specs=[
            pl.BlockSpec((L, conv_dim), lambda b, c: (b * nc + c, COL_QKV // conv_dim)),
            pl.BlockSpec((L, GDN_WIDTH), lambda b, c: (b * nc + c, COL_Z // GDN_WIDTH)),
            pl.BlockSpec((L, LANES), lambda b, c: (b * nc + c, COL_BA // LANES)),
            pl.BlockSpec((1, SUBLANES, conv_dim), lambda b, c: (b, 0, 0)),
            pl.BlockSpec((1, GDN_HEADS, HEAD_DIM, HEAD_DIM), lambda b, c: (b, 0, 0, 0)),
            pl.BlockSpec((GDN_CONV, conv_dim), lambda b, c: (0, 0)),
            pl.BlockSpec((1, LANES), lambda b, c: (0, 0)),
            pl.BlockSpec((1, LANES), lambda b, c: (0, 0)),
            pl.BlockSpec((1, HEAD_DIM), lambda b, c: (0, 0)),
        ],
        out_specs=(pl.BlockSpec((L, GDN_WIDTH), lambda b, c: (b * nc + c, 0)),
                   pl.BlockSpec((1, GDN_HEADS, HEAD_DIM, HEAD_DIM), lambda b, c: (b, 0, 0, 0))),
        scratch_shapes=[pltpu.VMEM((GDN_HEADS, HEAD_DIM, HEAD_DIM), F32),
                        pltpu.VMEM((SUBLANES + L + SUBLANES, conv_dim), F32)],
        compiler_params=_cparams(("parallel", "arbitrary")),
        name="gdn",
    )(proj, proj, proj, conv_prev8, s0, conv_w, _decay_lanes(a_log), _decay_lanes(dt_bias),
      norm_g.reshape(1, -1))
    return o, s_fin


def _lambda_value(lam_refs, lam_init):
    lq1, lk1, lq2, lk2 = (r[...] for r in lam_refs)
    return (jnp.exp(jnp.sum(lq1 * lk1, axis=-1, keepdims=True))
            - jnp.exp(jnp.sum(lq2 * lk2, axis=-1, keepdims=True)) + lam_init)


def _softmax_update(m_scr, l_scr, acc_scr, scores, values):
    idx = range(len(scores))
    m_old = [m_scr[n] for n in idx]
    m_new = [jnp.maximum(m_old[n], jnp.max(scores[n], axis=-1, keepdims=True)) for n in idx]
    p = [jnp.exp(scores[n] - m_new[n]) for n in idx]
    alpha = [jnp.exp(m_old[n] - m_new[n]) for n in idx]
    pv = [_dot(p[n].astype(BF16), values[n]) for n in idx]
    for n in idx:
        l_scr[n] = alpha[n] * l_scr[n] + jnp.sum(p[n], axis=-1, keepdims=True)
        acc_scr[n] = alpha[n] * acc_scr[n] + pv[n]
        m_scr[n] = m_new[n]


def _diff_finalize(o1, o2, lam_refs, g_ref, lam_init):
    lam = _lambda_value(lam_refs, lam_init)
    return _rms(o1 - lam * o2, g_ref[...]) * (1.0 - lam_init)


def _diff_init(m_scr, l_scr, acc_scr):
    m_scr[...] = jnp.full(m_scr.shape, -jnp.inf, F32)
    l_scr[...] = jnp.zeros(l_scr.shape, F32)
    acc_scr[...] = jnp.zeros(acc_scr.shape, F32)


def _near_bias(slope, r, q_pos, k_pos):
    visible = jnp.right_shift(k_pos, CHUNK_SHIFT) <= jnp.right_shift(q_pos, CHUNK_SHIFT)
    return jnp.where(visible, slope * (r - jnp.abs(q_pos - k_pos)).astype(F32), -jnp.inf)


def _split_distance(slope, d):
    return (-(slope * CHUNK) * jnp.right_shift(d, CHUNK_SHIFT).astype(F32),
            -slope * jnp.bitwise_and(d, CHUNK - 1).astype(F32))


def _diff_prompt_kernel(i_tab, j_tab, slopes_ref, q_ref, k_ref, v_ref, lq1, lk1, lq2, lk2, g_ref, o_ref,
                        m_scr, l_scr, acc_scr, lhs_scr, s_even, s_odd, *, tq, lam_init):
    h = pl.program_id(1)
    i = i_tab[pl.program_id(2)]
    j = j_tab[pl.program_id(2)]
    slope = slopes_ref[h]
    half = DIFF_HEAD_DIM
    lane = lax.broadcasted_iota(jnp.int32, (tq, LANES), 1)

    @pl.when(j == 0)
    def _():
        _diff_init(m_scr, l_scr, acc_scr)
        qs = q_ref[...] * (DIFF_HEAD_DIM ** -0.5)
        lhs_scr[0] = jnp.where(lane < half, qs, 1.0).astype(BF16)
        lhs_scr[1] = jnp.where(lane >= half, qs, 1.0).astype(BF16)

    sub = min(DIFF_SUB, tq)
    units = [(n, qb) for qb in range(tq // sub) for n in range(2)]

    def score_tile(near, dst):
        kb = k_ref[...]
        if near:
            aug0 = aug1 = jnp.zeros_like(kb)
        else:
            d = (i - j) * tq - lax.broadcasted_iota(jnp.int32, (tq, LANES), 0)
            t_hi, t_lo = _split_distance(slope, d)
            aug0 = jnp.where(lane == half, t_hi, jnp.where(lane == half + 1, t_lo, 0.0))
            aug1 = jnp.where(lane == 0, t_hi, jnp.where(lane == 1, t_lo, 0.0))
        keys = [jnp.where(lane < half, kb, aug0).astype(BF16), jnp.where(lane >= half, kb, aug1).astype(BF16)]
        bias = None
        for n, qb in units:
            cols = slice(qb * sub, (qb + 1) * sub)
            s = _dot_nt(keys[n], lhs_scr[n, cols, :])
            if near:
                if n == 0:
                    k_pos = lax.broadcasted_iota(jnp.int32, (tq, sub), 0)
                    q_pos = qb * sub + lax.broadcasted_iota(jnp.int32, (tq, sub), 1)
                    bias = _near_bias(slope, q_pos, q_pos, k_pos)
                s = s + bias
            dst[n, :, cols] = s

    def consume_tile(src):
        vt = v_ref[...].T.astype(BF16)
        idx = range(len(units))
        cols = [slice(qb * sub, (qb + 1) * sub) for _, qb in units]
        s = [src[n, :, cols[u]] for u, (n, _) in enumerate(units)]
        m_old = [m_scr[n, :, cols[u]] for u, (n, _) in enumerate(units)]
        m_new = [jnp.maximum(m_old[u], jnp.max(s[u], axis=0, keepdims=True)) for u in idx]
        p = [jnp.exp(s[u] - m_new[u]) for u in idx]
        alpha = [jnp.exp(m_old[u] - m_new[u]) for u in idx]
        pv = [_dot(vt, p[u].astype(BF16)) for u in idx]
        for u, (n, _) in enumerate(units):
            l_scr[n, :, cols[u]] = alpha[u] * l_scr[n, :, cols[u]] + jnp.sum(p[u], axis=0, keepdims=True)
            acc_scr[n, :, cols[u]] = alpha[u] * acc_scr[n, :, cols[u]] + pv[u]
            m_scr[n, :, cols[u]] = m_new[u]

    def stage(cond, near, score, consume, finalize=False):
        for parity, (dst, src) in enumerate(((s_even, s_odd), (s_odd, s_even))):
            @pl.when(jnp.logical_and(cond, j % 2 == parity))
            def _(dst=dst, src=src):
                if score:
                    score_tile(near, dst)
                if consume:
                    consume_tile(src)
                if finalize:
                    o_ref[...] = _diff_finalize((acc_scr[0] / l_scr[0]).T, (acc_scr[1] / l_scr[1]).T,
                                                (lq1, lk1, lq2, lk2), g_ref, lam_init).astype(o_ref.dtype)

    stage(jnp.logical_and(j == 0, i > 0), False, True, False)
    stage(jnp.logical_and(j == 0, i == 0), True, True, False)
    stage(jnp.logical_and(j >= 1, j < i), False, True, True)
    stage(jnp.logical_and(j >= 1, j == i), True, True, True)
    stage(j == i + 1, False, False, True, finalize=True)


def _alibi_slopes():
    return jnp.asarray([2.0 ** (-8.0 * (i + 1) / DIFF_HEADS) for i in range(DIFF_HEADS)], F32)


def _diff_prompt(proj, lams, subln_g, *, bsz, t, lam_init):
    tq = min(DIFF_TQ, t)
    nq = t // tq
    assert tq % CHUNK == 0 and t <= CHUNK * 255
    kern = functools.partial(_diff_prompt_kernel, tq=tq, lam_init=lam_init)
    qc, kc, vc = COL_DQ // LANES, COL_DK // LANES, COL_DV // LANES
    pairs = [(i, j) for i in range(nq) for j in range(i + 2)]
    i_tab = jnp.asarray([p[0] for p in pairs], jnp.int32)
    j_tab = jnp.asarray([p[1] for p in pairs], jnp.int32)
    zero = lambda b, h, s, it, jt: (0, 0)
    grid_spec = pltpu.PrefetchScalarGridSpec(
        num_scalar_prefetch=2,
        grid=(bsz, DIFF_HEADS, len(pairs)),
        in_specs=[
            pl.BlockSpec(memory_space=pltpu.SMEM),
            pl.BlockSpec((tq, LANES), lambda b, h, s, it, jt: (b * nq + it[s], qc + h)),
            pl.BlockSpec((tq, LANES), lambda b, h, s, it, jt: (b * nq + jnp.minimum(it[s], jt[s]), kc + h)),
            pl.BlockSpec((tq, LANES), lambda b, h, s, it, jt: (b * nq + jnp.maximum(jt[s] - 1, 0), vc + h)),
        ] + [pl.BlockSpec((1, DIFF_HEAD_DIM), zero)] * 4 + [pl.BlockSpec((1, HEAD_DIM), zero)],
        out_specs=pl.BlockSpec((tq, LANES), lambda b, h, s, it, jt: (b * nq + it[s], h)),
        scratch_shapes=[pltpu.VMEM((2, 1, tq), F32), pltpu.VMEM((2, 1, tq), F32),
                        pltpu.VMEM((2, HEAD_DIM, tq), F32), pltpu.VMEM((2, tq, LANES), BF16),
                        pltpu.VMEM((2, tq, tq), F32), pltpu.VMEM((2, tq, tq), F32)],
    )
    return pl.pallas_call(
        kern,
        out_shape=jax.ShapeDtypeStruct((bsz * t, DIFF_HEADS * HEAD_DIM), BF16),
        grid_spec=grid_spec,
        compiler_params=_cparams(("parallel", "parallel", "arbitrary")),
        name="diff_attn_prompt",
    )(i_tab, j_tab, _alibi_slopes(), proj, proj, proj, *lams, subln_g.reshape(1, -1))


def _diff_sample_kernel(slopes_ref, q_ref, kn_ref, vn_ref, kp_ref, vp_ref, lq1, lk1, lq2, lk2, g_ref, o_ref,
                        m_scr, l_scr, acc_scr, lhs_scr, *, t, tk, past, lam_init):
    j = pl.program_id(1)
    heads = range(DIFF_HEADS)

    def cols(h):
        return slice(h * HEAD_DIM, (h + 1) * HEAD_DIM)

    @pl.when(j == 0)
    def _():
        _diff_init(m_scr, l_scr, acc_scr)
        lane = lax.broadcasted_iota(jnp.int32, (t, LANES), 1)
        r = lax.broadcasted_iota(jnp.int32, (t, t), 0)
        c = lax.broadcasted_iota(jnp.int32, (t, t), 1)
        scores = []
        for h in heads:
            qs = q_ref[:, cols(h)] * (DIFF_HEAD_DIM ** -0.5)
            lhs_scr[h, 0:t, :] = jnp.where(lane < DIFF_HEAD_DIM, qs, 0.0).astype(BF16)
            lhs_scr[h, t:2 * t, :] = jnp.where(lane >= DIFF_HEAD_DIM, qs, 0.0).astype(BF16)
            bias = _near_bias(slopes_ref[h], r, past + r, past + c)
            scores.append(_dot_nt(lhs_scr[h], kn_ref[:, cols(h)].astype(BF16)) + jnp.concatenate([bias, bias], axis=0))
        _softmax_update(m_scr, l_scr, acc_scr, scores, [vn_ref[:, cols(h)].astype(BF16) for h in heads])

    d = (past - j * tk - lax.broadcasted_iota(jnp.int32, (1, tk), 1)).astype(F32)
    scores = [_dot_nt(lhs_scr[h], kp_ref[:, h, :].astype(BF16)) - slopes_ref[h] * d for h in heads]
    _softmax_update(m_scr, l_scr, acc_scr, scores, [vp_ref[:, h, :].astype(BF16) for h in heads])

    @pl.when(j == pl.num_programs(1) - 1)
    def _():
        for h in heads:
            o = acc_scr[h] / l_scr[h]
            o_ref[:, cols(h)] = _diff_finalize(o[:t], o[t:], (lq1, lk1, lq2, lk2), g_ref, lam_init).astype(o_ref.dtype)


def _diff_sample(proj, k_past, v_past, lams, subln_g, *, bsz, t, lam_init):
    past = k_past.shape[1]
    tk = min(512, past)
    nk = past // tk
    width = DIFF_HEADS * HEAD_DIM
    kern = functools.partial(_diff_sample_kernel, t=t, tk=tk, past=past, lam_init=lam_init)
    zero = lambda b, j: (0, 0)
    cache = pl.BlockSpec((None, tk, DIFF_HEADS, HEAD_DIM), lambda b, j: (b, j, 0, 0))
    return pl.pallas_call(
        kern,
        out_shape=jax.ShapeDtypeStruct((bsz * t, width), BF16),
        grid=(bsz, nk),
        in_specs=[
            pl.BlockSpec(memory_space=pltpu.SMEM),
            pl.BlockSpec((t, width), lambda b, j: (b, COL_DQ // width)),
            pl.BlockSpec((t, width), lambda b, j: (b, COL_DK // width)),
            pl.BlockSpec((t, width), lambda b, j: (b, COL_DV // width)),
            cache, cache,
        ] + [pl.BlockSpec((1, DIFF_HEAD_DIM), zero)] * 4 + [pl.BlockSpec((1, HEAD_DIM), zero)],
        out_specs=pl.BlockSpec((t, width), lambda b, j: (b, 0)),
        scratch_shapes=[pltpu.VMEM((DIFF_HEADS, 2 * t, 1), F32), pltpu.VMEM((DIFF_HEADS, 2 * t, 1), F32),
                        pltpu.VMEM((DIFF_HEADS, 2 * t, HEAD_DIM), F32), pltpu.VMEM((DIFF_HEADS, 2 * t, LANES), BF16)],
        compiler_params=_cparams(("parallel", "arbitrary")),
        name="diff_attn_sample",
    )(_alibi_slopes(), proj, proj, proj, k_past, v_past, *lams, subln_g.reshape(1, -1))


def _merge_kernel(oa_ref, ob_ref, wa_ref, wb_ref, ga_ref, gb_ref, o_ref):
    ya = _dot(oa_ref[...], wa_ref[...])
    yb = _dot(ob_ref[...], wb_ref[...])
    o_ref[...] = (jax.nn.sigmoid(ga_ref[...]) * ya + jax.nn.sigmoid(gb_ref[...]) * yb).astype(o_ref.dtype)


def _merge(o_a, o_b, wa, wb, proj, *, tm):
    m = o_a.shape[0]
    d = wa.shape[1]
    tn = 512
    ga0, gb0 = COL_GA // tn, COL_GB // tn
    return pl.pallas_call(
        _merge_kernel,
        out_shape=jax.ShapeDtypeStruct((m, d), BF16),
        grid=(m // tm, d // tn),
        in_specs=[
            pl.BlockSpec((tm, o_a.shape[1]), lambda i, j: (i, 0)),
            pl.BlockSpec((tm, o_b.shape[1]), lambda i, j: (i, 0)),
            pl.BlockSpec((wa.shape[0], tn), lambda i, j: (0, j)),
            pl.BlockSpec((wb.shape[0], tn), lambda i, j: (0, j)),
            pl.BlockSpec((tm, tn), lambda i, j: (i, ga0 + j)),
            pl.BlockSpec((tm, tn), lambda i, j: (i, gb0 + j)),
        ],
        out_specs=pl.BlockSpec((tm, tn), lambda i, j: (i, j)),
        compiler_params=_cparams(("parallel", "arbitrary")),
        name="merge",
    )(o_a, o_b, wa, wb, proj, proj)


def _cross_kernel(q_ref, k_ref, v_ref, o_ref):
    dh = q_ref.shape[-1] // MEM_HEADS
    for h in range(MEM_HEADS):
        sl = slice(h * dh, (h + 1) * dh)
        s = _dot_nt(q_ref[:, sl], k_ref[0, :, sl].astype(BF16)) * (dh ** -0.5)
        s = s - jnp.max(s, axis=-1, keepdims=True)
        p = jnp.exp(s)
        p = p / jnp.sum(p, axis=-1, keepdims=True)
        o_ref[:, sl] = _dot(p.astype(BF16), v_ref[0, :, sl].astype(BF16)).astype(o_ref.dtype)


def _cross_attend(q, mem_k, mem_v, *, bsz, t, tm):
    d = q.shape[1]
    nt = t // tm
    mt = mem_k.shape[1]
    return pl.pallas_call(
        _cross_kernel,
        out_shape=jax.ShapeDtypeStruct(q.shape, BF16),
        grid=(bsz, nt),
        in_specs=[
            pl.BlockSpec((tm, d), lambda b, i: (b * nt + i, 0)),
            pl.BlockSpec((1, mt, d), lambda b, i: (b, 0, 0)),
            pl.BlockSpec((1, mt, d), lambda b, i: (b, 0, 0)),
        ],
        out_specs=pl.BlockSpec((tm, d), lambda b, i: (b * nt + i, 0)),
        compiler_params=_cparams(("parallel", "arbitrary")),
        name="cross_attn",
    )(q, mem_k, mem_v)


def _topk_rows(work, cidx, k):
    n, tn = work.shape
    rid = lax.broadcasted_iota(jnp.int32, (n, tn), 0)
    kid = lax.broadcasted_iota(jnp.int32, (k, tn), 0)
    vals = jnp.zeros((k, tn), F32)
    idxs = jnp.zeros((k, tn), jnp.int32)
    for t in range(k):
        m = jnp.max(work, axis=0, keepdims=True)
        pos = jnp.min(jnp.where(work == m, rid, n), axis=0, keepdims=True)
        hit = rid == pos
        if cidx is None:
            picked = pos
        else:
            picked = jnp.sum(jnp.where(hit, cidx, 0), axis=0, keepdims=True)
        vals = jnp.where(kid == t, m, vals)
        idxs = jnp.where(kid == t, picked, idxs)
        work = jnp.where(hit, -jnp.inf, work)
    return vals, idxs


def _product_candidates(v1, i1, v2, i2, k):
    assert k == 2 * SUBLANES
    sub = lax.broadcasted_iota(jnp.int32, (SUBLANES, v1.shape[1]), 0)
    vals, idxs = [], []
    for a in range(k // 2):
        nb = k // (a + 1)
        width = k if nb > SUBLANES else SUBLANES
        cv = v1[a:a + 1, :] + v2[:width, :]
        ci = i1[a:a + 1, :] * PEER_N_KEYS + i2[:width, :]
        if nb < SUBLANES:
            cv = jnp.where(sub < nb, cv, -jnp.inf)
        vals.append(cv)
        idxs.append(ci)
    vals.append(v1[k // 2:, :] + v2[0:1, :])
    idxs.append(i1[k // 2:, :] * PEER_N_KEYS + i2[0:1, :])
    return jnp.concatenate(vals, axis=0), jnp.concatenate(idxs, axis=0)


def _peer_route_kernel(q_ref, keys_ref, e_ref, g_ref):
    tn = q_ref.shape[0]
    k = PEER_TOPK
    for h in range(PEER_HEADS):
        sub = []
        for p in range(2):
            c0 = (h * 2 + p) * PEER_N_KEYS
            s = _dot_nt(keys_ref[h, p], q_ref[:, c0:c0 + PEER_N_KEYS], HIGHEST)
            sub.append(_topk_rows(s, None, k))
        (v1, i1), (v2, i2) = sub
        cand, cidx = _product_candidates(v1, i1, v2, i2, k)
        best, eidx = _topk_rows(cand, cidx, k)
        ex = jnp.exp(best - jnp.max(best, axis=0, keepdims=True))
        gate = ex / jnp.sum(ex, axis=0, keepdims=True)
        e_ref[h * k:(h + 1) * k, :] = eidx
        g_ref[h * k:(h + 1) * k, :] = gate


HALF_KEYS = PEER_N_KEYS // 2
BF16_HIGH_MASK = 0xFFFF0000


def _peer_gate_kernel(et_ref, gt_ref, o_ref, e_scr, g_scr):
    tg = et_ref.shape[1]
    e_scr[...] = et_ref[...].T
    g_scr[...] = gt_ref[...].T
    kid = lax.broadcasted_iota(jnp.int32, (PEER_N_KEYS, PEER_N_KEYS), 0)

    def body(n8, carry):
        rows = pl.ds(pl.multiple_of(n8 * SUBLANES, SUBLANES), SUBLANES)
        e8 = e_scr[rows, :]
        g8 = g_scr[rows, :]
        a8 = jnp.right_shift(e8, 7)
        b8 = jnp.bitwise_and(e8, PEER_N_KEYS - 1)
        words = []
        for s in range(SUBLANES):
            at = jnp.where(a8[s:s + 1, :] == kid, g8[s:s + 1, :], 0.0).astype(BF16)
            bt = jnp.where(b8[s:s + 1, :] == kid, 1.0, 0.0).astype(BF16)
            grid = _dot_nt(at, bt)
            bits = pltpu.bitcast(grid.astype(BF16).astype(F32), jnp.uint32)
            words.append(jnp.bitwise_or(jnp.bitwise_and(bits[HALF_KEYS:, :], jnp.uint32(BF16_HIGH_MASK)),
                                        jnp.right_shift(bits[:HALF_KEYS, :], jnp.uint32(16))))
        o_ref[:, rows, :] = jnp.swapaxes(jnp.stack(words, axis=0), 0, 1)
        return carry

    lax.fori_loop(0, tg // SUBLANES, body, 0)


def _peer_route(qp, sub_keys, *, tn):
    n = qp.shape[0]
    rows = PEER_HEADS * PEER_TOPK
    e_t, g_t = pl.pallas_call(
        _peer_route_kernel,
        out_shape=(jax.ShapeDtypeStruct((rows, n), jnp.int32), jax.ShapeDtypeStruct((rows, n), F32)),
        grid=(n // tn,),
        in_specs=[
            pl.BlockSpec((tn, qp.shape[1]), lambda i: (i, 0)),
            pl.BlockSpec(sub_keys.shape, lambda i: (0, 0, 0, 0)),
        ],
        out_specs=(pl.BlockSpec((rows, tn), lambda i: (0, i)), pl.BlockSpec((rows, tn), lambda i: (0, i))),
        compiler_params=_cparams(("parallel",)),
        name="peer_route",
    )(qp, sub_keys)
    return pl.pallas_call(
        _peer_gate_kernel,
        out_shape=jax.ShapeDtypeStruct((HALF_KEYS, n, PEER_N_KEYS), jnp.uint32),
        grid=(n // tn,),
        in_specs=[pl.BlockSpec((rows, tn), lambda i: (0, i)), pl.BlockSpec((rows, tn), lambda i: (0, i))],
        out_specs=pl.BlockSpec((HALF_KEYS, tn, PEER_N_KEYS), lambda i: (0, i, 0)),
        scratch_shapes=[pltpu.VMEM((tn, rows), jnp.int32), pltpu.VMEM((tn, rows), F32)],
        compiler_params=_cparams(("parallel",)),
        name="peer_gates",
    )(e_t, g_t)


def _gelu_tanh(x):
    return 0.5 * x * (1.0 + jnp.tanh(math.sqrt(2.0 / math.pi) * (x + 0.044715 * (x * x * x))))


PEER_BLOCKS_PER_STEP = 4


def _peer_expert_kernel(x_ref, gn_ref, gpa_ref, gpb_ref, *rest):
    nblk = PEER_BLOCKS_PER_STEP
    u_refs, v_refs = rest[:nblk], rest[nblk:2 * nblk]
    gf_ref, o_ref, h_scr, acc_scr, coef_scr = rest[2 * nblk:]
    s = pl.program_id(1)
    last = pl.num_programs(1) - 1

    @pl.when(s == 0)
    def _():
        h_scr[...] = _rms(x_ref[...], gn_ref[...]).astype(BF16)
        acc_scr[...] = jnp.zeros(acc_scr.shape, F32)
        coef_scr[...] = jnp.zeros(coef_scr.shape, BF16)

    v_cat = jnp.concatenate([v[...] for v in v_refs], axis=0)
    acc_scr[...] += _dot(coef_scr[(s + 1) % 2], v_cat)

    def unpack(word):
        return (pltpu.bitcast(jnp.left_shift(word, jnp.uint32(16)), F32),
                pltpu.bitcast(jnp.bitwise_and(word, jnp.uint32(BF16_HIGH_MASK)), F32))

    a_lo, a_hi = unpack(gpa_ref[...])
    b_lo, b_hi = unpack(gpb_ref[...])
    u_cat = jnp.concatenate([u[...] for u in u_refs], axis=0)
    hid = _gelu_tanh(_dot_nt(h_scr[...], u_cat))
    coef_scr[s % 2] = (jnp.concatenate([a_lo, b_lo, a_hi, b_hi], axis=1) * hid).astype(BF16)

    @pl.when(s == last)
    def _():
        o_ref[...] = _rms(x_ref[...] + acc_scr[...], gf_ref[...])


def _peer_experts(x, norm_g, gates, u_tab, v_tab, final_g, *, tn):
    n, d = x.shape
    eb = PEER_N_KEYS
    assert u_tab.shape[0] == eb * eb
    nblk = PEER_BLOCKS_PER_STEP
    ns = eb // nblk

    def cur(s):
        return jnp.minimum(s, ns - 1)

    def prev(s):
        return jnp.maximum(s - 1, 0)

    def table(which, q):
        return pl.BlockSpec((eb, d), lambda i, s: (which(s) + q * ns, 0))

    return pl.pallas_call(
        _peer_expert_kernel,
        out_shape=jax.ShapeDtypeStruct((n, d), F32),
        grid=(n // tn, ns + 1),
        in_specs=[
            pl.BlockSpec((tn, d), lambda i, s: (i, 0)),
            pl.BlockSpec((1, d), lambda i, s: (0, 0)),
            pl.BlockSpec((None, tn, eb), lambda i, s: (cur(s), i, 0)),
            pl.BlockSpec((None, tn, eb), lambda i, s: (cur(s) + ns, i, 0)),
        ] + [table(cur, q) for q in range(nblk)] + [table(prev, q) for q in range(nblk)] + [
            pl.BlockSpec((1, d), lambda i, s: (0, 0)),
        ],
        out_specs=pl.BlockSpec((tn, d), lambda i, s: (i, 0)),
        scratch_shapes=[pltpu.VMEM((tn, d), BF16), pltpu.VMEM((tn, d), F32),
                        pltpu.VMEM((2, tn, nblk * eb), BF16)],
        compiler_params=_cparams(("parallel", "arbitrary")),
        name="peer_experts",
    )(x, norm_g.reshape(1, d), gates, gates, *([u_tab] * nblk), *([v_tab] * nblk), final_g.reshape(1, d))


def _reorder_w_in(w_in):
    c_qkv = 3 * GDN_WIDTH
    c_z = c_qkv + GDN_WIDTH
    c_ba = c_z + 2 * GDN_HEADS
    rest = w_in[:, c_ba:]
    pad = jnp.zeros((w_in.shape[0], IN_COLS_PADDED - COL_BA - 2 * GDN_HEADS), w_in.dtype)
    return jnp.concatenate([w_in[:, :c_z], rest, w_in[:, c_z:c_ba], pad], axis=1).astype(BF16)


def _layer(x, wts, layer, mem_k, mem_v, conv_prev, gdn_state, k_past, v_past, *, tm):
    bsz, t, d = x.shape
    n = bsz * t
    x2d = x.reshape(n, d)
    proj = _matmul(x2d, wts["w_in"], norm_g=wts["norm_mix_g"], tm=min(IN_TM, n), tn=IN_TN, name="in_proj")
    conv_dim = 3 * GDN_WIDTH
    conv_prev8 = jnp.concatenate(
        [jnp.zeros((bsz, SUBLANES - (GDN_CONV - 1), conv_dim), F32), conv_prev.astype(F32)], axis=1)
    o_a, s_new = _gdn(proj, conv_prev8, gdn_state.astype(F32), wts["gdn_conv_w"], wts["gdn_a_log"],
                      wts["gdn_dt_bias"], wts["gdn_norm_g"], bsz=bsz, t=t)
    conv_new = proj.reshape(bsz, t, -1)[:, t - (GDN_CONV - 1):, COL_QKV:COL_QKV + conv_dim]
    lam_init = 0.8 - 0.6 * math.exp(-0.3 * layer)
    lams = tuple(wts[k].reshape(1, -1) for k in ("diff_lambda_q1", "diff_lambda_k1", "diff_lambda_q2", "diff_lambda_k2"))
    if k_past is None:
        o_b = _diff_prompt(proj, lams, wts["diff_subln_g"], bsz=bsz, t=t, lam_init=lam_init)
    else:
        o_b = _diff_sample(proj, k_past, v_past, lams, wts["diff_subln_g"], bsz=bsz, t=t, lam_init=lam_init)
    k_rows = proj[:, COL_DK:COL_DK + DIFF_HEADS * HEAD_DIM].reshape(bsz, t, DIFF_HEADS, HEAD_DIM)
    v_rows = proj[:, COL_DV:COL_DV + DIFF_HEADS * HEAD_DIM].reshape(bsz, t, DIFF_HEADS, HEAD_DIM)
    merged = _merge(o_a, o_b, wts["w_branch_a"], wts["w_branch_b"], proj, tm=tm)
    x1 = _matmul(merged, wts["w_out"], residual=x2d, tm=tm, tn=512, name="out_proj")
    qm = _matmul(x1, wts["w_mq"], norm_g=wts["norm_cross_g"], tm=tm, tn=512, out_dtype=BF16, name="mem_q")
    mt = mem_k.shape[1]
    oc = _cross_attend(qm, mem_k.reshape(bsz, mt, d), mem_v.reshape(bsz, mt, d), bsz=bsz, t=t, tm=min(tm, t))
    x2 = _matmul(oc, wts["w_mo"], residual=x1, tm=tm, tn=512, name="mem_o")
    qp = _matmul(x2, wts["peer_w_q"], norm_g=wts["norm_ffn_g"], tm=tm, tn=512, name="peer_q")
    tn = min(512, n)
    gates = _peer_route(qp, wts["peer_sub_keys"], tn=LANES)
    y = _peer_experts(x2, wts["norm_ffn_g"], gates, wts["peer_u"], wts["peer_v"], wts["final_norm_g"], tn=tn)
    return y.reshape(bsz, t, d), k_rows, v_rows, s_new, conv_new


def kernel(x_prompt, x_sample, cache_diff_k, cache_diff_v, state_gdn, state_conv, cache_mem_k, cache_mem_v,
           mem_prompt, norm_mix_g, w_in, gdn_conv_w, gdn_a_log, gdn_dt_bias, gdn_norm_g,
           diff_lambda_q1, diff_lambda_k1, diff_lambda_q2, diff_lambda_k2, diff_subln_g,
           w_branch_a, w_branch_b, w_out, norm_cross_g, norm_mem_g, w_mq, w_mk, w_mv, w_mo,
           norm_ffn_g, peer_w_q, peer_sub_keys, peer_u, peer_v, final_norm_g):
    depth = w_in.shape[0]
    assert depth == 1, "final norm is fused into the last layer's PEER kernel"
    l = 0
    wts = {
        "norm_mix_g": norm_mix_g[l], "w_in": _reorder_w_in(w_in[l]), "gdn_conv_w": gdn_conv_w[l],
        "gdn_a_log": gdn_a_log[l], "gdn_dt_bias": gdn_dt_bias[l], "gdn_norm_g": gdn_norm_g[l],
        "diff_lambda_q1": diff_lambda_q1[l], "diff_lambda_k1": diff_lambda_k1[l],
        "diff_lambda_q2": diff_lambda_q2[l], "diff_lambda_k2": diff_lambda_k2[l],
        "diff_subln_g": diff_subln_g[l], "w_branch_a": w_branch_a[l].astype(BF16),
        "w_branch_b": w_branch_b[l].astype(BF16), "w_out": w_out[l].astype(BF16),
        "norm_cross_g": norm_cross_g[l], "w_mq": w_mq[l].astype(BF16), "w_mo": w_mo[l].astype(BF16),
        "norm_ffn_g": norm_ffn_g[l], "peer_w_q": peer_w_q[l].astype(BF16), "peer_sub_keys": peer_sub_keys[l],
        "peer_u": peer_u[l].astype(BF16), "peer_v": peer_v[l].astype(BF16), "final_norm_g": final_norm_g,
    }
    bp, tp, d = x_prompt.shape
    bs, ts, _ = x_sample.shape
    mem2d = mem_prompt.reshape(-1, d)
    tmm = min(512, mem2d.shape[0])
    mem_k = _matmul(mem2d, w_mk[l].astype(BF16), norm_g=norm_mem_g[l], tm=tmm, tn=512, name="mem_k")
    mem_v = _matmul(mem2d, w_mv[l].astype(BF16), norm_g=norm_mem_g[l], tm=tmm, tn=512, name="mem_v")
    mshape = mem_prompt.shape[:2] + (MEM_HEADS, d // MEM_HEADS)
    mem_k = mem_k.reshape(mshape)
    mem_v = mem_v.reshape(mshape)
    conv0 = jnp.zeros((bp, GDN_CONV - 1, 3 * GDN_WIDTH), F32)
    s0 = jnp.zeros((bp, GDN_HEADS, HEAD_DIM, HEAD_DIM), F32)
    yp, pk, pv, ps, pc = _layer(x_prompt, wts, l, mem_k, mem_v, conv0, s0, None, None, tm=min(512, bp * tp))
    ys, sk, sv, ss, sc = _layer(x_sample, wts, l, cache_mem_k[l], cache_mem_v[l], state_conv[l], state_gdn[l],
                                cache_diff_k[l], cache_diff_v[l], tm=min(256, bs * ts))
    return (yp, ys, pk[None], pv[None], ps[None], pc[None], mem_k[None], mem_v[None],
            sk[None], sv[None], ss[None], sc[None])
```

```python
import functools
import math

import jax
import jax.numpy as jnp
from jax import lax
from jax.experimental import pallas as pl
from jax.experimental.pallas import tpu as pltpu

F32 = jnp.float32
BF16 = jnp.bfloat16
HIGHEST = lax.Precision.HIGHEST

RMS_EPS = 1e-6
CHUNK = 64
CHUNK_SHIFT = 6
DIFF_TQ = 512
DIFF_SUB = 256
GDN_HEADS = 8
HEAD_DIM = 128
GDN_WIDTH = GDN_HEADS * HEAD_DIM
GDN_CONV = 4
DIFF_HEADS = 8
DIFF_HEAD_DIM = 64
MEM_HEADS = 4
PEER_HEADS = 8
PEER_N_KEYS = 128
PEER_TOPK = 16
LANES = 128
SUBLANES = 8
V7X_VMEM_LIMIT = 52 * 1024 * 1024

COL_QKV = 0
COL_Z = 3072
COL_DQ = 4096
COL_DK = 5120
COL_DV = 6144
COL_GA = 7168
COL_GB = 9216
COL_BA = 11264
IN_COLS_PADDED = 11520
IN_TN = 768
IN_TM = 1024


def _cparams(sem, vmem=V7X_VMEM_LIMIT):
    return pltpu.CompilerParams(dimension_semantics=sem, vmem_limit_bytes=vmem)


def _dot(a, b, precision=None):
    return jnp.dot(a, b, preferred_element_type=F32, precision=precision)


def _dot_nt(a, b, precision=None):
    return lax.dot_general(a, b, (((1,), (1,)), ((), ())), preferred_element_type=F32, precision=precision)


def _split_bf16(x):
    hi = x.astype(BF16)
    return hi, (x - hi.astype(F32)).astype(BF16)


def _dot3_impl(dot, a, b):
    a_hi, a_lo = _split_bf16(a)
    b_hi, b_lo = _split_bf16(b)
    m = a.shape[0]
    both = dot(jnp.concatenate([a_hi, a_lo], axis=0), b_hi)
    return both[:m] + both[m:] + dot(a_hi, b_lo)


def _dot3(a, b):
    return _dot3_impl(_dot, a, b)


def _dot3_nt(a, b):
    return _dot3_impl(_dot_nt, a, b)


def _rms(xf, g):
    return xf * lax.rsqrt(jnp.mean(xf * xf, axis=-1, keepdims=True) + RMS_EPS) * g


def _mm_kernel(*refs, has_norm, has_res):
    it = iter(refs)
    x_ref = next(it)
    g_ref = next(it) if has_norm else None
    w_ref = next(it)
    r_ref = next(it) if has_res else None
    o_ref = next(it)
    h_scr = next(it)

    @pl.when(pl.program_id(1) == 0)
    def _():
        xf = x_ref[...].astype(F32)
        if has_norm:
            xf = _rms(xf, g_ref[...])
        h_scr[...] = xf.astype(BF16)

    acc = _dot(h_scr[...], w_ref[...])
    if has_res:
        acc = acc + r_ref[...]
    o_ref[...] = acc.astype(o_ref.dtype)


def _matmul(x, w, *, norm_g=None, residual=None, tm, tn, out_dtype=F32, name="matmul"):
    m, k = x.shape
    n = w.shape[1]
    assert m % tm == 0 and n % tn == 0, (m, n, tm, tn)
    in_specs = [pl.BlockSpec((tm, k), lambda i, j: (i, 0))]
    args = [x]
    if norm_g is not None:
        in_specs.append(pl.BlockSpec((1, k), lambda i, j: (0, 0)))
        args.append(norm_g.reshape(1, k).astype(F32))
    in_specs.append(pl.BlockSpec((k, tn), lambda i, j: (0, j)))
    args.append(w)
    if residual is not None:
        in_specs.append(pl.BlockSpec((tm, tn), lambda i, j: (i, j)))
        args.append(residual)
    return pl.pallas_call(
        functools.partial(_mm_kernel, has_norm=norm_g is not None, has_res=residual is not None),
        out_shape=jax.ShapeDtypeStruct((m, n), out_dtype),
        grid=(m // tm, n // tn),
        in_specs=in_specs,
        out_specs=pl.BlockSpec((tm, tn), lambda i, j: (i, j)),
        scratch_shapes=[pltpu.VMEM((tm, k), BF16)],
        compiler_params=_cparams(("parallel", "arbitrary")),
        name=name,
    )(*args)


def _gdn_kernel(qkv_ref, z_ref, ba_ref, prev_ref, s0_ref, convw_ref, alog_ref, dtb_ref, ng_ref,
                o_ref, sfin_ref, s_scr, xbuf, *, L):
    c = pl.program_id(1)

    @pl.when(c == 0)
    def _():
        s_scr[...] = s0_ref[0]
        xbuf[0:SUBLANES, :] = prev_ref[0]

    xbuf[SUBLANES:SUBLANES + L, :] = qkv_ref[...]
    base = SUBLANES - (GDN_CONV - 1)
    y = xbuf[base:base + L, :] * convw_ref[0:1, :]
    for i in range(1, GDN_CONV):
        y = y + xbuf[base + i:base + i + L, :] * convw_ref[i:i + 1, :]
    y = y * jax.nn.sigmoid(y)
    tail = xbuf[L:L + SUBLANES, :]
    xbuf[0:SUBLANES, :] = tail

    row = lax.broadcasted_iota(jnp.int32, (L, L), 0)
    col = lax.broadcasted_iota(jnp.int32, (L, L), 1)
    lower = row >= col
    strict = row > col
    eye = (row == col).astype(F32)
    n_double = int(math.log2(L)) - 1
    assert 2 ** (n_double + 1) == L

    ba = ba_ref[...]
    beta_all = jax.nn.sigmoid(ba)
    a_in = ba + dtb_ref[...]
    softplus = jnp.maximum(a_in, 0.0) + jnp.log(1.0 + jnp.exp(-jnp.abs(a_in)))
    g_all = -jnp.exp(alog_ref[...]) * softplus
    g_cum = _dot(lower.astype(F32), g_all, HIGHEST)
    g_cum_t = g_cum.T

    heads = range(GDN_HEADS)
    cat0 = functools.partial(jnp.concatenate, axis=0)

    def head_cols(part, h):
        return y[:, part * GDN_WIDTH + h * HEAD_DIM:part * GDN_WIDTH + (h + 1) * HEAD_DIM]

    q = [head_cols(0, h) for h in heads]
    k = [head_cols(1, h) for h in heads]
    v = [head_cols(2, h) for h in heads]
    q = [a * lax.rsqrt(jnp.sum(a * a, axis=-1, keepdims=True) + 1e-6) * (HEAD_DIM ** -0.5) for a in q]
    k = [a * lax.rsqrt(jnp.sum(a * a, axis=-1, keepdims=True) + 1e-6) for a in k]
    beta = [beta_all[:, h:h + 1] for h in heads]
    g_col = [g_cum[:, GDN_HEADS + h:GDN_HEADS + h + 1] for h in heads]
    g_row = [g_cum_t[GDN_HEADS + h:GDN_HEADS + h + 1, :] for h in heads]
    decay = [jnp.where(lower, jnp.exp(jnp.where(lower, g_col[h] - g_row[h], 0.0)), 0.0) for h in heads]
    g_last = [g_col[h][L - 1:L, :] for h in heads]
    e_col = [jnp.exp(g_col[h]) for h in heads]
    qk_kk = [_dot3_nt(cat0([q[h], k[h]]), k[h]) for h in heads]
    x = [-jnp.where(strict, beta[h] * qk_kk[h][L:] * decay[h], 0.0) for h in heads]
    t_inv = [eye + x[h] for h in heads]
    xp = [_dot3(x[h], x[h]) for h in heads]
    for _ in range(n_double - 1):
        both = [_dot3(cat0([t_inv[h], xp[h]]), xp[h]) for h in heads]
        t_inv = [t_inv[h] + both[h][:L] for h in heads]
        xp = [both[h][L:] for h in heads]
    t_inv = [t_inv[h] + _dot3(t_inv[h], xp[h]) for h in heads]
    sol = [_dot3(t_inv[h], jnp.concatenate([v[h] * beta[h], k[h] * (beta[h] * e_col[h])], axis=-1)) for h in heads]
    qk = [jnp.where(lower, qk_kk[h][:L] * decay[h], 0.0) for h in heads]
    k_dec_t = [(k[h] * jnp.exp(g_last[h] - g_col[h])).T for h in heads]
    s = [s_scr[h] for h in heads]
    ws_qs = [_dot3(cat0([sol[h][:, HEAD_DIM:], q[h] * e_col[h]]), s[h]) for h in heads]
    v_new = [sol[h][:, :HEAD_DIM] - ws_qs[h][:L] for h in heads]
    tail = [_dot3(cat0([qk[h], k_dec_t[h]]), v_new[h]) for h in heads]
    for h in heads:
        s_scr[h] = s[h] * jnp.exp(g_last[h]) + tail[h][L:]
        zf = z_ref[:, h * HEAD_DIM:(h + 1) * HEAD_DIM]
        o = _rms(ws_qs[h][L:] + tail[h][:L], ng_ref[...]) * (zf * jax.nn.sigmoid(zf))
        o_ref[:, h * HEAD_DIM:(h + 1) * HEAD_DIM] = o.astype(o_ref.dtype)

    @pl.when(c == pl.num_programs(1) - 1)
    def _():
        sfin_ref[0] = s_scr[...]


def _decay_lanes(p):
    return jnp.zeros((1, LANES), F32).at[0, GDN_HEADS:2 * GDN_HEADS].set(p.astype(F32))


def _gdn(proj, conv_prev8, s0, conv_w, a_log, dt_bias, norm_g, *, bsz, t):
    L = min(CHUNK, t)
    nc = t // L
    conv_dim = 3 * GDN_WIDTH
    kern = functools.partial(_gdn_kernel, L=L)
    o, s_fin = pl.pallas_call(
        kern,
        out_shape=(jax.ShapeDtypeStruct((bsz * t, GDN_WIDTH), BF16),
                   jax.ShapeDtypeStruct((bsz, GDN_HEADS, HEAD_DIM, HEAD_DIM), F32)),
        grid=(bsz, nc),
        in_specs=[
            pl.BlockSpec((L, conv_dim), lambda b, c: (b * nc + c, COL_QKV // conv_dim)),
            pl.BlockSpec((L, GDN_WIDTH), lambda b, c: (b * nc + c, COL_Z // GDN_WIDTH)),
            pl.BlockSpec((L, LANES), lambda b, c: (b * nc + c, COL_BA // LANES)),
            pl.BlockSpec((1, SUBLANES, conv_dim), lambda b, c: (b, 0, 0)),
            pl.BlockSpec((1, GDN_HEADS, HEAD_DIM, HEAD_DIM), lambda b, c: (b, 0, 0, 0)),
            pl.BlockSpec((GDN_CONV, conv_dim), lambda b, c: (0, 0)),
            pl.BlockSpec((1, LANES), lambda b, c: (0, 0)),
            pl.BlockSpec((1, LANES), lambda b, c: (0, 0)),
            pl.BlockSpec((1, HEAD_DIM), lambda b, c: (0, 0)),
        ],
        out_specs=(pl.BlockSpec((L, GDN_WIDTH), lambda b, c: (b * nc + c, 0)),
                   pl.BlockSpec((1, GDN_HEADS, HEAD_DIM, HEAD_DIM), lambda b, c: (b, 0, 0, 0))),
        scratch_shapes=[pltpu.VMEM((GDN_HEADS, HEAD_DIM, HEAD_DIM), F32),
                        pltpu.VMEM((SUBLANES + L + SUBLANES, conv_dim), F32)],
        compiler_params=_cparams(("parallel", "arbitrary")),
        name="gdn",
    )(proj, proj, proj, conv_prev8, s0, conv_w, _decay_lanes(a_log), _decay_lanes(dt_bias),
      norm_g.reshape(1, -1))
    return o, s_fin


def _lambda_value(lam_refs, lam_init):
    lq1, lk1, lq2, lk2 = (r[...] for r in lam_refs)
    return (jnp.exp(jnp.sum(lq1 * lk1, axis=-1, keepdims=True))
            - jnp.exp(jnp.sum(lq2 * lk2, axis=-1, keepdims=True)) + lam_init)


def _softmax_update(m_scr, l_scr, acc_scr, scores, values):
    idx = range(len(scores))
    m_old = [m_scr[n] for n in idx]
    m_new = [jnp.maximum(m_old[n], jnp.max(scores[n], axis=-1, keepdims=True)) for n in idx]
    p = [jnp.exp(scores[n] - m_new[n]) for n in idx]
    alpha = [jnp.exp(m_old[n] - m_new[n]) for n in idx]
    pv = [_dot(p[n].astype(BF16), values[n]) for n in idx]
    for n in idx:
        l_scr[n] = alpha[n] * l_scr[n] + jnp.sum(p[n], axis=-1, keepdims=True)
        acc_scr[n] = alpha[n] * acc_scr[n] + pv[n]
        m_scr[n] = m_new[n]


def _diff_finalize(o1, o2, lam_refs, g_ref, lam_init):
    lam = _lambda_value(lam_refs, lam_init)
    return _rms(o1 - lam * o2, g_ref[...]) * (1.0 - lam_init)


def _diff_init(m_scr, l_scr, acc_scr):
    m_scr[...] = jnp.full(m_scr.shape, -jnp.inf, F32)
    l_scr[...] = jnp.zeros(l_scr.shape, F32)
    acc_scr[...] = jnp.zeros(acc_scr.shape, F32)


def _near_bias(slope, r, q_pos, k_pos):
    visible = jnp.right_shift(k_pos, CHUNK_SHIFT) <= jnp.right_shift(q_pos, CHUNK_SHIFT)
    return jnp.where(visible, slope * (r - jnp.abs(q_pos - k_pos)).astype(F32), -jnp.inf)


def _split_distance(slope, d):
    return (-(slope * CHUNK) * jnp.right_shift(d, CHUNK_SHIFT).astype(F32),
            -slope * jnp.bitwise_and(d, CHUNK - 1).astype(F32))


def _diff_prompt_kernel(i_tab, j_tab, slopes_ref, q_ref, k_ref, v_ref, lq1, lk1, lq2, lk2, g_ref, o_ref,
                        m_scr, l_scr, acc_scr, lhs_scr, s_even, s_odd, *, tq, lam_init):
    h = pl.program_id(1)
    s_id = pl.program_id(2)
    n_pairs = pl.num_programs(2) - 1
    score_on = s_id < n_pairs
    consume_on = s_id >= 1
    i = i_tab[s_id]
    j = j_tab[s_id]
    ci = i_tab[jnp.maximum(s_id - 1, 0)]
    cj = j_tab[jnp.maximum(s_id - 1, 0)]
    slope = slopes_ref[h]
    half = DIFF_HEAD_DIM
    lane = lax.broadcasted_iota(jnp.int32, (tq, LANES), 1)

    @pl.when(jnp.logical_and(consume_on, cj == 0))
    def _():
        _diff_init(m_scr, l_scr, acc_scr)

    @pl.when(jnp.logical_and(score_on, j == 0))
    def _():
        qs = q_ref[...] * (DIFF_HEAD_DIM ** -0.5)
        lhs_scr[0] = jnp.where(lane < half, qs, 1.0).astype(BF16)
        lhs_scr[1] = jnp.where(lane >= half, qs, 1.0).astype(BF16)

    sub = min(DIFF_SUB, tq)
    units = [(n, qb) for qb in range(tq // sub) for n in range(2)]

    def score_tile(near, dst):
        kb = k_ref[...]
        if near:
            aug0 = aug1 = jnp.zeros_like(kb)
        else:
            d = (i - j) * tq - lax.broadcasted_iota(jnp.int32, (tq, LANES), 0)
            t_hi, t_lo = _split_distance(slope, d)
            aug0 = jnp.where(lane == half, t_hi, jnp.where(lane == half + 1, t_lo, 0.0))
            aug1 = jnp.where(lane == 0, t_hi, jnp.where(lane == 1, t_lo, 0.0))
        keys = [jnp.where(lane < half, kb, aug0).astype(BF16), jnp.where(lane >= half, kb, aug1).astype(BF16)]
        bias = None
        for n, qb in units:
            cols = slice(qb * sub, (qb + 1) * sub)
            s = _dot_nt(keys[n], lhs_scr[n, cols, :])
            if near:
                if n == 0:
                    k_pos = lax.broadcasted_iota(jnp.int32, (tq, sub), 0)
                    q_pos = qb * sub + lax.broadcasted_iota(jnp.int32, (tq, sub), 1)
                    bias = _near_bias(slope, q_pos, q_pos, k_pos)
                s = s + bias
            dst[n, :, cols] = s

    def consume_tile(src):
        vt = v_ref[...].T.astype(BF16)
        idx = range(len(units))
        cols = [slice(qb * sub, (qb + 1) * sub) for _, qb in units]
        s = [src[n, :, cols[u]] for u, (n, _) in enumerate(units)]
        m_old = [m_scr[n, :, cols[u]] for u, (n, _) in enumerate(units)]
        m_new = [jnp.maximum(m_old[u], jnp.max(s[u], axis=0, keepdims=True)) for u in idx]
        p = [jnp.exp(s[u] - m_new[u]) for u in idx]
        alpha = [jnp.exp(m_old[u] - m_new[u]) for u in idx]
        pv = [_dot(vt, p[u].astype(BF16)) for u in idx]
        for u, (n, _) in enumerate(units):
            l_scr[n, :, cols[u]] = alpha[u] * l_scr[n, :, cols[u]] + jnp.sum(p[u], axis=0, keepdims=True)
            acc_scr[n, :, cols[u]] = alpha[u] * acc_scr[n, :, cols[u]] + pv[u]
            m_scr[n, :, cols[u]] = m_new[u]

    def stage(cond, near, score, consume):
        for parity, (dst, src) in enumerate(((s_even, s_odd), (s_odd, s_even))):
            @pl.when(jnp.logical_and(cond, s_id % 2 == parity))
            def _(dst=dst, src=src):
                if score:
                    score_tile(near, dst)
                if consume:
                    consume_tile(src)

    both = jnp.logical_and(score_on, consume_on)
    stage(jnp.logical_and(both, j < i), False, True, True)
    stage(jnp.logical_and(both, j == i), True, True, True)
    stage(jnp.logical_not(consume_on), True, True, False)
    stage(jnp.logical_not(score_on), False, False, True)

    @pl.when(jnp.logical_and(consume_on, cj == ci))
    def _():
        o_ref[...] = _diff_finalize((acc_scr[0] / l_scr[0]).T, (acc_scr[1] / l_scr[1]).T,
                                    (lq1, lk1, lq2, lk2), g_ref, lam_init).astype(o_ref.dtype)


def _alibi_slopes():
    return jnp.asarray([2.0 ** (-8.0 * (i + 1) / DIFF_HEADS) for i in range(DIFF_HEADS)], F32)


def _diff_prompt(proj, lams, subln_g, *, bsz, t, lam_init):
    tq = min(DIFF_TQ, t)
    nq = t // tq
    assert tq % CHUNK == 0 and t <= CHUNK * 255
    kern = functools.partial(_diff_prompt_kernel, tq=tq, lam_init=lam_init)
    qc, kc, vc = COL_DQ // LANES, COL_DK // LANES, COL_DV // LANES
    pairs = [(i, j) for i in range(nq) for j in range(i + 1)]
    pairs.append(pairs[-1])
    i_tab = jnp.asarray([p[0] for p in pairs], jnp.int32)
    j_tab = jnp.asarray([p[1] for p in pairs], jnp.int32)
    zero = lambda b, h, s, it, jt: (0, 0)

    def consumed(tab, s):
        return tab[jnp.maximum(s - 1, 0)]

    grid_spec = pltpu.PrefetchScalarGridSpec(
        num_scalar_prefetch=2,
        grid=(bsz, DIFF_HEADS, len(pairs)),
        in_specs=[
            pl.BlockSpec(memory_space=pltpu.SMEM),
            pl.BlockSpec((tq, LANES), lambda b, h, s, it, jt: (b * nq + it[s], qc + h)),
            pl.BlockSpec((tq, LANES), lambda b, h, s, it, jt: (b * nq + jt[s], kc + h)),
            pl.BlockSpec((tq, LANES), lambda b, h, s, it, jt: (b * nq + consumed(jt, s), vc + h)),
        ] + [pl.BlockSpec((1, DIFF_HEAD_DIM), zero)] * 4 + [pl.BlockSpec((1, HEAD_DIM), zero)],
        out_specs=pl.BlockSpec((tq, LANES), lambda b, h, s, it, jt: (b * nq + consumed(it, s), h)),
        scratch_shapes=[pltpu.VMEM((2, 1, tq), F32), pltpu.VMEM((2, 1, tq), F32),
                        pltpu.VMEM((2, HEAD_DIM, tq), F32), pltpu.VMEM((2, tq, LANES), BF16),
                        pltpu.VMEM((2, tq, tq), F32), pltpu.VMEM((2, tq, tq), F32)],
    )
    return pl.pallas_call(
        kern,
        out_shape=jax.ShapeDtypeStruct((bsz * t, DIFF_HEADS * HEAD_DIM), BF16),
        grid_spec=grid_spec,
        compiler_params=_cparams(("parallel", "parallel", "arbitrary")),
        name="diff_attn_prompt",
    )(i_tab, j_tab, _alibi_slopes(), proj, proj, proj, *lams, subln_g.reshape(1, -1))


def _diff_sample_kernel(slopes_ref, q_ref, kn_ref, vn_ref, kp_ref, vp_ref, lq1, lk1, lq2, lk2, g_ref, o_ref,
                        m_scr, l_scr, acc_scr, lhs_scr, *, t, tk, past, lam_init):
    j = pl.program_id(1)
    heads = range(DIFF_HEADS)

    def cols(h):
        return slice(h * HEAD_DIM, (h + 1) * HEAD_DIM)

    @pl.when(j == 0)
    def _():
        _diff_init(m_scr, l_scr, acc_scr)
        lane = lax.broadcasted_iota(jnp.int32, (t, LANES), 1)
        r = lax.broadcasted_iota(jnp.int32, (t, t), 0)
        c = lax.broadcasted_iota(jnp.int32, (t, t), 1)
        scores = []
        for h in heads:
            qs = q_ref[:, cols(h)] * (DIFF_HEAD_DIM ** -0.5)
            lhs_scr[h, 0:t, :] = jnp.where(lane < DIFF_HEAD_DIM, qs, 0.0).astype(BF16)
            lhs_scr[h, t:2 * t, :] = jnp.where(lane >= DIFF_HEAD_DIM, qs, 0.0).astype(BF16)
            bias = _near_bias(slopes_ref[h], r, past + r, past + c)
            scores.append(_dot_nt(lhs_scr[h], kn_ref[:, cols(h)].astype(BF16)) + jnp.concatenate([bias, bias], axis=0))
        _softmax_update(m_scr, l_scr, acc_scr, scores, [vn_ref[:, cols(h)].astype(BF16) for h in heads])

    d = (past - j * tk - lax.broadcasted_iota(jnp.int32, (1, tk), 1)).astype(F32)
    kt = jnp.swapaxes(kp_ref[...], 0, 1).astype(BF16)
    vt = jnp.swapaxes(vp_ref[...], 0, 1).astype(BF16)
    scores = [_dot_nt(lhs_scr[h], kt[h]) - slopes_ref[h] * d for h in heads]
    _softmax_update(m_scr, l_scr, acc_scr, scores, [vt[h] for h in heads])

    @pl.when(j == pl.num_programs(1) - 1)
    def _():
        for h in heads:
            o = acc_scr[h] / l_scr[h]
            o_ref[:, cols(h)] = _diff_finalize(o[:t], o[t:], (lq1, lk1, lq2, lk2), g_ref, lam_init).astype(o_ref.dtype)


def _diff_sample(proj, k_past, v_past, lams, subln_g, *, bsz, t, lam_init):
    past = k_past.shape[1]
    tk = min(512, past)
    nk = past // tk
    width = DIFF_HEADS * HEAD_DIM
    kern = functools.partial(_diff_sample_kernel, t=t, tk=tk, past=past, lam_init=lam_init)
    zero = lambda b, j: (0, 0)
    cache = pl.BlockSpec((None, tk, DIFF_HEADS, HEAD_DIM), lambda b, j: (b, j, 0, 0))
    return pl.pallas_call(
        kern,
        out_shape=jax.ShapeDtypeStruct((bsz * t, width), BF16),
        grid=(bsz, nk),
        in_specs=[
            pl.BlockSpec(memory_space=pltpu.SMEM),
            pl.BlockSpec((t, width), lambda b, j: (b, COL_DQ // width)),
            pl.BlockSpec((t, width), lambda b, j: (b, COL_DK // width)),
            pl.BlockSpec((t, width), lambda b, j: (b, COL_DV // width)),
            cache, cache,
        ] + [pl.BlockSpec((1, DIFF_HEAD_DIM), zero)] * 4 + [pl.BlockSpec((1, HEAD_DIM), zero)],
        out_specs=pl.BlockSpec((t, width), lambda b, j: (b, 0)),
        scratch_shapes=[pltpu.VMEM((DIFF_HEADS, 2 * t, 1), F32), pltpu.VMEM((DIFF_HEADS, 2 * t, 1), F32),
                        pltpu.VMEM((DIFF_HEADS, 2 * t, HEAD_DIM), F32), pltpu.VMEM((DIFF_HEADS, 2 * t, LANES), BF16)],
        compiler_params=_cparams(("parallel", "arbitrary")),
        name="diff_attn_sample",
    )(_alibi_slopes(), proj, proj, proj, k_past, v_past, *lams, subln_g.reshape(1, -1))


def _merge_kernel(oa_ref, ob_ref, wa_ref, wb_ref, ga_ref, gb_ref, o_ref):
    ya = _dot(oa_ref[...], wa_ref[...])
    yb = _dot(ob_ref[...], wb_ref[...])
    o_ref[...] = (jax.nn.sigmoid(ga_ref[...]) * ya + jax.nn.sigmoid(gb_ref[...]) * yb).astype(o_ref.dtype)


def _merge(o_a, o_b, wa, wb, proj, *, tm):
    m = o_a.shape[0]
    d = wa.shape[1]
    tn = 512
    ga0, gb0 = COL_GA // tn, COL_GB // tn
    return pl.pallas_call(
        _merge_kernel,
        out_shape=jax.ShapeDtypeStruct((m, d), BF16),
        grid=(m // tm, d // tn),
        in_specs=[
            pl.BlockSpec((tm, o_a.shape[1]), lambda i, j: (i, 0)),
            pl.BlockSpec((tm, o_b.shape[1]), lambda i, j: (i, 0)),
            pl.BlockSpec((wa.shape[0], tn), lambda i, j: (0, j)),
            pl.BlockSpec((wb.shape[0], tn), lambda i, j: (0, j)),
            pl.BlockSpec((tm, tn), lambda i, j: (i, ga0 + j)),
            pl.BlockSpec((tm, tn), lambda i, j: (i, gb0 + j)),
        ],
        out_specs=pl.BlockSpec((tm, tn), lambda i, j: (i, j)),
        compiler_params=_cparams(("parallel", "arbitrary")),
        name="merge",
    )(o_a, o_b, wa, wb, proj, proj)


def _cross_kernel(q_ref, k_ref, v_ref, o_ref):
    dh = q_ref.shape[-1] // MEM_HEADS
    for h in range(MEM_HEADS):
        sl = slice(h * dh, (h + 1) * dh)
        s = _dot_nt(q_ref[:, sl], k_ref[0, :, sl].astype(BF16)) * (dh ** -0.5)
        s = s - jnp.max(s, axis=-1, keepdims=True)
        p = jnp.exp(s)
        p = p / jnp.sum(p, axis=-1, keepdims=True)
        o_ref[:, sl] = _dot(p.astype(BF16), v_ref[0, :, sl].astype(BF16)).astype(o_ref.dtype)


def _cross_attend(q, mem_k, mem_v, *, bsz, t, tm):
    d = q.shape[1]
    nt = t // tm
    mt = mem_k.shape[1]
    return pl.pallas_call(
        _cross_kernel,
        out_shape=jax.ShapeDtypeStruct(q.shape, BF16),
        grid=(bsz, nt),
        in_specs=[
            pl.BlockSpec((tm, d), lambda b, i: (b * nt + i, 0)),
            pl.BlockSpec((1, mt, d), lambda b, i: (b, 0, 0)),
            pl.BlockSpec((1, mt, d), lambda b, i: (b, 0, 0)),
        ],
        out_specs=pl.BlockSpec((tm, d), lambda b, i: (b * nt + i, 0)),
        compiler_params=_cparams(("parallel", "arbitrary")),
        name="cross_attn",
    )(q, mem_k, mem_v)


def _topk_rows(work, cidx, k):
    n, tn = work.shape
    rid = lax.broadcasted_iota(jnp.int32, (n, tn), 0)
    kid = lax.broadcasted_iota(jnp.int32, (k, tn), 0)
    vals = jnp.zeros((k, tn), F32)
    idxs = jnp.zeros((k, tn), jnp.int32)
    for t in range(k):
        m = jnp.max(work, axis=0, keepdims=True)
        pos = jnp.min(jnp.where(work == m, rid, n), axis=0, keepdims=True)
        hit = rid == pos
        if cidx is None:
            picked = pos
        else:
            picked = jnp.sum(jnp.where(hit, cidx, 0), axis=0, keepdims=True)
        vals = jnp.where(kid == t, m, vals)
        idxs = jnp.where(kid == t, picked, idxs)
        work = jnp.where(hit, -jnp.inf, work)
    return vals, idxs


def _product_candidates(v1, i1, v2, i2, k):
    assert k == 2 * SUBLANES
    sub = lax.broadcasted_iota(jnp.int32, (SUBLANES, v1.shape[1]), 0)
    vals, idxs = [], []
    for a in range(k // 2):
        nb = k // (a + 1)
        width = k if nb > SUBLANES else SUBLANES
        cv = v1[a:a + 1, :] + v2[:width, :]
        ci = i1[a:a + 1, :] * PEER_N_KEYS + i2[:width, :]
        if nb < SUBLANES:
            cv = jnp.where(sub < nb, cv, -jnp.inf)
        vals.append(cv)
        idxs.append(ci)
    vals.append(v1[k // 2:, :] + v2[0:1, :])
    idxs.append(i1[k // 2:, :] * PEER_N_KEYS + i2[0:1, :])
    return jnp.concatenate(vals, axis=0), jnp.concatenate(idxs, axis=0)


def _peer_route_kernel(q_ref, keys_ref, e_ref, g_ref):
    tn = q_ref.shape[0]
    k = PEER_TOPK
    for h in range(PEER_HEADS):
        sub = []
        for p in range(2):
            c0 = (h * 2 + p) * PEER_N_KEYS
            s = _dot_nt(keys_ref[h, p], q_ref[:, c0:c0 + PEER_N_KEYS], HIGHEST)
            sub.append(_topk_rows(s, None, k))
        (v1, i1), (v2, i2) = sub
        cand, cidx = _product_candidates(v1, i1, v2, i2, k)
        best, eidx = _topk_rows(cand, cidx, k)
        ex = jnp.exp(best - jnp.max(best, axis=0, keepdims=True))
        gate = ex / jnp.sum(ex, axis=0, keepdims=True)
        e_ref[h * k:(h + 1) * k, :] = eidx
        g_ref[h * k:(h + 1) * k, :] = gate


HALF_KEYS = PEER_N_KEYS // 2
BF16_HIGH_MASK = 0xFFFF0000


def _peer_gate_kernel(et_ref, gt_ref, o_ref, e_scr, g_scr):
    tg = et_ref.shape[1]
    e_scr[...] = et_ref[...].T
    g_scr[...] = gt_ref[...].T
    kid = lax.broadcasted_iota(jnp.int32, (PEER_N_KEYS, PEER_N_KEYS), 0)

    def body(n8, carry):
        rows = pl.ds(pl.multiple_of(n8 * SUBLANES, SUBLANES), SUBLANES)
        e8 = e_scr[rows, :]
        g8 = g_scr[rows, :]
        a8 = jnp.right_shift(e8, 7)
        b8 = jnp.bitwise_and(e8, PEER_N_KEYS - 1)
        words = []
        for s in range(SUBLANES):
            at = jnp.where(a8[s:s + 1, :] == kid, g8[s:s + 1, :], 0.0).astype(BF16)
            bt = jnp.where(b8[s:s + 1, :] == kid, 1.0, 0.0).astype(BF16)
            grid = _dot_nt(at, bt)
            bits = pltpu.bitcast(grid.astype(BF16).astype(F32), jnp.uint32)
            words.append(jnp.bitwise_or(jnp.bitwise_and(bits[HALF_KEYS:, :], jnp.uint32(BF16_HIGH_MASK)),
                                        jnp.right_shift(bits[:HALF_KEYS, :], jnp.uint32(16))))
        o_ref[:, rows, :] = jnp.swapaxes(jnp.stack(words, axis=0), 0, 1)
        return carry

    lax.fori_loop(0, tg // SUBLANES, body, 0, unroll=4)


def _peer_route(qp, sub_keys, *, tn):
    n = qp.shape[0]
    rows = PEER_HEADS * PEER_TOPK
    e_t, g_t = pl.pallas_call(
        _peer_route_kernel,
        out_shape=(jax.ShapeDtypeStruct((rows, n), jnp.int32), jax.ShapeDtypeStruct((rows, n), F32)),
        grid=(n // tn,),
        in_specs=[
            pl.BlockSpec((tn, qp.shape[1]), lambda i: (i, 0)),
            pl.BlockSpec(sub_keys.shape, lambda i: (0, 0, 0, 0)),
        ],
        out_specs=(pl.BlockSpec((rows, tn), lambda i: (0, i)), pl.BlockSpec((rows, tn), lambda i: (0, i))),
        compiler_params=_cparams(("parallel",)),
        name="peer_route",
    )(qp, sub_keys)
    return pl.pallas_call(
        _peer_gate_kernel,
        out_shape=jax.ShapeDtypeStruct((HALF_KEYS, n, PEER_N_KEYS), jnp.uint32),
        grid=(n // tn,),
        in_specs=[pl.BlockSpec((rows, tn), lambda i: (0, i)), pl.BlockSpec((rows, tn), lambda i: (0, i))],
        out_specs=pl.BlockSpec((HALF_KEYS, tn, PEER_N_KEYS), lambda i: (0, i, 0)),
        scratch_shapes=[pltpu.VMEM((tn, rows), jnp.int32), pltpu.VMEM((tn, rows), F32)],
        compiler_params=_cparams(("parallel",)),
        name="peer_gates",
    )(e_t, g_t)


def _gelu_tanh(x):
    return 0.5 * x * (1.0 + jnp.tanh(math.sqrt(2.0 / math.pi) * (x + 0.044715 * (x * x * x))))


PEER_BLOCKS_PER_STEP = 4


def _peer_expert_kernel(x_ref, gn_ref, gpa_ref, gpb_ref, *rest):
    nblk = PEER_BLOCKS_PER_STEP
    u_refs, v_refs = rest[:nblk], rest[nblk:2 * nblk]
    gf_ref, o_ref, h_scr, acc_scr, coef_scr = rest[2 * nblk:]
    s = pl.program_id(1)
    last = pl.num_programs(1) - 1

    @pl.when(s == 0)
    def _():
        h_scr[...] = _rms(x_ref[...], gn_ref[...]).astype(BF16)
        acc_scr[...] = jnp.zeros(acc_scr.shape, F32)
        coef_scr[...] = jnp.zeros(coef_scr.shape, BF16)

    v_cat = jnp.concatenate([v[...] for v in v_refs], axis=0)
    acc_scr[...] += _dot(coef_scr[(s + 1) % 2], v_cat)

    def unpack(word):
        return (pltpu.bitcast(jnp.left_shift(word, jnp.uint32(16)), F32),
                pltpu.bitcast(jnp.bitwise_and(word, jnp.uint32(BF16_HIGH_MASK)), F32))

    a_lo, a_hi = unpack(gpa_ref[...])
    b_lo, b_hi = unpack(gpb_ref[...])
    u_cat = jnp.concatenate([u[...] for u in u_refs], axis=0)
    hid = _gelu_tanh(_dot_nt(h_scr[...], u_cat))
    coef_scr[s % 2] = (jnp.concatenate([a_lo, b_lo, a_hi, b_hi], axis=1) * hid).astype(BF16)

    @pl.when(s == last)
    def _():
        o_ref[...] = _rms(x_ref[...] + acc_scr[...], gf_ref[...])


def _peer_experts(x, norm_g, gates, u_tab, v_tab, final_g, *, tn):
    n, d = x.shape
    eb = PEER_N_KEYS
    assert u_tab.shape[0] == eb * eb
    nblk = PEER_BLOCKS_PER_STEP
    ns = eb // nblk

    def cur(s):
        return jnp.minimum(s, ns - 1)

    def prev(s):
        return jnp.maximum(s - 1, 0)

    def table(which, q):
        return pl.BlockSpec((eb, d), lambda i, s: (which(s) + q * ns, 0))

    return pl.pallas_call(
        _peer_expert_kernel,
        out_shape=jax.ShapeDtypeStruct((n, d), F32),
        grid=(n // tn, ns + 1),
        in_specs=[
            pl.BlockSpec((tn, d), lambda i, s: (i, 0)),
            pl.BlockSpec((1, d), lambda i, s: (0, 0)),
            pl.BlockSpec((None, tn, eb), lambda i, s: (cur(s), i, 0)),
            pl.BlockSpec((None, tn, eb), lambda i, s: (cur(s) + ns, i, 0)),
        ] + [table(cur, q) for q in range(nblk)] + [table(prev, q) for q in range(nblk)] + [
            pl.BlockSpec((1, d), lambda i, s: (0, 0)),
        ],
        out_specs=pl.BlockSpec((tn, d), lambda i, s: (i, 0)),
        scratch_shapes=[pltpu.VMEM((tn, d), BF16), pltpu.VMEM((tn, d), F32),
                        pltpu.VMEM((2, tn, nblk * eb), BF16)],
        compiler_params=_cparams(("parallel", "arbitrary")),
        name="peer_experts",
    )(x, norm_g.reshape(1, d), gates, gates, *([u_tab] * nblk), *([v_tab] * nblk), final_g.reshape(1, d))


def _reorder_w_in(w_in):
    c_qkv = 3 * GDN_WIDTH
    c_z = c_qkv + GDN_WIDTH
    c_ba = c_z + 2 * GDN_HEADS
    rest = w_in[:, c_ba:]
    pad = jnp.zeros((w_in.shape[0], IN_COLS_PADDED - COL_BA - 2 * GDN_HEADS), w_in.dtype)
    return jnp.concatenate([w_in[:, :c_z], rest, w_in[:, c_z:c_ba], pad], axis=1).astype(BF16)


def _layer(x, wts, layer, mem_k, mem_v, conv_prev, gdn_state, k_past, v_past, *, tm):
    bsz, t, d = x.shape
    n = bsz * t
    x2d = x.reshape(n, d)
    proj = _matmul(x2d, wts["w_in"], norm_g=wts["norm_mix_g"], tm=min(IN_TM, n), tn=IN_TN, name="in_proj")
    conv_dim = 3 * GDN_WIDTH
    conv_prev8 = jnp.concatenate(
        [jnp.zeros((bsz, SUBLANES - (GDN_CONV - 1), conv_dim), F32), conv_prev.astype(F32)], axis=1)
    o_a, s_new = _gdn(proj, conv_prev8, gdn_state.astype(F32), wts["gdn_conv_w"], wts["gdn_a_log"],
                      wts["gdn_dt_bias"], wts["gdn_norm_g"], bsz=bsz, t=t)
    conv_new = proj.reshape(bsz, t, -1)[:, t - (GDN_CONV - 1):, COL_QKV:COL_QKV + conv_dim]
    lam_init = 0.8 - 0.6 * math.exp(-0.3 * layer)
    lams = tuple(wts[k].reshape(1, -1) for k in ("diff_lambda_q1", "diff_lambda_k1", "diff_lambda_q2", "diff_lambda_k2"))
    if k_past is None:
        o_b = _diff_prompt(proj, lams, wts["diff_subln_g"], bsz=bsz, t=t, lam_init=lam_init)
    else:
        o_b = _diff_sample(proj, k_past, v_past, lams, wts["diff_subln_g"], bsz=bsz, t=t, lam_init=lam_init)
    k_rows = proj[:, COL_DK:COL_DK + DIFF_HEADS * HEAD_DIM].reshape(bsz, t, DIFF_HEADS, HEAD_DIM)
    v_rows = proj[:, COL_DV:COL_DV + DIFF_HEADS * HEAD_DIM].reshape(bsz, t, DIFF_HEADS, HEAD_DIM)
    merged = _merge(o_a, o_b, wts["w_branch_a"], wts["w_branch_b"], proj, tm=tm)
    x1 = _matmul(merged, wts["w_out"], residual=x2d, tm=tm, tn=512, name="out_proj")
    qm = _matmul(x1, wts["w_mq"], norm_g=wts["norm_cross_g"], tm=tm, tn=512, out_dtype=BF16, name="mem_q")
    mt = mem_k.shape[1]
    oc = _cross_attend(qm, mem_k.reshape(bsz, mt, d), mem_v.reshape(bsz, mt, d), bsz=bsz, t=t, tm=min(tm, t))
    x2 = _matmul(oc, wts["w_mo"], residual=x1, tm=tm, tn=512, name="mem_o")
    qp = _matmul(x2, wts["peer_w_q"], norm_g=wts["norm_ffn_g"], tm=tm, tn=512, name="peer_q")
    tn = min(512, n)
    gates = _peer_route(qp, wts["peer_sub_keys"], tn=LANES)
    y = _peer_experts(x2, wts["norm_ffn_g"], gates, wts["peer_u"], wts["peer_v"], wts["final_norm_g"], tn=tn)
    return y.reshape(bsz, t, d), k_rows, v_rows, s_new, conv_new


def kernel(x_prompt, x_sample, cache_diff_k, cache_diff_v, state_gdn, state_conv, cache_mem_k, cache_mem_v,
           mem_prompt, norm_mix_g, w_in, gdn_conv_w, gdn_a_log, gdn_dt_bias, gdn_norm_g,
           diff_lambda_q1, diff_lambda_k1, diff_lambda_q2, diff_lambda_k2, diff_subln_g,
           w_branch_a, w_branch_b, w_out, norm_cross_g, norm_mem_g, w_mq, w_mk, w_mv, w_mo,
           norm_ffn_g, peer_w_q, peer_sub_keys, peer_u, peer_v, final_norm_g):
    depth = w_in.shape[0]
    assert depth == 1, "final norm is fused into the last layer's PEER kernel"
    l = 0
    wts = {
        "norm_mix_g": norm_mix_g[l], "w_in": _reorder_w_in(w_in[l]), "gdn_conv_w": gdn_conv_w[l],
        "gdn_a_log": gdn_a_log[l], "gdn_dt_bias": gdn_dt_bias[l], "gdn_norm_g": gdn_norm_g[l],
        "diff_lambda_q1": diff_lambda_q1[l], "diff_lambda_k1": diff_lambda_k1[l],
        "diff_lambda_q2": diff_lambda_q2[l], "diff_lambda_k2": diff_lambda_k2[l],
        "diff_subln_g": diff_subln_g[l], "w_branch_a": w_branch_a[l].astype(BF16),
        "w_branch_b": w_branch_b[l].astype(BF16), "w_out": w_out[l].astype(BF16),
        "norm_cross_g": norm_cross_g[l], "w_mq": w_mq[l].astype(BF16), "w_mo": w_mo[l].astype(BF16),
        "norm_ffn_g": norm_ffn_g[l], "peer_w_q": peer_w_q[l].astype(BF16), "peer_sub_keys": peer_sub_keys[l],
        "peer_u": peer_u[l].astype(BF16), "peer_v": peer_v[l].astype(BF16), "final_norm_g": final_norm_g,
    }
    bp, tp, d = x_prompt.shape
    bs, ts, _ = x_sample.shape
    mem2d = mem_prompt.reshape(-1, d)
    tmm = min(512, mem2d.shape[0])
    mem_k = _matmul(mem2d, w_mk[l].astype(BF16), norm_g=norm_mem_g[l], tm=tmm, tn=512, name="mem_k")
    mem_v = _matmul(mem2d, w_mv[l].astype(BF16), norm_g=norm_mem_g[l], tm=tmm, tn=512, name="mem_v")
    mshape = mem_prompt.shape[:2] + (MEM_HEADS, d // MEM_HEADS)
    mem_k = mem_k.reshape(mshape)
    mem_v = mem_v.reshape(mshape)
    conv0 = jnp.zeros((bp, GDN_CONV - 1, 3 * GDN_WIDTH), F32)
    s0 = jnp.zeros((bp, GDN_HEADS, HEAD_DIM, HEAD_DIM), F32)
    yp, pk, pv, ps, pc = _layer(x_prompt, wts, l, mem_k, mem_v, conv0, s0, None, None, tm=min(1024, bp * tp))
    ys, sk, sv, ss, sc = _layer(x_sample, wts, l, cache_mem_k[l], cache_mem_v[l], state_conv[l], state_gdn[l],
                                cache_diff_k[l], cache_diff_v[l], tm=min(256, bs * ts))
    return (yp, ys, pk[None], pv[None], ps[None], pc[None], mem_k[None], mem_v[None],
            sk[None], sv[None], ss[None], sc[None])
```

```python
import functools
import math

import jax
import jax.numpy as jnp
from jax import lax
from jax.experimental import pallas as pl
from jax.experimental.pallas import tpu as pltpu

F32 = jnp.float32
BF16 = jnp.bfloat16
HIGHEST = lax.Precision.HIGHEST

RMS_EPS = 1e-6
CHUNK = 64
CHUNK_SHIFT = 6
DIFF_TQ = 512
DIFF_SUB = 256
GDN_HEADS = 8
HEAD_DIM = 128
GDN_WIDTH = GDN_HEADS * HEAD_DIM
GDN_CONV = 4
DIFF_HEADS = 8
DIFF_HEAD_DIM = 64
MEM_HEADS = 4
PEER_HEADS = 8
PEER_N_KEYS = 128
PEER_TOPK = 16
LANES = 128
SUBLANES = 8
V7X_VMEM_LIMIT = 52 * 1024 * 1024


def _cparams(sem, vmem=V7X_VMEM_LIMIT):
    return pltpu.CompilerParams(dimension_semantics=sem, vmem_limit_bytes=vmem)


def _dot(a, b, precision=None):
    return jnp.dot(a, b, preferred_element_type=F32, precision=precision)


def _dot_nt(a, b, precision=None):
    return lax.dot_general(a, b, (((1,), (1,)), ((), ())), preferred_element_type=F32, precision=precision)


def _split_bf16(x):
    hi = x.astype(BF16)
    return hi, (x - hi.astype(F32)).astype(BF16)


def _dot3_impl(dot, a, b):
    a_hi, a_lo = _split_bf16(a)
    b_hi, b_lo = _split_bf16(b)
    m = a.shape[0]
    both = dot(jnp.concatenate([a_hi, a_lo], axis=0), b_hi)
    return both[:m] + both[m:] + dot(a_hi, b_lo)


def _dot3(a, b):
    return _dot3_impl(_dot, a, b)


def _dot3_nt(a, b):
    return _dot3_impl(_dot_nt, a, b)


def _rms(xf, g):
    return xf * lax.rsqrt(jnp.mean(xf * xf, axis=-1, keepdims=True) + RMS_EPS) * g


def _mm_kernel(*refs, has_norm, has_res):
    it = iter(refs)
    x_ref = next(it)
    g_ref = next(it) if has_norm else None
    w_ref = next(it)
    r_ref = next(it) if has_res else None
    o_ref = next(it)
    h_scr = next(it)

    @pl.when(pl.program_id(1) == 0)
    def _():
        xf = x_ref[...].astype(F32)
        if has_norm:
            xf = _rms(xf, g_ref[...])
        h_scr[...] = xf.astype(BF16)

    acc = _dot(h_scr[...], w_ref[...])
    if has_res:
        acc = acc + r_ref[...]
    o_ref[...] = acc.astype(o_ref.dtype)


def _matmul(x, w, *, norm_g=None, residual=None, tm, tn, out_dtype=F32, name="matmul"):
    m, k = x.shape
    n = w.shape[1]
    assert m % tm == 0 and n % tn == 0, (m, n, tm, tn)
    in_specs = [pl.BlockSpec((tm, k), lambda i, j: (i, 0))]
    args = [x]
    if norm_g is not None:
        in_specs.append(pl.BlockSpec((1, k), lambda i, j: (0, 0)))
        args.append(norm_g.reshape(1, k).astype(F32))
    in_specs.append(pl.BlockSpec((k, tn), lambda i, j: (0, j)))
    args.append(w)
    if residual is not None:
        in_specs.append(pl.BlockSpec((tm, tn), lambda i, j: (i, j)))
        args.append(residual)
    return pl.pallas_call(
        functools.partial(_mm_kernel, has_norm=norm_g is not None, has_res=residual is not None),
        out_shape=jax.ShapeDtypeStruct((m, n), out_dtype),
        grid=(m // tm, n // tn),
        in_specs=in_specs,
        out_specs=pl.BlockSpec((tm, tn), lambda i, j: (i, j)),
        scratch_shapes=[pltpu.VMEM((tm, k), BF16)],
        compiler_params=_cparams(("parallel", "arbitrary")),
        name=name,
    )(*args)


IN_TN = 512
IN_TM = 1024
IN_W1_TILES = 8
IN_W2_TILES = 14
IN_MAIN_TILES = 10
IN_K_TILE0 = 10
IN_V_TILE0 = 12
IN_TAIL_TILE0 = 14
IN_TILES = IN_W1_TILES + IN_W2_TILES + 1
MAIN_COLS = IN_MAIN_TILES * IN_TN
TAIL_COLS = (IN_TILES - IN_TAIL_TILE0) * IN_TN
COL_DQ_MAIN = 4096
COL_GB_TAIL = 2048
COL_BA_TAIL = 4096


def _in_proj_kernel(x_ref, g_ref, w1_ref, w2_ref, w3_ref, main_ref, k_ref, v_ref, tail_ref, h_scr):
    j = pl.program_id(1)

    @pl.when(j == 0)
    def _():
        h_scr[...] = _rms(x_ref[...], g_ref[...]).astype(BF16)

    routes = ((0, IN_W1_TILES, w1_ref, main_ref),
              (IN_W1_TILES, IN_K_TILE0, w2_ref, main_ref),
              (IN_K_TILE0, IN_V_TILE0, w2_ref, k_ref),
              (IN_V_TILE0, IN_TAIL_TILE0, w2_ref, v_ref),
              (IN_TAIL_TILE0, IN_TILES - 1, w2_ref, tail_ref),
              (IN_TILES - 1, IN_TILES, w3_ref, tail_ref))
    for lo, hi, w_ref, dst in routes:
        @pl.when(jnp.logical_and(j >= lo, j < hi))
        def _(w_ref=w_ref, dst=dst):
            dst[...] = _dot(h_scr[...], w_ref[...])


def _in_proj_weights(w_in):
    c_z = 4 * GDN_WIDTH
    c_ba = c_z + 2 * GDN_HEADS
    assert c_z == IN_W1_TILES * IN_TN and w_in.shape[1] - c_ba == IN_W2_TILES * IN_TN
    w3 = jnp.pad(w_in[:, c_z:c_ba], ((0, 0), (0, IN_TN - 2 * GDN_HEADS)))
    return w_in[:, :c_z].astype(BF16), w_in[:, c_ba:].astype(BF16), w3.astype(BF16)


def _in_proj(x, norm_g, w_pieces, *, tm):
    m, d = x.shape
    w1, w2, w3 = w_pieces
    tn = IN_TN
    out = lambda cols: jax.ShapeDtypeStruct((m, cols), F32)
    return pl.pallas_call(
        _in_proj_kernel,
        out_shape=(out(MAIN_COLS), out(2 * IN_TN), out(2 * IN_TN), out(TAIL_COLS)),
        grid=(m // tm, IN_TILES),
        in_specs=[
            pl.BlockSpec((tm, d), lambda i, j: (i, 0)),
            pl.BlockSpec((1, d), lambda i, j: (0, 0)),
            pl.BlockSpec((d, tn), lambda i, j: (0, jnp.minimum(j, IN_W1_TILES - 1))),
            pl.BlockSpec((d, tn), lambda i, j: (0, jnp.clip(j - IN_W1_TILES, 0, IN_W2_TILES - 1))),
            pl.BlockSpec((d, tn), lambda i, j: (0, 0)),
        ],
        out_specs=(
            pl.BlockSpec((tm, tn), lambda i, j: (i, jnp.minimum(j, IN_MAIN_TILES - 1))),
            pl.BlockSpec((tm, tn), lambda i, j: (i, jnp.clip(j - IN_K_TILE0, 0, 1))),
            pl.BlockSpec((tm, tn), lambda i, j: (i, jnp.clip(j - IN_V_TILE0, 0, 1))),
            pl.BlockSpec((tm, tn), lambda i, j: (i, jnp.clip(j - IN_TAIL_TILE0, 0, IN_TILES - IN_TAIL_TILE0 - 1))),
        ),
        scratch_shapes=[pltpu.VMEM((tm, d), BF16)],
        compiler_params=_cparams(("parallel", "arbitrary")),
        name="in_proj",
    )(x, norm_g.reshape(1, d).astype(F32), w1, w2, w3)


def _gdn_kernel(qkv_ref, z_ref, ba_ref, prev_ref, s0_ref, convw_ref, alog_ref, dtb_ref, ng_ref,
                o_ref, sfin_ref, s_scr, xbuf, *, L):
    c = pl.program_id(1)

    @pl.when(c == 0)
    def _():
        s_scr[...] = s0_ref[0]
        xbuf[0:SUBLANES, :] = prev_ref[0]

    xbuf[SUBLANES:SUBLANES + L, :] = qkv_ref[...]
    base = SUBLANES - (GDN_CONV - 1)
    y = xbuf[base:base + L, :] * convw_ref[0:1, :]
    for i in range(1, GDN_CONV):
        y = y + xbuf[base + i:base + i + L, :] * convw_ref[i:i + 1, :]
    y = y * jax.nn.sigmoid(y)
    tail = xbuf[L:L + SUBLANES, :]
    xbuf[0:SUBLANES, :] = tail

    row = lax.broadcasted_iota(jnp.int32, (L, L), 0)
    col = lax.broadcasted_iota(jnp.int32, (L, L), 1)
    lower = row >= col
    strict = row > col
    eye = (row == col).astype(F32)
    n_double = int(math.log2(L)) - 1
    assert 2 ** (n_double + 1) == L

    ba = ba_ref[...]
    beta_all = jax.nn.sigmoid(ba)
    a_in = ba + dtb_ref[...]
    softplus = jnp.maximum(a_in, 0.0) + jnp.log(1.0 + jnp.exp(-jnp.abs(a_in)))
    g_all = -jnp.exp(alog_ref[...]) * softplus
    g_cum = _dot(lower.astype(F32), g_all, HIGHEST)
    g_cum_t = g_cum.T

    heads = range(GDN_HEADS)
    cat0 = functools.partial(jnp.concatenate, axis=0)

    def head_cols(part, h):
        return y[:, part * GDN_WIDTH + h * HEAD_DIM:part * GDN_WIDTH + (h + 1) * HEAD_DIM]

    q = [head_cols(0, h) for h in heads]
    k = [head_cols(1, h) for h in heads]
    v = [head_cols(2, h) for h in heads]
    q = [a * lax.rsqrt(jnp.sum(a * a, axis=-1, keepdims=True) + 1e-6) * (HEAD_DIM ** -0.5) for a in q]
    k = [a * lax.rsqrt(jnp.sum(a * a, axis=-1, keepdims=True) + 1e-6) for a in k]
    beta = [beta_all[:, h:h + 1] for h in heads]
    g_col = [g_cum[:, GDN_HEADS + h:GDN_HEADS + h + 1] for h in heads]
    g_row = [g_cum_t[GDN_HEADS + h:GDN_HEADS + h + 1, :] for h in heads]
    decay = [jnp.where(lower, jnp.exp(jnp.where(lower, g_col[h] - g_row[h], 0.0)), 0.0) for h in heads]
    g_last = [g_col[h][L - 1:L, :] for h in heads]
    e_col = [jnp.exp(g_col[h]) for h in heads]
    qk_kk = [_dot3_nt(cat0([q[h], k[h]]), k[h]) for h in heads]
    x = [-jnp.where(strict, beta[h] * qk_kk[h][L:] * decay[h], 0.0) for h in heads]
    t_inv = [eye + x[h] for h in heads]
    xp = [_dot3(x[h], x[h]) for h in heads]
    for _ in range(n_double - 1):
        both = [_dot3(cat0([t_inv[h], xp[h]]), xp[h]) for h in heads]
        t_inv = [t_inv[h] + both[h][:L] for h in heads]
        xp = [both[h][L:] for h in heads]
    t_inv = [t_inv[h] + _dot3(t_inv[h], xp[h]) for h in heads]
    sol = [_dot3(t_inv[h], jnp.concatenate([v[h] * beta[h], k[h] * (beta[h] * e_col[h])], axis=-1)) for h in heads]
    qk = [jnp.where(lower, qk_kk[h][:L] * decay[h], 0.0) for h in heads]
    k_dec_t = [(k[h] * jnp.exp(g_last[h] - g_col[h])).T for h in heads]
    s = [s_scr[h] for h in heads]
    ws_qs = [_dot3(cat0([sol[h][:, HEAD_DIM:], q[h] * e_col[h]]), s[h]) for h in heads]
    v_new = [sol[h][:, :HEAD_DIM] - ws_qs[h][:L] for h in heads]
    tail = [_dot3(cat0([qk[h], k_dec_t[h]]), v_new[h]) for h in heads]
    for h in heads:
        s_scr[h] = s[h] * jnp.exp(g_last[h]) + tail[h][L:]
        zf = z_ref[:, h * HEAD_DIM:(h + 1) * HEAD_DIM]
        o = _rms(ws_qs[h][L:] + tail[h][:L], ng_ref[...]) * (zf * jax.nn.sigmoid(zf))
        o_ref[:, h * HEAD_DIM:(h + 1) * HEAD_DIM] = o.astype(o_ref.dtype)

    @pl.when(c == pl.num_programs(1) - 1)
    def _():
        sfin_ref[0] = s_scr[...]


def _decay_lanes(p):
    return jnp.zeros((1, LANES), F32).at[0, GDN_HEADS:2 * GDN_HEADS].set(p.astype(F32))


def _gdn(main, tail, conv_prev8, s0, conv_w, a_log, dt_bias, norm_g, *, bsz, t):
    L = min(CHUNK, t)
    nc = t // L
    conv_dim = 3 * GDN_WIDTH
    kern = functools.partial(_gdn_kernel, L=L)
    o, s_fin = pl.pallas_call(
        kern,
        out_shape=(jax.ShapeDtypeStruct((bsz * t, GDN_WIDTH), BF16),
                   jax.ShapeDtypeStruct((bsz, GDN_HEADS, HEAD_DIM, HEAD_DIM), F32)),
        grid=(bsz, nc),
        in_specs=[
            pl.BlockSpec((L, conv_dim), lambda b, c: (b * nc + c, 0)),
            pl.BlockSpec((L, GDN_WIDTH), lambda b, c: (b * nc + c, conv_dim // GDN_WIDTH)),
            pl.BlockSpec((L, LANES), lambda b, c: (b * nc + c, COL_BA_TAIL // LANES)),
            pl.BlockSpec((1, SUBLANES, conv_dim), lambda b, c: (b, 0, 0)),
            pl.BlockSpec((1, GDN_HEADS, HEAD_DIM, HEAD_DIM), lambda b, c: (b, 0, 0, 0)),
            pl.BlockSpec((GDN_CONV, conv_dim), lambda b, c: (0, 0)),
            pl.BlockSpec((1, LANES), lambda b, c: (0, 0)),
            pl.BlockSpec((1, LANES), lambda b, c: (0, 0)),
            pl.BlockSpec((1, HEAD_DIM), lambda b, c: (0, 0)),
        ],
        out_specs=(pl.BlockSpec((L, GDN_WIDTH), lambda b, c: (b * nc + c, 0)),
                   pl.BlockSpec((1, GDN_HEADS, HEAD_DIM, HEAD_DIM), lambda b, c: (b, 0, 0, 0))),
        scratch_shapes=[pltpu.VMEM((GDN_HEADS, HEAD_DIM, HEAD_DIM), F32),
                        pltpu.VMEM((SUBLANES + L + SUBLANES, conv_dim), F32)],
        compiler_params=_cparams(("parallel", "arbitrary")),
        name="gdn",
    )(main, main, tail, conv_prev8, s0, conv_w, _decay_lanes(a_log), _decay_lanes(dt_bias),
      norm_g.reshape(1, -1))
    return o, s_fin


def _lambda_value(lam_refs, lam_init):
    lq1, lk1, lq2, lk2 = (r[...] for r in lam_refs)
    return (jnp.exp(jnp.sum(lq1 * lk1, axis=-1, keepdims=True))
            - jnp.exp(jnp.sum(lq2 * lk2, axis=-1, keepdims=True)) + lam_init)


def _softmax_update(m_scr, l_scr, acc_scr, scores, values):
    idx = range(len(scores))
    m_old = [m_scr[n] for n in idx]
    m_new = [jnp.maximum(m_old[n], jnp.max(scores[n], axis=-1, keepdims=True)) for n in idx]
    p = [jnp.exp(scores[n] - m_new[n]) for n in idx]
    alpha = [jnp.exp(m_old[n] - m_new[n]) for n in idx]
    pv = [_dot(p[n].astype(BF16), values[n]) for n in idx]
    for n in idx:
        l_scr[n] = alpha[n] * l_scr[n] + jnp.sum(p[n], axis=-1, keepdims=True)
        acc_scr[n] = alpha[n] * acc_scr[n] + pv[n]
        m_scr[n] = m_new[n]


def _diff_finalize(o1, o2, lam_refs, g_ref, lam_init):
    lam = _lambda_value(lam_refs, lam_init)
    return _rms(o1 - lam * o2, g_ref[...]) * (1.0 - lam_init)


def _diff_init(m_scr, l_scr, acc_scr):
    m_scr[...] = jnp.full(m_scr.shape, -jnp.inf, F32)
    l_scr[...] = jnp.zeros(l_scr.shape, F32)
    acc_scr[...] = jnp.zeros(acc_scr.shape, F32)


def _near_bias(slope, r, q_pos, k_pos):
    visible = jnp.right_shift(k_pos, CHUNK_SHIFT) <= jnp.right_shift(q_pos, CHUNK_SHIFT)
    return jnp.where(visible, slope * (r - jnp.abs(q_pos - k_pos)).astype(F32), -jnp.inf)


def _split_distance(slope, d):
    return (-(slope * CHUNK) * jnp.right_shift(d, CHUNK_SHIFT).astype(F32),
            -slope * jnp.bitwise_and(d, CHUNK - 1).astype(F32))


def _diff_prompt_kernel(i_tab, j_tab, slopes_ref, q_ref, k_ref, v_ref, lq1, lk1, lq2, lk2, g_ref, o_ref,
                        m_scr, l_scr, acc_scr, lhs_scr, s_even, s_odd, *, tq, lam_init):
    h = pl.program_id(1)
    s_id = pl.program_id(2)
    n_pairs = pl.num_programs(2) - 1
    score_on = s_id < n_pairs
    consume_on = s_id >= 1
    i = i_tab[s_id]
    j = j_tab[s_id]
    ci = i_tab[jnp.maximum(s_id - 1, 0)]
    cj = j_tab[jnp.maximum(s_id - 1, 0)]
    slope = slopes_ref[h]
    half = DIFF_HEAD_DIM
    lane = lax.broadcasted_iota(jnp.int32, (tq, LANES), 1)

    @pl.when(jnp.logical_and(consume_on, cj == 0))
    def _():
        _diff_init(m_scr, l_scr, acc_scr)

    @pl.when(jnp.logical_and(score_on, j == 0))
    def _():
        qs = q_ref[...] * (DIFF_HEAD_DIM ** -0.5)
        lhs_scr[0] = jnp.where(lane < half, qs, 1.0).astype(BF16)
        lhs_scr[1] = jnp.where(lane >= half, qs, 1.0).astype(BF16)

    sub = min(DIFF_SUB, tq)
    units = [(n, qb) for qb in range(tq // sub) for n in range(2)]

    def score_tile(near, dst):
        kb = k_ref[...]
        if near:
            aug0 = aug1 = jnp.zeros_like(kb)
        else:
            d = (i - j) * tq - lax.broadcasted_iota(jnp.int32, (tq, LANES), 0)
            t_hi, t_lo = _split_distance(slope, d)
            aug0 = jnp.where(lane == half, t_hi, jnp.where(lane == half + 1, t_lo, 0.0))
            aug1 = jnp.where(lane == 0, t_hi, jnp.where(lane == 1, t_lo, 0.0))
        keys = [jnp.where(lane < half, kb, aug0).astype(BF16), jnp.where(lane >= half, kb, aug1).astype(BF16)]
        bias = None
        for n, qb in units:
            cols = slice(qb * sub, (qb + 1) * sub)
            s = _dot_nt(keys[n], lhs_scr[n, cols, :])
            if near:
                if n == 0:
                    k_pos = lax.broadcasted_iota(jnp.int32, (tq, sub), 0)
                    q_pos = qb * sub + lax.broadcasted_iota(jnp.int32, (tq, sub), 1)
                    bias = _near_bias(slope, q_pos, q_pos, k_pos)
                s = s + bias
            dst[n, :, cols] = s

    def consume_tile(src):
        vt = v_ref[...].T.astype(BF16)
        idx = range(len(units))
        cols = [slice(qb * sub, (qb + 1) * sub) for _, qb in units]
        s = [src[n, :, cols[u]] for u, (n, _) in enumerate(units)]
        m_old = [m_scr[n, :, cols[u]] for u, (n, _) in enumerate(units)]
        m_new = [jnp.maximum(m_old[u], jnp.max(s[u], axis=0, keepdims=True)) for u in idx]
        p = [jnp.exp(s[u] - m_new[u]) for u in idx]
        alpha = [jnp.exp(m_old[u] - m_new[u]) for u in idx]
        pv = [_dot(vt, p[u].astype(BF16)) for u in idx]
        for u, (n, _) in enumerate(units):
            l_scr[n, :, cols[u]] = alpha[u] * l_scr[n, :, cols[u]] + jnp.sum(p[u], axis=0, keepdims=True)
            acc_scr[n, :, cols[u]] = alpha[u] * acc_scr[n, :, cols[u]] + pv[u]
            m_scr[n, :, cols[u]] = m_new[u]

    def stage(cond, near, score, consume):
        for parity, (dst, src) in enumerate(((s_even, s_odd), (s_odd, s_even))):
            @pl.when(jnp.logical_and(cond, s_id % 2 == parity))
            def _(dst=dst, src=src):
                if score:
                    score_tile(near, dst)
                if consume:
                    consume_tile(src)

    both = jnp.logical_and(score_on, consume_on)
    stage(jnp.logical_and(both, j < i), False, True, True)
    stage(jnp.logical_and(both, j == i), True, True, True)
    stage(jnp.logical_not(consume_on), True, True, False)
    stage(jnp.logical_not(score_on), False, False, True)

    @pl.when(jnp.logical_and(consume_on, cj == ci))
    def _():
        o_ref[...] = _diff_finalize((acc_scr[0] / l_scr[0]).T, (acc_scr[1] / l_scr[1]).T,
                                    (lq1, lk1, lq2, lk2), g_ref, lam_init).astype(o_ref.dtype)


def _alibi_slopes():
    return jnp.asarray([2.0 ** (-8.0 * (i + 1) / DIFF_HEADS) for i in range(DIFF_HEADS)], F32)


def _diff_prompt(main, k_new, v_new, lams, subln_g, *, bsz, t, lam_init):
    tq = min(DIFF_TQ, t)
    nq = t // tq
    assert tq % CHUNK == 0 and t <= CHUNK * 255
    kern = functools.partial(_diff_prompt_kernel, tq=tq, lam_init=lam_init)
    qc, kc, vc = COL_DQ_MAIN // LANES, 0, 0
    pairs = [(i, j) for i in range(nq) for j in range(i + 1)]
    pairs.append(pairs[-1])
    i_tab = jnp.asarray([p[0] for p in pairs], jnp.int32)
    j_tab = jnp.asarray([p[1] for p in pairs], jnp.int32)
    zero = lambda b, h, s, it, jt: (0, 0)

    def consumed(tab, s):
        return tab[jnp.maximum(s - 1, 0)]

    grid_spec = pltpu.PrefetchScalarGridSpec(
        num_scalar_prefetch=2,
        grid=(bsz, DIFF_HEADS, len(pairs)),
        in_specs=[
            pl.BlockSpec(memory_space=pltpu.SMEM),
            pl.BlockSpec((tq, LANES), lambda b, h, s, it, jt: (b * nq + it[s], qc + h)),
            pl.BlockSpec((tq, LANES), lambda b, h, s, it, jt: (b * nq + jt[s], kc + h)),
            pl.BlockSpec((tq, LANES), lambda b, h, s, it, jt: (b * nq + consumed(jt, s), vc + h)),
        ] + [pl.BlockSpec((1, DIFF_HEAD_DIM), zero)] * 4 + [pl.BlockSpec((1, HEAD_DIM), zero)],
        out_specs=pl.BlockSpec((tq, LANES), lambda b, h, s, it, jt: (b * nq + consumed(it, s), h)),
        scratch_shapes=[pltpu.VMEM((2, 1, tq), F32), pltpu.VMEM((2, 1, tq), F32),
                        pltpu.VMEM((2, HEAD_DIM, tq), F32), pltpu.VMEM((2, tq, LANES), BF16),
                        pltpu.VMEM((2, tq, tq), F32), pltpu.VMEM((2, tq, tq), F32)],
    )
    return pl.pallas_call(
        kern,
        out_shape=jax.ShapeDtypeStruct((bsz * t, DIFF_HEADS * HEAD_DIM), BF16),
        grid_spec=grid_spec,
        compiler_params=_cparams(("parallel", "parallel", "arbitrary")),
        name="diff_attn_prompt",
    )(i_tab, j_tab, _alibi_slopes(), main, k_new, v_new, *lams, subln_g.reshape(1, -1))


def _diff_sample_kernel(slopes_ref, q_ref, kn_ref, vn_ref, kp_ref, vp_ref, lq1, lk1, lq2, lk2, g_ref, o_ref,
                        m_scr, l_scr, acc_scr, lhs_scr, *, t, tk, past, lam_init):
    j = pl.program_id(1)
    heads = range(DIFF_HEADS)

    def cols(h):
        return slice(h * HEAD_DIM, (h + 1) * HEAD_DIM)

    @pl.when(j == 0)
    def _():
        _diff_init(m_scr, l_scr, acc_scr)
        lane = lax.broadcasted_iota(jnp.int32, (t, LANES), 1)
        r = lax.broadcasted_iota(jnp.int32, (t, t), 0)
        c = lax.broadcasted_iota(jnp.int32, (t, t), 1)
        scores = []
        for h in heads:
            qs = q_ref[:, cols(h)] * (DIFF_HEAD_DIM ** -0.5)
            lhs_scr[h, 0:t, :] = jnp.where(lane < DIFF_HEAD_DIM, qs, 0.0).astype(BF16)
            lhs_scr[h, t:2 * t, :] = jnp.where(lane >= DIFF_HEAD_DIM, qs, 0.0).astype(BF16)
            bias = _near_bias(slopes_ref[h], r, past + r, past + c)
            scores.append(_dot_nt(lhs_scr[h], kn_ref[:, cols(h)].astype(BF16)) + jnp.concatenate([bias, bias], axis=0))
        _softmax_update(m_scr, l_scr, acc_scr, scores, [vn_ref[:, cols(h)].astype(BF16) for h in heads])

    d = (past - j * tk - lax.broadcasted_iota(jnp.int32, (1, tk), 1)).astype(F32)
    kt = jnp.swapaxes(kp_ref[...], 0, 1).astype(BF16)
    vt = jnp.swapaxes(vp_ref[...], 0, 1).astype(BF16)
    scores = [_dot_nt(lhs_scr[h], kt[h]) - slopes_ref[h] * d for h in heads]
    _softmax_update(m_scr, l_scr, acc_scr, scores, [vt[h] for h in heads])

    @pl.when(j == pl.num_programs(1) - 1)
    def _():
        for h in heads:
            o = acc_scr[h] / l_scr[h]
            o_ref[:, cols(h)] = _diff_finalize(o[:t], o[t:], (lq1, lk1, lq2, lk2), g_ref, lam_init).astype(o_ref.dtype)


def _diff_sample(main, k_new, v_new, k_past, v_past, lams, subln_g, *, bsz, t, lam_init):
    past = k_past.shape[1]
    tk = min(512, past)
    nk = past // tk
    width = DIFF_HEADS * HEAD_DIM
    kern = functools.partial(_diff_sample_kernel, t=t, tk=tk, past=past, lam_init=lam_init)
    zero = lambda b, j: (0, 0)
    cache = pl.BlockSpec((None, tk, DIFF_HEADS, HEAD_DIM), lambda b, j: (b, j, 0, 0))
    return pl.pallas_call(
        kern,
        out_shape=jax.ShapeDtypeStruct((bsz * t, width), BF16),
        grid=(bsz, nk),
        in_specs=[
            pl.BlockSpec(memory_space=pltpu.SMEM),
            pl.BlockSpec((t, width), lambda b, j: (b, COL_DQ_MAIN // width)),
            pl.BlockSpec((t, width), lambda b, j: (b, 0)),
            pl.BlockSpec((t, width), lambda b, j: (b, 0)),
            cache, cache,
        ] + [pl.BlockSpec((1, DIFF_HEAD_DIM), zero)] * 4 + [pl.BlockSpec((1, HEAD_DIM), zero)],
        out_specs=pl.BlockSpec((t, width), lambda b, j: (b, 0)),
        scratch_shapes=[pltpu.VMEM((DIFF_HEADS, 2 * t, 1), F32), pltpu.VMEM((DIFF_HEADS, 2 * t, 1), F32),
                        pltpu.VMEM((DIFF_HEADS, 2 * t, HEAD_DIM), F32), pltpu.VMEM((DIFF_HEADS, 2 * t, LANES), BF16)],
        compiler_params=_cparams(("parallel", "arbitrary")),
        name="diff_attn_sample",
    )(_alibi_slopes(), main, k_new, v_new, k_past, v_past, *lams, subln_g.reshape(1, -1))


def _merge_kernel(oa_ref, ob_ref, wa_ref, wb_ref, ga_ref, gb_ref, o_ref):
    ya = _dot(oa_ref[...], wa_ref[...])
    yb = _dot(ob_ref[...], wb_ref[...])
    o_ref[...] = (jax.nn.sigmoid(ga_ref[...]) * ya + jax.nn.sigmoid(gb_ref[...]) * yb).astype(o_ref.dtype)


def _merge(o_a, o_b, wa, wb, tail, *, tm):
    m = o_a.shape[0]
    d = wa.shape[1]
    tn = 512
    ga0, gb0 = 0, COL_GB_TAIL // tn
    return pl.pallas_call(
        _merge_kernel,
        out_shape=jax.ShapeDtypeStruct((m, d), BF16),
        grid=(m // tm, d // tn),
        in_specs=[
            pl.BlockSpec((tm, o_a.shape[1]), lambda i, j: (i, 0)),
            pl.BlockSpec((tm, o_b.shape[1]), lambda i, j: (i, 0)),
            pl.BlockSpec((wa.shape[0], tn), lambda i, j: (0, j)),
            pl.BlockSpec((wb.shape[0], tn), lambda i, j: (0, j)),
            pl.BlockSpec((tm, tn), lambda i, j: (i, ga0 + j)),
            pl.BlockSpec((tm, tn), lambda i, j: (i, gb0 + j)),
        ],
        out_specs=pl.BlockSpec((tm, tn), lambda i, j: (i, j)),
        compiler_params=_cparams(("parallel", "arbitrary")),
        name="merge",
    )(o_a, o_b, wa, wb, tail, tail)


def _cross_kernel(q_ref, k_ref, v_ref, o_ref):
    dh = q_ref.shape[-1] // MEM_HEADS
    for h in range(MEM_HEADS):
        sl = slice(h * dh, (h + 1) * dh)
        s = _dot_nt(q_ref[:, sl], k_ref[0, :, sl].astype(BF16)) * (dh ** -0.5)
        s = s - jnp.max(s, axis=-1, keepdims=True)
        p = jnp.exp(s)
        p = p / jnp.sum(p, axis=-1, keepdims=True)
        o_ref[:, sl] = _dot(p.astype(BF16), v_ref[0, :, sl].astype(BF16)).astype(o_ref.dtype)


def _cross_attend(q, mem_k, mem_v, *, bsz, t, tm):
    d = q.shape[1]
    nt = t // tm
    mt = mem_k.shape[1]
    return pl.pallas_call(
        _cross_kernel,
        out_shape=jax.ShapeDtypeStruct(q.shape, BF16),
        grid=(bsz, nt),
        in_specs=[
            pl.BlockSpec((tm, d), lambda b, i: (b * nt + i, 0)),
            pl.BlockSpec((1, mt, d), lambda b, i: (b, 0, 0)),
            pl.BlockSpec((1, mt, d), lambda b, i: (b, 0, 0)),
        ],
        out_specs=pl.BlockSpec((tm, d), lambda b, i: (b * nt + i, 0)),
        compiler_params=_cparams(("parallel", "arbitrary")),
        name="cross_attn",
    )(q, mem_k, mem_v)


def _topk_rows(work, cidx, k):
    n, tn = work.shape
    rid = lax.broadcasted_iota(jnp.int32, (n, tn), 0)
    kid = lax.broadcasted_iota(jnp.int32, (k, tn), 0)
    vals = jnp.zeros((k, tn), F32)
    idxs = jnp.zeros((k, tn), jnp.int32)
    for t in range(k):
        m = jnp.max(work, axis=0, keepdims=True)
        pos = jnp.min(jnp.where(work == m, rid, n), axis=0, keepdims=True)
        hit = rid == pos
        if cidx is None:
            picked = pos
        else:
            picked = jnp.sum(jnp.where(hit, cidx, 0), axis=0, keepdims=True)
        vals = jnp.where(kid == t, m, vals)
        idxs = jnp.where(kid == t, picked, idxs)
        work = jnp.where(hit, -jnp.inf, work)
    return vals, idxs


def _product_candidates(v1, i1, v2, i2, k):
    assert k == 2 * SUBLANES
    sub = lax.broadcasted_iota(jnp.int32, (SUBLANES, v1.shape[1]), 0)
    vals, idxs = [], []
    for a in range(k // 2):
        nb = k // (a + 1)
        width = k if nb > SUBLANES else SUBLANES
        cv = v1[a:a + 1, :] + v2[:width, :]
        ci = i1[a:a + 1, :] * PEER_N_KEYS + i2[:width, :]
        if nb < SUBLANES:
            cv = jnp.where(sub < nb, cv, -jnp.inf)
        vals.append(cv)
        idxs.append(ci)
    vals.append(v1[k // 2:, :] + v2[0:1, :])
    idxs.append(i1[k // 2:, :] * PEER_N_KEYS + i2[0:1, :])
    return jnp.concatenate(vals, axis=0), jnp.concatenate(idxs, axis=0)


def _peer_route_kernel(q_ref, keys_ref, e_ref, g_ref):
    tn = q_ref.shape[0]
    k = PEER_TOPK
    for h in range(PEER_HEADS):
        sub = []
        for p in range(2):
            c0 = (h * 2 + p) * PEER_N_KEYS
            s = _dot_nt(keys_ref[h, p], q_ref[:, c0:c0 + PEER_N_KEYS], HIGHEST)
            sub.append(_topk_rows(s, None, k))
        (v1, i1), (v2, i2) = sub
        cand, cidx = _product_candidates(v1, i1, v2, i2, k)
        best, eidx = _topk_rows(cand, cidx, k)
        ex = jnp.exp(best - jnp.max(best, axis=0, keepdims=True))
        gate = ex / jnp.sum(ex, axis=0, keepdims=True)
        e_ref[h * k:(h + 1) * k, :] = eidx
        g_ref[h * k:(h + 1) * k, :] = gate


HALF_KEYS = PEER_N_KEYS // 2
BF16_HIGH_MASK = 0xFFFF0000


def _peer_gate_kernel(et_ref, gt_ref, o_ref, e_scr, g_scr):
    tg = et_ref.shape[1]
    e_scr[...] = et_ref[...].T
    g_scr[...] = gt_ref[...].T
    kid = lax.broadcasted_iota(jnp.int32, (PEER_N_KEYS, PEER_N_KEYS), 0)

    def body(n8, carry):
        rows = pl.ds(pl.multiple_of(n8 * SUBLANES, SUBLANES), SUBLANES)
        e8 = e_scr[rows, :]
        g8 = g_scr[rows, :]
        a8 = jnp.right_shift(e8, 7)
        b8 = jnp.bitwise_and(e8, PEER_N_KEYS - 1)
        words = []
        for s in range(SUBLANES):
            at = jnp.where(a8[s:s + 1, :] == kid, g8[s:s + 1, :], 0.0).astype(BF16)
            bt = jnp.where(b8[s:s + 1, :] == kid, 1.0, 0.0).astype(BF16)
            grid = _dot_nt(at, bt)
            bits = pltpu.bitcast(grid.astype(BF16).astype(F32), jnp.uint32)
            words.append(jnp.bitwise_or(jnp.bitwise_and(bits[HALF_KEYS:, :], jnp.uint32(BF16_HIGH_MASK)),
                                        jnp.right_shift(bits[:HALF_KEYS, :], jnp.uint32(16))))
        o_ref[:, rows, :] = jnp.swapaxes(jnp.stack(words, axis=0), 0, 1)
        return carry

    lax.fori_loop(0, tg // SUBLANES, body, 0, unroll=4)


def _peer_route(qp, sub_keys, *, tn):
    n = qp.shape[0]
    rows = PEER_HEADS * PEER_TOPK
    e_t, g_t = pl.pallas_call(
        _peer_route_kernel,
        out_shape=(jax.ShapeDtypeStruct((rows, n), jnp.int32), jax.ShapeDtypeStruct((rows, n), F32)),
        grid=(n // tn,),
        in_specs=[
            pl.BlockSpec((tn, qp.shape[1]), lambda i: (i, 0)),
            pl.BlockSpec(sub_keys.shape, lambda i: (0, 0, 0, 0)),
        ],
        out_specs=(pl.BlockSpec((rows, tn), lambda i: (0, i)), pl.BlockSpec((rows, tn), lambda i: (0, i))),
        compiler_params=_cparams(("parallel",)),
        name="peer_route",
    )(qp, sub_keys)
    return pl.pallas_call(
        _peer_gate_kernel,
        out_shape=jax.ShapeDtypeStruct((HALF_KEYS, n, PEER_N_KEYS), jnp.uint32),
        grid=(n // tn,),
        in_specs=[pl.BlockSpec((rows, tn), lambda i: (0, i)), pl.BlockSpec((rows, tn), lambda i: (0, i))],
        out_specs=pl.BlockSpec((HALF_KEYS, tn, PEER_N_KEYS), lambda i: (0, i, 0)),
        scratch_shapes=[pltpu.VMEM((tn, rows), jnp.int32), pltpu.VMEM((tn, rows), F32)],
        compiler_params=_cparams(("parallel",)),
        name="peer_gates",
    )(e_t, g_t)


def _gelu_tanh(x):
    return 0.5 * x * (1.0 + jnp.tanh(math.sqrt(2.0 / math.pi) * (x + 0.044715 * (x * x * x))))


PEER_BLOCKS_PER_STEP = 4
PEER_TN = 1024


def _peer_expert_kernel(x_ref, gn_ref, gpa_ref, gpb_ref, *rest):
    nblk = PEER_BLOCKS_PER_STEP
    u_refs, v_refs = rest[:nblk], rest[nblk:2 * nblk]
    gf_ref, o_ref, h_scr, acc_scr, coef_scr = rest[2 * nblk:]
    s = pl.program_id(1)
    last = pl.num_programs(1) - 1

    @pl.when(s == 0)
    def _():
        h_scr[...] = _rms(x_ref[...], gn_ref[...]).astype(BF16)
        acc_scr[...] = jnp.zeros(acc_scr.shape, F32)
        coef_scr[...] = jnp.zeros(coef_scr.shape, BF16)

    v_cat = jnp.concatenate([v[...] for v in v_refs], axis=0)
    acc_scr[...] += _dot(coef_scr[(s + 1) % 2], v_cat)

    def unpack(word):
        return (pltpu.bitcast(jnp.left_shift(word, jnp.uint32(16)), F32),
                pltpu.bitcast(jnp.bitwise_and(word, jnp.uint32(BF16_HIGH_MASK)), F32))

    a_lo, a_hi = unpack(gpa_ref[...])
    b_lo, b_hi = unpack(gpb_ref[...])
    u_cat = jnp.concatenate([u[...] for u in u_refs], axis=0)
    hid = _gelu_tanh(_dot_nt(h_scr[...], u_cat))
    coef_scr[s % 2] = (jnp.concatenate([a_lo, b_lo, a_hi, b_hi], axis=1) * hid).astype(BF16)

    @pl.when(s == last)
    def _():
        o_ref[...] = _rms(x_ref[...] + acc_scr[...], gf_ref[...])


def _peer_experts(x, norm_g, gates, u_tab, v_tab, final_g, *, tn):
    n, d = x.shape
    eb = PEER_N_KEYS
    assert u_tab.shape[0] == eb * eb
    nblk = PEER_BLOCKS_PER_STEP
    ns = eb // nblk

    def cur(s):
        return jnp.minimum(s, ns - 1)

    def prev(s):
        return jnp.maximum(s - 1, 0)

    def table(which, q):
        return pl.BlockSpec((eb, d), lambda i, s: (which(s) + q * ns, 0))

    return pl.pallas_call(
        _peer_expert_kernel,
        out_shape=jax.ShapeDtypeStruct((n, d), F32),
        grid=(n // tn, ns + 1),
        in_specs=[
            pl.BlockSpec((tn, d), lambda i, s: (i, 0), pipeline_mode=pl.Buffered(1)),
            pl.BlockSpec((1, d), lambda i, s: (0, 0)),
            pl.BlockSpec((None, tn, eb), lambda i, s: (cur(s), i, 0)),
            pl.BlockSpec((None, tn, eb), lambda i, s: (cur(s) + ns, i, 0)),
        ] + [table(cur, q) for q in range(nblk)] + [table(prev, q) for q in range(nblk)] + [
            pl.BlockSpec((1, d), lambda i, s: (0, 0)),
        ],
        out_specs=pl.BlockSpec((tn, d), lambda i, s: (i, 0), pipeline_mode=pl.Buffered(1)),
        scratch_shapes=[pltpu.VMEM((tn, d), BF16), pltpu.VMEM((tn, d), F32),
                        pltpu.VMEM((2, tn, nblk * eb), BF16)],
        compiler_params=_cparams(("parallel", "arbitrary")),
        name="peer_experts",
    )(x, norm_g.reshape(1, d), gates, gates, *([u_tab] * nblk), *([v_tab] * nblk), final_g.reshape(1, d))


def _layer(x, wts, layer, mem_k, mem_v, conv_prev, gdn_state, k_past, v_past, *, tm):
    bsz, t, d = x.shape
    n = bsz * t
    x2d = x.reshape(n, d)
    main, k_new, v_new, tail = _in_proj(x2d, wts["norm_mix_g"], wts["w_in"], tm=min(IN_TM, n))
    conv_dim = 3 * GDN_WIDTH
    conv_prev8 = jnp.concatenate(
        [jnp.zeros((bsz, SUBLANES - (GDN_CONV - 1), conv_dim), F32), conv_prev.astype(F32)], axis=1)
    o_a, s_new = _gdn(main, tail, conv_prev8, gdn_state.astype(F32), wts["gdn_conv_w"], wts["gdn_a_log"],
                      wts["gdn_dt_bias"], wts["gdn_norm_g"], bsz=bsz, t=t)
    conv_new = main.reshape(bsz, t, -1)[:, t - (GDN_CONV - 1):, :conv_dim]
    lam_init = 0.8 - 0.6 * math.exp(-0.3 * layer)
    lams = tuple(wts[k].reshape(1, -1) for k in ("diff_lambda_q1", "diff_lambda_k1", "diff_lambda_q2", "diff_lambda_k2"))
    if k_past is None:
        o_b = _diff_prompt(main, k_new, v_new, lams, wts["diff_subln_g"], bsz=bsz, t=t, lam_init=lam_init)
    else:
        o_b = _diff_sample(main, k_new, v_new, k_past, v_past, lams, wts["diff_subln_g"], bsz=bsz, t=t,
                           lam_init=lam_init)
    k_rows = k_new.reshape(bsz, t, DIFF_HEADS, HEAD_DIM)
    v_rows = v_new.reshape(bsz, t, DIFF_HEADS, HEAD_DIM)
    merged = _merge(o_a, o_b, wts["w_branch_a"], wts["w_branch_b"], tail, tm=tm)
    x1 = _matmul(merged, wts["w_out"], residual=x2d, tm=tm, tn=512, name="out_proj")
    qm = _matmul(x1, wts["w_mq"], norm_g=wts["norm_cross_g"], tm=tm, tn=512, out_dtype=BF16, name="mem_q")
    mt = mem_k.shape[1]
    oc = _cross_attend(qm, mem_k.reshape(bsz, mt, d), mem_v.reshape(bsz, mt, d), bsz=bsz, t=t, tm=min(tm, t))
    x2 = _matmul(oc, wts["w_mo"], residual=x1, tm=tm, tn=512, name="mem_o")
    qp = _matmul(x2, wts["peer_w_q"], norm_g=wts["norm_ffn_g"], tm=tm, tn=512, name="peer_q")
    tn = min(PEER_TN, n)
    gates = _peer_route(qp, wts["peer_sub_keys"], tn=LANES)
    y = _peer_experts(x2, wts["norm_ffn_g"], gates, wts["peer_u"], wts["peer_v"], wts["final_norm_g"], tn=tn)
    return y.reshape(bsz, t, d), k_rows, v_rows, s_new, conv_new


def kernel(x_prompt, x_sample, cache_diff_k, cache_diff_v, state_gdn, state_conv, cache_mem_k, cache_mem_v,
           mem_prompt, norm_mix_g, w_in, gdn_conv_w, gdn_a_log, gdn_dt_bias, gdn_norm_g,
           diff_lambda_q1, diff_lambda_k1, diff_lambda_q2, diff_lambda_k2, diff_subln_g,
           w_branch_a, w_branch_b, w_out, norm_cross_g, norm_mem_g, w_mq, w_mk, w_mv, w_mo,
           norm_ffn_g, peer_w_q, peer_sub_keys, peer_u, peer_v, final_norm_g):
    depth = w_in.shape[0]
    assert depth == 1, "final norm is fused into the last layer's PEER kernel"
    l = 0
    wts = {
        "norm_mix_g": norm_mix_g[l], "w_in": _in_proj_weights(w_in[l]), "gdn_conv_w": gdn_conv_w[l],
        "gdn_a_log": gdn_a_log[l], "gdn_dt_bias": gdn_dt_bias[l], "gdn_norm_g": gdn_norm_g[l],
        "diff_lambda_q1": diff_lambda_q1[l], "diff_lambda_k1": diff_lambda_k1[l],
        "diff_lambda_q2": diff_lambda_q2[l], "diff_lambda_k2": diff_lambda_k2[l],
        "diff_subln_g": diff_subln_g[l], "w_branch_a": w_branch_a[l].astype(BF16),
        "w_branch_b": w_branch_b[l].astype(BF16), "w_out": w_out[l].astype(BF16),
        "norm_cross_g": norm_cross_g[l], "w_mq": w_mq[l].astype(BF16), "w_mo": w_mo[l].astype(BF16),
        "norm_ffn_g": norm_ffn_g[l], "peer_w_q": peer_w_q[l].astype(BF16), "peer_sub_keys": peer_sub_keys[l],
        "peer_u": peer_u[l].astype(BF16), "peer_v": peer_v[l].astype(BF16), "final_norm_g": final_norm_g,
    }
    bp, tp, d = x_prompt.shape
    bs, ts, _ = x_sample.shape
    mem2d = mem_prompt.reshape(-1, d)
    tmm = min(512, mem2d.shape[0])
    mem_k = _matmul(mem2d, w_mk[l].astype(BF16), norm_g=norm_mem_g[l], tm=tmm, tn=512, name="mem_k")
    mem_v = _matmul(mem2d, w_mv[l].astype(BF16), norm_g=norm_mem_g[l], tm=tmm, tn=512, name="mem_v")
    mshape = mem_prompt.shape[:2] + (MEM_HEADS, d // MEM_HEADS)
    mem_k = mem_k.reshape(mshape)
    mem_v = mem_v.reshape(mshape)
    conv0 = jnp.zeros((bp, GDN_CONV - 1, 3 * GDN_WIDTH), F32)
    s0 = jnp.zeros((bp, GDN_HEADS, HEAD_DIM, HEAD_DIM), F32)
    yp, pk, pv, ps, pc = _layer(x_prompt, wts, l, mem_k, mem_v, conv0, s0, None, None, tm=min(1024, bp * tp))
    ys, sk, sv, ss, sc = _layer(x_sample, wts, l, cache_mem_k[l], cache_mem_v[l], state_conv[l], state_gdn[l],
                                cache_diff_k[l], cache_diff_v[l], tm=min(256, bs * ts))
    return (yp, ys, pk[None], pv[None], ps[None], pc[None], mem_k[None], mem_v[None],
            sk[None], sv[None], ss[None], sc[None])
```

```python
import functools
import math

import jax
import jax.numpy as jnp
from jax import lax
from jax.experimental import pallas as pl
from jax.experimental.pallas import tpu as pltpu

F32 = jnp.float32
BF16 = jnp.bfloat16
HIGHEST = lax.Precision.HIGHEST

RMS_EPS = 1e-6
CHUNK = 64
CHUNK_SHIFT = 6
DIFF_TQ = 1024
DIFF_SUB = 256
GDN_HEADS = 8
HEAD_DIM = 128
GDN_WIDTH = GDN_HEADS * HEAD_DIM
GDN_CONV = 4
DIFF_HEADS = 8
DIFF_HEAD_DIM = 64
MEM_HEADS = 4
PEER_HEADS = 8
PEER_N_KEYS = 128
PEER_TOPK = 16
LANES = 128
SUBLANES = 8
V7X_VMEM_LIMIT = 52 * 1024 * 1024
MM_TN = 1024


def _cparams(sem, vmem=V7X_VMEM_LIMIT):
    return pltpu.CompilerParams(dimension_semantics=sem, vmem_limit_bytes=vmem)


def _dot(a, b, precision=None):
    return jnp.dot(a, b, preferred_element_type=F32, precision=precision)


def _dot_nt(a, b, precision=None):
    return lax.dot_general(a, b, (((1,), (1,)), ((), ())), preferred_element_type=F32, precision=precision)


def _split_bf16(x):
    hi = x.astype(BF16)
    return hi, (x - hi.astype(F32)).astype(BF16)


def _dot3_impl(dot, a, b):
    a_hi, a_lo = _split_bf16(a)
    b_hi, b_lo = _split_bf16(b)
    m = a.shape[0]
    both = dot(jnp.concatenate([a_hi, a_lo], axis=0), b_hi)
    return both[:m] + both[m:] + dot(a_hi, b_lo)


def _dot3(a, b):
    return _dot3_impl(_dot, a, b)


def _dot3_nt(a, b):
    return _dot3_impl(_dot_nt, a, b)


def _rms(xf, g):
    return xf * lax.rsqrt(jnp.mean(xf * xf, axis=-1, keepdims=True) + RMS_EPS) * g


def _mm_kernel(*refs, has_norm, has_res):
    it = iter(refs)
    x_ref = next(it)
    g_ref = next(it) if has_norm else None
    w_ref = next(it)
    r_ref = next(it) if has_res else None
    o_ref = next(it)
    h_scr = next(it)

    @pl.when(pl.program_id(1) == 0)
    def _():
        xf = x_ref[...].astype(F32)
        if has_norm:
            xf = _rms(xf, g_ref[...])
        h_scr[...] = xf.astype(BF16)

    acc = _dot(h_scr[...], w_ref[...])
    if has_res:
        acc = acc + r_ref[...]
    o_ref[...] = acc.astype(o_ref.dtype)


def _matmul(x, w, *, norm_g=None, residual=None, tm, tn, out_dtype=F32, name="matmul"):
    m, k = x.shape
    n = w.shape[1]
    assert m % tm == 0 and n % tn == 0, (m, n, tm, tn)
    in_specs = [pl.BlockSpec((tm, k), lambda i, j: (i, 0))]
    args = [x]
    if norm_g is not None:
        in_specs.append(pl.BlockSpec((1, k), lambda i, j: (0, 0)))
        args.append(norm_g.reshape(1, k).astype(F32))
    in_specs.append(pl.BlockSpec((k, tn), lambda i, j: (0, j)))
    args.append(w)
    if residual is not None:
        in_specs.append(pl.BlockSpec((tm, tn), lambda i, j: (i, j)))
        args.append(residual)
    return pl.pallas_call(
        functools.partial(_mm_kernel, has_norm=norm_g is not None, has_res=residual is not None),
        out_shape=jax.ShapeDtypeStruct((m, n), out_dtype),
        grid=(m // tm, n // tn),
        in_specs=in_specs,
        out_specs=pl.BlockSpec((tm, tn), lambda i, j: (i, j)),
        scratch_shapes=[pltpu.VMEM((tm, k), BF16)],
        compiler_params=_cparams(("parallel", "arbitrary")),
        name=name,
    )(*args)


IN_TN = 512
IN_TM = 1024
IN_W1_TILES = 8
IN_W2_TILES = 14
IN_MAIN_TILES = 10
IN_K_TILE0 = 10
IN_V_TILE0 = 12
IN_TAIL_TILE0 = 14
IN_TILES = IN_W1_TILES + IN_W2_TILES + 1
MAIN_COLS = IN_MAIN_TILES * IN_TN
TAIL_COLS = (IN_TILES - IN_TAIL_TILE0) * IN_TN
COL_DQ_MAIN = 4096
COL_GB_TAIL = 2048
COL_BA_TAIL = 4096


def _in_proj_kernel(x_ref, g_ref, w1_ref, w2_ref, w3_ref, main_ref, k_ref, v_ref, tail_ref, h_scr):
    j = pl.program_id(1)

    @pl.when(j == 0)
    def _():
        h_scr[...] = _rms(x_ref[...], g_ref[...]).astype(BF16)

    routes = ((0, IN_W1_TILES, w1_ref, main_ref),
              (IN_W1_TILES, IN_K_TILE0, w2_ref, main_ref),
              (IN_K_TILE0, IN_V_TILE0, w2_ref, k_ref),
              (IN_V_TILE0, IN_TAIL_TILE0, w2_ref, v_ref),
              (IN_TAIL_TILE0, IN_TILES - 1, w2_ref, tail_ref),
              (IN_TILES - 1, IN_TILES, w3_ref, tail_ref))
    for lo, hi, w_ref, dst in routes:
        @pl.when(jnp.logical_and(j >= lo, j < hi))
        def _(w_ref=w_ref, dst=dst):
            dst[...] = _dot(h_scr[...], w_ref[...])


def _in_proj_weights(w_in):
    c_z = 4 * GDN_WIDTH
    c_ba = c_z + 2 * GDN_HEADS
    assert c_z == IN_W1_TILES * IN_TN and w_in.shape[1] - c_ba == IN_W2_TILES * IN_TN
    w3 = jnp.pad(w_in[:, c_z:c_ba], ((0, 0), (0, IN_TN - 2 * GDN_HEADS)))
    return w_in[:, :c_z].astype(BF16), w_in[:, c_ba:].astype(BF16), w3.astype(BF16)


def _in_proj(x, norm_g, w_pieces, *, tm):
    m, d = x.shape
    w1, w2, w3 = w_pieces
    tn = IN_TN
    out = lambda cols: jax.ShapeDtypeStruct((m, cols), F32)
    return pl.pallas_call(
        _in_proj_kernel,
        out_shape=(out(MAIN_COLS), out(2 * IN_TN), out(2 * IN_TN), out(TAIL_COLS)),
        grid=(m // tm, IN_TILES),
        in_specs=[
            pl.BlockSpec((tm, d), lambda i, j: (i, 0)),
            pl.BlockSpec((1, d), lambda i, j: (0, 0)),
            pl.BlockSpec((d, tn), lambda i, j: (0, jnp.minimum(j, IN_W1_TILES - 1))),
            pl.BlockSpec((d, tn), lambda i, j: (0, jnp.clip(j - IN_W1_TILES, 0, IN_W2_TILES - 1))),
            pl.BlockSpec((d, tn), lambda i, j: (0, 0)),
        ],
        out_specs=(
            pl.BlockSpec((tm, tn), lambda i, j: (i, jnp.minimum(j, IN_MAIN_TILES - 1))),
            pl.BlockSpec((tm, tn), lambda i, j: (i, jnp.clip(j - IN_K_TILE0, 0, 1))),
            pl.BlockSpec((tm, tn), lambda i, j: (i, jnp.clip(j - IN_V_TILE0, 0, 1))),
            pl.BlockSpec((tm, tn), lambda i, j: (i, jnp.clip(j - IN_TAIL_TILE0, 0, IN_TILES - IN_TAIL_TILE0 - 1))),
        ),
        scratch_shapes=[pltpu.VMEM((tm, d), BF16)],
        compiler_params=_cparams(("parallel", "arbitrary")),
        name="in_proj",
    )(x, norm_g.reshape(1, d).astype(F32), w1, w2, w3)


def _gdn_kernel(qkv_ref, z_ref, ba_ref, prev_ref, s0_ref, convw_ref, alog_ref, dtb_ref, ng_ref,
                o_ref, sfin_ref, s_scr, xbuf, *, L):
    c = pl.program_id(1)

    @pl.when(c == 0)
    def _():
        s_scr[...] = s0_ref[0]
        xbuf[0:SUBLANES, :] = prev_ref[0]

    xbuf[SUBLANES:SUBLANES + L, :] = qkv_ref[...]
    base = SUBLANES - (GDN_CONV - 1)
    y = xbuf[base:base + L, :] * convw_ref[0:1, :]
    for i in range(1, GDN_CONV):
        y = y + xbuf[base + i:base + i + L, :] * convw_ref[i:i + 1, :]
    y = y * jax.nn.sigmoid(y)
    tail = xbuf[L:L + SUBLANES, :]
    xbuf[0:SUBLANES, :] = tail

    row = lax.broadcasted_iota(jnp.int32, (L, L), 0)
    col = lax.broadcasted_iota(jnp.int32, (L, L), 1)
    lower = row >= col
    strict = row > col
    eye = (row == col).astype(F32)
    n_double = int(math.log2(L)) - 1
    assert 2 ** (n_double + 1) == L

    ba = ba_ref[...]
    beta_all = jax.nn.sigmoid(ba)
    a_in = ba + dtb_ref[...]
    softplus = jnp.maximum(a_in, 0.0) + jnp.log(1.0 + jnp.exp(-jnp.abs(a_in)))
    g_all = -jnp.exp(alog_ref[...]) * softplus
    g_cum = _dot(lower.astype(F32), g_all, HIGHEST)
    g_cum_t = g_cum.T

    heads = range(GDN_HEADS)
    cat0 = functools.partial(jnp.concatenate, axis=0)

    def head_cols(part, h):
        return y[:, part * GDN_WIDTH + h * HEAD_DIM:part * GDN_WIDTH + (h + 1) * HEAD_DIM]

    q = [head_cols(0, h) for h in heads]
    k = [head_cols(1, h) for h in heads]
    v = [head_cols(2, h) for h in heads]
    q = [a * lax.rsqrt(jnp.sum(a * a, axis=-1, keepdims=True) + 1e-6) * (HEAD_DIM ** -0.5) for a in q]
    k = [a * lax.rsqrt(jnp.sum(a * a, axis=-1, keepdims=True) + 1e-6) for a in k]
    beta = [beta_all[:, h:h + 1] for h in heads]
    g_col = [g_cum[:, GDN_HEADS + h:GDN_HEADS + h + 1] for h in heads]
    g_row = [g_cum_t[GDN_HEADS + h:GDN_HEADS + h + 1, :] for h in heads]
    decay = [jnp.where(lower, jnp.exp(jnp.where(lower, g_col[h] - g_row[h], 0.0)), 0.0) for h in heads]
    g_last = [g_col[h][L - 1:L, :] for h in heads]
    e_col = [jnp.exp(g_col[h]) for h in heads]
    qk_kk = [_dot3_nt(cat0([q[h], k[h]]), k[h]) for h in heads]
    x = [-jnp.where(strict, beta[h] * qk_kk[h][L:] * decay[h], 0.0) for h in heads]
    t_inv = [eye + x[h] for h in heads]
    xp = [_dot3(x[h], x[h]) for h in heads]
    for _ in range(n_double - 1):
        both = [_dot3(cat0([t_inv[h], xp[h]]), xp[h]) for h in heads]
        t_inv = [t_inv[h] + both[h][:L] for h in heads]
        xp = [both[h][L:] for h in heads]
    t_inv = [t_inv[h] + _dot3(t_inv[h], xp[h]) for h in heads]
    sol = [_dot3(t_inv[h], jnp.concatenate([v[h] * beta[h], k[h] * (beta[h] * e_col[h])], axis=-1)) for h in heads]
    qk = [jnp.where(lower, qk_kk[h][:L] * decay[h], 0.0) for h in heads]
    k_dec_t = [(k[h] * jnp.exp(g_last[h] - g_col[h])).T for h in heads]
    s = [s_scr[h] for h in heads]
    ws_qs = [_dot3(cat0([sol[h][:, HEAD_DIM:], q[h] * e_col[h]]), s[h]) for h in heads]
    v_new = [sol[h][:, :HEAD_DIM] - ws_qs[h][:L] for h in heads]
    tail = [_dot3(cat0([qk[h], k_dec_t[h]]), v_new[h]) for h in heads]
    for h in heads:
        s_scr[h] = s[h] * jnp.exp(g_last[h]) + tail[h][L:]
        zf = z_ref[:, h * HEAD_DIM:(h + 1) * HEAD_DIM]
        o = _rms(ws_qs[h][L:] + tail[h][:L], ng_ref[...]) * (zf * jax.nn.sigmoid(zf))
        o_ref[:, h * HEAD_DIM:(h + 1) * HEAD_DIM] = o.astype(o_ref.dtype)

    @pl.when(c == pl.num_programs(1) - 1)
    def _():
        sfin_ref[0] = s_scr[...]


def _decay_lanes(p):
    return jnp.zeros((1, LANES), F32).at[0, GDN_HEADS:2 * GDN_HEADS].set(p.astype(F32))


def _gdn(main, tail, conv_prev8, s0, conv_w, a_log, dt_bias, norm_g, *, bsz, t):
    L = min(CHUNK, t)
    nc = t // L
    conv_dim = 3 * GDN_WIDTH
    kern = functools.partial(_gdn_kernel, L=L)
    o, s_fin = pl.pallas_call(
        kern,
        out_shape=(jax.ShapeDtypeStruct((bsz * t, GDN_WIDTH), BF16),
                   jax.ShapeDtypeStruct((bsz, GDN_HEADS, HEAD_DIM, HEAD_DIM), F32)),
        grid=(bsz, nc),
        in_specs=[
            pl.BlockSpec((L, conv_dim), lambda b, c: (b * nc + c, 0)),
            pl.BlockSpec((L, GDN_WIDTH), lambda b, c: (b * nc + c, conv_dim // GDN_WIDTH)),
            pl.BlockSpec((L, LANES), lambda b, c: (b * nc + c, COL_BA_TAIL // LANES)),
            pl.BlockSpec((1, SUBLANES, conv_dim), lambda b, c: (b, 0, 0)),
            pl.BlockSpec((1, GDN_HEADS, HEAD_DIM, HEAD_DIM), lambda b, c: (b, 0, 0, 0)),
            pl.BlockSpec((GDN_CONV, conv_dim), lambda b, c: (0, 0)),
            pl.BlockSpec((1, LANES), lambda b, c: (0, 0)),
            pl.BlockSpec((1, LANES), lambda b, c: (0, 0)),
            pl.BlockSpec((1, HEAD_DIM), lambda b, c: (0, 0)),
        ],
        out_specs=(pl.BlockSpec((L, GDN_WIDTH), lambda b, c: (b * nc + c, 0)),
                   pl.BlockSpec((1, GDN_HEADS, HEAD_DIM, HEAD_DIM), lambda b, c: (b, 0, 0, 0))),
        scratch_shapes=[pltpu.VMEM((GDN_HEADS, HEAD_DIM, HEAD_DIM), F32),
                        pltpu.VMEM((SUBLANES + L + SUBLANES, conv_dim), F32)],
        compiler_params=_cparams(("parallel", "arbitrary")),
        name="gdn",
    )(main, main, tail, conv_prev8, s0, conv_w, _decay_lanes(a_log), _decay_lanes(dt_bias),
      norm_g.reshape(1, -1))
    return o, s_fin


def _lambda_value(lam_refs, lam_init):
    lq1, lk1, lq2, lk2 = (r[...] for r in lam_refs)
    return (jnp.exp(jnp.sum(lq1 * lk1, axis=-1, keepdims=True))
            - jnp.exp(jnp.sum(lq2 * lk2, axis=-1, keepdims=True)) + lam_init)


def _softmax_update(m_scr, l_scr, acc_scr, scores, values):
    idx = range(len(scores))
    m_old = [m_scr[n] for n in idx]
    m_new = [jnp.maximum(m_old[n], jnp.max(scores[n], axis=-1, keepdims=True)) for n in idx]
    p = [jnp.exp(scores[n] - m_new[n]) for n in idx]
    alpha = [jnp.exp(m_old[n] - m_new[n]) for n in idx]
    pv = [_dot(p[n].astype(BF16), values[n]) for n in idx]
    for n in idx:
        l_scr[n] = alpha[n] * l_scr[n] + jnp.sum(p[n], axis=-1, keepdims=True)
        acc_scr[n] = alpha[n] * acc_scr[n] + pv[n]
        m_scr[n] = m_new[n]


def _diff_finalize(o1, o2, lam_refs, g_ref, lam_init):
    lam = _lambda_value(lam_refs, lam_init)
    return _rms(o1 - lam * o2, g_ref[...]) * (1.0 - lam_init)


def _diff_init(m_scr, l_scr, acc_scr):
    m_scr[...] = jnp.full(m_scr.shape, -jnp.inf, F32)
    l_scr[...] = jnp.zeros(l_scr.shape, F32)
    acc_scr[...] = jnp.zeros(acc_scr.shape, F32)


def _near_bias(slope, r, q_pos, k_pos):
    visible = jnp.right_shift(k_pos, CHUNK_SHIFT) <= jnp.right_shift(q_pos, CHUNK_SHIFT)
    return jnp.where(visible, slope * (r - jnp.abs(q_pos - k_pos)).astype(F32), -jnp.inf)


def _split_distance(slope, d):
    return (-(slope * CHUNK) * jnp.right_shift(d, CHUNK_SHIFT).astype(F32),
            -slope * jnp.bitwise_and(d, CHUNK - 1).astype(F32))


def _diff_prompt_kernel(i_tab, j_tab, slopes_ref, q_ref, k_ref, v_ref, lq1, lk1, lq2, lk2, g_ref, o_ref,
                        m_scr, l_scr, acc_scr, lhs_scr, s_even, s_odd, *, tq, lam_init):
    h = pl.program_id(1)
    s_id = pl.program_id(2)
    n_pairs = pl.num_programs(2) - 1
    score_on = s_id < n_pairs
    consume_on = s_id >= 1
    i = i_tab[s_id]
    j = j_tab[s_id]
    ci = i_tab[jnp.maximum(s_id - 1, 0)]
    cj = j_tab[jnp.maximum(s_id - 1, 0)]
    slope = slopes_ref[h]
    half = DIFF_HEAD_DIM
    lane = lax.broadcasted_iota(jnp.int32, (tq, LANES), 1)

    @pl.when(jnp.logical_and(consume_on, cj == 0))
    def _():
        _diff_init(m_scr, l_scr, acc_scr)

    @pl.when(jnp.logical_and(score_on, j == 0))
    def _():
        qs = q_ref[...] * (DIFF_HEAD_DIM ** -0.5)
        lhs_scr[0] = jnp.where(lane < half, qs, 1.0).astype(BF16)
        lhs_scr[1] = jnp.where(lane >= half, qs, 1.0).astype(BF16)

    sub = min(DIFF_SUB, tq)
    units = [(n, qb) for qb in range(tq // sub) for n in range(2)]

    def score_tile(near, dst):
        kb = k_ref[...]
        if near:
            aug0 = aug1 = jnp.zeros_like(kb)
        else:
            d = (i - j) * tq - lax.broadcasted_iota(jnp.int32, (tq, LANES), 0)
            t_hi, t_lo = _split_distance(slope, d)
            aug0 = jnp.where(lane == half, t_hi, jnp.where(lane == half + 1, t_lo, 0.0))
            aug1 = jnp.where(lane == 0, t_hi, jnp.where(lane == 1, t_lo, 0.0))
        keys = [jnp.where(lane < half, kb, aug0).astype(BF16), jnp.where(lane >= half, kb, aug1).astype(BF16)]
        bias = None
        for n, qb in units:
            cols = slice(qb * sub, (qb + 1) * sub)
            if near:
                live = (qb + 1) * sub
                if n == 0:
                    k_pos = lax.broadcasted_iota(jnp.int32, (live, sub), 0)
                    q_pos = qb * sub + lax.broadcasted_iota(jnp.int32, (live, sub), 1)
                    bias = _near_bias(slope, q_pos, q_pos, k_pos)
                dst[n, :live, cols] = _dot_nt(keys[n][:live], lhs_scr[n, cols, :]) + bias
                if live < tq:
                    dst[n, live:, cols] = jnp.full((tq - live, sub), -jnp.inf, F32)
            else:
                dst[n, :, cols] = _dot_nt(keys[n], lhs_scr[n, cols, :])

    def consume_tile(src):
        vt = v_ref[...].T.astype(BF16)
        idx = range(len(units))
        cols = [slice(qb * sub, (qb + 1) * sub) for _, qb in units]
        s = [src[n, :, cols[u]] for u, (n, _) in enumerate(units)]
        m_old = [m_scr[n, :, cols[u]] for u, (n, _) in enumerate(units)]
        m_new = [jnp.maximum(m_old[u], jnp.max(s[u], axis=0, keepdims=True)) for u in idx]
        p = [jnp.exp(s[u] - m_new[u]) for u in idx]
        alpha = [jnp.exp(m_old[u] - m_new[u]) for u in idx]
        pv = [_dot(vt, p[u].astype(BF16)) for u in idx]
        for u, (n, _) in enumerate(units):
            l_scr[n, :, cols[u]] = alpha[u] * l_scr[n, :, cols[u]] + jnp.sum(p[u], axis=0, keepdims=True)
            acc_scr[n, :, cols[u]] = alpha[u] * acc_scr[n, :, cols[u]] + pv[u]
            m_scr[n, :, cols[u]] = m_new[u]

    def stage(cond, near, score, consume):
        for parity, (dst, src) in enumerate(((s_even, s_odd), (s_odd, s_even))):
            @pl.when(jnp.logical_and(cond, s_id % 2 == parity))
            def _(dst=dst, src=src):
                if score:
                    score_tile(near, dst)
                if consume:
                    consume_tile(src)

    both = jnp.logical_and(score_on, consume_on)
    stage(jnp.logical_and(both, j < i), False, True, True)
    stage(jnp.logical_and(both, j == i), True, True, True)
    stage(jnp.logical_not(consume_on), True, True, False)
    stage(jnp.logical_not(score_on), False, False, True)

    @pl.when(jnp.logical_and(consume_on, cj == ci))
    def _():
        o_ref[...] = _diff_finalize((acc_scr[0] / l_scr[0]).T, (acc_scr[1] / l_scr[1]).T,
                                    (lq1, lk1, lq2, lk2), g_ref, lam_init).astype(o_ref.dtype)


def _alibi_slopes():
    return jnp.asarray([2.0 ** (-8.0 * (i + 1) / DIFF_HEADS) for i in range(DIFF_HEADS)], F32)


def _diff_prompt(main, k_new, v_new, lams, subln_g, *, bsz, t, lam_init):
    tq = min(DIFF_TQ, t)
    nq = t // tq
    assert tq % CHUNK == 0 and t <= CHUNK * 255
    kern = functools.partial(_diff_prompt_kernel, tq=tq, lam_init=lam_init)
    qc, kc, vc = COL_DQ_MAIN // LANES, 0, 0
    pairs = [(i, j) for i in range(nq) for j in range(i + 1)]
    pairs.append(pairs[-1])
    i_tab = jnp.asarray([p[0] for p in pairs], jnp.int32)
    j_tab = jnp.asarray([p[1] for p in pairs], jnp.int32)
    zero = lambda b, h, s, it, jt: (0, 0)

    def consumed(tab, s):
        return tab[jnp.maximum(s - 1, 0)]

    grid_spec = pltpu.PrefetchScalarGridSpec(
        num_scalar_prefetch=2,
        grid=(bsz, DIFF_HEADS, len(pairs)),
        in_specs=[
            pl.BlockSpec(memory_space=pltpu.SMEM),
            pl.BlockSpec((tq, LANES), lambda b, h, s, it, jt: (b * nq + it[s], qc + h)),
            pl.BlockSpec((tq, LANES), lambda b, h, s, it, jt: (b * nq + jt[s], kc + h)),
            pl.BlockSpec((tq, LANES), lambda b, h, s, it, jt: (b * nq + consumed(jt, s), vc + h)),
        ] + [pl.BlockSpec((1, DIFF_HEAD_DIM), zero)] * 4 + [pl.BlockSpec((1, HEAD_DIM), zero)],
        out_specs=pl.BlockSpec((tq, LANES), lambda b, h, s, it, jt: (b * nq + consumed(it, s), h)),
        scratch_shapes=[pltpu.VMEM((2, 1, tq), F32), pltpu.VMEM((2, 1, tq), F32),
                        pltpu.VMEM((2, HEAD_DIM, tq), F32), pltpu.VMEM((2, tq, LANES), BF16),
                        pltpu.VMEM((2, tq, tq), F32), pltpu.VMEM((2, tq, tq), F32)],
    )
    return pl.pallas_call(
        kern,
        out_shape=jax.ShapeDtypeStruct((bsz * t, DIFF_HEADS * HEAD_DIM), BF16),
        grid_spec=grid_spec,
        compiler_params=_cparams(("parallel", "parallel", "arbitrary")),
        name="diff_attn_prompt",
    )(i_tab, j_tab, _alibi_slopes(), main, k_new, v_new, *lams, subln_g.reshape(1, -1))


def _diff_sample_kernel(slopes_ref, q_ref, kn_ref, vn_ref, kp_ref, vp_ref, lq1, lk1, lq2, lk2, g_ref, o_ref,
                        m_scr, l_scr, acc_scr, lhs_scr, *, t, tk, past, lam_init):
    j = pl.program_id(1)
    heads = range(DIFF_HEADS)

    def cols(h):
        return slice(h * HEAD_DIM, (h + 1) * HEAD_DIM)

    @pl.when(j == 0)
    def _():
        _diff_init(m_scr, l_scr, acc_scr)
        lane = lax.broadcasted_iota(jnp.int32, (t, LANES), 1)
        r = lax.broadcasted_iota(jnp.int32, (t, t), 0)
        c = lax.broadcasted_iota(jnp.int32, (t, t), 1)
        scores = []
        for h in heads:
            qs = q_ref[:, cols(h)] * (DIFF_HEAD_DIM ** -0.5)
            lhs_scr[h, 0:t, :] = jnp.where(lane < DIFF_HEAD_DIM, qs, 0.0).astype(BF16)
            lhs_scr[h, t:2 * t, :] = jnp.where(lane >= DIFF_HEAD_DIM, qs, 0.0).astype(BF16)
            bias = _near_bias(slopes_ref[h], r, past + r, past + c)
            scores.append(_dot_nt(lhs_scr[h], kn_ref[:, cols(h)].astype(BF16)) + jnp.concatenate([bias, bias], axis=0))
        _softmax_update(m_scr, l_scr, acc_scr, scores, [vn_ref[:, cols(h)].astype(BF16) for h in heads])

    d = (past - j * tk - lax.broadcasted_iota(jnp.int32, (1, tk), 1)).astype(F32)
    kt = jnp.swapaxes(kp_ref[...], 0, 1).astype(BF16)
    vt = jnp.swapaxes(vp_ref[...], 0, 1).astype(BF16)
    scores = [_dot_nt(lhs_scr[h], kt[h]) - slopes_ref[h] * d for h in heads]
    _softmax_update(m_scr, l_scr, acc_scr, scores, [vt[h] for h in heads])

    @pl.when(j == pl.num_programs(1) - 1)
    def _():
        for h in heads:
            o = acc_scr[h] / l_scr[h]
            o_ref[:, cols(h)] = _diff_finalize(o[:t], o[t:], (lq1, lk1, lq2, lk2), g_ref, lam_init).astype(o_ref.dtype)


def _diff_sample(main, k_new, v_new, k_past, v_past, lams, subln_g, *, bsz, t, lam_init):
    past = k_past.shape[1]
    tk = min(512, past)
    nk = past // tk
    width = DIFF_HEADS * HEAD_DIM
    kern = functools.partial(_diff_sample_kernel, t=t, tk=tk, past=past, lam_init=lam_init)
    zero = lambda b, j: (0, 0)
    cache = pl.BlockSpec((None, tk, DIFF_HEADS, HEAD_DIM), lambda b, j: (b, j, 0, 0))
    return pl.pallas_call(
        kern,
        out_shape=jax.ShapeDtypeStruct((bsz * t, width), BF16),
        grid=(bsz, nk),
        in_specs=[
            pl.BlockSpec(memory_space=pltpu.SMEM),
            pl.BlockSpec((t, width), lambda b, j: (b, COL_DQ_MAIN // width)),
            pl.BlockSpec((t, width), lambda b, j: (b, 0)),
            pl.BlockSpec((t, width), lambda b, j: (b, 0)),
            cache, cache,
        ] + [pl.BlockSpec((1, DIFF_HEAD_DIM), zero)] * 4 + [pl.BlockSpec((1, HEAD_DIM), zero)],
        out_specs=pl.BlockSpec((t, width), lambda b, j: (b, 0)),
        scratch_shapes=[pltpu.VMEM((DIFF_HEADS, 2 * t, 1), F32), pltpu.VMEM((DIFF_HEADS, 2 * t, 1), F32),
                        pltpu.VMEM((DIFF_HEADS, 2 * t, HEAD_DIM), F32), pltpu.VMEM((DIFF_HEADS, 2 * t, LANES), BF16)],
        compiler_params=_cparams(("parallel", "arbitrary")),
        name="diff_attn_sample",
    )(_alibi_slopes(), main, k_new, v_new, k_past, v_past, *lams, subln_g.reshape(1, -1))


def _merge_kernel(oa_ref, ob_ref, wa_ref, wb_ref, ga_ref, gb_ref, o_ref):
    ya = _dot(oa_ref[...], wa_ref[...])
    yb = _dot(ob_ref[...], wb_ref[...])
    o_ref[...] = (jax.nn.sigmoid(ga_ref[...]) * ya + jax.nn.sigmoid(gb_ref[...]) * yb).astype(o_ref.dtype)


def _merge(o_a, o_b, wa, wb, tail, *, tm):
    m = o_a.shape[0]
    d = wa.shape[1]
    tn = 512
    ga0, gb0 = 0, COL_GB_TAIL // tn
    return pl.pallas_call(
        _merge_kernel,
        out_shape=jax.ShapeDtypeStruct((m, d), BF16),
        grid=(m // tm, d // tn),
        in_specs=[
            pl.BlockSpec((tm, o_a.shape[1]), lambda i, j: (i, 0)),
            pl.BlockSpec((tm, o_b.shape[1]), lambda i, j: (i, 0)),
            pl.BlockSpec((wa.shape[0], tn), lambda i, j: (0, j)),
            pl.BlockSpec((wb.shape[0], tn), lambda i, j: (0, j)),
            pl.BlockSpec((tm, tn), lambda i, j: (i, ga0 + j)),
            pl.BlockSpec((tm, tn), lambda i, j: (i, gb0 + j)),
        ],
        out_specs=pl.BlockSpec((tm, tn), lambda i, j: (i, j)),
        compiler_params=_cparams(("parallel", "arbitrary")),
        name="merge",
    )(o_a, o_b, wa, wb, tail, tail)


def _cross_kernel(q_ref, k_ref, v_ref, o_ref):
    dh = q_ref.shape[-1] // MEM_HEADS
    for h in range(MEM_HEADS):
        sl = slice(h * dh, (h + 1) * dh)
        s = _dot_nt(q_ref[:, sl], k_ref[0, :, sl].astype(BF16)) * (dh ** -0.5)
        s = s - jnp.max(s, axis=-1, keepdims=True)
        p = jnp.exp(s)
        p = p / jnp.sum(p, axis=-1, keepdims=True)
        o_ref[:, sl] = _dot(p.astype(BF16), v_ref[0, :, sl].astype(BF16)).astype(o_ref.dtype)


def _cross_attend(q, mem_k, mem_v, *, bsz, t, tm):
    d = q.shape[1]
    nt = t // tm
    mt = mem_k.shape[1]
    return pl.pallas_call(
        _cross_kernel,
        out_shape=jax.ShapeDtypeStruct(q.shape, BF16),
        grid=(bsz, nt),
        in_specs=[
            pl.BlockSpec((tm, d), lambda b, i: (b * nt + i, 0)),
            pl.BlockSpec((1, mt, d), lambda b, i: (b, 0, 0)),
            pl.BlockSpec((1, mt, d), lambda b, i: (b, 0, 0)),
        ],
        out_specs=pl.BlockSpec((tm, d), lambda b, i: (b * nt + i, 0)),
        compiler_params=_cparams(("parallel", "arbitrary")),
        name="cross_attn",
    )(q, mem_k, mem_v)


def _topk_rows(work, cidx, k):
    n, tn = work.shape
    rid = lax.broadcasted_iota(jnp.int32, (n, tn), 0)
    kid = lax.broadcasted_iota(jnp.int32, (k, tn), 0)
    vals = jnp.zeros((k, tn), F32)
    idxs = jnp.zeros((k, tn), jnp.int32)
    for t in range(k):
        m = jnp.max(work, axis=0, keepdims=True)
        pos = jnp.min(jnp.where(work == m, rid, n), axis=0, keepdims=True)
        hit = rid == pos
        if cidx is None:
            picked = pos
        else:
            picked = jnp.sum(jnp.where(hit, cidx, 0), axis=0, keepdims=True)
        vals = jnp.where(kid == t, m, vals)
        idxs = jnp.where(kid == t, picked, idxs)
        work = jnp.where(hit, -jnp.inf, work)
    return vals, idxs


def _product_candidates(v1, i1, v2, i2, k):
    assert k == 2 * SUBLANES
    sub = lax.broadcasted_iota(jnp.int32, (SUBLANES, v1.shape[1]), 0)
    vals, idxs = [], []
    for a in range(k // 2):
        nb = k // (a + 1)
        width = k if nb > SUBLANES else SUBLANES
        cv = v1[a:a + 1, :] + v2[:width, :]
        ci = i1[a:a + 1, :] * PEER_N_KEYS + i2[:width, :]
        if nb < SUBLANES:
            cv = jnp.where(sub < nb, cv, -jnp.inf)
        vals.append(cv)
        idxs.append(ci)
    vals.append(v1[k // 2:, :] + v2[0:1, :])
    idxs.append(i1[k // 2:, :] * PEER_N_KEYS + i2[0:1, :])
    return jnp.concatenate(vals, axis=0), jnp.concatenate(idxs, axis=0)


def _peer_route_kernel(q_ref, keys_ref, e_ref, g_ref):
    tn = q_ref.shape[0]
    k = PEER_TOPK
    for h in range(PEER_HEADS):
        sub = []
        for p in range(2):
            c0 = (h * 2 + p) * PEER_N_KEYS
            s = _dot_nt(keys_ref[h, p], q_ref[:, c0:c0 + PEER_N_KEYS], HIGHEST)
            sub.append(_topk_rows(s, None, k))
        (v1, i1), (v2, i2) = sub
        cand, cidx = _product_candidates(v1, i1, v2, i2, k)
        best, eidx = _topk_rows(cand, cidx, k)
        ex = jnp.exp(best - jnp.max(best, axis=0, keepdims=True))
        gate = ex / jnp.sum(ex, axis=0, keepdims=True)
        e_ref[h * k:(h + 1) * k, :] = eidx
        g_ref[h * k:(h + 1) * k, :] = gate


HALF_KEYS = PEER_N_KEYS // 2
BF16_HIGH_MASK = 0xFFFF0000


def _peer_gate_kernel(et_ref, gt_ref, o_ref, e_scr, g_scr):
    tg = et_ref.shape[1]
    e_scr[...] = et_ref[...].T
    g_scr[...] = gt_ref[...].T
    kid = lax.broadcasted_iota(jnp.int32, (PEER_N_KEYS, PEER_N_KEYS), 0)

    def body(n8, carry):
        rows = pl.ds(pl.multiple_of(n8 * SUBLANES, SUBLANES), SUBLANES)
        e8 = e_scr[rows, :]
        g8 = g_scr[rows, :]
        a8 = jnp.right_shift(e8, 7)
        b8 = jnp.bitwise_and(e8, PEER_N_KEYS - 1)
        words = []
        for s in range(SUBLANES):
            at = jnp.where(a8[s:s + 1, :] == kid, g8[s:s + 1, :], 0.0).astype(BF16)
            bt = jnp.where(b8[s:s + 1, :] == kid, 1.0, 0.0).astype(BF16)
            grid = _dot_nt(at, bt)
            bits = pltpu.bitcast(grid.astype(BF16).astype(F32), jnp.uint32)
            words.append(jnp.bitwise_or(jnp.bitwise_and(bits[HALF_KEYS:, :], jnp.uint32(BF16_HIGH_MASK)),
                                        jnp.right_shift(bits[:HALF_KEYS, :], jnp.uint32(16))))
        o_ref[:, rows, :] = jnp.swapaxes(jnp.stack(words, axis=0), 0, 1)
        return carry

    lax.fori_loop(0, tg // SUBLANES, body, 0, unroll=4)


def _peer_route(qp, sub_keys, *, tn):
    n = qp.shape[0]
    rows = PEER_HEADS * PEER_TOPK
    e_t, g_t = pl.pallas_call(
        _peer_route_kernel,
        out_shape=(jax.ShapeDtypeStruct((rows, n), jnp.int32), jax.ShapeDtypeStruct((rows, n), F32)),
        grid=(n // tn,),
        in_specs=[
            pl.BlockSpec((tn, qp.shape[1]), lambda i: (i, 0)),
            pl.BlockSpec(sub_keys.shape, lambda i: (0, 0, 0, 0)),
        ],
        out_specs=(pl.BlockSpec((rows, tn), lambda i: (0, i)), pl.BlockSpec((rows, tn), lambda i: (0, i))),
        compiler_params=_cparams(("parallel",)),
        name="peer_route",
    )(qp, sub_keys)
    return pl.pallas_call(
        _peer_gate_kernel,
        out_shape=jax.ShapeDtypeStruct((HALF_KEYS, n, PEER_N_KEYS), jnp.uint32),
        grid=(n // tn,),
        in_specs=[pl.BlockSpec((rows, tn), lambda i: (0, i)), pl.BlockSpec((rows, tn), lambda i: (0, i))],
        out_specs=pl.BlockSpec((HALF_KEYS, tn, PEER_N_KEYS), lambda i: (0, i, 0)),
        scratch_shapes=[pltpu.VMEM((tn, rows), jnp.int32), pltpu.VMEM((tn, rows), F32)],
        compiler_params=_cparams(("parallel",)),
        name="peer_gates",
    )(e_t, g_t)


def _gelu_tanh(x):
    return 0.5 * x * (1.0 + jnp.tanh(math.sqrt(2.0 / math.pi) * (x + 0.044715 * (x * x * x))))


PEER_BLOCKS_PER_STEP = 4
PEER_TN = 1024


def _peer_expert_kernel(x_ref, gn_ref, gpa_ref, gpb_ref, *rest):
    nblk = PEER_BLOCKS_PER_STEP
    u_refs, v_refs = rest[:nblk], rest[nblk:2 * nblk]
    gf_ref, o_ref, h_scr, acc_scr, coef_scr = rest[2 * nblk:]
    s = pl.program_id(1)
    last = pl.num_programs(1) - 1

    @pl.when(s == 0)
    def _():
        h_scr[...] = _rms(x_ref[...], gn_ref[...]).astype(BF16)
        acc_scr[...] = jnp.zeros(acc_scr.shape, F32)
        coef_scr[...] = jnp.zeros(coef_scr.shape, BF16)

    v_cat = jnp.concatenate([v[...] for v in v_refs], axis=0)
    acc_scr[...] += _dot(coef_scr[(s + 1) % 2], v_cat)

    def unpack(word):
        return (pltpu.bitcast(jnp.left_shift(word, jnp.uint32(16)), F32),
                pltpu.bitcast(jnp.bitwise_and(word, jnp.uint32(BF16_HIGH_MASK)), F32))

    a_lo, a_hi = unpack(gpa_ref[...])
    b_lo, b_hi = unpack(gpb_ref[...])
    u_cat = jnp.concatenate([u[...] for u in u_refs], axis=0)
    hid = _gelu_tanh(_dot_nt(h_scr[...], u_cat))
    coef_scr[s % 2] = (jnp.concatenate([a_lo, b_lo, a_hi, b_hi], axis=1) * hid).astype(BF16)

    @pl.when(s == last)
    def _():
        o_ref[...] = _rms(x_ref[...] + acc_scr[...], gf_ref[...])


def _peer_experts(x, norm_g, gates, u_tab, v_tab, final_g, *, tn):
    n, d = x.shape
    eb = PEER_N_KEYS
    assert u_tab.shape[0] == eb * eb
    nblk = PEER_BLOCKS_PER_STEP
    ns = eb // nblk

    def cur(s):
        return jnp.minimum(s, ns - 1)

    def prev(s):
        return jnp.maximum(s - 1, 0)

    def table(which, q):
        return pl.BlockSpec((eb, d), lambda i, s: (which(s) + q * ns, 0))

    return pl.pallas_call(
        _peer_expert_kernel,
        out_shape=jax.ShapeDtypeStruct((n, d), F32),
        grid=(n // tn, ns + 1),
        in_specs=[
            pl.BlockSpec((tn, d), lambda i, s: (i, 0), pipeline_mode=pl.Buffered(1)),
            pl.BlockSpec((1, d), lambda i, s: (0, 0)),
            pl.BlockSpec((None, tn, eb), lambda i, s: (cur(s), i, 0)),
            pl.BlockSpec((None, tn, eb), lambda i, s: (cur(s) + ns, i, 0)),
        ] + [table(cur, q) for q in range(nblk)] + [table(prev, q) for q in range(nblk)] + [
            pl.BlockSpec((1, d), lambda i, s: (0, 0)),
        ],
        out_specs=pl.BlockSpec((tn, d), lambda i, s: (i, 0), pipeline_mode=pl.Buffered(1)),
        scratch_shapes=[pltpu.VMEM((tn, d), BF16), pltpu.VMEM((tn, d), F32),
                        pltpu.VMEM((2, tn, nblk * eb), BF16)],
        compiler_params=_cparams(("parallel", "arbitrary")),
        name="peer_experts",
    )(x, norm_g.reshape(1, d), gates, gates, *([u_tab] * nblk), *([v_tab] * nblk), final_g.reshape(1, d))


def _layer(x, wts, layer, mem_k, mem_v, conv_prev, gdn_state, k_past, v_past, *, tm):
    bsz, t, d = x.shape
    n = bsz * t
    x2d = x.reshape(n, d)
    main, k_new, v_new, tail = _in_proj(x2d, wts["norm_mix_g"], wts["w_in"], tm=min(IN_TM, n))
    conv_dim = 3 * GDN_WIDTH
    conv_prev8 = jnp.concatenate(
        [jnp.zeros((bsz, SUBLANES - (GDN_CONV - 1), conv_dim), F32), conv_prev.astype(F32)], axis=1)
    o_a, s_new = _gdn(main, tail, conv_prev8, gdn_state.astype(F32), wts["gdn_conv_w"], wts["gdn_a_log"],
                      wts["gdn_dt_bias"], wts["gdn_norm_g"], bsz=bsz, t=t)
    conv_new = main.reshape(bsz, t, -1)[:, t - (GDN_CONV - 1):, :conv_dim]
    lam_init = 0.8 - 0.6 * math.exp(-0.3 * layer)
    lams = tuple(wts[k].reshape(1, -1) for k in ("diff_lambda_q1", "diff_lambda_k1", "diff_lambda_q2", "diff_lambda_k2"))
    if k_past is None:
        o_b = _diff_prompt(main, k_new, v_new, lams, wts["diff_subln_g"], bsz=bsz, t=t, lam_init=lam_init)
    else:
        o_b = _diff_sample(main, k_new, v_new, k_past, v_past, lams, wts["diff_subln_g"], bsz=bsz, t=t,
                           lam_init=lam_init)
    k_rows = k_new.reshape(bsz, t, DIFF_HEADS, HEAD_DIM)
    v_rows = v_new.reshape(bsz, t, DIFF_HEADS, HEAD_DIM)
    merged = _merge(o_a, o_b, wts["w_branch_a"], wts["w_branch_b"], tail, tm=tm)
    x1 = _matmul(merged, wts["w_out"], residual=x2d, tm=tm, tn=MM_TN, name="out_proj")
    qm = _matmul(x1, wts["w_mq"], norm_g=wts["norm_cross_g"], tm=tm, tn=MM_TN, out_dtype=BF16, name="mem_q")
    mt = mem_k.shape[1]
    oc = _cross_attend(qm, mem_k.reshape(bsz, mt, d), mem_v.reshape(bsz, mt, d), bsz=bsz, t=t, tm=min(tm, t))
    x2 = _matmul(oc, wts["w_mo"], residual=x1, tm=tm, tn=MM_TN, name="mem_o")
    qp = _matmul(x2, wts["peer_w_q"], norm_g=wts["norm_ffn_g"], tm=tm, tn=MM_TN, name="peer_q")
    tn = min(PEER_TN, n)
    gates = _peer_route(qp, wts["peer_sub_keys"], tn=LANES)
    y = _peer_experts(x2, wts["norm_ffn_g"], gates, wts["peer_u"], wts["peer_v"], wts["final_norm_g"], tn=tn)
    return y.reshape(bsz, t, d), k_rows, v_rows, s_new, conv_new


def kernel(x_prompt, x_sample, cache_diff_k, cache_diff_v, state_gdn, state_conv, cache_mem_k, cache_mem_v,
           mem_prompt, norm_mix_g, w_in, gdn_conv_w, gdn_a_log, gdn_dt_bias, gdn_norm_g,
           diff_lambda_q1, diff_lambda_k1, diff_lambda_q2, diff_lambda_k2, diff_subln_g,
           w_branch_a, w_branch_b, w_out, norm_cross_g, norm_mem_g, w_mq, w_mk, w_mv, w_mo,
           norm_ffn_g, peer_w_q, peer_sub_keys, peer_u, peer_v, final_norm_g):
    depth = w_in.shape[0]
    assert depth == 1, "final norm is fused into the last layer's PEER kernel"
    l = 0
    wts = {
        "norm_mix_g": norm_mix_g[l], "w_in": _in_proj_weights(w_in[l]), "gdn_conv_w": gdn_conv_w[l],
        "gdn_a_log": gdn_a_log[l], "gdn_dt_bias": gdn_dt_bias[l], "gdn_norm_g": gdn_norm_g[l],
        "diff_lambda_q1": diff_lambda_q1[l], "diff_lambda_k1": diff_lambda_k1[l],
        "diff_lambda_q2": diff_lambda_q2[l], "diff_lambda_k2": diff_lambda_k2[l],
        "diff_subln_g": diff_subln_g[l], "w_branch_a": w_branch_a[l].astype(BF16),
        "w_branch_b": w_branch_b[l].astype(BF16), "w_out": w_out[l].astype(BF16),
        "norm_cross_g": norm_cross_g[l], "w_mq": w_mq[l].astype(BF16), "w_mo": w_mo[l].astype(BF16),
        "norm_ffn_g": norm_ffn_g[l], "peer_w_q": peer_w_q[l].astype(BF16), "peer_sub_keys": peer_sub_keys[l],
        "peer_u": peer_u[l].astype(BF16), "peer_v": peer_v[l].astype(BF16), "final_norm_g": final_norm_g,
    }
    bp, tp, d = x_prompt.shape
    bs, ts, _ = x_sample.shape
    mem2d = mem_prompt.reshape(-1, d)
    tmm = min(512, mem2d.shape[0])
    mem_k = _matmul(mem2d, w_mk[l].astype(BF16), norm_g=norm_mem_g[l], tm=tmm, tn=512, name="mem_k")
    mem_v = _matmul(mem2d, w_mv[l].astype(BF16), norm_g=norm_mem_g[l], tm=tmm, tn=512, name="mem_v")
    mshape = mem_prompt.shape[:2] + (MEM_HEADS, d // MEM_HEADS)
    mem_k = mem_k.reshape(mshape)
    mem_v = mem_v.reshape(mshape)
    conv0 = jnp.zeros((bp, GDN_CONV - 1, 3 * GDN_WIDTH), F32)
    s0 = jnp.zeros((bp, GDN_HEADS, HEAD_DIM, HEAD_DIM), F32)
    yp, pk, pv, ps, pc = _layer(x_prompt, wts, l, mem_k, mem_v, conv0, s0, None, None, tm=min(1024, bp * tp))
    ys, sk, sv, ss, sc = _layer(x_sample, wts, l, cache_mem_k[l], cache_mem_v[l], state_conv[l], state_gdn[l],
                                cache_diff_k[l], cache_diff_v[l], tm=min(256, bs * ts))
    return (yp, ys, pk[None], pv[None], ps[None], pc[None], mem_k[None], mem_v[None],
            sk[None], sv[None], ss[None], sc[None])
```

```python
import functools
import math

import jax
import jax.numpy as jnp
from jax import lax
from jax.experimental import pallas as pl
from jax.experimental.pallas import tpu as pltpu

F32 = jnp.float32
BF16 = jnp.bfloat16
HIGHEST = lax.Precision.HIGHEST

RMS_EPS = 1e-6
CHUNK = 64
CHUNK_SHIFT = 6
DIFF_TQ = 1024
GDN_CHUNKS_PER_STEP = 2
DIFF_SUB = 256
GDN_HEADS = 8
HEAD_DIM = 128
GDN_WIDTH = GDN_HEADS * HEAD_DIM
GDN_CONV = 4
DIFF_HEADS = 8
DIFF_HEAD_DIM = 64
MEM_HEADS = 4
PEER_HEADS = 8
PEER_N_KEYS = 128
PEER_TOPK = 16
LANES = 128
SUBLANES = 8
V7X_VMEM_LIMIT = 52 * 1024 * 1024
MM_TN = 1024


def _cparams(sem, vmem=V7X_VMEM_LIMIT):
    return pltpu.CompilerParams(dimension_semantics=sem, vmem_limit_bytes=vmem)


def _dot(a, b, precision=None):
    return jnp.dot(a, b, preferred_element_type=F32, precision=precision)


def _dot_nt(a, b, precision=None):
    return lax.dot_general(a, b, (((1,), (1,)), ((), ())), preferred_element_type=F32, precision=precision)


def _split_bf16(x):
    hi = x.astype(BF16)
    return hi, (x - hi.astype(F32)).astype(BF16)


def _cat_parts(p, q):
    return jnp.concatenate([p[0], q[0]], axis=0), jnp.concatenate([p[1], q[1]], axis=0)


def _dot3_parts(a_parts, b_parts, dot):
    a_hi, a_lo = a_parts
    b_hi, b_lo = b_parts
    m = a_hi.shape[0]
    both = dot(jnp.concatenate([a_hi, a_lo], axis=0), b_hi)
    return both[:m] + both[m:] + dot(a_hi, b_lo)


def _dot3(a, b):
    return _dot3_parts(_split_bf16(a), _split_bf16(b), _dot)


def _rms(xf, g):
    return xf * lax.rsqrt(jnp.mean(xf * xf, axis=-1, keepdims=True) + RMS_EPS) * g


def _mm_kernel(*refs, has_norm, has_res):
    it = iter(refs)
    x_ref = next(it)
    g_ref = next(it) if has_norm else None
    w_ref = next(it)
    r_ref = next(it) if has_res else None
    o_ref = next(it)
    h_scr = next(it)

    @pl.when(pl.program_id(1) == 0)
    def _():
        xf = x_ref[...].astype(F32)
        if has_norm:
            xf = _rms(xf, g_ref[...])
        h_scr[...] = xf.astype(BF16)

    acc = _dot(h_scr[...], w_ref[...])
    if has_res:
        acc = acc + r_ref[...]
    o_ref[...] = acc.astype(o_ref.dtype)


def _matmul(x, w, *, norm_g=None, residual=None, tm, tn, out_dtype=F32, name="matmul"):
    m, k = x.shape
    n = w.shape[1]
    assert m % tm == 0 and n % tn == 0, (m, n, tm, tn)
    in_specs = [pl.BlockSpec((tm, k), lambda i, j: (i, 0))]
    args = [x]
    if norm_g is not None:
        in_specs.append(pl.BlockSpec((1, k), lambda i, j: (0, 0)))
        args.append(norm_g.reshape(1, k).astype(F32))
    in_specs.append(pl.BlockSpec((k, tn), lambda i, j: (0, j)))
    args.append(w)
    if residual is not None:
        in_specs.append(pl.BlockSpec((tm, tn), lambda i, j: (i, j)))
        args.append(residual)
    return pl.pallas_call(
        functools.partial(_mm_kernel, has_norm=norm_g is not None, has_res=residual is not None),
        out_shape=jax.ShapeDtypeStruct((m, n), out_dtype),
        grid=(m // tm, n // tn),
        in_specs=in_specs,
        out_specs=pl.BlockSpec((tm, tn), lambda i, j: (i, j)),
        scratch_shapes=[pltpu.VMEM((tm, k), BF16)],
        compiler_params=_cparams(("parallel", "arbitrary")),
        name=name,
    )(*args)


IN_TN = 512
IN_TM = 1024
IN_W1_TILES = 8
IN_W2_TILES = 14
IN_MAIN_TILES = 10
IN_K_TILE0 = 10
IN_V_TILE0 = 12
IN_TAIL_TILE0 = 14
IN_TILES = IN_W1_TILES + IN_W2_TILES + 1
MAIN_COLS = IN_MAIN_TILES * IN_TN
TAIL_COLS = (IN_TILES - IN_TAIL_TILE0) * IN_TN
COL_DQ_MAIN = 4096
COL_GB_TAIL = 2048
COL_BA_TAIL = 4096


def _in_proj_kernel(x_ref, g_ref, w1_ref, w2_ref, w3_ref, main_ref, k_ref, v_ref, tail_ref, h_scr):
    j = pl.program_id(1)

    @pl.when(j == 0)
    def _():
        h_scr[...] = _rms(x_ref[...], g_ref[...]).astype(BF16)

    routes = ((0, IN_W1_TILES, w1_ref, main_ref),
              (IN_W1_TILES, IN_K_TILE0, w2_ref, main_ref),
              (IN_K_TILE0, IN_V_TILE0, w2_ref, k_ref),
              (IN_V_TILE0, IN_TAIL_TILE0, w2_ref, v_ref),
              (IN_TAIL_TILE0, IN_TILES - 1, w2_ref, tail_ref),
              (IN_TILES - 1, IN_TILES, w3_ref, tail_ref))
    for lo, hi, w_ref, dst in routes:
        @pl.when(jnp.logical_and(j >= lo, j < hi))
        def _(w_ref=w_ref, dst=dst):
            dst[...] = _dot(h_scr[...], w_ref[...])


def _in_proj_weights(w_in):
    c_z = 4 * GDN_WIDTH
    c_ba = c_z + 2 * GDN_HEADS
    assert c_z == IN_W1_TILES * IN_TN and w_in.shape[1] - c_ba == IN_W2_TILES * IN_TN
    w3 = jnp.pad(w_in[:, c_z:c_ba], ((0, 0), (0, IN_TN - 2 * GDN_HEADS)))
    return w_in[:, :c_z].astype(BF16), w_in[:, c_ba:].astype(BF16), w3.astype(BF16)


def _in_proj(x, norm_g, w_pieces, *, tm):
    m, d = x.shape
    w1, w2, w3 = w_pieces
    tn = IN_TN
    out = lambda cols: jax.ShapeDtypeStruct((m, cols), F32)
    return pl.pallas_call(
        _in_proj_kernel,
        out_shape=(out(MAIN_COLS), out(2 * IN_TN), out(2 * IN_TN), out(TAIL_COLS)),
        grid=(m // tm, IN_TILES),
        in_specs=[
            pl.BlockSpec((tm, d), lambda i, j: (i, 0)),
            pl.BlockSpec((1, d), lambda i, j: (0, 0)),
            pl.BlockSpec((d, tn), lambda i, j: (0, jnp.minimum(j, IN_W1_TILES - 1))),
            pl.BlockSpec((d, tn), lambda i, j: (0, jnp.clip(j - IN_W1_TILES, 0, IN_W2_TILES - 1))),
            pl.BlockSpec((d, tn), lambda i, j: (0, 0)),
        ],
        out_specs=(
            pl.BlockSpec((tm, tn), lambda i, j: (i, jnp.minimum(j, IN_MAIN_TILES - 1))),
            pl.BlockSpec((tm, tn), lambda i, j: (i, jnp.clip(j - IN_K_TILE0, 0, 1))),
            pl.BlockSpec((tm, tn), lambda i, j: (i, jnp.clip(j - IN_V_TILE0, 0, 1))),
            pl.BlockSpec((tm, tn), lambda i, j: (i, jnp.clip(j - IN_TAIL_TILE0, 0, IN_TILES - IN_TAIL_TILE0 - 1))),
        ),
        scratch_shapes=[pltpu.VMEM((tm, d), BF16)],
        compiler_params=_cparams(("parallel", "arbitrary")),
        name="in_proj",
    )(x, norm_g.reshape(1, d).astype(F32), w1, w2, w3)


def _gdn_kernel(qkv_ref, z_ref, ba_ref, prev_ref, s0_ref, convw_ref, alog_ref, dtb_ref, ng_ref,
                o_ref, sfin_ref, s_scr, xbuf, *, L, nch):
    step = pl.program_id(1)
    rows = nch * L

    @pl.when(step == 0)
    def _():
        s_scr[...] = s0_ref[0]
        xbuf[0:SUBLANES, :] = prev_ref[0]

    xbuf[SUBLANES:SUBLANES + rows, :] = qkv_ref[...]
    base = SUBLANES - (GDN_CONV - 1)
    y = xbuf[base:base + rows, :] * convw_ref[0:1, :]
    for i in range(1, GDN_CONV):
        y = y + xbuf[base + i:base + i + rows, :] * convw_ref[i:i + 1, :]
    y = y * jax.nn.sigmoid(y)
    carry = xbuf[rows:rows + SUBLANES, :]
    xbuf[0:SUBLANES, :] = carry

    row = lax.broadcasted_iota(jnp.int32, (L, L), 0)
    col = lax.broadcasted_iota(jnp.int32, (L, L), 1)
    lower = row >= col
    strict = row > col
    eye = (row == col).astype(F32)
    log_l = int(math.log2(L))
    n_double = log_l - 1
    assert 2 ** log_l == L

    ba = ba_ref[...]
    beta_all = jax.nn.sigmoid(ba)
    a_in = ba + dtb_ref[...]
    softplus = jnp.maximum(a_in, 0.0) + jnp.log(1.0 + jnp.exp(-jnp.abs(a_in)))
    g_all = -jnp.exp(alog_ref[...]) * softplus
    rr = lax.broadcasted_iota(jnp.int32, (rows, rows), 0)
    cc = lax.broadcasted_iota(jnp.int32, (rows, rows), 1)
    same_chunk_lower = jnp.logical_and(rr >= cc, jnp.right_shift(rr, log_l) == jnp.right_shift(cc, log_l))
    g_cum = _dot(same_chunk_lower.astype(F32), g_all, HIGHEST)
    g_cum_t = g_cum.T

    heads = range(GDN_HEADS)
    chains = [(c, h) for c in range(nch) for h in heads]
    ids = range(len(chains))

    def tok(c):
        return slice(c * L, (c + 1) * L)

    def head_cols(part, c, h):
        return y[tok(c), part * GDN_WIDTH + h * HEAD_DIM:part * GDN_WIDTH + (h + 1) * HEAD_DIM]

    q = [head_cols(0, c, h) for c, h in chains]
    k = [head_cols(1, c, h) for c, h in chains]
    v = [head_cols(2, c, h) for c, h in chains]
    q = [a * lax.rsqrt(jnp.sum(a * a, axis=-1, keepdims=True) + 1e-6) * (HEAD_DIM ** -0.5) for a in q]
    k = [a * lax.rsqrt(jnp.sum(a * a, axis=-1, keepdims=True) + 1e-6) for a in k]
    beta = [beta_all[tok(c), h:h + 1] for c, h in chains]
    g_col = [g_cum[tok(c), GDN_HEADS + h:GDN_HEADS + h + 1] for c, h in chains]
    g_row = [g_cum_t[GDN_HEADS + h:GDN_HEADS + h + 1, tok(c)] for c, h in chains]
    decay = [jnp.where(lower, jnp.exp(jnp.where(lower, g_col[n] - g_row[n], 0.0)), 0.0) for n in ids]
    g_last = [g_col[n][L - 1:L, :] for n in ids]
    e_col = [jnp.exp(g_col[n]) for n in ids]
    k_parts = [_split_bf16(k[n]) for n in ids]
    qk_kk = [_dot3_parts(_cat_parts(_split_bf16(q[n]), k_parts[n]), k_parts[n], _dot_nt) for n in ids]
    x = [-jnp.where(strict, beta[n] * qk_kk[n][L:] * decay[n], 0.0) for n in ids]
    t_inv = [eye + x[n] for n in ids]
    x_parts = [_split_bf16(x[n]) for n in ids]
    xp = [_dot3_parts(x_parts[n], x_parts[n], _dot) for n in ids]
    for _ in range(n_double - 1):
        xp_parts = [_split_bf16(xp[n]) for n in ids]
        both = [_dot3_parts(_cat_parts(_split_bf16(t_inv[n]), xp_parts[n]), xp_parts[n], _dot) for n in ids]
        t_inv = [t_inv[n] + both[n][:L] for n in ids]
        xp = [both[n][L:] for n in ids]
    t_inv = [t_inv[n] + _dot3(t_inv[n], xp[n]) for n in ids]
    sol = [_dot3(t_inv[n], jnp.concatenate([v[n] * beta[n], k[n] * (beta[n] * e_col[n])], axis=-1)) for n in ids]
    qk = [jnp.where(lower, qk_kk[n][:L] * decay[n], 0.0) for n in ids]
    k_dec_t = [(k[n] * jnp.exp(g_last[n] - g_col[n])).T for n in ids]
    left_s = [_cat_parts(_split_bf16(sol[n][:, HEAD_DIM:]), _split_bf16(q[n] * e_col[n])) for n in ids]
    left_v = [_cat_parts(_split_bf16(qk[n]), _split_bf16(k_dec_t[n])) for n in ids]

    s = [s_scr[h] for h in heads]
    for c in range(nch):
        n0 = c * GDN_HEADS
        ws_qs = [_dot3_parts(left_s[n0 + h], _split_bf16(s[h]), _dot) for h in heads]
        v_new = [sol[n0 + h][:, :HEAD_DIM] - ws_qs[h][:L] for h in heads]
        tail = [_dot3_parts(left_v[n0 + h], _split_bf16(v_new[h]), _dot) for h in heads]
        s = [s[h] * jnp.exp(g_last[n0 + h]) + tail[h][L:] for h in heads]
        for h in heads:
            zf = z_ref[tok(c), h * HEAD_DIM:(h + 1) * HEAD_DIM]
            o = _rms(ws_qs[h][L:] + tail[h][:L], ng_ref[...]) * (zf * jax.nn.sigmoid(zf))
            o_ref[tok(c), h * HEAD_DIM:(h + 1) * HEAD_DIM] = o.astype(o_ref.dtype)
    for h in heads:
        s_scr[h] = s[h]

    @pl.when(step == pl.num_programs(1) - 1)
    def _():
        sfin_ref[0] = s_scr[...]


def _decay_lanes(p):
    return jnp.zeros((1, LANES), F32).at[0, GDN_HEADS:2 * GDN_HEADS].set(p.astype(F32))


def _gdn(main, tail, conv_prev8, s0, conv_w, a_log, dt_bias, norm_g, *, bsz, t):
    L = min(CHUNK, t)
    nch = GDN_CHUNKS_PER_STEP if (t // L) % GDN_CHUNKS_PER_STEP == 0 else 1
    nc = t // (L * nch)
    rows = L * nch
    conv_dim = 3 * GDN_WIDTH
    kern = functools.partial(_gdn_kernel, L=L, nch=nch)
    o, s_fin = pl.pallas_call(
        kern,
        out_shape=(jax.ShapeDtypeStruct((bsz * t, GDN_WIDTH), BF16),
                   jax.ShapeDtypeStruct((bsz, GDN_HEADS, HEAD_DIM, HEAD_DIM), F32)),
        grid=(bsz, nc),
        in_specs=[
            pl.BlockSpec((rows, conv_dim), lambda b, c: (b * nc + c, 0)),
            pl.BlockSpec((rows, GDN_WIDTH), lambda b, c: (b * nc + c, conv_dim // GDN_WIDTH)),
            pl.BlockSpec((rows, LANES), lambda b, c: (b * nc + c, COL_BA_TAIL // LANES)),
            pl.BlockSpec((1, SUBLANES, conv_dim), lambda b, c: (b, 0, 0)),
            pl.BlockSpec((1, GDN_HEADS, HEAD_DIM, HEAD_DIM), lambda b, c: (b, 0, 0, 0)),
            pl.BlockSpec((GDN_CONV, conv_dim), lambda b, c: (0, 0)),
            pl.BlockSpec((1, LANES), lambda b, c: (0, 0)),
            pl.BlockSpec((1, LANES), lambda b, c: (0, 0)),
            pl.BlockSpec((1, HEAD_DIM), lambda b, c: (0, 0)),
        ],
        out_specs=(pl.BlockSpec((rows, GDN_WIDTH), lambda b, c: (b * nc + c, 0)),
                   pl.BlockSpec((1, GDN_HEADS, HEAD_DIM, HEAD_DIM), lambda b, c: (b, 0, 0, 0))),
        scratch_shapes=[pltpu.VMEM((GDN_HEADS, HEAD_DIM, HEAD_DIM), F32),
                        pltpu.VMEM((SUBLANES + rows + SUBLANES, conv_dim), F32)],
        compiler_params=_cparams(("parallel", "arbitrary")),
        name="gdn",
    )(main, main, tail, conv_prev8, s0, conv_w, _decay_lanes(a_log), _decay_lanes(dt_bias),
      norm_g.reshape(1, -1))
    return o, s_fin


def _lambda_value(lam_refs, lam_init):
    lq1, lk1, lq2, lk2 = (r[...] for r in lam_refs)
    return (jnp.exp(jnp.sum(lq1 * lk1, axis=-1, keepdims=True))
            - jnp.exp(jnp.sum(lq2 * lk2, axis=-1, keepdims=True)) + lam_init)


def _softmax_update(m_scr, l_scr, acc_scr, scores, values):
    idx = range(len(scores))
    m_old = [m_scr[n] for n in idx]
    m_new = [jnp.maximum(m_old[n], jnp.max(scores[n], axis=-1, keepdims=True)) for n in idx]
    p = [jnp.exp(scores[n] - m_new[n]) for n in idx]
    alpha = [jnp.exp(m_old[n] - m_new[n]) for n in idx]
    pv = [_dot(p[n].astype(BF16), values[n]) for n in idx]
    for n in idx:
        l_scr[n] = alpha[n] * l_scr[n] + jnp.sum(p[n], axis=-1, keepdims=True)
        acc_scr[n] = alpha[n] * acc_scr[n] + pv[n]
        m_scr[n] = m_new[n]


def _diff_finalize(o1, o2, lam_refs, g_ref, lam_init):
    lam = _lambda_value(lam_refs, lam_init)
    return _rms(o1 - lam * o2, g_ref[...]) * (1.0 - lam_init)


def _diff_init(m_scr, l_scr, acc_scr):
    m_scr[...] = jnp.full(m_scr.shape, -jnp.inf, F32)
    l_scr[...] = jnp.zeros(l_scr.shape, F32)
    acc_scr[...] = jnp.zeros(acc_scr.shape, F32)


def _near_bias(slope, r, q_pos, k_pos):
    visible = jnp.right_shift(k_pos, CHUNK_SHIFT) <= jnp.right_shift(q_pos, CHUNK_SHIFT)
    return jnp.where(visible, slope * (r - jnp.abs(q_pos - k_pos)).astype(F32), -jnp.inf)


def _split_distance(slope, d):
    return (-(slope * CHUNK) * jnp.right_shift(d, CHUNK_SHIFT).astype(F32),
            -slope * jnp.bitwise_and(d, CHUNK - 1).astype(F32))


def _diff_prompt_kernel(i_tab, j_tab, slopes_ref, q_ref, k_ref, v_ref, lq1, lk1, lq2, lk2, g_ref, o_ref,
                        m_scr, l_scr, acc_scr, lhs_scr, s_even, s_odd, *, tq, lam_init):
    h = pl.program_id(1)
    s_id = pl.program_id(2)
    n_pairs = pl.num_programs(2) - 1
    score_on = s_id < n_pairs
    consume_on = s_id >= 1
    i = i_tab[s_id]
    j = j_tab[s_id]
    ci = i_tab[jnp.maximum(s_id - 1, 0)]
    cj = j_tab[jnp.maximum(s_id - 1, 0)]
    slope = slopes_ref[h]
    half = DIFF_HEAD_DIM
    lane = lax.broadcasted_iota(jnp.int32, (tq, LANES), 1)

    @pl.when(jnp.logical_and(consume_on, cj == 0))
    def _():
        _diff_init(m_scr, l_scr, acc_scr)

    @pl.when(jnp.logical_and(score_on, j == 0))
    def _():
        qs = q_ref[...] * (DIFF_HEAD_DIM ** -0.5)
        lhs_scr[0] = jnp.where(lane < half, qs, 1.0).astype(BF16)
        lhs_scr[1] = jnp.where(lane >= half, qs, 1.0).astype(BF16)

    sub = min(DIFF_SUB, tq)
    units = [(n, qb) for qb in range(tq // sub) for n in range(2)]

    def score_tile(near, dst):
        kb = k_ref[...]
        if near:
            aug0 = aug1 = jnp.zeros_like(kb)
        else:
            d = (i - j) * tq - lax.broadcasted_iota(jnp.int32, (tq, LANES), 0)
            t_hi, t_lo = _split_distance(slope, d)
            aug0 = jnp.where(lane == half, t_hi, jnp.where(lane == half + 1, t_lo, 0.0))
            aug1 = jnp.where(lane == 0, t_hi, jnp.where(lane == 1, t_lo, 0.0))
        keys = [jnp.where(lane < half, kb, aug0).astype(BF16), jnp.where(lane >= half, kb, aug1).astype(BF16)]
        bias = None
        for n, qb in units:
            cols = slice(qb * sub, (qb + 1) * sub)
            if near:
                live = (qb + 1) * sub
                if n == 0:
                    k_pos = lax.broadcasted_iota(jnp.int32, (live, sub), 0)
                    q_pos = qb * sub + lax.broadcasted_iota(jnp.int32, (live, sub), 1)
                    bias = _near_bias(slope, q_pos, q_pos, k_pos)
                dst[n, :live, cols] = _dot_nt(keys[n][:live], lhs_scr[n, cols, :]) + bias
                if live < tq:
                    dst[n, live:, cols] = jnp.full((tq - live, sub), -jnp.inf, F32)
            else:
                dst[n, :, cols] = _dot_nt(keys[n], lhs_scr[n, cols, :])

    def consume_tile(src):
        vt = v_ref[...].T.astype(BF16)
        idx = range(len(units))
        cols = [slice(qb * sub, (qb + 1) * sub) for _, qb in units]
        s = [src[n, :, cols[u]] for u, (n, _) in enumerate(units)]
        m_old = [m_scr[n, :, cols[u]] for u, (n, _) in enumerate(units)]
        m_new = [jnp.maximum(m_old[u], jnp.max(s[u], axis=0, keepdims=True)) for u in idx]
        p = [jnp.exp(s[u] - m_new[u]) for u in idx]
        alpha = [jnp.exp(m_old[u] - m_new[u]) for u in idx]
        pv = [_dot(vt, p[u].astype(BF16)) for u in idx]
        for u, (n, _) in enumerate(units):
            l_scr[n, :, cols[u]] = alpha[u] * l_scr[n, :, cols[u]] + jnp.sum(p[u], axis=0, keepdims=True)
            acc_scr[n, :, cols[u]] = alpha[u] * acc_scr[n, :, cols[u]] + pv[u]
            m_scr[n, :, cols[u]] = m_new[u]

    def stage(cond, near, score, consume):
        for parity, (dst, src) in enumerate(((s_even, s_odd), (s_odd, s_even))):
            @pl.when(jnp.logical_and(cond, s_id % 2 == parity))
            def _(dst=dst, src=src):
                if score:
                    score_tile(near, dst)
                if consume:
                    consume_tile(src)

    both = jnp.logical_and(score_on, consume_on)
    stage(jnp.logical_and(both, j < i), False, True, True)
    stage(jnp.logical_and(both, j == i), True, True, True)
    stage(jnp.logical_not(consume_on), True, True, False)
    stage(jnp.logical_not(score_on), False, False, True)

    @pl.when(jnp.logical_and(consume_on, cj == ci))
    def _():
        o_ref[...] = _diff_finalize((acc_scr[0] / l_scr[0]).T, (acc_scr[1] / l_scr[1]).T,
                                    (lq1, lk1, lq2, lk2), g_ref, lam_init).astype(o_ref.dtype)


def _alibi_slopes():
    return jnp.asarray([2.0 ** (-8.0 * (i + 1) / DIFF_HEADS) for i in range(DIFF_HEADS)], F32)


def _diff_prompt(main, k_new, v_new, lams, subln_g, *, bsz, t, lam_init):
    tq = min(DIFF_TQ, t)
    nq = t // tq
    assert tq % CHUNK == 0 and t <= CHUNK * 255
    kern = functools.partial(_diff_prompt_kernel, tq=tq, lam_init=lam_init)
    qc, kc, vc = COL_DQ_MAIN // LANES, 0, 0
    pairs = [(i, j) for i in range(nq) for j in range(i + 1)]
    pairs.append(pairs[-1])
    i_tab = jnp.asarray([p[0] for p in pairs], jnp.int32)
    j_tab = jnp.asarray([p[1] for p in pairs], jnp.int32)
    zero = lambda b, h, s, it, jt: (0, 0)

    def consumed(tab, s):
        return tab[jnp.maximum(s - 1, 0)]

    grid_spec = pltpu.PrefetchScalarGridSpec(
        num_scalar_prefetch=2,
        grid=(bsz, DIFF_HEADS, len(pairs)),
        in_specs=[
            pl.BlockSpec(memory_space=pltpu.SMEM),
            pl.BlockSpec((tq, LANES), lambda b, h, s, it, jt: (b * nq + it[s], qc + h)),
            pl.BlockSpec((tq, LANES), lambda b, h, s, it, jt: (b * nq + jt[s], kc + h)),
            pl.BlockSpec((tq, LANES), lambda b, h, s, it, jt: (b * nq + consumed(jt, s), vc + h)),
        ] + [pl.BlockSpec((1, DIFF_HEAD_DIM), zero)] * 4 + [pl.BlockSpec((1, HEAD_DIM), zero)],
        out_specs=pl.BlockSpec((tq, LANES), lambda b, h, s, it, jt: (b * nq + consumed(it, s), h)),
        scratch_shapes=[pltpu.VMEM((2, 1, tq), F32), pltpu.VMEM((2, 1, tq), F32),
                        pltpu.VMEM((2, HEAD_DIM, tq), F32), pltpu.VMEM((2, tq, LANES), BF16),
                        pltpu.VMEM((2, tq, tq), F32), pltpu.VMEM((2, tq, tq), F32)],
    )
    return pl.pallas_call(
        kern,
        out_shape=jax.ShapeDtypeStruct((bsz * t, DIFF_HEADS * HEAD_DIM), BF16),
        grid_spec=grid_spec,
        compiler_params=_cparams(("parallel", "parallel", "arbitrary")),
        name="diff_attn_prompt",
    )(i_tab, j_tab, _alibi_slopes(), main, k_new, v_new, *lams, subln_g.reshape(1, -1))


def _diff_sample_kernel(slopes_ref, q_ref, kn_ref, vn_ref, kp_ref, vp_ref, lq1, lk1, lq2, lk2, g_ref, o_ref,
                        m_scr, l_scr, acc_scr, lhs_scr, *, t, tk, past, lam_init):
    j = pl.program_id(1)
    heads = range(DIFF_HEADS)

    def cols(h):
        return slice(h * HEAD_DIM, (h + 1) * HEAD_DIM)

    @pl.when(j == 0)
    def _():
        _diff_init(m_scr, l_scr, acc_scr)
        lane = lax.broadcasted_iota(jnp.int32, (t, LANES), 1)
        r = lax.broadcasted_iota(jnp.int32, (t, t), 0)
        c = lax.broadcasted_iota(jnp.int32, (t, t), 1)
        scores = []
        for h in heads:
            qs = q_ref[:, cols(h)] * (DIFF_HEAD_DIM ** -0.5)
            lhs_scr[h, 0:t, :] = jnp.where(lane < DIFF_HEAD_DIM, qs, 0.0).astype(BF16)
            lhs_scr[h, t:2 * t, :] = jnp.where(lane >= DIFF_HEAD_DIM, qs, 0.0).astype(BF16)
            bias = _near_bias(slopes_ref[h], r, past + r, past + c)
            scores.append(_dot_nt(lhs_scr[h], kn_ref[:, cols(h)].astype(BF16)) + jnp.concatenate([bias, bias], axis=0))
        _softmax_update(m_scr, l_scr, acc_scr, scores, [vn_ref[:, cols(h)].astype(BF16) for h in heads])

    d = (past - j * tk - lax.broadcasted_iota(jnp.int32, (1, tk), 1)).astype(F32)
    kt = jnp.swapaxes(kp_ref[...], 0, 1).astype(BF16)
    vt = jnp.swapaxes(vp_ref[...], 0, 1).astype(BF16)
    scores = [_dot_nt(lhs_scr[h], kt[h]) - slopes_ref[h] * d for h in heads]
    _softmax_update(m_scr, l_scr, acc_scr, scores, [vt[h] for h in heads])

    @pl.when(j == pl.num_programs(1) - 1)
    def _():
        for h in heads:
            o = acc_scr[h] / l_scr[h]
            o_ref[:, cols(h)] = _diff_finalize(o[:t], o[t:], (lq1, lk1, lq2, lk2), g_ref, lam_init).astype(o_ref.dtype)


def _diff_sample(main, k_new, v_new, k_past, v_past, lams, subln_g, *, bsz, t, lam_init):
    past = k_past.shape[1]
    tk = min(512, past)
    nk = past // tk
    width = DIFF_HEADS * HEAD_DIM
    kern = functools.partial(_diff_sample_kernel, t=t, tk=tk, past=past, lam_init=lam_init)
    zero = lambda b, j: (0, 0)
    cache = pl.BlockSpec((None, tk, DIFF_HEADS, HEAD_DIM), lambda b, j: (b, j, 0, 0))
    return pl.pallas_call(
        kern,
        out_shape=jax.ShapeDtypeStruct((bsz * t, width), BF16),
        grid=(bsz, nk),
        in_specs=[
            pl.BlockSpec(memory_space=pltpu.SMEM),
            pl.BlockSpec((t, width), lambda b, j: (b, COL_DQ_MAIN // width)),
            pl.BlockSpec((t, width), lambda b, j: (b, 0)),
            pl.BlockSpec((t, width), lambda b, j: (b, 0)),
            cache, cache,
        ] + [pl.BlockSpec((1, DIFF_HEAD_DIM), zero)] * 4 + [pl.BlockSpec((1, HEAD_DIM), zero)],
        out_specs=pl.BlockSpec((t, width), lambda b, j: (b, 0)),
        scratch_shapes=[pltpu.VMEM((DIFF_HEADS, 2 * t, 1), F32), pltpu.VMEM((DIFF_HEADS, 2 * t, 1), F32),
                        pltpu.VMEM((DIFF_HEADS, 2 * t, HEAD_DIM), F32), pltpu.VMEM((DIFF_HEADS, 2 * t, LANES), BF16)],
        compiler_params=_cparams(("parallel", "arbitrary")),
        name="diff_attn_sample",
    )(_alibi_slopes(), main, k_new, v_new, k_past, v_past, *lams, subln_g.reshape(1, -1))


def _merge_kernel(oa_ref, ob_ref, wa_ref, wb_ref, ga_ref, gb_ref, o_ref):
    ya = _dot(oa_ref[...], wa_ref[...])
    yb = _dot(ob_ref[...], wb_ref[...])
    o_ref[...] = (jax.nn.sigmoid(ga_ref[...]) * ya + jax.nn.sigmoid(gb_ref[...]) * yb).astype(o_ref.dtype)


def _merge(o_a, o_b, wa, wb, tail, *, tm):
    m = o_a.shape[0]
    d = wa.shape[1]
    tn = 512
    ga0, gb0 = 0, COL_GB_TAIL // tn
    return pl.pallas_call(
        _merge_kernel,
        out_shape=jax.ShapeDtypeStruct((m, d), BF16),
        grid=(m // tm, d // tn),
        in_specs=[
            pl.BlockSpec((tm, o_a.shape[1]), lambda i, j: (i, 0)),
            pl.BlockSpec((tm, o_b.shape[1]), lambda i, j: (i, 0)),
            pl.BlockSpec((wa.shape[0], tn), lambda i, j: (0, j)),
            pl.BlockSpec((wb.shape[0], tn), lambda i, j: (0, j)),
            pl.BlockSpec((tm, tn), lambda i, j: (i, ga0 + j)),
            pl.BlockSpec((tm, tn), lambda i, j: (i, gb0 + j)),
        ],
        out_specs=pl.BlockSpec((tm, tn), lambda i, j: (i, j)),
        compiler_params=_cparams(("parallel", "arbitrary")),
        name="merge",
    )(o_a, o_b, wa, wb, tail, tail)


def _cross_kernel(q_ref, k_ref, v_ref, o_ref):
    dh = q_ref.shape[-1] // MEM_HEADS
    for h in range(MEM_HEADS):
        sl = slice(h * dh, (h + 1) * dh)
        s = _dot_nt(q_ref[:, sl], k_ref[0, :, sl].astype(BF16)) * (dh ** -0.5)
        s = s - jnp.max(s, axis=-1, keepdims=True)
        p = jnp.exp(s)
        p = p / jnp.sum(p, axis=-1, keepdims=True)
        o_ref[:, sl] = _dot(p.astype(BF16), v_ref[0, :, sl].astype(BF16)).astype(o_ref.dtype)


def _cross_attend(q, mem_k, mem_v, *, bsz, t, tm):
    d = q.shape[1]
    nt = t // tm
    mt = mem_k.shape[1]
    return pl.pallas_call(
        _cross_kernel,
        out_shape=jax.ShapeDtypeStruct(q.shape, BF16),
        grid=(bsz, nt),
        in_specs=[
            pl.BlockSpec((tm, d), lambda b, i: (b * nt + i, 0)),
            pl.BlockSpec((1, mt, d), lambda b, i: (b, 0, 0)),
            pl.BlockSpec((1, mt, d), lambda b, i: (b, 0, 0)),
        ],
        out_specs=pl.BlockSpec((tm, d), lambda b, i: (b * nt + i, 0)),
        compiler_params=_cparams(("parallel", "arbitrary")),
        name="cross_attn",
    )(q, mem_k, mem_v)


def _topk_rows(work, cidx, k):
    n, tn = work.shape
    rid = lax.broadcasted_iota(jnp.int32, (n, tn), 0)
    kid = lax.broadcasted_iota(jnp.int32, (k, tn), 0)
    vals = jnp.zeros((k, tn), F32)
    idxs = jnp.zeros((k, tn), jnp.int32)
    for t in range(k):
        m = jnp.max(work, axis=0, keepdims=True)
        pos = jnp.min(jnp.where(work == m, rid, n), axis=0, keepdims=True)
        hit = rid == pos
        if cidx is None:
            picked = pos
        else:
            picked = jnp.sum(jnp.where(hit, cidx, 0), axis=0, keepdims=True)
        vals = jnp.where(kid == t, m, vals)
        idxs = jnp.where(kid == t, picked, idxs)
        work = jnp.where(hit, -jnp.inf, work)
    return vals, idxs


def _product_candidates(v1, i1, v2, i2, k):
    assert k == 2 * SUBLANES
    sub = lax.broadcasted_iota(jnp.int32, (SUBLANES, v1.shape[1]), 0)
    vals, idxs = [], []
    for a in range(k // 2):
        nb = k // (a + 1)
        width = k if nb > SUBLANES else SUBLANES
        cv = v1[a:a + 1, :] + v2[:width, :]
        ci = i1[a:a + 1, :] * PEER_N_KEYS + i2[:width, :]
        if nb < SUBLANES:
            cv = jnp.where(sub < nb, cv, -jnp.inf)
        vals.append(cv)
        idxs.append(ci)
    vals.append(v1[k // 2:, :] + v2[0:1, :])
    idxs.append(i1[k // 2:, :] * PEER_N_KEYS + i2[0:1, :])
    return jnp.concatenate(vals, axis=0), jnp.concatenate(idxs, axis=0)


def _peer_route_kernel(q_ref, keys_ref, e_ref, g_ref):
    tn = q_ref.shape[0]
    k = PEER_TOPK
    for h in range(PEER_HEADS):
        sub = []
        for p in range(2):
            c0 = (h * 2 + p) * PEER_N_KEYS
            s = _dot_nt(keys_ref[h, p], q_ref[:, c0:c0 + PEER_N_KEYS], HIGHEST)
            sub.append(_topk_rows(s, None, k))
        (v1, i1), (v2, i2) = sub
        cand, cidx = _product_candidates(v1, i1, v2, i2, k)
        best, eidx = _topk_rows(cand, cidx, k)
        ex = jnp.exp(best - jnp.max(best, axis=0, keepdims=True))
        gate = ex / jnp.sum(ex, axis=0, keepdims=True)
        e_ref[h * k:(h + 1) * k, :] = eidx
        g_ref[h * k:(h + 1) * k, :] = gate


HALF_KEYS = PEER_N_KEYS // 2
BF16_HIGH_MASK = 0xFFFF0000


def _peer_gate_kernel(et_ref, gt_ref, o_ref, e_scr, g_scr):
    tg = et_ref.shape[1]
    e_scr[...] = et_ref[...].T
    g_scr[...] = gt_ref[...].T
    kid = lax.broadcasted_iota(jnp.int32, (PEER_N_KEYS, PEER_N_KEYS), 0)

    def body(n8, carry):
        rows = pl.ds(pl.multiple_of(n8 * SUBLANES, SUBLANES), SUBLANES)
        e8 = e_scr[rows, :]
        g8 = g_scr[rows, :]
        a8 = jnp.right_shift(e8, 7)
        b8 = jnp.bitwise_and(e8, PEER_N_KEYS - 1)
        words = []
        for s in range(SUBLANES):
            at = jnp.where(a8[s:s + 1, :] == kid, g8[s:s + 1, :], 0.0).astype(BF16)
            bt = jnp.where(b8[s:s + 1, :] == kid, 1.0, 0.0).astype(BF16)
            grid = _dot_nt(at, bt)
            bits = pltpu.bitcast(grid.astype(BF16).astype(F32), jnp.uint32)
            words.append(jnp.bitwise_or(jnp.bitwise_and(bits[HALF_KEYS:, :], jnp.uint32(BF16_HIGH_MASK)),
                                        jnp.right_shift(bits[:HALF_KEYS, :], jnp.uint32(16))))
        o_ref[:, rows, :] = jnp.swapaxes(jnp.stack(words, axis=0), 0, 1)
        return carry

    lax.fori_loop(0, tg // SUBLANES, body, 0, unroll=4)


def _peer_route(qp, sub_keys, *, tn):
    n = qp.shape[0]
    rows = PEER_HEADS * PEER_TOPK
    e_t, g_t = pl.pallas_call(
        _peer_route_kernel,
        out_shape=(jax.ShapeDtypeStruct((rows, n), jnp.int32), jax.ShapeDtypeStruct((rows, n), F32)),
        grid=(n // tn,),
        in_specs=[
            pl.BlockSpec((tn, qp.shape[1]), lambda i: (i, 0)),
            pl.BlockSpec(sub_keys.shape, lambda i: (0, 0, 0, 0)),
        ],
        out_specs=(pl.BlockSpec((rows, tn), lambda i: (0, i)), pl.BlockSpec((rows, tn), lambda i: (0, i))),
        compiler_params=_cparams(("parallel",)),
        name="peer_route",
    )(qp, sub_keys)
    return pl.pallas_call(
        _peer_gate_kernel,
        out_shape=jax.ShapeDtypeStruct((HALF_KEYS, n, PEER_N_KEYS), jnp.uint32),
        grid=(n // tn,),
        in_specs=[pl.BlockSpec((rows, tn), lambda i: (0, i)), pl.BlockSpec((rows, tn), lambda i: (0, i))],
        out_specs=pl.BlockSpec((HALF_KEYS, tn, PEER_N_KEYS), lambda i: (0, i, 0)),
        scratch_shapes=[pltpu.VMEM((tn, rows), jnp.int32), pltpu.VMEM((tn, rows), F32)],
        compiler_params=_cparams(("parallel",)),
        name="peer_gates",
    )(e_t, g_t)


def _gelu_tanh(x):
    return 0.5 * x * (1.0 + jnp.tanh(math.sqrt(2.0 / math.pi) * (x + 0.044715 * (x * x * x))))


PEER_BLOCKS_PER_STEP = 4
PEER_TN = 1024


def _peer_expert_kernel(x_ref, gn_ref, gpa_ref, gpb_ref, *rest):
    nblk = PEER_BLOCKS_PER_STEP
    u_refs, v_refs = rest[:nblk], rest[nblk:2 * nblk]
    gf_ref, o_ref, h_scr, acc_scr, coef_even, coef_odd = rest[2 * nblk:]
    s = pl.program_id(1)
    last = pl.num_programs(1) - 1
    tn = x_ref.shape[0]

    @pl.when(s == 0)
    def _():
        h_scr[...] = _rms(x_ref[...], gn_ref[...]).astype(BF16)
        acc_scr[...] = jnp.zeros(acc_scr.shape, F32)
        coef_odd[...] = jnp.zeros(coef_odd.shape, BF16)

    def unpack(word):
        return (pltpu.bitcast(jnp.left_shift(word, jnp.uint32(16)), F32),
                pltpu.bitcast(jnp.bitwise_and(word, jnp.uint32(BF16_HIGH_MASK)), F32))

    def step(coef_new, coef_old):
        v_cat = jnp.concatenate([v[...] for v in v_refs], axis=0)
        u_cat = jnp.concatenate([u[...] for u in u_refs], axis=0)
        for rows in (pl.ds(0, tn // 2), pl.ds(tn // 2, tn // 2)):
            acc_scr[rows, :] += _dot(coef_old[rows, :], v_cat)
            a_lo, a_hi = unpack(gpa_ref[rows, :])
            b_lo, b_hi = unpack(gpb_ref[rows, :])
            hid = _gelu_tanh(_dot_nt(h_scr[rows, :], u_cat))
            coef_new[rows, :] = (jnp.concatenate([a_lo, b_lo, a_hi, b_hi], axis=1) * hid).astype(BF16)

    @pl.when(s % 2 == 0)
    def _():
        step(coef_even, coef_odd)

    @pl.when(s % 2 == 1)
    def _():
        step(coef_odd, coef_even)

    @pl.when(s == last)
    def _():
        o_ref[...] = _rms(x_ref[...] + acc_scr[...], gf_ref[...])


def _peer_experts(x, norm_g, gates, u_tab, v_tab, final_g, *, tn):
    n, d = x.shape
    eb = PEER_N_KEYS
    assert u_tab.shape[0] == eb * eb
    nblk = PEER_BLOCKS_PER_STEP
    ns = eb // nblk

    def cur(s):
        return jnp.minimum(s, ns - 1)

    def prev(s):
        return jnp.maximum(s - 1, 0)

    def table(which, q):
        return pl.BlockSpec((eb, d), lambda i, s: (which(s) + q * ns, 0))

    return pl.pallas_call(
        _peer_expert_kernel,
        out_shape=jax.ShapeDtypeStruct((n, d), F32),
        grid=(n // tn, ns + 1),
        in_specs=[
            pl.BlockSpec((tn, d), lambda i, s: (i, 0), pipeline_mode=pl.Buffered(1)),
            pl.BlockSpec((1, d), lambda i, s: (0, 0)),
            pl.BlockSpec((None, tn, eb), lambda i, s: (cur(s), i, 0)),
            pl.BlockSpec((None, tn, eb), lambda i, s: (cur(s) + ns, i, 0)),
        ] + [table(cur, q) for q in range(nblk)] + [table(prev, q) for q in range(nblk)] + [
            pl.BlockSpec((1, d), lambda i, s: (0, 0)),
        ],
        out_specs=pl.BlockSpec((tn, d), lambda i, s: (i, 0), pipeline_mode=pl.Buffered(1)),
        scratch_shapes=[pltpu.VMEM((tn, d), BF16), pltpu.VMEM((tn, d), F32),
                        pltpu.VMEM((tn, nblk * eb), BF16), pltpu.VMEM((tn, nblk * eb), BF16)],
        compiler_params=_cparams(("parallel", "arbitrary")),
        name="peer_experts",
    )(x, norm_g.reshape(1, d), gates, gates, *([u_tab] * nblk), *([v_tab] * nblk), final_g.reshape(1, d))


def _layer(x, wts, layer, mem_k, mem_v, conv_prev, gdn_state, k_past, v_past, *, tm):
    bsz, t, d = x.shape
    n = bsz * t
    x2d = x.reshape(n, d)
    main, k_new, v_new, tail = _in_proj(x2d, wts["norm_mix_g"], wts["w_in"], tm=min(IN_TM, n))
    conv_dim = 3 * GDN_WIDTH
    conv_prev8 = jnp.concatenate(
        [jnp.zeros((bsz, SUBLANES - (GDN_CONV - 1), conv_dim), F32), conv_prev.astype(F32)], axis=1)
    o_a, s_new = _gdn(main, tail, conv_prev8, gdn_state.astype(F32), wts["gdn_conv_w"], wts["gdn_a_log"],
                      wts["gdn_dt_bias"], wts["gdn_norm_g"], bsz=bsz, t=t)
    conv_new = main.reshape(bsz, t, -1)[:, t - (GDN_CONV - 1):, :conv_dim]
    lam_init = 0.8 - 0.6 * math.exp(-0.3 * layer)
    lams = tuple(wts[k].reshape(1, -1) for k in ("diff_lambda_q1", "diff_lambda_k1", "diff_lambda_q2", "diff_lambda_k2"))
    if k_past is None:
        o_b = _diff_prompt(main, k_new, v_new, lams, wts["diff_subln_g"], bsz=bsz, t=t, lam_init=lam_init)
    else:
        o_b = _diff_sample(main, k_new, v_new, k_past, v_past, lams, wts["diff_subln_g"], bsz=bsz, t=t,
                           lam_init=lam_init)
    k_rows = k_new.reshape(bsz, t, DIFF_HEADS, HEAD_DIM)
    v_rows = v_new.reshape(bsz, t, DIFF_HEADS, HEAD_DIM)
    merged = _merge(o_a, o_b, wts["w_branch_a"], wts["w_branch_b"], tail, tm=tm)
    x1 = _matmul(merged, wts["w_out"], residual=x2d, tm=tm, tn=MM_TN, name="out_proj")
    qm = _matmul(x1, wts["w_mq"], norm_g=wts["norm_cross_g"], tm=tm, tn=MM_TN, out_dtype=BF16, name="mem_q")
    mt = mem_k.shape[1]
    oc = _cross_attend(qm, mem_k.reshape(bsz, mt, d), mem_v.reshape(bsz, mt, d), bsz=bsz, t=t, tm=min(tm, t))
    x2 = _matmul(oc, wts["w_mo"], residual=x1, tm=tm, tn=MM_TN, name="mem_o")
    qp = _matmul(x2, wts["peer_w_q"], norm_g=wts["norm_ffn_g"], tm=tm, tn=MM_TN, name="peer_q")
    tn = min(PEER_TN, n)
    gates = _peer_route(qp, wts["peer_sub_keys"], tn=LANES)
    y = _peer_experts(x2, wts["norm_ffn_g"], gates, wts["peer_u"], wts["peer_v"], wts["final_norm_g"], tn=tn)
    return y.reshape(bsz, t, d), k_rows, v_rows, s_new, conv_new


def kernel(x_prompt, x_sample, cache_diff_k, cache_diff_v, state_gdn, state_conv, cache_mem_k, cache_mem_v,
           mem_prompt, norm_mix_g, w_in, gdn_conv_w, gdn_a_log, gdn_dt_bias, gdn_norm_g,
           diff_lambda_q1, diff_lambda_k1, diff_lambda_q2, diff_lambda_k2, diff_subln_g,
           w_branch_a, w_branch_b, w_out, norm_cross_g, norm_mem_g, w_mq, w_mk, w_mv, w_mo,
           norm_ffn_g, peer_w_q, peer_sub_keys, peer_u, peer_v, final_norm_g):
    depth = w_in.shape[0]
    assert depth == 1, "final norm is fused into the last layer's PEER kernel"
    l = 0
    wts = {
        "norm_mix_g": norm_mix_g[l], "w_in": _in_proj_weights(w_in[l]), "gdn_conv_w": gdn_conv_w[l],
        "gdn_a_log": gdn_a_log[l], "gdn_dt_bias": gdn_dt_bias[l], "gdn_norm_g": gdn_norm_g[l],
        "diff_lambda_q1": diff_lambda_q1[l], "diff_lambda_k1": diff_lambda_k1[l],
        "diff_lambda_q2": diff_lambda_q2[l], "diff_lambda_k2": diff_lambda_k2[l],
        "diff_subln_g": diff_subln_g[l], "w_branch_a": w_branch_a[l].astype(BF16),
        "w_branch_b": w_branch_b[l].astype(BF16), "w_out": w_out[l].astype(BF16),
        "norm_cross_g": norm_cross_g[l], "w_mq": w_mq[l].astype(BF16), "w_mo": w_mo[l].astype(BF16),
        "norm_ffn_g": norm_ffn_g[l], "peer_w_q": peer_w_q[l].astype(BF16), "peer_sub_keys": peer_sub_keys[l],
        "peer_u": peer_u[l].astype(BF16), "peer_v": peer_v[l].astype(BF16), "final_norm_g": final_norm_g,
    }
    bp, tp, d = x_prompt.shape
    bs, ts, _ = x_sample.shape
    mem2d = mem_prompt.reshape(-1, d)
    tmm = min(512, mem2d.shape[0])
    mem_k = _matmul(mem2d, w_mk[l].astype(BF16), norm_g=norm_mem_g[l], tm=tmm, tn=512, name="mem_k")
    mem_v = _matmul(mem2d, w_mv[l].astype(BF16), norm_g=norm_mem_g[l], tm=tmm, tn=512, name="mem_v")
    mshape = mem_prompt.shape[:2] + (MEM_HEADS, d // MEM_HEADS)
    mem_k = mem_k.reshape(mshape)
    mem_v = mem_v.reshape(mshape)
    conv0 = jnp.zeros((bp, GDN_CONV - 1, 3 * GDN_WIDTH), F32)
    s0 = jnp.zeros((bp, GDN_HEADS, HEAD_DIM, HEAD_DIM), F32)
    yp, pk, pv, ps, pc = _layer(x_prompt, wts, l, mem_k, mem_v, conv0, s0, None, None, tm=min(1024, bp * tp))
    ys, sk, sv, ss, sc = _layer(x_sample, wts, l, cache_mem_k[l], cache_mem_v[l], state_conv[l], state_gdn[l],
                                cache_diff_k[l], cache_diff_v[l], tm=min(256, bs * ts))
    return (yp, ys, pk[None], pv[None], ps[None], pc[None], mem_k[None], mem_v[None],
            sk[None], sv[None], ss[None], sc[None])
```

```python
import functools
import math

import jax
import jax.numpy as jnp
from jax import lax
from jax.experimental import pallas as pl
from jax.experimental.pallas import tpu as pltpu

F32 = jnp.float32
BF16 = jnp.bfloat16
HIGHEST = lax.Precision.HIGHEST

RMS_EPS = 1e-6
CHUNK = 64
CHUNK_SHIFT = 6
DIFF_TQ = 1024
GDN_CHUNKS_PER_STEP = 2
DIFF_SUB = 256
GDN_HEADS = 8
HEAD_DIM = 128
GDN_WIDTH = GDN_HEADS * HEAD_DIM
GDN_CONV = 4
DIFF_HEADS = 8
DIFF_HEAD_DIM = 64
MEM_HEADS = 4
PEER_HEADS = 8
PEER_N_KEYS = 128
PEER_TOPK = 16
LANES = 128
SUBLANES = 8
V7X_VMEM_LIMIT = 52 * 1024 * 1024
MM_TN = 1024


def _cparams(sem, vmem=V7X_VMEM_LIMIT):
    return pltpu.CompilerParams(dimension_semantics=sem, vmem_limit_bytes=vmem)


def _dot(a, b, precision=None):
    return jnp.dot(a, b, preferred_element_type=F32, precision=precision)


def _dot_nt(a, b, precision=None):
    return lax.dot_general(a, b, (((1,), (1,)), ((), ())), preferred_element_type=F32, precision=precision)


def _split_bf16(x):
    hi = x.astype(BF16)
    return hi, (x - hi.astype(F32)).astype(BF16)


def _cat_parts(p, q):
    return jnp.concatenate([p[0], q[0]], axis=0), jnp.concatenate([p[1], q[1]], axis=0)


def _dot3_parts(a_parts, b_parts, dot):
    a_hi, a_lo = a_parts
    b_hi, b_lo = b_parts
    m = a_hi.shape[0]
    both = dot(jnp.concatenate([a_hi, a_lo], axis=0), b_hi)
    return both[:m] + both[m:] + dot(a_hi, b_lo)


def _dot3(a, b):
    return _dot3_parts(_split_bf16(a), _split_bf16(b), _dot)


def _rms(xf, g):
    return xf * lax.rsqrt(jnp.mean(xf * xf, axis=-1, keepdims=True) + RMS_EPS) * g


def _mm_kernel(*refs, has_norm, has_res):
    it = iter(refs)
    x_ref = next(it)
    g_ref = next(it) if has_norm else None
    w_ref = next(it)
    r_ref = next(it) if has_res else None
    o_ref = next(it)
    h_scr = next(it)

    @pl.when(pl.program_id(1) == 0)
    def _():
        xf = x_ref[...].astype(F32)
        if has_norm:
            xf = _rms(xf, g_ref[...])
        h_scr[...] = xf.astype(BF16)

    acc = _dot(h_scr[...], w_ref[...])
    if has_res:
        acc = acc + r_ref[...]
    o_ref[...] = acc.astype(o_ref.dtype)


def _matmul(x, w, *, norm_g=None, residual=None, tm, tn, out_dtype=F32, name="matmul"):
    m, k = x.shape
    n = w.shape[1]
    assert m % tm == 0 and n % tn == 0, (m, n, tm, tn)
    in_specs = [pl.BlockSpec((tm, k), lambda i, j: (i, 0))]
    args = [x]
    if norm_g is not None:
        in_specs.append(pl.BlockSpec((1, k), lambda i, j: (0, 0)))
        args.append(norm_g.reshape(1, k).astype(F32))
    in_specs.append(pl.BlockSpec((k, tn), lambda i, j: (0, j)))
    args.append(w)
    if residual is not None:
        in_specs.append(pl.BlockSpec((tm, tn), lambda i, j: (i, j)))
        args.append(residual)
    return pl.pallas_call(
        functools.partial(_mm_kernel, has_norm=norm_g is not None, has_res=residual is not None),
        out_shape=jax.ShapeDtypeStruct((m, n), out_dtype),
        grid=(m // tm, n // tn),
        in_specs=in_specs,
        out_specs=pl.BlockSpec((tm, tn), lambda i, j: (i, j)),
        scratch_shapes=[pltpu.VMEM((tm, k), BF16)],
        compiler_params=_cparams(("parallel", "arbitrary")),
        name=name,
    )(*args)


IN_TN = 512
IN_TM = 1024
IN_W1_TILES = 8
IN_W2_TILES = 14
IN_MAIN_TILES = 10
IN_K_TILE0 = 10
IN_V_TILE0 = 12
IN_TAIL_TILE0 = 14
IN_TILES = IN_W1_TILES + IN_W2_TILES + 1
MAIN_COLS = IN_MAIN_TILES * IN_TN
TAIL_COLS = (IN_TILES - IN_TAIL_TILE0) * IN_TN
COL_DQ_MAIN = 4096
COL_GB_TAIL = 2048
COL_BA_TAIL = 4096


def _in_proj_kernel(x_ref, g_ref, w1_ref, w2_ref, w3_ref, main_ref, k_ref, v_ref, tail_ref, h_scr):
    j = pl.program_id(1)

    @pl.when(j == 0)
    def _():
        h_scr[...] = _rms(x_ref[...], g_ref[...]).astype(BF16)

    routes = ((0, IN_W1_TILES, w1_ref, main_ref),
              (IN_W1_TILES, IN_K_TILE0, w2_ref, main_ref),
              (IN_K_TILE0, IN_V_TILE0, w2_ref, k_ref),
              (IN_V_TILE0, IN_TAIL_TILE0, w2_ref, v_ref),
              (IN_TAIL_TILE0, IN_TILES - 1, w2_ref, tail_ref),
              (IN_TILES - 1, IN_TILES, w3_ref, tail_ref))
    for lo, hi, w_ref, dst in routes:
        @pl.when(jnp.logical_and(j >= lo, j < hi))
        def _(w_ref=w_ref, dst=dst):
            dst[...] = _dot(h_scr[...], w_ref[...])


def _in_proj_weights(w_in):
    c_z = 4 * GDN_WIDTH
    c_ba = c_z + 2 * GDN_HEADS
    assert c_z == IN_W1_TILES * IN_TN and w_in.shape[1] - c_ba == IN_W2_TILES * IN_TN
    w3 = jnp.pad(w_in[:, c_z:c_ba], ((0, 0), (0, IN_TN - 2 * GDN_HEADS)))
    return w_in[:, :c_z].astype(BF16), w_in[:, c_ba:].astype(BF16), w3.astype(BF16)


def _in_proj(x, norm_g, w_pieces, *, tm):
    m, d = x.shape
    w1, w2, w3 = w_pieces
    tn = IN_TN
    out = lambda cols: jax.ShapeDtypeStruct((m, cols), F32)
    return pl.pallas_call(
        _in_proj_kernel,
        out_shape=(out(MAIN_COLS), out(2 * IN_TN), out(2 * IN_TN), out(TAIL_COLS)),
        grid=(m // tm, IN_TILES),
        in_specs=[
            pl.BlockSpec((tm, d), lambda i, j: (i, 0)),
            pl.BlockSpec((1, d), lambda i, j: (0, 0)),
            pl.BlockSpec((d, tn), lambda i, j: (0, jnp.minimum(j, IN_W1_TILES - 1))),
            pl.BlockSpec((d, tn), lambda i, j: (0, jnp.clip(j - IN_W1_TILES, 0, IN_W2_TILES - 1))),
            pl.BlockSpec((d, tn), lambda i, j: (0, 0)),
        ],
        out_specs=(
            pl.BlockSpec((tm, tn), lambda i, j: (i, jnp.minimum(j, IN_MAIN_TILES - 1))),
            pl.BlockSpec((tm, tn), lambda i, j: (i, jnp.clip(j - IN_K_TILE0, 0, 1))),
            pl.BlockSpec((tm, tn), lambda i, j: (i, jnp.clip(j - IN_V_TILE0, 0, 1))),
            pl.BlockSpec((tm, tn), lambda i, j: (i, jnp.clip(j - IN_TAIL_TILE0, 0, IN_TILES - IN_TAIL_TILE0 - 1))),
        ),
        scratch_shapes=[pltpu.VMEM((tm, d), BF16)],
        compiler_params=_cparams(("parallel", "arbitrary")),
        name="in_proj",
    )(x, norm_g.reshape(1, d).astype(F32), w1, w2, w3)


def _gdn_kernel(qkv_ref, z_ref, ba_ref, prev_ref, s0_ref, convw_ref, alog_ref, dtb_ref, ng_ref,
                o_ref, sfin_ref, s_scr, xbuf, *, L, nch):
    step = pl.program_id(1)
    rows = nch * L

    @pl.when(step == 0)
    def _():
        s_scr[...] = s0_ref[0]
        xbuf[0:SUBLANES, :] = prev_ref[0]

    xbuf[SUBLANES:SUBLANES + rows, :] = qkv_ref[...]
    base = SUBLANES - (GDN_CONV - 1)
    y = xbuf[base:base + rows, :] * convw_ref[0:1, :]
    for i in range(1, GDN_CONV):
        y = y + xbuf[base + i:base + i + rows, :] * convw_ref[i:i + 1, :]
    y = y * jax.nn.sigmoid(y)
    carry = xbuf[rows:rows + SUBLANES, :]
    xbuf[0:SUBLANES, :] = carry

    row = lax.broadcasted_iota(jnp.int32, (L, L), 0)
    col = lax.broadcasted_iota(jnp.int32, (L, L), 1)
    lower = row >= col
    strict = row > col
    eye = (row == col).astype(F32)
    log_l = int(math.log2(L))
    n_double = log_l - 1
    assert 2 ** log_l == L

    ba = ba_ref[...]
    beta_all = jax.nn.sigmoid(ba)
    a_in = ba + dtb_ref[...]
    softplus = jnp.maximum(a_in, 0.0) + jnp.log(1.0 + jnp.exp(-jnp.abs(a_in)))
    g_all = -jnp.exp(alog_ref[...]) * softplus
    rr = lax.broadcasted_iota(jnp.int32, (rows, rows), 0)
    cc = lax.broadcasted_iota(jnp.int32, (rows, rows), 1)
    same_chunk_lower = jnp.logical_and(rr >= cc, jnp.right_shift(rr, log_l) == jnp.right_shift(cc, log_l))
    g_cum = _dot(same_chunk_lower.astype(F32), g_all, HIGHEST)
    g_cum_t = g_cum.T

    heads = range(GDN_HEADS)
    chains = [(c, h) for c in range(nch) for h in heads]
    ids = range(len(chains))

    def tok(c):
        return slice(c * L, (c + 1) * L)

    def head_cols(part, c, h):
        return y[tok(c), part * GDN_WIDTH + h * HEAD_DIM:part * GDN_WIDTH + (h + 1) * HEAD_DIM]

    q = [head_cols(0, c, h) for c, h in chains]
    k = [head_cols(1, c, h) for c, h in chains]
    v = [head_cols(2, c, h) for c, h in chains]
    q = [a * lax.rsqrt(jnp.sum(a * a, axis=-1, keepdims=True) + 1e-6) * (HEAD_DIM ** -0.5) for a in q]
    k = [a * lax.rsqrt(jnp.sum(a * a, axis=-1, keepdims=True) + 1e-6) for a in k]
    beta = [beta_all[tok(c), h:h + 1] for c, h in chains]
    g_col = [g_cum[tok(c), GDN_HEADS + h:GDN_HEADS + h + 1] for c, h in chains]
    g_row = [g_cum_t[GDN_HEADS + h:GDN_HEADS + h + 1, tok(c)] for c, h in chains]
    decay = [jnp.where(lower, jnp.exp(jnp.where(lower, g_col[n] - g_row[n], 0.0)), 0.0) for n in ids]
    g_last = [g_col[n][L - 1:L, :] for n in ids]
    e_col = [jnp.exp(g_col[n]) for n in ids]
    qk_kk = [_dot_nt(jnp.concatenate([q[n], k[n]], axis=0).astype(BF16), k[n].astype(BF16)) for n in ids]
    x = [-jnp.where(strict, beta[n] * qk_kk[n][L:] * decay[n], 0.0) for n in ids]
    t_inv = [eye + x[n] for n in ids]
    x_parts = [_split_bf16(x[n]) for n in ids]
    xp = [_dot3_parts(x_parts[n], x_parts[n], _dot) for n in ids]
    for _ in range(n_double - 1):
        xp_parts = [_split_bf16(xp[n]) for n in ids]
        both = [_dot3_parts(_cat_parts(_split_bf16(t_inv[n]), xp_parts[n]), xp_parts[n], _dot) for n in ids]
        t_inv = [t_inv[n] + both[n][:L] for n in ids]
        xp = [both[n][L:] for n in ids]
    t_inv = [t_inv[n] + _dot3(t_inv[n], xp[n]) for n in ids]
    sol = [_dot3(t_inv[n], jnp.concatenate([v[n] * beta[n], k[n] * (beta[n] * e_col[n])], axis=-1)) for n in ids]
    qk = [jnp.where(lower, qk_kk[n][:L] * decay[n], 0.0) for n in ids]
    k_dec_t = [(k[n] * jnp.exp(g_last[n] - g_col[n])).T for n in ids]
    left_s = [jnp.concatenate([sol[n][:, HEAD_DIM:], q[n] * e_col[n]], axis=0).astype(BF16) for n in ids]
    left_v = [jnp.concatenate([qk[n], k_dec_t[n]], axis=0).astype(BF16) for n in ids]

    s = [s_scr[h] for h in heads]
    for c in range(nch):
        n0 = c * GDN_HEADS
        ws_qs = [_dot(left_s[n0 + h], s[h].astype(BF16)) for h in heads]
        v_new = [sol[n0 + h][:, :HEAD_DIM] - ws_qs[h][:L] for h in heads]
        tail = [_dot(left_v[n0 + h], v_new[h].astype(BF16)) for h in heads]
        s = [s[h] * jnp.exp(g_last[n0 + h]) + tail[h][L:] for h in heads]
        for h in heads:
            zf = z_ref[tok(c), h * HEAD_DIM:(h + 1) * HEAD_DIM]
            o = _rms(ws_qs[h][L:] + tail[h][:L], ng_ref[...]) * (zf * jax.nn.sigmoid(zf))
            o_ref[tok(c), h * HEAD_DIM:(h + 1) * HEAD_DIM] = o.astype(o_ref.dtype)
    for h in heads:
        s_scr[h] = s[h]

    @pl.when(step == pl.num_programs(1) - 1)
    def _():
        sfin_ref[0] = s_scr[...]


def _decay_lanes(p):
    return jnp.zeros((1, LANES), F32).at[0, GDN_HEADS:2 * GDN_HEADS].set(p.astype(F32))


def _gdn(main, tail, conv_prev8, s0, conv_w, a_log, dt_bias, norm_g, *, bsz, t):
    L = min(CHUNK, t)
    nch = GDN_CHUNKS_PER_STEP if (t // L) % GDN_CHUNKS_PER_STEP == 0 else 1
    nc = t // (L * nch)
    rows = L * nch
    conv_dim = 3 * GDN_WIDTH
    kern = functools.partial(_gdn_kernel, L=L, nch=nch)
    o, s_fin = pl.pallas_call(
        kern,
        out_shape=(jax.ShapeDtypeStruct((bsz * t, GDN_WIDTH), BF16),
                   jax.ShapeDtypeStruct((bsz, GDN_HEADS, HEAD_DIM, HEAD_DIM), F32)),
        grid=(bsz, nc),
        in_specs=[
            pl.BlockSpec((rows, conv_dim), lambda b, c: (b * nc + c, 0)),
            pl.BlockSpec((rows, GDN_WIDTH), lambda b, c: (b * nc + c, conv_dim // GDN_WIDTH)),
            pl.BlockSpec((rows, LANES), lambda b, c: (b * nc + c, COL_BA_TAIL // LANES)),
            pl.BlockSpec((1, SUBLANES, conv_dim), lambda b, c: (b, 0, 0)),
            pl.BlockSpec((1, GDN_HEADS, HEAD_DIM, HEAD_DIM), lambda b, c: (b, 0, 0, 0)),
            pl.BlockSpec((GDN_CONV, conv_dim), lambda b, c: (0, 0)),
            pl.BlockSpec((1, LANES), lambda b, c: (0, 0)),
            pl.BlockSpec((1, LANES), lambda b, c: (0, 0)),
            pl.BlockSpec((1, HEAD_DIM), lambda b, c: (0, 0)),
        ],
        out_specs=(pl.BlockSpec((rows, GDN_WIDTH), lambda b, c: (b * nc + c, 0)),
                   pl.BlockSpec((1, GDN_HEADS, HEAD_DIM, HEAD_DIM), lambda b, c: (b, 0, 0, 0))),
        scratch_shapes=[pltpu.VMEM((GDN_HEADS, HEAD_DIM, HEAD_DIM), F32),
                        pltpu.VMEM((SUBLANES + rows + SUBLANES, conv_dim), F32)],
        compiler_params=_cparams(("parallel", "arbitrary")),
        name="gdn",
    )(main, main, tail, conv_prev8, s0, conv_w, _decay_lanes(a_log), _decay_lanes(dt_bias),
      norm_g.reshape(1, -1))
    return o, s_fin


def _lambda_value(lam_refs, lam_init):
    lq1, lk1, lq2, lk2 = (r[...] for r in lam_refs)
    return (jnp.exp(jnp.sum(lq1 * lk1, axis=-1, keepdims=True))
            - jnp.exp(jnp.sum(lq2 * lk2, axis=-1, keepdims=True)) + lam_init)


def _softmax_update(m_scr, l_scr, acc_scr, scores, values):
    idx = range(len(scores))
    m_old = [m_scr[n] for n in idx]
    m_new = [jnp.maximum(m_old[n], jnp.max(scores[n], axis=-1, keepdims=True)) for n in idx]
    p = [jnp.exp(scores[n] - m_new[n]) for n in idx]
    alpha = [jnp.exp(m_old[n] - m_new[n]) for n in idx]
    pv = [_dot(p[n].astype(BF16), values[n]) for n in idx]
    for n in idx:
        l_scr[n] = alpha[n] * l_scr[n] + jnp.sum(p[n], axis=-1, keepdims=True)
        acc_scr[n] = alpha[n] * acc_scr[n] + pv[n]
        m_scr[n] = m_new[n]


def _diff_finalize(o1, o2, lam_refs, g_ref, lam_init):
    lam = _lambda_value(lam_refs, lam_init)
    return _rms(o1 - lam * o2, g_ref[...]) * (1.0 - lam_init)


def _diff_init(m_scr, l_scr, acc_scr):
    m_scr[...] = jnp.full(m_scr.shape, -jnp.inf, F32)
    l_scr[...] = jnp.zeros(l_scr.shape, F32)
    acc_scr[...] = jnp.zeros(acc_scr.shape, F32)


def _near_bias(slope, r, q_pos, k_pos):
    visible = jnp.right_shift(k_pos, CHUNK_SHIFT) <= jnp.right_shift(q_pos, CHUNK_SHIFT)
    return jnp.where(visible, slope * (r - jnp.abs(q_pos - k_pos)).astype(F32), -jnp.inf)


def _split_distance(slope, d):
    return (-(slope * CHUNK) * jnp.right_shift(d, CHUNK_SHIFT).astype(F32),
            -slope * jnp.bitwise_and(d, CHUNK - 1).astype(F32))


def _diff_prompt_kernel(i_tab, j_tab, slopes_ref, q_ref, k_ref, v_ref, lq1, lk1, lq2, lk2, g_ref, o_ref,
                        m_scr, l_scr, acc_scr, lhs_scr, s_even, s_odd, *, tq, lam_init):
    h = pl.program_id(1)
    s_id = pl.program_id(2)
    n_pairs = pl.num_programs(2) - 1
    score_on = s_id < n_pairs
    consume_on = s_id >= 1
    i = i_tab[s_id]
    j = j_tab[s_id]
    ci = i_tab[jnp.maximum(s_id - 1, 0)]
    cj = j_tab[jnp.maximum(s_id - 1, 0)]
    slope = slopes_ref[h]
    half = DIFF_HEAD_DIM
    lane = lax.broadcasted_iota(jnp.int32, (tq, LANES), 1)

    @pl.when(jnp.logical_and(consume_on, cj == 0))
    def _():
        _diff_init(m_scr, l_scr, acc_scr)

    @pl.when(jnp.logical_and(score_on, j == 0))
    def _():
        qs = q_ref[...] * (DIFF_HEAD_DIM ** -0.5)
        lhs_scr[0] = jnp.where(lane < half, qs, 1.0).astype(BF16)
        lhs_scr[1] = jnp.where(lane >= half, qs, 1.0).astype(BF16)

    sub = min(DIFF_SUB, tq)
    units = [(n, qb) for qb in range(tq // sub) for n in range(2)]

    def score_tile(near, dst):
        kb = k_ref[...]
        if near:
            aug0 = aug1 = jnp.zeros_like(kb)
        else:
            d = (i - j) * tq - lax.broadcasted_iota(jnp.int32, (tq, LANES), 0)
            t_hi, t_lo = _split_distance(slope, d)
            aug0 = jnp.where(lane == half, t_hi, jnp.where(lane == half + 1, t_lo, 0.0))
            aug1 = jnp.where(lane == 0, t_hi, jnp.where(lane == 1, t_lo, 0.0))
        keys = [jnp.where(lane < half, kb, aug0).astype(BF16), jnp.where(lane >= half, kb, aug1).astype(BF16)]
        bias = None
        for n, qb in units:
            cols = slice(qb * sub, (qb + 1) * sub)
            if near:
                live = (qb + 1) * sub
                if n == 0:
                    k_pos = lax.broadcasted_iota(jnp.int32, (live, sub), 0)
                    q_pos = qb * sub + lax.broadcasted_iota(jnp.int32, (live, sub), 1)
                    bias = _near_bias(slope, q_pos, q_pos, k_pos)
                dst[n, :live, cols] = _dot_nt(keys[n][:live], lhs_scr[n, cols, :]) + bias
                if live < tq:
                    dst[n, live:, cols] = jnp.full((tq - live, sub), -jnp.inf, F32)
            else:
                dst[n, :, cols] = _dot_nt(keys[n], lhs_scr[n, cols, :])

    def consume_tile(src):
        vt = v_ref[...].T.astype(BF16)
        idx = range(len(units))
        cols = [slice(qb * sub, (qb + 1) * sub) for _, qb in units]
        s = [src[n, :, cols[u]] for u, (n, _) in enumerate(units)]
        m_old = [m_scr[n, :, cols[u]] for u, (n, _) in enumerate(units)]
        m_new = [jnp.maximum(m_old[u], jnp.max(s[u], axis=0, keepdims=True)) for u in idx]
        p = [jnp.exp(s[u] - m_new[u]) for u in idx]
        alpha = [jnp.exp(m_old[u] - m_new[u]) for u in idx]
        pv = [_dot(vt, p[u].astype(BF16)) for u in idx]
        for u, (n, _) in enumerate(units):
            l_scr[n, :, cols[u]] = alpha[u] * l_scr[n, :, cols[u]] + jnp.sum(p[u], axis=0, keepdims=True)
            acc_scr[n, :, cols[u]] = alpha[u] * acc_scr[n, :, cols[u]] + pv[u]
            m_scr[n, :, cols[u]] = m_new[u]

    def stage(cond, near, score, consume):
        for parity, (dst, src) in enumerate(((s_even, s_odd), (s_odd, s_even))):
            @pl.when(jnp.logical_and(cond, s_id % 2 == parity))
            def _(dst=dst, src=src):
                if score:
                    score_tile(near, dst)
                if consume:
                    consume_tile(src)

    both = jnp.logical_and(score_on, consume_on)
    stage(jnp.logical_and(both, j < i), False, True, True)
    stage(jnp.logical_and(both, j == i), True, True, True)
    stage(jnp.logical_not(consume_on), True, True, False)
    stage(jnp.logical_not(score_on), False, False, True)

    @pl.when(jnp.logical_and(consume_on, cj == ci))
    def _():
        o_ref[...] = _diff_finalize((acc_scr[0] / l_scr[0]).T, (acc_scr[1] / l_scr[1]).T,
                                    (lq1, lk1, lq2, lk2), g_ref, lam_init).astype(o_ref.dtype)


def _alibi_slopes():
    return jnp.asarray([2.0 ** (-8.0 * (i + 1) / DIFF_HEADS) for i in range(DIFF_HEADS)], F32)


def _diff_prompt(main, k_new, v_new, lams, subln_g, *, bsz, t, lam_init):
    tq = min(DIFF_TQ, t)
    nq = t // tq
    assert tq % CHUNK == 0 and t <= CHUNK * 255
    kern = functools.partial(_diff_prompt_kernel, tq=tq, lam_init=lam_init)
    qc, kc, vc = COL_DQ_MAIN // LANES, 0, 0
    pairs = [(i, j) for i in range(nq) for j in range(i + 1)]
    pairs.append(pairs[-1])
    i_tab = jnp.asarray([p[0] for p in pairs], jnp.int32)
    j_tab = jnp.asarray([p[1] for p in pairs], jnp.int32)
    zero = lambda b, h, s, it, jt: (0, 0)

    def consumed(tab, s):
        return tab[jnp.maximum(s - 1, 0)]

    grid_spec = pltpu.PrefetchScalarGridSpec(
        num_scalar_prefetch=2,
        grid=(bsz, DIFF_HEADS, len(pairs)),
        in_specs=[
            pl.BlockSpec(memory_space=pltpu.SMEM),
            pl.BlockSpec((tq, LANES), lambda b, h, s, it, jt: (b * nq + it[s], qc + h)),
            pl.BlockSpec((tq, LANES), lambda b, h, s, it, jt: (b * nq + jt[s], kc + h)),
            pl.BlockSpec((tq, LANES), lambda b, h, s, it, jt: (b * nq + consumed(jt, s), vc + h)),
        ] + [pl.BlockSpec((1, DIFF_HEAD_DIM), zero)] * 4 + [pl.BlockSpec((1, HEAD_DIM), zero)],
        out_specs=pl.BlockSpec((tq, LANES), lambda b, h, s, it, jt: (b * nq + consumed(it, s), h)),
        scratch_shapes=[pltpu.VMEM((2, 1, tq), F32), pltpu.VMEM((2, 1, tq), F32),
                        pltpu.VMEM((2, HEAD_DIM, tq), F32), pltpu.VMEM((2, tq, LANES), BF16),
                        pltpu.VMEM((2, tq, tq), F32), pltpu.VMEM((2, tq, tq), F32)],
    )
    return pl.pallas_call(
        kern,
        out_shape=jax.ShapeDtypeStruct((bsz * t, DIFF_HEADS * HEAD_DIM), BF16),
        grid_spec=grid_spec,
        compiler_params=_cparams(("parallel", "parallel", "arbitrary")),
        name="diff_attn_prompt",
    )(i_tab, j_tab, _alibi_slopes(), main, k_new, v_new, *lams, subln_g.reshape(1, -1))


def _diff_sample_kernel(slopes_ref, q_ref, kn_ref, vn_ref, kp_ref, vp_ref, lq1, lk1, lq2, lk2, g_ref, o_ref,
                        m_scr, l_scr, acc_scr, lhs_scr, *, t, tk, past, lam_init):
    j = pl.program_id(1)
    heads = range(DIFF_HEADS)

    def cols(h):
        return slice(h * HEAD_DIM, (h + 1) * HEAD_DIM)

    @pl.when(j == 0)
    def _():
        _diff_init(m_scr, l_scr, acc_scr)
        lane = lax.broadcasted_iota(jnp.int32, (t, LANES), 1)
        r = lax.broadcasted_iota(jnp.int32, (t, t), 0)
        c = lax.broadcasted_iota(jnp.int32, (t, t), 1)
        scores = []
        for h in heads:
            qs = q_ref[:, cols(h)] * (DIFF_HEAD_DIM ** -0.5)
            lhs_scr[h, 0:t, :] = jnp.where(lane < DIFF_HEAD_DIM, qs, 0.0).astype(BF16)
            lhs_scr[h, t:2 * t, :] = jnp.where(lane >= DIFF_HEAD_DIM, qs, 0.0).astype(BF16)
            bias = _near_bias(slopes_ref[h], r, past + r, past + c)
            scores.append(_dot_nt(lhs_scr[h], kn_ref[:, cols(h)].astype(BF16)) + jnp.concatenate([bias, bias], axis=0))
        _softmax_update(m_scr, l_scr, acc_scr, scores, [vn_ref[:, cols(h)].astype(BF16) for h in heads])

    d = (past - j * tk - lax.broadcasted_iota(jnp.int32, (1, tk), 1)).astype(F32)
    kt = jnp.swapaxes(kp_ref[...], 0, 1).astype(BF16)
    vt = jnp.swapaxes(vp_ref[...], 0, 1).astype(BF16)
    scores = [_dot_nt(lhs_scr[h], kt[h]) - slopes_ref[h] * d for h in heads]
    _softmax_update(m_scr, l_scr, acc_scr, scores, [vt[h] for h in heads])

    @pl.when(j == pl.num_programs(1) - 1)
    def _():
        for h in heads:
            o = acc_scr[h] / l_scr[h]
            o_ref[:, cols(h)] = _diff_finalize(o[:t], o[t:], (lq1, lk1, lq2, lk2), g_ref, lam_init).astype(o_ref.dtype)


def _diff_sample(main, k_new, v_new, k_past, v_past, lams, subln_g, *, bsz, t, lam_init):
    past = k_past.shape[1]
    tk = min(512, past)
    nk = past // tk
    width = DIFF_HEADS * HEAD_DIM
    kern = functools.partial(_diff_sample_kernel, t=t, tk=tk, past=past, lam_init=lam_init)
    zero = lambda b, j: (0, 0)
    cache = pl.BlockSpec((None, tk, DIFF_HEADS, HEAD_DIM), lambda b, j: (b, j, 0, 0))
    return pl.pallas_call(
        kern,
        out_shape=jax.ShapeDtypeStruct((bsz * t, width), BF16),
        grid=(bsz, nk),
        in_specs=[
            pl.BlockSpec(memory_space=pltpu.SMEM),
            pl.BlockSpec((t, width), lambda b, j: (b, COL_DQ_MAIN // width)),
            pl.BlockSpec((t, width), lambda b, j: (b, 0)),
            pl.BlockSpec((t, width), lambda b, j: (b, 0)),
            cache, cache,
        ] + [pl.BlockSpec((1, DIFF_HEAD_DIM), zero)] * 4 + [pl.BlockSpec((1, HEAD_DIM), zero)],
        out_specs=pl.BlockSpec((t, width), lambda b, j: (b, 0)),
        scratch_shapes=[pltpu.VMEM((DIFF_HEADS, 2 * t, 1), F32), pltpu.VMEM((DIFF_HEADS, 2 * t, 1), F32),
                        pltpu.VMEM((DIFF_HEADS, 2 * t, HEAD_DIM), F32), pltpu.VMEM((DIFF_HEADS, 2 * t, LANES), BF16)],
        compiler_params=_cparams(("parallel", "arbitrary")),
        name="diff_attn_sample",
    )(_alibi_slopes(), main, k_new, v_new, k_past, v_past, *lams, subln_g.reshape(1, -1))


def _merge_kernel(oa_ref, ob_ref, wa_ref, wb_ref, ga_ref, gb_ref, o_ref):
    ya = _dot(oa_ref[...], wa_ref[...])
    yb = _dot(ob_ref[...], wb_ref[...])
    o_ref[...] = (jax.nn.sigmoid(ga_ref[...]) * ya + jax.nn.sigmoid(gb_ref[...]) * yb).astype(o_ref.dtype)


def _merge(o_a, o_b, wa, wb, tail, *, tm):
    m = o_a.shape[0]
    d = wa.shape[1]
    tn = 512
    ga0, gb0 = 0, COL_GB_TAIL // tn
    return pl.pallas_call(
        _merge_kernel,
        out_shape=jax.ShapeDtypeStruct((m, d), BF16),
        grid=(m // tm, d // tn),
        in_specs=[
            pl.BlockSpec((tm, o_a.shape[1]), lambda i, j: (i, 0)),
            pl.BlockSpec((tm, o_b.shape[1]), lambda i, j: (i, 0)),
            pl.BlockSpec((wa.shape[0], tn), lambda i, j: (0, j)),
            pl.BlockSpec((wb.shape[0], tn), lambda i, j: (0, j)),
            pl.BlockSpec((tm, tn), lambda i, j: (i, ga0 + j)),
            pl.BlockSpec((tm, tn), lambda i, j: (i, gb0 + j)),
        ],
        out_specs=pl.BlockSpec((tm, tn), lambda i, j: (i, j)),
        compiler_params=_cparams(("parallel", "arbitrary")),
        name="merge",
    )(o_a, o_b, wa, wb, tail, tail)


def _cross_kernel(q_ref, k_ref, v_ref, o_ref):
    dh = q_ref.shape[-1] // MEM_HEADS
    for h in range(MEM_HEADS):
        sl = slice(h * dh, (h + 1) * dh)
        s = _dot_nt(q_ref[:, sl], k_ref[0, :, sl].astype(BF16)) * (dh ** -0.5)
        s = s - jnp.max(s, axis=-1, keepdims=True)
        p = jnp.exp(s)
        p = p / jnp.sum(p, axis=-1, keepdims=True)
        o_ref[:, sl] = _dot(p.astype(BF16), v_ref[0, :, sl].astype(BF16)).astype(o_ref.dtype)


def _cross_attend(q, mem_k, mem_v, *, bsz, t, tm):
    d = q.shape[1]
    nt = t // tm
    mt = mem_k.shape[1]
    return pl.pallas_call(
        _cross_kernel,
        out_shape=jax.ShapeDtypeStruct(q.shape, BF16),
        grid=(bsz, nt),
        in_specs=[
            pl.BlockSpec((tm, d), lambda b, i: (b * nt + i, 0)),
            pl.BlockSpec((1, mt, d), lambda b, i: (b, 0, 0)),
            pl.BlockSpec((1, mt, d), lambda b, i: (b, 0, 0)),
        ],
        out_specs=pl.BlockSpec((tm, d), lambda b, i: (b * nt + i, 0)),
        compiler_params=_cparams(("parallel", "arbitrary")),
        name="cross_attn",
    )(q, mem_k, mem_v)


def _topk_rows(work, cidx, k):
    n, tn = work.shape
    rid = lax.broadcasted_iota(jnp.int32, (n, tn), 0)
    kid = lax.broadcasted_iota(jnp.int32, (k, tn), 0)
    vals = jnp.zeros((k, tn), F32)
    idxs = jnp.zeros((k, tn), jnp.int32)
    for t in range(k):
        m = jnp.max(work, axis=0, keepdims=True)
        pos = jnp.min(jnp.where(work == m, rid, n), axis=0, keepdims=True)
        hit = rid == pos
        if cidx is None:
            picked = pos
        else:
            picked = jnp.sum(jnp.where(hit, cidx, 0), axis=0, keepdims=True)
        vals = jnp.where(kid == t, m, vals)
        idxs = jnp.where(kid == t, picked, idxs)
        work = jnp.where(hit, -jnp.inf, work)
    return vals, idxs


def _product_candidates(v1, i1, v2, i2, k):
    assert k == 2 * SUBLANES
    sub = lax.broadcasted_iota(jnp.int32, (SUBLANES, v1.shape[1]), 0)
    vals, idxs = [], []
    for a in range(k // 2):
        nb = k // (a + 1)
        width = k if nb > SUBLANES else SUBLANES
        cv = v1[a:a + 1, :] + v2[:width, :]
        ci = i1[a:a + 1, :] * PEER_N_KEYS + i2[:width, :]
        if nb < SUBLANES:
            cv = jnp.where(sub < nb, cv, -jnp.inf)
        vals.append(cv)
        idxs.append(ci)
    vals.append(v1[k // 2:, :] + v2[0:1, :])
    idxs.append(i1[k // 2:, :] * PEER_N_KEYS + i2[0:1, :])
    return jnp.concatenate(vals, axis=0), jnp.concatenate(idxs, axis=0)


def _peer_route_kernel(q_ref, keys_ref, e_ref, g_ref):
    tn = q_ref.shape[0]
    k = PEER_TOPK
    for h in range(PEER_HEADS):
        sub = []
        for p in range(2):
            c0 = (h * 2 + p) * PEER_N_KEYS
            s = _dot_nt(keys_ref[h, p], q_ref[:, c0:c0 + PEER_N_KEYS], HIGHEST)
            sub.append(_topk_rows(s, None, k))
        (v1, i1), (v2, i2) = sub
        cand, cidx = _product_candidates(v1, i1, v2, i2, k)
        best, eidx = _topk_rows(cand, cidx, k)
        ex = jnp.exp(best - jnp.max(best, axis=0, keepdims=True))
        gate = ex / jnp.sum(ex, axis=0, keepdims=True)
        e_ref[h * k:(h + 1) * k, :] = eidx
        g_ref[h * k:(h + 1) * k, :] = gate


HALF_KEYS = PEER_N_KEYS // 2
BF16_HIGH_MASK = 0xFFFF0000


def _peer_gate_kernel(et_ref, gt_ref, o_ref, e_scr, g_scr):
    tg = et_ref.shape[1]
    e_scr[...] = et_ref[...].T
    g_scr[...] = gt_ref[...].T
    kid = lax.broadcasted_iota(jnp.int32, (PEER_N_KEYS, PEER_N_KEYS), 0)

    def body(n8, carry):
        rows = pl.ds(pl.multiple_of(n8 * SUBLANES, SUBLANES), SUBLANES)
        e8 = e_scr[rows, :]
        g8 = g_scr[rows, :]
        a8 = jnp.right_shift(e8, 7)
        b8 = jnp.bitwise_and(e8, PEER_N_KEYS - 1)
        words = []
        for s in range(SUBLANES):
            at = jnp.where(a8[s:s + 1, :] == kid, g8[s:s + 1, :], 0.0).astype(BF16)
            bt = jnp.where(b8[s:s + 1, :] == kid, 1.0, 0.0).astype(BF16)
            grid = _dot_nt(at, bt)
            bits = pltpu.bitcast(grid.astype(BF16).astype(F32), jnp.uint32)
            words.append(jnp.bitwise_or(jnp.bitwise_and(bits[HALF_KEYS:, :], jnp.uint32(BF16_HIGH_MASK)),
                                        jnp.right_shift(bits[:HALF_KEYS, :], jnp.uint32(16))))
        o_ref[:, rows, :] = jnp.swapaxes(jnp.stack(words, axis=0), 0, 1)
        return carry

    lax.fori_loop(0, tg // SUBLANES, body, 0, unroll=4)


def _peer_route(qp, sub_keys, *, tn):
    n = qp.shape[0]
    rows = PEER_HEADS * PEER_TOPK
    e_t, g_t = pl.pallas_call(
        _peer_route_kernel,
        out_shape=(jax.ShapeDtypeStruct((rows, n), jnp.int32), jax.ShapeDtypeStruct((rows, n), F32)),
        grid=(n // tn,),
        in_specs=[
            pl.BlockSpec((tn, qp.shape[1]), lambda i: (i, 0)),
            pl.BlockSpec(sub_keys.shape, lambda i: (0, 0, 0, 0)),
        ],
        out_specs=(pl.BlockSpec((rows, tn), lambda i: (0, i)), pl.BlockSpec((rows, tn), lambda i: (0, i))),
        compiler_params=_cparams(("parallel",)),
        name="peer_route",
    )(qp, sub_keys)
    return pl.pallas_call(
        _peer_gate_kernel,
        out_shape=jax.ShapeDtypeStruct((HALF_KEYS, n, PEER_N_KEYS), jnp.uint32),
        grid=(n // tn,),
        in_specs=[pl.BlockSpec((rows, tn), lambda i: (0, i)), pl.BlockSpec((rows, tn), lambda i: (0, i))],
        out_specs=pl.BlockSpec((HALF_KEYS, tn, PEER_N_KEYS), lambda i: (0, i, 0)),
        scratch_shapes=[pltpu.VMEM((tn, rows), jnp.int32), pltpu.VMEM((tn, rows), F32)],
        compiler_params=_cparams(("parallel",)),
        name="peer_gates",
    )(e_t, g_t)


def _gelu_tanh(x):
    return 0.5 * x * (1.0 + jnp.tanh(math.sqrt(2.0 / math.pi) * (x + 0.044715 * (x * x * x))))


PEER_BLOCKS_PER_STEP = 4
PEER_TN = 1024


def _peer_expert_kernel(x_ref, gn_ref, gpa_ref, gpb_ref, *rest):
    nblk = PEER_BLOCKS_PER_STEP
    u_refs, v_refs = rest[:nblk], rest[nblk:2 * nblk]
    gf_ref, o_ref, h_scr, acc_scr, coef_scr = rest[2 * nblk:]
    s = pl.program_id(1)
    last = pl.num_programs(1) - 1

    @pl.when(s == 0)
    def _():
        h_scr[...] = _rms(x_ref[...], gn_ref[...]).astype(BF16)
        acc_scr[...] = jnp.zeros(acc_scr.shape, F32)
        coef_scr[...] = jnp.zeros(coef_scr.shape, BF16)

    v_cat = jnp.concatenate([v[...] for v in v_refs], axis=0)
    acc_scr[...] += _dot(coef_scr[(s + 1) % 2], v_cat)

    def unpack(word):
        return (pltpu.bitcast(jnp.left_shift(word, jnp.uint32(16)), F32),
                pltpu.bitcast(jnp.bitwise_and(word, jnp.uint32(BF16_HIGH_MASK)), F32))

    a_lo, a_hi = unpack(gpa_ref[...])
    b_lo, b_hi = unpack(gpb_ref[...])
    u_cat = jnp.concatenate([u[...] for u in u_refs], axis=0)
    hid = _gelu_tanh(_dot_nt(h_scr[...], u_cat))
    coef_scr[s % 2] = (jnp.concatenate([a_lo, b_lo, a_hi, b_hi], axis=1) * hid).astype(BF16)

    @pl.when(s == last)
    def _():
        o_ref[...] = _rms(x_ref[...] + acc_scr[...], gf_ref[...])


def _peer_experts(x, norm_g, gates, u_tab, v_tab, final_g, *, tn):
    n, d = x.shape
    eb = PEER_N_KEYS
    assert u_tab.shape[0] == eb * eb
    nblk = PEER_BLOCKS_PER_STEP
    ns = eb // nblk

    def cur(s):
        return jnp.minimum(s, ns - 1)

    def prev(s):
        return jnp.maximum(s - 1, 0)

    def table(which, q):
        return pl.BlockSpec((eb, d), lambda i, s: (which(s) + q * ns, 0))

    return pl.pallas_call(
        _peer_expert_kernel,
        out_shape=jax.ShapeDtypeStruct((n, d), F32),
        grid=(n // tn, ns + 1),
        in_specs=[
            pl.BlockSpec((tn, d), lambda i, s: (i, 0), pipeline_mode=pl.Buffered(1)),
            pl.BlockSpec((1, d), lambda i, s: (0, 0)),
            pl.BlockSpec((None, tn, eb), lambda i, s: (cur(s), i, 0)),
            pl.BlockSpec((None, tn, eb), lambda i, s: (cur(s) + ns, i, 0)),
        ] + [table(cur, q) for q in range(nblk)] + [table(prev, q) for q in range(nblk)] + [
            pl.BlockSpec((1, d), lambda i, s: (0, 0)),
        ],
        out_specs=pl.BlockSpec((tn, d), lambda i, s: (i, 0), pipeline_mode=pl.Buffered(1)),
        scratch_shapes=[pltpu.VMEM((tn, d), BF16), pltpu.VMEM((tn, d), F32),
                        pltpu.VMEM((2, tn, nblk * eb), BF16)],
        compiler_params=_cparams(("parallel", "arbitrary")),
        name="peer_experts",
    )(x, norm_g.reshape(1, d), gates, gates, *([u_tab] * nblk), *([v_tab] * nblk), final_g.reshape(1, d))


def _layer(x, wts, layer, mem_k, mem_v, conv_prev, gdn_state, k_past, v_past, *, tm):
    bsz, t, d = x.shape
    n = bsz * t
    x2d = x.reshape(n, d)
    main, k_new, v_new, tail = _in_proj(x2d, wts["norm_mix_g"], wts["w_in"], tm=min(IN_TM, n))
    conv_dim = 3 * GDN_WIDTH
    conv_prev8 = jnp.concatenate(
        [jnp.zeros((bsz, SUBLANES - (GDN_CONV - 1), conv_dim), F32), conv_prev.astype(F32)], axis=1)
    o_a, s_new = _gdn(main, tail, conv_prev8, gdn_state.astype(F32), wts["gdn_conv_w"], wts["gdn_a_log"],
                      wts["gdn_dt_bias"], wts["gdn_norm_g"], bsz=bsz, t=t)
    conv_new = main.reshape(bsz, t, -1)[:, t - (GDN_CONV - 1):, :conv_dim]
    lam_init = 0.8 - 0.6 * math.exp(-0.3 * layer)
    lams = tuple(wts[k].reshape(1, -1) for k in ("diff_lambda_q1", "diff_lambda_k1", "diff_lambda_q2", "diff_lambda_k2"))
    if k_past is None:
        o_b = _diff_prompt(main, k_new, v_new, lams, wts["diff_subln_g"], bsz=bsz, t=t, lam_init=lam_init)
    else:
        o_b = _diff_sample(main, k_new, v_new, k_past, v_past, lams, wts["diff_subln_g"], bsz=bsz, t=t,
                           lam_init=lam_init)
    k_rows = k_new.reshape(bsz, t, DIFF_HEADS, HEAD_DIM)
    v_rows = v_new.reshape(bsz, t, DIFF_HEADS, HEAD_DIM)
    merged = _merge(o_a, o_b, wts["w_branch_a"], wts["w_branch_b"], tail, tm=tm)
    x1 = _matmul(merged, wts["w_out"], residual=x2d, tm=tm, tn=MM_TN, name="out_proj")
    qm = _matmul(x1, wts["w_mq"], norm_g=wts["norm_cross_g"], tm=tm, tn=MM_TN, out_dtype=BF16, name="mem_q")
    mt = mem_k.shape[1]
    oc = _cross_attend(qm, mem_k.reshape(bsz, mt, d), mem_v.reshape(bsz, mt, d), bsz=bsz, t=t, tm=min(tm, t))
    x2 = _matmul(oc, wts["w_mo"], residual=x1, tm=tm, tn=MM_TN, name="mem_o")
    qp = _matmul(x2, wts["peer_w_q"], norm_g=wts["norm_ffn_g"], tm=tm, tn=MM_TN, name="peer_q")
    tn = min(PEER_TN, n)
    gates = _peer_route(qp, wts["peer_sub_keys"], tn=LANES)
    y = _peer_experts(x2, wts["norm_ffn_g"], gates, wts["peer_u"], wts["peer_v"], wts["final_norm_g"], tn=tn)
    return y.reshape(bsz, t, d), k_rows, v_rows, s_new, conv_new


def kernel(x_prompt, x_sample, cache_diff_k, cache_diff_v, state_gdn, state_conv, cache_mem_k, cache_mem_v,
           mem_prompt, norm_mix_g, w_in, gdn_conv_w, gdn_a_log, gdn_dt_bias, gdn_norm_g,
           diff_lambda_q1, diff_lambda_k1, diff_lambda_q2, diff_lambda_k2, diff_subln_g,
           w_branch_a, w_branch_b, w_out, norm_cross_g, norm_mem_g, w_mq, w_mk, w_mv, w_mo,
           norm_ffn_g, peer_w_q, peer_sub_keys, peer_u, peer_v, final_norm_g):
    depth = w_in.shape[0]
    assert depth == 1, "final norm is fused into the last layer's PEER kernel"
    l = 0
    wts = {
        "norm_mix_g": norm_mix_g[l], "w_in": _in_proj_weights(w_in[l]), "gdn_conv_w": gdn_conv_w[l],
        "gdn_a_log": gdn_a_log[l], "gdn_dt_bias": gdn_dt_bias[l], "gdn_norm_g": gdn_norm_g[l],
        "diff_lambda_q1": diff_lambda_q1[l], "diff_lambda_k1": diff_lambda_k1[l],
        "diff_lambda_q2": diff_lambda_q2[l], "diff_lambda_k2": diff_lambda_k2[l],
        "diff_subln_g": diff_subln_g[l], "w_branch_a": w_branch_a[l].astype(BF16),
        "w_branch_b": w_branch_b[l].astype(BF16), "w_out": w_out[l].astype(BF16),
        "norm_cross_g": norm_cross_g[l], "w_mq": w_mq[l].astype(BF16), "w_mo": w_mo[l].astype(BF16),
        "norm_ffn_g": norm_ffn_g[l], "peer_w_q": peer_w_q[l].astype(BF16), "peer_sub_keys": peer_sub_keys[l],
        "peer_u": peer_u[l].astype(BF16), "peer_v": peer_v[l].astype(BF16), "final_norm_g": final_norm_g,
    }
    bp, tp, d = x_prompt.shape
    bs, ts, _ = x_sample.shape
    mem2d = mem_prompt.reshape(-1, d)
    tmm = min(512, mem2d.shape[0])
    mem_k = _matmul(mem2d, w_mk[l].astype(BF16), norm_g=norm_mem_g[l], tm=tmm, tn=512, name="mem_k")
    mem_v = _matmul(mem2d, w_mv[l].astype(BF16), norm_g=norm_mem_g[l], tm=tmm, tn=512, name="mem_v")
    mshape = mem_prompt.shape[:2] + (MEM_HEADS, d // MEM_HEADS)
    mem_k = mem_k.reshape(mshape)
    mem_v = mem_v.reshape(mshape)
    conv0 = jnp.zeros((bp, GDN_CONV - 1, 3 * GDN_WIDTH), F32)
    s0 = jnp.zeros((bp, GDN_HEADS, HEAD_DIM, HEAD_DIM), F32)
    yp, pk, pv, ps, pc = _layer(x_prompt, wts, l, mem_k, mem_v, conv0, s0, None, None, tm=min(1024, bp * tp))
    ys, sk, sv, ss, sc = _layer(x_sample, wts, l, cache_mem_k[l], cache_mem_v[l], state_conv[l], state_gdn[l],
                                cache_diff_k[l], cache_diff_v[l], tm=min(256, bs * ts))
    return (yp, ys, pk[None], pv[None], ps[None], pc[None], mem_k[None], mem_v[None],
            sk[None], sv[None], ss[None], sc[None])
```

```python
import functools
import math

import jax
import jax.numpy as jnp
from jax import lax
from jax.experimental import pallas as pl
from jax.experimental.pallas import tpu as pltpu

F32 = jnp.float32
BF16 = jnp.bfloat16
HIGHEST = lax.Precision.HIGHEST

RMS_EPS = 1e-6
CHUNK = 64
CHUNK_SHIFT = 6
DIFF_TQ = 1024
GDN_CHUNKS_PER_STEP = 2
DIFF_SUB = 256
GDN_HEADS = 8
HEAD_DIM = 128
GDN_WIDTH = GDN_HEADS * HEAD_DIM
GDN_CONV = 4
DIFF_HEADS = 8
DIFF_HEAD_DIM = 64
MEM_HEADS = 4
PEER_HEADS = 8
PEER_N_KEYS = 128
PEER_TOPK = 16
LANES = 128
SUBLANES = 8
V7X_VMEM_LIMIT = 52 * 1024 * 1024
MM_TN = 1024


def _cparams(sem, vmem=V7X_VMEM_LIMIT):
    return pltpu.CompilerParams(dimension_semantics=sem, vmem_limit_bytes=vmem)


def _dot(a, b, precision=None):
    return jnp.dot(a, b, preferred_element_type=F32, precision=precision)


def _dot_nt(a, b, precision=None):
    return lax.dot_general(a, b, (((1,), (1,)), ((), ())), preferred_element_type=F32, precision=precision)


def _split_bf16(x):
    hi = x.astype(BF16)
    return hi, (x - hi.astype(F32)).astype(BF16)


def _cat_parts(p, q):
    return jnp.concatenate([p[0], q[0]], axis=0), jnp.concatenate([p[1], q[1]], axis=0)


def _dot3_parts(a_parts, b_parts, dot):
    a_hi, a_lo = a_parts
    b_hi, b_lo = b_parts
    m = a_hi.shape[0]
    both = dot(jnp.concatenate([a_hi, a_lo], axis=0), b_hi)
    return both[:m] + both[m:] + dot(a_hi, b_lo)


def _dot3(a, b):
    return _dot3_parts(_split_bf16(a), _split_bf16(b), _dot)


def _rms(xf, g):
    return xf * lax.rsqrt(jnp.mean(xf * xf, axis=-1, keepdims=True) + RMS_EPS) * g


def _mm_kernel(*refs, has_norm, has_res):
    it = iter(refs)
    x_ref = next(it)
    g_ref = next(it) if has_norm else None
    w_ref = next(it)
    r_ref = next(it) if has_res else None
    o_ref = next(it)
    h_scr = next(it)

    @pl.when(pl.program_id(1) == 0)
    def _():
        xf = x_ref[...].astype(F32)
        if has_norm:
            xf = _rms(xf, g_ref[...])
        h_scr[...] = xf.astype(BF16)

    acc = _dot(h_scr[...], w_ref[...])
    if has_res:
        acc = acc + r_ref[...]
    o_ref[...] = acc.astype(o_ref.dtype)


def _matmul(x, w, *, norm_g=None, residual=None, tm, tn, out_dtype=F32, name="matmul"):
    m, k = x.shape
    n = w.shape[1]
    assert m % tm == 0 and n % tn == 0, (m, n, tm, tn)
    in_specs = [pl.BlockSpec((tm, k), lambda i, j: (i, 0))]
    args = [x]
    if norm_g is not None:
        in_specs.append(pl.BlockSpec((1, k), lambda i, j: (0, 0)))
        args.append(norm_g.reshape(1, k).astype(F32))
    in_specs.append(pl.BlockSpec((k, tn), lambda i, j: (0, j)))
    args.append(w)
    if residual is not None:
        in_specs.append(pl.BlockSpec((tm, tn), lambda i, j: (i, j)))
        args.append(residual)
    return pl.pallas_call(
        functools.partial(_mm_kernel, has_norm=norm_g is not None, has_res=residual is not None),
        out_shape=jax.ShapeDtypeStruct((m, n), out_dtype),
        grid=(m // tm, n // tn),
        in_specs=in_specs,
        out_specs=pl.BlockSpec((tm, tn), lambda i, j: (i, j)),
        scratch_shapes=[pltpu.VMEM((tm, k), BF16)],
        compiler_params=_cparams(("parallel", "arbitrary")),
        name=name,
    )(*args)


IN_TN = 512
IN_TM = 1024
IN_W1_TILES = 8
IN_W2_TILES = 14
IN_MAIN_TILES = 10
IN_K_TILE0 = 10
IN_V_TILE0 = 12
IN_TAIL_TILE0 = 14
IN_TILES = IN_W1_TILES + IN_W2_TILES + 1
MAIN_COLS = IN_MAIN_TILES * IN_TN
TAIL_COLS = (IN_TILES - IN_TAIL_TILE0) * IN_TN
COL_DQ_MAIN = 4096
COL_GB_TAIL = 2048
COL_BA_TAIL = 4096


def _in_proj_kernel(x_ref, g_ref, w1_ref, w2_ref, w3_ref, main_ref, k_ref, v_ref, tail_ref, h_scr):
    j = pl.program_id(1)

    @pl.when(j == 0)
    def _():
        h_scr[...] = _rms(x_ref[...], g_ref[...]).astype(BF16)

    routes = ((0, IN_W1_TILES, w1_ref, main_ref),
              (IN_W1_TILES, IN_K_TILE0, w2_ref, main_ref),
              (IN_K_TILE0, IN_V_TILE0, w2_ref, k_ref),
              (IN_V_TILE0, IN_TAIL_TILE0, w2_ref, v_ref),
              (IN_TAIL_TILE0, IN_TILES - 1, w2_ref, tail_ref),
              (IN_TILES - 1, IN_TILES, w3_ref, tail_ref))
    for lo, hi, w_ref, dst in routes:
        @pl.when(jnp.logical_and(j >= lo, j < hi))
        def _(w_ref=w_ref, dst=dst):
            dst[...] = _dot(h_scr[...], w_ref[...])


def _in_proj_weights(w_in):
    c_z = 4 * GDN_WIDTH
    c_ba = c_z + 2 * GDN_HEADS
    assert c_z == IN_W1_TILES * IN_TN and w_in.shape[1] - c_ba == IN_W2_TILES * IN_TN
    w3 = jnp.pad(w_in[:, c_z:c_ba], ((0, 0), (0, IN_TN - 2 * GDN_HEADS)))
    return w_in[:, :c_z].astype(BF16), w_in[:, c_ba:].astype(BF16), w3.astype(BF16)


def _in_proj(x, norm_g, w_pieces, *, tm):
    m, d = x.shape
    w1, w2, w3 = w_pieces
    tn = IN_TN
    out = lambda cols: jax.ShapeDtypeStruct((m, cols), F32)
    return pl.pallas_call(
        _in_proj_kernel,
        out_shape=(out(MAIN_COLS), out(2 * IN_TN), out(2 * IN_TN), out(TAIL_COLS)),
        grid=(m // tm, IN_TILES),
        in_specs=[
            pl.BlockSpec((tm, d), lambda i, j: (i, 0)),
            pl.BlockSpec((1, d), lambda i, j: (0, 0)),
            pl.BlockSpec((d, tn), lambda i, j: (0, jnp.minimum(j, IN_W1_TILES - 1))),
            pl.BlockSpec((d, tn), lambda i, j: (0, jnp.clip(j - IN_W1_TILES, 0, IN_W2_TILES - 1))),
            pl.BlockSpec((d, tn), lambda i, j: (0, 0)),
        ],
        out_specs=(
            pl.BlockSpec((tm, tn), lambda i, j: (i, jnp.minimum(j, IN_MAIN_TILES - 1))),
            pl.BlockSpec((tm, tn), lambda i, j: (i, jnp.clip(j - IN_K_TILE0, 0, 1))),
            pl.BlockSpec((tm, tn), lambda i, j: (i, jnp.clip(j - IN_V_TILE0, 0, 1))),
            pl.BlockSpec((tm, tn), lambda i, j: (i, jnp.clip(j - IN_TAIL_TILE0, 0, IN_TILES - IN_TAIL_TILE0 - 1))),
        ),
        scratch_shapes=[pltpu.VMEM((tm, d), BF16)],
        compiler_params=_cparams(("parallel", "arbitrary")),
        name="in_proj",
    )(x, norm_g.reshape(1, d).astype(F32), w1, w2, w3)


def _gdn_kernel(qkv_ref, z_ref, ba_ref, prev_ref, s0_ref, convw_ref, alog_ref, dtb_ref, ng_ref,
                o_ref, sfin_ref, s_scr, xbuf, *, L, nch):
    step = pl.program_id(1)
    rows = nch * L

    @pl.when(step == 0)
    def _():
        s_scr[...] = s0_ref[0]
        xbuf[0:SUBLANES, :] = prev_ref[0]

    xbuf[SUBLANES:SUBLANES + rows, :] = qkv_ref[...]
    base = SUBLANES - (GDN_CONV - 1)
    y = xbuf[base:base + rows, :] * convw_ref[0:1, :]
    for i in range(1, GDN_CONV):
        y = y + xbuf[base + i:base + i + rows, :] * convw_ref[i:i + 1, :]
    y = y * jax.nn.sigmoid(y)
    carry = xbuf[rows:rows + SUBLANES, :]
    xbuf[0:SUBLANES, :] = carry

    row = lax.broadcasted_iota(jnp.int32, (L, L), 0)
    col = lax.broadcasted_iota(jnp.int32, (L, L), 1)
    lower = row >= col
    strict = row > col
    eye = (row == col).astype(F32)
    log_l = int(math.log2(L))
    n_double = log_l - 1
    assert 2 ** log_l == L

    ba = ba_ref[...]
    beta_all = jax.nn.sigmoid(ba)
    a_in = ba + dtb_ref[...]
    softplus = jnp.maximum(a_in, 0.0) + jnp.log(1.0 + jnp.exp(-jnp.abs(a_in)))
    g_all = -jnp.exp(alog_ref[...]) * softplus
    rr = lax.broadcasted_iota(jnp.int32, (rows, rows), 0)
    cc = lax.broadcasted_iota(jnp.int32, (rows, rows), 1)
    same_chunk_lower = jnp.logical_and(rr >= cc, jnp.right_shift(rr, log_l) == jnp.right_shift(cc, log_l))
    g_cum = _dot(same_chunk_lower.astype(F32), g_all, HIGHEST)
    g_cum_t = g_cum.T

    heads = range(GDN_HEADS)
    chains = [(c, h) for c in range(nch) for h in heads]
    ids = range(len(chains))

    def tok(c):
        return slice(c * L, (c + 1) * L)

    def head_cols(part, c, h):
        return y[tok(c), part * GDN_WIDTH + h * HEAD_DIM:part * GDN_WIDTH + (h + 1) * HEAD_DIM]

    q = [head_cols(0, c, h) for c, h in chains]
    k = [head_cols(1, c, h) for c, h in chains]
    v = [head_cols(2, c, h) for c, h in chains]
    q = [a * lax.rsqrt(jnp.sum(a * a, axis=-1, keepdims=True) + 1e-6) * (HEAD_DIM ** -0.5) for a in q]
    k = [a * lax.rsqrt(jnp.sum(a * a, axis=-1, keepdims=True) + 1e-6) for a in k]
    beta = [beta_all[tok(c), h:h + 1] for c, h in chains]
    g_col = [g_cum[tok(c), GDN_HEADS + h:GDN_HEADS + h + 1] for c, h in chains]
    g_row = [g_cum_t[GDN_HEADS + h:GDN_HEADS + h + 1, tok(c)] for c, h in chains]
    decay = [jnp.where(lower, jnp.exp(jnp.where(lower, g_col[n] - g_row[n], 0.0)), 0.0) for n in ids]
    g_last = [g_col[n][L - 1:L, :] for n in ids]
    e_col = [jnp.exp(g_col[n]) for n in ids]
    qk_kk = [_dot_nt(jnp.concatenate([q[n], k[n]], axis=0).astype(BF16), k[n].astype(BF16)) for n in ids]
    x = [-jnp.where(strict, beta[n] * qk_kk[n][L:] * decay[n], 0.0) for n in ids]
    t_inv = [eye + x[n] for n in ids]
    x_parts = [_split_bf16(x[n]) for n in ids]
    xp = [_dot3_parts(x_parts[n], x_parts[n], _dot) for n in ids]
    for _ in range(n_double - 1):
        xp_parts = [_split_bf16(xp[n]) for n in ids]
        both = [_dot3_parts(_cat_parts(_split_bf16(t_inv[n]), xp_parts[n]), xp_parts[n], _dot) for n in ids]
        t_inv = [t_inv[n] + both[n][:L] for n in ids]
        xp = [both[n][L:] for n in ids]
    t_inv = [t_inv[n] + _dot3(t_inv[n], xp[n]) for n in ids]
    sol = [_dot3(t_inv[n], jnp.concatenate([v[n] * beta[n], k[n] * (beta[n] * e_col[n])], axis=-1)) for n in ids]
    qk = [jnp.where(lower, qk_kk[n][:L] * decay[n], 0.0) for n in ids]
    k_dec_t = [(k[n] * jnp.exp(g_last[n] - g_col[n])).T for n in ids]
    left_s = [jnp.concatenate([sol[n][:, HEAD_DIM:], q[n] * e_col[n]], axis=0).astype(BF16) for n in ids]
    left_v = [jnp.concatenate([qk[n], k_dec_t[n]], axis=0).astype(BF16) for n in ids]

    s = [s_scr[h] for h in heads]
    for c in range(nch):
        n0 = c * GDN_HEADS
        ws_qs = [_dot(left_s[n0 + h], s[h].astype(BF16)) for h in heads]
        v_new = [sol[n0 + h][:, :HEAD_DIM] - ws_qs[h][:L] for h in heads]
        tail = [_dot(left_v[n0 + h], v_new[h].astype(BF16)) for h in heads]
        s = [s[h] * jnp.exp(g_last[n0 + h]) + tail[h][L:] for h in heads]
        for h in heads:
            zf = z_ref[tok(c), h * HEAD_DIM:(h + 1) * HEAD_DIM]
            o = _rms(ws_qs[h][L:] + tail[h][:L], ng_ref[...]) * (zf * jax.nn.sigmoid(zf))
            o_ref[tok(c), h * HEAD_DIM:(h + 1) * HEAD_DIM] = o.astype(o_ref.dtype)
    for h in heads:
        s_scr[h] = s[h]

    @pl.when(step == pl.num_programs(1) - 1)
    def _():
        sfin_ref[0] = s_scr[...]


def _decay_lanes(p):
    return jnp.zeros((1, LANES), F32).at[0, GDN_HEADS:2 * GDN_HEADS].set(p.astype(F32))


def _gdn(main, tail, conv_prev8, s0, conv_w, a_log, dt_bias, norm_g, *, bsz, t):
    L = min(CHUNK, t)
    nch = GDN_CHUNKS_PER_STEP if (t // L) % GDN_CHUNKS_PER_STEP == 0 else 1
    nc = t // (L * nch)
    rows = L * nch
    conv_dim = 3 * GDN_WIDTH
    kern = functools.partial(_gdn_kernel, L=L, nch=nch)
    o, s_fin = pl.pallas_call(
        kern,
        out_shape=(jax.ShapeDtypeStruct((bsz * t, GDN_WIDTH), BF16),
                   jax.ShapeDtypeStruct((bsz, GDN_HEADS, HEAD_DIM, HEAD_DIM), F32)),
        grid=(bsz, nc),
        in_specs=[
            pl.BlockSpec((rows, conv_dim), lambda b, c: (b * nc + c, 0)),
            pl.BlockSpec((rows, GDN_WIDTH), lambda b, c: (b * nc + c, conv_dim // GDN_WIDTH)),
            pl.BlockSpec((rows, LANES), lambda b, c: (b * nc + c, COL_BA_TAIL // LANES)),
            pl.BlockSpec((1, SUBLANES, conv_dim), lambda b, c: (b, 0, 0)),
            pl.BlockSpec((1, GDN_HEADS, HEAD_DIM, HEAD_DIM), lambda b, c: (b, 0, 0, 0)),
            pl.BlockSpec((GDN_CONV, conv_dim), lambda b, c: (0, 0)),
            pl.BlockSpec((1, LANES), lambda b, c: (0, 0)),
            pl.BlockSpec((1, LANES), lambda b, c: (0, 0)),
            pl.BlockSpec((1, HEAD_DIM), lambda b, c: (0, 0)),
        ],
        out_specs=(pl.BlockSpec((rows, GDN_WIDTH), lambda b, c: (b * nc + c, 0)),
                   pl.BlockSpec((1, GDN_HEADS, HEAD_DIM, HEAD_DIM), lambda b, c: (b, 0, 0, 0))),
        scratch_shapes=[pltpu.VMEM((GDN_HEADS, HEAD_DIM, HEAD_DIM), F32),
                        pltpu.VMEM((SUBLANES + rows + SUBLANES, conv_dim), F32)],
        compiler_params=_cparams(("parallel", "arbitrary")),
        name="gdn",
    )(main, main, tail, conv_prev8, s0, conv_w, _decay_lanes(a_log), _decay_lanes(dt_bias),
      norm_g.reshape(1, -1))
    return o, s_fin


def _lambda_value(lam_refs, lam_init):
    lq1, lk1, lq2, lk2 = (r[...] for r in lam_refs)
    return (jnp.exp(jnp.sum(lq1 * lk1, axis=-1, keepdims=True))
            - jnp.exp(jnp.sum(lq2 * lk2, axis=-1, keepdims=True)) + lam_init)


def _softmax_update(m_scr, l_scr, acc_scr, scores, values):
    idx = range(len(scores))
    m_old = [m_scr[n] for n in idx]
    m_new = [jnp.maximum(m_old[n], jnp.max(scores[n], axis=-1, keepdims=True)) for n in idx]
    p = [jnp.exp(scores[n] - m_new[n]) for n in idx]
    alpha = [jnp.exp(m_old[n] - m_new[n]) for n in idx]
    pv = [_dot(p[n].astype(BF16), values[n]) for n in idx]
    for n in idx:
        l_scr[n] = alpha[n] * l_scr[n] + jnp.sum(p[n], axis=-1, keepdims=True)
        acc_scr[n] = alpha[n] * acc_scr[n] + pv[n]
        m_scr[n] = m_new[n]


def _diff_finalize(o1, o2, lam_refs, g_ref, lam_init):
    lam = _lambda_value(lam_refs, lam_init)
    return _rms(o1 - lam * o2, g_ref[...]) * (1.0 - lam_init)


def _diff_init(m_scr, l_scr, acc_scr):
    m_scr[...] = jnp.full(m_scr.shape, -jnp.inf, F32)
    l_scr[...] = jnp.zeros(l_scr.shape, F32)
    acc_scr[...] = jnp.zeros(acc_scr.shape, F32)


def _near_bias(slope, r, q_pos, k_pos):
    visible = jnp.right_shift(k_pos, CHUNK_SHIFT) <= jnp.right_shift(q_pos, CHUNK_SHIFT)
    return jnp.where(visible, slope * (r - jnp.abs(q_pos - k_pos)).astype(F32), -jnp.inf)


def _split_distance(slope, d):
    return (-(slope * CHUNK) * jnp.right_shift(d, CHUNK_SHIFT).astype(F32),
            -slope * jnp.bitwise_and(d, CHUNK - 1).astype(F32))


def _diff_prompt_kernel(i_tab, j_tab, slopes_ref, q_ref, k_ref, v_ref, lq1, lk1, lq2, lk2, g_ref, o_ref,
                        m_scr, l_scr, acc_scr, lhs_scr, s_even, s_odd, *, tq, lam_init):
    h = pl.program_id(1)
    s_id = pl.program_id(2)
    n_pairs = pl.num_programs(2) - 1
    score_on = s_id < n_pairs
    consume_on = s_id >= 1
    i = i_tab[s_id]
    j = j_tab[s_id]
    ci = i_tab[jnp.maximum(s_id - 1, 0)]
    cj = j_tab[jnp.maximum(s_id - 1, 0)]
    slope = slopes_ref[h]
    half = DIFF_HEAD_DIM
    lane = lax.broadcasted_iota(jnp.int32, (tq, LANES), 1)

    @pl.when(jnp.logical_and(consume_on, cj == 0))
    def _():
        _diff_init(m_scr, l_scr, acc_scr)

    @pl.when(jnp.logical_and(score_on, j == 0))
    def _():
        qs = q_ref[...] * (DIFF_HEAD_DIM ** -0.5)
        lhs_scr[0] = jnp.where(lane < half, qs, 1.0).astype(BF16)
        lhs_scr[1] = jnp.where(lane >= half, qs, 1.0).astype(BF16)

    sub = min(DIFF_SUB, tq)
    units = [(n, qb) for qb in range(tq // sub) for n in range(2)]

    def score_tile(near, dst):
        kb = k_ref[...]
        if near:
            aug0 = aug1 = jnp.zeros_like(kb)
        else:
            d = (i - j) * tq - lax.broadcasted_iota(jnp.int32, (tq, LANES), 0)
            t_hi, t_lo = _split_distance(slope, d)
            aug0 = jnp.where(lane == half, t_hi, jnp.where(lane == half + 1, t_lo, 0.0))
            aug1 = jnp.where(lane == 0, t_hi, jnp.where(lane == 1, t_lo, 0.0))
        keys = [jnp.where(lane < half, kb, aug0).astype(BF16), jnp.where(lane >= half, kb, aug1).astype(BF16)]
        bias = None
        for n, qb in units:
            cols = slice(qb * sub, (qb + 1) * sub)
            if near:
                live = (qb + 1) * sub
                if n == 0:
                    k_pos = lax.broadcasted_iota(jnp.int32, (live, sub), 0)
                    q_pos = qb * sub + lax.broadcasted_iota(jnp.int32, (live, sub), 1)
                    bias = _near_bias(slope, q_pos, q_pos, k_pos)
                dst[n, :live, cols] = _dot_nt(keys[n][:live], lhs_scr[n, cols, :]) + bias
                if live < tq:
                    dst[n, live:, cols] = jnp.full((tq - live, sub), -jnp.inf, F32)
            else:
                dst[n, :, cols] = _dot_nt(keys[n], lhs_scr[n, cols, :])

    def consume_tile(src):
        vt = v_ref[...].T.astype(BF16)
        idx = range(len(units))
        cols = [slice(qb * sub, (qb + 1) * sub) for _, qb in units]
        s = [src[n, :, cols[u]] for u, (n, _) in enumerate(units)]
        m_old = [m_scr[n, :, cols[u]] for u, (n, _) in enumerate(units)]
        m_new = [jnp.maximum(m_old[u], jnp.max(s[u], axis=0, keepdims=True)) for u in idx]
        p = [jnp.exp(s[u] - m_new[u]) for u in idx]
        alpha = [jnp.exp(m_old[u] - m_new[u]) for u in idx]
        pv = [_dot(vt, p[u].astype(BF16)) for u in idx]
        for u, (n, _) in enumerate(units):
            l_scr[n, :, cols[u]] = alpha[u] * l_scr[n, :, cols[u]] + jnp.sum(p[u], axis=0, keepdims=True)
            acc_scr[n, :, cols[u]] = alpha[u] * acc_scr[n, :, cols[u]] + pv[u]
            m_scr[n, :, cols[u]] = m_new[u]

    def stage(cond, near, score, consume):
        for parity, (dst, src) in enumerate(((s_even, s_odd), (s_odd, s_even))):
            @pl.when(jnp.logical_and(cond, s_id % 2 == parity))
            def _(dst=dst, src=src):
                if score:
                    score_tile(near, dst)
                if consume:
                    consume_tile(src)

    both = jnp.logical_and(score_on, consume_on)
    stage(jnp.logical_and(both, j < i), False, True, True)
    stage(jnp.logical_and(both, j == i), True, True, True)
    stage(jnp.logical_not(consume_on), True, True, False)
    stage(jnp.logical_not(score_on), False, False, True)

    @pl.when(jnp.logical_and(consume_on, cj == ci))
    def _():
        o_ref[...] = _diff_finalize((acc_scr[0] / l_scr[0]).T, (acc_scr[1] / l_scr[1]).T,
                                    (lq1, lk1, lq2, lk2), g_ref, lam_init).astype(o_ref.dtype)


def _alibi_slopes():
    return jnp.asarray([2.0 ** (-8.0 * (i + 1) / DIFF_HEADS) for i in range(DIFF_HEADS)], F32)


def _diff_prompt(main, k_new, v_new, lams, subln_g, *, bsz, t, lam_init):
    tq = min(DIFF_TQ, t)
    nq = t // tq
    assert tq % CHUNK == 0 and t <= CHUNK * 255
    kern = functools.partial(_diff_prompt_kernel, tq=tq, lam_init=lam_init)
    qc, kc, vc = COL_DQ_MAIN // LANES, 0, 0
    pairs = [(i, j) for i in range(nq) for j in range(i + 1)]
    pairs.append(pairs[-1])
    i_tab = jnp.asarray([p[0] for p in pairs], jnp.int32)
    j_tab = jnp.asarray([p[1] for p in pairs], jnp.int32)
    zero = lambda b, h, s, it, jt: (0, 0)

    def consumed(tab, s):
        return tab[jnp.maximum(s - 1, 0)]

    grid_spec = pltpu.PrefetchScalarGridSpec(
        num_scalar_prefetch=2,
        grid=(bsz, DIFF_HEADS, len(pairs)),
        in_specs=[
            pl.BlockSpec(memory_space=pltpu.SMEM),
            pl.BlockSpec((tq, LANES), lambda b, h, s, it, jt: (b * nq + it[s], qc + h)),
            pl.BlockSpec((tq, LANES), lambda b, h, s, it, jt: (b * nq + jt[s], kc + h)),
            pl.BlockSpec((tq, LANES), lambda b, h, s, it, jt: (b * nq + consumed(jt, s), vc + h)),
        ] + [pl.BlockSpec((1, DIFF_HEAD_DIM), zero)] * 4 + [pl.BlockSpec((1, HEAD_DIM), zero)],
        out_specs=pl.BlockSpec((tq, LANES), lambda b, h, s, it, jt: (b * nq + consumed(it, s), h)),
        scratch_shapes=[pltpu.VMEM((2, 1, tq), F32), pltpu.VMEM((2, 1, tq), F32),
                        pltpu.VMEM((2, HEAD_DIM, tq), F32), pltpu.VMEM((2, tq, LANES), BF16),
                        pltpu.VMEM((2, tq, tq), F32), pltpu.VMEM((2, tq, tq), F32)],
    )
    return pl.pallas_call(
        kern,
        out_shape=jax.ShapeDtypeStruct((bsz * t, DIFF_HEADS * HEAD_DIM), BF16),
        grid_spec=grid_spec,
        compiler_params=_cparams(("parallel", "parallel", "arbitrary")),
        name="diff_attn_prompt",
    )(i_tab, j_tab, _alibi_slopes(), main, k_new, v_new, *lams, subln_g.reshape(1, -1))


def _diff_sample_kernel(slopes_ref, q_ref, kn_ref, vn_ref, kp_ref, vp_ref, lq1, lk1, lq2, lk2, g_ref, o_ref,
                        m_scr, l_scr, acc_scr, lhs_scr, *, t, tk, past, lam_init):
    j = pl.program_id(1)
    heads = range(DIFF_HEADS)

    def cols(h):
        return slice(h * HEAD_DIM, (h + 1) * HEAD_DIM)

    @pl.when(j == 0)
    def _():
        _diff_init(m_scr, l_scr, acc_scr)
        lane = lax.broadcasted_iota(jnp.int32, (t, LANES), 1)
        r = lax.broadcasted_iota(jnp.int32, (t, t), 0)
        c = lax.broadcasted_iota(jnp.int32, (t, t), 1)
        scores = []
        for h in heads:
            qs = q_ref[:, cols(h)] * (DIFF_HEAD_DIM ** -0.5)
            lhs_scr[h, 0:t, :] = jnp.where(lane < DIFF_HEAD_DIM, qs, 0.0).astype(BF16)
            lhs_scr[h, t:2 * t, :] = jnp.where(lane >= DIFF_HEAD_DIM, qs, 0.0).astype(BF16)
            bias = _near_bias(slopes_ref[h], r, past + r, past + c)
            scores.append(_dot_nt(lhs_scr[h], kn_ref[:, cols(h)].astype(BF16)) + jnp.concatenate([bias, bias], axis=0))
        _softmax_update(m_scr, l_scr, acc_scr, scores, [vn_ref[:, cols(h)].astype(BF16) for h in heads])

    d = (past - j * tk - lax.broadcasted_iota(jnp.int32, (1, tk), 1)).astype(F32)
    kt = jnp.swapaxes(kp_ref[...], 0, 1).astype(BF16)
    vt = jnp.swapaxes(vp_ref[...], 0, 1).astype(BF16)
    scores = [_dot_nt(lhs_scr[h], kt[h]) - slopes_ref[h] * d for h in heads]
    _softmax_update(m_scr, l_scr, acc_scr, scores, [vt[h] for h in heads])

    @pl.when(j == pl.num_programs(1) - 1)
    def _():
        for h in heads:
            o = acc_scr[h] / l_scr[h]
            o_ref[:, cols(h)] = _diff_finalize(o[:t], o[t:], (lq1, lk1, lq2, lk2), g_ref, lam_init).astype(o_ref.dtype)


def _diff_sample(main, k_new, v_new, k_past, v_past, lams, subln_g, *, bsz, t, lam_init):
    past = k_past.shape[1]
    tk = min(512, past)
    nk = past // tk
    width = DIFF_HEADS * HEAD_DIM
    kern = functools.partial(_diff_sample_kernel, t=t, tk=tk, past=past, lam_init=lam_init)
    zero = lambda b, j: (0, 0)
    cache = pl.BlockSpec((None, tk, DIFF_HEADS, HEAD_DIM), lambda b, j: (b, j, 0, 0))
    return pl.pallas_call(
        kern,
        out_shape=jax.ShapeDtypeStruct((bsz * t, width), BF16),
        grid=(bsz, nk),
        in_specs=[
            pl.BlockSpec(memory_space=pltpu.SMEM),
            pl.BlockSpec((t, width), lambda b, j: (b, COL_DQ_MAIN // width)),
            pl.BlockSpec((t, width), lambda b, j: (b, 0)),
            pl.BlockSpec((t, width), lambda b, j: (b, 0)),
            cache, cache,
        ] + [pl.BlockSpec((1, DIFF_HEAD_DIM), zero)] * 4 + [pl.BlockSpec((1, HEAD_DIM), zero)],
        out_specs=pl.BlockSpec((t, width), lambda b, j: (b, 0)),
        scratch_shapes=[pltpu.VMEM((DIFF_HEADS, 2 * t, 1), F32), pltpu.VMEM((DIFF_HEADS, 2 * t, 1), F32),
                        pltpu.VMEM((DIFF_HEADS, 2 * t, HEAD_DIM), F32), pltpu.VMEM((DIFF_HEADS, 2 * t, LANES), BF16)],
        compiler_params=_cparams(("parallel", "arbitrary")),
        name="diff_attn_sample",
    )(_alibi_slopes(), main, k_new, v_new, k_past, v_past, *lams, subln_g.reshape(1, -1))


def _merge_kernel(oa_ref, ob_ref, wa_ref, wb_ref, ga_ref, gb_ref, o_ref):
    ya = _dot(oa_ref[...], wa_ref[...])
    yb = _dot(ob_ref[...], wb_ref[...])
    o_ref[...] = (jax.nn.sigmoid(ga_ref[...]) * ya + jax.nn.sigmoid(gb_ref[...]) * yb).astype(o_ref.dtype)


def _merge(o_a, o_b, wa, wb, tail, *, tm):
    m = o_a.shape[0]
    d = wa.shape[1]
    tn = 512
    ga0, gb0 = 0, COL_GB_TAIL // tn
    return pl.pallas_call(
        _merge_kernel,
        out_shape=jax.ShapeDtypeStruct((m, d), BF16),
        grid=(m // tm, d // tn),
        in_specs=[
            pl.BlockSpec((tm, o_a.shape[1]), lambda i, j: (i, 0)),
            pl.BlockSpec((tm, o_b.shape[1]), lambda i, j: (i, 0)),
            pl.BlockSpec((wa.shape[0], tn), lambda i, j: (0, j)),
            pl.BlockSpec((wb.shape[0], tn), lambda i, j: (0, j)),
            pl.BlockSpec((tm, tn), lambda i, j: (i, ga0 + j)),
            pl.BlockSpec((tm, tn), lambda i, j: (i, gb0 + j)),
        ],
        out_specs=pl.BlockSpec((tm, tn), lambda i, j: (i, j)),
        compiler_params=_cparams(("parallel", "arbitrary")),
        name="merge",
    )(o_a, o_b, wa, wb, tail, tail)


def _cross_kernel(q_ref, k_ref, v_ref, o_ref):
    dh = q_ref.shape[-1] // MEM_HEADS
    for h in range(MEM_HEADS):
        sl = slice(h * dh, (h + 1) * dh)
        s = _dot_nt(q_ref[:, sl], k_ref[0, :, sl].astype(BF16)) * (dh ** -0.5)
        s = s - jnp.max(s, axis=-1, keepdims=True)
        p = jnp.exp(s)
        p = p / jnp.sum(p, axis=-1, keepdims=True)
        o_ref[:, sl] = _dot(p.astype(BF16), v_ref[0, :, sl].astype(BF16)).astype(o_ref.dtype)


def _cross_attend(q, mem_k, mem_v, *, bsz, t, tm):
    d = q.shape[1]
    nt = t // tm
    mt = mem_k.shape[1]
    return pl.pallas_call(
        _cross_kernel,
        out_shape=jax.ShapeDtypeStruct(q.shape, BF16),
        grid=(bsz, nt),
        in_specs=[
            pl.BlockSpec((tm, d), lambda b, i: (b * nt + i, 0)),
            pl.BlockSpec((1, mt, d), lambda b, i: (b, 0, 0)),
            pl.BlockSpec((1, mt, d), lambda b, i: (b, 0, 0)),
        ],
        out_specs=pl.BlockSpec((tm, d), lambda b, i: (b * nt + i, 0)),
        compiler_params=_cparams(("parallel", "arbitrary")),
        name="cross_attn",
    )(q, mem_k, mem_v)


def _topk_rows(work, cidx, k):
    n, tn = work.shape
    rid = lax.broadcasted_iota(jnp.int32, (n, tn), 0).astype(F32)
    kid = lax.broadcasted_iota(jnp.int32, (k, tn), 0)
    vals = jnp.zeros((k, tn), F32)
    idxs = jnp.zeros((k, tn), F32)
    for t in range(k):
        m = jnp.max(work, axis=0, keepdims=True)
        pos = jnp.min(jnp.where(work == m, rid, float(n)), axis=0, keepdims=True)
        hit = rid == pos
        if cidx is None:
            picked = pos
        else:
            picked = jnp.sum(jnp.where(hit, cidx, 0.0), axis=0, keepdims=True)
        vals = jnp.where(kid == t, m, vals)
        idxs = jnp.where(kid == t, picked, idxs)
        work = jnp.where(hit, -jnp.inf, work)
    return vals, idxs


def _product_candidates(v1, i1, v2, i2, k):
    assert k == 2 * SUBLANES
    sub = lax.broadcasted_iota(jnp.int32, (SUBLANES, v1.shape[1]), 0)
    vals, idxs = [], []
    for a in range(k // 2):
        nb = k // (a + 1)
        width = k if nb > SUBLANES else SUBLANES
        cv = v1[a:a + 1, :] + v2[:width, :]
        ci = i1[a:a + 1, :] * PEER_N_KEYS + i2[:width, :]
        if nb < SUBLANES:
            cv = jnp.where(sub < nb, cv, -jnp.inf)
        vals.append(cv)
        idxs.append(ci)
    vals.append(v1[k // 2:, :] + v2[0:1, :])
    idxs.append(i1[k // 2:, :] * PEER_N_KEYS + i2[0:1, :])
    return jnp.concatenate(vals, axis=0), jnp.concatenate(idxs, axis=0)


def _peer_route_kernel(q_ref, keys_hi_ref, keys_lo_ref, e_ref, g_ref):
    k = PEER_TOPK
    for h in range(PEER_HEADS):
        sub = []
        for p in range(2):
            c0 = (h * 2 + p) * PEER_N_KEYS
            q_parts = _split_bf16(q_ref[:, c0:c0 + PEER_N_KEYS])
            s = _dot3_parts((keys_hi_ref[h, p], keys_lo_ref[h, p]), q_parts, _dot_nt)
            sub.append(_topk_rows(s, None, k))
        (v1, i1), (v2, i2) = sub
        cand, cidx = _product_candidates(v1, i1, v2, i2, k)
        best, eidx = _topk_rows(cand, cidx, k)
        ex = jnp.exp(best - jnp.max(best, axis=0, keepdims=True))
        gate = ex / jnp.sum(ex, axis=0, keepdims=True)
        e_ref[h * k:(h + 1) * k, :] = eidx.astype(jnp.int32)
        g_ref[h * k:(h + 1) * k, :] = gate


HALF_KEYS = PEER_N_KEYS // 2
BF16_HIGH_MASK = 0xFFFF0000


def _peer_gate_kernel(et_ref, gt_ref, o_ref, e_scr, g_scr):
    tg = et_ref.shape[1]
    e_scr[...] = et_ref[...].T
    g_scr[...] = gt_ref[...].T
    kid = lax.broadcasted_iota(jnp.int32, (PEER_N_KEYS, PEER_N_KEYS), 0)

    def body(n8, carry):
        rows = pl.ds(pl.multiple_of(n8 * SUBLANES, SUBLANES), SUBLANES)
        e8 = e_scr[rows, :]
        g8 = g_scr[rows, :]
        a8 = jnp.right_shift(e8, 7)
        b8 = jnp.bitwise_and(e8, PEER_N_KEYS - 1)
        toks = range(SUBLANES)
        at = [jnp.where(a8[s:s + 1, :] == kid, g8[s:s + 1, :], 0.0).astype(BF16) for s in toks]
        bt = [jnp.where(b8[s:s + 1, :] == kid, 1.0, 0.0).astype(BF16) for s in toks]
        grids = [_dot_nt(at[s], bt[s]) for s in toks]
        bits = [pltpu.bitcast(grids[s].astype(BF16).astype(F32), jnp.uint32) for s in toks]
        words = [jnp.bitwise_or(jnp.bitwise_and(bits[s][HALF_KEYS:, :], jnp.uint32(BF16_HIGH_MASK)),
                                jnp.right_shift(bits[s][:HALF_KEYS, :], jnp.uint32(16))) for s in toks]
        o_ref[:, rows, :] = jnp.swapaxes(jnp.stack(words, axis=0), 0, 1)
        return carry

    lax.fori_loop(0, tg // SUBLANES, body, 0, unroll=4)


def _peer_route(qp, sub_keys, *, tn):
    n = qp.shape[0]
    rows = PEER_HEADS * PEER_TOPK
    keys_hi, keys_lo = _split_bf16(sub_keys.astype(F32))
    keys_spec = pl.BlockSpec(sub_keys.shape, lambda i: (0, 0, 0, 0))
    e_t, g_t = pl.pallas_call(
        _peer_route_kernel,
        out_shape=(jax.ShapeDtypeStruct((rows, n), jnp.int32), jax.ShapeDtypeStruct((rows, n), F32)),
        grid=(n // tn,),
        in_specs=[pl.BlockSpec((tn, qp.shape[1]), lambda i: (i, 0)), keys_spec, keys_spec],
        out_specs=(pl.BlockSpec((rows, tn), lambda i: (0, i)), pl.BlockSpec((rows, tn), lambda i: (0, i))),
        compiler_params=_cparams(("parallel",)),
        name="peer_route",
    )(qp, keys_hi, keys_lo)
    return pl.pallas_call(
        _peer_gate_kernel,
        out_shape=jax.ShapeDtypeStruct((HALF_KEYS, n, PEER_N_KEYS), jnp.uint32),
        grid=(n // tn,),
        in_specs=[pl.BlockSpec((rows, tn), lambda i: (0, i)), pl.BlockSpec((rows, tn), lambda i: (0, i))],
        out_specs=pl.BlockSpec((HALF_KEYS, tn, PEER_N_KEYS), lambda i: (0, i, 0)),
        scratch_shapes=[pltpu.VMEM((tn, rows), jnp.int32), pltpu.VMEM((tn, rows), F32)],
        compiler_params=_cparams(("parallel",)),
        name="peer_gates",
    )(e_t, g_t)


def _gelu_tanh(x):
    return 0.5 * x * (1.0 + jnp.tanh(math.sqrt(2.0 / math.pi) * (x + 0.044715 * (x * x * x))))


PEER_BLOCKS_PER_STEP = 4
PEER_TN = 1024


def _peer_expert_kernel(x_ref, gn_ref, gpa_ref, gpb_ref, *rest):
    nblk = PEER_BLOCKS_PER_STEP
    u_refs, v_refs = rest[:nblk], rest[nblk:2 * nblk]
    gf_ref, o_ref, h_scr, acc_scr, coef_scr = rest[2 * nblk:]
    s = pl.program_id(1)
    last = pl.num_programs(1) - 1

    @pl.when(s == 0)
    def _():
        h_scr[...] = _rms(x_ref[...], gn_ref[...]).astype(BF16)
        acc_scr[...] = jnp.zeros(acc_scr.shape, F32)
        coef_scr[...] = jnp.zeros(coef_scr.shape, BF16)

    v_cat = jnp.concatenate([v[...] for v in v_refs], axis=0)
    acc_scr[...] += _dot(coef_scr[(s + 1) % 2], v_cat)

    def unpack(word):
        return (pltpu.bitcast(jnp.left_shift(word, jnp.uint32(16)), F32),
                pltpu.bitcast(jnp.bitwise_and(word, jnp.uint32(BF16_HIGH_MASK)), F32))

    a_lo, a_hi = unpack(gpa_ref[...])
    b_lo, b_hi = unpack(gpb_ref[...])
    u_cat = jnp.concatenate([u[...] for u in u_refs], axis=0)
    hid = _gelu_tanh(_dot_nt(h_scr[...], u_cat))
    coef_scr[s % 2] = (jnp.concatenate([a_lo, b_lo, a_hi, b_hi], axis=1) * hid).astype(BF16)

    @pl.when(s == last)
    def _():
        o_ref[...] = _rms(x_ref[...] + acc_scr[...], gf_ref[...])


def _peer_experts(x, norm_g, gates, u_tab, v_tab, final_g, *, tn):
    n, d = x.shape
    eb = PEER_N_KEYS
    assert u_tab.shape[0] == eb * eb
    nblk = PEER_BLOCKS_PER_STEP
    ns = eb // nblk

    def cur(s):
        return jnp.minimum(s, ns - 1)

    def prev(s):
        return jnp.maximum(s - 1, 0)

    def table(which, q):
        return pl.BlockSpec((eb, d), lambda i, s: (which(s) + q * ns, 0))

    return pl.pallas_call(
        _peer_expert_kernel,
        out_shape=jax.ShapeDtypeStruct((n, d), F32),
        grid=(n // tn, ns + 1),
        in_specs=[
            pl.BlockSpec((tn, d), lambda i, s: (i, 0), pipeline_mode=pl.Buffered(1)),
            pl.BlockSpec((1, d), lambda i, s: (0, 0)),
            pl.BlockSpec((None, tn, eb), lambda i, s: (cur(s), i, 0)),
            pl.BlockSpec((None, tn, eb), lambda i, s: (cur(s) + ns, i, 0)),
        ] + [table(cur, q) for q in range(nblk)] + [table(prev, q) for q in range(nblk)] + [
            pl.BlockSpec((1, d), lambda i, s: (0, 0)),
        ],
        out_specs=pl.BlockSpec((tn, d), lambda i, s: (i, 0), pipeline_mode=pl.Buffered(1)),
        scratch_shapes=[pltpu.VMEM((tn, d), BF16), pltpu.VMEM((tn, d), F32),
                        pltpu.VMEM((2, tn, nblk * eb), BF16)],
        compiler_params=_cparams(("parallel", "arbitrary")),
        name="peer_experts",
    )(x, norm_g.reshape(1, d), gates, gates, *([u_tab] * nblk), *([v_tab] * nblk), final_g.reshape(1, d))


def _layer(x, wts, layer, mem_k, mem_v, conv_prev, gdn_state, k_past, v_past, *, tm):
    bsz, t, d = x.shape
    n = bsz * t
    x2d = x.reshape(n, d)
    main, k_new, v_new, tail = _in_proj(x2d, wts["norm_mix_g"], wts["w_in"], tm=min(IN_TM, n))
    conv_dim = 3 * GDN_WIDTH
    conv_prev8 = jnp.concatenate(
        [jnp.zeros((bsz, SUBLANES - (GDN_CONV - 1), conv_dim), F32), conv_prev.astype(F32)], axis=1)
    o_a, s_new = _gdn(main, tail, conv_prev8, gdn_state.astype(F32), wts["gdn_conv_w"], wts["gdn_a_log"],
                      wts["gdn_dt_bias"], wts["gdn_norm_g"], bsz=bsz, t=t)
    conv_new = main.reshape(bsz, t, -1)[:, t - (GDN_CONV - 1):, :conv_dim]
    lam_init = 0.8 - 0.6 * math.exp(-0.3 * layer)
    lams = tuple(wts[k].reshape(1, -1) for k in ("diff_lambda_q1", "diff_lambda_k1", "diff_lambda_q2", "diff_lambda_k2"))
    if k_past is None:
        o_b = _diff_prompt(main, k_new, v_new, lams, wts["diff_subln_g"], bsz=bsz, t=t, lam_init=lam_init)
    else:
        o_b = _diff_sample(main, k_new, v_new, k_past, v_past, lams, wts["diff_subln_g"], bsz=bsz, t=t,
                           lam_init=lam_init)
    k_rows = k_new.reshape(bsz, t, DIFF_HEADS, HEAD_DIM)
    v_rows = v_new.reshape(bsz, t, DIFF_HEADS, HEAD_DIM)
    merged = _merge(o_a, o_b, wts["w_branch_a"], wts["w_branch_b"], tail, tm=tm)
    x1 = _matmul(merged, wts["w_out"], residual=x2d, tm=tm, tn=MM_TN, name="out_proj")
    qm = _matmul(x1, wts["w_mq"], norm_g=wts["norm_cross_g"], tm=tm, tn=MM_TN, out_dtype=BF16, name="mem_q")
    mt = mem_k.shape[1]
    oc = _cross_attend(qm, mem_k.reshape(bsz, mt, d), mem_v.reshape(bsz, mt, d), bsz=bsz, t=t, tm=min(tm, t))
    x2 = _matmul(oc, wts["w_mo"], residual=x1, tm=tm, tn=MM_TN, name="mem_o")
    qp = _matmul(x2, wts["peer_w_q"], norm_g=wts["norm_ffn_g"], tm=tm, tn=MM_TN, name="peer_q")
    tn = min(PEER_TN, n)
    gates = _peer_route(qp, wts["peer_sub_keys"], tn=LANES)
    y = _peer_experts(x2, wts["norm_ffn_g"], gates, wts["peer_u"], wts["peer_v"], wts["final_norm_g"], tn=tn)
    return y.reshape(bsz, t, d), k_rows, v_rows, s_new, conv_new


def kernel(x_prompt, x_sample, cache_diff_k, cache_diff_v, state_gdn, state_conv, cache_mem_k, cache_mem_v,
           mem_prompt, norm_mix_g, w_in, gdn_conv_w, gdn_a_log, gdn_dt_bias, gdn_norm_g,
           diff_lambda_q1, diff_lambda_k1, diff_lambda_q2, diff_lambda_k2, diff_subln_g,
           w_branch_a, w_branch_b, w_out, norm_cross_g, norm_mem_g, w_mq, w_mk, w_mv, w_mo,
           norm_ffn_g, peer_w_q, peer_sub_keys, peer_u, peer_v, final_norm_g):
    depth = w_in.shape[0]
    assert depth == 1, "final norm is fused into the last layer's PEER kernel"
    l = 0
    wts = {
        "norm_mix_g": norm_mix_g[l], "w_in": _in_proj_weights(w_in[l]), "gdn_conv_w": gdn_conv_w[l],
        "gdn_a_log": gdn_a_log[l], "gdn_dt_bias": gdn_dt_bias[l], "gdn_norm_g": gdn_norm_g[l],
        "diff_lambda_q1": diff_lambda_q1[l], "diff_lambda_k1": diff_lambda_k1[l],
        "diff_lambda_q2": diff_lambda_q2[l], "diff_lambda_k2": diff_lambda_k2[l],
        "diff_subln_g": diff_subln_g[l], "w_branch_a": w_branch_a[l].astype(BF16),
        "w_branch_b": w_branch_b[l].astype(BF16), "w_out": w_out[l].astype(BF16),
        "norm_cross_g": norm_cross_g[l], "w_mq": w_mq[l].astype(BF16), "w_mo": w_mo[l].astype(BF16),
        "norm_ffn_g": norm_ffn_g[l], "peer_w_q": peer_w_q[l].astype(BF16), "peer_sub_keys": peer_sub_keys[l],
        "peer_u": peer_u[l].astype(BF16), "peer_v": peer_v[l].astype(BF16), "final_norm_g": final_norm_g,
    }
    bp, tp, d = x_prompt.shape
    bs, ts, _ = x_sample.shape
    mem2d = mem_prompt.reshape(-1, d)
    tmm = min(512, mem2d.shape[0])
    mem_k = _matmul(mem2d, w_mk[l].astype(BF16), norm_g=norm_mem_g[l], tm=tmm, tn=512, name="mem_k")
    mem_v = _matmul(mem2d, w_mv[l].astype(BF16), norm_g=norm_mem_g[l], tm=tmm, tn=512, name="mem_v")
    mshape = mem_prompt.shape[:2] + (MEM_HEADS, d // MEM_HEADS)
    mem_k = mem_k.reshape(mshape)
    mem_v = mem_v.reshape(mshape)
    conv0 = jnp.zeros((bp, GDN_CONV - 1, 3 * GDN_WIDTH), F32)
    s0 = jnp.zeros((bp, GDN_HEADS, HEAD_DIM, HEAD_DIM), F32)
    yp, pk, pv, ps, pc = _layer(x_prompt, wts, l, mem_k, mem_v, conv0, s0, None, None, tm=min(1024, bp * tp))
    ys, sk, sv, ss, sc = _layer(x_sample, wts, l, cache_mem_k[l], cache_mem_v[l], state_conv[l], state_gdn[l],
                                cache_diff_k[l], cache_diff_v[l], tm=min(256, bs * ts))
    return (yp, ys, pk[None], pv[None], ps[None], pc[None], mem_k[None], mem_v[None],
            sk[None], sv[None], ss[None], sc[None])
```

```python
import functools
import math

import jax
import jax.numpy as jnp
from jax import lax
from jax.experimental import pallas as pl
from jax.experimental.pallas import tpu as pltpu

F32 = jnp.float32
BF16 = jnp.bfloat16
HIGHEST = lax.Precision.HIGHEST

RMS_EPS = 1e-6
CHUNK = 64
CHUNK_SHIFT = 6
DIFF_TQ = 1024
GDN_CHUNKS_PER_STEP = 2
DIFF_SUB = 256
GDN_HEADS = 8
HEAD_DIM = 128
GDN_WIDTH = GDN_HEADS * HEAD_DIM
GDN_CONV = 4
DIFF_HEADS = 8
DIFF_HEAD_DIM = 64
MEM_HEADS = 4
PEER_HEADS = 8
PEER_N_KEYS = 128
PEER_TOPK = 16
LANES = 128
SUBLANES = 8
V7X_VMEM_LIMIT = 52 * 1024 * 1024
MM_TN = 1024


def _cparams(sem, vmem=V7X_VMEM_LIMIT):
    return pltpu.CompilerParams(dimension_semantics=sem, vmem_limit_bytes=vmem)


def _dot(a, b, precision=None):
    return jnp.dot(a, b, preferred_element_type=F32, precision=precision)


def _dot_nt(a, b, precision=None):
    return lax.dot_general(a, b, (((1,), (1,)), ((), ())), preferred_element_type=F32, precision=precision)


def _split_bf16(x):
    hi = x.astype(BF16)
    return hi, (x - hi.astype(F32)).astype(BF16)


def _cat_parts(p, q):
    return jnp.concatenate([p[0], q[0]], axis=0), jnp.concatenate([p[1], q[1]], axis=0)


def _dot3_parts(a_parts, b_parts, dot):
    a_hi, a_lo = a_parts
    b_hi, b_lo = b_parts
    m = a_hi.shape[0]
    both = dot(jnp.concatenate([a_hi, a_lo], axis=0), b_hi)
    return both[:m] + both[m:] + dot(a_hi, b_lo)


def _dot3(a, b):
    return _dot3_parts(_split_bf16(a), _split_bf16(b), _dot)


def _rms(xf, g):
    return xf * lax.rsqrt(jnp.mean(xf * xf, axis=-1, keepdims=True) + RMS_EPS) * g


def _mm_kernel(*refs, has_norm, has_res):
    it = iter(refs)
    x_ref = next(it)
    g_ref = next(it) if has_norm else None
    w_ref = next(it)
    r_ref = next(it) if has_res else None
    o_ref = next(it)
    h_scr = next(it)

    @pl.when(pl.program_id(1) == 0)
    def _():
        xf = x_ref[...].astype(F32)
        if has_norm:
            xf = _rms(xf, g_ref[...])
        h_scr[...] = xf.astype(BF16)

    acc = _dot(h_scr[...], w_ref[...])
    if has_res:
        acc = acc + r_ref[...]
    o_ref[...] = acc.astype(o_ref.dtype)


def _column_tiles(w, tn):
    k, n = w.shape
    return jnp.transpose(w.reshape(k, n // tn, tn), (1, 0, 2))


def _matmul(x, w, *, norm_g=None, residual=None, tm, tn, out_dtype=F32, name="matmul"):
    m, k = x.shape
    n = w.shape[1]
    assert m % tm == 0 and n % tn == 0, (m, n, tm, tn)
    in_specs = [pl.BlockSpec((tm, k), lambda i, j: (i, 0))]
    args = [x]
    if norm_g is not None:
        in_specs.append(pl.BlockSpec((1, k), lambda i, j: (0, 0)))
        args.append(norm_g.reshape(1, k).astype(F32))
    in_specs.append(pl.BlockSpec((None, k, tn), lambda i, j: (j, 0, 0)))
    args.append(_column_tiles(w, tn))
    if residual is not None:
        in_specs.append(pl.BlockSpec((tm, tn), lambda i, j: (i, j)))
        args.append(residual)
    return pl.pallas_call(
        functools.partial(_mm_kernel, has_norm=norm_g is not None, has_res=residual is not None),
        out_shape=jax.ShapeDtypeStruct((m, n), out_dtype),
        grid=(m // tm, n // tn),
        in_specs=in_specs,
        out_specs=pl.BlockSpec((tm, tn), lambda i, j: (i, j)),
        scratch_shapes=[pltpu.VMEM((tm, k), BF16)],
        compiler_params=_cparams(("parallel", "arbitrary")),
        name=name,
    )(*args)


IN_TN = 512
IN_TM = 1024
IN_W1_TILES = 8
IN_W2_TILES = 14
IN_MAIN_TILES = 10
IN_K_TILE0 = 10
IN_V_TILE0 = 12
IN_TAIL_TILE0 = 14
IN_TILES = IN_W1_TILES + IN_W2_TILES + 1
MAIN_COLS = IN_MAIN_TILES * IN_TN
TAIL_COLS = (IN_TILES - IN_TAIL_TILE0) * IN_TN
COL_DQ_MAIN = 4096
COL_GB_TAIL = 2048
COL_BA_TAIL = 4096


def _in_proj_kernel(x_ref, g_ref, w1_ref, w2_ref, w3_ref, main_ref, k_ref, v_ref, tail_ref, h_scr):
    j = pl.program_id(1)

    @pl.when(j == 0)
    def _():
        h_scr[...] = _rms(x_ref[...], g_ref[...]).astype(BF16)

    routes = ((0, IN_W1_TILES, w1_ref, main_ref),
              (IN_W1_TILES, IN_K_TILE0, w2_ref, main_ref),
              (IN_K_TILE0, IN_V_TILE0, w2_ref, k_ref),
              (IN_V_TILE0, IN_TAIL_TILE0, w2_ref, v_ref),
              (IN_TAIL_TILE0, IN_TILES - 1, w2_ref, tail_ref),
              (IN_TILES - 1, IN_TILES, w3_ref, tail_ref))
    for lo, hi, w_ref, dst in routes:
        @pl.when(jnp.logical_and(j >= lo, j < hi))
        def _(w_ref=w_ref, dst=dst):
            dst[...] = _dot(h_scr[...], w_ref[...])


def _in_proj_weights(w_in):
    c_z = 4 * GDN_WIDTH
    c_ba = c_z + 2 * GDN_HEADS
    assert c_z == IN_W1_TILES * IN_TN and w_in.shape[1] - c_ba == IN_W2_TILES * IN_TN
    w3 = jnp.pad(w_in[:, c_z:c_ba], ((0, 0), (0, IN_TN - 2 * GDN_HEADS)))
    return tuple(_column_tiles(w.astype(BF16), IN_TN) for w in (w_in[:, :c_z], w_in[:, c_ba:], w3))


def _in_proj(x, norm_g, w_pieces, *, tm):
    m, d = x.shape
    w1, w2, w3 = w_pieces
    tn = IN_TN
    out = lambda cols: jax.ShapeDtypeStruct((m, cols), F32)
    return pl.pallas_call(
        _in_proj_kernel,
        out_shape=(out(MAIN_COLS), out(2 * IN_TN), out(2 * IN_TN), out(TAIL_COLS)),
        grid=(m // tm, IN_TILES),
        in_specs=[
            pl.BlockSpec((tm, d), lambda i, j: (i, 0)),
            pl.BlockSpec((1, d), lambda i, j: (0, 0)),
            pl.BlockSpec((None, d, tn), lambda i, j: (jnp.minimum(j, IN_W1_TILES - 1), 0, 0)),
            pl.BlockSpec((None, d, tn), lambda i, j: (jnp.clip(j - IN_W1_TILES, 0, IN_W2_TILES - 1), 0, 0)),
            pl.BlockSpec((None, d, tn), lambda i, j: (0, 0, 0)),
        ],
        out_specs=(
            pl.BlockSpec((tm, tn), lambda i, j: (i, jnp.minimum(j, IN_MAIN_TILES - 1))),
            pl.BlockSpec((tm, tn), lambda i, j: (i, jnp.clip(j - IN_K_TILE0, 0, 1))),
            pl.BlockSpec((tm, tn), lambda i, j: (i, jnp.clip(j - IN_V_TILE0, 0, 1))),
            pl.BlockSpec((tm, tn), lambda i, j: (i, jnp.clip(j - IN_TAIL_TILE0, 0, IN_TILES - IN_TAIL_TILE0 - 1))),
        ),
        scratch_shapes=[pltpu.VMEM((tm, d), BF16)],
        compiler_params=_cparams(("parallel", "arbitrary")),
        name="in_proj",
    )(x, norm_g.reshape(1, d).astype(F32), w1, w2, w3)


def _gdn_kernel(qkv_ref, z_ref, ba_ref, prev_ref, s0_ref, convw_ref, alog_ref, dtb_ref, ng_ref,
                o_ref, sfin_ref, s_scr, xbuf, *, L, nch):
    step = pl.program_id(1)
    rows = nch * L

    @pl.when(step == 0)
    def _():
        s_scr[...] = s0_ref[0]
        xbuf[0:SUBLANES, :] = prev_ref[0]

    xbuf[SUBLANES:SUBLANES + rows, :] = qkv_ref[...]
    base = SUBLANES - (GDN_CONV - 1)
    y = xbuf[base:base + rows, :] * convw_ref[0:1, :]
    for i in range(1, GDN_CONV):
        y = y + xbuf[base + i:base + i + rows, :] * convw_ref[i:i + 1, :]
    y = y * jax.nn.sigmoid(y)
    carry = xbuf[rows:rows + SUBLANES, :]
    xbuf[0:SUBLANES, :] = carry

    row = lax.broadcasted_iota(jnp.int32, (L, L), 0)
    col = lax.broadcasted_iota(jnp.int32, (L, L), 1)
    lower = row >= col
    strict = row > col
    eye = (row == col).astype(F32)
    log_l = int(math.log2(L))
    n_double = log_l - 1
    assert 2 ** log_l == L

    ba = ba_ref[...]
    beta_all = jax.nn.sigmoid(ba)
    a_in = ba + dtb_ref[...]
    softplus = jnp.maximum(a_in, 0.0) + jnp.log(1.0 + jnp.exp(-jnp.abs(a_in)))
    g_all = -jnp.exp(alog_ref[...]) * softplus
    rr = lax.broadcasted_iota(jnp.int32, (rows, rows), 0)
    cc = lax.broadcasted_iota(jnp.int32, (rows, rows), 1)
    same_chunk_lower = jnp.logical_and(rr >= cc, jnp.right_shift(rr, log_l) == jnp.right_shift(cc, log_l))
    g_cum = _dot(same_chunk_lower.astype(F32), g_all, HIGHEST)
    g_cum_t = g_cum.T

    heads = range(GDN_HEADS)
    chains = [(c, h) for c in range(nch) for h in heads]
    ids = range(len(chains))

    def tok(c):
        return slice(c * L, (c + 1) * L)

    def head_cols(part, c, h):
        return y[tok(c), part * GDN_WIDTH + h * HEAD_DIM:part * GDN_WIDTH + (h + 1) * HEAD_DIM]

    q = [head_cols(0, c, h) for c, h in chains]
    k = [head_cols(1, c, h) for c, h in chains]
    v = [head_cols(2, c, h) for c, h in chains]
    q = [a * lax.rsqrt(jnp.sum(a * a, axis=-1, keepdims=True) + 1e-6) * (HEAD_DIM ** -0.5) for a in q]
    k = [a * lax.rsqrt(jnp.sum(a * a, axis=-1, keepdims=True) + 1e-6) for a in k]
    beta = [beta_all[tok(c), h:h + 1] for c, h in chains]
    g_col = [g_cum[tok(c), GDN_HEADS + h:GDN_HEADS + h + 1] for c, h in chains]
    g_row = [g_cum_t[GDN_HEADS + h:GDN_HEADS + h + 1, tok(c)] for c, h in chains]
    decay = [jnp.where(lower, jnp.exp(jnp.where(lower, g_col[n] - g_row[n], 0.0)), 0.0) for n in ids]
    g_last = [g_col[n][L - 1:L, :] for n in ids]
    e_col = [jnp.exp(g_col[n]) for n in ids]
    qk_kk = [_dot_nt(jnp.concatenate([q[n], k[n]], axis=0).astype(BF16), k[n].astype(BF16)) for n in ids]
    x = [-jnp.where(strict, beta[n] * qk_kk[n][L:] * decay[n], 0.0) for n in ids]
    t_inv = [eye + x[n] for n in ids]
    x_parts = [_split_bf16(x[n]) for n in ids]
    xp = [_dot3_parts(x_parts[n], x_parts[n], _dot) for n in ids]
    for _ in range(n_double - 1):
        xp_parts = [_split_bf16(xp[n]) for n in ids]
        both = [_dot3_parts(_cat_parts(_split_bf16(t_inv[n]), xp_parts[n]), xp_parts[n], _dot) for n in ids]
        t_inv = [t_inv[n] + both[n][:L] for n in ids]
        xp = [both[n][L:] for n in ids]
    t_inv = [t_inv[n] + _dot3(t_inv[n], xp[n]) for n in ids]
    sol = [_dot3(t_inv[n], jnp.concatenate([v[n] * beta[n], k[n] * (beta[n] * e_col[n])], axis=-1)) for n in ids]
    qk = [jnp.where(lower, qk_kk[n][:L] * decay[n], 0.0) for n in ids]
    k_dec_t = [(k[n] * jnp.exp(g_last[n] - g_col[n])).T for n in ids]
    left_s = [jnp.concatenate([sol[n][:, HEAD_DIM:], q[n] * e_col[n]], axis=0).astype(BF16) for n in ids]
    left_v = [jnp.concatenate([qk[n], k_dec_t[n]], axis=0).astype(BF16) for n in ids]

    s = [s_scr[h] for h in heads]
    for c in range(nch):
        n0 = c * GDN_HEADS
        ws_qs = [_dot(left_s[n0 + h], s[h].astype(BF16)) for h in heads]
        v_new = [sol[n0 + h][:, :HEAD_DIM] - ws_qs[h][:L] for h in heads]
        tail = [_dot(left_v[n0 + h], v_new[h].astype(BF16)) for h in heads]
        s = [s[h] * jnp.exp(g_last[n0 + h]) + tail[h][L:] for h in heads]
        for h in heads:
            zf = z_ref[tok(c), h * HEAD_DIM:(h + 1) * HEAD_DIM]
            o = _rms(ws_qs[h][L:] + tail[h][:L], ng_ref[...]) * (zf * jax.nn.sigmoid(zf))
            o_ref[tok(c), h * HEAD_DIM:(h + 1) * HEAD_DIM] = o.astype(o_ref.dtype)
    for h in heads:
        s_scr[h] = s[h]

    @pl.when(step == pl.num_programs(1) - 1)
    def _():
        sfin_ref[0] = s_scr[...]


def _decay_lanes(p):
    return jnp.zeros((1, LANES), F32).at[0, GDN_HEADS:2 * GDN_HEADS].set(p.astype(F32))


def _gdn(main, tail, conv_prev8, s0, conv_w, a_log, dt_bias, norm_g, *, bsz, t):
    L = min(CHUNK, t)
    nch = GDN_CHUNKS_PER_STEP if (t // L) % GDN_CHUNKS_PER_STEP == 0 else 1
    nc = t // (L * nch)
    rows = L * nch
    conv_dim = 3 * GDN_WIDTH
    kern = functools.partial(_gdn_kernel, L=L, nch=nch)
    o, s_fin = pl.pallas_call(
        kern,
        out_shape=(jax.ShapeDtypeStruct((bsz * t, GDN_WIDTH), BF16),
                   jax.ShapeDtypeStruct((bsz, GDN_HEADS, HEAD_DIM, HEAD_DIM), F32)),
        grid=(bsz, nc),
        in_specs=[
            pl.BlockSpec((rows, conv_dim), lambda b, c: (b * nc + c, 0)),
            pl.BlockSpec((rows, GDN_WIDTH), lambda b, c: (b * nc + c, conv_dim // GDN_WIDTH)),
            pl.BlockSpec((rows, LANES), lambda b, c: (b * nc + c, COL_BA_TAIL // LANES)),
            pl.BlockSpec((1, SUBLANES, conv_dim), lambda b, c: (b, 0, 0)),
            pl.BlockSpec((1, GDN_HEADS, HEAD_DIM, HEAD_DIM), lambda b, c: (b, 0, 0, 0)),
            pl.BlockSpec((GDN_CONV, conv_dim), lambda b, c: (0, 0)),
            pl.BlockSpec((1, LANES), lambda b, c: (0, 0)),
            pl.BlockSpec((1, LANES), lambda b, c: (0, 0)),
            pl.BlockSpec((1, HEAD_DIM), lambda b, c: (0, 0)),
        ],
        out_specs=(pl.BlockSpec((rows, GDN_WIDTH), lambda b, c: (b * nc + c, 0)),
                   pl.BlockSpec((1, GDN_HEADS, HEAD_DIM, HEAD_DIM), lambda b, c: (b, 0, 0, 0))),
        scratch_shapes=[pltpu.VMEM((GDN_HEADS, HEAD_DIM, HEAD_DIM), F32),
                        pltpu.VMEM((SUBLANES + rows + SUBLANES, conv_dim), F32)],
        compiler_params=_cparams(("parallel", "arbitrary")),
        name="gdn",
    )(main, main, tail, conv_prev8, s0, conv_w, _decay_lanes(a_log), _decay_lanes(dt_bias),
      norm_g.reshape(1, -1))
    return o, s_fin


def _lambda_value(lam_refs, lam_init):
    lq1, lk1, lq2, lk2 = (r[...] for r in lam_refs)
    return (jnp.exp(jnp.sum(lq1 * lk1, axis=-1, keepdims=True))
            - jnp.exp(jnp.sum(lq2 * lk2, axis=-1, keepdims=True)) + lam_init)


def _softmax_update(m_scr, l_scr, acc_scr, scores, values):
    idx = range(len(scores))
    m_old = [m_scr[n] for n in idx]
    m_new = [jnp.maximum(m_old[n], jnp.max(scores[n], axis=-1, keepdims=True)) for n in idx]
    p = [jnp.exp(scores[n] - m_new[n]) for n in idx]
    alpha = [jnp.exp(m_old[n] - m_new[n]) for n in idx]
    pv = [_dot(p[n].astype(BF16), values[n]) for n in idx]
    for n in idx:
        l_scr[n] = alpha[n] * l_scr[n] + jnp.sum(p[n], axis=-1, keepdims=True)
        acc_scr[n] = alpha[n] * acc_scr[n] + pv[n]
        m_scr[n] = m_new[n]


def _diff_finalize(o1, o2, lam_refs, g_ref, lam_init):
    lam = _lambda_value(lam_refs, lam_init)
    return _rms(o1 - lam * o2, g_ref[...]) * (1.0 - lam_init)


def _diff_init(m_scr, l_scr, acc_scr):
    m_scr[...] = jnp.full(m_scr.shape, -jnp.inf, F32)
    l_scr[...] = jnp.zeros(l_scr.shape, F32)
    acc_scr[...] = jnp.zeros(acc_scr.shape, F32)


def _near_bias(slope, r, q_pos, k_pos):
    visible = jnp.right_shift(k_pos, CHUNK_SHIFT) <= jnp.right_shift(q_pos, CHUNK_SHIFT)
    return jnp.where(visible, slope * (r - jnp.abs(q_pos - k_pos)).astype(F32), -jnp.inf)


def _split_distance(slope, d):
    return (-(slope * CHUNK) * jnp.right_shift(d, CHUNK_SHIFT).astype(F32),
            -slope * jnp.bitwise_and(d, CHUNK - 1).astype(F32))


def _diff_prompt_kernel(i_tab, j_tab, slopes_ref, q_ref, k_ref, v_ref, lq1, lk1, lq2, lk2, g_ref, o_ref,
                        m_scr, l_scr, acc_scr, lhs_scr, s_even, s_odd, *, tq, lam_init):
    h = pl.program_id(1)
    s_id = pl.program_id(2)
    n_pairs = pl.num_programs(2) - 1
    score_on = s_id < n_pairs
    consume_on = s_id >= 1
    i = i_tab[s_id]
    j = j_tab[s_id]
    ci = i_tab[jnp.maximum(s_id - 1, 0)]
    cj = j_tab[jnp.maximum(s_id - 1, 0)]
    slope = slopes_ref[h]
    half = DIFF_HEAD_DIM
    lane = lax.broadcasted_iota(jnp.int32, (tq, LANES), 1)

    @pl.when(jnp.logical_and(consume_on, cj == 0))
    def _():
        _diff_init(m_scr, l_scr, acc_scr)

    @pl.when(jnp.logical_and(score_on, j == 0))
    def _():
        qs = q_ref[...] * (DIFF_HEAD_DIM ** -0.5)
        lhs_scr[0] = jnp.where(lane < half, qs, 1.0).astype(BF16)
        lhs_scr[1] = jnp.where(lane >= half, qs, 1.0).astype(BF16)

    sub = min(DIFF_SUB, tq)
    units = [(n, qb) for qb in range(tq // sub) for n in range(2)]

    def score_tile(near, dst):
        kb = k_ref[...]
        if near:
            aug0 = aug1 = jnp.zeros_like(kb)
        else:
            d = (i - j) * tq - lax.broadcasted_iota(jnp.int32, (tq, LANES), 0)
            t_hi, t_lo = _split_distance(slope, d)
            aug0 = jnp.where(lane == half, t_hi, jnp.where(lane == half + 1, t_lo, 0.0))
            aug1 = jnp.where(lane == 0, t_hi, jnp.where(lane == 1, t_lo, 0.0))
        keys = [jnp.where(lane < half, kb, aug0).astype(BF16), jnp.where(lane >= half, kb, aug1).astype(BF16)]
        bias = None
        for n, qb in units:
            cols = slice(qb * sub, (qb + 1) * sub)
            if near:
                live = (qb + 1) * sub
                if n == 0:
                    k_pos = lax.broadcasted_iota(jnp.int32, (live, sub), 0)
                    q_pos = qb * sub + lax.broadcasted_iota(jnp.int32, (live, sub), 1)
                    bias = _near_bias(slope, q_pos, q_pos, k_pos)
                dst[n, :live, cols] = _dot_nt(keys[n][:live], lhs_scr[n, cols, :]) + bias
                if live < tq:
                    dst[n, live:, cols] = jnp.full((tq - live, sub), -jnp.inf, F32)
            else:
                dst[n, :, cols] = _dot_nt(keys[n], lhs_scr[n, cols, :])

    def consume_tile(src):
        vt = v_ref[...].T.astype(BF16)
        idx = range(len(units))
        cols = [slice(qb * sub, (qb + 1) * sub) for _, qb in units]
        s = [src[n, :, cols[u]] for u, (n, _) in enumerate(units)]
        m_old = [m_scr[n, :, cols[u]] for u, (n, _) in enumerate(units)]
        m_new = [jnp.maximum(m_old[u], jnp.max(s[u], axis=0, keepdims=True)) for u in idx]
        p = [jnp.exp(s[u] - m_new[u]) for u in idx]
        alpha = [jnp.exp(m_old[u] - m_new[u]) for u in idx]
        pv = [_dot(vt, p[u].astype(BF16)) for u in idx]
        for u, (n, _) in enumerate(units):
            l_scr[n, :, cols[u]] = alpha[u] * l_scr[n, :, cols[u]] + jnp.sum(p[u], axis=0, keepdims=True)
            acc_scr[n, :, cols[u]] = alpha[u] * acc_scr[n, :, cols[u]] + pv[u]
            m_scr[n, :, cols[u]] = m_new[u]

    def stage(cond, near, score, consume):
        for parity, (dst, src) in enumerate(((s_even, s_odd), (s_odd, s_even))):
            @pl.when(jnp.logical_and(cond, s_id % 2 == parity))
            def _(dst=dst, src=src):
                if score:
                    score_tile(near, dst)
                if consume:
                    consume_tile(src)

    both = jnp.logical_and(score_on, consume_on)
    stage(jnp.logical_and(both, j < i), False, True, True)
    stage(jnp.logical_and(both, j == i), True, True, True)
    stage(jnp.logical_not(consume_on), True, True, False)
    stage(jnp.logical_not(score_on), False, False, True)

    @pl.when(jnp.logical_and(consume_on, cj == ci))
    def _():
        o_ref[...] = _diff_finalize((acc_scr[0] / l_scr[0]).T, (acc_scr[1] / l_scr[1]).T,
                                    (lq1, lk1, lq2, lk2), g_ref, lam_init).astype(o_ref.dtype)


def _alibi_slopes():
    return jnp.asarray([2.0 ** (-8.0 * (i + 1) / DIFF_HEADS) for i in range(DIFF_HEADS)], F32)


def _diff_prompt(main, k_new, v_new, lams, subln_g, *, bsz, t, lam_init):
    tq = min(DIFF_TQ, t)
    nq = t // tq
    assert tq % CHUNK == 0 and t <= CHUNK * 255
    kern = functools.partial(_diff_prompt_kernel, tq=tq, lam_init=lam_init)
    qc, kc, vc = COL_DQ_MAIN // LANES, 0, 0
    pairs = [(i, j) for i in range(nq) for j in range(i + 1)]
    pairs.append(pairs[-1])
    i_tab = jnp.asarray([p[0] for p in pairs], jnp.int32)
    j_tab = jnp.asarray([p[1] for p in pairs], jnp.int32)
    zero = lambda b, h, s, it, jt: (0, 0)

    def consumed(tab, s):
        return tab[jnp.maximum(s - 1, 0)]

    grid_spec = pltpu.PrefetchScalarGridSpec(
        num_scalar_prefetch=2,
        grid=(bsz, DIFF_HEADS, len(pairs)),
        in_specs=[
            pl.BlockSpec(memory_space=pltpu.SMEM),
            pl.BlockSpec((tq, LANES), lambda b, h, s, it, jt: (b * nq + it[s], qc + h)),
            pl.BlockSpec((tq, LANES), lambda b, h, s, it, jt: (b * nq + jt[s], kc + h)),
            pl.BlockSpec((tq, LANES), lambda b, h, s, it, jt: (b * nq + consumed(jt, s), vc + h)),
        ] + [pl.BlockSpec((1, DIFF_HEAD_DIM), zero)] * 4 + [pl.BlockSpec((1, HEAD_DIM), zero)],
        out_specs=pl.BlockSpec((tq, LANES), lambda b, h, s, it, jt: (b * nq + consumed(it, s), h)),
        scratch_shapes=[pltpu.VMEM((2, 1, tq), F32), pltpu.VMEM((2, 1, tq), F32),
                        pltpu.VMEM((2, HEAD_DIM, tq), F32), pltpu.VMEM((2, tq, LANES), BF16),
                        pltpu.VMEM((2, tq, tq), F32), pltpu.VMEM((2, tq, tq), F32)],
    )
    return pl.pallas_call(
        kern,
        out_shape=jax.ShapeDtypeStruct((bsz * t, DIFF_HEADS * HEAD_DIM), BF16),
        grid_spec=grid_spec,
        compiler_params=_cparams(("parallel", "parallel", "arbitrary")),
        name="diff_attn_prompt",
    )(i_tab, j_tab, _alibi_slopes(), main, k_new, v_new, *lams, subln_g.reshape(1, -1))


def _diff_sample_kernel(slopes_ref, q_ref, kn_ref, vn_ref, kp_ref, vp_ref, lq1, lk1, lq2, lk2, g_ref, o_ref,
                        m_scr, l_scr, acc_scr, lhs_scr, *, t, tk, past, lam_init):
    j = pl.program_id(1)
    heads = range(DIFF_HEADS)

    def cols(h):
        return slice(h * HEAD_DIM, (h + 1) * HEAD_DIM)

    @pl.when(j == 0)
    def _():
        _diff_init(m_scr, l_scr, acc_scr)
        lane = lax.broadcasted_iota(jnp.int32, (t, LANES), 1)
        r = lax.broadcasted_iota(jnp.int32, (t, t), 0)
        c = lax.broadcasted_iota(jnp.int32, (t, t), 1)
        scores = []
        for h in heads:
            qs = q_ref[:, cols(h)] * (DIFF_HEAD_DIM ** -0.5)
            lhs_scr[h, 0:t, :] = jnp.where(lane < DIFF_HEAD_DIM, qs, 0.0).astype(BF16)
            lhs_scr[h, t:2 * t, :] = jnp.where(lane >= DIFF_HEAD_DIM, qs, 0.0).astype(BF16)
            bias = _near_bias(slopes_ref[h], r, past + r, past + c)
            scores.append(_dot_nt(lhs_scr[h], kn_ref[:, cols(h)].astype(BF16)) + jnp.concatenate([bias, bias], axis=0))
        _softmax_update(m_scr, l_scr, acc_scr, scores, [vn_ref[:, cols(h)].astype(BF16) for h in heads])

    d = (past - j * tk - lax.broadcasted_iota(jnp.int32, (1, tk), 1)).astype(F32)
    kt = jnp.swapaxes(kp_ref[...], 0, 1).astype(BF16)
    vt = jnp.swapaxes(vp_ref[...], 0, 1).astype(BF16)
    scores = [_dot_nt(lhs_scr[h], kt[h]) - slopes_ref[h] * d for h in heads]
    _softmax_update(m_scr, l_scr, acc_scr, scores, [vt[h] for h in heads])

    @pl.when(j == pl.num_programs(1) - 1)
    def _():
        for h in heads:
            o = acc_scr[h] / l_scr[h]
            o_ref[:, cols(h)] = _diff_finalize(o[:t], o[t:], (lq1, lk1, lq2, lk2), g_ref, lam_init).astype(o_ref.dtype)


def _diff_sample(main, k_new, v_new, k_past, v_past, lams, subln_g, *, bsz, t, lam_init):
    past = k_past.shape[1]
    tk = min(512, past)
    nk = past // tk
    width = DIFF_HEADS * HEAD_DIM
    kern = functools.partial(_diff_sample_kernel, t=t, tk=tk, past=past, lam_init=lam_init)
    zero = lambda b, j: (0, 0)
    cache = pl.BlockSpec((None, tk, DIFF_HEADS, HEAD_DIM), lambda b, j: (b, j, 0, 0))
    return pl.pallas_call(
        kern,
        out_shape=jax.ShapeDtypeStruct((bsz * t, width), BF16),
        grid=(bsz, nk),
        in_specs=[
            pl.BlockSpec(memory_space=pltpu.SMEM),
            pl.BlockSpec((t, width), lambda b, j: (b, COL_DQ_MAIN // width)),
            pl.BlockSpec((t, width), lambda b, j: (b, 0)),
            pl.BlockSpec((t, width), lambda b, j: (b, 0)),
            cache, cache,
        ] + [pl.BlockSpec((1, DIFF_HEAD_DIM), zero)] * 4 + [pl.BlockSpec((1, HEAD_DIM), zero)],
        out_specs=pl.BlockSpec((t, width), lambda b, j: (b, 0)),
        scratch_shapes=[pltpu.VMEM((DIFF_HEADS, 2 * t, 1), F32), pltpu.VMEM((DIFF_HEADS, 2 * t, 1), F32),
                        pltpu.VMEM((DIFF_HEADS, 2 * t, HEAD_DIM), F32), pltpu.VMEM((DIFF_HEADS, 2 * t, LANES), BF16)],
        compiler_params=_cparams(("parallel", "arbitrary")),
        name="diff_attn_sample",
    )(_alibi_slopes(), main, k_new, v_new, k_past, v_past, *lams, subln_g.reshape(1, -1))


def _merge_kernel(oa_ref, ob_ref, wa_ref, wb_ref, ga_ref, gb_ref, o_ref):
    ya = _dot(oa_ref[...], wa_ref[...])
    yb = _dot(ob_ref[...], wb_ref[...])
    o_ref[...] = (jax.nn.sigmoid(ga_ref[...]) * ya + jax.nn.sigmoid(gb_ref[...]) * yb).astype(o_ref.dtype)


def _merge(o_a, o_b, wa, wb, tail, *, tm):
    m = o_a.shape[0]
    d = wa.shape[1]
    tn = 512
    ga0, gb0 = 0, COL_GB_TAIL // tn
    return pl.pallas_call(
        _merge_kernel,
        out_shape=jax.ShapeDtypeStruct((m, d), BF16),
        grid=(m // tm, d // tn),
        in_specs=[
            pl.BlockSpec((tm, o_a.shape[1]), lambda i, j: (i, 0)),
            pl.BlockSpec((tm, o_b.shape[1]), lambda i, j: (i, 0)),
            pl.BlockSpec((wa.shape[0], tn), lambda i, j: (0, j)),
            pl.BlockSpec((wb.shape[0], tn), lambda i, j: (0, j)),
            pl.BlockSpec((tm, tn), lambda i, j: (i, ga0 + j)),
            pl.BlockSpec((tm, tn), lambda i, j: (i, gb0 + j)),
        ],
        out_specs=pl.BlockSpec((tm, tn), lambda i, j: (i, j)),
        compiler_params=_cparams(("parallel", "arbitrary")),
        name="merge",
    )(o_a, o_b, wa, wb, tail, tail)


def _cross_kernel(q_ref, k_ref, v_ref, o_ref):
    dh = q_ref.shape[-1] // MEM_HEADS
    for h in range(MEM_HEADS):
        sl = slice(h * dh, (h + 1) * dh)
        s = _dot_nt(q_ref[:, sl], k_ref[0, :, sl].astype(BF16)) * (dh ** -0.5)
        s = s - jnp.max(s, axis=-1, keepdims=True)
        p = jnp.exp(s)
        p = p / jnp.sum(p, axis=-1, keepdims=True)
        o_ref[:, sl] = _dot(p.astype(BF16), v_ref[0, :, sl].astype(BF16)).astype(o_ref.dtype)


def _cross_attend(q, mem_k, mem_v, *, bsz, t, tm):
    d = q.shape[1]
    nt = t // tm
    mt = mem_k.shape[1]
    return pl.pallas_call(
        _cross_kernel,
        out_shape=jax.ShapeDtypeStruct(q.shape, BF16),
        grid=(bsz, nt),
        in_specs=[
            pl.BlockSpec((tm, d), lambda b, i: (b * nt + i, 0)),
            pl.BlockSpec((1, mt, d), lambda b, i: (b, 0, 0)),
            pl.BlockSpec((1, mt, d), lambda b, i: (b, 0, 0)),
        ],
        out_specs=pl.BlockSpec((tm, d), lambda b, i: (b * nt + i, 0)),
        compiler_params=_cparams(("parallel", "arbitrary")),
        name="cross_attn",
    )(q, mem_k, mem_v)


def _topk_rows(work, cidx, k):
    n, tn = work.shape
    rid = lax.broadcasted_iota(jnp.int32, (n, tn), 0).astype(F32)
    kid = lax.broadcasted_iota(jnp.int32, (k, tn), 0)
    vals = jnp.zeros((k, tn), F32)
    idxs = jnp.zeros((k, tn), F32)
    for t in range(k):
        m = jnp.max(work, axis=0, keepdims=True)
        pos = jnp.min(jnp.where(work == m, rid, float(n)), axis=0, keepdims=True)
        hit = rid == pos
        if cidx is None:
            picked = pos
        else:
            picked = jnp.sum(jnp.where(hit, cidx, 0.0), axis=0, keepdims=True)
        vals = jnp.where(kid == t, m, vals)
        idxs = jnp.where(kid == t, picked, idxs)
        work = jnp.where(hit, -jnp.inf, work)
    return vals, idxs


def _product_candidates(v1, i1, v2, i2, k):
    assert k == 2 * SUBLANES
    sub = lax.broadcasted_iota(jnp.int32, (SUBLANES, v1.shape[1]), 0)
    vals, idxs = [], []
    for a in range(k // 2):
        nb = k // (a + 1)
        width = k if nb > SUBLANES else SUBLANES
        cv = v1[a:a + 1, :] + v2[:width, :]
        ci = i1[a:a + 1, :] * PEER_N_KEYS + i2[:width, :]
        if nb < SUBLANES:
            cv = jnp.where(sub < nb, cv, -jnp.inf)
        vals.append(cv)
        idxs.append(ci)
    vals.append(v1[k // 2:, :] + v2[0:1, :])
    idxs.append(i1[k // 2:, :] * PEER_N_KEYS + i2[0:1, :])
    return jnp.concatenate(vals, axis=0), jnp.concatenate(idxs, axis=0)


def _peer_route_kernel(q_ref, keys_hi_ref, keys_lo_ref, e_ref, g_ref):
    k = PEER_TOPK
    for h in range(PEER_HEADS):
        sub = []
        for p in range(2):
            c0 = (h * 2 + p) * PEER_N_KEYS
            q_parts = _split_bf16(q_ref[:, c0:c0 + PEER_N_KEYS])
            s = _dot3_parts((keys_hi_ref[h, p], keys_lo_ref[h, p]), q_parts, _dot_nt)
            sub.append(_topk_rows(s, None, k))
        (v1, i1), (v2, i2) = sub
        cand, cidx = _product_candidates(v1, i1, v2, i2, k)
        best, eidx = _topk_rows(cand, cidx, k)
        ex = jnp.exp(best - jnp.max(best, axis=0, keepdims=True))
        gate = ex / jnp.sum(ex, axis=0, keepdims=True)
        e_ref[h * k:(h + 1) * k, :] = eidx.astype(jnp.int32)
        g_ref[h * k:(h + 1) * k, :] = gate


HALF_KEYS = PEER_N_KEYS // 2
BF16_HIGH_MASK = 0xFFFF0000


def _peer_gate_kernel(et_ref, gt_ref, o_ref, e_scr, g_scr):
    tg = et_ref.shape[1]
    e_scr[...] = et_ref[...].T
    g_scr[...] = gt_ref[...].T
    kid = lax.broadcasted_iota(jnp.int32, (PEER_N_KEYS, PEER_N_KEYS), 0)

    def body(n8, carry):
        rows = pl.ds(pl.multiple_of(n8 * SUBLANES, SUBLANES), SUBLANES)
        e8 = e_scr[rows, :]
        g8 = g_scr[rows, :]
        a8 = jnp.right_shift(e8, 7)
        b8 = jnp.bitwise_and(e8, PEER_N_KEYS - 1)
        toks = range(SUBLANES)
        at = [jnp.where(a8[s:s + 1, :] == kid, g8[s:s + 1, :], 0.0).astype(BF16) for s in toks]
        bt = [jnp.where(b8[s:s + 1, :] == kid, 1.0, 0.0).astype(BF16) for s in toks]
        grids = [_dot_nt(at[s], bt[s]) for s in toks]
        bits = [pltpu.bitcast(grids[s].astype(BF16).astype(F32), jnp.uint32) for s in toks]
        words = [jnp.bitwise_or(jnp.bitwise_and(bits[s][HALF_KEYS:, :], jnp.uint32(BF16_HIGH_MASK)),
                                jnp.right_shift(bits[s][:HALF_KEYS, :], jnp.uint32(16))) for s in toks]
        o_ref[:, rows, :] = jnp.swapaxes(jnp.stack(words, axis=0), 0, 1)
        return carry

    lax.fori_loop(0, tg // SUBLANES, body, 0, unroll=4)


def _peer_route(qp, sub_keys, *, tn):
    n = qp.shape[0]
    rows = PEER_HEADS * PEER_TOPK
    keys_hi, keys_lo = _split_bf16(sub_keys.astype(F32))
    keys_spec = pl.BlockSpec(sub_keys.shape, lambda i: (0, 0, 0, 0))
    e_t, g_t = pl.pallas_call(
        _peer_route_kernel,
        out_shape=(jax.ShapeDtypeStruct((rows, n), jnp.int32), jax.ShapeDtypeStruct((rows, n), F32)),
        grid=(n // tn,),
        in_specs=[pl.BlockSpec((tn, qp.shape[1]), lambda i: (i, 0)), keys_spec, keys_spec],
        out_specs=(pl.BlockSpec((rows, tn), lambda i: (0, i)), pl.BlockSpec((rows, tn), lambda i: (0, i))),
        compiler_params=_cparams(("parallel",)),
        name="peer_route",
    )(qp, keys_hi, keys_lo)
    return pl.pallas_call(
        _peer_gate_kernel,
        out_shape=jax.ShapeDtypeStruct((HALF_KEYS, n, PEER_N_KEYS), jnp.uint32),
        grid=(n // tn,),
        in_specs=[pl.BlockSpec((rows, tn), lambda i: (0, i)), pl.BlockSpec((rows, tn), lambda i: (0, i))],
        out_specs=pl.BlockSpec((HALF_KEYS, tn, PEER_N_KEYS), lambda i: (0, i, 0)),
        scratch_shapes=[pltpu.VMEM((tn, rows), jnp.int32), pltpu.VMEM((tn, rows), F32)],
        compiler_params=_cparams(("parallel",)),
        name="peer_gates",
    )(e_t, g_t)


def _gelu_tanh(x):
    return 0.5 * x * (1.0 + jnp.tanh(math.sqrt(2.0 / math.pi) * (x + 0.044715 * (x * x * x))))


PEER_BLOCKS_PER_STEP = 4
PEER_TN = 1024


def _peer_expert_kernel(x_ref, gn_ref, gpa_ref, gpb_ref, *rest):
    nblk = PEER_BLOCKS_PER_STEP
    u_refs, v_refs = rest[:nblk], rest[nblk:2 * nblk]
    gf_ref, o_ref, h_scr, acc_scr, coef_scr = rest[2 * nblk:]
    s = pl.program_id(1)
    last = pl.num_programs(1) - 1

    @pl.when(s == 0)
    def _():
        h_scr[...] = _rms(x_ref[...], gn_ref[...]).astype(BF16)
        acc_scr[...] = jnp.zeros(acc_scr.shape, F32)
        coef_scr[...] = jnp.zeros(coef_scr.shape, BF16)

    v_cat = jnp.concatenate([v[...] for v in v_refs], axis=0)
    acc_scr[...] += _dot(coef_scr[(s + 1) % 2], v_cat)

    def unpack(word):
        return (pltpu.bitcast(jnp.left_shift(word, jnp.uint32(16)), F32),
                pltpu.bitcast(jnp.bitwise_and(word, jnp.uint32(BF16_HIGH_MASK)), F32))

    a_lo, a_hi = unpack(gpa_ref[...])
    b_lo, b_hi = unpack(gpb_ref[...])
    u_cat = jnp.concatenate([u[...] for u in u_refs], axis=0)
    hid = _gelu_tanh(_dot_nt(h_scr[...], u_cat))
    coef_scr[s % 2] = (jnp.concatenate([a_lo, b_lo, a_hi, b_hi], axis=1) * hid).astype(BF16)

    @pl.when(s == last)
    def _():
        o_ref[...] = _rms(x_ref[...] + acc_scr[...], gf_ref[...])


def _peer_experts(x, norm_g, gates, u_tab, v_tab, final_g, *, tn):
    n, d = x.shape
    eb = PEER_N_KEYS
    assert u_tab.shape[0] == eb * eb
    nblk = PEER_BLOCKS_PER_STEP
    ns = eb // nblk

    def cur(s):
        return jnp.minimum(s, ns - 1)

    def prev(s):
        return jnp.maximum(s - 1, 0)

    def table(which, q):
        return pl.BlockSpec((eb, d), lambda i, s: (which(s) + q * ns, 0))

    return pl.pallas_call(
        _peer_expert_kernel,
        out_shape=jax.ShapeDtypeStruct((n, d), F32),
        grid=(n // tn, ns + 1),
        in_specs=[
            pl.BlockSpec((tn, d), lambda i, s: (i, 0), pipeline_mode=pl.Buffered(1)),
            pl.BlockSpec((1, d), lambda i, s: (0, 0)),
            pl.BlockSpec((None, tn, eb), lambda i, s: (cur(s), i, 0)),
            pl.BlockSpec((None, tn, eb), lambda i, s: (cur(s) + ns, i, 0)),
        ] + [table(cur, q) for q in range(nblk)] + [table(prev, q) for q in range(nblk)] + [
            pl.BlockSpec((1, d), lambda i, s: (0, 0)),
        ],
        out_specs=pl.BlockSpec((tn, d), lambda i, s: (i, 0), pipeline_mode=pl.Buffered(1)),
        scratch_shapes=[pltpu.VMEM((tn, d), BF16), pltpu.VMEM((tn, d), F32),
                        pltpu.VMEM((2, tn, nblk * eb), BF16)],
        compiler_params=_cparams(("parallel", "arbitrary")),
        name="peer_experts",
    )(x, norm_g.reshape(1, d), gates, gates, *([u_tab] * nblk), *([v_tab] * nblk), final_g.reshape(1, d))


def _layer(x, wts, layer, mem_k, mem_v, conv_prev, gdn_state, k_past, v_past, *, tm):
    bsz, t, d = x.shape
    n = bsz * t
    x2d = x.reshape(n, d)
    main, k_new, v_new, tail = _in_proj(x2d, wts["norm_mix_g"], wts["w_in"], tm=min(IN_TM, n))
    conv_dim = 3 * GDN_WIDTH
    conv_prev8 = jnp.concatenate(
        [jnp.zeros((bsz, SUBLANES - (GDN_CONV - 1), conv_dim), F32), conv_prev.astype(F32)], axis=1)
    o_a, s_new = _gdn(main, tail, conv_prev8, gdn_state.astype(F32), wts["gdn_conv_w"], wts["gdn_a_log"],
                      wts["gdn_dt_bias"], wts["gdn_norm_g"], bsz=bsz, t=t)
    conv_new = main.reshape(bsz, t, -1)[:, t - (GDN_CONV - 1):, :conv_dim]
    lam_init = 0.8 - 0.6 * math.exp(-0.3 * layer)
    lams = tuple(wts[k].reshape(1, -1) for k in ("diff_lambda_q1", "diff_lambda_k1", "diff_lambda_q2", "diff_lambda_k2"))
    if k_past is None:
        o_b = _diff_prompt(main, k_new, v_new, lams, wts["diff_subln_g"], bsz=bsz, t=t, lam_init=lam_init)
    else:
        o_b = _diff_sample(main, k_new, v_new, k_past, v_past, lams, wts["diff_subln_g"], bsz=bsz, t=t,
                           lam_init=lam_init)
    k_rows = k_new.reshape(bsz, t, DIFF_HEADS, HEAD_DIM)
    v_rows = v_new.reshape(bsz, t, DIFF_HEADS, HEAD_DIM)
    merged = _merge(o_a, o_b, wts["w_branch_a"], wts["w_branch_b"], tail, tm=tm)
    x1 = _matmul(merged, wts["w_out"], residual=x2d, tm=tm, tn=MM_TN, name="out_proj")
    qm = _matmul(x1, wts["w_mq"], norm_g=wts["norm_cross_g"], tm=tm, tn=MM_TN, out_dtype=BF16, name="mem_q")
    mt = mem_k.shape[1]
    oc = _cross_attend(qm, mem_k.reshape(bsz, mt, d), mem_v.reshape(bsz, mt, d), bsz=bsz, t=t, tm=min(tm, t))
    x2 = _matmul(oc, wts["w_mo"], residual=x1, tm=tm, tn=MM_TN, name="mem_o")
    qp = _matmul(x2, wts["peer_w_q"], norm_g=wts["norm_ffn_g"], tm=tm, tn=MM_TN, name="peer_q")
    tn = min(PEER_TN, n)
    gates = _peer_route(qp, wts["peer_sub_keys"], tn=LANES)
    y = _peer_experts(x2, wts["norm_ffn_g"], gates, wts["peer_u"], wts["peer_v"], wts["final_norm_g"], tn=tn)
    return y.reshape(bsz, t, d), k_rows, v_rows, s_new, conv_new


def kernel(x_prompt, x_sample, cache_diff_k, cache_diff_v, state_gdn, state_conv, cache_mem_k, cache_mem_v,
           mem_prompt, norm_mix_g, w_in, gdn_conv_w, gdn_a_log, gdn_dt_bias, gdn_norm_g,
           diff_lambda_q1, diff_lambda_k1, diff_lambda_q2, diff_lambda_k2, diff_subln_g,
           w_branch_a, w_branch_b, w_out, norm_cross_g, norm_mem_g, w_mq, w_mk, w_mv, w_mo,
           norm_ffn_g, peer_w_q, peer_sub_keys, peer_u, peer_v, final_norm_g):
    depth = w_in.shape[0]
    assert depth == 1, "final norm is fused into the last layer's PEER kernel"
    l = 0
    wts = {
        "norm_mix_g": norm_mix_g[l], "w_in": _in_proj_weights(w_in[l]), "gdn_conv_w": gdn_conv_w[l],
        "gdn_a_log": gdn_a_log[l], "gdn_dt_bias": gdn_dt_bias[l], "gdn_norm_g": gdn_norm_g[l],
        "diff_lambda_q1": diff_lambda_q1[l], "diff_lambda_k1": diff_lambda_k1[l],
        "diff_lambda_q2": diff_lambda_q2[l], "diff_lambda_k2": diff_lambda_k2[l],
        "diff_subln_g": diff_subln_g[l], "w_branch_a": w_branch_a[l].astype(BF16),
        "w_branch_b": w_branch_b[l].astype(BF16), "w_out": w_out[l].astype(BF16),
        "norm_cross_g": norm_cross_g[l], "w_mq": w_mq[l].astype(BF16), "w_mo": w_mo[l].astype(BF16),
        "norm_ffn_g": norm_ffn_g[l], "peer_w_q": peer_w_q[l].astype(BF16), "peer_sub_keys": peer_sub_keys[l],
        "peer_u": peer_u[l].astype(BF16), "peer_v": peer_v[l].astype(BF16), "final_norm_g": final_norm_g,
    }
    bp, tp, d = x_prompt.shape
    bs, ts, _ = x_sample.shape
    mem2d = mem_prompt.reshape(-1, d)
    tmm = min(512, mem2d.shape[0])
    mem_k = _matmul(mem2d, w_mk[l].astype(BF16), norm_g=norm_mem_g[l], tm=tmm, tn=512, name="mem_k")
    mem_v = _matmul(mem2d, w_mv[l].astype(BF16), norm_g=norm_mem_g[l], tm=tmm, tn=512, name="mem_v")
    mshape = mem_prompt.shape[:2] + (MEM_HEADS, d // MEM_HEADS)
    mem_k = mem_k.reshape(mshape)
    mem_v = mem_v.reshape(mshape)
    conv0 = jnp.zeros((bp, GDN_CONV - 1, 3 * GDN_WIDTH), F32)
    s0 = jnp.zeros((bp, GDN_HEADS, HEAD_DIM, HEAD_DIM), F32)
    yp, pk, pv, ps, pc = _layer(x_prompt, wts, l, mem_k, mem_v, conv0, s0, None, None, tm=min(1024, bp * tp))
    ys, sk, sv, ss, sc = _layer(x_sample, wts, l, cache_mem_k[l], cache_mem_v[l], state_conv[l], state_gdn[l],
                                cache_diff_k[l], cache_diff_v[l], tm=min(256, bs * ts))
    return (yp, ys, pk[None], pv[None], ps[None], pc[None], mem_k[None], mem_v[None],
            sk[None], sv[None], ss[None], sc[None])
```

```python
import functools
import math

import jax
import jax.numpy as jnp
from jax import lax
from jax.experimental import pallas as pl
from jax.experimental.pallas import tpu as pltpu

F32 = jnp.float32
BF16 = jnp.bfloat16
HIGHEST = lax.Precision.HIGHEST

RMS_EPS = 1e-6
CHUNK = 64
CHUNK_SHIFT = 6
DIFF_TQ = 1024
GDN_CHUNKS_PER_STEP = 2
DIFF_SUB = 256
GDN_HEADS = 8
HEAD_DIM = 128
GDN_WIDTH = GDN_HEADS * HEAD_DIM
GDN_CONV = 4
DIFF_HEADS = 8
DIFF_HEAD_DIM = 64
MEM_HEADS = 4
PEER_HEADS = 8
PEER_N_KEYS = 128
PEER_TOPK = 16
LANES = 128
SUBLANES = 8
V7X_VMEM_LIMIT = 52 * 1024 * 1024
MM_TN = 1024


def _cparams(sem, vmem=V7X_VMEM_LIMIT):
    return pltpu.CompilerParams(dimension_semantics=sem, vmem_limit_bytes=vmem)


def _dot(a, b, precision=None):
    return jnp.dot(a, b, preferred_element_type=F32, precision=precision)


def _dot_nt(a, b, precision=None):
    return lax.dot_general(a, b, (((1,), (1,)), ((), ())), preferred_element_type=F32, precision=precision)


def _split_bf16(x):
    hi = x.astype(BF16)
    return hi, (x - hi.astype(F32)).astype(BF16)


def _cat_parts(p, q):
    return jnp.concatenate([p[0], q[0]], axis=0), jnp.concatenate([p[1], q[1]], axis=0)


def _dot3_parts(a_parts, b_parts, dot):
    a_hi, a_lo = a_parts
    b_hi, b_lo = b_parts
    m = a_hi.shape[0]
    both = dot(jnp.concatenate([a_hi, a_lo], axis=0), b_hi)
    return both[:m] + both[m:] + dot(a_hi, b_lo)


def _dot3(a, b):
    return _dot3_parts(_split_bf16(a), _split_bf16(b), _dot)


def _rms(xf, g):
    return xf * lax.rsqrt(jnp.mean(xf * xf, axis=-1, keepdims=True) + RMS_EPS) * g


def _mm_kernel(*refs, has_norm, has_res):
    it = iter(refs)
    x_ref = next(it)
    g_ref = next(it) if has_norm else None
    w_ref = next(it)
    r_ref = next(it) if has_res else None
    o_ref = next(it)
    h_scr = next(it)

    @pl.when(pl.program_id(1) == 0)
    def _():
        xf = x_ref[...].astype(F32)
        if has_norm:
            xf = _rms(xf, g_ref[...])
        h_scr[...] = xf.astype(BF16)

    acc = _dot(h_scr[...], w_ref[...])
    if has_res:
        acc = acc + r_ref[...]
    o_ref[...] = acc.astype(o_ref.dtype)


def _matmul(x, w, *, norm_g=None, residual=None, tm, tn, out_dtype=F32, name="matmul"):
    m, k = x.shape
    n = w.shape[1]
    assert m % tm == 0 and n % tn == 0, (m, n, tm, tn)
    in_specs = [pl.BlockSpec((tm, k), lambda i, j: (i, 0))]
    args = [x]
    if norm_g is not None:
        in_specs.append(pl.BlockSpec((1, k), lambda i, j: (0, 0)))
        args.append(norm_g.reshape(1, k).astype(F32))
    in_specs.append(pl.BlockSpec((k, tn), lambda i, j: (0, j)))
    args.append(w)
    if residual is not None:
        in_specs.append(pl.BlockSpec((tm, tn), lambda i, j: (i, j)))
        args.append(residual)
    return pl.pallas_call(
        functools.partial(_mm_kernel, has_norm=norm_g is not None, has_res=residual is not None),
        out_shape=jax.ShapeDtypeStruct((m, n), out_dtype),
        grid=(m // tm, n // tn),
        in_specs=in_specs,
        out_specs=pl.BlockSpec((tm, tn), lambda i, j: (i, j)),
        scratch_shapes=[pltpu.VMEM((tm, k), BF16)],
        compiler_params=_cparams(("parallel", "arbitrary")),
        name=name,
    )(*args)


IN_TN = 512
IN_TM = 1024
IN_W1_TILES = 8
IN_W2_TILES = 14
IN_MAIN_TILES = 10
IN_K_TILE0 = 10
IN_V_TILE0 = 12
IN_TAIL_TILE0 = 14
IN_TILES = IN_W1_TILES + IN_W2_TILES + 1
MAIN_COLS = IN_MAIN_TILES * IN_TN
TAIL_COLS = (IN_TILES - IN_TAIL_TILE0) * IN_TN
COL_DQ_MAIN = 4096
COL_GB_TAIL = 2048
COL_BA_TAIL = 4096


def _in_proj_kernel(x_ref, g_ref, w1_ref, w2_ref, w3_ref, main_ref, k_ref, v_ref, tail_ref, h_scr):
    j = pl.program_id(1)

    @pl.when(j == 0)
    def _():
        h_scr[...] = _rms(x_ref[...], g_ref[...]).astype(BF16)

    routes = ((0, IN_W1_TILES, w1_ref, main_ref),
              (IN_W1_TILES, IN_K_TILE0, w2_ref, main_ref),
              (IN_K_TILE0, IN_V_TILE0, w2_ref, k_ref),
              (IN_V_TILE0, IN_TAIL_TILE0, w2_ref, v_ref),
              (IN_TAIL_TILE0, IN_TILES - 1, w2_ref, tail_ref),
              (IN_TILES - 1, IN_TILES, w3_ref, tail_ref))
    for lo, hi, w_ref, dst in routes:
        @pl.when(jnp.logical_and(j >= lo, j < hi))
        def _(w_ref=w_ref, dst=dst):
            dst[...] = _dot(h_scr[...], w_ref[...])


def _in_proj_weights(w_in):
    c_z = 4 * GDN_WIDTH
    c_ba = c_z + 2 * GDN_HEADS
    assert c_z == IN_W1_TILES * IN_TN and w_in.shape[1] - c_ba == IN_W2_TILES * IN_TN
    w3 = jnp.pad(w_in[:, c_z:c_ba], ((0, 0), (0, IN_TN - 2 * GDN_HEADS)))
    return w_in[:, :c_z].astype(BF16), w_in[:, c_ba:].astype(BF16), w3.astype(BF16)


def _in_proj(x, norm_g, w_pieces, *, tm):
    m, d = x.shape
    w1, w2, w3 = w_pieces
    tn = IN_TN
    out = lambda cols: jax.ShapeDtypeStruct((m, cols), F32)
    return pl.pallas_call(
        _in_proj_kernel,
        out_shape=(out(MAIN_COLS), out(2 * IN_TN), out(2 * IN_TN), out(TAIL_COLS)),
        grid=(m // tm, IN_TILES),
        in_specs=[
            pl.BlockSpec((tm, d), lambda i, j: (i, 0)),
            pl.BlockSpec((1, d), lambda i, j: (0, 0)),
            pl.BlockSpec((d, tn), lambda i, j: (0, jnp.minimum(j, IN_W1_TILES - 1))),
            pl.BlockSpec((d, tn), lambda i, j: (0, jnp.clip(j - IN_W1_TILES, 0, IN_W2_TILES - 1))),
            pl.BlockSpec((d, tn), lambda i, j: (0, 0)),
        ],
        out_specs=(
            pl.BlockSpec((tm, tn), lambda i, j: (i, jnp.minimum(j, IN_MAIN_TILES - 1))),
            pl.BlockSpec((tm, tn), lambda i, j: (i, jnp.clip(j - IN_K_TILE0, 0, 1))),
            pl.BlockSpec((tm, tn), lambda i, j: (i, jnp.clip(j - IN_V_TILE0, 0, 1))),
            pl.BlockSpec((tm, tn), lambda i, j: (i, jnp.clip(j - IN_TAIL_TILE0, 0, IN_TILES - IN_TAIL_TILE0 - 1))),
        ),
        scratch_shapes=[pltpu.VMEM((tm, d), BF16)],
        compiler_params=_cparams(("parallel", "arbitrary")),
        name="in_proj",
    )(x, norm_g.reshape(1, d).astype(F32), w1, w2, w3)


def _gdn_kernel(qkv_ref, z_ref, ba_ref, prev_ref, s0_ref, convw_ref, alog_ref, dtb_ref, ng_ref,
                o_ref, sfin_ref, s_scr, xbuf, *, L, nch):
    step = pl.program_id(1)
    rows = nch * L

    @pl.when(step == 0)
    def _():
        s_scr[...] = s0_ref[0]
        xbuf[0:SUBLANES, :] = prev_ref[0]

    xbuf[SUBLANES:SUBLANES + rows, :] = qkv_ref[...]
    base = SUBLANES - (GDN_CONV - 1)
    y = xbuf[base:base + rows, :] * convw_ref[0:1, :]
    for i in range(1, GDN_CONV):
        y = y + xbuf[base + i:base + i + rows, :] * convw_ref[i:i + 1, :]
    y = y * jax.nn.sigmoid(y)
    carry = xbuf[rows:rows + SUBLANES, :]
    xbuf[0:SUBLANES, :] = carry

    row = lax.broadcasted_iota(jnp.int32, (L, L), 0)
    col = lax.broadcasted_iota(jnp.int32, (L, L), 1)
    lower = row >= col
    strict = row > col
    eye = (row == col).astype(F32)
    log_l = int(math.log2(L))
    n_double = log_l - 1
    assert 2 ** log_l == L

    ba = ba_ref[...]
    beta_all = jax.nn.sigmoid(ba)
    a_in = ba + dtb_ref[...]
    softplus = jnp.maximum(a_in, 0.0) + jnp.log(1.0 + jnp.exp(-jnp.abs(a_in)))
    g_all = -jnp.exp(alog_ref[...]) * softplus
    rr = lax.broadcasted_iota(jnp.int32, (rows, rows), 0)
    cc = lax.broadcasted_iota(jnp.int32, (rows, rows), 1)
    same_chunk_lower = jnp.logical_and(rr >= cc, jnp.right_shift(rr, log_l) == jnp.right_shift(cc, log_l))
    g_cum = _dot(same_chunk_lower.astype(F32), g_all, HIGHEST)
    g_cum_t = g_cum.T

    heads = range(GDN_HEADS)
    chains = [(c, h) for c in range(nch) for h in heads]
    ids = range(len(chains))

    def tok(c):
        return slice(c * L, (c + 1) * L)

    def head_cols(part, c, h):
        return y[tok(c), part * GDN_WIDTH + h * HEAD_DIM:part * GDN_WIDTH + (h + 1) * HEAD_DIM]

    q = [head_cols(0, c, h) for c, h in chains]
    k = [head_cols(1, c, h) for c, h in chains]
    v = [head_cols(2, c, h) for c, h in chains]
    q = [a * lax.rsqrt(jnp.sum(a * a, axis=-1, keepdims=True) + 1e-6) * (HEAD_DIM ** -0.5) for a in q]
    k = [a * lax.rsqrt(jnp.sum(a * a, axis=-1, keepdims=True) + 1e-6) for a in k]
    beta = [beta_all[tok(c), h:h + 1] for c, h in chains]
    g_col = [g_cum[tok(c), GDN_HEADS + h:GDN_HEADS + h + 1] for c, h in chains]
    g_row = [g_cum_t[GDN_HEADS + h:GDN_HEADS + h + 1, tok(c)] for c, h in chains]
    decay = [jnp.where(lower, jnp.exp(jnp.where(lower, g_col[n] - g_row[n], 0.0)), 0.0) for n in ids]
    g_last = [g_col[n][L - 1:L, :] for n in ids]
    e_col = [jnp.exp(g_col[n]) for n in ids]
    qk_kk = [_dot_nt(jnp.concatenate([q[n], k[n]], axis=0).astype(BF16), k[n].astype(BF16)) for n in ids]
    x = [-jnp.where(strict, beta[n] * qk_kk[n][L:] * decay[n], 0.0) for n in ids]
    t_inv = [eye + x[n] for n in ids]
    x_parts = [_split_bf16(x[n]) for n in ids]
    xp = [_dot3_parts(x_parts[n], x_parts[n], _dot) for n in ids]
    for _ in range(n_double - 1):
        xp_parts = [_split_bf16(xp[n]) for n in ids]
        both = [_dot3_parts(_cat_parts(_split_bf16(t_inv[n]), xp_parts[n]), xp_parts[n], _dot) for n in ids]
        t_inv = [t_inv[n] + both[n][:L] for n in ids]
        xp = [both[n][L:] for n in ids]
    t_inv = [t_inv[n] + _dot3(t_inv[n], xp[n]) for n in ids]
    sol = [_dot3(t_inv[n], jnp.concatenate([v[n] * beta[n], k[n] * (beta[n] * e_col[n])], axis=-1)) for n in ids]
    qk = [jnp.where(lower, qk_kk[n][:L] * decay[n], 0.0) for n in ids]
    k_dec_t = [(k[n] * jnp.exp(g_last[n] - g_col[n])).T for n in ids]
    left_s = [jnp.concatenate([sol[n][:, HEAD_DIM:], q[n] * e_col[n]], axis=0).astype(BF16) for n in ids]
    left_v = [jnp.concatenate([qk[n], k_dec_t[n]], axis=0).astype(BF16) for n in ids]

    s = [s_scr[h] for h in heads]
    for c in range(nch):
        n0 = c * GDN_HEADS
        ws_qs = [_dot(left_s[n0 + h], s[h].astype(BF16)) for h in heads]
        v_new = [sol[n0 + h][:, :HEAD_DIM] - ws_qs[h][:L] for h in heads]
        tail = [_dot(left_v[n0 + h], v_new[h].astype(BF16)) for h in heads]
        s = [s[h] * jnp.exp(g_last[n0 + h]) + tail[h][L:] for h in heads]
        for h in heads:
            zf = z_ref[tok(c), h * HEAD_DIM:(h + 1) * HEAD_DIM]
            o = _rms(ws_qs[h][L:] + tail[h][:L], ng_ref[...]) * (zf * jax.nn.sigmoid(zf))
            o_ref[tok(c), h * HEAD_DIM:(h + 1) * HEAD_DIM] = o.astype(o_ref.dtype)
    for h in heads:
        s_scr[h] = s[h]

    @pl.when(step == pl.num_programs(1) - 1)
    def _():
        sfin_ref[0] = s_scr[...]


def _decay_lanes(p):
    return jnp.zeros((1, LANES), F32).at[0, GDN_HEADS:2 * GDN_HEADS].set(p.astype(F32))


def _gdn(main, tail, conv_prev8, s0, conv_w, a_log, dt_bias, norm_g, *, bsz, t):
    L = min(CHUNK, t)
    nch = GDN_CHUNKS_PER_STEP if (t // L) % GDN_CHUNKS_PER_STEP == 0 else 1
    nc = t // (L * nch)
    rows = L * nch
    conv_dim = 3 * GDN_WIDTH
    kern = functools.partial(_gdn_kernel, L=L, nch=nch)
    o, s_fin = pl.pallas_call(
        kern,
        out_shape=(jax.ShapeDtypeStruct((bsz * t, GDN_WIDTH), BF16),
                   jax.ShapeDtypeStruct((bsz, GDN_HEADS, HEAD_DIM, HEAD_DIM), F32)),
        grid=(bsz, nc),
        in_specs=[
            pl.BlockSpec((rows, conv_dim), lambda b, c: (b * nc + c, 0)),
            pl.BlockSpec((rows, GDN_WIDTH), lambda b, c: (b * nc + c, conv_dim // GDN_WIDTH)),
            pl.BlockSpec((rows, LANES), lambda b, c: (b * nc + c, COL_BA_TAIL // LANES)),
            pl.BlockSpec((1, SUBLANES, conv_dim), lambda b, c: (b, 0, 0)),
            pl.BlockSpec((1, GDN_HEADS, HEAD_DIM, HEAD_DIM), lambda b, c: (b, 0, 0, 0)),
            pl.BlockSpec((GDN_CONV, conv_dim), lambda b, c: (0, 0)),
            pl.BlockSpec((1, LANES), lambda b, c: (0, 0)),
            pl.BlockSpec((1, LANES), lambda b, c: (0, 0)),
            pl.BlockSpec((1, HEAD_DIM), lambda b, c: (0, 0)),
        ],
        out_specs=(pl.BlockSpec((rows, GDN_WIDTH), lambda b, c: (b * nc + c, 0)),
                   pl.BlockSpec((1, GDN_HEADS, HEAD_DIM, HEAD_DIM), lambda b, c: (b, 0, 0, 0))),
        scratch_shapes=[pltpu.VMEM((GDN_HEADS, HEAD_DIM, HEAD_DIM), F32),
                        pltpu.VMEM((SUBLANES + rows + SUBLANES, conv_dim), F32)],
        compiler_params=_cparams(("parallel", "arbitrary")),
        name="gdn",
    )(main, main, tail, conv_prev8, s0, conv_w, _decay_lanes(a_log), _decay_lanes(dt_bias),
      norm_g.reshape(1, -1))
    return o, s_fin


def _lambda_value(lam_refs, lam_init):
    lq1, lk1, lq2, lk2 = (r[...] for r in lam_refs)
    return (jnp.exp(jnp.sum(lq1 * lk1, axis=-1, keepdims=True))
            - jnp.exp(jnp.sum(lq2 * lk2, axis=-1, keepdims=True)) + lam_init)


def _softmax_update(m_scr, l_scr, acc_scr, scores, values):
    idx = range(len(scores))
    m_old = [m_scr[n] for n in idx]
    m_new = [jnp.maximum(m_old[n], jnp.max(scores[n], axis=-1, keepdims=True)) for n in idx]
    p = [jnp.exp(scores[n] - m_new[n]) for n in idx]
    alpha = [jnp.exp(m_old[n] - m_new[n]) for n in idx]
    pv = [_dot(p[n].astype(BF16), values[n]) for n in idx]
    for n in idx:
        l_scr[n] = alpha[n] * l_scr[n] + jnp.sum(p[n], axis=-1, keepdims=True)
        acc_scr[n] = alpha[n] * acc_scr[n] + pv[n]
        m_scr[n] = m_new[n]


def _diff_finalize(o1, o2, lam_refs, g_ref, lam_init):
    lam = _lambda_value(lam_refs, lam_init)
    return _rms(o1 - lam * o2, g_ref[...]) * (1.0 - lam_init)


def _diff_init(m_scr, l_scr, acc_scr):
    m_scr[...] = jnp.full(m_scr.shape, -jnp.inf, F32)
    l_scr[...] = jnp.zeros(l_scr.shape, F32)
    acc_scr[...] = jnp.zeros(acc_scr.shape, F32)


def _near_bias(slope, r, q_pos, k_pos):
    visible = jnp.right_shift(k_pos, CHUNK_SHIFT) <= jnp.right_shift(q_pos, CHUNK_SHIFT)
    return jnp.where(visible, slope * (r - jnp.abs(q_pos - k_pos)).astype(F32), -jnp.inf)


def _split_distance(slope, d):
    return (-(slope * CHUNK) * jnp.right_shift(d, CHUNK_SHIFT).astype(F32),
            -slope * jnp.bitwise_and(d, CHUNK - 1).astype(F32))


def _diff_prompt_kernel(i_tab, j_tab, slopes_ref, q_ref, k_ref, v_ref, lq1, lk1, lq2, lk2, g_ref, o_ref,
                        m_scr, l_scr, acc_scr, lhs_scr, s_even, s_odd, *, tq, lam_init):
    h = pl.program_id(1)
    s_id = pl.program_id(2)
    n_pairs = pl.num_programs(2) - 1
    score_on = s_id < n_pairs
    consume_on = s_id >= 1
    i = i_tab[s_id]
    j = j_tab[s_id]
    ci = i_tab[jnp.maximum(s_id - 1, 0)]
    cj = j_tab[jnp.maximum(s_id - 1, 0)]
    slope = slopes_ref[h]
    half = DIFF_HEAD_DIM
    lane = lax.broadcasted_iota(jnp.int32, (tq, LANES), 1)

    @pl.when(jnp.logical_and(consume_on, cj == 0))
    def _():
        _diff_init(m_scr, l_scr, acc_scr)

    @pl.when(jnp.logical_and(score_on, j == 0))
    def _():
        qs = q_ref[...] * (DIFF_HEAD_DIM ** -0.5)
        lhs_scr[0] = jnp.where(lane < half, qs, 1.0).astype(BF16)
        lhs_scr[1] = jnp.where(lane >= half, qs, 1.0).astype(BF16)

    sub = min(DIFF_SUB, tq)
    units = [(n, qb) for qb in range(tq // sub) for n in range(2)]

    def score_tile(near, dst):
        kb = k_ref[...]
        if near:
            aug0 = aug1 = jnp.zeros_like(kb)
        else:
            d = (i - j) * tq - lax.broadcasted_iota(jnp.int32, (tq, LANES), 0)
            t_hi, t_lo = _split_distance(slope, d)
            aug0 = jnp.where(lane == half, t_hi, jnp.where(lane == half + 1, t_lo, 0.0))
            aug1 = jnp.where(lane == 0, t_hi, jnp.where(lane == 1, t_lo, 0.0))
        keys = [jnp.where(lane < half, kb, aug0).astype(BF16), jnp.where(lane >= half, kb, aug1).astype(BF16)]
        bias = None
        for n, qb in units:
            cols = slice(qb * sub, (qb + 1) * sub)
            if near:
                live = (qb + 1) * sub
                if n == 0:
                    k_pos = lax.broadcasted_iota(jnp.int32, (live, sub), 0)
                    q_pos = qb * sub + lax.broadcasted_iota(jnp.int32, (live, sub), 1)
                    bias = _near_bias(slope, q_pos, q_pos, k_pos)
                dst[n, :live, cols] = _dot_nt(keys[n][:live], lhs_scr[n, cols, :]) + bias
                if live < tq:
                    dst[n, live:, cols] = jnp.full((tq - live, sub), -jnp.inf, F32)
            else:
                dst[n, :, cols] = _dot_nt(keys[n], lhs_scr[n, cols, :])

    def consume_tile(src):
        vt = v_ref[...].T.astype(BF16)
        idx = range(len(units))
        cols = [slice(qb * sub, (qb + 1) * sub) for _, qb in units]
        s = [src[n, :, cols[u]] for u, (n, _) in enumerate(units)]
        m_old = [m_scr[n, :, cols[u]] for u, (n, _) in enumerate(units)]
        m_new = [jnp.maximum(m_old[u], jnp.max(s[u], axis=0, keepdims=True)) for u in idx]
        p = [jnp.exp(s[u] - m_new[u]) for u in idx]
        alpha = [jnp.exp(m_old[u] - m_new[u]) for u in idx]
        pv = [_dot(vt, p[u].astype(BF16)) for u in idx]
        for u, (n, _) in enumerate(units):
            l_scr[n, :, cols[u]] = alpha[u] * l_scr[n, :, cols[u]] + jnp.sum(p[u], axis=0, keepdims=True)
            acc_scr[n, :, cols[u]] = alpha[u] * acc_scr[n, :, cols[u]] + pv[u]
            m_scr[n, :, cols[u]] = m_new[u]

    def stage(cond, near, score, consume):
        for parity, (dst, src) in enumerate(((s_even, s_odd), (s_odd, s_even))):
            @pl.when(jnp.logical_and(cond, s_id % 2 == parity))
            def _(dst=dst, src=src):
                if score:
                    score_tile(near, dst)
                if consume:
                    consume_tile(src)

    both = jnp.logical_and(score_on, consume_on)
    stage(jnp.logical_and(both, j < i), False, True, True)
    stage(jnp.logical_and(both, j == i), True, True, True)
    stage(jnp.logical_not(consume_on), True, True, False)
    stage(jnp.logical_not(score_on), False, False, True)

    @pl.when(jnp.logical_and(consume_on, cj == ci))
    def _():
        o_ref[...] = _diff_finalize((acc_scr[0] / l_scr[0]).T, (acc_scr[1] / l_scr[1]).T,
                                    (lq1, lk1, lq2, lk2), g_ref, lam_init).astype(o_ref.dtype)


def _alibi_slopes():
    return jnp.asarray([2.0 ** (-8.0 * (i + 1) / DIFF_HEADS) for i in range(DIFF_HEADS)], F32)


def _diff_prompt(main, k_new, v_new, lams, subln_g, *, bsz, t, lam_init):
    tq = min(DIFF_TQ, t)
    nq = t // tq
    assert tq % CHUNK == 0 and t <= CHUNK * 255
    kern = functools.partial(_diff_prompt_kernel, tq=tq, lam_init=lam_init)
    qc, kc, vc = COL_DQ_MAIN // LANES, 0, 0
    pairs = [(i, j) for i in range(nq) for j in range(i + 1)]
    pairs.append(pairs[-1])
    i_tab = jnp.asarray([p[0] for p in pairs], jnp.int32)
    j_tab = jnp.asarray([p[1] for p in pairs], jnp.int32)
    zero = lambda b, h, s, it, jt: (0, 0)

    def consumed(tab, s):
        return tab[jnp.maximum(s - 1, 0)]

    grid_spec = pltpu.PrefetchScalarGridSpec(
        num_scalar_prefetch=2,
        grid=(bsz, DIFF_HEADS, len(pairs)),
        in_specs=[
            pl.BlockSpec(memory_space=pltpu.SMEM),
            pl.BlockSpec((tq, LANES), lambda b, h, s, it, jt: (b * nq + it[s], qc + h)),
            pl.BlockSpec((tq, LANES), lambda b, h, s, it, jt: (b * nq + jt[s], kc + h)),
            pl.BlockSpec((tq, LANES), lambda b, h, s, it, jt: (b * nq + consumed(jt, s), vc + h)),
        ] + [pl.BlockSpec((1, DIFF_HEAD_DIM), zero)] * 4 + [pl.BlockSpec((1, HEAD_DIM), zero)],
        out_specs=pl.BlockSpec((tq, LANES), lambda b, h, s, it, jt: (b * nq + consumed(it, s), h)),
        scratch_shapes=[pltpu.VMEM((2, 1, tq), F32), pltpu.VMEM((2, 1, tq), F32),
                        pltpu.VMEM((2, HEAD_DIM, tq), F32), pltpu.VMEM((2, tq, LANES), BF16),
                        pltpu.VMEM((2, tq, tq), F32), pltpu.VMEM((2, tq, tq), F32)],
    )
    return pl.pallas_call(
        kern,
        out_shape=jax.ShapeDtypeStruct((bsz * t, DIFF_HEADS * HEAD_DIM), BF16),
        grid_spec=grid_spec,
        compiler_params=_cparams(("parallel", "parallel", "arbitrary")),
        name="diff_attn_prompt",
    )(i_tab, j_tab, _alibi_slopes(), main, k_new, v_new, *lams, subln_g.reshape(1, -1))


def _diff_sample_kernel(slopes_ref, q_ref, kn_ref, vn_ref, kp_ref, vp_ref, lq1, lk1, lq2, lk2, g_ref, o_ref,
                        m_scr, l_scr, acc_scr, lhs_scr, *, t, tk, past, lam_init):
    j = pl.program_id(1)
    heads = range(DIFF_HEADS)

    def cols(h):
        return slice(h * HEAD_DIM, (h + 1) * HEAD_DIM)

    @pl.when(j == 0)
    def _():
        _diff_init(m_scr, l_scr, acc_scr)
        lane = lax.broadcasted_iota(jnp.int32, (t, LANES), 1)
        r = lax.broadcasted_iota(jnp.int32, (t, t), 0)
        c = lax.broadcasted_iota(jnp.int32, (t, t), 1)
        scores = []
        for h in heads:
            qs = q_ref[:, cols(h)] * (DIFF_HEAD_DIM ** -0.5)
            lhs_scr[h, 0:t, :] = jnp.where(lane < DIFF_HEAD_DIM, qs, 0.0).astype(BF16)
            lhs_scr[h, t:2 * t, :] = jnp.where(lane >= DIFF_HEAD_DIM, qs, 0.0).astype(BF16)
            bias = _near_bias(slopes_ref[h], r, past + r, past + c)
            scores.append(_dot_nt(lhs_scr[h], kn_ref[:, cols(h)].astype(BF16)) + jnp.concatenate([bias, bias], axis=0))
        _softmax_update(m_scr, l_scr, acc_scr, scores, [vn_ref[:, cols(h)].astype(BF16) for h in heads])

    d = (past - j * tk - lax.broadcasted_iota(jnp.int32, (1, tk), 1)).astype(F32)
    kt = jnp.swapaxes(kp_ref[...], 0, 1).astype(BF16)
    vt = jnp.swapaxes(vp_ref[...], 0, 1).astype(BF16)
    scores = [_dot_nt(lhs_scr[h], kt[h]) - slopes_ref[h] * d for h in heads]
    _softmax_update(m_scr, l_scr, acc_scr, scores, [vt[h] for h in heads])

    @pl.when(j == pl.num_programs(1) - 1)
    def _():
        for h in heads:
            o = acc_scr[h] / l_scr[h]
            o_ref[:, cols(h)] = _diff_finalize(o[:t], o[t:], (lq1, lk1, lq2, lk2), g_ref, lam_init).astype(o_ref.dtype)


def _diff_sample(main, k_new, v_new, k_past, v_past, lams, subln_g, *, bsz, t, lam_init):
    past = k_past.shape[1]
    tk = min(512, past)
    nk = past // tk
    width = DIFF_HEADS * HEAD_DIM
    kern = functools.partial(_diff_sample_kernel, t=t, tk=tk, past=past, lam_init=lam_init)
    zero = lambda b, j: (0, 0)
    cache = pl.BlockSpec((None, tk, DIFF_HEADS, HEAD_DIM), lambda b, j: (b, j, 0, 0))
    return pl.pallas_call(
        kern,
        out_shape=jax.ShapeDtypeStruct((bsz * t, width), BF16),
        grid=(bsz, nk),
        in_specs=[
            pl.BlockSpec(memory_space=pltpu.SMEM),
            pl.BlockSpec((t, width), lambda b, j: (b, COL_DQ_MAIN // width)),
            pl.BlockSpec((t, width), lambda b, j: (b, 0)),
            pl.BlockSpec((t, width), lambda b, j: (b, 0)),
            cache, cache,
        ] + [pl.BlockSpec((1, DIFF_HEAD_DIM), zero)] * 4 + [pl.BlockSpec((1, HEAD_DIM), zero)],
        out_specs=pl.BlockSpec((t, width), lambda b, j: (b, 0)),
        scratch_shapes=[pltpu.VMEM((DIFF_HEADS, 2 * t, 1), F32), pltpu.VMEM((DIFF_HEADS, 2 * t, 1), F32),
                        pltpu.VMEM((DIFF_HEADS, 2 * t, HEAD_DIM), F32), pltpu.VMEM((DIFF_HEADS, 2 * t, LANES), BF16)],
        compiler_params=_cparams(("parallel", "arbitrary")),
        name="diff_attn_sample",
    )(_alibi_slopes(), main, k_new, v_new, k_past, v_past, *lams, subln_g.reshape(1, -1))


def _merge_kernel(oa_ref, ob_ref, wa_ref, wb_ref, ga_ref, gb_ref, o_ref):
    ya = _dot(oa_ref[...], wa_ref[...])
    yb = _dot(ob_ref[...], wb_ref[...])
    o_ref[...] = (jax.nn.sigmoid(ga_ref[...]) * ya + jax.nn.sigmoid(gb_ref[...]) * yb).astype(o_ref.dtype)


def _merge(o_a, o_b, wa, wb, tail, *, tm):
    m = o_a.shape[0]
    d = wa.shape[1]
    tn = 512
    ga0, gb0 = 0, COL_GB_TAIL // tn
    return pl.pallas_call(
        _merge_kernel,
        out_shape=jax.ShapeDtypeStruct((m, d), BF16),
        grid=(m // tm, d // tn),
        in_specs=[
            pl.BlockSpec((tm, o_a.shape[1]), lambda i, j: (i, 0)),
            pl.BlockSpec((tm, o_b.shape[1]), lambda i, j: (i, 0)),
            pl.BlockSpec((wa.shape[0], tn), lambda i, j: (0, j)),
            pl.BlockSpec((wb.shape[0], tn), lambda i, j: (0, j)),
            pl.BlockSpec((tm, tn), lambda i, j: (i, ga0 + j)),
            pl.BlockSpec((tm, tn), lambda i, j: (i, gb0 + j)),
        ],
        out_specs=pl.BlockSpec((tm, tn), lambda i, j: (i, j)),
        compiler_params=_cparams(("parallel", "arbitrary")),
        name="merge",
    )(o_a, o_b, wa, wb, tail, tail)


def _cross_kernel(q_ref, k_ref, v_ref, o_ref):
    dh = q_ref.shape[-1] // MEM_HEADS
    for h in range(MEM_HEADS):
        sl = slice(h * dh, (h + 1) * dh)
        s = _dot_nt(q_ref[:, sl], k_ref[0, :, sl].astype(BF16)) * (dh ** -0.5)
        s = s - jnp.max(s, axis=-1, keepdims=True)
        p = jnp.exp(s)
        p = p / jnp.sum(p, axis=-1, keepdims=True)
        o_ref[:, sl] = _dot(p.astype(BF16), v_ref[0, :, sl].astype(BF16)).astype(o_ref.dtype)


def _cross_attend(q, mem_k, mem_v, *, bsz, t, tm):
    d = q.shape[1]
    nt = t // tm
    mt = mem_k.shape[1]
    return pl.pallas_call(
        _cross_kernel,
        out_shape=jax.ShapeDtypeStruct(q.shape, BF16),
        grid=(bsz, nt),
        in_specs=[
            pl.BlockSpec((tm, d), lambda b, i: (b * nt + i, 0)),
            pl.BlockSpec((1, mt, d), lambda b, i: (b, 0, 0)),
            pl.BlockSpec((1, mt, d), lambda b, i: (b, 0, 0)),
        ],
        out_specs=pl.BlockSpec((tm, d), lambda b, i: (b * nt + i, 0)),
        compiler_params=_cparams(("parallel", "arbitrary")),
        name="cross_attn",
    )(q, mem_k, mem_v)


def _topk_rows(work, cidx, k):
    n, tn = work.shape
    rid = lax.broadcasted_iota(jnp.int32, (n, tn), 0).astype(F32)
    kid = lax.broadcasted_iota(jnp.int32, (k, tn), 0)
    vals = jnp.zeros((k, tn), F32)
    idxs = jnp.zeros((k, tn), F32)
    for t in range(k):
        m = jnp.max(work, axis=0, keepdims=True)
        pos = jnp.min(jnp.where(work == m, rid, float(n)), axis=0, keepdims=True)
        hit = rid == pos
        if cidx is None:
            picked = pos
        else:
            picked = jnp.sum(jnp.where(hit, cidx, 0.0), axis=0, keepdims=True)
        vals = jnp.where(kid == t, m, vals)
        idxs = jnp.where(kid == t, picked, idxs)
        work = jnp.where(hit, -jnp.inf, work)
    return vals, idxs


def _product_candidates(v1, i1, v2, i2, k):
    assert k == 2 * SUBLANES
    sub = lax.broadcasted_iota(jnp.int32, (SUBLANES, v1.shape[1]), 0)
    vals, idxs = [], []
    for a in range(k // 2):
        nb = k // (a + 1)
        width = k if nb > SUBLANES else SUBLANES
        cv = v1[a:a + 1, :] + v2[:width, :]
        ci = i1[a:a + 1, :] * PEER_N_KEYS + i2[:width, :]
        if nb < SUBLANES:
            cv = jnp.where(sub < nb, cv, -jnp.inf)
        vals.append(cv)
        idxs.append(ci)
    vals.append(v1[k // 2:, :] + v2[0:1, :])
    idxs.append(i1[k // 2:, :] * PEER_N_KEYS + i2[0:1, :])
    return jnp.concatenate(vals, axis=0), jnp.concatenate(idxs, axis=0)


def _peer_route_kernel(q_ref, keys_hi_ref, keys_lo_ref, e_ref, g_ref):
    k = PEER_TOPK
    for h in range(PEER_HEADS):
        sub = []
        for p in range(2):
            c0 = (h * 2 + p) * PEER_N_KEYS
            q_parts = _split_bf16(q_ref[:, c0:c0 + PEER_N_KEYS])
            s = _dot3_parts((keys_hi_ref[h, p], keys_lo_ref[h, p]), q_parts, _dot_nt)
            sub.append(_topk_rows(s, None, k))
        (v1, i1), (v2, i2) = sub
        cand, cidx = _product_candidates(v1, i1, v2, i2, k)
        best, eidx = _topk_rows(cand, cidx, k)
        ex = jnp.exp(best - jnp.max(best, axis=0, keepdims=True))
        gate = ex / jnp.sum(ex, axis=0, keepdims=True)
        e_ref[h * k:(h + 1) * k, :] = eidx.astype(jnp.int32)
        g_ref[h * k:(h + 1) * k, :] = gate


HALF_KEYS = PEER_N_KEYS // 2
BF16_HIGH_MASK = 0xFFFF0000


def _peer_gate_kernel(et_ref, gt_ref, o_ref, e_scr, g_scr):
    tg = et_ref.shape[1]
    e_scr[...] = et_ref[...].T
    g_scr[...] = gt_ref[...].T
    kid = lax.broadcasted_iota(jnp.int32, (PEER_N_KEYS, PEER_N_KEYS), 0)

    def body(n8, carry):
        rows = pl.ds(pl.multiple_of(n8 * SUBLANES, SUBLANES), SUBLANES)
        e8 = e_scr[rows, :]
        g8 = g_scr[rows, :]
        a8 = jnp.right_shift(e8, 7)
        b8 = jnp.bitwise_and(e8, PEER_N_KEYS - 1)
        toks = range(SUBLANES)
        at = [jnp.where(a8[s:s + 1, :] == kid, g8[s:s + 1, :], 0.0).astype(BF16) for s in toks]
        bt = [jnp.where(b8[s:s + 1, :] == kid, 1.0, 0.0).astype(BF16) for s in toks]
        grids = [_dot_nt(at[s], bt[s]) for s in toks]
        bits = [pltpu.bitcast(grids[s].astype(BF16).astype(F32), jnp.uint32) for s in toks]
        words = [jnp.bitwise_or(jnp.bitwise_and(bits[s][HALF_KEYS:, :], jnp.uint32(BF16_HIGH_MASK)),
                                jnp.right_shift(bits[s][:HALF_KEYS, :], jnp.uint32(16))) for s in toks]
        o_ref[:, rows, :] = jnp.swapaxes(jnp.stack(words, axis=0), 0, 1)
        return carry

    lax.fori_loop(0, tg // SUBLANES, body, 0, unroll=4)


def _peer_route(qp, sub_keys, *, tn):
    n = qp.shape[0]
    rows = PEER_HEADS * PEER_TOPK
    keys_hi, keys_lo = _split_bf16(sub_keys.astype(F32))
    keys_spec = pl.BlockSpec(sub_keys.shape, lambda i: (0, 0, 0, 0))
    e_t, g_t = pl.pallas_call(
        _peer_route_kernel,
        out_shape=(jax.ShapeDtypeStruct((rows, n), jnp.int32), jax.ShapeDtypeStruct((rows, n), F32)),
        grid=(n // tn,),
        in_specs=[pl.BlockSpec((tn, qp.shape[1]), lambda i: (i, 0)), keys_spec, keys_spec],
        out_specs=(pl.BlockSpec((rows, tn), lambda i: (0, i)), pl.BlockSpec((rows, tn), lambda i: (0, i))),
        compiler_params=_cparams(("parallel",)),
        name="peer_route",
    )(qp, keys_hi, keys_lo)
    return pl.pallas_call(
        _peer_gate_kernel,
        out_shape=jax.ShapeDtypeStruct((HALF_KEYS, n, PEER_N_KEYS), jnp.uint32),
        grid=(n // tn,),
        in_specs=[pl.BlockSpec((rows, tn), lambda i: (0, i)), pl.BlockSpec((rows, tn), lambda i: (0, i))],
        out_specs=pl.BlockSpec((HALF_KEYS, tn, PEER_N_KEYS), lambda i: (0, i, 0)),
        scratch_shapes=[pltpu.VMEM((tn, rows), jnp.int32), pltpu.VMEM((tn, rows), F32)],
        compiler_params=_cparams(("parallel",)),
        name="peer_gates",
    )(e_t, g_t)


def _gelu_tanh(x):
    return 0.5 * x * (1.0 + jnp.tanh(math.sqrt(2.0 / math.pi) * (x + 0.044715 * (x * x * x))))


PEER_BLOCKS_PER_STEP = 4
PEER_TN = 1024
PEER_VMEM_LIMIT = 60 * 1024 * 1024


def _peer_expert_kernel(x_ref, gn_ref, gpa_ref, gpb_ref, *rest):
    nblk = PEER_BLOCKS_PER_STEP
    u_refs, v_refs = rest[:nblk], rest[nblk:2 * nblk]
    gf_ref, o_ref, h_scr, acc_scr, coef_scr = rest[2 * nblk:]
    s = pl.program_id(1)
    last = pl.num_programs(1) - 1

    @pl.when(s == 0)
    def _():
        h_scr[...] = _rms(x_ref[...], gn_ref[...]).astype(BF16)
        acc_scr[...] = jnp.zeros(acc_scr.shape, F32)
        coef_scr[...] = jnp.zeros(coef_scr.shape, BF16)

    v_cat = jnp.concatenate([v[...].astype(BF16) for v in v_refs], axis=0)
    acc_scr[...] += _dot(coef_scr[(s + 1) % 2], v_cat)

    def unpack(word):
        return (pltpu.bitcast(jnp.left_shift(word, jnp.uint32(16)), F32),
                pltpu.bitcast(jnp.bitwise_and(word, jnp.uint32(BF16_HIGH_MASK)), F32))

    a_lo, a_hi = unpack(gpa_ref[...])
    b_lo, b_hi = unpack(gpb_ref[...])
    u_cat = jnp.concatenate([u[...].astype(BF16) for u in u_refs], axis=0)
    hid = _gelu_tanh(_dot_nt(h_scr[...], u_cat))
    coef_scr[s % 2] = (jnp.concatenate([a_lo, b_lo, a_hi, b_hi], axis=1) * hid).astype(BF16)

    @pl.when(s == last)
    def _():
        o_ref[...] = _rms(x_ref[...] + acc_scr[...], gf_ref[...])


def _peer_experts(x, norm_g, gates, u_tab, v_tab, final_g, *, tn):
    n, d = x.shape
    eb = PEER_N_KEYS
    assert u_tab.shape[0] == eb * eb
    nblk = PEER_BLOCKS_PER_STEP
    ns = eb // nblk

    def cur(s):
        return jnp.minimum(s, ns - 1)

    def prev(s):
        return jnp.maximum(s - 1, 0)

    def table(which, q):
        return pl.BlockSpec((eb, d), lambda i, s: (which(s) + q * ns, 0))

    return pl.pallas_call(
        _peer_expert_kernel,
        out_shape=jax.ShapeDtypeStruct((n, d), F32),
        grid=(n // tn, ns + 1),
        in_specs=[
            pl.BlockSpec((tn, d), lambda i, s: (i, 0), pipeline_mode=pl.Buffered(1)),
            pl.BlockSpec((1, d), lambda i, s: (0, 0)),
            pl.BlockSpec((None, tn, eb), lambda i, s: (cur(s), i, 0)),
            pl.BlockSpec((None, tn, eb), lambda i, s: (cur(s) + ns, i, 0)),
        ] + [table(cur, q) for q in range(nblk)] + [table(prev, q) for q in range(nblk)] + [
            pl.BlockSpec((1, d), lambda i, s: (0, 0)),
        ],
        out_specs=pl.BlockSpec((tn, d), lambda i, s: (i, 0), pipeline_mode=pl.Buffered(1)),
        scratch_shapes=[pltpu.VMEM((tn, d), BF16), pltpu.VMEM((tn, d), F32),
                        pltpu.VMEM((2, tn, nblk * eb), BF16)],
        compiler_params=_cparams(("parallel", "arbitrary"), vmem=PEER_VMEM_LIMIT),
        name="peer_experts",
    )(x, norm_g.reshape(1, d), gates, gates, *([u_tab] * nblk), *([v_tab] * nblk), final_g.reshape(1, d))


def _layer(x, wts, layer, mem_k, mem_v, conv_prev, gdn_state, k_past, v_past, *, tm):
    bsz, t, d = x.shape
    n = bsz * t
    x2d = x.reshape(n, d)
    main, k_new, v_new, tail = _in_proj(x2d, wts["norm_mix_g"], wts["w_in"], tm=min(IN_TM, n))
    conv_dim = 3 * GDN_WIDTH
    conv_prev8 = jnp.concatenate(
        [jnp.zeros((bsz, SUBLANES - (GDN_CONV - 1), conv_dim), F32), conv_prev.astype(F32)], axis=1)
    o_a, s_new = _gdn(main, tail, conv_prev8, gdn_state.astype(F32), wts["gdn_conv_w"], wts["gdn_a_log"],
                      wts["gdn_dt_bias"], wts["gdn_norm_g"], bsz=bsz, t=t)
    conv_new = main.reshape(bsz, t, -1)[:, t - (GDN_CONV - 1):, :conv_dim]
    lam_init = 0.8 - 0.6 * math.exp(-0.3 * layer)
    lams = tuple(wts[k].reshape(1, -1) for k in ("diff_lambda_q1", "diff_lambda_k1", "diff_lambda_q2", "diff_lambda_k2"))
    if k_past is None:
        o_b = _diff_prompt(main, k_new, v_new, lams, wts["diff_subln_g"], bsz=bsz, t=t, lam_init=lam_init)
    else:
        o_b = _diff_sample(main, k_new, v_new, k_past, v_past, lams, wts["diff_subln_g"], bsz=bsz, t=t,
                           lam_init=lam_init)
    k_rows = k_new.reshape(bsz, t, DIFF_HEADS, HEAD_DIM)
    v_rows = v_new.reshape(bsz, t, DIFF_HEADS, HEAD_DIM)
    merged = _merge(o_a, o_b, wts["w_branch_a"], wts["w_branch_b"], tail, tm=tm)
    x1 = _matmul(merged, wts["w_out"], residual=x2d, tm=tm, tn=MM_TN, name="out_proj")
    qm = _matmul(x1, wts["w_mq"], norm_g=wts["norm_cross_g"], tm=tm, tn=MM_TN, out_dtype=BF16, name="mem_q")
    mt = mem_k.shape[1]
    oc = _cross_attend(qm, mem_k.reshape(bsz, mt, d), mem_v.reshape(bsz, mt, d), bsz=bsz, t=t, tm=min(tm, t))
    x2 = _matmul(oc, wts["w_mo"], residual=x1, tm=tm, tn=MM_TN, name="mem_o")
    qp = _matmul(x2, wts["peer_w_q"], norm_g=wts["norm_ffn_g"], tm=tm, tn=MM_TN, name="peer_q")
    tn = min(PEER_TN, n)
    gates = _peer_route(qp, wts["peer_sub_keys"], tn=LANES)
    y = _peer_experts(x2, wts["norm_ffn_g"], gates, wts["peer_u"], wts["peer_v"], wts["final_norm_g"], tn=tn)
    return y.reshape(bsz, t, d), k_rows, v_rows, s_new, conv_new


def kernel(x_prompt, x_sample, cache_diff_k, cache_diff_v, state_gdn, state_conv, cache_mem_k, cache_mem_v,
           mem_prompt, norm_mix_g, w_in, gdn_conv_w, gdn_a_log, gdn_dt_bias, gdn_norm_g,
           diff_lambda_q1, diff_lambda_k1, diff_lambda_q2, diff_lambda_k2, diff_subln_g,
           w_branch_a, w_branch_b, w_out, norm_cross_g, norm_mem_g, w_mq, w_mk, w_mv, w_mo,
           norm_ffn_g, peer_w_q, peer_sub_keys, peer_u, peer_v, final_norm_g):
    depth = w_in.shape[0]
    assert depth == 1, "final norm is fused into the last layer's PEER kernel"
    l = 0
    wts = {
        "norm_mix_g": norm_mix_g[l], "w_in": _in_proj_weights(w_in[l]), "gdn_conv_w": gdn_conv_w[l],
        "gdn_a_log": gdn_a_log[l], "gdn_dt_bias": gdn_dt_bias[l], "gdn_norm_g": gdn_norm_g[l],
        "diff_lambda_q1": diff_lambda_q1[l], "diff_lambda_k1": diff_lambda_k1[l],
        "diff_lambda_q2": diff_lambda_q2[l], "diff_lambda_k2": diff_lambda_k2[l],
        "diff_subln_g": diff_subln_g[l], "w_branch_a": w_branch_a[l].astype(BF16),
        "w_branch_b": w_branch_b[l].astype(BF16), "w_out": w_out[l].astype(BF16),
        "norm_cross_g": norm_cross_g[l], "w_mq": w_mq[l].astype(BF16), "w_mo": w_mo[l].astype(BF16),
        "norm_ffn_g": norm_ffn_g[l], "peer_w_q": peer_w_q[l].astype(BF16), "peer_sub_keys": peer_sub_keys[l],
        "peer_u": peer_u[l], "peer_v": peer_v[l], "final_norm_g": final_norm_g,
    }
    bp, tp, d = x_prompt.shape
    bs, ts, _ = x_sample.shape
    mem2d = mem_prompt.reshape(-1, d)
    tmm = min(512, mem2d.shape[0])
    mem_k = _matmul(mem2d, w_mk[l].astype(BF16), norm_g=norm_mem_g[l], tm=tmm, tn=512, name="mem_k")
    mem_v = _matmul(mem2d, w_mv[l].astype(BF16), norm_g=norm_mem_g[l], tm=tmm, tn=512, name="mem_v")
    mshape = mem_prompt.shape[:2] + (MEM_HEADS, d // MEM_HEADS)
    mem_k = mem_k.reshape(mshape)
    mem_v = mem_v.reshape(mshape)
    conv0 = jnp.zeros((bp, GDN_CONV - 1, 3 * GDN_WIDTH), F32)
    s0 = jnp.zeros((bp, GDN_HEADS, HEAD_DIM, HEAD_DIM), F32)
    yp, pk, pv, ps, pc = _layer(x_prompt, wts, l, mem_k, mem_v, conv0, s0, None, None, tm=min(1024, bp * tp))
    ys, sk, sv, ss, sc = _layer(x_sample, wts, l, cache_mem_k[l], cache_mem_v[l], state_conv[l], state_gdn[l],
                                cache_diff_k[l], cache_diff_v[l], tm=min(256, bs * ts))
    return (yp, ys, pk[None], pv[None], ps[None], pc[None], mem_k[None], mem_v[None],
            sk[None], sv[None], ss[None], sc[None])
```

```python
import functools
import math

import jax
import jax.numpy as jnp
from jax import lax
from jax.experimental import pallas as pl
from jax.experimental.pallas import tpu as pltpu

F32 = jnp.float32
BF16 = jnp.bfloat16
HIGHEST = lax.Precision.HIGHEST

RMS_EPS = 1e-6
CHUNK = 64
CHUNK_SHIFT = 6
DIFF_TQ = 1024
GDN_CHUNKS_PER_STEP = 2
DIFF_SUB = 256
GDN_HEADS = 8
HEAD_DIM = 128
GDN_WIDTH = GDN_HEADS * HEAD_DIM
GDN_CONV = 4
DIFF_HEADS = 8
DIFF_HEAD_DIM = 64
MEM_HEADS = 4
PEER_HEADS = 8
PEER_N_KEYS = 128
PEER_TOPK = 16
LANES = 128
SUBLANES = 8
V7X_VMEM_LIMIT = 52 * 1024 * 1024
MM_TN = 1024


def _cparams(sem, vmem=V7X_VMEM_LIMIT):
    return pltpu.CompilerParams(dimension_semantics=sem, vmem_limit_bytes=vmem)


def _dot(a, b, precision=None):
    return jnp.dot(a, b, preferred_element_type=F32, precision=precision)


def _dot_nt(a, b, precision=None):
    return lax.dot_general(a, b, (((1,), (1,)), ((), ())), preferred_element_type=F32, precision=precision)


def _split_bf16(x):
    hi = x.astype(BF16)
    return hi, (x - hi.astype(F32)).astype(BF16)


def _cat_parts(p, q):
    return jnp.concatenate([p[0], q[0]], axis=0), jnp.concatenate([p[1], q[1]], axis=0)


def _dot3_parts(a_parts, b_parts, dot):
    a_hi, a_lo = a_parts
    b_hi, b_lo = b_parts
    m = a_hi.shape[0]
    both = dot(jnp.concatenate([a_hi, a_lo], axis=0), b_hi)
    return both[:m] + both[m:] + dot(a_hi, b_lo)


def _dot3(a, b):
    return _dot3_parts(_split_bf16(a), _split_bf16(b), _dot)


def _rms(xf, g):
    return xf * lax.rsqrt(jnp.mean(xf * xf, axis=-1, keepdims=True) + RMS_EPS) * g


def _mm_kernel(*refs, has_norm, has_res):
    it = iter(refs)
    x_ref = next(it)
    g_ref = next(it) if has_norm else None
    w_ref = next(it)
    r_ref = next(it) if has_res else None
    o_ref = next(it)
    h_scr = next(it)

    @pl.when(pl.program_id(1) == 0)
    def _():
        xf = x_ref[...].astype(F32)
        if has_norm:
            xf = _rms(xf, g_ref[...])
        h_scr[...] = xf.astype(BF16)

    acc = _dot(h_scr[...], w_ref[...].astype(BF16))
    if has_res:
        acc = acc + r_ref[...]
    o_ref[...] = acc.astype(o_ref.dtype)


def _matmul(x, w, *, norm_g=None, residual=None, tm, tn, out_dtype=F32, name="matmul"):
    m, k = x.shape
    n = w.shape[1]
    assert m % tm == 0 and n % tn == 0, (m, n, tm, tn)
    in_specs = [pl.BlockSpec((tm, k), lambda i, j: (i, 0))]
    args = [x]
    if norm_g is not None:
        in_specs.append(pl.BlockSpec((1, k), lambda i, j: (0, 0)))
        args.append(norm_g.reshape(1, k).astype(F32))
    in_specs.append(pl.BlockSpec((k, tn), lambda i, j: (0, j)))
    args.append(w)
    if residual is not None:
        in_specs.append(pl.BlockSpec((tm, tn), lambda i, j: (i, j)))
        args.append(residual)
    return pl.pallas_call(
        functools.partial(_mm_kernel, has_norm=norm_g is not None, has_res=residual is not None),
        out_shape=jax.ShapeDtypeStruct((m, n), out_dtype),
        grid=(m // tm, n // tn),
        in_specs=in_specs,
        out_specs=pl.BlockSpec((tm, tn), lambda i, j: (i, j)),
        scratch_shapes=[pltpu.VMEM((tm, k), BF16)],
        compiler_params=_cparams(("parallel", "arbitrary")),
        name=name,
    )(*args)


IN_TN = 512
IN_TM = 1024
IN_W1_TILES = 8
IN_W2_TILES = 14
IN_MAIN_TILES = 10
IN_K_TILE0 = 10
IN_V_TILE0 = 12
IN_TAIL_TILE0 = 14
IN_TILES = IN_W1_TILES + IN_W2_TILES + 1
MAIN_COLS = IN_MAIN_TILES * IN_TN
TAIL_COLS = (IN_TILES - IN_TAIL_TILE0) * IN_TN
COL_DQ_MAIN = 4096
COL_GB_TAIL = 2048
COL_BA_TAIL = 4096


def _in_proj_kernel(x_ref, g_ref, w1_ref, w2_ref, w3_ref, main_ref, k_ref, v_ref, tail_ref, h_scr):
    j = pl.program_id(1)

    @pl.when(j == 0)
    def _():
        h_scr[...] = _rms(x_ref[...], g_ref[...]).astype(BF16)

    routes = ((0, IN_W1_TILES, w1_ref, main_ref),
              (IN_W1_TILES, IN_K_TILE0, w2_ref, main_ref),
              (IN_K_TILE0, IN_V_TILE0, w2_ref, k_ref),
              (IN_V_TILE0, IN_TAIL_TILE0, w2_ref, v_ref),
              (IN_TAIL_TILE0, IN_TILES - 1, w2_ref, tail_ref),
              (IN_TILES - 1, IN_TILES, w3_ref, tail_ref))
    for lo, hi, w_ref, dst in routes:
        @pl.when(jnp.logical_and(j >= lo, j < hi))
        def _(w_ref=w_ref, dst=dst):
            dst[...] = _dot(h_scr[...], w_ref[...])


def _in_proj_weights(w_in):
    c_z = 4 * GDN_WIDTH
    c_ba = c_z + 2 * GDN_HEADS
    assert c_z == IN_W1_TILES * IN_TN and w_in.shape[1] - c_ba == IN_W2_TILES * IN_TN
    w_bf16 = w_in.astype(BF16)
    return w_bf16, w_bf16[:, c_ba:]


def _in_proj(x, norm_g, w_pieces, *, tm):
    m, d = x.shape
    w_all, w2 = w_pieces
    w1 = w3 = w_all
    tn = IN_TN
    out = lambda cols: jax.ShapeDtypeStruct((m, cols), F32)
    return pl.pallas_call(
        _in_proj_kernel,
        out_shape=(out(MAIN_COLS), out(2 * IN_TN), out(2 * IN_TN), out(TAIL_COLS)),
        grid=(m // tm, IN_TILES),
        in_specs=[
            pl.BlockSpec((tm, d), lambda i, j: (i, 0)),
            pl.BlockSpec((1, d), lambda i, j: (0, 0)),
            pl.BlockSpec((d, tn), lambda i, j: (0, jnp.minimum(j, IN_W1_TILES - 1))),
            pl.BlockSpec((d, tn), lambda i, j: (0, jnp.clip(j - IN_W1_TILES, 0, IN_W2_TILES - 1))),
            pl.BlockSpec((d, tn), lambda i, j: (0, IN_W1_TILES)),
        ],
        out_specs=(
            pl.BlockSpec((tm, tn), lambda i, j: (i, jnp.minimum(j, IN_MAIN_TILES - 1))),
            pl.BlockSpec((tm, tn), lambda i, j: (i, jnp.clip(j - IN_K_TILE0, 0, 1))),
            pl.BlockSpec((tm, tn), lambda i, j: (i, jnp.clip(j - IN_V_TILE0, 0, 1))),
            pl.BlockSpec((tm, tn), lambda i, j: (i, jnp.clip(j - IN_TAIL_TILE0, 0, IN_TILES - IN_TAIL_TILE0 - 1))),
        ),
        scratch_shapes=[pltpu.VMEM((tm, d), BF16)],
        compiler_params=_cparams(("parallel", "arbitrary")),
        name="in_proj",
    )(x, norm_g.reshape(1, d).astype(F32), w1, w2, w3)


def _gdn_kernel(qkv_ref, z_ref, ba_ref, prev_ref, s0_ref, convw_ref, alog_ref, dtb_ref, ng_ref,
                o_ref, sfin_ref, s_scr, xbuf, *, L, nch):
    step = pl.program_id(1)
    rows = nch * L

    @pl.when(step == 0)
    def _():
        s_scr[...] = s0_ref[0]
        xbuf[0:SUBLANES, :] = prev_ref[0]

    xbuf[SUBLANES:SUBLANES + rows, :] = qkv_ref[...]
    base = SUBLANES - (GDN_CONV - 1)
    y = xbuf[base:base + rows, :] * convw_ref[0:1, :]
    for i in range(1, GDN_CONV):
        y = y + xbuf[base + i:base + i + rows, :] * convw_ref[i:i + 1, :]
    y = y * jax.nn.sigmoid(y)
    carry = xbuf[rows:rows + SUBLANES, :]
    xbuf[0:SUBLANES, :] = carry

    row = lax.broadcasted_iota(jnp.int32, (L, L), 0)
    col = lax.broadcasted_iota(jnp.int32, (L, L), 1)
    lower = row >= col
    strict = row > col
    eye = (row == col).astype(F32)
    log_l = int(math.log2(L))
    n_double = log_l - 1
    assert 2 ** log_l == L

    ba = ba_ref[...]
    beta_all = jax.nn.sigmoid(ba)
    a_in = ba + dtb_ref[...]
    softplus = jnp.maximum(a_in, 0.0) + jnp.log(1.0 + jnp.exp(-jnp.abs(a_in)))
    g_all = -jnp.exp(alog_ref[...]) * softplus
    rr = lax.broadcasted_iota(jnp.int32, (rows, rows), 0)
    cc = lax.broadcasted_iota(jnp.int32, (rows, rows), 1)
    same_chunk_lower = jnp.logical_and(rr >= cc, jnp.right_shift(rr, log_l) == jnp.right_shift(cc, log_l))
    g_cum = _dot(same_chunk_lower.astype(F32), g_all, HIGHEST)
    g_cum_t = g_cum.T

    heads = range(GDN_HEADS)
    chains = [(c, h) for c in range(nch) for h in heads]
    ids = range(len(chains))

    def tok(c):
        return slice(c * L, (c + 1) * L)

    def head_cols(part, c, h):
        return y[tok(c), part * GDN_WIDTH + h * HEAD_DIM:part * GDN_WIDTH + (h + 1) * HEAD_DIM]

    q = [head_cols(0, c, h) for c, h in chains]
    k = [head_cols(1, c, h) for c, h in chains]
    v = [head_cols(2, c, h) for c, h in chains]
    q = [a * lax.rsqrt(jnp.sum(a * a, axis=-1, keepdims=True) + 1e-6) * (HEAD_DIM ** -0.5) for a in q]
    k = [a * lax.rsqrt(jnp.sum(a * a, axis=-1, keepdims=True) + 1e-6) for a in k]
    beta = [beta_all[tok(c), h:h + 1] for c, h in chains]
    g_col = [g_cum[tok(c), GDN_HEADS + h:GDN_HEADS + h + 1] for c, h in chains]
    g_row = [g_cum_t[GDN_HEADS + h:GDN_HEADS + h + 1, tok(c)] for c, h in chains]
    decay = [jnp.where(lower, jnp.exp(jnp.where(lower, g_col[n] - g_row[n], 0.0)), 0.0) for n in ids]
    g_last = [g_col[n][L - 1:L, :] for n in ids]
    e_col = [jnp.exp(g_col[n]) for n in ids]
    qk_kk = [_dot_nt(jnp.concatenate([q[n], k[n]], axis=0).astype(BF16), k[n].astype(BF16)) for n in ids]
    x = [-jnp.where(strict, beta[n] * qk_kk[n][L:] * decay[n], 0.0) for n in ids]
    t_inv = [eye + x[n] for n in ids]
    x_parts = [_split_bf16(x[n]) for n in ids]
    xp = [_dot3_parts(x_parts[n], x_parts[n], _dot) for n in ids]
    for _ in range(n_double - 1):
        xp_parts = [_split_bf16(xp[n]) for n in ids]
        both = [_dot3_parts(_cat_parts(_split_bf16(t_inv[n]), xp_parts[n]), xp_parts[n], _dot) for n in ids]
        t_inv = [t_inv[n] + both[n][:L] for n in ids]
        xp = [both[n][L:] for n in ids]
    t_inv = [t_inv[n] + _dot3(t_inv[n], xp[n]) for n in ids]
    sol = [_dot3(t_inv[n], jnp.concatenate([v[n] * beta[n], k[n] * (beta[n] * e_col[n])], axis=-1)) for n in ids]
    qk = [jnp.where(lower, qk_kk[n][:L] * decay[n], 0.0) for n in ids]
    k_dec_t = [(k[n] * jnp.exp(g_last[n] - g_col[n])).T for n in ids]
    left_s = [jnp.concatenate([sol[n][:, HEAD_DIM:], q[n] * e_col[n]], axis=0).astype(BF16) for n in ids]
    left_v = [jnp.concatenate([qk[n], k_dec_t[n]], axis=0).astype(BF16) for n in ids]

    s = [s_scr[h] for h in heads]
    for c in range(nch):
        n0 = c * GDN_HEADS
        ws_qs = [_dot(left_s[n0 + h], s[h].astype(BF16)) for h in heads]
        v_new = [sol[n0 + h][:, :HEAD_DIM] - ws_qs[h][:L] for h in heads]
        tail = [_dot(left_v[n0 + h], v_new[h].astype(BF16)) for h in heads]
        s = [s[h] * jnp.exp(g_last[n0 + h]) + tail[h][L:] for h in heads]
        for h in heads:
            zf = z_ref[tok(c), h * HEAD_DIM:(h + 1) * HEAD_DIM]
            o = _rms(ws_qs[h][L:] + tail[h][:L], ng_ref[...]) * (zf * jax.nn.sigmoid(zf))
            o_ref[tok(c), h * HEAD_DIM:(h + 1) * HEAD_DIM] = o.astype(o_ref.dtype)
    for h in heads:
        s_scr[h] = s[h]

    @pl.when(step == pl.num_programs(1) - 1)
    def _():
        sfin_ref[0] = s_scr[...]


def _decay_lanes(p):
    return jnp.zeros((1, LANES), F32).at[0, GDN_HEADS:2 * GDN_HEADS].set(p.astype(F32))


def _gdn(main, tail, conv_prev8, s0, conv_w, a_log, dt_bias, norm_g, *, bsz, t):
    L = min(CHUNK, t)
    nch = GDN_CHUNKS_PER_STEP if (t // L) % GDN_CHUNKS_PER_STEP == 0 else 1
    nc = t // (L * nch)
    rows = L * nch
    conv_dim = 3 * GDN_WIDTH
    kern = functools.partial(_gdn_kernel, L=L, nch=nch)
    o, s_fin = pl.pallas_call(
        kern,
        out_shape=(jax.ShapeDtypeStruct((bsz * t, GDN_WIDTH), BF16),
                   jax.ShapeDtypeStruct((bsz, GDN_HEADS, HEAD_DIM, HEAD_DIM), F32)),
        grid=(bsz, nc),
        in_specs=[
            pl.BlockSpec((rows, conv_dim), lambda b, c: (b * nc + c, 0)),
            pl.BlockSpec((rows, GDN_WIDTH), lambda b, c: (b * nc + c, conv_dim // GDN_WIDTH)),
            pl.BlockSpec((rows, LANES), lambda b, c: (b * nc + c, COL_BA_TAIL // LANES)),
            pl.BlockSpec((1, SUBLANES, conv_dim), lambda b, c: (b, 0, 0)),
            pl.BlockSpec((1, GDN_HEADS, HEAD_DIM, HEAD_DIM), lambda b, c: (b, 0, 0, 0)),
            pl.BlockSpec((GDN_CONV, conv_dim), lambda b, c: (0, 0)),
            pl.BlockSpec((1, LANES), lambda b, c: (0, 0)),
            pl.BlockSpec((1, LANES), lambda b, c: (0, 0)),
            pl.BlockSpec((1, HEAD_DIM), lambda b, c: (0, 0)),
        ],
        out_specs=(pl.BlockSpec((rows, GDN_WIDTH), lambda b, c: (b * nc + c, 0)),
                   pl.BlockSpec((1, GDN_HEADS, HEAD_DIM, HEAD_DIM), lambda b, c: (b, 0, 0, 0))),
        scratch_shapes=[pltpu.VMEM((GDN_HEADS, HEAD_DIM, HEAD_DIM), F32),
                        pltpu.VMEM((SUBLANES + rows + SUBLANES, conv_dim), F32)],
        compiler_params=_cparams(("parallel", "arbitrary")),
        name="gdn",
    )(main, main, tail, conv_prev8, s0, conv_w, _decay_lanes(a_log), _decay_lanes(dt_bias),
      norm_g.reshape(1, -1))
    return o, s_fin


def _lambda_value(lam_refs, lam_init):
    lq1, lk1, lq2, lk2 = (r[...] for r in lam_refs)
    return (jnp.exp(jnp.sum(lq1 * lk1, axis=-1, keepdims=True))
            - jnp.exp(jnp.sum(lq2 * lk2, axis=-1, keepdims=True)) + lam_init)


def _softmax_update(m_scr, l_scr, acc_scr, scores, values):
    idx = range(len(scores))
    m_old = [m_scr[n] for n in idx]
    m_new = [jnp.maximum(m_old[n], jnp.max(scores[n], axis=-1, keepdims=True)) for n in idx]
    p = [jnp.exp(scores[n] - m_new[n]) for n in idx]
    alpha = [jnp.exp(m_old[n] - m_new[n]) for n in idx]
    pv = [_dot(p[n].astype(BF16), values[n]) for n in idx]
    for n in idx:
        l_scr[n] = alpha[n] * l_scr[n] + jnp.sum(p[n], axis=-1, keepdims=True)
        acc_scr[n] = alpha[n] * acc_scr[n] + pv[n]
        m_scr[n] = m_new[n]


def _diff_finalize(o1, o2, lam_refs, g_ref, lam_init):
    lam = _lambda_value(lam_refs, lam_init)
    return _rms(o1 - lam * o2, g_ref[...]) * (1.0 - lam_init)


def _diff_init(m_scr, l_scr, acc_scr):
    m_scr[...] = jnp.full(m_scr.shape, -jnp.inf, F32)
    l_scr[...] = jnp.zeros(l_scr.shape, F32)
    acc_scr[...] = jnp.zeros(acc_scr.shape, F32)


def _near_bias(slope, r, q_pos, k_pos):
    visible = jnp.right_shift(k_pos, CHUNK_SHIFT) <= jnp.right_shift(q_pos, CHUNK_SHIFT)
    return jnp.where(visible, slope * (r - jnp.abs(q_pos - k_pos)).astype(F32), -jnp.inf)


def _split_distance(slope, d):
    return (-(slope * CHUNK) * jnp.right_shift(d, CHUNK_SHIFT).astype(F32),
            -slope * jnp.bitwise_and(d, CHUNK - 1).astype(F32))


def _diff_prompt_kernel(i_tab, j_tab, slopes_ref, q_ref, k_ref, v_ref, lq1, lk1, lq2, lk2, g_ref, o_ref,
                        m_scr, l_scr, acc_scr, lhs_scr, s_even, s_odd, *, tq, lam_init):
    h = pl.program_id(1)
    s_id = pl.program_id(2)
    n_pairs = pl.num_programs(2) - 1
    score_on = s_id < n_pairs
    consume_on = s_id >= 1
    i = i_tab[s_id]
    j = j_tab[s_id]
    ci = i_tab[jnp.maximum(s_id - 1, 0)]
    cj = j_tab[jnp.maximum(s_id - 1, 0)]
    slope = slopes_ref[h]
    half = DIFF_HEAD_DIM
    lane = lax.broadcasted_iota(jnp.int32, (tq, LANES), 1)

    @pl.when(jnp.logical_and(consume_on, cj == 0))
    def _():
        _diff_init(m_scr, l_scr, acc_scr)

    @pl.when(jnp.logical_and(score_on, j == 0))
    def _():
        qs = q_ref[...] * (DIFF_HEAD_DIM ** -0.5)
        lhs_scr[0] = jnp.where(lane < half, qs, 1.0).astype(BF16)
        lhs_scr[1] = jnp.where(lane >= half, qs, 1.0).astype(BF16)

    sub = min(DIFF_SUB, tq)
    units = [(n, qb) for qb in range(tq // sub) for n in range(2)]

    def score_tile(near, dst):
        kb = k_ref[...]
        if near:
            aug0 = aug1 = jnp.zeros_like(kb)
        else:
            d = (i - j) * tq - lax.broadcasted_iota(jnp.int32, (tq, LANES), 0)
            t_hi, t_lo = _split_distance(slope, d)
            aug0 = jnp.where(lane == half, t_hi, jnp.where(lane == half + 1, t_lo, 0.0))
            aug1 = jnp.where(lane == 0, t_hi, jnp.where(lane == 1, t_lo, 0.0))
        keys = [jnp.where(lane < half, kb, aug0).astype(BF16), jnp.where(lane >= half, kb, aug1).astype(BF16)]
        bias = None
        for n, qb in units:
            cols = slice(qb * sub, (qb + 1) * sub)
            if near:
                live = (qb + 1) * sub
                if n == 0:
                    k_pos = lax.broadcasted_iota(jnp.int32, (live, sub), 0)
                    q_pos = qb * sub + lax.broadcasted_iota(jnp.int32, (live, sub), 1)
                    bias = _near_bias(slope, q_pos, q_pos, k_pos)
                dst[n, :live, cols] = _dot_nt(keys[n][:live], lhs_scr[n, cols, :]) + bias
                if live < tq:
                    dst[n, live:, cols] = jnp.full((tq - live, sub), -jnp.inf, F32)
            else:
                dst[n, :, cols] = _dot_nt(keys[n], lhs_scr[n, cols, :])

    def consume_tile(src):
        vt = v_ref[...].T.astype(BF16)
        idx = range(len(units))
        cols = [slice(qb * sub, (qb + 1) * sub) for _, qb in units]
        s = [src[n, :, cols[u]] for u, (n, _) in enumerate(units)]
        m_old = [m_scr[n, :, cols[u]] for u, (n, _) in enumerate(units)]
        m_new = [jnp.maximum(m_old[u], jnp.max(s[u], axis=0, keepdims=True)) for u in idx]
        p = [jnp.exp(s[u] - m_new[u]) for u in idx]
        alpha = [jnp.exp(m_old[u] - m_new[u]) for u in idx]
        pv = [_dot(vt, p[u].astype(BF16)) for u in idx]
        for u, (n, _) in enumerate(units):
            l_scr[n, :, cols[u]] = alpha[u] * l_scr[n, :, cols[u]] + jnp.sum(p[u], axis=0, keepdims=True)
            acc_scr[n, :, cols[u]] = alpha[u] * acc_scr[n, :, cols[u]] + pv[u]
            m_scr[n, :, cols[u]] = m_new[u]

    def stage(cond, near, score, consume):
        for parity, (dst, src) in enumerate(((s_even, s_odd), (s_odd, s_even))):
            @pl.when(jnp.logical_and(cond, s_id % 2 == parity))
            def _(dst=dst, src=src):
                if score:
                    score_tile(near, dst)
                if consume:
                    consume_tile(src)

    both = jnp.logical_and(score_on, consume_on)
    stage(jnp.logical_and(both, j < i), False, True, True)
    stage(jnp.logical_and(both, j == i), True, True, True)
    stage(jnp.logical_not(consume_on), True, True, False)
    stage(jnp.logical_not(score_on), False, False, True)

    @pl.when(jnp.logical_and(consume_on, cj == ci))
    def _():
        o_ref[...] = _diff_finalize((acc_scr[0] / l_scr[0]).T, (acc_scr[1] / l_scr[1]).T,
                                    (lq1, lk1, lq2, lk2), g_ref, lam_init).astype(o_ref.dtype)


def _alibi_slopes():
    return jnp.asarray([2.0 ** (-8.0 * (i + 1) / DIFF_HEADS) for i in range(DIFF_HEADS)], F32)


def _diff_prompt(main, k_new, v_new, lams, subln_g, *, bsz, t, lam_init):
    tq = min(DIFF_TQ, t)
    nq = t // tq
    assert tq % CHUNK == 0 and t <= CHUNK * 255
    kern = functools.partial(_diff_prompt_kernel, tq=tq, lam_init=lam_init)
    qc, kc, vc = COL_DQ_MAIN // LANES, 0, 0
    pairs = [(i, j) for i in range(nq) for j in range(i + 1)]
    pairs.append(pairs[-1])
    i_tab = jnp.asarray([p[0] for p in pairs], jnp.int32)
    j_tab = jnp.asarray([p[1] for p in pairs], jnp.int32)
    zero = lambda b, h, s, it, jt: (0, 0)

    def consumed(tab, s):
        return tab[jnp.maximum(s - 1, 0)]

    grid_spec = pltpu.PrefetchScalarGridSpec(
        num_scalar_prefetch=2,
        grid=(bsz, DIFF_HEADS, len(pairs)),
        in_specs=[
            pl.BlockSpec(memory_space=pltpu.SMEM),
            pl.BlockSpec((tq, LANES), lambda b, h, s, it, jt: (b * nq + it[s], qc + h)),
            pl.BlockSpec((tq, LANES), lambda b, h, s, it, jt: (b * nq + jt[s], kc + h)),
            pl.BlockSpec((tq, LANES), lambda b, h, s, it, jt: (b * nq + consumed(jt, s), vc + h)),
        ] + [pl.BlockSpec((1, DIFF_HEAD_DIM), zero)] * 4 + [pl.BlockSpec((1, HEAD_DIM), zero)],
        out_specs=pl.BlockSpec((tq, LANES), lambda b, h, s, it, jt: (b * nq + consumed(it, s), h)),
        scratch_shapes=[pltpu.VMEM((2, 1, tq), F32), pltpu.VMEM((2, 1, tq), F32),
                        pltpu.VMEM((2, HEAD_DIM, tq), F32), pltpu.VMEM((2, tq, LANES), BF16),
                        pltpu.VMEM((2, tq, tq), F32), pltpu.VMEM((2, tq, tq), F32)],
    )
    return pl.pallas_call(
        kern,
        out_shape=jax.ShapeDtypeStruct((bsz * t, DIFF_HEADS * HEAD_DIM), BF16),
        grid_spec=grid_spec,
        compiler_params=_cparams(("parallel", "parallel", "arbitrary")),
        name="diff_attn_prompt",
    )(i_tab, j_tab, _alibi_slopes(), main, k_new, v_new, *lams, subln_g.reshape(1, -1))


def _diff_sample_kernel(slopes_ref, q_ref, kn_ref, vn_ref, kp_ref, vp_ref, lq1, lk1, lq2, lk2, g_ref, o_ref,
                        m_scr, l_scr, acc_scr, lhs_scr, *, t, tk, past, lam_init):
    j = pl.program_id(1)
    heads = range(DIFF_HEADS)

    def cols(h):
        return slice(h * HEAD_DIM, (h + 1) * HEAD_DIM)

    @pl.when(j == 0)
    def _():
        _diff_init(m_scr, l_scr, acc_scr)
        lane = lax.broadcasted_iota(jnp.int32, (t, LANES), 1)
        r = lax.broadcasted_iota(jnp.int32, (t, t), 0)
        c = lax.broadcasted_iota(jnp.int32, (t, t), 1)
        scores = []
        for h in heads:
            qs = q_ref[:, cols(h)] * (DIFF_HEAD_DIM ** -0.5)
            lhs_scr[h, 0:t, :] = jnp.where(lane < DIFF_HEAD_DIM, qs, 0.0).astype(BF16)
            lhs_scr[h, t:2 * t, :] = jnp.where(lane >= DIFF_HEAD_DIM, qs, 0.0).astype(BF16)
            bias = _near_bias(slopes_ref[h], r, past + r, past + c)
            scores.append(_dot_nt(lhs_scr[h], kn_ref[:, cols(h)].astype(BF16)) + jnp.concatenate([bias, bias], axis=0))
        _softmax_update(m_scr, l_scr, acc_scr, scores, [vn_ref[:, cols(h)].astype(BF16) for h in heads])

    d = (past - j * tk - lax.broadcasted_iota(jnp.int32, (1, tk), 1)).astype(F32)
    kt = jnp.swapaxes(kp_ref[...], 0, 1).astype(BF16)
    vt = jnp.swapaxes(vp_ref[...], 0, 1).astype(BF16)
    scores = [_dot_nt(lhs_scr[h], kt[h]) - slopes_ref[h] * d for h in heads]
    _softmax_update(m_scr, l_scr, acc_scr, scores, [vt[h] for h in heads])

    @pl.when(j == pl.num_programs(1) - 1)
    def _():
        for h in heads:
            o = acc_scr[h] / l_scr[h]
            o_ref[:, cols(h)] = _diff_finalize(o[:t], o[t:], (lq1, lk1, lq2, lk2), g_ref, lam_init).astype(o_ref.dtype)


def _diff_sample(main, k_new, v_new, k_past, v_past, lams, subln_g, *, bsz, t, lam_init):
    past = k_past.shape[1]
    tk = min(1024, past)
    nk = past // tk
    width = DIFF_HEADS * HEAD_DIM
    kern = functools.partial(_diff_sample_kernel, t=t, tk=tk, past=past, lam_init=lam_init)
    zero = lambda b, j: (0, 0)
    cache = pl.BlockSpec((None, tk, DIFF_HEADS, HEAD_DIM), lambda b, j: (b, j, 0, 0))
    return pl.pallas_call(
        kern,
        out_shape=jax.ShapeDtypeStruct((bsz * t, width), BF16),
        grid=(bsz, nk),
        in_specs=[
            pl.BlockSpec(memory_space=pltpu.SMEM),
            pl.BlockSpec((t, width), lambda b, j: (b, COL_DQ_MAIN // width)),
            pl.BlockSpec((t, width), lambda b, j: (b, 0)),
            pl.BlockSpec((t, width), lambda b, j: (b, 0)),
            cache, cache,
        ] + [pl.BlockSpec((1, DIFF_HEAD_DIM), zero)] * 4 + [pl.BlockSpec((1, HEAD_DIM), zero)],
        out_specs=pl.BlockSpec((t, width), lambda b, j: (b, 0)),
        scratch_shapes=[pltpu.VMEM((DIFF_HEADS, 2 * t, 1), F32), pltpu.VMEM((DIFF_HEADS, 2 * t, 1), F32),
                        pltpu.VMEM((DIFF_HEADS, 2 * t, HEAD_DIM), F32), pltpu.VMEM((DIFF_HEADS, 2 * t, LANES), BF16)],
        compiler_params=_cparams(("parallel", "arbitrary")),
        name="diff_attn_sample",
    )(_alibi_slopes(), main, k_new, v_new, k_past, v_past, *lams, subln_g.reshape(1, -1))


def _merge_kernel(oa_ref, ob_ref, wa_ref, wb_ref, ga_ref, gb_ref, o_ref):
    ya = _dot(oa_ref[...], wa_ref[...])
    yb = _dot(ob_ref[...], wb_ref[...])
    o_ref[...] = (jax.nn.sigmoid(ga_ref[...]) * ya + jax.nn.sigmoid(gb_ref[...]) * yb).astype(o_ref.dtype)


def _merge(o_a, o_b, wa, wb, tail, *, tm):
    m = o_a.shape[0]
    d = wa.shape[1]
    tn = 512
    ga0, gb0 = 0, COL_GB_TAIL // tn
    return pl.pallas_call(
        _merge_kernel,
        out_shape=jax.ShapeDtypeStruct((m, d), BF16),
        grid=(m // tm, d // tn),
        in_specs=[
            pl.BlockSpec((tm, o_a.shape[1]), lambda i, j: (i, 0)),
            pl.BlockSpec((tm, o_b.shape[1]), lambda i, j: (i, 0)),
            pl.BlockSpec((wa.shape[0], tn), lambda i, j: (0, j)),
            pl.BlockSpec((wb.shape[0], tn), lambda i, j: (0, j)),
            pl.BlockSpec((tm, tn), lambda i, j: (i, ga0 + j)),
            pl.BlockSpec((tm, tn), lambda i, j: (i, gb0 + j)),
        ],
        out_specs=pl.BlockSpec((tm, tn), lambda i, j: (i, j)),
        compiler_params=_cparams(("parallel", "arbitrary")),
        name="merge",
    )(o_a, o_b, wa, wb, tail, tail)


def _cross_kernel(q_ref, k_ref, v_ref, o_ref):
    dh = q_ref.shape[-1] // MEM_HEADS
    for h in range(MEM_HEADS):
        sl = slice(h * dh, (h + 1) * dh)
        s = _dot_nt(q_ref[:, sl], k_ref[0, :, sl].astype(BF16)) * (dh ** -0.5)
        s = s - jnp.max(s, axis=-1, keepdims=True)
        p = jnp.exp(s)
        p = p / jnp.sum(p, axis=-1, keepdims=True)
        o_ref[:, sl] = _dot(p.astype(BF16), v_ref[0, :, sl].astype(BF16)).astype(o_ref.dtype)


def _cross_attend(q, mem_k, mem_v, *, bsz, t, tm):
    d = q.shape[1]
    nt = t // tm
    mt = mem_k.shape[1]
    return pl.pallas_call(
        _cross_kernel,
        out_shape=jax.ShapeDtypeStruct(q.shape, BF16),
        grid=(bsz, nt),
        in_specs=[
            pl.BlockSpec((tm, d), lambda b, i: (b * nt + i, 0)),
            pl.BlockSpec((1, mt, d), lambda b, i: (b, 0, 0)),
            pl.BlockSpec((1, mt, d), lambda b, i: (b, 0, 0)),
        ],
        out_specs=pl.BlockSpec((tm, d), lambda b, i: (b * nt + i, 0)),
        compiler_params=_cparams(("parallel", "arbitrary")),
        name="cross_attn",
    )(q, mem_k, mem_v)


def _topk_rows(work, cidx, k):
    n, tn = work.shape
    rid = lax.broadcasted_iota(jnp.int32, (n, tn), 0).astype(F32)
    kid = lax.broadcasted_iota(jnp.int32, (k, tn), 0)
    vals = jnp.zeros((k, tn), F32)
    idxs = jnp.zeros((k, tn), F32)
    for t in range(k):
        m = jnp.max(work, axis=0, keepdims=True)
        pos = jnp.min(jnp.where(work == m, rid, float(n)), axis=0, keepdims=True)
        hit = rid == pos
        if cidx is None:
            picked = pos
        else:
            picked = jnp.sum(jnp.where(hit, cidx, 0.0), axis=0, keepdims=True)
        vals = jnp.where(kid == t, m, vals)
        idxs = jnp.where(kid == t, picked, idxs)
        work = jnp.where(hit, -jnp.inf, work)
    return vals, idxs


def _product_candidates(v1, i1, v2, i2, k):
    assert k == 2 * SUBLANES
    sub = lax.broadcasted_iota(jnp.int32, (SUBLANES, v1.shape[1]), 0)
    vals, idxs = [], []
    for a in range(k // 2):
        nb = k // (a + 1)
        width = k if nb > SUBLANES else SUBLANES
        cv = v1[a:a + 1, :] + v2[:width, :]
        ci = i1[a:a + 1, :] * PEER_N_KEYS + i2[:width, :]
        if nb < SUBLANES:
            cv = jnp.where(sub < nb, cv, -jnp.inf)
        vals.append(cv)
        idxs.append(ci)
    vals.append(v1[k // 2:, :] + v2[0:1, :])
    idxs.append(i1[k // 2:, :] * PEER_N_KEYS + i2[0:1, :])
    return jnp.concatenate(vals, axis=0), jnp.concatenate(idxs, axis=0)


def _peer_route_kernel(q_ref, keys_hi_ref, keys_lo_ref, e_ref, g_ref):
    k = PEER_TOPK
    for h in range(PEER_HEADS):
        sub = []
        for p in range(2):
            c0 = (h * 2 + p) * PEER_N_KEYS
            q_parts = _split_bf16(q_ref[:, c0:c0 + PEER_N_KEYS])
            s = _dot3_parts((keys_hi_ref[h, p], keys_lo_ref[h, p]), q_parts, _dot_nt)
            sub.append(_topk_rows(s, None, k))
        (v1, i1), (v2, i2) = sub
        cand, cidx = _product_candidates(v1, i1, v2, i2, k)
        best, eidx = _topk_rows(cand, cidx, k)
        ex = jnp.exp(best - jnp.max(best, axis=0, keepdims=True))
        gate = ex / jnp.sum(ex, axis=0, keepdims=True)
        e_ref[h * k:(h + 1) * k, :] = eidx.astype(jnp.int32)
        g_ref[h * k:(h + 1) * k, :] = gate


HALF_KEYS = PEER_N_KEYS // 2
PEER_GATE_TN = 256
BF16_HIGH_MASK = 0xFFFF0000


def _peer_gate_kernel(et_ref, gt_ref, o_ref, e_scr, g_scr):
    tg = et_ref.shape[1]
    e_scr[...] = et_ref[...].T
    g_scr[...] = gt_ref[...].T
    kid = lax.broadcasted_iota(jnp.int32, (PEER_N_KEYS, PEER_N_KEYS), 0)

    def body(n8, carry):
        rows = pl.ds(pl.multiple_of(n8 * SUBLANES, SUBLANES), SUBLANES)
        e8 = e_scr[rows, :]
        g8 = g_scr[rows, :]
        a8 = jnp.right_shift(e8, 7)
        b8 = jnp.bitwise_and(e8, PEER_N_KEYS - 1)
        toks = range(SUBLANES)
        at = [jnp.where(a8[s:s + 1, :] == kid, g8[s:s + 1, :], 0.0).astype(BF16) for s in toks]
        bt = [jnp.where(b8[s:s + 1, :] == kid, 1.0, 0.0).astype(BF16) for s in toks]
        grids = [_dot_nt(at[s], bt[s]) for s in toks]
        bits = [pltpu.bitcast(grids[s].astype(BF16).astype(F32), jnp.uint32) for s in toks]
        words = [jnp.bitwise_or(jnp.bitwise_and(bits[s][HALF_KEYS:, :], jnp.uint32(BF16_HIGH_MASK)),
                                jnp.right_shift(bits[s][:HALF_KEYS, :], jnp.uint32(16))) for s in toks]
        o_ref[:, rows, :] = jnp.swapaxes(jnp.stack(words, axis=0), 0, 1)
        return carry

    lax.fori_loop(0, tg // SUBLANES, body, 0, unroll=4)


def _peer_route(qp, sub_keys, *, tn):
    n = qp.shape[0]
    rows = PEER_HEADS * PEER_TOPK
    keys_hi, keys_lo = _split_bf16(sub_keys.astype(F32))
    keys_spec = pl.BlockSpec(sub_keys.shape, lambda i: (0, 0, 0, 0))
    e_t, g_t = pl.pallas_call(
        _peer_route_kernel,
        out_shape=(jax.ShapeDtypeStruct((rows, n), jnp.int32), jax.ShapeDtypeStruct((rows, n), F32)),
        grid=(n // tn,),
        in_specs=[pl.BlockSpec((tn, qp.shape[1]), lambda i: (i, 0)), keys_spec, keys_spec],
        out_specs=(pl.BlockSpec((rows, tn), lambda i: (0, i)), pl.BlockSpec((rows, tn), lambda i: (0, i))),
        compiler_params=_cparams(("parallel",)),
        name="peer_route",
    )(qp, keys_hi, keys_lo)
    tg = min(PEER_GATE_TN, n)
    return pl.pallas_call(
        _peer_gate_kernel,
        out_shape=jax.ShapeDtypeStruct((HALF_KEYS, n, PEER_N_KEYS), jnp.uint32),
        grid=(n // tg,),
        in_specs=[pl.BlockSpec((rows, tg), lambda i: (0, i)), pl.BlockSpec((rows, tg), lambda i: (0, i))],
        out_specs=pl.BlockSpec((HALF_KEYS, tg, PEER_N_KEYS), lambda i: (0, i, 0)),
        scratch_shapes=[pltpu.VMEM((tg, rows), jnp.int32), pltpu.VMEM((tg, rows), F32)],
        compiler_params=_cparams(("parallel",)),
        name="peer_gates",
    )(e_t, g_t)


def _gelu_tanh(x):
    return 0.5 * x * (1.0 + jnp.tanh(math.sqrt(2.0 / math.pi) * (x + 0.044715 * (x * x * x))))


PEER_BLOCKS_PER_STEP = 4
PEER_TN = 1024
PEER_VMEM_LIMIT = 60 * 1024 * 1024


def _peer_expert_kernel(x_ref, gn_ref, gpa_ref, gpb_ref, *rest):
    nblk = PEER_BLOCKS_PER_STEP
    u_refs, v_refs = rest[:nblk], rest[nblk:2 * nblk]
    gf_ref, o_ref, h_scr, acc_scr, coef_scr = rest[2 * nblk:]
    s = pl.program_id(1)
    last = pl.num_programs(1) - 1

    @pl.when(s == 0)
    def _():
        h_scr[...] = _rms(x_ref[...], gn_ref[...]).astype(BF16)
        acc_scr[...] = jnp.zeros(acc_scr.shape, F32)
        coef_scr[...] = jnp.zeros(coef_scr.shape, BF16)

    v_cat = jnp.concatenate([v[...].astype(BF16) for v in v_refs], axis=0)
    acc_scr[...] += _dot(coef_scr[(s + 1) % 2], v_cat)

    def unpack(word):
        return (pltpu.bitcast(jnp.left_shift(word, jnp.uint32(16)), F32),
                pltpu.bitcast(jnp.bitwise_and(word, jnp.uint32(BF16_HIGH_MASK)), F32))

    a_lo, a_hi = unpack(gpa_ref[...])
    b_lo, b_hi = unpack(gpb_ref[...])
    u_cat = jnp.concatenate([u[...].astype(BF16) for u in u_refs], axis=0)
    hid = _gelu_tanh(_dot_nt(h_scr[...], u_cat))
    coef_scr[s % 2] = (jnp.concatenate([a_lo, b_lo, a_hi, b_hi], axis=1) * hid).astype(BF16)

    @pl.when(s == last)
    def _():
        o_ref[...] = _rms(x_ref[...] + acc_scr[...], gf_ref[...])


def _peer_experts(x, norm_g, gates, u_tab, v_tab, final_g, *, tn):
    n, d = x.shape
    eb = PEER_N_KEYS
    assert u_tab.shape[0] == eb * eb
    nblk = PEER_BLOCKS_PER_STEP
    ns = eb // nblk

    def cur(s):
        return jnp.minimum(s, ns - 1)

    def prev(s):
        return jnp.maximum(s - 1, 0)

    def table(which, q):
        return pl.BlockSpec((eb, d), lambda i, s: (which(s) + q * ns, 0))

    return pl.pallas_call(
        _peer_expert_kernel,
        out_shape=jax.ShapeDtypeStruct((n, d), F32),
        grid=(n // tn, ns + 1),
        in_specs=[
            pl.BlockSpec((tn, d), lambda i, s: (i, 0), pipeline_mode=pl.Buffered(1)),
            pl.BlockSpec((1, d), lambda i, s: (0, 0)),
            pl.BlockSpec((None, tn, eb), lambda i, s: (cur(s), i, 0)),
            pl.BlockSpec((None, tn, eb), lambda i, s: (cur(s) + ns, i, 0)),
        ] + [table(cur, q) for q in range(nblk)] + [table(prev, q) for q in range(nblk)] + [
            pl.BlockSpec((1, d), lambda i, s: (0, 0)),
        ],
        out_specs=pl.BlockSpec((tn, d), lambda i, s: (i, 0), pipeline_mode=pl.Buffered(1)),
        scratch_shapes=[pltpu.VMEM((tn, d), BF16), pltpu.VMEM((tn, d), F32),
                        pltpu.VMEM((2, tn, nblk * eb), BF16)],
        compiler_params=_cparams(("parallel", "arbitrary"), vmem=PEER_VMEM_LIMIT),
        name="peer_experts",
    )(x, norm_g.reshape(1, d), gates, gates, *([u_tab] * nblk), *([v_tab] * nblk), final_g.reshape(1, d))


def _layer(x, wts, layer, mem_k, mem_v, conv_prev, gdn_state, k_past, v_past, *, tm):
    bsz, t, d = x.shape
    n = bsz * t
    x2d = x.reshape(n, d)
    main, k_new, v_new, tail = _in_proj(x2d, wts["norm_mix_g"], wts["w_in"], tm=min(IN_TM, n))
    conv_dim = 3 * GDN_WIDTH
    conv_prev8 = jnp.concatenate(
        [jnp.zeros((bsz, SUBLANES - (GDN_CONV - 1), conv_dim), F32), conv_prev.astype(F32)], axis=1)
    o_a, s_new = _gdn(main, tail, conv_prev8, gdn_state.astype(F32), wts["gdn_conv_w"], wts["gdn_a_log"],
                      wts["gdn_dt_bias"], wts["gdn_norm_g"], bsz=bsz, t=t)
    conv_new = main.reshape(bsz, t, -1)[:, t - (GDN_CONV - 1):, :conv_dim]
    lam_init = 0.8 - 0.6 * math.exp(-0.3 * layer)
    lams = tuple(wts[k].reshape(1, -1) for k in ("diff_lambda_q1", "diff_lambda_k1", "diff_lambda_q2", "diff_lambda_k2"))
    if k_past is None:
        o_b = _diff_prompt(main, k_new, v_new, lams, wts["diff_subln_g"], bsz=bsz, t=t, lam_init=lam_init)
    else:
        o_b = _diff_sample(main, k_new, v_new, k_past, v_past, lams, wts["diff_subln_g"], bsz=bsz, t=t,
                           lam_init=lam_init)
    k_rows = k_new.reshape(bsz, t, DIFF_HEADS, HEAD_DIM)
    v_rows = v_new.reshape(bsz, t, DIFF_HEADS, HEAD_DIM)
    merged = _merge(o_a, o_b, wts["w_branch_a"], wts["w_branch_b"], tail, tm=tm)
    x1 = _matmul(merged, wts["w_out"], residual=x2d, tm=tm, tn=MM_TN, name="out_proj")
    qm = _matmul(x1, wts["w_mq"], norm_g=wts["norm_cross_g"], tm=tm, tn=MM_TN, out_dtype=BF16, name="mem_q")
    mt = mem_k.shape[1]
    oc = _cross_attend(qm, mem_k.reshape(bsz, mt, d), mem_v.reshape(bsz, mt, d), bsz=bsz, t=t, tm=min(tm, t))
    x2 = _matmul(oc, wts["w_mo"], residual=x1, tm=tm, tn=MM_TN, name="mem_o")
    qp = _matmul(x2, wts["peer_w_q"], norm_g=wts["norm_ffn_g"], tm=tm, tn=MM_TN, name="peer_q")
    tn = min(PEER_TN, n)
    gates = _peer_route(qp, wts["peer_sub_keys"], tn=LANES)
    y = _peer_experts(x2, wts["norm_ffn_g"], gates, wts["peer_u"], wts["peer_v"], wts["final_norm_g"], tn=tn)
    return y.reshape(bsz, t, d), k_rows, v_rows, s_new, conv_new


def kernel(x_prompt, x_sample, cache_diff_k, cache_diff_v, state_gdn, state_conv, cache_mem_k, cache_mem_v,
           mem_prompt, norm_mix_g, w_in, gdn_conv_w, gdn_a_log, gdn_dt_bias, gdn_norm_g,
           diff_lambda_q1, diff_lambda_k1, diff_lambda_q2, diff_lambda_k2, diff_subln_g,
           w_branch_a, w_branch_b, w_out, norm_cross_g, norm_mem_g, w_mq, w_mk, w_mv, w_mo,
           norm_ffn_g, peer_w_q, peer_sub_keys, peer_u, peer_v, final_norm_g):
    depth = w_in.shape[0]
    assert depth == 1, "final norm is fused into the last layer's PEER kernel"
    l = 0
    wts = {
        "norm_mix_g": norm_mix_g[l], "w_in": _in_proj_weights(w_in[l]), "gdn_conv_w": gdn_conv_w[l],
        "gdn_a_log": gdn_a_log[l], "gdn_dt_bias": gdn_dt_bias[l], "gdn_norm_g": gdn_norm_g[l],
        "diff_lambda_q1": diff_lambda_q1[l], "diff_lambda_k1": diff_lambda_k1[l],
        "diff_lambda_q2": diff_lambda_q2[l], "diff_lambda_k2": diff_lambda_k2[l],
        "diff_subln_g": diff_subln_g[l], "w_branch_a": w_branch_a[l].astype(BF16),
        "w_branch_b": w_branch_b[l].astype(BF16), "w_out": w_out[l].astype(BF16),
        "norm_cross_g": norm_cross_g[l], "w_mq": w_mq[l].astype(BF16), "w_mo": w_mo[l].astype(BF16),
        "norm_ffn_g": norm_ffn_g[l], "peer_w_q": peer_w_q[l].astype(BF16), "peer_sub_keys": peer_sub_keys[l],
        "peer_u": peer_u[l], "peer_v": peer_v[l], "final_norm_g": final_norm_g,
    }
    bp, tp, d = x_prompt.shape
    bs, ts, _ = x_sample.shape
    mem2d = mem_prompt.reshape(-1, d)
    tmm = min(512, mem2d.shape[0])
    mem_k = _matmul(mem2d, w_mk[l], norm_g=norm_mem_g[l], tm=tmm, tn=512, name="mem_k")
    mem_v = _matmul(mem2d, w_mv[l], norm_g=norm_mem_g[l], tm=tmm, tn=512, name="mem_v")
    mshape = mem_prompt.shape[:2] + (MEM_HEADS, d // MEM_HEADS)
    mem_k = mem_k.reshape(mshape)
    mem_v = mem_v.reshape(mshape)
    conv0 = jnp.zeros((bp, GDN_CONV - 1, 3 * GDN_WIDTH), F32)
    s0 = jnp.zeros((bp, GDN_HEADS, HEAD_DIM, HEAD_DIM), F32)
    yp, pk, pv, ps, pc = _layer(x_prompt, wts, l, mem_k, mem_v, conv0, s0, None, None, tm=min(1024, bp * tp))
    ys, sk, sv, ss, sc = _layer(x_sample, wts, l, cache_mem_k[l], cache_mem_v[l], state_conv[l], state_gdn[l],
                                cache_diff_k[l], cache_diff_v[l], tm=min(256, bs * ts))
    return (yp, ys, pk[None], pv[None], ps[None], pc[None], mem_k[None], mem_v[None],
            sk[None], sv[None], ss[None], sc[None])
```

```python
import functools
import math

import jax
import jax.numpy as jnp
from jax import lax
from jax.experimental import pallas as pl
from jax.experimental.pallas import tpu as pltpu

F32 = jnp.float32
BF16 = jnp.bfloat16
HIGHEST = lax.Precision.HIGHEST

RMS_EPS = 1e-6
CHUNK = 64
CHUNK_SHIFT = 6
DIFF_TQ = 1024
GDN_CHUNKS_PER_STEP = 4
DIFF_SUB = 256
GDN_HEADS = 8
HEAD_DIM = 128
GDN_WIDTH = GDN_HEADS * HEAD_DIM
GDN_CONV = 4
DIFF_HEADS = 8
DIFF_HEAD_DIM = 64
MEM_HEADS = 4
PEER_HEADS = 8
PEER_N_KEYS = 128
PEER_TOPK = 16
LANES = 128
SUBLANES = 8
V7X_VMEM_LIMIT = 52 * 1024 * 1024
MM_TN = 1024


def _cparams(sem, vmem=V7X_VMEM_LIMIT):
    return pltpu.CompilerParams(dimension_semantics=sem, vmem_limit_bytes=vmem)


def _dot(a, b, precision=None):
    return jnp.dot(a, b, preferred_element_type=F32, precision=precision)


def _dot_nt(a, b, precision=None):
    return lax.dot_general(a, b, (((1,), (1,)), ((), ())), preferred_element_type=F32, precision=precision)


def _split_bf16(x):
    hi = x.astype(BF16)
    return hi, (x - hi.astype(F32)).astype(BF16)


def _cat_parts(p, q):
    return jnp.concatenate([p[0], q[0]], axis=0), jnp.concatenate([p[1], q[1]], axis=0)


def _dot3_parts(a_parts, b_parts, dot):
    a_hi, a_lo = a_parts
    b_hi, b_lo = b_parts
    m = a_hi.shape[0]
    both = dot(jnp.concatenate([a_hi, a_lo], axis=0), b_hi)
    return both[:m] + both[m:] + dot(a_hi, b_lo)


def _dot3(a, b):
    return _dot3_parts(_split_bf16(a), _split_bf16(b), _dot)


def _rms(xf, g):
    return xf * lax.rsqrt(jnp.mean(xf * xf, axis=-1, keepdims=True) + RMS_EPS) * g


def _mm_kernel(*refs, has_norm, has_res):
    it = iter(refs)
    x_ref = next(it)
    g_ref = next(it) if has_norm else None
    w_ref = next(it)
    r_ref = next(it) if has_res else None
    o_ref = next(it)
    h_scr = next(it)

    @pl.when(pl.program_id(1) == 0)
    def _():
        xf = x_ref[...].astype(F32)
        if has_norm:
            xf = _rms(xf, g_ref[...])
        h_scr[...] = xf.astype(BF16)

    acc = _dot(h_scr[...], w_ref[...].astype(BF16))
    if has_res:
        acc = acc + r_ref[...]
    o_ref[...] = acc.astype(o_ref.dtype)


def _matmul(x, w, *, norm_g=None, residual=None, tm, tn, out_dtype=F32, name="matmul"):
    m, k = x.shape
    n = w.shape[1]
    assert m % tm == 0 and n % tn == 0, (m, n, tm, tn)
    in_specs = [pl.BlockSpec((tm, k), lambda i, j: (i, 0))]
    args = [x]
    if norm_g is not None:
        in_specs.append(pl.BlockSpec((1, k), lambda i, j: (0, 0)))
        args.append(norm_g.reshape(1, k).astype(F32))
    in_specs.append(pl.BlockSpec((k, tn), lambda i, j: (0, j)))
    args.append(w)
    if residual is not None:
        in_specs.append(pl.BlockSpec((tm, tn), lambda i, j: (i, j)))
        args.append(residual)
    return pl.pallas_call(
        functools.partial(_mm_kernel, has_norm=norm_g is not None, has_res=residual is not None),
        out_shape=jax.ShapeDtypeStruct((m, n), out_dtype),
        grid=(m // tm, n // tn),
        in_specs=in_specs,
        out_specs=pl.BlockSpec((tm, tn), lambda i, j: (i, j)),
        scratch_shapes=[pltpu.VMEM((tm, k), BF16)],
        compiler_params=_cparams(("parallel", "arbitrary")),
        name=name,
    )(*args)


IN_TN = 512
IN_TM = 1024
IN_W1_TILES = 8
IN_W2_TILES = 14
IN_MAIN_TILES = 10
IN_K_TILE0 = 10
IN_V_TILE0 = 12
IN_TAIL_TILE0 = 14
IN_TILES = IN_W1_TILES + IN_W2_TILES + 1
MAIN_COLS = IN_MAIN_TILES * IN_TN
TAIL_COLS = (IN_TILES - IN_TAIL_TILE0) * IN_TN
COL_DQ_MAIN = 4096
COL_GB_TAIL = 2048
COL_BA_TAIL = 4096


def _in_proj_kernel(x_ref, g_ref, w1_ref, w2_ref, w3_ref, main_ref, k_ref, v_ref, tail_ref, h_scr):
    j = pl.program_id(1)

    @pl.when(j == 0)
    def _():
        h_scr[...] = _rms(x_ref[...], g_ref[...]).astype(BF16)

    routes = ((0, IN_W1_TILES, w1_ref, main_ref),
              (IN_W1_TILES, IN_K_TILE0, w2_ref, main_ref),
              (IN_K_TILE0, IN_V_TILE0, w2_ref, k_ref),
              (IN_V_TILE0, IN_TAIL_TILE0, w2_ref, v_ref),
              (IN_TAIL_TILE0, IN_TILES - 1, w2_ref, tail_ref),
              (IN_TILES - 1, IN_TILES, w3_ref, tail_ref))
    for lo, hi, w_ref, dst in routes:
        @pl.when(jnp.logical_and(j >= lo, j < hi))
        def _(w_ref=w_ref, dst=dst):
            dst[...] = _dot(h_scr[...], w_ref[...])


def _in_proj_weights(w_in):
    c_z = 4 * GDN_WIDTH
    c_ba = c_z + 2 * GDN_HEADS
    assert c_z == IN_W1_TILES * IN_TN and w_in.shape[1] - c_ba == IN_W2_TILES * IN_TN
    w_bf16 = w_in.astype(BF16)
    return w_bf16, w_bf16[:, c_ba:]


def _in_proj(x, norm_g, w_pieces, *, tm):
    m, d = x.shape
    w_all, w2 = w_pieces
    w1 = w3 = w_all
    tn = IN_TN
    out = lambda cols: jax.ShapeDtypeStruct((m, cols), F32)
    return pl.pallas_call(
        _in_proj_kernel,
        out_shape=(out(MAIN_COLS), out(2 * IN_TN), out(2 * IN_TN), out(TAIL_COLS)),
        grid=(m // tm, IN_TILES),
        in_specs=[
            pl.BlockSpec((tm, d), lambda i, j: (i, 0)),
            pl.BlockSpec((1, d), lambda i, j: (0, 0)),
            pl.BlockSpec((d, tn), lambda i, j: (0, jnp.minimum(j, IN_W1_TILES - 1))),
            pl.BlockSpec((d, tn), lambda i, j: (0, jnp.clip(j - IN_W1_TILES, 0, IN_W2_TILES - 1))),
            pl.BlockSpec((d, tn), lambda i, j: (0, IN_W1_TILES)),
        ],
        out_specs=(
            pl.BlockSpec((tm, tn), lambda i, j: (i, jnp.minimum(j, IN_MAIN_TILES - 1))),
            pl.BlockSpec((tm, tn), lambda i, j: (i, jnp.clip(j - IN_K_TILE0, 0, 1))),
            pl.BlockSpec((tm, tn), lambda i, j: (i, jnp.clip(j - IN_V_TILE0, 0, 1))),
            pl.BlockSpec((tm, tn), lambda i, j: (i, jnp.clip(j - IN_TAIL_TILE0, 0, IN_TILES - IN_TAIL_TILE0 - 1))),
        ),
        scratch_shapes=[pltpu.VMEM((tm, d), BF16)],
        compiler_params=_cparams(("parallel", "arbitrary")),
        name="in_proj",
    )(x, norm_g.reshape(1, d).astype(F32), w1, w2, w3)


def _gdn_kernel(qkv_ref, z_ref, ba_ref, prev_ref, s0_ref, convw_ref, alog_ref, dtb_ref, ng_ref,
                o_ref, sfin_ref, s_scr, xbuf, *, L, nch):
    step = pl.program_id(1)
    rows = nch * L

    @pl.when(step == 0)
    def _():
        s_scr[...] = s0_ref[0]
        xbuf[0:SUBLANES, :] = prev_ref[0]

    xbuf[SUBLANES:SUBLANES + rows, :] = qkv_ref[...]
    base = SUBLANES - (GDN_CONV - 1)
    y = xbuf[base:base + rows, :] * convw_ref[0:1, :]
    for i in range(1, GDN_CONV):
        y = y + xbuf[base + i:base + i + rows, :] * convw_ref[i:i + 1, :]
    y = y * jax.nn.sigmoid(y)
    carry = xbuf[rows:rows + SUBLANES, :]
    xbuf[0:SUBLANES, :] = carry

    row = lax.broadcasted_iota(jnp.int32, (L, L), 0)
    col = lax.broadcasted_iota(jnp.int32, (L, L), 1)
    lower = row >= col
    strict = row > col
    eye = (row == col).astype(F32)
    log_l = int(math.log2(L))
    n_double = log_l - 1
    assert 2 ** log_l == L

    ba = ba_ref[...]
    beta_all = jax.nn.sigmoid(ba)
    a_in = ba + dtb_ref[...]
    softplus = jnp.maximum(a_in, 0.0) + jnp.log(1.0 + jnp.exp(-jnp.abs(a_in)))
    g_all = -jnp.exp(alog_ref[...]) * softplus
    rr = lax.broadcasted_iota(jnp.int32, (rows, rows), 0)
    cc = lax.broadcasted_iota(jnp.int32, (rows, rows), 1)
    same_chunk_lower = jnp.logical_and(rr >= cc, jnp.right_shift(rr, log_l) == jnp.right_shift(cc, log_l))
    g_cum = _dot(same_chunk_lower.astype(F32), g_all, HIGHEST)
    g_cum_t = g_cum.T

    heads = range(GDN_HEADS)
    chains = [(c, h) for c in range(nch) for h in heads]
    ids = range(len(chains))

    def tok(c):
        return slice(c * L, (c + 1) * L)

    def head_cols(part, c, h):
        return y[tok(c), part * GDN_WIDTH + h * HEAD_DIM:part * GDN_WIDTH + (h + 1) * HEAD_DIM]

    q = [head_cols(0, c, h) for c, h in chains]
    k = [head_cols(1, c, h) for c, h in chains]
    v = [head_cols(2, c, h) for c, h in chains]
    q = [a * lax.rsqrt(jnp.sum(a * a, axis=-1, keepdims=True) + 1e-6) * (HEAD_DIM ** -0.5) for a in q]
    k = [a * lax.rsqrt(jnp.sum(a * a, axis=-1, keepdims=True) + 1e-6) for a in k]
    beta = [beta_all[tok(c), h:h + 1] for c, h in chains]
    g_col = [g_cum[tok(c), GDN_HEADS + h:GDN_HEADS + h + 1] for c, h in chains]
    g_row = [g_cum_t[GDN_HEADS + h:GDN_HEADS + h + 1, tok(c)] for c, h in chains]
    decay = [jnp.where(lower, jnp.exp(jnp.where(lower, g_col[n] - g_row[n], 0.0)), 0.0) for n in ids]
    g_last = [g_col[n][L - 1:L, :] for n in ids]
    e_col = [jnp.exp(g_col[n]) for n in ids]
    qk_kk = [_dot_nt(jnp.concatenate([q[n], k[n]], axis=0).astype(BF16), k[n].astype(BF16)) for n in ids]
    x = [-jnp.where(strict, beta[n] * qk_kk[n][L:] * decay[n], 0.0) for n in ids]
    t_inv = [eye + x[n] for n in ids]
    x_parts = [_split_bf16(x[n]) for n in ids]
    xp = [_dot3_parts(x_parts[n], x_parts[n], _dot) for n in ids]
    for _ in range(n_double - 1):
        xp_parts = [_split_bf16(xp[n]) for n in ids]
        both = [_dot3_parts(_cat_parts(_split_bf16(t_inv[n]), xp_parts[n]), xp_parts[n], _dot) for n in ids]
        t_inv = [t_inv[n] + both[n][:L] for n in ids]
        xp = [both[n][L:] for n in ids]
    t_inv = [t_inv[n] + _dot3(t_inv[n], xp[n]) for n in ids]
    sol = [_dot3(t_inv[n], jnp.concatenate([v[n] * beta[n], k[n] * (beta[n] * e_col[n])], axis=-1)) for n in ids]
    qk = [jnp.where(lower, qk_kk[n][:L] * decay[n], 0.0) for n in ids]
    k_dec_t = [(k[n] * jnp.exp(g_last[n] - g_col[n])).T for n in ids]
    left_s = [jnp.concatenate([sol[n][:, HEAD_DIM:], q[n] * e_col[n]], axis=0).astype(BF16) for n in ids]
    left_v = [jnp.concatenate([qk[n], k_dec_t[n]], axis=0).astype(BF16) for n in ids]

    s = [s_scr[h] for h in heads]
    for c in range(nch):
        n0 = c * GDN_HEADS
        ws_qs = [_dot(left_s[n0 + h], s[h].astype(BF16)) for h in heads]
        v_new = [sol[n0 + h][:, :HEAD_DIM] - ws_qs[h][:L] for h in heads]
        tail = [_dot(left_v[n0 + h], v_new[h].astype(BF16)) for h in heads]
        s = [s[h] * jnp.exp(g_last[n0 + h]) + tail[h][L:] for h in heads]
        for h in heads:
            zf = z_ref[tok(c), h * HEAD_DIM:(h + 1) * HEAD_DIM]
            o = _rms(ws_qs[h][L:] + tail[h][:L], ng_ref[...]) * (zf * jax.nn.sigmoid(zf))
            o_ref[tok(c), h * HEAD_DIM:(h + 1) * HEAD_DIM] = o.astype(o_ref.dtype)
    for h in heads:
        s_scr[h] = s[h]

    @pl.when(step == pl.num_programs(1) - 1)
    def _():
        sfin_ref[0] = s_scr[...]


def _decay_lanes(p):
    return jnp.zeros((1, LANES), F32).at[0, GDN_HEADS:2 * GDN_HEADS].set(p.astype(F32))


def _gdn(main, tail, conv_prev8, s0, conv_w, a_log, dt_bias, norm_g, *, bsz, t):
    L = min(CHUNK, t)
    nch = GDN_CHUNKS_PER_STEP if (t // L) % GDN_CHUNKS_PER_STEP == 0 else 1
    nc = t // (L * nch)
    rows = L * nch
    conv_dim = 3 * GDN_WIDTH
    kern = functools.partial(_gdn_kernel, L=L, nch=nch)
    o, s_fin = pl.pallas_call(
        kern,
        out_shape=(jax.ShapeDtypeStruct((bsz * t, GDN_WIDTH), BF16),
                   jax.ShapeDtypeStruct((bsz, GDN_HEADS, HEAD_DIM, HEAD_DIM), F32)),
        grid=(bsz, nc),
        in_specs=[
            pl.BlockSpec((rows, conv_dim), lambda b, c: (b * nc + c, 0)),
            pl.BlockSpec((rows, GDN_WIDTH), lambda b, c: (b * nc + c, conv_dim // GDN_WIDTH)),
            pl.BlockSpec((rows, LANES), lambda b, c: (b * nc + c, COL_BA_TAIL // LANES)),
            pl.BlockSpec((1, SUBLANES, conv_dim), lambda b, c: (b, 0, 0)),
            pl.BlockSpec((1, GDN_HEADS, HEAD_DIM, HEAD_DIM), lambda b, c: (b, 0, 0, 0)),
            pl.BlockSpec((GDN_CONV, conv_dim), lambda b, c: (0, 0)),
            pl.BlockSpec((1, LANES), lambda b, c: (0, 0)),
            pl.BlockSpec((1, LANES), lambda b, c: (0, 0)),
            pl.BlockSpec((1, HEAD_DIM), lambda b, c: (0, 0)),
        ],
        out_specs=(pl.BlockSpec((rows, GDN_WIDTH), lambda b, c: (b * nc + c, 0)),
                   pl.BlockSpec((1, GDN_HEADS, HEAD_DIM, HEAD_DIM), lambda b, c: (b, 0, 0, 0))),
        scratch_shapes=[pltpu.VMEM((GDN_HEADS, HEAD_DIM, HEAD_DIM), F32),
                        pltpu.VMEM((SUBLANES + rows + SUBLANES, conv_dim), F32)],
        compiler_params=_cparams(("parallel", "arbitrary")),
        name="gdn",
    )(main, main, tail, conv_prev8, s0, conv_w, _decay_lanes(a_log), _decay_lanes(dt_bias),
      norm_g.reshape(1, -1))
    return o, s_fin


def _lambda_value(lam_refs, lam_init):
    lq1, lk1, lq2, lk2 = (r[...] for r in lam_refs)
    return (jnp.exp(jnp.sum(lq1 * lk1, axis=-1, keepdims=True))
            - jnp.exp(jnp.sum(lq2 * lk2, axis=-1, keepdims=True)) + lam_init)


def _softmax_update(m_scr, l_scr, acc_scr, scores, values):
    idx = range(len(scores))
    m_old = [m_scr[n] for n in idx]
    m_new = [jnp.maximum(m_old[n], jnp.max(scores[n], axis=-1, keepdims=True)) for n in idx]
    p = [jnp.exp(scores[n] - m_new[n]) for n in idx]
    alpha = [jnp.exp(m_old[n] - m_new[n]) for n in idx]
    pv = [_dot(p[n].astype(BF16), values[n]) for n in idx]
    for n in idx:
        l_scr[n] = alpha[n] * l_scr[n] + jnp.sum(p[n], axis=-1, keepdims=True)
        acc_scr[n] = alpha[n] * acc_scr[n] + pv[n]
        m_scr[n] = m_new[n]


def _diff_finalize(o1, o2, lam_refs, g_ref, lam_init):
    lam = _lambda_value(lam_refs, lam_init)
    return _rms(o1 - lam * o2, g_ref[...]) * (1.0 - lam_init)


def _diff_init(m_scr, l_scr, acc_scr):
    m_scr[...] = jnp.full(m_scr.shape, -jnp.inf, F32)
    l_scr[...] = jnp.zeros(l_scr.shape, F32)
    acc_scr[...] = jnp.zeros(acc_scr.shape, F32)


def _near_bias(slope, r, q_pos, k_pos):
    visible = jnp.right_shift(k_pos, CHUNK_SHIFT) <= jnp.right_shift(q_pos, CHUNK_SHIFT)
    return jnp.where(visible, slope * (r - jnp.abs(q_pos - k_pos)).astype(F32), -jnp.inf)


def _split_distance(slope, d):
    return (-(slope * CHUNK) * jnp.right_shift(d, CHUNK_SHIFT).astype(F32),
            -slope * jnp.bitwise_and(d, CHUNK - 1).astype(F32))


def _diff_prompt_kernel(i_tab, j_tab, slopes_ref, q_ref, k_ref, v_ref, lq1, lk1, lq2, lk2, g_ref, o_ref,
                        m_scr, l_scr, acc_scr, lhs_scr, s_even, s_odd, *, tq, lam_init):
    h = pl.program_id(1)
    s_id = pl.program_id(2)
    n_pairs = pl.num_programs(2) - 1
    score_on = s_id < n_pairs
    consume_on = s_id >= 1
    i = i_tab[s_id]
    j = j_tab[s_id]
    ci = i_tab[jnp.maximum(s_id - 1, 0)]
    cj = j_tab[jnp.maximum(s_id - 1, 0)]
    slope = slopes_ref[h]
    half = DIFF_HEAD_DIM
    lane = lax.broadcasted_iota(jnp.int32, (tq, LANES), 1)

    @pl.when(jnp.logical_and(consume_on, cj == 0))
    def _():
        _diff_init(m_scr, l_scr, acc_scr)

    @pl.when(jnp.logical_and(score_on, j == 0))
    def _():
        qs = q_ref[...] * (DIFF_HEAD_DIM ** -0.5)
        lhs_scr[0] = jnp.where(lane < half, qs, 1.0).astype(BF16)
        lhs_scr[1] = jnp.where(lane >= half, qs, 1.0).astype(BF16)

    sub = min(DIFF_SUB, tq)
    units = [(n, qb) for qb in range(tq // sub) for n in range(2)]

    def score_tile(near, dst):
        kb = k_ref[...]
        if near:
            aug0 = aug1 = jnp.zeros_like(kb)
        else:
            d = (i - j) * tq - lax.broadcasted_iota(jnp.int32, (tq, LANES), 0)
            t_hi, t_lo = _split_distance(slope, d)
            aug0 = jnp.where(lane == half, t_hi, jnp.where(lane == half + 1, t_lo, 0.0))
            aug1 = jnp.where(lane == 0, t_hi, jnp.where(lane == 1, t_lo, 0.0))
        keys = [jnp.where(lane < half, kb, aug0).astype(BF16), jnp.where(lane >= half, kb, aug1).astype(BF16)]
        bias = None
        for n, qb in units:
            cols = slice(qb * sub, (qb + 1) * sub)
            if near:
                live = (qb + 1) * sub
                if n == 0:
                    k_pos = lax.broadcasted_iota(jnp.int32, (live, sub), 0)
                    q_pos = qb * sub + lax.broadcasted_iota(jnp.int32, (live, sub), 1)
                    bias = _near_bias(slope, q_pos, q_pos, k_pos)
                dst[n, :live, cols] = _dot_nt(keys[n][:live], lhs_scr[n, cols, :]) + bias
                if live < tq:
                    dst[n, live:, cols] = jnp.full((tq - live, sub), -jnp.inf, F32)
            else:
                dst[n, :, cols] = _dot_nt(keys[n], lhs_scr[n, cols, :])

    def consume_tile(src):
        vt = v_ref[...].T.astype(BF16)
        idx = range(len(units))
        cols = [slice(qb * sub, (qb + 1) * sub) for _, qb in units]
        s = [src[n, :, cols[u]] for u, (n, _) in enumerate(units)]
        m_old = [m_scr[n, :, cols[u]] for u, (n, _) in enumerate(units)]
        m_new = [jnp.maximum(m_old[u], jnp.max(s[u], axis=0, keepdims=True)) for u in idx]
        p = [jnp.exp(s[u] - m_new[u]) for u in idx]
        alpha = [jnp.exp(m_old[u] - m_new[u]) for u in idx]
        pv = [_dot(vt, p[u].astype(BF16)) for u in idx]
        for u, (n, _) in enumerate(units):
            l_scr[n, :, cols[u]] = alpha[u] * l_scr[n, :, cols[u]] + jnp.sum(p[u], axis=0, keepdims=True)
            acc_scr[n, :, cols[u]] = alpha[u] * acc_scr[n, :, cols[u]] + pv[u]
            m_scr[n, :, cols[u]] = m_new[u]

    def stage(cond, near, score, consume):
        for parity, (dst, src) in enumerate(((s_even, s_odd), (s_odd, s_even))):
            @pl.when(jnp.logical_and(cond, s_id % 2 == parity))
            def _(dst=dst, src=src):
                if score:
                    score_tile(near, dst)
                if consume:
                    consume_tile(src)

    both = jnp.logical_and(score_on, consume_on)
    stage(jnp.logical_and(both, j < i), False, True, True)
    stage(jnp.logical_and(both, j == i), True, True, True)
    stage(jnp.logical_not(consume_on), True, True, False)
    stage(jnp.logical_not(score_on), False, False, True)

    @pl.when(jnp.logical_and(consume_on, cj == ci))
    def _():
        o_ref[...] = _diff_finalize((acc_scr[0] / l_scr[0]).T, (acc_scr[1] / l_scr[1]).T,
                                    (lq1, lk1, lq2, lk2), g_ref, lam_init).astype(o_ref.dtype)


def _alibi_slopes():
    return jnp.asarray([2.0 ** (-8.0 * (i + 1) / DIFF_HEADS) for i in range(DIFF_HEADS)], F32)


def _diff_prompt(main, k_new, v_new, lams, subln_g, *, bsz, t, lam_init):
    tq = min(DIFF_TQ, t)
    nq = t // tq
    assert tq % CHUNK == 0 and t <= CHUNK * 255
    kern = functools.partial(_diff_prompt_kernel, tq=tq, lam_init=lam_init)
    qc, kc, vc = COL_DQ_MAIN // LANES, 0, 0
    pairs = [(i, j) for i in range(nq) for j in range(i + 1)]
    pairs.append(pairs[-1])
    i_tab = jnp.asarray([p[0] for p in pairs], jnp.int32)
    j_tab = jnp.asarray([p[1] for p in pairs], jnp.int32)
    zero = lambda b, h, s, it, jt: (0, 0)

    def consumed(tab, s):
        return tab[jnp.maximum(s - 1, 0)]

    grid_spec = pltpu.PrefetchScalarGridSpec(
        num_scalar_prefetch=2,
        grid=(bsz, DIFF_HEADS, len(pairs)),
        in_specs=[
            pl.BlockSpec(memory_space=pltpu.SMEM),
            pl.BlockSpec((tq, LANES), lambda b, h, s, it, jt: (b * nq + it[s], qc + h)),
            pl.BlockSpec((tq, LANES), lambda b, h, s, it, jt: (b * nq + jt[s], kc + h)),
            pl.BlockSpec((tq, LANES), lambda b, h, s, it, jt: (b * nq + consumed(jt, s), vc + h)),
        ] + [pl.BlockSpec((1, DIFF_HEAD_DIM), zero)] * 4 + [pl.BlockSpec((1, HEAD_DIM), zero)],
        out_specs=pl.BlockSpec((tq, LANES), lambda b, h, s, it, jt: (b * nq + consumed(it, s), h)),
        scratch_shapes=[pltpu.VMEM((2, 1, tq), F32), pltpu.VMEM((2, 1, tq), F32),
                        pltpu.VMEM((2, HEAD_DIM, tq), F32), pltpu.VMEM((2, tq, LANES), BF16),
                        pltpu.VMEM((2, tq, tq), F32), pltpu.VMEM((2, tq, tq), F32)],
    )
    return pl.pallas_call(
        kern,
        out_shape=jax.ShapeDtypeStruct((bsz * t, DIFF_HEADS * HEAD_DIM), BF16),
        grid_spec=grid_spec,
        compiler_params=_cparams(("parallel", "parallel", "arbitrary")),
        name="diff_attn_prompt",
    )(i_tab, j_tab, _alibi_slopes(), main, k_new, v_new, *lams, subln_g.reshape(1, -1))


def _diff_sample_kernel(slopes_ref, q_ref, kn_ref, vn_ref, kp_ref, vp_ref, lq1, lk1, lq2, lk2, g_ref, o_ref,
                        m_scr, l_scr, acc_scr, lhs_scr, *, t, tk, past, lam_init):
    j = pl.program_id(1)
    heads = range(DIFF_HEADS)

    def cols(h):
        return slice(h * HEAD_DIM, (h + 1) * HEAD_DIM)

    @pl.when(j == 0)
    def _():
        _diff_init(m_scr, l_scr, acc_scr)
        lane = lax.broadcasted_iota(jnp.int32, (t, LANES), 1)
        r = lax.broadcasted_iota(jnp.int32, (t, t), 0)
        c = lax.broadcasted_iota(jnp.int32, (t, t), 1)
        scores = []
        for h in heads:
            qs = q_ref[:, cols(h)] * (DIFF_HEAD_DIM ** -0.5)
            lhs_scr[h, 0:t, :] = jnp.where(lane < DIFF_HEAD_DIM, qs, 0.0).astype(BF16)
            lhs_scr[h, t:2 * t, :] = jnp.where(lane >= DIFF_HEAD_DIM, qs, 0.0).astype(BF16)
            bias = _near_bias(slopes_ref[h], r, past + r, past + c)
            scores.append(_dot_nt(lhs_scr[h], kn_ref[:, cols(h)].astype(BF16)) + jnp.concatenate([bias, bias], axis=0))
        _softmax_update(m_scr, l_scr, acc_scr, scores, [vn_ref[:, cols(h)].astype(BF16) for h in heads])

    d = (past - j * tk - lax.broadcasted_iota(jnp.int32, (1, tk), 1)).astype(F32)
    kt = jnp.swapaxes(kp_ref[...], 0, 1).astype(BF16)
    vt = jnp.swapaxes(vp_ref[...], 0, 1).astype(BF16)
    scores = [_dot_nt(lhs_scr[h], kt[h]) - slopes_ref[h] * d for h in heads]
    _softmax_update(m_scr, l_scr, acc_scr, scores, [vt[h] for h in heads])

    @pl.when(j == pl.num_programs(1) - 1)
    def _():
        for h in heads:
            o = acc_scr[h] / l_scr[h]
            o_ref[:, cols(h)] = _diff_finalize(o[:t], o[t:], (lq1, lk1, lq2, lk2), g_ref, lam_init).astype(o_ref.dtype)


def _diff_sample(main, k_new, v_new, k_past, v_past, lams, subln_g, *, bsz, t, lam_init):
    past = k_past.shape[1]
    tk = min(1024, past)
    nk = past // tk
    width = DIFF_HEADS * HEAD_DIM
    kern = functools.partial(_diff_sample_kernel, t=t, tk=tk, past=past, lam_init=lam_init)
    zero = lambda b, j: (0, 0)
    cache = pl.BlockSpec((None, tk, DIFF_HEADS, HEAD_DIM), lambda b, j: (b, j, 0, 0))
    return pl.pallas_call(
        kern,
        out_shape=jax.ShapeDtypeStruct((bsz * t, width), BF16),
        grid=(bsz, nk),
        in_specs=[
            pl.BlockSpec(memory_space=pltpu.SMEM),
            pl.BlockSpec((t, width), lambda b, j: (b, COL_DQ_MAIN // width)),
            pl.BlockSpec((t, width), lambda b, j: (b, 0)),
            pl.BlockSpec((t, width), lambda b, j: (b, 0)),
            cache, cache,
        ] + [pl.BlockSpec((1, DIFF_HEAD_DIM), zero)] * 4 + [pl.BlockSpec((1, HEAD_DIM), zero)],
        out_specs=pl.BlockSpec((t, width), lambda b, j: (b, 0)),
        scratch_shapes=[pltpu.VMEM((DIFF_HEADS, 2 * t, 1), F32), pltpu.VMEM((DIFF_HEADS, 2 * t, 1), F32),
                        pltpu.VMEM((DIFF_HEADS, 2 * t, HEAD_DIM), F32), pltpu.VMEM((DIFF_HEADS, 2 * t, LANES), BF16)],
        compiler_params=_cparams(("parallel", "arbitrary")),
        name="diff_attn_sample",
    )(_alibi_slopes(), main, k_new, v_new, k_past, v_past, *lams, subln_g.reshape(1, -1))


def _merge_kernel(oa_ref, ob_ref, wa_ref, wb_ref, ga_ref, gb_ref, o_ref):
    ya = _dot(oa_ref[...], wa_ref[...])
    yb = _dot(ob_ref[...], wb_ref[...])
    o_ref[...] = (jax.nn.sigmoid(ga_ref[...]) * ya + jax.nn.sigmoid(gb_ref[...]) * yb).astype(o_ref.dtype)


def _merge(o_a, o_b, wa, wb, tail, *, tm):
    m = o_a.shape[0]
    d = wa.shape[1]
    tn = 512
    ga0, gb0 = 0, COL_GB_TAIL // tn
    return pl.pallas_call(
        _merge_kernel,
        out_shape=jax.ShapeDtypeStruct((m, d), BF16),
        grid=(m // tm, d // tn),
        in_specs=[
            pl.BlockSpec((tm, o_a.shape[1]), lambda i, j: (i, 0)),
            pl.BlockSpec((tm, o_b.shape[1]), lambda i, j: (i, 0)),
            pl.BlockSpec((wa.shape[0], tn), lambda i, j: (0, j)),
            pl.BlockSpec((wb.shape[0], tn), lambda i, j: (0, j)),
            pl.BlockSpec((tm, tn), lambda i, j: (i, ga0 + j)),
            pl.BlockSpec((tm, tn), lambda i, j: (i, gb0 + j)),
        ],
        out_specs=pl.BlockSpec((tm, tn), lambda i, j: (i, j)),
        compiler_params=_cparams(("parallel", "arbitrary")),
        name="merge",
    )(o_a, o_b, wa, wb, tail, tail)


def _cross_kernel(q_ref, k_ref, v_ref, o_ref):
    dh = q_ref.shape[-1] // MEM_HEADS
    for h in range(MEM_HEADS):
        sl = slice(h * dh, (h + 1) * dh)
        s = _dot_nt(q_ref[:, sl], k_ref[0, :, sl].astype(BF16)) * (dh ** -0.5)
        s = s - jnp.max(s, axis=-1, keepdims=True)
        p = jnp.exp(s)
        p = p / jnp.sum(p, axis=-1, keepdims=True)
        o_ref[:, sl] = _dot(p.astype(BF16), v_ref[0, :, sl].astype(BF16)).astype(o_ref.dtype)


def _cross_attend(q, mem_k, mem_v, *, bsz, t, tm):
    d = q.shape[1]
    nt = t // tm
    mt = mem_k.shape[1]
    return pl.pallas_call(
        _cross_kernel,
        out_shape=jax.ShapeDtypeStruct(q.shape, BF16),
        grid=(bsz, nt),
        in_specs=[
            pl.BlockSpec((tm, d), lambda b, i: (b * nt + i, 0)),
            pl.BlockSpec((1, mt, d), lambda b, i: (b, 0, 0)),
            pl.BlockSpec((1, mt, d), lambda b, i: (b, 0, 0)),
        ],
        out_specs=pl.BlockSpec((tm, d), lambda b, i: (b * nt + i, 0)),
        compiler_params=_cparams(("parallel", "arbitrary")),
        name="cross_attn",
    )(q, mem_k, mem_v)


def _topk_rows(work, cidx, k):
    n, tn = work.shape
    rid = lax.broadcasted_iota(jnp.int32, (n, tn), 0).astype(F32)
    kid = lax.broadcasted_iota(jnp.int32, (k, tn), 0)
    vals = jnp.zeros((k, tn), F32)
    idxs = jnp.zeros((k, tn), F32)
    for t in range(k):
        m = jnp.max(work, axis=0, keepdims=True)
        pos = jnp.min(jnp.where(work == m, rid, float(n)), axis=0, keepdims=True)
        hit = rid == pos
        if cidx is None:
            picked = pos
        else:
            picked = jnp.sum(jnp.where(hit, cidx, 0.0), axis=0, keepdims=True)
        vals = jnp.where(kid == t, m, vals)
        idxs = jnp.where(kid == t, picked, idxs)
        work = jnp.where(hit, -jnp.inf, work)
    return vals, idxs


def _product_candidates(v1, i1, v2, i2, k):
    assert k == 2 * SUBLANES
    sub = lax.broadcasted_iota(jnp.int32, (SUBLANES, v1.shape[1]), 0)
    vals, idxs = [], []
    for a in range(k // 2):
        nb = k // (a + 1)
        width = k if nb > SUBLANES else SUBLANES
        cv = v1[a:a + 1, :] + v2[:width, :]
        ci = i1[a:a + 1, :] * PEER_N_KEYS + i2[:width, :]
        if nb < SUBLANES:
            cv = jnp.where(sub < nb, cv, -jnp.inf)
        vals.append(cv)
        idxs.append(ci)
    vals.append(v1[k // 2:, :] + v2[0:1, :])
    idxs.append(i1[k // 2:, :] * PEER_N_KEYS + i2[0:1, :])
    return jnp.concatenate(vals, axis=0), jnp.concatenate(idxs, axis=0)


def _peer_route_kernel(q_ref, keys_hi_ref, keys_lo_ref, e_ref, g_ref):
    k = PEER_TOPK
    for h in range(PEER_HEADS):
        sub = []
        for p in range(2):
            c0 = (h * 2 + p) * PEER_N_KEYS
            q_parts = _split_bf16(q_ref[:, c0:c0 + PEER_N_KEYS])
            s = _dot3_parts((keys_hi_ref[h, p], keys_lo_ref[h, p]), q_parts, _dot_nt)
            sub.append(_topk_rows(s, None, k))
        (v1, i1), (v2, i2) = sub
        cand, cidx = _product_candidates(v1, i1, v2, i2, k)
        best, eidx = _topk_rows(cand, cidx, k)
        ex = jnp.exp(best - jnp.max(best, axis=0, keepdims=True))
        gate = ex / jnp.sum(ex, axis=0, keepdims=True)
        e_ref[h * k:(h + 1) * k, :] = eidx.astype(jnp.int32)
        g_ref[h * k:(h + 1) * k, :] = gate


HALF_KEYS = PEER_N_KEYS // 2
PEER_GATE_TN = 256
BF16_HIGH_MASK = 0xFFFF0000


def _peer_gate_kernel(et_ref, gt_ref, o_ref, e_scr, g_scr):
    tg = et_ref.shape[1]
    e_scr[...] = et_ref[...].T
    g_scr[...] = gt_ref[...].T
    kid = lax.broadcasted_iota(jnp.int32, (PEER_N_KEYS, PEER_N_KEYS), 0)

    def body(n8, carry):
        rows = pl.ds(pl.multiple_of(n8 * SUBLANES, SUBLANES), SUBLANES)
        e8 = e_scr[rows, :]
        g8 = g_scr[rows, :]
        a8 = jnp.right_shift(e8, 7)
        b8 = jnp.bitwise_and(e8, PEER_N_KEYS - 1)
        toks = range(SUBLANES)
        at = [jnp.where(a8[s:s + 1, :] == kid, g8[s:s + 1, :], 0.0).astype(BF16) for s in toks]
        bt = [jnp.where(b8[s:s + 1, :] == kid, 1.0, 0.0).astype(BF16) for s in toks]
        grids = [_dot_nt(at[s], bt[s]) for s in toks]
        bits = [pltpu.bitcast(grids[s].astype(BF16).astype(F32), jnp.uint32) for s in toks]
        words = [jnp.bitwise_or(jnp.bitwise_and(bits[s][HALF_KEYS:, :], jnp.uint32(BF16_HIGH_MASK)),
                                jnp.right_shift(bits[s][:HALF_KEYS, :], jnp.uint32(16))) for s in toks]
        o_ref[:, rows, :] = jnp.swapaxes(jnp.stack(words, axis=0), 0, 1)
        return carry

    lax.fori_loop(0, tg // SUBLANES, body, 0, unroll=4)


def _peer_route(qp, sub_keys, *, tn):
    n = qp.shape[0]
    rows = PEER_HEADS * PEER_TOPK
    keys_hi, keys_lo = _split_bf16(sub_keys.astype(F32))
    keys_spec = pl.BlockSpec(sub_keys.shape, lambda i: (0, 0, 0, 0))
    e_t, g_t = pl.pallas_call(
        _peer_route_kernel,
        out_shape=(jax.ShapeDtypeStruct((rows, n), jnp.int32), jax.ShapeDtypeStruct((rows, n), F32)),
        grid=(n // tn,),
        in_specs=[pl.BlockSpec((tn, qp.shape[1]), lambda i: (i, 0)), keys_spec, keys_spec],
        out_specs=(pl.BlockSpec((rows, tn), lambda i: (0, i)), pl.BlockSpec((rows, tn), lambda i: (0, i))),
        compiler_params=_cparams(("parallel",)),
        name="peer_route",
    )(qp, keys_hi, keys_lo)
    tg = min(PEER_GATE_TN, n)
    return pl.pallas_call(
        _peer_gate_kernel,
        out_shape=jax.ShapeDtypeStruct((HALF_KEYS, n, PEER_N_KEYS), jnp.uint32),
        grid=(n // tg,),
        in_specs=[pl.BlockSpec((rows, tg), lambda i: (0, i)), pl.BlockSpec((rows, tg), lambda i: (0, i))],
        out_specs=pl.BlockSpec((HALF_KEYS, tg, PEER_N_KEYS), lambda i: (0, i, 0)),
        scratch_shapes=[pltpu.VMEM((tg, rows), jnp.int32), pltpu.VMEM((tg, rows), F32)],
        compiler_params=_cparams(("parallel",)),
        name="peer_gates",
    )(e_t, g_t)


def _gelu_tanh(x):
    return 0.5 * x * (1.0 + jnp.tanh(math.sqrt(2.0 / math.pi) * (x + 0.044715 * (x * x * x))))


PEER_BLOCKS_PER_STEP = 4
PEER_TN = 1024
PEER_VMEM_LIMIT = 60 * 1024 * 1024


def _peer_expert_kernel(x_ref, gn_ref, gpa_ref, gpb_ref, *rest):
    nblk = PEER_BLOCKS_PER_STEP
    u_refs, v_refs = rest[:nblk], rest[nblk:2 * nblk]
    gf_ref, o_ref, h_scr, acc_scr, coef_scr = rest[2 * nblk:]
    s = pl.program_id(1)
    last = pl.num_programs(1) - 1

    @pl.when(s == 0)
    def _():
        h_scr[...] = _rms(x_ref[...], gn_ref[...]).astype(BF16)
        acc_scr[...] = jnp.zeros(acc_scr.shape, F32)
        coef_scr[...] = jnp.zeros(coef_scr.shape, BF16)

    v_cat = jnp.concatenate([v[...].astype(BF16) for v in v_refs], axis=0)
    acc_scr[...] += _dot(coef_scr[(s + 1) % 2], v_cat)

    def unpack(word):
        return (pltpu.bitcast(jnp.left_shift(word, jnp.uint32(16)), F32),
                pltpu.bitcast(jnp.bitwise_and(word, jnp.uint32(BF16_HIGH_MASK)), F32))

    a_lo, a_hi = unpack(gpa_ref[...])
    b_lo, b_hi = unpack(gpb_ref[...])
    u_cat = jnp.concatenate([u[...].astype(BF16) for u in u_refs], axis=0)
    hid = _gelu_tanh(_dot_nt(h_scr[...], u_cat))
    coef_scr[s % 2] = (jnp.concatenate([a_lo, b_lo, a_hi, b_hi], axis=1) * hid).astype(BF16)

    @pl.when(s == last)
    def _():
        o_ref[...] = _rms(x_ref[...] + acc_scr[...], gf_ref[...])


def _peer_experts(x, norm_g, gates, u_tab, v_tab, final_g, *, tn):
    n, d = x.shape
    eb = PEER_N_KEYS
    assert u_tab.shape[0] == eb * eb
    nblk = PEER_BLOCKS_PER_STEP
    ns = eb // nblk

    def cur(s):
        return jnp.minimum(s, ns - 1)

    def prev(s):
        return jnp.maximum(s - 1, 0)

    def table(which, q):
        return pl.BlockSpec((eb, d), lambda i, s: (which(s) + q * ns, 0))

    return pl.pallas_call(
        _peer_expert_kernel,
        out_shape=jax.ShapeDtypeStruct((n, d), F32),
        grid=(n // tn, ns + 1),
        in_specs=[
            pl.BlockSpec((tn, d), lambda i, s: (i, 0), pipeline_mode=pl.Buffered(1)),
            pl.BlockSpec((1, d), lambda i, s: (0, 0)),
            pl.BlockSpec((None, tn, eb), lambda i, s: (cur(s), i, 0)),
            pl.BlockSpec((None, tn, eb), lambda i, s: (cur(s) + ns, i, 0)),
        ] + [table(cur, q) for q in range(nblk)] + [table(prev, q) for q in range(nblk)] + [
            pl.BlockSpec((1, d), lambda i, s: (0, 0)),
        ],
        out_specs=pl.BlockSpec((tn, d), lambda i, s: (i, 0), pipeline_mode=pl.Buffered(1)),
        scratch_shapes=[pltpu.VMEM((tn, d), BF16), pltpu.VMEM((tn, d), F32),
                        pltpu.VMEM((2, tn, nblk * eb), BF16)],
        compiler_params=_cparams(("parallel", "arbitrary"), vmem=PEER_VMEM_LIMIT),
        name="peer_experts",
    )(x, norm_g.reshape(1, d), gates, gates, *([u_tab] * nblk), *([v_tab] * nblk), final_g.reshape(1, d))


def _layer(x, wts, layer, mem_k, mem_v, conv_prev, gdn_state, k_past, v_past, *, tm):
    bsz, t, d = x.shape
    n = bsz * t
    x2d = x.reshape(n, d)
    main, k_new, v_new, tail = _in_proj(x2d, wts["norm_mix_g"], wts["w_in"], tm=min(IN_TM, n))
    conv_dim = 3 * GDN_WIDTH
    conv_prev8 = jnp.concatenate(
        [jnp.zeros((bsz, SUBLANES - (GDN_CONV - 1), conv_dim), F32), conv_prev.astype(F32)], axis=1)
    o_a, s_new = _gdn(main, tail, conv_prev8, gdn_state.astype(F32), wts["gdn_conv_w"], wts["gdn_a_log"],
                      wts["gdn_dt_bias"], wts["gdn_norm_g"], bsz=bsz, t=t)
    conv_new = main.reshape(bsz, t, -1)[:, t - (GDN_CONV - 1):, :conv_dim]
    lam_init = 0.8 - 0.6 * math.exp(-0.3 * layer)
    lams = tuple(wts[k].reshape(1, -1) for k in ("diff_lambda_q1", "diff_lambda_k1", "diff_lambda_q2", "diff_lambda_k2"))
    if k_past is None:
        o_b = _diff_prompt(main, k_new, v_new, lams, wts["diff_subln_g"], bsz=bsz, t=t, lam_init=lam_init)
    else:
        o_b = _diff_sample(main, k_new, v_new, k_past, v_past, lams, wts["diff_subln_g"], bsz=bsz, t=t,
                           lam_init=lam_init)
    k_rows = k_new.reshape(bsz, t, DIFF_HEADS, HEAD_DIM)
    v_rows = v_new.reshape(bsz, t, DIFF_HEADS, HEAD_DIM)
    merged = _merge(o_a, o_b, wts["w_branch_a"], wts["w_branch_b"], tail, tm=tm)
    x1 = _matmul(merged, wts["w_out"], residual=x2d, tm=tm, tn=MM_TN, name="out_proj")
    qm = _matmul(x1, wts["w_mq"], norm_g=wts["norm_cross_g"], tm=tm, tn=MM_TN, out_dtype=BF16, name="mem_q")
    mt = mem_k.shape[1]
    oc = _cross_attend(qm, mem_k.reshape(bsz, mt, d), mem_v.reshape(bsz, mt, d), bsz=bsz, t=t, tm=min(tm, t))
    x2 = _matmul(oc, wts["w_mo"], residual=x1, tm=tm, tn=MM_TN, name="mem_o")
    qp = _matmul(x2, wts["peer_w_q"], norm_g=wts["norm_ffn_g"], tm=tm, tn=MM_TN, name="peer_q")
    tn = min(PEER_TN, n)
    gates = _peer_route(qp, wts["peer_sub_keys"], tn=LANES)
    y = _peer_experts(x2, wts["norm_ffn_g"], gates, wts["peer_u"], wts["peer_v"], wts["final_norm_g"], tn=tn)
    return y.reshape(bsz, t, d), k_rows, v_rows, s_new, conv_new


def kernel(x_prompt, x_sample, cache_diff_k, cache_diff_v, state_gdn, state_conv, cache_mem_k, cache_mem_v,
           mem_prompt, norm_mix_g, w_in, gdn_conv_w, gdn_a_log, gdn_dt_bias, gdn_norm_g,
           diff_lambda_q1, diff_lambda_k1, diff_lambda_q2, diff_lambda_k2, diff_subln_g,
           w_branch_a, w_branch_b, w_out, norm_cross_g, norm_mem_g, w_mq, w_mk, w_mv, w_mo,
           norm_ffn_g, peer_w_q, peer_sub_keys, peer_u, peer_v, final_norm_g):
    depth = w_in.shape[0]
    assert depth == 1, "final norm is fused into the last layer's PEER kernel"
    l = 0
    wts = {
        "norm_mix_g": norm_mix_g[l], "w_in": _in_proj_weights(w_in[l]), "gdn_conv_w": gdn_conv_w[l],
        "gdn_a_log": gdn_a_log[l], "gdn_dt_bias": gdn_dt_bias[l], "gdn_norm_g": gdn_norm_g[l],
        "diff_lambda_q1": diff_lambda_q1[l], "diff_lambda_k1": diff_lambda_k1[l],
        "diff_lambda_q2": diff_lambda_q2[l], "diff_lambda_k2": diff_lambda_k2[l],
        "diff_subln_g": diff_subln_g[l], "w_branch_a": w_branch_a[l].astype(BF16),
        "w_branch_b": w_branch_b[l].astype(BF16), "w_out": w_out[l].astype(BF16),
        "norm_cross_g": norm_cross_g[l], "w_mq": w_mq[l].astype(BF16), "w_mo": w_mo[l].astype(BF16),
        "norm_ffn_g": norm_ffn_g[l], "peer_w_q": peer_w_q[l].astype(BF16), "peer_sub_keys": peer_sub_keys[l],
        "peer_u": peer_u[l], "peer_v": peer_v[l], "final_norm_g": final_norm_g,
    }
    bp, tp, d = x_prompt.shape
    bs, ts, _ = x_sample.shape
    mem2d = mem_prompt.reshape(-1, d)
    tmm = min(512, mem2d.shape[0])
    mem_k = _matmul(mem2d, w_mk[l], norm_g=norm_mem_g[l], tm=tmm, tn=512, name="mem_k")
    mem_v = _matmul(mem2d, w_mv[l], norm_g=norm_mem_g[l], tm=tmm, tn=512, name="mem_v")
    mshape = mem_prompt.shape[:2] + (MEM_HEADS, d // MEM_HEADS)
    mem_k = mem_k.reshape(mshape)
    mem_v = mem_v.reshape(mshape)
    conv0 = jnp.zeros((bp, GDN_CONV - 1, 3 * GDN_WIDTH), F32)
    s0 = jnp.zeros((bp, GDN_HEADS, HEAD_DIM, HEAD_DIM), F32)
    yp, pk, pv, ps, pc = _layer(x_prompt, wts, l, mem_k, mem_v, conv0, s0, None, None, tm=min(1024, bp * tp))
    ys, sk, sv, ss, sc = _layer(x_sample, wts, l, cache_mem_k[l], cache_mem_v[l], state_conv[l], state_gdn[l],
                                cache_diff_k[l], cache_diff_v[l], tm=min(256, bs * ts))
    return (yp, ys, pk[None], pv[None], ps[None], pc[None], mem_k[None], mem_v[None],
            sk[None], sv[None], ss[None], sc[None])
```

```python
import functools
import math

import jax
import jax.numpy as jnp
from jax import lax
from jax.experimental import pallas as pl
from jax.experimental.pallas import tpu as pltpu

F32 = jnp.float32
BF16 = jnp.bfloat16
HIGHEST = lax.Precision.HIGHEST

RMS_EPS = 1e-6
CHUNK = 64
CHUNK_SHIFT = 6
DIFF_TQ = 1024
GDN_CHUNKS_PER_STEP = 4
DIFF_SUB = 256
GDN_HEADS = 8
HEAD_DIM = 128
GDN_WIDTH = GDN_HEADS * HEAD_DIM
GDN_CONV = 4
DIFF_HEADS = 8
DIFF_HEAD_DIM = 64
MEM_HEADS = 4
PEER_HEADS = 8
PEER_N_KEYS = 128
PEER_TOPK = 16
LANES = 128
SUBLANES = 8
V7X_VMEM_LIMIT = 52 * 1024 * 1024
MM_TN = 1024


def _cparams(sem, vmem=V7X_VMEM_LIMIT):
    return pltpu.CompilerParams(dimension_semantics=sem, vmem_limit_bytes=vmem)


def _dot(a, b, precision=None):
    return jnp.dot(a, b, preferred_element_type=F32, precision=precision)


def _dot_nt(a, b, precision=None):
    return lax.dot_general(a, b, (((1,), (1,)), ((), ())), preferred_element_type=F32, precision=precision)


def _split_bf16(x):
    hi = x.astype(BF16)
    return hi, (x - hi.astype(F32)).astype(BF16)


def _cat_parts(p, q):
    return jnp.concatenate([p[0], q[0]], axis=0), jnp.concatenate([p[1], q[1]], axis=0)


def _dot3_parts(a_parts, b_parts, dot):
    a_hi, a_lo = a_parts
    b_hi, b_lo = b_parts
    m = a_hi.shape[0]
    both = dot(jnp.concatenate([a_hi, a_lo], axis=0), b_hi)
    return both[:m] + both[m:] + dot(a_hi, b_lo)


def _dot3(a, b):
    return _dot3_parts(_split_bf16(a), _split_bf16(b), _dot)


def _rms(xf, g):
    return xf * lax.rsqrt(jnp.mean(xf * xf, axis=-1, keepdims=True) + RMS_EPS) * g


def _mm_kernel(*refs, has_norm, has_res):
    it = iter(refs)
    x_ref = next(it)
    g_ref = next(it) if has_norm else None
    w_ref = next(it)
    r_ref = next(it) if has_res else None
    o_ref = next(it)
    h_scr = next(it)

    @pl.when(pl.program_id(1) == 0)
    def _():
        xf = x_ref[...].astype(F32)
        if has_norm:
            xf = _rms(xf, g_ref[...])
        h_scr[...] = xf.astype(BF16)

    acc = _dot(h_scr[...], w_ref[...])
    if has_res:
        acc = acc + r_ref[...]
    o_ref[...] = acc.astype(o_ref.dtype)


def _matmul(x, w, *, norm_g=None, residual=None, tm, tn, out_dtype=F32, name="matmul"):
    m, k = x.shape
    n = w.shape[1]
    assert m % tm == 0 and n % tn == 0, (m, n, tm, tn)
    in_specs = [pl.BlockSpec((tm, k), lambda i, j: (i, 0))]
    args = [x]
    if norm_g is not None:
        in_specs.append(pl.BlockSpec((1, k), lambda i, j: (0, 0)))
        args.append(norm_g.reshape(1, k).astype(F32))
    in_specs.append(pl.BlockSpec((k, tn), lambda i, j: (0, j)))
    args.append(w)
    if residual is not None:
        in_specs.append(pl.BlockSpec((tm, tn), lambda i, j: (i, j)))
        args.append(residual)
    return pl.pallas_call(
        functools.partial(_mm_kernel, has_norm=norm_g is not None, has_res=residual is not None),
        out_shape=jax.ShapeDtypeStruct((m, n), out_dtype),
        grid=(m // tm, n // tn),
        in_specs=in_specs,
        out_specs=pl.BlockSpec((tm, tn), lambda i, j: (i, j)),
        scratch_shapes=[pltpu.VMEM((tm, k), BF16)],
        compiler_params=_cparams(("parallel", "arbitrary")),
        name=name,
    )(*args)


IN_TN = 512
IN_TM = 1024
IN_CAST_ROWS = 128
IN_W1_TILES = 8
IN_W2_TILES = 14
IN_MAIN_TILES = 10
IN_K_TILE0 = 10
IN_V_TILE0 = 12
IN_TAIL_TILE0 = 14
IN_TILES = IN_W1_TILES + IN_W2_TILES + 1
MAIN_COLS = IN_MAIN_TILES * IN_TN
TAIL_COLS = (IN_TILES - IN_TAIL_TILE0) * IN_TN
COL_DQ_MAIN = 4096
COL_GB_TAIL = 2048
COL_BA_TAIL = 4096


def _in_proj_kernel(x_ref, g_ref, w1_ref, w2_ref, w3_ref, main_ref, k_ref, v_ref, tail_ref, h_scr):
    j = pl.program_id(1)

    @pl.when(j == 0)
    def _():
        h_scr[...] = _rms(x_ref[...], g_ref[...]).astype(BF16)

    routes = ((0, IN_W1_TILES, w1_ref, main_ref),
              (IN_W1_TILES, IN_K_TILE0, w2_ref, main_ref),
              (IN_K_TILE0, IN_V_TILE0, w2_ref, k_ref),
              (IN_V_TILE0, IN_TAIL_TILE0, w2_ref, v_ref),
              (IN_TAIL_TILE0, IN_TILES - 1, w2_ref, tail_ref),
              (IN_TILES - 1, IN_TILES, w3_ref, tail_ref))
    for lo, hi, w_ref, dst in routes:
        @pl.when(jnp.logical_and(j >= lo, j < hi))
        def _(w_ref=w_ref, dst=dst):
            dst[...] = _dot(h_scr[...], w_ref[...])


def _in_proj_weights(w_in):
    d, n = w_in.shape
    c_z = 4 * GDN_WIDTH
    c_ba = c_z + 2 * GDN_HEADS
    assert c_z == IN_W1_TILES * IN_TN and n - c_ba == IN_W2_TILES * IN_TN
    rows = IN_CAST_ROWS

    def cast_kernel(w_ref, all_ref, tail_ref):
        w = w_ref[...]
        all_ref[...] = w.astype(BF16)
        tail_ref[...] = w[:, c_ba:].astype(BF16)

    return pl.pallas_call(
        cast_kernel,
        out_shape=(jax.ShapeDtypeStruct((d, n), BF16), jax.ShapeDtypeStruct((d, n - c_ba), BF16)),
        grid=(d // rows,),
        in_specs=[pl.BlockSpec((rows, n), lambda i: (i, 0))],
        out_specs=(pl.BlockSpec((rows, n), lambda i: (i, 0)), pl.BlockSpec((rows, n - c_ba), lambda i: (i, 0))),
        compiler_params=_cparams(("parallel",)),
        name="w_in_cast",
    )(w_in)


def _in_proj(x, norm_g, w_pieces, *, tm):
    m, d = x.shape
    w_all, w2 = w_pieces
    w1 = w3 = w_all
    tn = IN_TN
    out = lambda cols: jax.ShapeDtypeStruct((m, cols), F32)
    return pl.pallas_call(
        _in_proj_kernel,
        out_shape=(out(MAIN_COLS), out(2 * IN_TN), out(2 * IN_TN), out(TAIL_COLS)),
        grid=(m // tm, IN_TILES),
        in_specs=[
            pl.BlockSpec((tm, d), lambda i, j: (i, 0)),
            pl.BlockSpec((1, d), lambda i, j: (0, 0)),
            pl.BlockSpec((d, tn), lambda i, j: (0, jnp.minimum(j, IN_W1_TILES - 1))),
            pl.BlockSpec((d, tn), lambda i, j: (0, jnp.clip(j - IN_W1_TILES, 0, IN_W2_TILES - 1))),
            pl.BlockSpec((d, tn), lambda i, j: (0, IN_W1_TILES)),
        ],
        out_specs=(
            pl.BlockSpec((tm, tn), lambda i, j: (i, jnp.minimum(j, IN_MAIN_TILES - 1))),
            pl.BlockSpec((tm, tn), lambda i, j: (i, jnp.clip(j - IN_K_TILE0, 0, 1))),
            pl.BlockSpec((tm, tn), lambda i, j: (i, jnp.clip(j - IN_V_TILE0, 0, 1))),
            pl.BlockSpec((tm, tn), lambda i, j: (i, jnp.clip(j - IN_TAIL_TILE0, 0, IN_TILES - IN_TAIL_TILE0 - 1))),
        ),
        scratch_shapes=[pltpu.VMEM((tm, d), BF16)],
        compiler_params=_cparams(("parallel", "arbitrary")),
        name="in_proj",
    )(x, norm_g.reshape(1, d).astype(F32), w1, w2, w3)


def _gdn_kernel(qkv_ref, z_ref, ba_ref, prev_ref, s0_ref, convw_ref, alog_ref, dtb_ref, ng_ref,
                o_ref, sfin_ref, s_scr, xbuf, *, L, nch):
    step = pl.program_id(1)
    rows = nch * L

    @pl.when(step == 0)
    def _():
        s_scr[...] = s0_ref[0]
        xbuf[0:SUBLANES, :] = prev_ref[0]

    xbuf[SUBLANES:SUBLANES + rows, :] = qkv_ref[...]
    base = SUBLANES - (GDN_CONV - 1)
    y = xbuf[base:base + rows, :] * convw_ref[0:1, :]
    for i in range(1, GDN_CONV):
        y = y + xbuf[base + i:base + i + rows, :] * convw_ref[i:i + 1, :]
    y = y * jax.nn.sigmoid(y)
    carry = xbuf[rows:rows + SUBLANES, :]
    xbuf[0:SUBLANES, :] = carry

    row = lax.broadcasted_iota(jnp.int32, (L, L), 0)
    col = lax.broadcasted_iota(jnp.int32, (L, L), 1)
    lower = row >= col
    strict = row > col
    eye = (row == col).astype(F32)
    log_l = int(math.log2(L))
    n_double = log_l - 1
    assert 2 ** log_l == L

    ba = ba_ref[...]
    beta_all = jax.nn.sigmoid(ba)
    a_in = ba + dtb_ref[...]
    softplus = jnp.maximum(a_in, 0.0) + jnp.log(1.0 + jnp.exp(-jnp.abs(a_in)))
    g_all = -jnp.exp(alog_ref[...]) * softplus
    rr = lax.broadcasted_iota(jnp.int32, (rows, rows), 0)
    cc = lax.broadcasted_iota(jnp.int32, (rows, rows), 1)
    same_chunk_lower = jnp.logical_and(rr >= cc, jnp.right_shift(rr, log_l) == jnp.right_shift(cc, log_l))
    g_cum = _dot(same_chunk_lower.astype(F32), g_all, HIGHEST)
    g_cum_t = g_cum.T

    heads = range(GDN_HEADS)
    chains = [(c, h) for c in range(nch) for h in heads]
    ids = range(len(chains))

    def tok(c):
        return slice(c * L, (c + 1) * L)

    def head_cols(part, c, h):
        return y[tok(c), part * GDN_WIDTH + h * HEAD_DIM:part * GDN_WIDTH + (h + 1) * HEAD_DIM]

    q = [head_cols(0, c, h) for c, h in chains]
    k = [head_cols(1, c, h) for c, h in chains]
    v = [head_cols(2, c, h) for c, h in chains]
    q = [a * lax.rsqrt(jnp.sum(a * a, axis=-1, keepdims=True) + 1e-6) * (HEAD_DIM ** -0.5) for a in q]
    k = [a * lax.rsqrt(jnp.sum(a * a, axis=-1, keepdims=True) + 1e-6) for a in k]
    beta = [beta_all[tok(c), h:h + 1] for c, h in chains]
    g_col = [g_cum[tok(c), GDN_HEADS + h:GDN_HEADS + h + 1] for c, h in chains]
    g_row = [g_cum_t[GDN_HEADS + h:GDN_HEADS + h + 1, tok(c)] for c, h in chains]
    decay = [jnp.where(lower, jnp.exp(jnp.where(lower, g_col[n] - g_row[n], 0.0)), 0.0) for n in ids]
    g_last = [g_col[n][L - 1:L, :] for n in ids]
    e_col = [jnp.exp(g_col[n]) for n in ids]
    qk_kk = [_dot_nt(jnp.concatenate([q[n], k[n]], axis=0).astype(BF16), k[n].astype(BF16)) for n in ids]
    x = [-jnp.where(strict, beta[n] * qk_kk[n][L:] * decay[n], 0.0) for n in ids]
    t_inv = [eye + x[n] for n in ids]
    x_parts = [_split_bf16(x[n]) for n in ids]
    xp = [_dot3_parts(x_parts[n], x_parts[n], _dot) for n in ids]
    for _ in range(n_double - 1):
        xp_parts = [_split_bf16(xp[n]) for n in ids]
        both = [_dot3_parts(_cat_parts(_split_bf16(t_inv[n]), xp_parts[n]), xp_parts[n], _dot) for n in ids]
        t_inv = [t_inv[n] + both[n][:L] for n in ids]
        xp = [both[n][L:] for n in ids]
    t_inv = [t_inv[n] + _dot3(t_inv[n], xp[n]) for n in ids]
    sol = [_dot3(t_inv[n], jnp.concatenate([v[n] * beta[n], k[n] * (beta[n] * e_col[n])], axis=-1)) for n in ids]
    qk = [jnp.where(lower, qk_kk[n][:L] * decay[n], 0.0) for n in ids]
    k_dec_t = [(k[n] * jnp.exp(g_last[n] - g_col[n])).T for n in ids]
    left_s = [jnp.concatenate([sol[n][:, HEAD_DIM:], q[n] * e_col[n]], axis=0).astype(BF16) for n in ids]
    left_v = [jnp.concatenate([qk[n], k_dec_t[n]], axis=0).astype(BF16) for n in ids]

    s = [s_scr[h] for h in heads]
    for c in range(nch):
        n0 = c * GDN_HEADS
        ws_qs = [_dot(left_s[n0 + h], s[h].astype(BF16)) for h in heads]
        v_new = [sol[n0 + h][:, :HEAD_DIM] - ws_qs[h][:L] for h in heads]
        tail = [_dot(left_v[n0 + h], v_new[h].astype(BF16)) for h in heads]
        s = [s[h] * jnp.exp(g_last[n0 + h]) + tail[h][L:] for h in heads]
        for h in heads:
            zf = z_ref[tok(c), h * HEAD_DIM:(h + 1) * HEAD_DIM]
            o = _rms(ws_qs[h][L:] + tail[h][:L], ng_ref[...]) * (zf * jax.nn.sigmoid(zf))
            o_ref[tok(c), h * HEAD_DIM:(h + 1) * HEAD_DIM] = o.astype(o_ref.dtype)
    for h in heads:
        s_scr[h] = s[h]

    @pl.when(step == pl.num_programs(1) - 1)
    def _():
        sfin_ref[0] = s_scr[...]


def _decay_lanes(p):
    return jnp.zeros((1, LANES), F32).at[0, GDN_HEADS:2 * GDN_HEADS].set(p.astype(F32))


def _gdn(main, tail, conv_prev8, s0, conv_w, a_log, dt_bias, norm_g, *, bsz, t):
    L = min(CHUNK, t)
    nch = GDN_CHUNKS_PER_STEP if (t // L) % GDN_CHUNKS_PER_STEP == 0 else 1
    nc = t // (L * nch)
    rows = L * nch
    conv_dim = 3 * GDN_WIDTH
    kern = functools.partial(_gdn_kernel, L=L, nch=nch)
    o, s_fin = pl.pallas_call(
        kern,
        out_shape=(jax.ShapeDtypeStruct((bsz * t, GDN_WIDTH), BF16),
                   jax.ShapeDtypeStruct((bsz, GDN_HEADS, HEAD_DIM, HEAD_DIM), F32)),
        grid=(bsz, nc),
        in_specs=[
            pl.BlockSpec((rows, conv_dim), lambda b, c: (b * nc + c, 0)),
            pl.BlockSpec((rows, GDN_WIDTH), lambda b, c: (b * nc + c, conv_dim // GDN_WIDTH)),
            pl.BlockSpec((rows, LANES), lambda b, c: (b * nc + c, COL_BA_TAIL // LANES)),
            pl.BlockSpec((1, SUBLANES, conv_dim), lambda b, c: (b, 0, 0)),
            pl.BlockSpec((1, GDN_HEADS, HEAD_DIM, HEAD_DIM), lambda b, c: (b, 0, 0, 0)),
            pl.BlockSpec((GDN_CONV, conv_dim), lambda b, c: (0, 0)),
            pl.BlockSpec((1, LANES), lambda b, c: (0, 0)),
            pl.BlockSpec((1, LANES), lambda b, c: (0, 0)),
            pl.BlockSpec((1, HEAD_DIM), lambda b, c: (0, 0)),
        ],
        out_specs=(pl.BlockSpec((rows, GDN_WIDTH), lambda b, c: (b * nc + c, 0)),
                   pl.BlockSpec((1, GDN_HEADS, HEAD_DIM, HEAD_DIM), lambda b, c: (b, 0, 0, 0))),
        scratch_shapes=[pltpu.VMEM((GDN_HEADS, HEAD_DIM, HEAD_DIM), F32),
                        pltpu.VMEM((SUBLANES + rows + SUBLANES, conv_dim), F32)],
        compiler_params=_cparams(("parallel", "arbitrary")),
        name="gdn",
    )(main, main, tail, conv_prev8, s0, conv_w, _decay_lanes(a_log), _decay_lanes(dt_bias),
      norm_g.reshape(1, -1))
    return o, s_fin


def _lambda_value(lam_refs, lam_init):
    lq1, lk1, lq2, lk2 = (r[...] for r in lam_refs)
    return (jnp.exp(jnp.sum(lq1 * lk1, axis=-1, keepdims=True))
            - jnp.exp(jnp.sum(lq2 * lk2, axis=-1, keepdims=True)) + lam_init)


def _softmax_update(m_scr, l_scr, acc_scr, scores, values):
    idx = range(len(scores))
    m_old = [m_scr[n] for n in idx]
    m_new = [jnp.maximum(m_old[n], jnp.max(scores[n], axis=-1, keepdims=True)) for n in idx]
    p = [jnp.exp(scores[n] - m_new[n]) for n in idx]
    alpha = [jnp.exp(m_old[n] - m_new[n]) for n in idx]
    pv = [_dot(p[n].astype(BF16), values[n]) for n in idx]
    for n in idx:
        l_scr[n] = alpha[n] * l_scr[n] + jnp.sum(p[n], axis=-1, keepdims=True)
        acc_scr[n] = alpha[n] * acc_scr[n] + pv[n]
        m_scr[n] = m_new[n]


def _diff_finalize(o1, o2, lam_refs, g_ref, lam_init):
    lam = _lambda_value(lam_refs, lam_init)
    return _rms(o1 - lam * o2, g_ref[...]) * (1.0 - lam_init)


def _diff_init(m_scr, l_scr, acc_scr):
    m_scr[...] = jnp.full(m_scr.shape, -jnp.inf, F32)
    l_scr[...] = jnp.zeros(l_scr.shape, F32)
    acc_scr[...] = jnp.zeros(acc_scr.shape, F32)


def _near_bias(slope, r, q_pos, k_pos):
    visible = jnp.right_shift(k_pos, CHUNK_SHIFT) <= jnp.right_shift(q_pos, CHUNK_SHIFT)
    return jnp.where(visible, slope * (r - jnp.abs(q_pos - k_pos)).astype(F32), -jnp.inf)


def _split_distance(slope, d):
    return (-(slope * CHUNK) * jnp.right_shift(d, CHUNK_SHIFT).astype(F32),
            -slope * jnp.bitwise_and(d, CHUNK - 1).astype(F32))


def _diff_prompt_kernel(i_tab, j_tab, slopes_ref, q_ref, k_ref, v_ref, lq1, lk1, lq2, lk2, g_ref, o_ref,
                        m_scr, l_scr, acc_scr, lhs_scr, s_even, s_odd, *, tq, lam_init):
    h = pl.program_id(1)
    s_id = pl.program_id(2)
    n_pairs = pl.num_programs(2) - 1
    score_on = s_id < n_pairs
    consume_on = s_id >= 1
    i = i_tab[s_id]
    j = j_tab[s_id]
    ci = i_tab[jnp.maximum(s_id - 1, 0)]
    cj = j_tab[jnp.maximum(s_id - 1, 0)]
    slope = slopes_ref[h]
    half = DIFF_HEAD_DIM
    lane = lax.broadcasted_iota(jnp.int32, (tq, LANES), 1)

    @pl.when(jnp.logical_and(consume_on, cj == 0))
    def _():
        _diff_init(m_scr, l_scr, acc_scr)

    @pl.when(jnp.logical_and(score_on, j == 0))
    def _():
        qs = q_ref[...] * (DIFF_HEAD_DIM ** -0.5)
        lhs_scr[0] = jnp.where(lane < half, qs, 1.0).astype(BF16)
        lhs_scr[1] = jnp.where(lane >= half, qs, 1.0).astype(BF16)

    sub = min(DIFF_SUB, tq)
    units = [(n, qb) for qb in range(tq // sub) for n in range(2)]

    def score_tile(near, dst):
        kb = k_ref[...]
        if near:
            aug0 = aug1 = jnp.zeros_like(kb)
        else:
            d = (i - j) * tq - lax.broadcasted_iota(jnp.int32, (tq, LANES), 0)
            t_hi, t_lo = _split_distance(slope, d)
            aug0 = jnp.where(lane == half, t_hi, jnp.where(lane == half + 1, t_lo, 0.0))
            aug1 = jnp.where(lane == 0, t_hi, jnp.where(lane == 1, t_lo, 0.0))
        keys = [jnp.where(lane < half, kb, aug0).astype(BF16), jnp.where(lane >= half, kb, aug1).astype(BF16)]
        bias = None
        for n, qb in units:
            cols = slice(qb * sub, (qb + 1) * sub)
            if near:
                live = (qb + 1) * sub
                if n == 0:
                    k_pos = lax.broadcasted_iota(jnp.int32, (live, sub), 0)
                    q_pos = qb * sub + lax.broadcasted_iota(jnp.int32, (live, sub), 1)
                    bias = _near_bias(slope, q_pos, q_pos, k_pos)
                dst[n, :live, cols] = _dot_nt(keys[n][:live], lhs_scr[n, cols, :]) + bias
                if live < tq:
                    dst[n, live:, cols] = jnp.full((tq - live, sub), -jnp.inf, F32)
            else:
                dst[n, :, cols] = _dot_nt(keys[n], lhs_scr[n, cols, :])

    def consume_tile(src):
        vt = v_ref[...].T.astype(BF16)
        idx = range(len(units))
        cols = [slice(qb * sub, (qb + 1) * sub) for _, qb in units]
        s = [src[n, :, cols[u]] for u, (n, _) in enumerate(units)]
        m_old = [m_scr[n, :, cols[u]] for u, (n, _) in enumerate(units)]
        m_new = [jnp.maximum(m_old[u], jnp.max(s[u], axis=0, keepdims=True)) for u in idx]
        p = [jnp.exp(s[u] - m_new[u]) for u in idx]
        alpha = [jnp.exp(m_old[u] - m_new[u]) for u in idx]
        pv = [_dot(vt, p[u].astype(BF16)) for u in idx]
        for u, (n, _) in enumerate(units):
            l_scr[n, :, cols[u]] = alpha[u] * l_scr[n, :, cols[u]] + jnp.sum(p[u], axis=0, keepdims=True)
            acc_scr[n, :, cols[u]] = alpha[u] * acc_scr[n, :, cols[u]] + pv[u]
            m_scr[n, :, cols[u]] = m_new[u]

    def stage(cond, near, score, consume):
        for parity, (dst, src) in enumerate(((s_even, s_odd), (s_odd, s_even))):
            @pl.when(jnp.logical_and(cond, s_id % 2 == parity))
            def _(dst=dst, src=src):
                if score:
                    score_tile(near, dst)
                if consume:
                    consume_tile(src)

    both = jnp.logical_and(score_on, consume_on)
    stage(jnp.logical_and(both, j < i), False, True, True)
    stage(jnp.logical_and(both, j == i), True, True, True)
    stage(jnp.logical_not(consume_on), True, True, False)
    stage(jnp.logical_not(score_on), False, False, True)

    @pl.when(jnp.logical_and(consume_on, cj == ci))
    def _():
        o_ref[...] = _diff_finalize((acc_scr[0] / l_scr[0]).T, (acc_scr[1] / l_scr[1]).T,
                                    (lq1, lk1, lq2, lk2), g_ref, lam_init).astype(o_ref.dtype)


def _alibi_slopes():
    return jnp.asarray([2.0 ** (-8.0 * (i + 1) / DIFF_HEADS) for i in range(DIFF_HEADS)], F32)


def _diff_prompt(main, k_new, v_new, lams, subln_g, *, bsz, t, lam_init):
    tq = min(DIFF_TQ, t)
    nq = t // tq
    assert tq % CHUNK == 0 and t <= CHUNK * 255
    kern = functools.partial(_diff_prompt_kernel, tq=tq, lam_init=lam_init)
    qc, kc, vc = COL_DQ_MAIN // LANES, 0, 0
    pairs = [(i, j) for i in range(nq) for j in range(i + 1)]
    pairs.append(pairs[-1])
    i_tab = jnp.asarray([p[0] for p in pairs], jnp.int32)
    j_tab = jnp.asarray([p[1] for p in pairs], jnp.int32)
    zero = lambda b, h, s, it, jt: (0, 0)

    def consumed(tab, s):
        return tab[jnp.maximum(s - 1, 0)]

    grid_spec = pltpu.PrefetchScalarGridSpec(
        num_scalar_prefetch=2,
        grid=(bsz, DIFF_HEADS, len(pairs)),
        in_specs=[
            pl.BlockSpec(memory_space=pltpu.SMEM),
            pl.BlockSpec((tq, LANES), lambda b, h, s, it, jt: (b * nq + it[s], qc + h)),
            pl.BlockSpec((tq, LANES), lambda b, h, s, it, jt: (b * nq + jt[s], kc + h)),
            pl.BlockSpec((tq, LANES), lambda b, h, s, it, jt: (b * nq + consumed(jt, s), vc + h)),
        ] + [pl.BlockSpec((1, DIFF_HEAD_DIM), zero)] * 4 + [pl.BlockSpec((1, HEAD_DIM), zero)],
        out_specs=pl.BlockSpec((tq, LANES), lambda b, h, s, it, jt: (b * nq + consumed(it, s), h)),
        scratch_shapes=[pltpu.VMEM((2, 1, tq), F32), pltpu.VMEM((2, 1, tq), F32),
                        pltpu.VMEM((2, HEAD_DIM, tq), F32), pltpu.VMEM((2, tq, LANES), BF16),
                        pltpu.VMEM((2, tq, tq), F32), pltpu.VMEM((2, tq, tq), F32)],
    )
    return pl.pallas_call(
        kern,
        out_shape=jax.ShapeDtypeStruct((bsz * t, DIFF_HEADS * HEAD_DIM), BF16),
        grid_spec=grid_spec,
        compiler_params=_cparams(("parallel", "parallel", "arbitrary")),
        name="diff_attn_prompt",
    )(i_tab, j_tab, _alibi_slopes(), main, k_new, v_new, *lams, subln_g.reshape(1, -1))


def _diff_sample_kernel(slopes_ref, q_ref, kn_ref, vn_ref, kp_ref, vp_ref, lq1, lk1, lq2, lk2, g_ref, o_ref,
                        m_scr, l_scr, acc_scr, lhs_scr, *, t, tk, past, lam_init):
    j = pl.program_id(1)
    heads = range(DIFF_HEADS)

    def cols(h):
        return slice(h * HEAD_DIM, (h + 1) * HEAD_DIM)

    @pl.when(j == 0)
    def _():
        _diff_init(m_scr, l_scr, acc_scr)
        lane = lax.broadcasted_iota(jnp.int32, (t, LANES), 1)
        r = lax.broadcasted_iota(jnp.int32, (t, t), 0)
        c = lax.broadcasted_iota(jnp.int32, (t, t), 1)
        scores = []
        for h in heads:
            qs = q_ref[:, cols(h)] * (DIFF_HEAD_DIM ** -0.5)
            lhs_scr[h, 0:t, :] = jnp.where(lane < DIFF_HEAD_DIM, qs, 0.0).astype(BF16)
            lhs_scr[h, t:2 * t, :] = jnp.where(lane >= DIFF_HEAD_DIM, qs, 0.0).astype(BF16)
            bias = _near_bias(slopes_ref[h], r, past + r, past + c)
            scores.append(_dot_nt(lhs_scr[h], kn_ref[:, cols(h)].astype(BF16)) + jnp.concatenate([bias, bias], axis=0))
        _softmax_update(m_scr, l_scr, acc_scr, scores, [vn_ref[:, cols(h)].astype(BF16) for h in heads])

    d = (past - j * tk - lax.broadcasted_iota(jnp.int32, (1, tk), 1)).astype(F32)
    kt = jnp.swapaxes(kp_ref[...], 0, 1).astype(BF16)
    vt = jnp.swapaxes(vp_ref[...], 0, 1).astype(BF16)
    scores = [_dot_nt(lhs_scr[h], kt[h]) - slopes_ref[h] * d for h in heads]
    _softmax_update(m_scr, l_scr, acc_scr, scores, [vt[h] for h in heads])

    @pl.when(j == pl.num_programs(1) - 1)
    def _():
        for h in heads:
            o = acc_scr[h] / l_scr[h]
            o_ref[:, cols(h)] = _diff_finalize(o[:t], o[t:], (lq1, lk1, lq2, lk2), g_ref, lam_init).astype(o_ref.dtype)


def _diff_sample(main, k_new, v_new, k_past, v_past, lams, subln_g, *, bsz, t, lam_init):
    past = k_past.shape[1]
    tk = min(1024, past)
    nk = past // tk
    width = DIFF_HEADS * HEAD_DIM
    kern = functools.partial(_diff_sample_kernel, t=t, tk=tk, past=past, lam_init=lam_init)
    zero = lambda b, j: (0, 0)
    cache = pl.BlockSpec((None, tk, DIFF_HEADS, HEAD_DIM), lambda b, j: (b, j, 0, 0))
    return pl.pallas_call(
        kern,
        out_shape=jax.ShapeDtypeStruct((bsz * t, width), BF16),
        grid=(bsz, nk),
        in_specs=[
            pl.BlockSpec(memory_space=pltpu.SMEM),
            pl.BlockSpec((t, width), lambda b, j: (b, COL_DQ_MAIN // width)),
            pl.BlockSpec((t, width), lambda b, j: (b, 0)),
            pl.BlockSpec((t, width), lambda b, j: (b, 0)),
            cache, cache,
        ] + [pl.BlockSpec((1, DIFF_HEAD_DIM), zero)] * 4 + [pl.BlockSpec((1, HEAD_DIM), zero)],
        out_specs=pl.BlockSpec((t, width), lambda b, j: (b, 0)),
        scratch_shapes=[pltpu.VMEM((DIFF_HEADS, 2 * t, 1), F32), pltpu.VMEM((DIFF_HEADS, 2 * t, 1), F32),
                        pltpu.VMEM((DIFF_HEADS, 2 * t, HEAD_DIM), F32), pltpu.VMEM((DIFF_HEADS, 2 * t, LANES), BF16)],
        compiler_params=_cparams(("parallel", "arbitrary")),
        name="diff_attn_sample",
    )(_alibi_slopes(), main, k_new, v_new, k_past, v_past, *lams, subln_g.reshape(1, -1))


def _merge_kernel(oa_ref, ob_ref, wa_ref, wb_ref, ga_ref, gb_ref, o_ref):
    ya = _dot(oa_ref[...], wa_ref[...])
    yb = _dot(ob_ref[...], wb_ref[...])
    o_ref[...] = (jax.nn.sigmoid(ga_ref[...]) * ya + jax.nn.sigmoid(gb_ref[...]) * yb).astype(o_ref.dtype)


def _merge(o_a, o_b, wa, wb, tail, *, tm):
    m = o_a.shape[0]
    d = wa.shape[1]
    tn = 512
    ga0, gb0 = 0, COL_GB_TAIL // tn
    return pl.pallas_call(
        _merge_kernel,
        out_shape=jax.ShapeDtypeStruct((m, d), BF16),
        grid=(m // tm, d // tn),
        in_specs=[
            pl.BlockSpec((tm, o_a.shape[1]), lambda i, j: (i, 0)),
            pl.BlockSpec((tm, o_b.shape[1]), lambda i, j: (i, 0)),
            pl.BlockSpec((wa.shape[0], tn), lambda i, j: (0, j)),
            pl.BlockSpec((wb.shape[0], tn), lambda i, j: (0, j)),
            pl.BlockSpec((tm, tn), lambda i, j: (i, ga0 + j)),
            pl.BlockSpec((tm, tn), lambda i, j: (i, gb0 + j)),
        ],
        out_specs=pl.BlockSpec((tm, tn), lambda i, j: (i, j)),
        compiler_params=_cparams(("parallel", "arbitrary")),
        name="merge",
    )(o_a, o_b, wa, wb, tail, tail)


def _memory_proj_kernel(x_ref, g_ref, w_ref, o_ref, h_scr):
    j = pl.program_id(0)

    @pl.when(j == 0)
    def _():
        h_scr[...] = _rms(x_ref[...], g_ref[...]).astype(BF16)

    acc = _dot(h_scr[...], w_ref[...].astype(BF16))
    for h in range(MEM_HEADS):
        @pl.when(j == h)
        def _(h=h):
            o_ref[:, h, :] = acc


def _memory_proj(mem, w, norm_g, *, name):
    m, d = mem.shape
    dh = d // MEM_HEADS
    return pl.pallas_call(
        _memory_proj_kernel,
        out_shape=jax.ShapeDtypeStruct((m, MEM_HEADS, dh), F32),
        grid=(MEM_HEADS,),
        in_specs=[pl.BlockSpec((m, d), lambda j: (0, 0)),
                  pl.BlockSpec((1, d), lambda j: (0, 0)),
                  pl.BlockSpec((d, dh), lambda j: (0, j))],
        out_specs=pl.BlockSpec((m, MEM_HEADS, dh), lambda j: (0, 0, 0)),
        scratch_shapes=[pltpu.VMEM((m, d), BF16)],
        compiler_params=_cparams(("arbitrary",)),
        name=name,
    )(mem, norm_g.reshape(1, d).astype(F32), w)


def _cross_kernel(q_ref, k_ref, v_ref, o_ref):
    dh = q_ref.shape[-1] // MEM_HEADS
    for h in range(MEM_HEADS):
        sl = slice(h * dh, (h + 1) * dh)
        s = _dot_nt(q_ref[:, sl], k_ref[:, h, :].astype(BF16)) * (dh ** -0.5)
        s = s - jnp.max(s, axis=-1, keepdims=True)
        p = jnp.exp(s)
        p = p / jnp.sum(p, axis=-1, keepdims=True)
        o_ref[:, sl] = _dot(p.astype(BF16), v_ref[:, h, :].astype(BF16)).astype(o_ref.dtype)


def _cross_attend(q, mem_k, mem_v, *, bsz, t, tm):
    d = q.shape[1]
    nt = t // tm
    _, mt, nh, dh = mem_k.shape
    mem_spec = pl.BlockSpec((None, mt, nh, dh), lambda b, i: (b, 0, 0, 0))
    return pl.pallas_call(
        _cross_kernel,
        out_shape=jax.ShapeDtypeStruct(q.shape, BF16),
        grid=(bsz, nt),
        in_specs=[pl.BlockSpec((tm, d), lambda b, i: (b * nt + i, 0)), mem_spec, mem_spec],
        out_specs=pl.BlockSpec((tm, d), lambda b, i: (b * nt + i, 0)),
        compiler_params=_cparams(("parallel", "arbitrary")),
        name="cross_attn",
    )(q, mem_k, mem_v)


def _topk_rows(work, cidx, k):
    n, tn = work.shape
    rid = lax.broadcasted_iota(jnp.int32, (n, tn), 0).astype(F32)
    kid = lax.broadcasted_iota(jnp.int32, (k, tn), 0)
    vals = jnp.zeros((k, tn), F32)
    idxs = jnp.zeros((k, tn), F32)
    for t in range(k):
        m = jnp.max(work, axis=0, keepdims=True)
        pos = jnp.min(jnp.where(work == m, rid, float(n)), axis=0, keepdims=True)
        hit = rid == pos
        if cidx is None:
            picked = pos
        else:
            picked = jnp.sum(jnp.where(hit, cidx, 0.0), axis=0, keepdims=True)
        vals = jnp.where(kid == t, m, vals)
        idxs = jnp.where(kid == t, picked, idxs)
        work = jnp.where(hit, -jnp.inf, work)
    return vals, idxs


def _product_candidates(v1, i1, v2, i2, k):
    assert k == 2 * SUBLANES
    sub = lax.broadcasted_iota(jnp.int32, (SUBLANES, v1.shape[1]), 0)
    vals, idxs = [], []
    for a in range(k // 2):
        nb = k // (a + 1)
        width = k if nb > SUBLANES else SUBLANES
        cv = v1[a:a + 1, :] + v2[:width, :]
        ci = i1[a:a + 1, :] * PEER_N_KEYS + i2[:width, :]
        if nb < SUBLANES:
            cv = jnp.where(sub < nb, cv, -jnp.inf)
        vals.append(cv)
        idxs.append(ci)
    vals.append(v1[k // 2:, :] + v2[0:1, :])
    idxs.append(i1[k // 2:, :] * PEER_N_KEYS + i2[0:1, :])
    return jnp.concatenate(vals, axis=0), jnp.concatenate(idxs, axis=0)


def _peer_route_kernel(q_ref, keys_hi_ref, keys_lo_ref, e_ref, g_ref):
    k = PEER_TOPK
    for h in range(PEER_HEADS):
        sub = []
        for p in range(2):
            c0 = (h * 2 + p) * PEER_N_KEYS
            q_parts = _split_bf16(q_ref[:, c0:c0 + PEER_N_KEYS])
            s = _dot3_parts((keys_hi_ref[h, p], keys_lo_ref[h, p]), q_parts, _dot_nt)
            sub.append(_topk_rows(s, None, k))
        (v1, i1), (v2, i2) = sub
        cand, cidx = _product_candidates(v1, i1, v2, i2, k)
        best, eidx = _topk_rows(cand, cidx, k)
        ex = jnp.exp(best - jnp.max(best, axis=0, keepdims=True))
        gate = ex / jnp.sum(ex, axis=0, keepdims=True)
        e_ref[h * k:(h + 1) * k, :] = eidx.astype(jnp.int32)
        g_ref[h * k:(h + 1) * k, :] = gate


HALF_KEYS = PEER_N_KEYS // 2
PEER_GATE_TN = 256
BF16_HIGH_MASK = 0xFFFF0000


def _peer_gate_kernel(et_ref, gt_ref, o_ref, e_scr, g_scr):
    tg = et_ref.shape[1]
    e_scr[...] = et_ref[...].T
    g_scr[...] = gt_ref[...].T
    kid = lax.broadcasted_iota(jnp.int32, (PEER_N_KEYS, PEER_N_KEYS), 0)

    def body(n8, carry):
        rows = pl.ds(pl.multiple_of(n8 * SUBLANES, SUBLANES), SUBLANES)
        e8 = e_scr[rows, :]
        g8 = g_scr[rows, :]
        a8 = jnp.right_shift(e8, 7)
        b8 = jnp.bitwise_and(e8, PEER_N_KEYS - 1)
        toks = range(SUBLANES)
        at = [jnp.where(a8[s:s + 1, :] == kid, g8[s:s + 1, :], 0.0).astype(BF16) for s in toks]
        bt = [jnp.where(b8[s:s + 1, :] == kid, 1.0, 0.0).astype(BF16) for s in toks]
        grids = [_dot_nt(at[s], bt[s]) for s in toks]
        bits = [pltpu.bitcast(grids[s].astype(BF16).astype(F32), jnp.uint32) for s in toks]
        words = [jnp.bitwise_or(jnp.bitwise_and(bits[s][HALF_KEYS:, :], jnp.uint32(BF16_HIGH_MASK)),
                                jnp.right_shift(bits[s][:HALF_KEYS, :], jnp.uint32(16))) for s in toks]
        o_ref[:, rows, :] = jnp.swapaxes(jnp.stack(words, axis=0), 0, 1)
        return carry

    lax.fori_loop(0, tg // SUBLANES, body, 0, unroll=8)


def _peer_route(qp, sub_keys, *, tn):
    n = qp.shape[0]
    rows = PEER_HEADS * PEER_TOPK
    keys_hi, keys_lo = _split_bf16(sub_keys.astype(F32))
    keys_spec = pl.BlockSpec(sub_keys.shape, lambda i: (0, 0, 0, 0))
    e_t, g_t = pl.pallas_call(
        _peer_route_kernel,
        out_shape=(jax.ShapeDtypeStruct((rows, n), jnp.int32), jax.ShapeDtypeStruct((rows, n), F32)),
        grid=(n // tn,),
        in_specs=[pl.BlockSpec((tn, qp.shape[1]), lambda i: (i, 0)), keys_spec, keys_spec],
        out_specs=(pl.BlockSpec((rows, tn), lambda i: (0, i)), pl.BlockSpec((rows, tn), lambda i: (0, i))),
        compiler_params=_cparams(("parallel",)),
        name="peer_route",
    )(qp, keys_hi, keys_lo)
    tg = min(PEER_GATE_TN, n)
    return pl.pallas_call(
        _peer_gate_kernel,
        out_shape=jax.ShapeDtypeStruct((HALF_KEYS, n, PEER_N_KEYS), jnp.uint32),
        grid=(n // tg,),
        in_specs=[pl.BlockSpec((rows, tg), lambda i: (0, i)), pl.BlockSpec((rows, tg), lambda i: (0, i))],
        out_specs=pl.BlockSpec((HALF_KEYS, tg, PEER_N_KEYS), lambda i: (0, i, 0)),
        scratch_shapes=[pltpu.VMEM((tg, rows), jnp.int32), pltpu.VMEM((tg, rows), F32)],
        compiler_params=_cparams(("parallel",)),
        name="peer_gates",
    )(e_t, g_t)


def _gelu_tanh(x):
    return 0.5 * x * (1.0 + jnp.tanh(math.sqrt(2.0 / math.pi) * (x + 0.044715 * (x * x * x))))


PEER_BLOCKS_PER_STEP = 4
PEER_TN = 1024
PEER_VMEM_LIMIT = 60 * 1024 * 1024


def _peer_expert_kernel(x_ref, gn_ref, gpa_ref, gpb_ref, *rest):
    nblk = PEER_BLOCKS_PER_STEP
    u_refs, v_refs = rest[:nblk], rest[nblk:2 * nblk]
    gf_ref, o_ref, h_scr, acc_scr, coef_scr = rest[2 * nblk:]
    s = pl.program_id(1)
    last = pl.num_programs(1) - 1

    @pl.when(s == 0)
    def _():
        h_scr[...] = _rms(x_ref[...], gn_ref[...]).astype(BF16)
        acc_scr[...] = jnp.zeros(acc_scr.shape, F32)
        coef_scr[...] = jnp.zeros(coef_scr.shape, BF16)

    v_cat = jnp.concatenate([v[...].astype(BF16) for v in v_refs], axis=0)
    acc_scr[...] += _dot(coef_scr[(s + 1) % 2], v_cat)

    def unpack(word):
        return (pltpu.bitcast(jnp.left_shift(word, jnp.uint32(16)), F32),
                pltpu.bitcast(jnp.bitwise_and(word, jnp.uint32(BF16_HIGH_MASK)), F32))

    a_lo, a_hi = unpack(gpa_ref[...])
    b_lo, b_hi = unpack(gpb_ref[...])
    u_cat = jnp.concatenate([u[...].astype(BF16) for u in u_refs], axis=0)
    hid = _gelu_tanh(_dot_nt(h_scr[...], u_cat))
    coef_scr[s % 2] = (jnp.concatenate([a_lo, b_lo, a_hi, b_hi], axis=1) * hid).astype(BF16)

    @pl.when(s == last)
    def _():
        o_ref[...] = _rms(x_ref[...] + acc_scr[...], gf_ref[...])


def _peer_experts(x, norm_g, gates, u_tab, v_tab, final_g, *, tn):
    n, d = x.shape
    eb = PEER_N_KEYS
    assert u_tab.shape[0] == eb * eb
    nblk = PEER_BLOCKS_PER_STEP
    ns = eb // nblk

    def cur(s):
        return jnp.minimum(s, ns - 1)

    def prev(s):
        return jnp.maximum(s - 1, 0)

    def table(which, q):
        return pl.BlockSpec((eb, d), lambda i, s: (which(s) + q * ns, 0))

    return pl.pallas_call(
        _peer_expert_kernel,
        out_shape=jax.ShapeDtypeStruct((n, d), F32),
        grid=(n // tn, ns + 1),
        in_specs=[
            pl.BlockSpec((tn, d), lambda i, s: (i, 0), pipeline_mode=pl.Buffered(1)),
            pl.BlockSpec((1, d), lambda i, s: (0, 0)),
            pl.BlockSpec((None, tn, eb), lambda i, s: (cur(s), i, 0)),
            pl.BlockSpec((None, tn, eb), lambda i, s: (cur(s) + ns, i, 0)),
        ] + [table(cur, q) for q in range(nblk)] + [table(prev, q) for q in range(nblk)] + [
            pl.BlockSpec((1, d), lambda i, s: (0, 0)),
        ],
        out_specs=pl.BlockSpec((tn, d), lambda i, s: (i, 0), pipeline_mode=pl.Buffered(1)),
        scratch_shapes=[pltpu.VMEM((tn, d), BF16), pltpu.VMEM((tn, d), F32),
                        pltpu.VMEM((2, tn, nblk * eb), BF16)],
        compiler_params=_cparams(("parallel", "arbitrary"), vmem=PEER_VMEM_LIMIT),
        name="peer_experts",
    )(x, norm_g.reshape(1, d), gates, gates, *([u_tab] * nblk), *([v_tab] * nblk), final_g.reshape(1, d))


def _layer(x, wts, layer, mem_k, mem_v, conv_prev, gdn_state, k_past, v_past, *, tm):
    bsz, t, d = x.shape
    n = bsz * t
    x2d = x.reshape(n, d)
    main, k_new, v_new, tail = _in_proj(x2d, wts["norm_mix_g"], wts["w_in"], tm=min(IN_TM, n))
    conv_dim = 3 * GDN_WIDTH
    conv_prev8 = jnp.concatenate(
        [jnp.zeros((bsz, SUBLANES - (GDN_CONV - 1), conv_dim), F32), conv_prev.astype(F32)], axis=1)
    o_a, s_new = _gdn(main, tail, conv_prev8, gdn_state.astype(F32), wts["gdn_conv_w"], wts["gdn_a_log"],
                      wts["gdn_dt_bias"], wts["gdn_norm_g"], bsz=bsz, t=t)
    conv_new = main.reshape(bsz, t, -1)[:, t - (GDN_CONV - 1):, :conv_dim]
    lam_init = 0.8 - 0.6 * math.exp(-0.3 * layer)
    lams = tuple(wts[k].reshape(1, -1) for k in ("diff_lambda_q1", "diff_lambda_k1", "diff_lambda_q2", "diff_lambda_k2"))
    if k_past is None:
        o_b = _diff_prompt(main, k_new, v_new, lams, wts["diff_subln_g"], bsz=bsz, t=t, lam_init=lam_init)
    else:
        o_b = _diff_sample(main, k_new, v_new, k_past, v_past, lams, wts["diff_subln_g"], bsz=bsz, t=t,
                           lam_init=lam_init)
    k_rows = k_new.reshape(bsz, t, DIFF_HEADS, HEAD_DIM)
    v_rows = v_new.reshape(bsz, t, DIFF_HEADS, HEAD_DIM)
    merged = _merge(o_a, o_b, wts["w_branch_a"], wts["w_branch_b"], tail, tm=tm)
    x1 = _matmul(merged, wts["w_out"], residual=x2d, tm=tm, tn=MM_TN, name="out_proj")
    qm = _matmul(x1, wts["w_mq"], norm_g=wts["norm_cross_g"], tm=tm, tn=MM_TN, out_dtype=BF16, name="mem_q")
    oc = _cross_attend(qm, mem_k, mem_v, bsz=bsz, t=t, tm=min(tm, t))
    x2 = _matmul(oc, wts["w_mo"], residual=x1, tm=tm, tn=MM_TN, name="mem_o")
    qp = _matmul(x2, wts["peer_w_q"], norm_g=wts["norm_ffn_g"], tm=tm, tn=MM_TN, name="peer_q")
    tn = min(PEER_TN, n)
    gates = _peer_route(qp, wts["peer_sub_keys"], tn=LANES)
    y = _peer_experts(x2, wts["norm_ffn_g"], gates, wts["peer_u"], wts["peer_v"], wts["final_norm_g"], tn=tn)
    return y.reshape(bsz, t, d), k_rows, v_rows, s_new, conv_new


def kernel(x_prompt, x_sample, cache_diff_k, cache_diff_v, state_gdn, state_conv, cache_mem_k, cache_mem_v,
           mem_prompt, norm_mix_g, w_in, gdn_conv_w, gdn_a_log, gdn_dt_bias, gdn_norm_g,
           diff_lambda_q1, diff_lambda_k1, diff_lambda_q2, diff_lambda_k2, diff_subln_g,
           w_branch_a, w_branch_b, w_out, norm_cross_g, norm_mem_g, w_mq, w_mk, w_mv, w_mo,
           norm_ffn_g, peer_w_q, peer_sub_keys, peer_u, peer_v, final_norm_g):
    depth = w_in.shape[0]
    assert depth == 1, "final norm is fused into the last layer's PEER kernel"
    l = 0
    wts = {
        "norm_mix_g": norm_mix_g[l], "w_in": _in_proj_weights(w_in[l]), "gdn_conv_w": gdn_conv_w[l],
        "gdn_a_log": gdn_a_log[l], "gdn_dt_bias": gdn_dt_bias[l], "gdn_norm_g": gdn_norm_g[l],
        "diff_lambda_q1": diff_lambda_q1[l], "diff_lambda_k1": diff_lambda_k1[l],
        "diff_lambda_q2": diff_lambda_q2[l], "diff_lambda_k2": diff_lambda_k2[l],
        "diff_subln_g": diff_subln_g[l], "w_branch_a": w_branch_a[l].astype(BF16),
        "w_branch_b": w_branch_b[l].astype(BF16), "w_out": w_out[l].astype(BF16),
        "norm_cross_g": norm_cross_g[l], "w_mq": w_mq[l].astype(BF16), "w_mo": w_mo[l].astype(BF16),
        "norm_ffn_g": norm_ffn_g[l], "peer_w_q": peer_w_q[l].astype(BF16), "peer_sub_keys": peer_sub_keys[l],
        "peer_u": peer_u[l], "peer_v": peer_v[l], "final_norm_g": final_norm_g,
    }
    bp, tp, d = x_prompt.shape
    bs, ts, _ = x_sample.shape
    mem2d = mem_prompt.reshape(-1, d)
    mshape = mem_prompt.shape[:2] + (MEM_HEADS, d // MEM_HEADS)
    mem_k = _memory_proj(mem2d, w_mk[l], norm_mem_g[l], name="mem_k").reshape(mshape)
    mem_v = _memory_proj(mem2d, w_mv[l], norm_mem_g[l], name="mem_v").reshape(mshape)
    conv0 = jnp.zeros((bp, GDN_CONV - 1, 3 * GDN_WIDTH), F32)
    s0 = jnp.zeros((bp, GDN_HEADS, HEAD_DIM, HEAD_DIM), F32)
    yp, pk, pv, ps, pc = _layer(x_prompt, wts, l, mem_k, mem_v, conv0, s0, None, None, tm=min(1024, bp * tp))
    ys, sk, sv, ss, sc = _layer(x_sample, wts, l, cache_mem_k[l], cache_mem_v[l], state_conv[l], state_gdn[l],
                                cache_diff_k[l], cache_diff_v[l], tm=min(256, bs * ts))
    return (yp, ys, pk[None], pv[None], ps[None], pc[None], mem_k[None], mem_v[None],
            sk[None], sv[None], ss[None], sc[None])
```

```python
import functools
import math

import jax
import jax.numpy as jnp
from jax import lax
from jax.experimental import pallas as pl
from jax.experimental.pallas import tpu as pltpu

F32 = jnp.float32
BF16 = jnp.bfloat16
HIGHEST = lax.Precision.HIGHEST

RMS_EPS = 1e-6
CHUNK = 64
CHUNK_SHIFT = 6
DIFF_TQ = 1024
GDN_CHUNKS_PER_STEP = 4
DIFF_SUB = 256
GDN_HEADS = 8
HEAD_DIM = 128
GDN_WIDTH = GDN_HEADS * HEAD_DIM
GDN_CONV = 4
DIFF_HEADS = 8
DIFF_HEAD_DIM = 64
MEM_HEADS = 4
PEER_HEADS = 8
PEER_N_KEYS = 128
PEER_TOPK = 16
LANES = 128
SUBLANES = 8
V7X_VMEM_LIMIT = 52 * 1024 * 1024
MM_TN = 1024


def _cparams(sem, vmem=V7X_VMEM_LIMIT):
    return pltpu.CompilerParams(dimension_semantics=sem, vmem_limit_bytes=vmem)


def _dot(a, b, precision=None):
    return jnp.dot(a, b, preferred_element_type=F32, precision=precision)


def _dot_nt(a, b, precision=None):
    return lax.dot_general(a, b, (((1,), (1,)), ((), ())), preferred_element_type=F32, precision=precision)


def _split_bf16(x):
    hi = x.astype(BF16)
    return hi, (x - hi.astype(F32)).astype(BF16)


def _cat_parts(p, q):
    return jnp.concatenate([p[0], q[0]], axis=0), jnp.concatenate([p[1], q[1]], axis=0)


def _dot3_parts(a_parts, b_parts, dot):
    a_hi, a_lo = a_parts
    b_hi, b_lo = b_parts
    m = a_hi.shape[0]
    both = dot(jnp.concatenate([a_hi, a_lo], axis=0), b_hi)
    return both[:m] + both[m:] + dot(a_hi, b_lo)


def _dot3(a, b):
    return _dot3_parts(_split_bf16(a), _split_bf16(b), _dot)


def _rms(xf, g):
    return xf * lax.rsqrt(jnp.mean(xf * xf, axis=-1, keepdims=True) + RMS_EPS) * g


def _mm_kernel(*refs, has_norm, has_res):
    it = iter(refs)
    x_ref = next(it)
    g_ref = next(it) if has_norm else None
    w_ref = next(it)
    r_ref = next(it) if has_res else None
    o_ref = next(it)
    h_scr = next(it)

    @pl.when(pl.program_id(1) == 0)
    def _():
        xf = x_ref[...].astype(F32)
        if has_norm:
            xf = _rms(xf, g_ref[...])
        h_scr[...] = xf.astype(BF16)

    acc = _dot(h_scr[...], w_ref[...].astype(BF16))
    if has_res:
        acc = acc + r_ref[...]
    o_ref[...] = acc.astype(o_ref.dtype)


def _matmul(x, w, *, norm_g=None, residual=None, tm, tn, out_dtype=F32, name="matmul"):
    m, k = x.shape
    n = w.shape[1]
    assert m % tm == 0 and n % tn == 0, (m, n, tm, tn)
    in_specs = [pl.BlockSpec((tm, k), lambda i, j: (i, 0))]
    args = [x]
    if norm_g is not None:
        in_specs.append(pl.BlockSpec((1, k), lambda i, j: (0, 0)))
        args.append(norm_g.reshape(1, k).astype(F32))
    in_specs.append(pl.BlockSpec((k, tn), lambda i, j: (0, j)))
    args.append(w)
    if residual is not None:
        in_specs.append(pl.BlockSpec((tm, tn), lambda i, j: (i, j)))
        args.append(residual)
    return pl.pallas_call(
        functools.partial(_mm_kernel, has_norm=norm_g is not None, has_res=residual is not None),
        out_shape=jax.ShapeDtypeStruct((m, n), out_dtype),
        grid=(m // tm, n // tn),
        in_specs=in_specs,
        out_specs=pl.BlockSpec((tm, tn), lambda i, j: (i, j)),
        scratch_shapes=[pltpu.VMEM((tm, k), BF16)],
        compiler_params=_cparams(("parallel", "arbitrary")),
        name=name,
    )(*args)


IN_TN = 512
IN_TM = 1024
IN_W1_TILES = 8
IN_W2_TILES = 14
IN_MAIN_TILES = 10
IN_K_TILE0 = 10
IN_V_TILE0 = 12
IN_TAIL_TILE0 = 14
IN_TILES = IN_W1_TILES + IN_W2_TILES + 1
MAIN_COLS = IN_MAIN_TILES * IN_TN
TAIL_COLS = (IN_TILES - IN_TAIL_TILE0) * IN_TN
COL_DQ_MAIN = 4096
COL_GB_TAIL = 2048
COL_BA_TAIL = 4096


def _in_proj_kernel(x_ref, g_ref, w1_ref, w2_ref, w3_ref, main_ref, k_ref, v_ref, tail_ref, h_scr):
    j = pl.program_id(1)

    @pl.when(j == 0)
    def _():
        h_scr[...] = _rms(x_ref[...], g_ref[...]).astype(BF16)

    routes = ((0, IN_W1_TILES, w1_ref, main_ref),
              (IN_W1_TILES, IN_K_TILE0, w2_ref, main_ref),
              (IN_K_TILE0, IN_V_TILE0, w2_ref, k_ref),
              (IN_V_TILE0, IN_TAIL_TILE0, w2_ref, v_ref),
              (IN_TAIL_TILE0, IN_TILES - 1, w2_ref, tail_ref),
              (IN_TILES - 1, IN_TILES, w3_ref, tail_ref))
    for lo, hi, w_ref, dst in routes:
        @pl.when(jnp.logical_and(j >= lo, j < hi))
        def _(w_ref=w_ref, dst=dst):
            dst[...] = _dot(h_scr[...], w_ref[...])


def _in_proj_weights(w_in):
    c_z = 4 * GDN_WIDTH
    c_ba = c_z + 2 * GDN_HEADS
    assert c_z == IN_W1_TILES * IN_TN and w_in.shape[1] - c_ba == IN_W2_TILES * IN_TN
    w_bf16 = w_in.astype(BF16)
    return w_bf16, w_bf16[:, c_ba:]


def _in_proj(x, norm_g, w_pieces, *, tm):
    m, d = x.shape
    w_all, w2 = w_pieces
    w1 = w3 = w_all
    tn = IN_TN
    out = lambda cols: jax.ShapeDtypeStruct((m, cols), F32)
    return pl.pallas_call(
        _in_proj_kernel,
        out_shape=(out(MAIN_COLS), out(2 * IN_TN), out(2 * IN_TN), out(TAIL_COLS)),
        grid=(m // tm, IN_TILES),
        in_specs=[
            pl.BlockSpec((tm, d), lambda i, j: (i, 0)),
            pl.BlockSpec((1, d), lambda i, j: (0, 0)),
            pl.BlockSpec((d, tn), lambda i, j: (0, jnp.minimum(j, IN_W1_TILES - 1))),
            pl.BlockSpec((d, tn), lambda i, j: (0, jnp.clip(j - IN_W1_TILES, 0, IN_W2_TILES - 1))),
            pl.BlockSpec((d, tn), lambda i, j: (0, IN_W1_TILES)),
        ],
        out_specs=(
            pl.BlockSpec((tm, tn), lambda i, j: (i, jnp.minimum(j, IN_MAIN_TILES - 1))),
            pl.BlockSpec((tm, tn), lambda i, j: (i, jnp.clip(j - IN_K_TILE0, 0, 1))),
            pl.BlockSpec((tm, tn), lambda i, j: (i, jnp.clip(j - IN_V_TILE0, 0, 1))),
            pl.BlockSpec((tm, tn), lambda i, j: (i, jnp.clip(j - IN_TAIL_TILE0, 0, IN_TILES - IN_TAIL_TILE0 - 1))),
        ),
        scratch_shapes=[pltpu.VMEM((tm, d), BF16)],
        compiler_params=_cparams(("parallel", "arbitrary")),
        name="in_proj",
    )(x, norm_g.reshape(1, d).astype(F32), w1, w2, w3)


def _gdn_kernel(qkv_ref, z_ref, ba_ref, prev_ref, s0_ref, convw_ref, alog_ref, dtb_ref, ng_ref,
                o_ref, sfin_ref, s_scr, xbuf, *, L, nch):
    step = pl.program_id(1)
    rows = nch * L

    @pl.when(step == 0)
    def _():
        s_scr[...] = s0_ref[0]
        xbuf[0:SUBLANES, :] = prev_ref[0]

    xbuf[SUBLANES:SUBLANES + rows, :] = qkv_ref[...]
    base = SUBLANES - (GDN_CONV - 1)
    y = xbuf[base:base + rows, :] * convw_ref[0:1, :]
    for i in range(1, GDN_CONV):
        y = y + xbuf[base + i:base + i + rows, :] * convw_ref[i:i + 1, :]
    y = y * jax.nn.sigmoid(y)
    carry = xbuf[rows:rows + SUBLANES, :]
    xbuf[0:SUBLANES, :] = carry

    row = lax.broadcasted_iota(jnp.int32, (L, L), 0)
    col = lax.broadcasted_iota(jnp.int32, (L, L), 1)
    lower = row >= col
    strict = row > col
    eye = (row == col).astype(F32)
    log_l = int(math.log2(L))
    n_double = log_l - 1
    assert 2 ** log_l == L

    ba = ba_ref[...]
    beta_all = jax.nn.sigmoid(ba)
    a_in = ba + dtb_ref[...]
    softplus = jnp.maximum(a_in, 0.0) + jnp.log(1.0 + jnp.exp(-jnp.abs(a_in)))
    g_all = -jnp.exp(alog_ref[...]) * softplus
    rr = lax.broadcasted_iota(jnp.int32, (rows, rows), 0)
    cc = lax.broadcasted_iota(jnp.int32, (rows, rows), 1)
    same_chunk_lower = jnp.logical_and(rr >= cc, jnp.right_shift(rr, log_l) == jnp.right_shift(cc, log_l))
    g_cum = _dot(same_chunk_lower.astype(F32), g_all, HIGHEST)
    g_cum_t = g_cum.T

    heads = range(GDN_HEADS)
    chains = [(c, h) for c in range(nch) for h in heads]
    ids = range(len(chains))

    def tok(c):
        return slice(c * L, (c + 1) * L)

    def head_cols(part, c, h):
        return y[tok(c), part * GDN_WIDTH + h * HEAD_DIM:part * GDN_WIDTH + (h + 1) * HEAD_DIM]

    q = [head_cols(0, c, h) for c, h in chains]
    k = [head_cols(1, c, h) for c, h in chains]
    v = [head_cols(2, c, h) for c, h in chains]
    q = [a * lax.rsqrt(jnp.sum(a * a, axis=-1, keepdims=True) + 1e-6) * (HEAD_DIM ** -0.5) for a in q]
    k = [a * lax.rsqrt(jnp.sum(a * a, axis=-1, keepdims=True) + 1e-6) for a in k]
    beta = [beta_all[tok(c), h:h + 1] for c, h in chains]
    g_col = [g_cum[tok(c), GDN_HEADS + h:GDN_HEADS + h + 1] for c, h in chains]
    g_row = [g_cum_t[GDN_HEADS + h:GDN_HEADS + h + 1, tok(c)] for c, h in chains]
    decay = [jnp.where(lower, jnp.exp(jnp.where(lower, g_col[n] - g_row[n], 0.0)), 0.0) for n in ids]
    g_last = [g_col[n][L - 1:L, :] for n in ids]
    e_col = [jnp.exp(g_col[n]) for n in ids]
    qk_kk = [_dot_nt(jnp.concatenate([q[n], k[n]], axis=0).astype(BF16), k[n].astype(BF16)) for n in ids]
    x = [-jnp.where(strict, beta[n] * qk_kk[n][L:] * decay[n], 0.0) for n in ids]
    t_inv = [eye + x[n] for n in ids]
    x_parts = [_split_bf16(x[n]) for n in ids]
    xp = [_dot3_parts(x_parts[n], x_parts[n], _dot) for n in ids]
    for _ in range(n_double - 1):
        xp_parts = [_split_bf16(xp[n]) for n in ids]
        both = [_dot3_parts(_cat_parts(_split_bf16(t_inv[n]), xp_parts[n]), xp_parts[n], _dot) for n in ids]
        t_inv = [t_inv[n] + both[n][:L] for n in ids]
        xp = [both[n][L:] for n in ids]
    t_inv = [t_inv[n] + _dot3(t_inv[n], xp[n]) for n in ids]
    sol = [_dot3(t_inv[n], jnp.concatenate([v[n] * beta[n], k[n] * (beta[n] * e_col[n])], axis=-1)) for n in ids]
    qk = [jnp.where(lower, qk_kk[n][:L] * decay[n], 0.0) for n in ids]
    k_dec_t = [(k[n] * jnp.exp(g_last[n] - g_col[n])).T for n in ids]
    left_s = [jnp.concatenate([sol[n][:, HEAD_DIM:], q[n] * e_col[n]], axis=0).astype(BF16) for n in ids]
    left_v = [jnp.concatenate([qk[n], k_dec_t[n]], axis=0).astype(BF16) for n in ids]

    s = [s_scr[h] for h in heads]
    for c in range(nch):
        n0 = c * GDN_HEADS
        ws_qs = [_dot(left_s[n0 + h], s[h].astype(BF16)) for h in heads]
        v_new = [sol[n0 + h][:, :HEAD_DIM] - ws_qs[h][:L] for h in heads]
        tail = [_dot(left_v[n0 + h], v_new[h].astype(BF16)) for h in heads]
        s = [s[h] * jnp.exp(g_last[n0 + h]) + tail[h][L:] for h in heads]
        for h in heads:
            zf = z_ref[tok(c), h * HEAD_DIM:(h + 1) * HEAD_DIM]
            o = _rms(ws_qs[h][L:] + tail[h][:L], ng_ref[...]) * (zf * jax.nn.sigmoid(zf))
            o_ref[tok(c), h * HEAD_DIM:(h + 1) * HEAD_DIM] = o.astype(o_ref.dtype)
    for h in heads:
        s_scr[h] = s[h]

    @pl.when(step == pl.num_programs(1) - 1)
    def _():
        sfin_ref[0] = s_scr[...]


def _decay_lanes(p):
    return jnp.zeros((1, LANES), F32).at[0, GDN_HEADS:2 * GDN_HEADS].set(p.astype(F32))


def _gdn(main, tail, conv_prev8, s0, conv_w, a_log, dt_bias, norm_g, *, bsz, t):
    L = min(CHUNK, t)
    nch = GDN_CHUNKS_PER_STEP if (t // L) % GDN_CHUNKS_PER_STEP == 0 else 1
    nc = t // (L * nch)
    rows = L * nch
    conv_dim = 3 * GDN_WIDTH
    kern = functools.partial(_gdn_kernel, L=L, nch=nch)
    o, s_fin = pl.pallas_call(
        kern,
        out_shape=(jax.ShapeDtypeStruct((bsz * t, GDN_WIDTH), BF16),
                   jax.ShapeDtypeStruct((bsz, GDN_HEADS, HEAD_DIM, HEAD_DIM), F32)),
        grid=(bsz, nc),
        in_specs=[
            pl.BlockSpec((rows, conv_dim), lambda b, c: (b * nc + c, 0)),
            pl.BlockSpec((rows, GDN_WIDTH), lambda b, c: (b * nc + c, conv_dim // GDN_WIDTH)),
            pl.BlockSpec((rows, LANES), lambda b, c: (b * nc + c, COL_BA_TAIL // LANES)),
            pl.BlockSpec((1, SUBLANES, conv_dim), lambda b, c: (b, 0, 0)),
            pl.BlockSpec((1, GDN_HEADS, HEAD_DIM, HEAD_DIM), lambda b, c: (b, 0, 0, 0)),
            pl.BlockSpec((GDN_CONV, conv_dim), lambda b, c: (0, 0)),
            pl.BlockSpec((1, LANES), lambda b, c: (0, 0)),
            pl.BlockSpec((1, LANES), lambda b, c: (0, 0)),
            pl.BlockSpec((1, HEAD_DIM), lambda b, c: (0, 0)),
        ],
        out_specs=(pl.BlockSpec((rows, GDN_WIDTH), lambda b, c: (b * nc + c, 0)),
                   pl.BlockSpec((1, GDN_HEADS, HEAD_DIM, HEAD_DIM), lambda b, c: (b, 0, 0, 0))),
        scratch_shapes=[pltpu.VMEM((GDN_HEADS, HEAD_DIM, HEAD_DIM), F32),
                        pltpu.VMEM((SUBLANES + rows + SUBLANES, conv_dim), F32)],
        compiler_params=_cparams(("parallel", "arbitrary")),
        name="gdn",
    )(main, main, tail, conv_prev8, s0, conv_w, _decay_lanes(a_log), _decay_lanes(dt_bias),
      norm_g.reshape(1, -1))
    return o, s_fin


def _lambda_value(lam_refs, lam_init):
    lq1, lk1, lq2, lk2 = (r[...] for r in lam_refs)
    return (jnp.exp(jnp.sum(lq1 * lk1, axis=-1, keepdims=True))
            - jnp.exp(jnp.sum(lq2 * lk2, axis=-1, keepdims=True)) + lam_init)


def _softmax_update(m_scr, l_scr, acc_scr, scores, values):
    idx = range(len(scores))
    m_old = [m_scr[n] for n in idx]
    m_new = [jnp.maximum(m_old[n], jnp.max(scores[n], axis=-1, keepdims=True)) for n in idx]
    p = [jnp.exp(scores[n] - m_new[n]) for n in idx]
    alpha = [jnp.exp(m_old[n] - m_new[n]) for n in idx]
    pv = [_dot(p[n].astype(BF16), values[n]) for n in idx]
    for n in idx:
        l_scr[n] = alpha[n] * l_scr[n] + jnp.sum(p[n], axis=-1, keepdims=True)
        acc_scr[n] = alpha[n] * acc_scr[n] + pv[n]
        m_scr[n] = m_new[n]


def _diff_finalize(o1, o2, lam_refs, g_ref, lam_init):
    lam = _lambda_value(lam_refs, lam_init)
    return _rms(o1 - lam * o2, g_ref[...]) * (1.0 - lam_init)


def _diff_init(m_scr, l_scr, acc_scr):
    m_scr[...] = jnp.full(m_scr.shape, -jnp.inf, F32)
    l_scr[...] = jnp.zeros(l_scr.shape, F32)
    acc_scr[...] = jnp.zeros(acc_scr.shape, F32)


def _near_bias(slope, r, q_pos, k_pos):
    visible = jnp.right_shift(k_pos, CHUNK_SHIFT) <= jnp.right_shift(q_pos, CHUNK_SHIFT)
    return jnp.where(visible, slope * (r - jnp.abs(q_pos - k_pos)).astype(F32), -jnp.inf)


def _split_distance(slope, d):
    return (-(slope * CHUNK) * jnp.right_shift(d, CHUNK_SHIFT).astype(F32),
            -slope * jnp.bitwise_and(d, CHUNK - 1).astype(F32))


def _diff_prompt_kernel(i_tab, j_tab, slopes_ref, q_ref, k_ref, v_ref, lq1, lk1, lq2, lk2, g_ref, o_ref,
                        m_scr, l_scr, acc_scr, lhs_scr, s_even, s_odd, *, tq, lam_init):
    h = pl.program_id(1)
    s_id = pl.program_id(2)
    n_pairs = pl.num_programs(2) - 1
    score_on = s_id < n_pairs
    consume_on = s_id >= 1
    i = i_tab[s_id]
    j = j_tab[s_id]
    ci = i_tab[jnp.maximum(s_id - 1, 0)]
    cj = j_tab[jnp.maximum(s_id - 1, 0)]
    slope = slopes_ref[h]
    half = DIFF_HEAD_DIM
    lane = lax.broadcasted_iota(jnp.int32, (tq, LANES), 1)

    @pl.when(jnp.logical_and(consume_on, cj == 0))
    def _():
        _diff_init(m_scr, l_scr, acc_scr)

    @pl.when(jnp.logical_and(score_on, j == 0))
    def _():
        qs = q_ref[...] * (DIFF_HEAD_DIM ** -0.5)
        lhs_scr[0] = jnp.where(lane < half, qs, 1.0).astype(BF16)
        lhs_scr[1] = jnp.where(lane >= half, qs, 1.0).astype(BF16)

    sub = min(DIFF_SUB, tq)
    units = [(n, qb) for qb in range(tq // sub) for n in range(2)]

    def score_tile(near, dst):
        kb = k_ref[...]
        if near:
            aug0 = aug1 = jnp.zeros_like(kb)
        else:
            d = (i - j) * tq - lax.broadcasted_iota(jnp.int32, (tq, LANES), 0)
            t_hi, t_lo = _split_distance(slope, d)
            aug0 = jnp.where(lane == half, t_hi, jnp.where(lane == half + 1, t_lo, 0.0))
            aug1 = jnp.where(lane == 0, t_hi, jnp.where(lane == 1, t_lo, 0.0))
        keys = [jnp.where(lane < half, kb, aug0).astype(BF16), jnp.where(lane >= half, kb, aug1).astype(BF16)]
        bias = None
        for n, qb in units:
            cols = slice(qb * sub, (qb + 1) * sub)
            if near:
                live = (qb + 1) * sub
                if n == 0:
                    k_pos = lax.broadcasted_iota(jnp.int32, (live, sub), 0)
                    q_pos = qb * sub + lax.broadcasted_iota(jnp.int32, (live, sub), 1)
                    bias = _near_bias(slope, q_pos, q_pos, k_pos)
                dst[n, :live, cols] = _dot_nt(keys[n][:live], lhs_scr[n, cols, :]) + bias
                if live < tq:
                    dst[n, live:, cols] = jnp.full((tq - live, sub), -jnp.inf, F32)
            else:
                dst[n, :, cols] = _dot_nt(keys[n], lhs_scr[n, cols, :])

    def consume_tile(src):
        vt = v_ref[...].T.astype(BF16)
        idx = range(len(units))
        cols = [slice(qb * sub, (qb + 1) * sub) for _, qb in units]
        s = [src[n, :, cols[u]] for u, (n, _) in enumerate(units)]
        m_old = [m_scr[n, :, cols[u]] for u, (n, _) in enumerate(units)]
        m_new = [jnp.maximum(m_old[u], jnp.max(s[u], axis=0, keepdims=True)) for u in idx]
        p = [jnp.exp(s[u] - m_new[u]) for u in idx]
        alpha = [jnp.exp(m_old[u] - m_new[u]) for u in idx]
        pv = [_dot(vt, p[u].astype(BF16)) for u in idx]
        for u, (n, _) in enumerate(units):
            l_scr[n, :, cols[u]] = alpha[u] * l_scr[n, :, cols[u]] + jnp.sum(p[u], axis=0, keepdims=True)
            acc_scr[n, :, cols[u]] = alpha[u] * acc_scr[n, :, cols[u]] + pv[u]
            m_scr[n, :, cols[u]] = m_new[u]

    def stage(cond, near, score, consume):
        for parity, (dst, src) in enumerate(((s_even, s_odd), (s_odd, s_even))):
            @pl.when(jnp.logical_and(cond, s_id % 2 == parity))
            def _(dst=dst, src=src):
                if score:
                    score_tile(near, dst)
                if consume:
                    consume_tile(src)

    both = jnp.logical_and(score_on, consume_on)
    stage(jnp.logical_and(both, j < i), False, True, True)
    stage(jnp.logical_and(both, j == i), True, True, True)
    stage(jnp.logical_not(consume_on), True, True, False)
    stage(jnp.logical_not(score_on), False, False, True)

    @pl.when(jnp.logical_and(consume_on, cj == ci))
    def _():
        o_ref[...] = _diff_finalize((acc_scr[0] / l_scr[0]).T, (acc_scr[1] / l_scr[1]).T,
                                    (lq1, lk1, lq2, lk2), g_ref, lam_init).astype(o_ref.dtype)


def _alibi_slopes():
    return jnp.asarray([2.0 ** (-8.0 * (i + 1) / DIFF_HEADS) for i in range(DIFF_HEADS)], F32)


def _diff_prompt(main, k_new, v_new, lams, subln_g, *, bsz, t, lam_init):
    tq = min(DIFF_TQ, t)
    nq = t // tq
    assert tq % CHUNK == 0 and t <= CHUNK * 255
    kern = functools.partial(_diff_prompt_kernel, tq=tq, lam_init=lam_init)
    qc, kc, vc = COL_DQ_MAIN // LANES, 0, 0
    pairs = [(i, j) for i in range(nq) for j in range(i + 1)]
    pairs.append(pairs[-1])
    i_tab = jnp.asarray([p[0] for p in pairs], jnp.int32)
    j_tab = jnp.asarray([p[1] for p in pairs], jnp.int32)
    zero = lambda b, h, s, it, jt: (0, 0)

    def consumed(tab, s):
        return tab[jnp.maximum(s - 1, 0)]

    grid_spec = pltpu.PrefetchScalarGridSpec(
        num_scalar_prefetch=2,
        grid=(bsz, DIFF_HEADS, len(pairs)),
        in_specs=[
            pl.BlockSpec(memory_space=pltpu.SMEM),
            pl.BlockSpec((tq, LANES), lambda b, h, s, it, jt: (b * nq + it[s], qc + h)),
            pl.BlockSpec((tq, LANES), lambda b, h, s, it, jt: (b * nq + jt[s], kc + h)),
            pl.BlockSpec((tq, LANES), lambda b, h, s, it, jt: (b * nq + consumed(jt, s), vc + h)),
        ] + [pl.BlockSpec((1, DIFF_HEAD_DIM), zero)] * 4 + [pl.BlockSpec((1, HEAD_DIM), zero)],
        out_specs=pl.BlockSpec((tq, LANES), lambda b, h, s, it, jt: (b * nq + consumed(it, s), h)),
        scratch_shapes=[pltpu.VMEM((2, 1, tq), F32), pltpu.VMEM((2, 1, tq), F32),
                        pltpu.VMEM((2, HEAD_DIM, tq), F32), pltpu.VMEM((2, tq, LANES), BF16),
                        pltpu.VMEM((2, tq, tq), F32), pltpu.VMEM((2, tq, tq), F32)],
    )
    return pl.pallas_call(
        kern,
        out_shape=jax.ShapeDtypeStruct((bsz * t, DIFF_HEADS * HEAD_DIM), BF16),
        grid_spec=grid_spec,
        compiler_params=_cparams(("parallel", "parallel", "arbitrary")),
        name="diff_attn_prompt",
    )(i_tab, j_tab, _alibi_slopes(), main, k_new, v_new, *lams, subln_g.reshape(1, -1))


def _diff_sample_kernel(slopes_ref, q_ref, kn_ref, vn_ref, kp_ref, vp_ref, lq1, lk1, lq2, lk2, g_ref, o_ref,
                        m_scr, l_scr, acc_scr, lhs_scr, *, t, tk, past, lam_init):
    j = pl.program_id(1)
    heads = range(DIFF_HEADS)

    def cols(h):
        return slice(h * HEAD_DIM, (h + 1) * HEAD_DIM)

    @pl.when(j == 0)
    def _():
        _diff_init(m_scr, l_scr, acc_scr)
        lane = lax.broadcasted_iota(jnp.int32, (t, LANES), 1)
        r = lax.broadcasted_iota(jnp.int32, (t, t), 0)
        c = lax.broadcasted_iota(jnp.int32, (t, t), 1)
        scores = []
        for h in heads:
            qs = q_ref[:, cols(h)] * (DIFF_HEAD_DIM ** -0.5)
            lhs_scr[h, 0:t, :] = jnp.where(lane < DIFF_HEAD_DIM, qs, 0.0).astype(BF16)
            lhs_scr[h, t:2 * t, :] = jnp.where(lane >= DIFF_HEAD_DIM, qs, 0.0).astype(BF16)
            bias = _near_bias(slopes_ref[h], r, past + r, past + c)
            scores.append(_dot_nt(lhs_scr[h], kn_ref[:, cols(h)].astype(BF16)) + jnp.concatenate([bias, bias], axis=0))
        _softmax_update(m_scr, l_scr, acc_scr, scores, [vn_ref[:, cols(h)].astype(BF16) for h in heads])

    d = (past - j * tk - lax.broadcasted_iota(jnp.int32, (1, tk), 1)).astype(F32)
    kt = jnp.swapaxes(kp_ref[...], 0, 1).astype(BF16)
    vt = jnp.swapaxes(vp_ref[...], 0, 1).astype(BF16)
    scores = [_dot_nt(lhs_scr[h], kt[h]) - slopes_ref[h] * d for h in heads]
    _softmax_update(m_scr, l_scr, acc_scr, scores, [vt[h] for h in heads])

    @pl.when(j == pl.num_programs(1) - 1)
    def _():
        for h in heads:
            o = acc_scr[h] / l_scr[h]
            o_ref[:, cols(h)] = _diff_finalize(o[:t], o[t:], (lq1, lk1, lq2, lk2), g_ref, lam_init).astype(o_ref.dtype)


def _diff_sample(main, k_new, v_new, k_past, v_past, lams, subln_g, *, bsz, t, lam_init):
    past = k_past.shape[1]
    tk = min(1024, past)
    nk = past // tk
    width = DIFF_HEADS * HEAD_DIM
    kern = functools.partial(_diff_sample_kernel, t=t, tk=tk, past=past, lam_init=lam_init)
    zero = lambda b, j: (0, 0)
    cache = pl.BlockSpec((None, tk, DIFF_HEADS, HEAD_DIM), lambda b, j: (b, j, 0, 0))
    return pl.pallas_call(
        kern,
        out_shape=jax.ShapeDtypeStruct((bsz * t, width), BF16),
        grid=(bsz, nk),
        in_specs=[
            pl.BlockSpec(memory_space=pltpu.SMEM),
            pl.BlockSpec((t, width), lambda b, j: (b, COL_DQ_MAIN // width)),
            pl.BlockSpec((t, width), lambda b, j: (b, 0)),
            pl.BlockSpec((t, width), lambda b, j: (b, 0)),
            cache, cache,
        ] + [pl.BlockSpec((1, DIFF_HEAD_DIM), zero)] * 4 + [pl.BlockSpec((1, HEAD_DIM), zero)],
        out_specs=pl.BlockSpec((t, width), lambda b, j: (b, 0)),
        scratch_shapes=[pltpu.VMEM((DIFF_HEADS, 2 * t, 1), F32), pltpu.VMEM((DIFF_HEADS, 2 * t, 1), F32),
                        pltpu.VMEM((DIFF_HEADS, 2 * t, HEAD_DIM), F32), pltpu.VMEM((DIFF_HEADS, 2 * t, LANES), BF16)],
        compiler_params=_cparams(("parallel", "arbitrary")),
        name="diff_attn_sample",
    )(_alibi_slopes(), main, k_new, v_new, k_past, v_past, *lams, subln_g.reshape(1, -1))


def _merge_kernel(oa_ref, ob_ref, wa_ref, wb_ref, ga_ref, gb_ref, o_ref):
    ya = _dot(oa_ref[...], wa_ref[...])
    yb = _dot(ob_ref[...], wb_ref[...])
    o_ref[...] = (jax.nn.sigmoid(ga_ref[...]) * ya + jax.nn.sigmoid(gb_ref[...]) * yb).astype(o_ref.dtype)


def _merge(o_a, o_b, wa, wb, tail, *, tm):
    m = o_a.shape[0]
    d = wa.shape[1]
    tn = 512
    ga0, gb0 = 0, COL_GB_TAIL // tn
    return pl.pallas_call(
        _merge_kernel,
        out_shape=jax.ShapeDtypeStruct((m, d), BF16),
        grid=(m // tm, d // tn),
        in_specs=[
            pl.BlockSpec((tm, o_a.shape[1]), lambda i, j: (i, 0)),
            pl.BlockSpec((tm, o_b.shape[1]), lambda i, j: (i, 0)),
            pl.BlockSpec((wa.shape[0], tn), lambda i, j: (0, j)),
            pl.BlockSpec((wb.shape[0], tn), lambda i, j: (0, j)),
            pl.BlockSpec((tm, tn), lambda i, j: (i, ga0 + j)),
            pl.BlockSpec((tm, tn), lambda i, j: (i, gb0 + j)),
        ],
        out_specs=pl.BlockSpec((tm, tn), lambda i, j: (i, j)),
        compiler_params=_cparams(("parallel", "arbitrary")),
        name="merge",
    )(o_a, o_b, wa, wb, tail, tail)


def _cross_kernel(q_ref, k_ref, v_ref, o_ref):
    dh = q_ref.shape[-1] // MEM_HEADS
    for h in range(MEM_HEADS):
        sl = slice(h * dh, (h + 1) * dh)
        s = _dot_nt(q_ref[:, sl], k_ref[0, :, sl].astype(BF16)) * (dh ** -0.5)
        s = s - jnp.max(s, axis=-1, keepdims=True)
        p = jnp.exp(s)
        p = p / jnp.sum(p, axis=-1, keepdims=True)
        o_ref[:, sl] = _dot(p.astype(BF16), v_ref[0, :, sl].astype(BF16)).astype(o_ref.dtype)


def _cross_attend(q, mem_k, mem_v, *, bsz, t, tm):
    d = q.shape[1]
    nt = t // tm
    mt = mem_k.shape[1]
    return pl.pallas_call(
        _cross_kernel,
        out_shape=jax.ShapeDtypeStruct(q.shape, BF16),
        grid=(bsz, nt),
        in_specs=[
            pl.BlockSpec((tm, d), lambda b, i: (b * nt + i, 0)),
            pl.BlockSpec((1, mt, d), lambda b, i: (b, 0, 0)),
            pl.BlockSpec((1, mt, d), lambda b, i: (b, 0, 0)),
        ],
        out_specs=pl.BlockSpec((tm, d), lambda b, i: (b * nt + i, 0)),
        compiler_params=_cparams(("parallel", "arbitrary")),
        name="cross_attn",
    )(q, mem_k, mem_v)


def _topk_rows(work, cidx, k):
    n, tn = work.shape
    rid = lax.broadcasted_iota(jnp.int32, (n, tn), 0).astype(F32)
    kid = lax.broadcasted_iota(jnp.int32, (k, tn), 0)
    vals = jnp.zeros((k, tn), F32)
    idxs = jnp.zeros((k, tn), F32)
    for t in range(k):
        m = jnp.max(work, axis=0, keepdims=True)
        pos = jnp.min(jnp.where(work == m, rid, float(n)), axis=0, keepdims=True)
        hit = rid == pos
        if cidx is None:
            picked = pos
        else:
            picked = jnp.sum(jnp.where(hit, cidx, 0.0), axis=0, keepdims=True)
        vals = jnp.where(kid == t, m, vals)
        idxs = jnp.where(kid == t, picked, idxs)
        work = jnp.where(hit, -jnp.inf, work)
    return vals, idxs


def _product_candidates(v1, i1, v2, i2, k):
    assert k == 2 * SUBLANES
    sub = lax.broadcasted_iota(jnp.int32, (SUBLANES, v1.shape[1]), 0)
    vals, idxs = [], []
    for a in range(k // 2):
        nb = k // (a + 1)
        width = k if nb > SUBLANES else SUBLANES
        cv = v1[a:a + 1, :] + v2[:width, :]
        ci = i1[a:a + 1, :] * PEER_N_KEYS + i2[:width, :]
        if nb < SUBLANES:
            cv = jnp.where(sub < nb, cv, -jnp.inf)
        vals.append(cv)
        idxs.append(ci)
    vals.append(v1[k // 2:, :] + v2[0:1, :])
    idxs.append(i1[k // 2:, :] * PEER_N_KEYS + i2[0:1, :])
    return jnp.concatenate(vals, axis=0), jnp.concatenate(idxs, axis=0)


def _peer_route_kernel(q_ref, keys_hi_ref, keys_lo_ref, e_ref, g_ref):
    k = PEER_TOPK
    for h in range(PEER_HEADS):
        sub = []
        for p in range(2):
            c0 = (h * 2 + p) * PEER_N_KEYS
            q_parts = _split_bf16(q_ref[:, c0:c0 + PEER_N_KEYS])
            s = _dot3_parts((keys_hi_ref[h, p], keys_lo_ref[h, p]), q_parts, _dot_nt)
            sub.append(_topk_rows(s, None, k))
        (v1, i1), (v2, i2) = sub
        cand, cidx = _product_candidates(v1, i1, v2, i2, k)
        best, eidx = _topk_rows(cand, cidx, k)
        ex = jnp.exp(best - jnp.max(best, axis=0, keepdims=True))
        gate = ex / jnp.sum(ex, axis=0, keepdims=True)
        e_ref[h * k:(h + 1) * k, :] = eidx.astype(jnp.int32)
        g_ref[h * k:(h + 1) * k, :] = gate


PEER_GATE_TN = 256


def _peer_gate_kernel(et_ref, gt_ref, o_ref, e_scr, g_scr):
    tg = et_ref.shape[1]
    e_scr[...] = et_ref[...].T
    g_scr[...] = gt_ref[...].T
    kid = lax.broadcasted_iota(jnp.int32, (PEER_N_KEYS, PEER_N_KEYS), 0)
    group = 2 * SUBLANES

    def body(n, carry):
        rows = pl.ds(pl.multiple_of(n * group, group), group)
        e16 = e_scr[rows, :]
        g16 = g_scr[rows, :]
        a16 = jnp.right_shift(e16, 7)
        b16 = jnp.bitwise_and(e16, PEER_N_KEYS - 1)
        toks = range(group)
        at = [jnp.where(a16[s:s + 1, :] == kid, g16[s:s + 1, :], 0.0).astype(BF16) for s in toks]
        bt = [jnp.where(b16[s:s + 1, :] == kid, 1.0, 0.0).astype(BF16) for s in toks]
        grids = [_dot_nt(at[s], bt[s]) for s in toks]
        o_ref[:, rows, :] = jnp.swapaxes(jnp.stack(grids, axis=0), 0, 1).astype(BF16)
        return carry

    lax.fori_loop(0, tg // group, body, 0, unroll=4)


def _peer_route(qp, sub_keys, *, tn):
    n = qp.shape[0]
    rows = PEER_HEADS * PEER_TOPK
    keys_hi, keys_lo = _split_bf16(sub_keys.astype(F32))
    keys_spec = pl.BlockSpec(sub_keys.shape, lambda i: (0, 0, 0, 0))
    e_t, g_t = pl.pallas_call(
        _peer_route_kernel,
        out_shape=(jax.ShapeDtypeStruct((rows, n), jnp.int32), jax.ShapeDtypeStruct((rows, n), F32)),
        grid=(n // tn,),
        in_specs=[pl.BlockSpec((tn, qp.shape[1]), lambda i: (i, 0)), keys_spec, keys_spec],
        out_specs=(pl.BlockSpec((rows, tn), lambda i: (0, i)), pl.BlockSpec((rows, tn), lambda i: (0, i))),
        compiler_params=_cparams(("parallel",)),
        name="peer_route",
    )(qp, keys_hi, keys_lo)
    tg = min(PEER_GATE_TN, n)
    return pl.pallas_call(
        _peer_gate_kernel,
        out_shape=jax.ShapeDtypeStruct((PEER_N_KEYS, n, PEER_N_KEYS), BF16),
        grid=(n // tg,),
        in_specs=[pl.BlockSpec((rows, tg), lambda i: (0, i)), pl.BlockSpec((rows, tg), lambda i: (0, i))],
        out_specs=pl.BlockSpec((PEER_N_KEYS, tg, PEER_N_KEYS), lambda i: (0, i, 0)),
        scratch_shapes=[pltpu.VMEM((tg, rows), jnp.int32), pltpu.VMEM((tg, rows), F32)],
        compiler_params=_cparams(("parallel",)),
        name="peer_gates",
    )(e_t, g_t)


def _gelu_tanh(x):
    return 0.5 * x * (1.0 + jnp.tanh(math.sqrt(2.0 / math.pi) * (x + 0.044715 * (x * x * x))))


PEER_BLOCKS_PER_STEP = 4
PEER_TN = 1024
PEER_VMEM_LIMIT = 60 * 1024 * 1024


def _peer_expert_kernel(x_ref, gn_ref, *rest):
    nblk = PEER_BLOCKS_PER_STEP
    gate_refs, u_refs, v_refs = rest[:nblk], rest[nblk:2 * nblk], rest[2 * nblk:3 * nblk]
    gf_ref, o_ref, h_scr, acc_scr, coef_scr = rest[3 * nblk:]
    s = pl.program_id(1)
    last = pl.num_programs(1) - 1

    @pl.when(s == 0)
    def _():
        h_scr[...] = _rms(x_ref[...], gn_ref[...]).astype(BF16)
        acc_scr[...] = jnp.zeros(acc_scr.shape, F32)
        coef_scr[...] = jnp.zeros(coef_scr.shape, BF16)

    v_cat = jnp.concatenate([v[...].astype(BF16) for v in v_refs], axis=0)
    acc_scr[...] += _dot(coef_scr[(s + 1) % 2], v_cat)

    gate = jnp.concatenate([g[...].astype(F32) for g in gate_refs], axis=1)
    u_cat = jnp.concatenate([u[...].astype(BF16) for u in u_refs], axis=0)
    hid = _gelu_tanh(_dot_nt(h_scr[...], u_cat))
    coef_scr[s % 2] = (gate * hid).astype(BF16)

    @pl.when(s == last)
    def _():
        o_ref[...] = _rms(x_ref[...] + acc_scr[...], gf_ref[...])


def _peer_experts(x, norm_g, gates, u_tab, v_tab, final_g, *, tn):
    n, d = x.shape
    eb = PEER_N_KEYS
    assert u_tab.shape[0] == eb * eb
    nblk = PEER_BLOCKS_PER_STEP
    ns = eb // nblk

    def cur(s):
        return jnp.minimum(s, ns - 1)

    def prev(s):
        return jnp.maximum(s - 1, 0)

    def table(which, q):
        return pl.BlockSpec((eb, d), lambda i, s: (which(s) + q * ns, 0))

    return pl.pallas_call(
        _peer_expert_kernel,
        out_shape=jax.ShapeDtypeStruct((n, d), F32),
        grid=(n // tn, ns + 1),
        in_specs=[
            pl.BlockSpec((tn, d), lambda i, s: (i, 0), pipeline_mode=pl.Buffered(1)),
            pl.BlockSpec((1, d), lambda i, s: (0, 0)),
        ] + [pl.BlockSpec((None, tn, eb), lambda i, s, q=q: (cur(s) + q * ns, i, 0)) for q in range(nblk)] + [
            table(cur, q) for q in range(nblk)] + [table(prev, q) for q in range(nblk)] + [
            pl.BlockSpec((1, d), lambda i, s: (0, 0)),
        ],
        out_specs=pl.BlockSpec((tn, d), lambda i, s: (i, 0), pipeline_mode=pl.Buffered(1)),
        scratch_shapes=[pltpu.VMEM((tn, d), BF16), pltpu.VMEM((tn, d), F32),
                        pltpu.VMEM((2, tn, nblk * eb), BF16)],
        compiler_params=_cparams(("parallel", "arbitrary"), vmem=PEER_VMEM_LIMIT),
        name="peer_experts",
    )(x, norm_g.reshape(1, d), *([gates] * nblk), *([u_tab] * nblk), *([v_tab] * nblk), final_g.reshape(1, d))


def _layer(x, wts, layer, mem_k, mem_v, conv_prev, gdn_state, k_past, v_past, *, tm):
    bsz, t, d = x.shape
    n = bsz * t
    x2d = x.reshape(n, d)
    main, k_new, v_new, tail = _in_proj(x2d, wts["norm_mix_g"], wts["w_in"], tm=min(IN_TM, n))
    conv_dim = 3 * GDN_WIDTH
    conv_prev8 = jnp.concatenate(
        [jnp.zeros((bsz, SUBLANES - (GDN_CONV - 1), conv_dim), F32), conv_prev.astype(F32)], axis=1)
    o_a, s_new = _gdn(main, tail, conv_prev8, gdn_state.astype(F32), wts["gdn_conv_w"], wts["gdn_a_log"],
                      wts["gdn_dt_bias"], wts["gdn_norm_g"], bsz=bsz, t=t)
    conv_new = main.reshape(bsz, t, -1)[:, t - (GDN_CONV - 1):, :conv_dim]
    lam_init = 0.8 - 0.6 * math.exp(-0.3 * layer)
    lams = tuple(wts[k].reshape(1, -1) for k in ("diff_lambda_q1", "diff_lambda_k1", "diff_lambda_q2", "diff_lambda_k2"))
    if k_past is None:
        o_b = _diff_prompt(main, k_new, v_new, lams, wts["diff_subln_g"], bsz=bsz, t=t, lam_init=lam_init)
    else:
        o_b = _diff_sample(main, k_new, v_new, k_past, v_past, lams, wts["diff_subln_g"], bsz=bsz, t=t,
                           lam_init=lam_init)
    k_rows = k_new.reshape(bsz, t, DIFF_HEADS, HEAD_DIM)
    v_rows = v_new.reshape(bsz, t, DIFF_HEADS, HEAD_DIM)
    merged = _merge(o_a, o_b, wts["w_branch_a"], wts["w_branch_b"], tail, tm=tm)
    x1 = _matmul(merged, wts["w_out"], residual=x2d, tm=tm, tn=MM_TN, name="out_proj")
    qm = _matmul(x1, wts["w_mq"], norm_g=wts["norm_cross_g"], tm=tm, tn=MM_TN, out_dtype=BF16, name="mem_q")
    mt = mem_k.shape[1]
    oc = _cross_attend(qm, mem_k.reshape(bsz, mt, d), mem_v.reshape(bsz, mt, d), bsz=bsz, t=t, tm=min(tm, t))
    x2 = _matmul(oc, wts["w_mo"], residual=x1, tm=tm, tn=MM_TN, name="mem_o")
    qp = _matmul(x2, wts["peer_w_q"], norm_g=wts["norm_ffn_g"], tm=tm, tn=MM_TN, name="peer_q")
    tn = min(PEER_TN, n)
    gates = _peer_route(qp, wts["peer_sub_keys"], tn=LANES)
    y = _peer_experts(x2, wts["norm_ffn_g"], gates, wts["peer_u"], wts["peer_v"], wts["final_norm_g"], tn=tn)
    return y.reshape(bsz, t, d), k_rows, v_rows, s_new, conv_new


def kernel(x_prompt, x_sample, cache_diff_k, cache_diff_v, state_gdn, state_conv, cache_mem_k, cache_mem_v,
           mem_prompt, norm_mix_g, w_in, gdn_conv_w, gdn_a_log, gdn_dt_bias, gdn_norm_g,
           diff_lambda_q1, diff_lambda_k1, diff_lambda_q2, diff_lambda_k2, diff_subln_g,
           w_branch_a, w_branch_b, w_out, norm_cross_g, norm_mem_g, w_mq, w_mk, w_mv, w_mo,
           norm_ffn_g, peer_w_q, peer_sub_keys, peer_u, peer_v, final_norm_g):
    depth = w_in.shape[0]
    assert depth == 1, "final norm is fused into the last layer's PEER kernel"
    l = 0
    wts = {
        "norm_mix_g": norm_mix_g[l], "w_in": _in_proj_weights(w_in[l]), "gdn_conv_w": gdn_conv_w[l],
        "gdn_a_log": gdn_a_log[l], "gdn_dt_bias": gdn_dt_bias[l], "gdn_norm_g": gdn_norm_g[l],
        "diff_lambda_q1": diff_lambda_q1[l], "diff_lambda_k1": diff_lambda_k1[l],
        "diff_lambda_q2": diff_lambda_q2[l], "diff_lambda_k2": diff_lambda_k2[l],
        "diff_subln_g": diff_subln_g[l], "w_branch_a": w_branch_a[l].astype(BF16),
        "w_branch_b": w_branch_b[l].astype(BF16), "w_out": w_out[l].astype(BF16),
        "norm_cross_g": norm_cross_g[l], "w_mq": w_mq[l].astype(BF16), "w_mo": w_mo[l].astype(BF16),
        "norm_ffn_g": norm_ffn_g[l], "peer_w_q": peer_w_q[l].astype(BF16), "peer_sub_keys": peer_sub_keys[l],
        "peer_u": peer_u[l], "peer_v": peer_v[l], "final_norm_g": final_norm_g,
    }
    bp, tp, d = x_prompt.shape
    bs, ts, _ = x_sample.shape
    mem2d = mem_prompt.reshape(-1, d)
    tmm = min(512, mem2d.shape[0])
    mem_k = _matmul(mem2d, w_mk[l], norm_g=norm_mem_g[l], tm=tmm, tn=512, name="mem_k")
    mem_v = _matmul(mem2d, w_mv[l], norm_g=norm_mem_g[l], tm=tmm, tn=512, name="mem_v")
    mshape = mem_prompt.shape[:2] + (MEM_HEADS, d // MEM_HEADS)
    mem_k = mem_k.reshape(mshape)
    mem_v = mem_v.reshape(mshape)
    conv0 = jnp.zeros((bp, GDN_CONV - 1, 3 * GDN_WIDTH), F32)
    s0 = jnp.zeros((bp, GDN_HEADS, HEAD_DIM, HEAD_DIM), F32)
    yp, pk, pv, ps, pc = _layer(x_prompt, wts, l, mem_k, mem_v, conv0, s0, None, None, tm=min(1024, bp * tp))
    ys, sk, sv, ss, sc = _layer(x_sample, wts, l, cache_mem_k[l], cache_mem_v[l], state_conv[l], state_gdn[l],
                                cache_diff_k[l], cache_diff_v[l], tm=min(256, bs * ts))
    return (yp, ys, pk[None], pv[None], ps[None], pc[None], mem_k[None], mem_v[None],
            sk[None], sv[None], ss[None], sc[None])
```

```python
import functools
import math

import jax
import jax.numpy as jnp
from jax import lax
from jax.experimental import pallas as pl
from jax.experimental.pallas import tpu as pltpu

F32 = jnp.float32
BF16 = jnp.bfloat16
HIGHEST = lax.Precision.HIGHEST

RMS_EPS = 1e-6
CHUNK = 64
CHUNK_SHIFT = 6
DIFF_TQ = 1024
GDN_CHUNKS_PER_STEP = 4
DIFF_SUB = 256
GDN_HEADS = 8
HEAD_DIM = 128
GDN_WIDTH = GDN_HEADS * HEAD_DIM
GDN_CONV = 4
DIFF_HEADS = 8
DIFF_HEAD_DIM = 64
MEM_HEADS = 4
PEER_HEADS = 8
PEER_N_KEYS = 128
PEER_TOPK = 16
LANES = 128
SUBLANES = 8
V7X_VMEM_LIMIT = 52 * 1024 * 1024
MM_TN = 1024


def _cparams(sem, vmem=V7X_VMEM_LIMIT):
    return pltpu.CompilerParams(dimension_semantics=sem, vmem_limit_bytes=vmem)


def _dot(a, b, precision=None):
    return jnp.dot(a, b, preferred_element_type=F32, precision=precision)


def _dot_nt(a, b, precision=None):
    return lax.dot_general(a, b, (((1,), (1,)), ((), ())), preferred_element_type=F32, precision=precision)


def _split_bf16(x):
    hi = x.astype(BF16)
    return hi, (x - hi.astype(F32)).astype(BF16)


def _cat_parts(p, q):
    return jnp.concatenate([p[0], q[0]], axis=0), jnp.concatenate([p[1], q[1]], axis=0)


def _dot3_parts(a_parts, b_parts, dot):
    a_hi, a_lo = a_parts
    b_hi, b_lo = b_parts
    m = a_hi.shape[0]
    both = dot(jnp.concatenate([a_hi, a_lo], axis=0), b_hi)
    return both[:m] + both[m:] + dot(a_hi, b_lo)


def _dot3(a, b):
    return _dot3_parts(_split_bf16(a), _split_bf16(b), _dot)


def _rms(xf, g):
    return xf * lax.rsqrt(jnp.mean(xf * xf, axis=-1, keepdims=True) + RMS_EPS) * g


def _mm_kernel(*refs, has_norm, has_res):
    it = iter(refs)
    x_ref = next(it)
    g_ref = next(it) if has_norm else None
    w_ref = next(it)
    r_ref = next(it) if has_res else None
    o_ref = next(it)
    h_scr = next(it)

    @pl.when(pl.program_id(1) == 0)
    def _():
        xf = x_ref[...].astype(F32)
        if has_norm:
            xf = _rms(xf, g_ref[...])
        h_scr[...] = xf.astype(BF16)

    acc = _dot(h_scr[...], w_ref[...].astype(BF16))
    if has_res:
        acc = acc + r_ref[...]
    o_ref[...] = acc.astype(o_ref.dtype)


def _matmul(x, w, *, norm_g=None, residual=None, tm, tn, out_dtype=F32, name="matmul"):
    m, k = x.shape
    n = w.shape[1]
    assert m % tm == 0 and n % tn == 0, (m, n, tm, tn)
    in_specs = [pl.BlockSpec((tm, k), lambda i, j: (i, 0))]
    args = [x]
    if norm_g is not None:
        in_specs.append(pl.BlockSpec((1, k), lambda i, j: (0, 0)))
        args.append(norm_g.reshape(1, k).astype(F32))
    in_specs.append(pl.BlockSpec((k, tn), lambda i, j: (0, j)))
    args.append(w)
    if residual is not None:
        in_specs.append(pl.BlockSpec((tm, tn), lambda i, j: (i, j)))
        args.append(residual)
    return pl.pallas_call(
        functools.partial(_mm_kernel, has_norm=norm_g is not None, has_res=residual is not None),
        out_shape=jax.ShapeDtypeStruct((m, n), out_dtype),
        grid=(m // tm, n // tn),
        in_specs=in_specs,
        out_specs=pl.BlockSpec((tm, tn), lambda i, j: (i, j)),
        scratch_shapes=[pltpu.VMEM((tm, k), BF16)],
        compiler_params=_cparams(("parallel", "arbitrary")),
        name=name,
    )(*args)


IN_TN = 512
IN_TM = 1024
IN_W1_TILES = 8
IN_W2_TILES = 14
IN_MAIN_TILES = 10
IN_K_TILE0 = 10
IN_V_TILE0 = 12
IN_TAIL_TILE0 = 14
IN_TILES = IN_W1_TILES + IN_W2_TILES + 1
MAIN_COLS = IN_MAIN_TILES * IN_TN
TAIL_COLS = (IN_TILES - IN_TAIL_TILE0) * IN_TN
COL_DQ_MAIN = 4096
COL_GB_TAIL = 2048
COL_BA_TAIL = 4096


def _in_proj_kernel(x_ref, g_ref, w1_ref, w2_ref, w3_ref, main_ref, k_ref, v_ref, tail_ref, h_scr):
    j = pl.program_id(1)

    @pl.when(j == 0)
    def _():
        h_scr[...] = _rms(x_ref[...], g_ref[...]).astype(BF16)

    routes = ((0, IN_W1_TILES, w1_ref, main_ref),
              (IN_W1_TILES, IN_K_TILE0, w2_ref, main_ref),
              (IN_K_TILE0, IN_V_TILE0, w2_ref, k_ref),
              (IN_V_TILE0, IN_TAIL_TILE0, w2_ref, v_ref),
              (IN_TAIL_TILE0, IN_TILES - 1, w2_ref, tail_ref),
              (IN_TILES - 1, IN_TILES, w3_ref, tail_ref))
    for lo, hi, w_ref, dst in routes:
        @pl.when(jnp.logical_and(j >= lo, j < hi))
        def _(w_ref=w_ref, dst=dst):
            dst[...] = _dot(h_scr[...], w_ref[...])


def _in_proj_weights(w_in):
    c_z = 4 * GDN_WIDTH
    c_ba = c_z + 2 * GDN_HEADS
    assert c_z == IN_W1_TILES * IN_TN and w_in.shape[1] - c_ba == IN_W2_TILES * IN_TN
    w_bf16 = w_in.astype(BF16)
    return w_bf16, w_bf16[:, c_ba:]


def _in_proj(x, norm_g, w_pieces, *, tm):
    m, d = x.shape
    w_all, w2 = w_pieces
    w1 = w3 = w_all
    tn = IN_TN
    out = lambda cols: jax.ShapeDtypeStruct((m, cols), F32)
    return pl.pallas_call(
        _in_proj_kernel,
        out_shape=(out(MAIN_COLS), out(2 * IN_TN), out(2 * IN_TN), out(TAIL_COLS)),
        grid=(m // tm, IN_TILES),
        in_specs=[
            pl.BlockSpec((tm, d), lambda i, j: (i, 0)),
            pl.BlockSpec((1, d), lambda i, j: (0, 0)),
            pl.BlockSpec((d, tn), lambda i, j: (0, jnp.minimum(j, IN_W1_TILES - 1))),
            pl.BlockSpec((d, tn), lambda i, j: (0, jnp.clip(j - IN_W1_TILES, 0, IN_W2_TILES - 1))),
            pl.BlockSpec((d, tn), lambda i, j: (0, IN_W1_TILES)),
        ],
        out_specs=(
            pl.BlockSpec((tm, tn), lambda i, j: (i, jnp.minimum(j, IN_MAIN_TILES - 1))),
            pl.BlockSpec((tm, tn), lambda i, j: (i, jnp.clip(j - IN_K_TILE0, 0, 1))),
            pl.BlockSpec((tm, tn), lambda i, j: (i, jnp.clip(j - IN_V_TILE0, 0, 1))),
            pl.BlockSpec((tm, tn), lambda i, j: (i, jnp.clip(j - IN_TAIL_TILE0, 0, IN_TILES - IN_TAIL_TILE0 - 1))),
        ),
        scratch_shapes=[pltpu.VMEM((tm, d), BF16)],
        compiler_params=_cparams(("parallel", "arbitrary")),
        name="in_proj",
    )(x, norm_g.reshape(1, d).astype(F32), w1, w2, w3)


def _gdn_kernel(qkv_ref, z_ref, ba_ref, prev_ref, s0_ref, convw_ref, alog_ref, dtb_ref, ng_ref,
                o_ref, sfin_ref, s_scr, xbuf, *, L, nch):
    step = pl.program_id(1)
    rows = nch * L

    @pl.when(step == 0)
    def _():
        s_scr[...] = s0_ref[0]
        xbuf[0:SUBLANES, :] = prev_ref[0]

    xbuf[SUBLANES:SUBLANES + rows, :] = qkv_ref[...]
    base = SUBLANES - (GDN_CONV - 1)
    y = xbuf[base:base + rows, :] * convw_ref[0:1, :]
    for i in range(1, GDN_CONV):
        y = y + xbuf[base + i:base + i + rows, :] * convw_ref[i:i + 1, :]
    y = y * jax.nn.sigmoid(y)
    carry = xbuf[rows:rows + SUBLANES, :]
    xbuf[0:SUBLANES, :] = carry

    row = lax.broadcasted_iota(jnp.int32, (L, L), 0)
    col = lax.broadcasted_iota(jnp.int32, (L, L), 1)
    lower = row >= col
    strict = row > col
    eye = (row == col).astype(F32)
    log_l = int(math.log2(L))
    n_double = log_l - 1
    assert 2 ** log_l == L

    ba = ba_ref[...]
    beta_all = jax.nn.sigmoid(ba)
    a_in = ba + dtb_ref[...]
    softplus = jnp.maximum(a_in, 0.0) + jnp.log(1.0 + jnp.exp(-jnp.abs(a_in)))
    g_all = -jnp.exp(alog_ref[...]) * softplus
    rr = lax.broadcasted_iota(jnp.int32, (rows, rows), 0)
    cc = lax.broadcasted_iota(jnp.int32, (rows, rows), 1)
    same_chunk_lower = jnp.logical_and(rr >= cc, jnp.right_shift(rr, log_l) == jnp.right_shift(cc, log_l))
    g_cum = _dot(same_chunk_lower.astype(F32), g_all, HIGHEST)
    g_cum_t = g_cum.T

    heads = range(GDN_HEADS)
    chains = [(c, h) for c in range(nch) for h in heads]
    ids = range(len(chains))

    def tok(c):
        return slice(c * L, (c + 1) * L)

    def head_cols(part, c, h):
        return y[tok(c), part * GDN_WIDTH + h * HEAD_DIM:part * GDN_WIDTH + (h + 1) * HEAD_DIM]

    q = [head_cols(0, c, h) for c, h in chains]
    k = [head_cols(1, c, h) for c, h in chains]
    v = [head_cols(2, c, h) for c, h in chains]
    q = [a * lax.rsqrt(jnp.sum(a * a, axis=-1, keepdims=True) + 1e-6) * (HEAD_DIM ** -0.5) for a in q]
    k = [a * lax.rsqrt(jnp.sum(a * a, axis=-1, keepdims=True) + 1e-6) for a in k]
    beta = [beta_all[tok(c), h:h + 1] for c, h in chains]
    g_col = [g_cum[tok(c), GDN_HEADS + h:GDN_HEADS + h + 1] for c, h in chains]
    g_row = [g_cum_t[GDN_HEADS + h:GDN_HEADS + h + 1, tok(c)] for c, h in chains]
    decay = [jnp.where(lower, jnp.exp(jnp.where(lower, g_col[n] - g_row[n], 0.0)), 0.0) for n in ids]
    g_last = [g_col[n][L - 1:L, :] for n in ids]
    e_col = [jnp.exp(g_col[n]) for n in ids]
    qk_kk = [_dot_nt(jnp.concatenate([q[n], k[n]], axis=0).astype(BF16), k[n].astype(BF16)) for n in ids]
    x = [-jnp.where(strict, beta[n] * qk_kk[n][L:] * decay[n], 0.0) for n in ids]
    t_inv = [eye + x[n] for n in ids]
    x_parts = [_split_bf16(x[n]) for n in ids]
    xp = [_dot3_parts(x_parts[n], x_parts[n], _dot) for n in ids]
    for _ in range(n_double - 1):
        xp_parts = [_split_bf16(xp[n]) for n in ids]
        both = [_dot3_parts(_cat_parts(_split_bf16(t_inv[n]), xp_parts[n]), xp_parts[n], _dot) for n in ids]
        t_inv = [t_inv[n] + both[n][:L] for n in ids]
        xp = [both[n][L:] for n in ids]
    t_inv = [t_inv[n] + _dot3(t_inv[n], xp[n]) for n in ids]
    sol = [_dot3(t_inv[n], jnp.concatenate([v[n] * beta[n], k[n] * (beta[n] * e_col[n])], axis=-1)) for n in ids]
    qk = [jnp.where(lower, qk_kk[n][:L] * decay[n], 0.0) for n in ids]
    k_dec_t = [(k[n] * jnp.exp(g_last[n] - g_col[n])).T for n in ids]
    left_s = [jnp.concatenate([sol[n][:, HEAD_DIM:], q[n] * e_col[n]], axis=0).astype(BF16) for n in ids]
    left_v = [jnp.concatenate([qk[n], k_dec_t[n]], axis=0).astype(BF16) for n in ids]

    s = [s_scr[h] for h in heads]
    for c in range(nch):
        n0 = c * GDN_HEADS
        ws_qs = [_dot(left_s[n0 + h], s[h].astype(BF16)) for h in heads]
        v_new = [sol[n0 + h][:, :HEAD_DIM] - ws_qs[h][:L] for h in heads]
        tail = [_dot(left_v[n0 + h], v_new[h].astype(BF16)) for h in heads]
        s = [s[h] * jnp.exp(g_last[n0 + h]) + tail[h][L:] for h in heads]
        for h in heads:
            zf = z_ref[tok(c), h * HEAD_DIM:(h + 1) * HEAD_DIM]
            o = _rms(ws_qs[h][L:] + tail[h][:L], ng_ref[...]) * (zf * jax.nn.sigmoid(zf))
            o_ref[tok(c), h * HEAD_DIM:(h + 1) * HEAD_DIM] = o.astype(o_ref.dtype)
    for h in heads:
        s_scr[h] = s[h]

    @pl.when(step == pl.num_programs(1) - 1)
    def _():
        sfin_ref[0] = s_scr[...]


def _decay_lanes(p):
    return jnp.zeros((1, LANES), F32).at[0, GDN_HEADS:2 * GDN_HEADS].set(p.astype(F32))


def _gdn(main, tail, conv_prev8, s0, conv_w, a_log, dt_bias, norm_g, *, bsz, t):
    L = min(CHUNK, t)
    nch = GDN_CHUNKS_PER_STEP if (t // L) % GDN_CHUNKS_PER_STEP == 0 else 1
    nc = t // (L * nch)
    rows = L * nch
    conv_dim = 3 * GDN_WIDTH
    kern = functools.partial(_gdn_kernel, L=L, nch=nch)
    o, s_fin = pl.pallas_call(
        kern,
        out_shape=(jax.ShapeDtypeStruct((bsz * t, GDN_WIDTH), BF16),
                   jax.ShapeDtypeStruct((bsz, GDN_HEADS, HEAD_DIM, HEAD_DIM), F32)),
        grid=(bsz, nc),
        in_specs=[
            pl.BlockSpec((rows, conv_dim), lambda b, c: (b * nc + c, 0)),
            pl.BlockSpec((rows, GDN_WIDTH), lambda b, c: (b * nc + c, conv_dim // GDN_WIDTH)),
            pl.BlockSpec((rows, LANES), lambda b, c: (b * nc + c, COL_BA_TAIL // LANES)),
            pl.BlockSpec((1, SUBLANES, conv_dim), lambda b, c: (b, 0, 0)),
            pl.BlockSpec((1, GDN_HEADS, HEAD_DIM, HEAD_DIM), lambda b, c: (b, 0, 0, 0)),
            pl.BlockSpec((GDN_CONV, conv_dim), lambda b, c: (0, 0)),
            pl.BlockSpec((1, LANES), lambda b, c: (0, 0)),
            pl.BlockSpec((1, LANES), lambda b, c: (0, 0)),
            pl.BlockSpec((1, HEAD_DIM), lambda b, c: (0, 0)),
        ],
        out_specs=(pl.BlockSpec((rows, GDN_WIDTH), lambda b, c: (b * nc + c, 0)),
                   pl.BlockSpec((1, GDN_HEADS, HEAD_DIM, HEAD_DIM), lambda b, c: (b, 0, 0, 0))),
        scratch_shapes=[pltpu.VMEM((GDN_HEADS, HEAD_DIM, HEAD_DIM), F32),
                        pltpu.VMEM((SUBLANES + rows + SUBLANES, conv_dim), F32)],
        compiler_params=_cparams(("parallel", "arbitrary")),
        name="gdn",
    )(main, main, tail, conv_prev8, s0, conv_w, _decay_lanes(a_log), _decay_lanes(dt_bias),
      norm_g.reshape(1, -1))
    return o, s_fin


def _lambda_value(lam_refs, lam_init):
    lq1, lk1, lq2, lk2 = (r[...] for r in lam_refs)
    return (jnp.exp(jnp.sum(lq1 * lk1, axis=-1, keepdims=True))
            - jnp.exp(jnp.sum(lq2 * lk2, axis=-1, keepdims=True)) + lam_init)


def _softmax_update(m_scr, l_scr, acc_scr, scores, values):
    idx = range(len(scores))
    m_old = [m_scr[n] for n in idx]
    m_new = [jnp.maximum(m_old[n], jnp.max(scores[n], axis=-1, keepdims=True)) for n in idx]
    p = [jnp.exp(scores[n] - m_new[n]) for n in idx]
    alpha = [jnp.exp(m_old[n] - m_new[n]) for n in idx]
    pv = [_dot(p[n].astype(BF16), values[n]) for n in idx]
    for n in idx:
        l_scr[n] = alpha[n] * l_scr[n] + jnp.sum(p[n], axis=-1, keepdims=True)
        acc_scr[n] = alpha[n] * acc_scr[n] + pv[n]
        m_scr[n] = m_new[n]


def _diff_finalize(o1, o2, lam_refs, g_ref, lam_init):
    lam = _lambda_value(lam_refs, lam_init)
    return _rms(o1 - lam * o2, g_ref[...]) * (1.0 - lam_init)


def _diff_init(m_scr, l_scr, acc_scr):
    m_scr[...] = jnp.full(m_scr.shape, -jnp.inf, F32)
    l_scr[...] = jnp.zeros(l_scr.shape, F32)
    acc_scr[...] = jnp.zeros(acc_scr.shape, F32)


def _near_bias(slope, r, q_pos, k_pos):
    visible = jnp.right_shift(k_pos, CHUNK_SHIFT) <= jnp.right_shift(q_pos, CHUNK_SHIFT)
    return jnp.where(visible, slope * (r - jnp.abs(q_pos - k_pos)).astype(F32), -jnp.inf)


def _split_distance(slope, d):
    return (-(slope * CHUNK) * jnp.right_shift(d, CHUNK_SHIFT).astype(F32),
            -slope * jnp.bitwise_and(d, CHUNK - 1).astype(F32))


def _diff_prompt_kernel(i_tab, j_tab, slopes_ref, q_ref, k_ref, v_ref, lq1, lk1, lq2, lk2, g_ref, o_ref,
                        m_scr, l_scr, acc_scr, lhs_scr, s_even, s_odd, *, tq, lam_init):
    h = pl.program_id(1)
    s_id = pl.program_id(2)
    n_pairs = pl.num_programs(2) - 1
    score_on = s_id < n_pairs
    consume_on = s_id >= 1
    i = i_tab[s_id]
    j = j_tab[s_id]
    ci = i_tab[jnp.maximum(s_id - 1, 0)]
    cj = j_tab[jnp.maximum(s_id - 1, 0)]
    slope = slopes_ref[h]
    half = DIFF_HEAD_DIM
    lane = lax.broadcasted_iota(jnp.int32, (tq, LANES), 1)

    @pl.when(jnp.logical_and(consume_on, cj == 0))
    def _():
        _diff_init(m_scr, l_scr, acc_scr)

    @pl.when(jnp.logical_and(score_on, j == 0))
    def _():
        qs = q_ref[...] * (DIFF_HEAD_DIM ** -0.5)
        lhs_scr[0] = jnp.where(lane < half, qs, 1.0).astype(BF16)
        lhs_scr[1] = jnp.where(lane >= half, qs, 1.0).astype(BF16)

    sub = min(DIFF_SUB, tq)
    units = [(n, qb) for qb in range(tq // sub) for n in range(2)]

    def score_tile(near, dst):
        kb = k_ref[...]
        if near:
            aug0 = aug1 = jnp.zeros_like(kb)
        else:
            d = (i - j) * tq - lax.broadcasted_iota(jnp.int32, (tq, LANES), 0)
            t_hi, t_lo = _split_distance(slope, d)
            aug0 = jnp.where(lane == half, t_hi, jnp.where(lane == half + 1, t_lo, 0.0))
            aug1 = jnp.where(lane == 0, t_hi, jnp.where(lane == 1, t_lo, 0.0))
        keys = [jnp.where(lane < half, kb, aug0).astype(BF16), jnp.where(lane >= half, kb, aug1).astype(BF16)]
        bias = None
        for n, qb in units:
            cols = slice(qb * sub, (qb + 1) * sub)
            if near:
                live = (qb + 1) * sub
                if n == 0:
                    k_pos = lax.broadcasted_iota(jnp.int32, (live, sub), 0)
                    q_pos = qb * sub + lax.broadcasted_iota(jnp.int32, (live, sub), 1)
                    bias = _near_bias(slope, q_pos, q_pos, k_pos)
                dst[n, :live, cols] = _dot_nt(keys[n][:live], lhs_scr[n, cols, :]) + bias
            else:
                dst[n, :, cols] = _dot_nt(keys[n], lhs_scr[n, cols, :])

    def consume_tile(near, src):
        vt = v_ref[...].T.astype(BF16)
        idx = range(len(units))
        cols = [slice(qb * sub, (qb + 1) * sub) for _, qb in units]
        live = [(qb + 1) * sub if near else tq for _, qb in units]
        s = [src[n, :live[u], cols[u]] for u, (n, _) in enumerate(units)]
        m_old = [m_scr[n, :, cols[u]] for u, (n, _) in enumerate(units)]
        m_new = [jnp.maximum(m_old[u], jnp.max(s[u], axis=0, keepdims=True)) for u in idx]
        p = [jnp.exp(s[u] - m_new[u]) for u in idx]
        alpha = [jnp.exp(m_old[u] - m_new[u]) for u in idx]
        pv = [_dot(vt[:, :live[u]], p[u].astype(BF16)) for u in idx]
        for u, (n, _) in enumerate(units):
            l_scr[n, :, cols[u]] = alpha[u] * l_scr[n, :, cols[u]] + jnp.sum(p[u], axis=0, keepdims=True)
            acc_scr[n, :, cols[u]] = alpha[u] * acc_scr[n, :, cols[u]] + pv[u]
            m_scr[n, :, cols[u]] = m_new[u]

    def stage(cond, score_near=None, consume_near=None):
        for parity, (dst, src) in enumerate(((s_even, s_odd), (s_odd, s_even))):
            @pl.when(jnp.logical_and(cond, s_id % 2 == parity))
            def _(dst=dst, src=src):
                if score_near is not None:
                    score_tile(score_near, dst)
                if consume_near is not None:
                    consume_tile(consume_near, src)

    both = jnp.logical_and(score_on, consume_on)
    stage(jnp.logical_and(both, jnp.logical_and(j < i, cj < ci)), score_near=False, consume_near=False)
    stage(jnp.logical_and(both, jnp.logical_and(j < i, cj == ci)), score_near=False, consume_near=True)
    stage(jnp.logical_and(both, j == i), score_near=True, consume_near=False)
    stage(jnp.logical_not(consume_on), score_near=True)
    stage(jnp.logical_not(score_on), consume_near=True)

    @pl.when(jnp.logical_and(consume_on, cj == ci))
    def _():
        o_ref[...] = _diff_finalize((acc_scr[0] / l_scr[0]).T, (acc_scr[1] / l_scr[1]).T,
                                    (lq1, lk1, lq2, lk2), g_ref, lam_init).astype(o_ref.dtype)


def _alibi_slopes():
    return jnp.asarray([2.0 ** (-8.0 * (i + 1) / DIFF_HEADS) for i in range(DIFF_HEADS)], F32)


def _diff_prompt(main, k_new, v_new, lams, subln_g, *, bsz, t, lam_init):
    tq = min(DIFF_TQ, t)
    nq = t // tq
    assert tq % CHUNK == 0 and t <= CHUNK * 255
    kern = functools.partial(_diff_prompt_kernel, tq=tq, lam_init=lam_init)
    qc, kc, vc = COL_DQ_MAIN // LANES, 0, 0
    pairs = [(i, j) for i in range(nq) for j in range(i + 1)]
    pairs.append(pairs[-1])
    i_tab = jnp.asarray([p[0] for p in pairs], jnp.int32)
    j_tab = jnp.asarray([p[1] for p in pairs], jnp.int32)
    zero = lambda b, h, s, it, jt: (0, 0)

    def consumed(tab, s):
        return tab[jnp.maximum(s - 1, 0)]

    grid_spec = pltpu.PrefetchScalarGridSpec(
        num_scalar_prefetch=2,
        grid=(bsz, DIFF_HEADS, len(pairs)),
        in_specs=[
            pl.BlockSpec(memory_space=pltpu.SMEM),
            pl.BlockSpec((tq, LANES), lambda b, h, s, it, jt: (b * nq + it[s], qc + h)),
            pl.BlockSpec((tq, LANES), lambda b, h, s, it, jt: (b * nq + jt[s], kc + h)),
            pl.BlockSpec((tq, LANES), lambda b, h, s, it, jt: (b * nq + consumed(jt, s), vc + h)),
        ] + [pl.BlockSpec((1, DIFF_HEAD_DIM), zero)] * 4 + [pl.BlockSpec((1, HEAD_DIM), zero)],
        out_specs=pl.BlockSpec((tq, LANES), lambda b, h, s, it, jt: (b * nq + consumed(it, s), h)),
        scratch_shapes=[pltpu.VMEM((2, 1, tq), F32), pltpu.VMEM((2, 1, tq), F32),
                        pltpu.VMEM((2, HEAD_DIM, tq), F32), pltpu.VMEM((2, tq, LANES), BF16),
                        pltpu.VMEM((2, tq, tq), F32), pltpu.VMEM((2, tq, tq), F32)],
    )
    return pl.pallas_call(
        kern,
        out_shape=jax.ShapeDtypeStruct((bsz * t, DIFF_HEADS * HEAD_DIM), BF16),
        grid_spec=grid_spec,
        compiler_params=_cparams(("parallel", "parallel", "arbitrary")),
        name="diff_attn_prompt",
    )(i_tab, j_tab, _alibi_slopes(), main, k_new, v_new, *lams, subln_g.reshape(1, -1))


def _diff_sample_kernel(slopes_ref, q_ref, kn_ref, vn_ref, kp_ref, vp_ref, lq1, lk1, lq2, lk2, g_ref, o_ref,
                        m_scr, l_scr, acc_scr, lhs_scr, *, t, tk, past, lam_init):
    j = pl.program_id(1)
    heads = range(DIFF_HEADS)

    def cols(h):
        return slice(h * HEAD_DIM, (h + 1) * HEAD_DIM)

    @pl.when(j == 0)
    def _():
        _diff_init(m_scr, l_scr, acc_scr)
        lane = lax.broadcasted_iota(jnp.int32, (t, LANES), 1)
        r = lax.broadcasted_iota(jnp.int32, (t, t), 0)
        c = lax.broadcasted_iota(jnp.int32, (t, t), 1)
        scores = []
        for h in heads:
            qs = q_ref[:, cols(h)] * (DIFF_HEAD_DIM ** -0.5)
            lhs_scr[h, 0:t, :] = jnp.where(lane < DIFF_HEAD_DIM, qs, 0.0).astype(BF16)
            lhs_scr[h, t:2 * t, :] = jnp.where(lane >= DIFF_HEAD_DIM, qs, 0.0).astype(BF16)
            bias = _near_bias(slopes_ref[h], r, past + r, past + c)
            scores.append(_dot_nt(lhs_scr[h], kn_ref[:, cols(h)].astype(BF16)) + jnp.concatenate([bias, bias], axis=0))
        _softmax_update(m_scr, l_scr, acc_scr, scores, [vn_ref[:, cols(h)].astype(BF16) for h in heads])

    d = (past - j * tk - lax.broadcasted_iota(jnp.int32, (1, tk), 1)).astype(F32)
    kt = jnp.swapaxes(kp_ref[...], 0, 1).astype(BF16)
    vt = jnp.swapaxes(vp_ref[...], 0, 1).astype(BF16)
    scores = [_dot_nt(lhs_scr[h], kt[h]) - slopes_ref[h] * d for h in heads]
    _softmax_update(m_scr, l_scr, acc_scr, scores, [vt[h] for h in heads])

    @pl.when(j == pl.num_programs(1) - 1)
    def _():
        for h in heads:
            o = acc_scr[h] / l_scr[h]
            o_ref[:, cols(h)] = _diff_finalize(o[:t], o[t:], (lq1, lk1, lq2, lk2), g_ref, lam_init).astype(o_ref.dtype)


def _diff_sample(main, k_new, v_new, k_past, v_past, lams, subln_g, *, bsz, t, lam_init):
    past = k_past.shape[1]
    tk = min(1024, past)
    nk = past // tk
    width = DIFF_HEADS * HEAD_DIM
    kern = functools.partial(_diff_sample_kernel, t=t, tk=tk, past=past, lam_init=lam_init)
    zero = lambda b, j: (0, 0)
    cache = pl.BlockSpec((None, tk, DIFF_HEADS, HEAD_DIM), lambda b, j: (b, j, 0, 0))
    return pl.pallas_call(
        kern,
        out_shape=jax.ShapeDtypeStruct((bsz * t, width), BF16),
        grid=(bsz, nk),
        in_specs=[
            pl.BlockSpec(memory_space=pltpu.SMEM),
            pl.BlockSpec((t, width), lambda b, j: (b, COL_DQ_MAIN // width)),
            pl.BlockSpec((t, width), lambda b, j: (b, 0)),
            pl.BlockSpec((t, width), lambda b, j: (b, 0)),
            cache, cache,
        ] + [pl.BlockSpec((1, DIFF_HEAD_DIM), zero)] * 4 + [pl.BlockSpec((1, HEAD_DIM), zero)],
        out_specs=pl.BlockSpec((t, width), lambda b, j: (b, 0)),
        scratch_shapes=[pltpu.VMEM((DIFF_HEADS, 2 * t, 1), F32), pltpu.VMEM((DIFF_HEADS, 2 * t, 1), F32),
                        pltpu.VMEM((DIFF_HEADS, 2 * t, HEAD_DIM), F32), pltpu.VMEM((DIFF_HEADS, 2 * t, LANES), BF16)],
        compiler_params=_cparams(("parallel", "arbitrary")),
        name="diff_attn_sample",
    )(_alibi_slopes(), main, k_new, v_new, k_past, v_past, *lams, subln_g.reshape(1, -1))


def _merge_kernel(oa_ref, ob_ref, wa_ref, wb_ref, ga_ref, gb_ref, o_ref):
    ya = _dot(oa_ref[...], wa_ref[...])
    yb = _dot(ob_ref[...], wb_ref[...])
    o_ref[...] = (jax.nn.sigmoid(ga_ref[...]) * ya + jax.nn.sigmoid(gb_ref[...]) * yb).astype(o_ref.dtype)


def _merge(o_a, o_b, wa, wb, tail, *, tm):
    m = o_a.shape[0]
    d = wa.shape[1]
    tn = 512
    ga0, gb0 = 0, COL_GB_TAIL // tn
    return pl.pallas_call(
        _merge_kernel,
        out_shape=jax.ShapeDtypeStruct((m, d), BF16),
        grid=(m // tm, d // tn),
        in_specs=[
            pl.BlockSpec((tm, o_a.shape[1]), lambda i, j: (i, 0)),
            pl.BlockSpec((tm, o_b.shape[1]), lambda i, j: (i, 0)),
            pl.BlockSpec((wa.shape[0], tn), lambda i, j: (0, j)),
            pl.BlockSpec((wb.shape[0], tn), lambda i, j: (0, j)),
            pl.BlockSpec((tm, tn), lambda i, j: (i, ga0 + j)),
            pl.BlockSpec((tm, tn), lambda i, j: (i, gb0 + j)),
        ],
        out_specs=pl.BlockSpec((tm, tn), lambda i, j: (i, j)),
        compiler_params=_cparams(("parallel", "arbitrary")),
        name="merge",
    )(o_a, o_b, wa, wb, tail, tail)


def _cross_kernel(q_ref, k_ref, v_ref, o_ref):
    dh = q_ref.shape[-1] // MEM_HEADS
    for h in range(MEM_HEADS):
        sl = slice(h * dh, (h + 1) * dh)
        s = _dot_nt(q_ref[:, sl], k_ref[0, :, sl].astype(BF16)) * (dh ** -0.5)
        s = s - jnp.max(s, axis=-1, keepdims=True)
        p = jnp.exp(s)
        p = p / jnp.sum(p, axis=-1, keepdims=True)
        o_ref[:, sl] = _dot(p.astype(BF16), v_ref[0, :, sl].astype(BF16)).astype(o_ref.dtype)


def _cross_attend(q, mem_k, mem_v, *, bsz, t, tm):
    d = q.shape[1]
    nt = t // tm
    mt = mem_k.shape[1]
    return pl.pallas_call(
        _cross_kernel,
        out_shape=jax.ShapeDtypeStruct(q.shape, BF16),
        grid=(bsz, nt),
        in_specs=[
            pl.BlockSpec((tm, d), lambda b, i: (b * nt + i, 0)),
            pl.BlockSpec((1, mt, d), lambda b, i: (b, 0, 0)),
            pl.BlockSpec((1, mt, d), lambda b, i: (b, 0, 0)),
        ],
        out_specs=pl.BlockSpec((tm, d), lambda b, i: (b * nt + i, 0)),
        compiler_params=_cparams(("parallel", "arbitrary")),
        name="cross_attn",
    )(q, mem_k, mem_v)


def _topk_rows(work, cidx, k):
    n, tn = work.shape
    rid = lax.broadcasted_iota(jnp.int32, (n, tn), 0).astype(F32)
    kid = lax.broadcasted_iota(jnp.int32, (k, tn), 0)
    vals = jnp.zeros((k, tn), F32)
    idxs = jnp.zeros((k, tn), F32)
    for t in range(k):
        m = jnp.max(work, axis=0, keepdims=True)
        pos = jnp.min(jnp.where(work == m, rid, float(n)), axis=0, keepdims=True)
        hit = rid == pos
        if cidx is None:
            picked = pos
        else:
            picked = jnp.sum(jnp.where(hit, cidx, 0.0), axis=0, keepdims=True)
        vals = jnp.where(kid == t, m, vals)
        idxs = jnp.where(kid == t, picked, idxs)
        work = jnp.where(hit, -jnp.inf, work)
    return vals, idxs


def _product_candidates(v1, i1, v2, i2, k):
    assert k == 2 * SUBLANES
    sub = lax.broadcasted_iota(jnp.int32, (SUBLANES, v1.shape[1]), 0)
    vals, idxs = [], []
    for a in range(k // 2):
        nb = k // (a + 1)
        width = k if nb > SUBLANES else SUBLANES
        cv = v1[a:a + 1, :] + v2[:width, :]
        ci = i1[a:a + 1, :] * PEER_N_KEYS + i2[:width, :]
        if nb < SUBLANES:
            cv = jnp.where(sub < nb, cv, -jnp.inf)
        vals.append(cv)
        idxs.append(ci)
    vals.append(v1[k // 2:, :] + v2[0:1, :])
    idxs.append(i1[k // 2:, :] * PEER_N_KEYS + i2[0:1, :])
    return jnp.concatenate(vals, axis=0), jnp.concatenate(idxs, axis=0)


def _peer_route_kernel(q_ref, keys_hi_ref, keys_lo_ref, e_ref, g_ref):
    k = PEER_TOPK
    for h in range(PEER_HEADS):
        sub = []
        for p in range(2):
            c0 = (h * 2 + p) * PEER_N_KEYS
            q_parts = _split_bf16(q_ref[:, c0:c0 + PEER_N_KEYS])
            s = _dot3_parts((keys_hi_ref[h, p], keys_lo_ref[h, p]), q_parts, _dot_nt)
            sub.append(_topk_rows(s, None, k))
        (v1, i1), (v2, i2) = sub
        cand, cidx = _product_candidates(v1, i1, v2, i2, k)
        best, eidx = _topk_rows(cand, cidx, k)
        ex = jnp.exp(best - jnp.max(best, axis=0, keepdims=True))
        gate = ex / jnp.sum(ex, axis=0, keepdims=True)
        e_ref[h * k:(h + 1) * k, :] = eidx.astype(jnp.int32)
        g_ref[h * k:(h + 1) * k, :] = gate


PEER_GATE_TN = 256


def _peer_gate_kernel(et_ref, gt_ref, o_ref, e_scr, g_scr):
    tg = et_ref.shape[1]
    e_scr[...] = et_ref[...].T
    g_scr[...] = gt_ref[...].T
    kid = lax.broadcasted_iota(jnp.int32, (PEER_N_KEYS, PEER_N_KEYS), 0)
    group = 2 * SUBLANES

    def body(n, carry):
        rows = pl.ds(pl.multiple_of(n * group, group), group)
        e16 = e_scr[rows, :]
        g16 = g_scr[rows, :]
        a16 = jnp.right_shift(e16, 7)
        b16 = jnp.bitwise_and(e16, PEER_N_KEYS - 1)
        toks = range(group)
        at = [jnp.where(a16[s:s + 1, :] == kid, g16[s:s + 1, :], 0.0).astype(BF16) for s in toks]
        bt = [jnp.where(b16[s:s + 1, :] == kid, 1.0, 0.0).astype(BF16) for s in toks]
        grids = [_dot_nt(at[s], bt[s]) for s in toks]
        o_ref[:, rows, :] = jnp.swapaxes(jnp.stack(grids, axis=0), 0, 1).astype(BF16)
        return carry

    lax.fori_loop(0, tg // group, body, 0, unroll=8)


def _peer_route(qp, sub_keys, *, tn):
    n = qp.shape[0]
    rows = PEER_HEADS * PEER_TOPK
    keys_hi, keys_lo = _split_bf16(sub_keys.astype(F32))
    keys_spec = pl.BlockSpec(sub_keys.shape, lambda i: (0, 0, 0, 0))
    e_t, g_t = pl.pallas_call(
        _peer_route_kernel,
        out_shape=(jax.ShapeDtypeStruct((rows, n), jnp.int32), jax.ShapeDtypeStruct((rows, n), F32)),
        grid=(n // tn,),
        in_specs=[pl.BlockSpec((tn, qp.shape[1]), lambda i: (i, 0)), keys_spec, keys_spec],
        out_specs=(pl.BlockSpec((rows, tn), lambda i: (0, i)), pl.BlockSpec((rows, tn), lambda i: (0, i))),
        compiler_params=_cparams(("parallel",)),
        name="peer_route",
    )(qp, keys_hi, keys_lo)
    tg = min(PEER_GATE_TN, n)
    return pl.pallas_call(
        _peer_gate_kernel,
        out_shape=jax.ShapeDtypeStruct((PEER_N_KEYS, n, PEER_N_KEYS), BF16),
        grid=(n // tg,),
        in_specs=[pl.BlockSpec((rows, tg), lambda i: (0, i)), pl.BlockSpec((rows, tg), lambda i: (0, i))],
        out_specs=pl.BlockSpec((PEER_N_KEYS, tg, PEER_N_KEYS), lambda i: (0, i, 0)),
        scratch_shapes=[pltpu.VMEM((tg, rows), jnp.int32), pltpu.VMEM((tg, rows), F32)],
        compiler_params=_cparams(("parallel",)),
        name="peer_gates",
    )(e_t, g_t)


def _gelu_tanh(x):
    return 0.5 * x * (1.0 + jnp.tanh(math.sqrt(2.0 / math.pi) * (x + 0.044715 * (x * x * x))))


PEER_BLOCKS_PER_STEP = 4
PEER_TN = 1024
PEER_VMEM_LIMIT = 60 * 1024 * 1024


def _peer_expert_kernel(x_ref, gn_ref, *rest):
    nblk = PEER_BLOCKS_PER_STEP
    gate_refs, u_refs, v_refs = rest[:nblk], rest[nblk:2 * nblk], rest[2 * nblk:3 * nblk]
    gf_ref, o_ref, h_scr, acc_scr, coef_scr = rest[3 * nblk:]
    s = pl.program_id(1)
    last = pl.num_programs(1) - 1

    @pl.when(s == 0)
    def _():
        h_scr[...] = _rms(x_ref[...], gn_ref[...]).astype(BF16)
        acc_scr[...] = jnp.zeros(acc_scr.shape, F32)
        coef_scr[...] = jnp.zeros(coef_scr.shape, BF16)

    v_cat = jnp.concatenate([v[...].astype(BF16) for v in v_refs], axis=0)
    acc_scr[...] += _dot(coef_scr[(s + 1) % 2], v_cat)

    gate = jnp.concatenate([g[...].astype(F32) for g in gate_refs], axis=1)
    u_cat = jnp.concatenate([u[...].astype(BF16) for u in u_refs], axis=0)
    hid = _gelu_tanh(_dot_nt(h_scr[...], u_cat))
    coef_scr[s % 2] = (gate * hid).astype(BF16)

    @pl.when(s == last)
    def _():
        o_ref[...] = _rms(x_ref[...] + acc_scr[...], gf_ref[...])


def _peer_experts(x, norm_g, gates, u_tab, v_tab, final_g, *, tn):
    n, d = x.shape
    eb = PEER_N_KEYS
    assert u_tab.shape[0] == eb * eb
    nblk = PEER_BLOCKS_PER_STEP
    ns = eb // nblk

    def cur(s):
        return jnp.minimum(s, ns - 1)

    def prev(s):
        return jnp.maximum(s - 1, 0)

    def table(which, q):
        return pl.BlockSpec((eb, d), lambda i, s: (which(s) + q * ns, 0))

    return pl.pallas_call(
        _peer_expert_kernel,
        out_shape=jax.ShapeDtypeStruct((n, d), F32),
        grid=(n // tn, ns + 1),
        in_specs=[
            pl.BlockSpec((tn, d), lambda i, s: (i, 0), pipeline_mode=pl.Buffered(1)),
            pl.BlockSpec((1, d), lambda i, s: (0, 0)),
        ] + [pl.BlockSpec((None, tn, eb), lambda i, s, q=q: (cur(s) + q * ns, i, 0)) for q in range(nblk)] + [
            table(cur, q) for q in range(nblk)] + [table(prev, q) for q in range(nblk)] + [
            pl.BlockSpec((1, d), lambda i, s: (0, 0)),
        ],
        out_specs=pl.BlockSpec((tn, d), lambda i, s: (i, 0), pipeline_mode=pl.Buffered(1)),
        scratch_shapes=[pltpu.VMEM((tn, d), BF16), pltpu.VMEM((tn, d), F32),
                        pltpu.VMEM((2, tn, nblk * eb), BF16)],
        compiler_params=_cparams(("parallel", "arbitrary"), vmem=PEER_VMEM_LIMIT),
        name="peer_experts",
    )(x, norm_g.reshape(1, d), *([gates] * nblk), *([u_tab] * nblk), *([v_tab] * nblk), final_g.reshape(1, d))


def _layer(x, wts, layer, mem_k, mem_v, conv_prev, gdn_state, k_past, v_past, *, tm):
    bsz, t, d = x.shape
    n = bsz * t
    x2d = x.reshape(n, d)
    main, k_new, v_new, tail = _in_proj(x2d, wts["norm_mix_g"], wts["w_in"], tm=min(IN_TM, n))
    conv_dim = 3 * GDN_WIDTH
    conv_prev8 = jnp.concatenate(
        [jnp.zeros((bsz, SUBLANES - (GDN_CONV - 1), conv_dim), F32), conv_prev.astype(F32)], axis=1)
    o_a, s_new = _gdn(main, tail, conv_prev8, gdn_state.astype(F32), wts["gdn_conv_w"], wts["gdn_a_log"],
                      wts["gdn_dt_bias"], wts["gdn_norm_g"], bsz=bsz, t=t)
    conv_new = main.reshape(bsz, t, -1)[:, t - (GDN_CONV - 1):, :conv_dim]
    lam_init = 0.8 - 0.6 * math.exp(-0.3 * layer)
    lams = tuple(wts[k].reshape(1, -1) for k in ("diff_lambda_q1", "diff_lambda_k1", "diff_lambda_q2", "diff_lambda_k2"))
    if k_past is None:
        o_b = _diff_prompt(main, k_new, v_new, lams, wts["diff_subln_g"], bsz=bsz, t=t, lam_init=lam_init)
    else:
        o_b = _diff_sample(main, k_new, v_new, k_past, v_past, lams, wts["diff_subln_g"], bsz=bsz, t=t,
                           lam_init=lam_init)
    k_rows = k_new.reshape(bsz, t, DIFF_HEADS, HEAD_DIM)
    v_rows = v_new.reshape(bsz, t, DIFF_HEADS, HEAD_DIM)
    merged = _merge(o_a, o_b, wts["w_branch_a"], wts["w_branch_b"], tail, tm=tm)
    x1 = _matmul(merged, wts["w_out"], residual=x2d, tm=tm, tn=MM_TN, name="out_proj")
    qm = _matmul(x1, wts["w_mq"], norm_g=wts["norm_cross_g"], tm=tm, tn=MM_TN, out_dtype=BF16, name="mem_q")
    mt = mem_k.shape[1]
    oc = _cross_attend(qm, mem_k.reshape(bsz, mt, d), mem_v.reshape(bsz, mt, d), bsz=bsz, t=t, tm=min(tm, t))
    x2 = _matmul(oc, wts["w_mo"], residual=x1, tm=tm, tn=MM_TN, name="mem_o")
    qp = _matmul(x2, wts["peer_w_q"], norm_g=wts["norm_ffn_g"], tm=tm, tn=MM_TN, name="peer_q")
    tn = min(PEER_TN, n)
    gates = _peer_route(qp, wts["peer_sub_keys"], tn=LANES)
    y = _peer_experts(x2, wts["norm_ffn_g"], gates, wts["peer_u"], wts["peer_v"], wts["final_norm_g"], tn=tn)
    return y.reshape(bsz, t, d), k_rows, v_rows, s_new, conv_new


def kernel(x_prompt, x_sample, cache_diff_k, cache_diff_v, state_gdn, state_conv, cache_mem_k, cache_mem_v,
           mem_prompt, norm_mix_g, w_in, gdn_conv_w, gdn_a_log, gdn_dt_bias, gdn_norm_g,
           diff_lambda_q1, diff_lambda_k1, diff_lambda_q2, diff_lambda_k2, diff_subln_g,
           w_branch_a, w_branch_b, w_out, norm_cross_g, norm_mem_g, w_mq, w_mk, w_mv, w_mo,
           norm_ffn_g, peer_w_q, peer_sub_keys, peer_u, peer_v, final_norm_g):
    depth = w_in.shape[0]
    assert depth == 1, "final norm is fused into the last layer's PEER kernel"
    l = 0
    wts = {
        "norm_mix_g": norm_mix_g[l], "w_in": _in_proj_weights(w_in[l]), "gdn_conv_w": gdn_conv_w[l],
        "gdn_a_log": gdn_a_log[l], "gdn_dt_bias": gdn_dt_bias[l], "gdn_norm_g": gdn_norm_g[l],
        "diff_lambda_q1": diff_lambda_q1[l], "diff_lambda_k1": diff_lambda_k1[l],
        "diff_lambda_q2": diff_lambda_q2[l], "diff_lambda_k2": diff_lambda_k2[l],
        "diff_subln_g": diff_subln_g[l], "w_branch_a": w_branch_a[l].astype(BF16),
        "w_branch_b": w_branch_b[l].astype(BF16), "w_out": w_out[l].astype(BF16),
        "norm_cross_g": norm_cross_g[l], "w_mq": w_mq[l].astype(BF16), "w_mo": w_mo[l].astype(BF16),
        "norm_ffn_g": norm_ffn_g[l], "peer_w_q": peer_w_q[l].astype(BF16), "peer_sub_keys": peer_sub_keys[l],
        "peer_u": peer_u[l], "peer_v": peer_v[l], "final_norm_g": final_norm_g,
    }
    bp, tp, d = x_prompt.shape
    bs, ts, _ = x_sample.shape
    mem2d = mem_prompt.reshape(-1, d)
    tmm = min(512, mem2d.shape[0])
    mem_k = _matmul(mem2d, w_mk[l], norm_g=norm_mem_g[l], tm=tmm, tn=512, name="mem_k")
    mem_v = _matmul(mem2d, w_mv[l], norm_g=norm_mem_g[l], tm=tmm, tn=512, name="mem_v")
    mshape = mem_prompt.shape[:2] + (MEM_HEADS, d // MEM_HEADS)
    mem_k = mem_k.reshape(mshape)
    mem_v = mem_v.reshape(mshape)
    conv0 = jnp.zeros((bp, GDN_CONV - 1, 3 * GDN_WIDTH), F32)
    s0 = jnp.zeros((bp, GDN_HEADS, HEAD_DIM, HEAD_DIM), F32)
    yp, pk, pv, ps, pc = _layer(x_prompt, wts, l, mem_k, mem_v, conv0, s0, None, None, tm=min(1024, bp * tp))
    ys, sk, sv, ss, sc = _layer(x_sample, wts, l, cache_mem_k[l], cache_mem_v[l], state_conv[l], state_gdn[l],
                                cache_diff_k[l], cache_diff_v[l], tm=min(256, bs * ts))
    return (yp, ys, pk[None], pv[None], ps[None], pc[None], mem_k[None], mem_v[None],
            sk[None], sv[None], ss[None], sc[None])
```

```python
import functools
import math

import jax
import jax.numpy as jnp
from jax import lax
from jax.experimental import pallas as pl
from jax.experimental.pallas import tpu as pltpu

F32 = jnp.float32
BF16 = jnp.bfloat16
HIGHEST = lax.Precision.HIGHEST

RMS_EPS = 1e-6
CHUNK = 64
CHUNK_SHIFT = 6
DIFF_TQ = 1024
GDN_CHUNKS_PER_STEP = 4
DIFF_SUB = 256
GDN_HEADS = 8
HEAD_DIM = 128
GDN_WIDTH = GDN_HEADS * HEAD_DIM
GDN_CONV = 4
DIFF_HEADS = 8
DIFF_HEAD_DIM = 64
MEM_HEADS = 4
PEER_HEADS = 8
PEER_N_KEYS = 128
PEER_TOPK = 16
LANES = 128
SUBLANES = 8
V7X_VMEM_LIMIT = 52 * 1024 * 1024
MM_TN = 1024


def _cparams(sem, vmem=V7X_VMEM_LIMIT):
    return pltpu.CompilerParams(dimension_semantics=sem, vmem_limit_bytes=vmem)


def _dot(a, b, precision=None):
    return jnp.dot(a, b, preferred_element_type=F32, precision=precision)


def _dot_nt(a, b, precision=None):
    return lax.dot_general(a, b, (((1,), (1,)), ((), ())), preferred_element_type=F32, precision=precision)


def _split_bf16(x):
    hi = x.astype(BF16)
    return hi, (x - hi.astype(F32)).astype(BF16)


def _cat_parts(p, q):
    return jnp.concatenate([p[0], q[0]], axis=0), jnp.concatenate([p[1], q[1]], axis=0)


def _dot3_parts(a_parts, b_parts, dot):
    a_hi, a_lo = a_parts
    b_hi, b_lo = b_parts
    m = a_hi.shape[0]
    both = dot(jnp.concatenate([a_hi, a_lo], axis=0), b_hi)
    return both[:m] + both[m:] + dot(a_hi, b_lo)


def _dot3(a, b):
    return _dot3_parts(_split_bf16(a), _split_bf16(b), _dot)


def _rms(xf, g):
    return xf * lax.rsqrt(jnp.mean(xf * xf, axis=-1, keepdims=True) + RMS_EPS) * g


def _mm_kernel(*refs, has_norm, has_res):
    it = iter(refs)
    x_ref = next(it)
    g_ref = next(it) if has_norm else None
    w_ref = next(it)
    r_ref = next(it) if has_res else None
    o_ref = next(it)
    h_scr = next(it)

    @pl.when(pl.program_id(1) == 0)
    def _():
        xf = x_ref[...].astype(F32)
        if has_norm:
            xf = _rms(xf, g_ref[...])
        h_scr[...] = xf.astype(BF16)

    acc = _dot(h_scr[...], w_ref[...].astype(BF16))
    if has_res:
        acc = acc + r_ref[...]
    o_ref[...] = acc.astype(o_ref.dtype)


def _matmul(x, w, *, norm_g=None, residual=None, tm, tn, out_dtype=F32, name="matmul"):
    m, k = x.shape
    n = w.shape[1]
    assert m % tm == 0 and n % tn == 0, (m, n, tm, tn)
    in_specs = [pl.BlockSpec((tm, k), lambda i, j: (i, 0))]
    args = [x]
    if norm_g is not None:
        in_specs.append(pl.BlockSpec((1, k), lambda i, j: (0, 0)))
        args.append(norm_g.reshape(1, k).astype(F32))
    in_specs.append(pl.BlockSpec((k, tn), lambda i, j: (0, j)))
    args.append(w)
    if residual is not None:
        in_specs.append(pl.BlockSpec((tm, tn), lambda i, j: (i, j)))
        args.append(residual)
    return pl.pallas_call(
        functools.partial(_mm_kernel, has_norm=norm_g is not None, has_res=residual is not None),
        out_shape=jax.ShapeDtypeStruct((m, n), out_dtype),
        grid=(m // tm, n // tn),
        in_specs=in_specs,
        out_specs=pl.BlockSpec((tm, tn), lambda i, j: (i, j)),
        scratch_shapes=[pltpu.VMEM((tm, k), BF16)],
        compiler_params=_cparams(("parallel", "arbitrary")),
        name=name,
    )(*args)


IN_TN = 512
IN_TM = 1024
IN_W1_TILES = 8
IN_W2_TILES = 14
IN_MAIN_TILES = 10
IN_K_TILE0 = 10
IN_V_TILE0 = 12
IN_TAIL_TILE0 = 14
IN_TILES = IN_W1_TILES + IN_W2_TILES + 1
MAIN_COLS = IN_MAIN_TILES * IN_TN
TAIL_COLS = (IN_TILES - IN_TAIL_TILE0) * IN_TN
COL_DQ_MAIN = 4096
COL_GB_TAIL = 2048
COL_BA_TAIL = 4096


def _in_proj_kernel(x_ref, g_ref, w1_ref, w2_ref, w3_ref, main_ref, k_ref, v_ref, tail_ref, h_scr):
    j = pl.program_id(1)

    @pl.when(j == 0)
    def _():
        h_scr[...] = _rms(x_ref[...], g_ref[...]).astype(BF16)

    routes = ((0, IN_W1_TILES, w1_ref, main_ref),
              (IN_W1_TILES, IN_K_TILE0, w2_ref, main_ref),
              (IN_K_TILE0, IN_V_TILE0, w2_ref, k_ref),
              (IN_V_TILE0, IN_TAIL_TILE0, w2_ref, v_ref),
              (IN_TAIL_TILE0, IN_TILES - 1, w2_ref, tail_ref),
              (IN_TILES - 1, IN_TILES, w3_ref, tail_ref))
    for lo, hi, w_ref, dst in routes:
        @pl.when(jnp.logical_and(j >= lo, j < hi))
        def _(w_ref=w_ref, dst=dst):
            dst[...] = _dot(h_scr[...], w_ref[...])


def _in_proj_weights(w_in):
    c_z = 4 * GDN_WIDTH
    c_ba = c_z + 2 * GDN_HEADS
    assert c_z == IN_W1_TILES * IN_TN and w_in.shape[1] - c_ba == IN_W2_TILES * IN_TN
    w_bf16 = w_in.astype(BF16)
    return w_bf16, w_bf16[:, c_ba:]


def _in_proj(x, norm_g, w_pieces, *, tm):
    m, d = x.shape
    w_all, w2 = w_pieces
    w1 = w3 = w_all
    tn = IN_TN
    out = lambda cols: jax.ShapeDtypeStruct((m, cols), F32)
    return pl.pallas_call(
        _in_proj_kernel,
        out_shape=(out(MAIN_COLS), out(2 * IN_TN), out(2 * IN_TN), out(TAIL_COLS)),
        grid=(m // tm, IN_TILES),
        in_specs=[
            pl.BlockSpec((tm, d), lambda i, j: (i, 0)),
            pl.BlockSpec((1, d), lambda i, j: (0, 0)),
            pl.BlockSpec((d, tn), lambda i, j: (0, jnp.minimum(j, IN_W1_TILES - 1))),
            pl.BlockSpec((d, tn), lambda i, j: (0, jnp.clip(j - IN_W1_TILES, 0, IN_W2_TILES - 1))),
            pl.BlockSpec((d, tn), lambda i, j: (0, IN_W1_TILES)),
        ],
        out_specs=(
            pl.BlockSpec((tm, tn), lambda i, j: (i, jnp.minimum(j, IN_MAIN_TILES - 1))),
            pl.BlockSpec((tm, tn), lambda i, j: (i, jnp.clip(j - IN_K_TILE0, 0, 1))),
            pl.BlockSpec((tm, tn), lambda i, j: (i, jnp.clip(j - IN_V_TILE0, 0, 1))),
            pl.BlockSpec((tm, tn), lambda i, j: (i, jnp.clip(j - IN_TAIL_TILE0, 0, IN_TILES - IN_TAIL_TILE0 - 1))),
        ),
        scratch_shapes=[pltpu.VMEM((tm, d), BF16)],
        compiler_params=_cparams(("parallel", "arbitrary")),
        name="in_proj",
    )(x, norm_g.reshape(1, d).astype(F32), w1, w2, w3)


def _gdn_kernel(qkv_ref, z_ref, ba_ref, prev_ref, s0_ref, convw_ref, alog_ref, dtb_ref, ng_ref,
                o_ref, sfin_ref, s_scr, xbuf, *, L, nch):
    step = pl.program_id(1)
    rows = nch * L

    @pl.when(step == 0)
    def _():
        s_scr[...] = s0_ref[0]
        xbuf[0:SUBLANES, :] = prev_ref[0]

    xbuf[SUBLANES:SUBLANES + rows, :] = qkv_ref[...]
    base = SUBLANES - (GDN_CONV - 1)
    y = xbuf[base:base + rows, :] * convw_ref[0:1, :]
    for i in range(1, GDN_CONV):
        y = y + xbuf[base + i:base + i + rows, :] * convw_ref[i:i + 1, :]
    y = y * jax.nn.sigmoid(y)
    carry = xbuf[rows:rows + SUBLANES, :]
    xbuf[0:SUBLANES, :] = carry

    row = lax.broadcasted_iota(jnp.int32, (L, L), 0)
    col = lax.broadcasted_iota(jnp.int32, (L, L), 1)
    lower = row >= col
    strict = row > col
    eye = (row == col).astype(F32)
    log_l = int(math.log2(L))
    n_double = log_l - 1
    assert 2 ** log_l == L

    ba = ba_ref[...]
    beta_all = jax.nn.sigmoid(ba)
    a_in = ba + dtb_ref[...]
    softplus = jnp.maximum(a_in, 0.0) + jnp.log(1.0 + jnp.exp(-jnp.abs(a_in)))
    g_all = -jnp.exp(alog_ref[...]) * softplus
    rr = lax.broadcasted_iota(jnp.int32, (rows, rows), 0)
    cc = lax.broadcasted_iota(jnp.int32, (rows, rows), 1)
    same_chunk_lower = jnp.logical_and(rr >= cc, jnp.right_shift(rr, log_l) == jnp.right_shift(cc, log_l))
    g_cum = _dot(same_chunk_lower.astype(F32), g_all, HIGHEST)
    g_cum_t = g_cum.T

    heads = range(GDN_HEADS)
    chains = [(c, h) for c in range(nch) for h in heads]
    ids = range(len(chains))

    def tok(c):
        return slice(c * L, (c + 1) * L)

    def head_cols(part, c, h):
        return y[tok(c), part * GDN_WIDTH + h * HEAD_DIM:part * GDN_WIDTH + (h + 1) * HEAD_DIM]

    q = [head_cols(0, c, h) for c, h in chains]
    k = [head_cols(1, c, h) for c, h in chains]
    v = [head_cols(2, c, h) for c, h in chains]
    q = [a * lax.rsqrt(jnp.sum(a * a, axis=-1, keepdims=True) + 1e-6) * (HEAD_DIM ** -0.5) for a in q]
    k = [a * lax.rsqrt(jnp.sum(a * a, axis=-1, keepdims=True) + 1e-6) for a in k]
    beta = [beta_all[tok(c), h:h + 1] for c, h in chains]
    g_col = [g_cum[tok(c), GDN_HEADS + h:GDN_HEADS + h + 1] for c, h in chains]
    g_row = [g_cum_t[GDN_HEADS + h:GDN_HEADS + h + 1, tok(c)] for c, h in chains]
    decay = [jnp.where(lower, jnp.exp(jnp.where(lower, g_col[n] - g_row[n], 0.0)), 0.0) for n in ids]
    g_last = [g_col[n][L - 1:L, :] for n in ids]
    e_col = [jnp.exp(g_col[n]) for n in ids]
    qk_kk = [_dot_nt(jnp.concatenate([q[n], k[n]], axis=0).astype(BF16), k[n].astype(BF16)) for n in ids]
    x = [-jnp.where(strict, beta[n] * qk_kk[n][L:] * decay[n], 0.0) for n in ids]
    t_inv = [eye + x[n] for n in ids]
    x_parts = [_split_bf16(x[n]) for n in ids]
    xp = [_dot3_parts(x_parts[n], x_parts[n], _dot) for n in ids]
    for _ in range(n_double - 1):
        xp_parts = [_split_bf16(xp[n]) for n in ids]
        both = [_dot3_parts(_cat_parts(_split_bf16(t_inv[n]), xp_parts[n]), xp_parts[n], _dot) for n in ids]
        t_inv = [t_inv[n] + both[n][:L] for n in ids]
        xp = [both[n][L:] for n in ids]
    t_inv = [t_inv[n] + _dot3(t_inv[n], xp[n]) for n in ids]
    sol = [_dot3(t_inv[n], jnp.concatenate([v[n] * beta[n], k[n] * (beta[n] * e_col[n])], axis=-1)) for n in ids]
    qk = [jnp.where(lower, qk_kk[n][:L] * decay[n], 0.0) for n in ids]
    k_dec_t = [(k[n] * jnp.exp(g_last[n] - g_col[n])).T for n in ids]
    left_s = [jnp.concatenate([sol[n][:, HEAD_DIM:], q[n] * e_col[n]], axis=0).astype(BF16) for n in ids]
    left_v = [jnp.concatenate([qk[n], k_dec_t[n]], axis=0).astype(BF16) for n in ids]

    s = [s_scr[h] for h in heads]
    for c in range(nch):
        n0 = c * GDN_HEADS
        ws_qs = [_dot(left_s[n0 + h], s[h].astype(BF16)) for h in heads]
        v_new = [sol[n0 + h][:, :HEAD_DIM] - ws_qs[h][:L] for h in heads]
        tail = [_dot(left_v[n0 + h], v_new[h].astype(BF16)) for h in heads]
        s = [s[h] * jnp.exp(g_last[n0 + h]) + tail[h][L:] for h in heads]
        for h in heads:
            zf = z_ref[tok(c), h * HEAD_DIM:(h + 1) * HEAD_DIM]
            o = _rms(ws_qs[h][L:] + tail[h][:L], ng_ref[...]) * (zf * jax.nn.sigmoid(zf))
            o_ref[tok(c), h * HEAD_DIM:(h + 1) * HEAD_DIM] = o.astype(o_ref.dtype)
    for h in heads:
        s_scr[h] = s[h]

    @pl.when(step == pl.num_programs(1) - 1)
    def _():
        sfin_ref[0] = s_scr[...]


def _decay_lanes(p):
    return jnp.zeros((1, LANES), F32).at[0, GDN_HEADS:2 * GDN_HEADS].set(p.astype(F32))


def _gdn(main, tail, conv_prev8, s0, conv_w, a_log, dt_bias, norm_g, *, bsz, t):
    L = min(CHUNK, t)
    nch = GDN_CHUNKS_PER_STEP if (t // L) % GDN_CHUNKS_PER_STEP == 0 else 1
    nc = t // (L * nch)
    rows = L * nch
    conv_dim = 3 * GDN_WIDTH
    kern = functools.partial(_gdn_kernel, L=L, nch=nch)
    o, s_fin = pl.pallas_call(
        kern,
        out_shape=(jax.ShapeDtypeStruct((bsz * t, GDN_WIDTH), BF16),
                   jax.ShapeDtypeStruct((bsz, GDN_HEADS, HEAD_DIM, HEAD_DIM), F32)),
        grid=(bsz, nc),
        in_specs=[
            pl.BlockSpec((rows, conv_dim), lambda b, c: (b * nc + c, 0)),
            pl.BlockSpec((rows, GDN_WIDTH), lambda b, c: (b * nc + c, conv_dim // GDN_WIDTH)),
            pl.BlockSpec((rows, LANES), lambda b, c: (b * nc + c, COL_BA_TAIL // LANES)),
            pl.BlockSpec((1, SUBLANES, conv_dim), lambda b, c: (b, 0, 0)),
            pl.BlockSpec((1, GDN_HEADS, HEAD_DIM, HEAD_DIM), lambda b, c: (b, 0, 0, 0)),
            pl.BlockSpec((GDN_CONV, conv_dim), lambda b, c: (0, 0)),
            pl.BlockSpec((1, LANES), lambda b, c: (0, 0)),
            pl.BlockSpec((1, LANES), lambda b, c: (0, 0)),
            pl.BlockSpec((1, HEAD_DIM), lambda b, c: (0, 0)),
        ],
        out_specs=(pl.BlockSpec((rows, GDN_WIDTH), lambda b, c: (b * nc + c, 0)),
                   pl.BlockSpec((1, GDN_HEADS, HEAD_DIM, HEAD_DIM), lambda b, c: (b, 0, 0, 0))),
        scratch_shapes=[pltpu.VMEM((GDN_HEADS, HEAD_DIM, HEAD_DIM), F32),
                        pltpu.VMEM((SUBLANES + rows + SUBLANES, conv_dim), F32)],
        compiler_params=_cparams(("parallel", "arbitrary")),
        name="gdn",
    )(main, main, tail, conv_prev8, s0, conv_w, _decay_lanes(a_log), _decay_lanes(dt_bias),
      norm_g.reshape(1, -1))
    return o, s_fin


def _lambda_value(lam_refs, lam_init):
    lq1, lk1, lq2, lk2 = (r[...] for r in lam_refs)
    return (jnp.exp(jnp.sum(lq1 * lk1, axis=-1, keepdims=True))
            - jnp.exp(jnp.sum(lq2 * lk2, axis=-1, keepdims=True)) + lam_init)


def _softmax_update(m_scr, l_scr, acc_scr, scores, values):
    idx = range(len(scores))
    m_old = [m_scr[n] for n in idx]
    m_new = [jnp.maximum(m_old[n], jnp.max(scores[n], axis=-1, keepdims=True)) for n in idx]
    p = [jnp.exp(scores[n] - m_new[n]) for n in idx]
    alpha = [jnp.exp(m_old[n] - m_new[n]) for n in idx]
    pv = [_dot(p[n].astype(BF16), values[n]) for n in idx]
    for n in idx:
        l_scr[n] = alpha[n] * l_scr[n] + jnp.sum(p[n], axis=-1, keepdims=True)
        acc_scr[n] = alpha[n] * acc_scr[n] + pv[n]
        m_scr[n] = m_new[n]


def _diff_finalize(o1, o2, lam_refs, g_ref, lam_init):
    lam = _lambda_value(lam_refs, lam_init)
    return _rms(o1 - lam * o2, g_ref[...]) * (1.0 - lam_init)


def _diff_init(m_scr, l_scr, acc_scr):
    m_scr[...] = jnp.full(m_scr.shape, -jnp.inf, F32)
    l_scr[...] = jnp.zeros(l_scr.shape, F32)
    acc_scr[...] = jnp.zeros(acc_scr.shape, F32)


def _near_bias(slope, r, q_pos, k_pos):
    visible = jnp.right_shift(k_pos, CHUNK_SHIFT) <= jnp.right_shift(q_pos, CHUNK_SHIFT)
    return jnp.where(visible, slope * (r - jnp.abs(q_pos - k_pos)).astype(F32), -jnp.inf)


def _split_distance(slope, d):
    return (-(slope * CHUNK) * jnp.right_shift(d, CHUNK_SHIFT).astype(F32),
            -slope * jnp.bitwise_and(d, CHUNK - 1).astype(F32))


def _diff_prompt_kernel(i_tab, j_tab, slopes_ref, q_ref, k_ref, v_ref, lq1, lk1, lq2, lk2, g_ref, o_ref,
                        m_scr, l_scr, acc_scr, lhs_scr, s_even, s_odd, *, tq, lam_init):
    h = pl.program_id(1)
    s_id = pl.program_id(2)
    n_pairs = pl.num_programs(2) - 1
    score_on = s_id < n_pairs
    consume_on = s_id >= 1
    i = i_tab[s_id]
    j = j_tab[s_id]
    ci = i_tab[jnp.maximum(s_id - 1, 0)]
    cj = j_tab[jnp.maximum(s_id - 1, 0)]
    slope = slopes_ref[h]
    half = DIFF_HEAD_DIM
    lane = lax.broadcasted_iota(jnp.int32, (tq, LANES), 1)

    @pl.when(jnp.logical_and(consume_on, cj == 0))
    def _():
        _diff_init(m_scr, l_scr, acc_scr)

    @pl.when(jnp.logical_and(score_on, j == 0))
    def _():
        qs = q_ref[...] * (DIFF_HEAD_DIM ** -0.5)
        lhs_scr[0] = jnp.where(lane < half, qs, 1.0).astype(BF16)
        lhs_scr[1] = jnp.where(lane >= half, qs, 1.0).astype(BF16)

    sub = min(DIFF_SUB, tq)
    units = [(n, qb) for qb in range(tq // sub) for n in range(2)]

    def score_tile(near, dst):
        kb = k_ref[...]
        if near:
            aug0 = aug1 = jnp.zeros_like(kb)
        else:
            d = (i - j) * tq - lax.broadcasted_iota(jnp.int32, (tq, LANES), 0)
            t_hi, t_lo = _split_distance(slope, d)
            aug0 = jnp.where(lane == half, t_hi, jnp.where(lane == half + 1, t_lo, 0.0))
            aug1 = jnp.where(lane == 0, t_hi, jnp.where(lane == 1, t_lo, 0.0))
        keys = [jnp.where(lane < half, kb, aug0).astype(BF16), jnp.where(lane >= half, kb, aug1).astype(BF16)]
        bias = None
        for n, qb in units:
            cols = slice(qb * sub, (qb + 1) * sub)
            if near:
                live = (qb + 1) * sub
                if n == 0:
                    k_pos = lax.broadcasted_iota(jnp.int32, (live, sub), 0)
                    q_pos = qb * sub + lax.broadcasted_iota(jnp.int32, (live, sub), 1)
                    bias = _near_bias(slope, q_pos, q_pos, k_pos)
                dst[n, :live, cols] = _dot_nt(keys[n][:live], lhs_scr[n, cols, :]) + bias
            else:
                dst[n, :, cols] = _dot_nt(keys[n], lhs_scr[n, cols, :])

    def consume_tile(near, src):
        vt = v_ref[...].T.astype(BF16)
        idx = range(len(units))
        cols = [slice(qb * sub, (qb + 1) * sub) for _, qb in units]
        live = [(qb + 1) * sub if near else tq for _, qb in units]
        s = [src[n, :live[u], cols[u]] for u, (n, _) in enumerate(units)]
        m_old = [m_scr[n, :, cols[u]] for u, (n, _) in enumerate(units)]
        m_new = [jnp.maximum(m_old[u], jnp.max(s[u], axis=0, keepdims=True)) for u in idx]
        p = [jnp.exp(s[u] - m_new[u]) for u in idx]
        alpha = [jnp.exp(m_old[u] - m_new[u]) for u in idx]
        pv = [_dot(vt[:, :live[u]], p[u].astype(BF16)) for u in idx]
        for u, (n, _) in enumerate(units):
            l_scr[n, :, cols[u]] = alpha[u] * l_scr[n, :, cols[u]] + jnp.sum(p[u], axis=0, keepdims=True)
            acc_scr[n, :, cols[u]] = alpha[u] * acc_scr[n, :, cols[u]] + pv[u]
            m_scr[n, :, cols[u]] = m_new[u]

    def stage(cond, score_near=None, consume_near=None):
        for parity, (dst, src) in enumerate(((s_even, s_odd), (s_odd, s_even))):
            @pl.when(jnp.logical_and(cond, s_id % 2 == parity))
            def _(dst=dst, src=src):
                if score_near is not None:
                    score_tile(score_near, dst)
                if consume_near is not None:
                    consume_tile(consume_near, src)

    both = jnp.logical_and(score_on, consume_on)
    stage(jnp.logical_and(both, jnp.logical_and(j < i, cj < ci)), score_near=False, consume_near=False)
    stage(jnp.logical_and(both, jnp.logical_and(j < i, cj == ci)), score_near=False, consume_near=True)
    stage(jnp.logical_and(both, j == i), score_near=True, consume_near=False)
    stage(jnp.logical_not(consume_on), score_near=True)
    stage(jnp.logical_not(score_on), consume_near=True)

    @pl.when(jnp.logical_and(consume_on, cj == ci))
    def _():
        o_ref[...] = _diff_finalize((acc_scr[0] / l_scr[0]).T, (acc_scr[1] / l_scr[1]).T,
                                    (lq1, lk1, lq2, lk2), g_ref, lam_init).astype(o_ref.dtype)


def _alibi_slopes():
    return jnp.asarray([2.0 ** (-8.0 * (i + 1) / DIFF_HEADS) for i in range(DIFF_HEADS)], F32)


def _diff_prompt(main, k_new, v_new, lams, subln_g, *, bsz, t, lam_init):
    tq = min(DIFF_TQ, t)
    nq = t // tq
    assert tq % CHUNK == 0 and t <= CHUNK * 255
    kern = functools.partial(_diff_prompt_kernel, tq=tq, lam_init=lam_init)
    qc, kc, vc = COL_DQ_MAIN // LANES, 0, 0
    pairs = [(i, j) for i in range(nq) for j in range(i + 1)]
    pairs.append(pairs[-1])
    i_tab = jnp.asarray([p[0] for p in pairs], jnp.int32)
    j_tab = jnp.asarray([p[1] for p in pairs], jnp.int32)
    zero = lambda b, h, s, it, jt: (0, 0)

    def consumed(tab, s):
        return tab[jnp.maximum(s - 1, 0)]

    grid_spec = pltpu.PrefetchScalarGridSpec(
        num_scalar_prefetch=2,
        grid=(bsz, DIFF_HEADS, len(pairs)),
        in_specs=[
            pl.BlockSpec(memory_space=pltpu.SMEM),
            pl.BlockSpec((tq, LANES), lambda b, h, s, it, jt: (b * nq + it[s], qc + h)),
            pl.BlockSpec((tq, LANES), lambda b, h, s, it, jt: (b * nq + jt[s], kc + h)),
            pl.BlockSpec((tq, LANES), lambda b, h, s, it, jt: (b * nq + consumed(jt, s), vc + h)),
        ] + [pl.BlockSpec((1, DIFF_HEAD_DIM), zero)] * 4 + [pl.BlockSpec((1, HEAD_DIM), zero)],
        out_specs=pl.BlockSpec((tq, LANES), lambda b, h, s, it, jt: (b * nq + consumed(it, s), h)),
        scratch_shapes=[pltpu.VMEM((2, 1, tq), F32), pltpu.VMEM((2, 1, tq), F32),
                        pltpu.VMEM((2, HEAD_DIM, tq), F32), pltpu.VMEM((2, tq, LANES), BF16),
                        pltpu.VMEM((2, tq, tq), F32), pltpu.VMEM((2, tq, tq), F32)],
    )
    return pl.pallas_call(
        kern,
        out_shape=jax.ShapeDtypeStruct((bsz * t, DIFF_HEADS * HEAD_DIM), BF16),
        grid_spec=grid_spec,
        compiler_params=_cparams(("parallel", "parallel", "arbitrary")),
        name="diff_attn_prompt",
    )(i_tab, j_tab, _alibi_slopes(), main, k_new, v_new, *lams, subln_g.reshape(1, -1))


def _diff_sample_kernel(slopes_ref, q_ref, kn_ref, vn_ref, kp_ref, vp_ref, lq1, lk1, lq2, lk2, g_ref, o_ref,
                        m_scr, l_scr, acc_scr, lhs_scr, *, t, tk, past, lam_init):
    j = pl.program_id(1)
    heads = range(DIFF_HEADS)

    def cols(h):
        return slice(h * HEAD_DIM, (h + 1) * HEAD_DIM)

    @pl.when(j == 0)
    def _():
        _diff_init(m_scr, l_scr, acc_scr)
        lane = lax.broadcasted_iota(jnp.int32, (t, LANES), 1)
        r = lax.broadcasted_iota(jnp.int32, (t, t), 0)
        c = lax.broadcasted_iota(jnp.int32, (t, t), 1)
        scores = []
        for h in heads:
            qs = q_ref[:, cols(h)] * (DIFF_HEAD_DIM ** -0.5)
            lhs_scr[h, 0:t, :] = jnp.where(lane < DIFF_HEAD_DIM, qs, 0.0).astype(BF16)
            lhs_scr[h, t:2 * t, :] = jnp.where(lane >= DIFF_HEAD_DIM, qs, 0.0).astype(BF16)
            bias = _near_bias(slopes_ref[h], r, past + r, past + c)
            scores.append(_dot_nt(lhs_scr[h], kn_ref[:, cols(h)].astype(BF16)) + jnp.concatenate([bias, bias], axis=0))
        _softmax_update(m_scr, l_scr, acc_scr, scores, [vn_ref[:, cols(h)].astype(BF16) for h in heads])

    d = (past - j * tk - lax.broadcasted_iota(jnp.int32, (1, tk), 1)).astype(F32)
    kt = jnp.swapaxes(kp_ref[...], 0, 1).astype(BF16)
    vt = jnp.swapaxes(vp_ref[...], 0, 1).astype(BF16)
    scores = [_dot_nt(lhs_scr[h], kt[h]) - slopes_ref[h] * d for h in heads]
    _softmax_update(m_scr, l_scr, acc_scr, scores, [vt[h] for h in heads])

    @pl.when(j == pl.num_programs(1) - 1)
    def _():
        for h in heads:
            o = acc_scr[h] / l_scr[h]
            o_ref[:, cols(h)] = _diff_finalize(o[:t], o[t:], (lq1, lk1, lq2, lk2), g_ref, lam_init).astype(o_ref.dtype)


def _diff_sample(main, k_new, v_new, k_past, v_past, lams, subln_g, *, bsz, t, lam_init):
    past = k_past.shape[1]
    tk = min(1024, past)
    nk = past // tk
    width = DIFF_HEADS * HEAD_DIM
    kern = functools.partial(_diff_sample_kernel, t=t, tk=tk, past=past, lam_init=lam_init)
    zero = lambda b, j: (0, 0)
    cache = pl.BlockSpec((None, tk, DIFF_HEADS, HEAD_DIM), lambda b, j: (b, j, 0, 0))
    return pl.pallas_call(
        kern,
        out_shape=jax.ShapeDtypeStruct((bsz * t, width), BF16),
        grid=(bsz, nk),
        in_specs=[
            pl.BlockSpec(memory_space=pltpu.SMEM),
            pl.BlockSpec((t, width), lambda b, j: (b, COL_DQ_MAIN // width)),
            pl.BlockSpec((t, width), lambda b, j: (b, 0)),
            pl.BlockSpec((t, width), lambda b, j: (b, 0)),
            cache, cache,
        ] + [pl.BlockSpec((1, DIFF_HEAD_DIM), zero)] * 4 + [pl.BlockSpec((1, HEAD_DIM), zero)],
        out_specs=pl.BlockSpec((t, width), lambda b, j: (b, 0)),
        scratch_shapes=[pltpu.VMEM((DIFF_HEADS, 2 * t, 1), F32), pltpu.VMEM((DIFF_HEADS, 2 * t, 1), F32),
                        pltpu.VMEM((DIFF_HEADS, 2 * t, HEAD_DIM), F32), pltpu.VMEM((DIFF_HEADS, 2 * t, LANES), BF16)],
        compiler_params=_cparams(("parallel", "arbitrary")),
        name="diff_attn_sample",
    )(_alibi_slopes(), main, k_new, v_new, k_past, v_past, *lams, subln_g.reshape(1, -1))


def _merge_kernel(oa_ref, ob_ref, wa_ref, wb_ref, ga_ref, gb_ref, o_ref):
    ya = _dot(oa_ref[...], wa_ref[...])
    yb = _dot(ob_ref[...], wb_ref[...])
    o_ref[...] = (jax.nn.sigmoid(ga_ref[...]) * ya + jax.nn.sigmoid(gb_ref[...]) * yb).astype(o_ref.dtype)


def _merge(o_a, o_b, wa, wb, tail, *, tm):
    m = o_a.shape[0]
    d = wa.shape[1]
    tn = 512
    ga0, gb0 = 0, COL_GB_TAIL // tn
    return pl.pallas_call(
        _merge_kernel,
        out_shape=jax.ShapeDtypeStruct((m, d), BF16),
        grid=(m // tm, d // tn),
        in_specs=[
            pl.BlockSpec((tm, o_a.shape[1]), lambda i, j: (i, 0)),
            pl.BlockSpec((tm, o_b.shape[1]), lambda i, j: (i, 0)),
            pl.BlockSpec((wa.shape[0], tn), lambda i, j: (0, j)),
            pl.BlockSpec((wb.shape[0], tn), lambda i, j: (0, j)),
            pl.BlockSpec((tm, tn), lambda i, j: (i, ga0 + j)),
            pl.BlockSpec((tm, tn), lambda i, j: (i, gb0 + j)),
        ],
        out_specs=pl.BlockSpec((tm, tn), lambda i, j: (i, j)),
        compiler_params=_cparams(("parallel", "arbitrary")),
        name="merge",
    )(o_a, o_b, wa, wb, tail, tail)


def _cross_kernel(q_ref, k_ref, v_ref, o_ref):
    dh = q_ref.shape[-1] // MEM_HEADS
    for h in range(MEM_HEADS):
        sl = slice(h * dh, (h + 1) * dh)
        s = _dot_nt(q_ref[:, sl], k_ref[0, :, sl].astype(BF16)) * (dh ** -0.5)
        s = s - jnp.max(s, axis=-1, keepdims=True)
        p = jnp.exp(s)
        p = p / jnp.sum(p, axis=-1, keepdims=True)
        o_ref[:, sl] = _dot(p.astype(BF16), v_ref[0, :, sl].astype(BF16)).astype(o_ref.dtype)


def _cross_attend(q, mem_k, mem_v, *, bsz, t, tm):
    d = q.shape[1]
    nt = t // tm
    mt = mem_k.shape[1]
    return pl.pallas_call(
        _cross_kernel,
        out_shape=jax.ShapeDtypeStruct(q.shape, BF16),
        grid=(bsz, nt),
        in_specs=[
            pl.BlockSpec((tm, d), lambda b, i: (b * nt + i, 0)),
            pl.BlockSpec((1, mt, d), lambda b, i: (b, 0, 0)),
            pl.BlockSpec((1, mt, d), lambda b, i: (b, 0, 0)),
        ],
        out_specs=pl.BlockSpec((tm, d), lambda b, i: (b * nt + i, 0)),
        compiler_params=_cparams(("parallel", "arbitrary")),
        name="cross_attn",
    )(q, mem_k, mem_v)


def _topk_rows(work, cidx, k):
    n, tn = work.shape
    rid = lax.broadcasted_iota(jnp.int32, (n, tn), 0).astype(F32)
    kid = lax.broadcasted_iota(jnp.int32, (k, tn), 0)
    vals = jnp.zeros((k, tn), F32)
    idxs = jnp.zeros((k, tn), F32)
    for t in range(k):
        m = jnp.max(work, axis=0, keepdims=True)
        pos = jnp.min(jnp.where(work == m, rid, float(n)), axis=0, keepdims=True)
        hit = rid == pos
        if cidx is None:
            picked = pos
        else:
            picked = jnp.sum(jnp.where(hit, cidx, 0.0), axis=0, keepdims=True)
        vals = jnp.where(kid == t, m, vals)
        idxs = jnp.where(kid == t, picked, idxs)
        work = jnp.where(hit, -jnp.inf, work)
    return vals, idxs


def _product_candidates(v1, i1, v2, i2, k):
    assert k == 2 * SUBLANES
    sub = lax.broadcasted_iota(jnp.int32, (SUBLANES, v1.shape[1]), 0)
    vals, idxs = [], []
    for a in range(k // 2):
        nb = k // (a + 1)
        width = k if nb > SUBLANES else SUBLANES
        cv = v1[a:a + 1, :] + v2[:width, :]
        ci = i1[a:a + 1, :] * PEER_N_KEYS + i2[:width, :]
        if nb < SUBLANES:
            cv = jnp.where(sub < nb, cv, -jnp.inf)
        vals.append(cv)
        idxs.append(ci)
    vals.append(v1[k // 2:, :] + v2[0:1, :])
    idxs.append(i1[k // 2:, :] * PEER_N_KEYS + i2[0:1, :])
    return jnp.concatenate(vals, axis=0), jnp.concatenate(idxs, axis=0)


def _peer_route_kernel(q_ref, keys_hi_ref, keys_lo_ref, e_ref, g_ref):
    k = PEER_TOPK
    for h in range(PEER_HEADS):
        sub = []
        for p in range(2):
            c0 = (h * 2 + p) * PEER_N_KEYS
            q_parts = _split_bf16(q_ref[:, c0:c0 + PEER_N_KEYS])
            s = _dot3_parts((keys_hi_ref[h, p], keys_lo_ref[h, p]), q_parts, _dot_nt)
            sub.append(_topk_rows(s, None, k))
        (v1, i1), (v2, i2) = sub
        cand, cidx = _product_candidates(v1, i1, v2, i2, k)
        best, eidx = _topk_rows(cand, cidx, k)
        ex = jnp.exp(best - jnp.max(best, axis=0, keepdims=True))
        gate = ex / jnp.sum(ex, axis=0, keepdims=True)
        e_ref[h * k:(h + 1) * k, :] = eidx.astype(jnp.int32)
        g_ref[h * k:(h + 1) * k, :] = gate


PEER_GATE_TN = 256


def _peer_gate_kernel(et_ref, gt_ref, o_ref, e_scr, g_scr):
    tg = et_ref.shape[1]
    e_scr[...] = et_ref[...].T
    g_scr[...] = gt_ref[...].T
    kid = lax.broadcasted_iota(jnp.int32, (PEER_N_KEYS, PEER_N_KEYS), 0)
    group = 2 * SUBLANES

    def body(n, carry):
        rows = pl.ds(pl.multiple_of(n * group, group), group)
        e16 = e_scr[rows, :]
        g16 = g_scr[rows, :]
        a16 = jnp.right_shift(e16, 7)
        b16 = jnp.bitwise_and(e16, PEER_N_KEYS - 1)
        toks = range(group)
        at = [jnp.where(a16[s:s + 1, :] == kid, g16[s:s + 1, :], 0.0).astype(BF16) for s in toks]
        bt = [jnp.where(b16[s:s + 1, :] == kid, 1.0, 0.0).astype(BF16) for s in toks]
        grids = [_dot_nt(at[s], bt[s]) for s in toks]
        o_ref[:, rows, :] = jnp.swapaxes(jnp.stack(grids, axis=0), 0, 1).astype(BF16)
        return carry

    lax.fori_loop(0, tg // group, body, 0, unroll=8)


def _peer_route(qp, sub_keys, *, tn):
    n = qp.shape[0]
    rows = PEER_HEADS * PEER_TOPK
    keys_hi, keys_lo = _split_bf16(sub_keys.astype(F32))
    keys_spec = pl.BlockSpec(sub_keys.shape, lambda i: (0, 0, 0, 0))
    e_t, g_t = pl.pallas_call(
        _peer_route_kernel,
        out_shape=(jax.ShapeDtypeStruct((rows, n), jnp.int32), jax.ShapeDtypeStruct((rows, n), F32)),
        grid=(n // tn,),
        in_specs=[pl.BlockSpec((tn, qp.shape[1]), lambda i: (i, 0)), keys_spec, keys_spec],
        out_specs=(pl.BlockSpec((rows, tn), lambda i: (0, i)), pl.BlockSpec((rows, tn), lambda i: (0, i))),
        compiler_params=_cparams(("parallel",)),
        name="peer_route",
    )(qp, keys_hi, keys_lo)
    tg = min(PEER_GATE_TN, n)
    return pl.pallas_call(
        _peer_gate_kernel,
        out_shape=jax.ShapeDtypeStruct((PEER_N_KEYS, n, PEER_N_KEYS), BF16),
        grid=(n // tg,),
        in_specs=[pl.BlockSpec((rows, tg), lambda i: (0, i)), pl.BlockSpec((rows, tg), lambda i: (0, i))],
        out_specs=pl.BlockSpec((PEER_N_KEYS, tg, PEER_N_KEYS), lambda i: (0, i, 0)),
        scratch_shapes=[pltpu.VMEM((tg, rows), jnp.int32), pltpu.VMEM((tg, rows), F32)],
        compiler_params=_cparams(("parallel",)),
        name="peer_gates",
    )(e_t, g_t)


def _gelu_tanh(x):
    return 0.5 * x * (1.0 + jnp.tanh(math.sqrt(2.0 / math.pi) * (x + 0.044715 * (x * x * x))))


PEER_BLOCKS_PER_STEP = 4
PEER_TN = 1024
PEER_VMEM_LIMIT = 60 * 1024 * 1024


def _peer_expert_kernel(x_ref, gn_ref, gate_ref, u_ref, v_ref, gf_ref, o_ref, h_scr, acc_scr, coef_scr):
    s = pl.program_id(1)
    last = pl.num_programs(1) - 1

    @pl.when(s == 0)
    def _():
        h_scr[...] = _rms(x_ref[...], gn_ref[...]).astype(BF16)
        acc_scr[...] = jnp.zeros(acc_scr.shape, F32)
        coef_scr[...] = jnp.zeros(coef_scr.shape, BF16)

    acc_scr[...] += _dot(coef_scr[(s + 1) % 2], v_ref[...].astype(BF16))

    gate = jnp.concatenate([gate_ref[q].astype(F32) for q in range(gate_ref.shape[0])], axis=1)
    hid = _gelu_tanh(_dot_nt(h_scr[...], u_ref[...].astype(BF16)))
    coef_scr[s % 2] = (gate * hid).astype(BF16)

    @pl.when(s == last)
    def _():
        o_ref[...] = _rms(x_ref[...] + acc_scr[...], gf_ref[...])


def _peer_experts(x, norm_g, gates, u_tab, v_tab, final_g, *, tn):
    n, d = x.shape
    eb = PEER_N_KEYS
    assert u_tab.shape[0] == eb * eb
    nblk = PEER_BLOCKS_PER_STEP
    ns = eb // nblk

    def cur(s):
        return jnp.minimum(s, ns - 1)

    def prev(s):
        return jnp.maximum(s - 1, 0)

    return pl.pallas_call(
        _peer_expert_kernel,
        out_shape=jax.ShapeDtypeStruct((n, d), F32),
        grid=(n // tn, ns + 1),
        in_specs=[
            pl.BlockSpec((tn, d), lambda i, s: (i, 0), pipeline_mode=pl.Buffered(1)),
            pl.BlockSpec((1, d), lambda i, s: (0, 0)),
            pl.BlockSpec((nblk, tn, eb), lambda i, s: (cur(s), i, 0)),
            pl.BlockSpec((nblk * eb, d), lambda i, s: (cur(s), 0)),
            pl.BlockSpec((nblk * eb, d), lambda i, s: (prev(s), 0)),
            pl.BlockSpec((1, d), lambda i, s: (0, 0)),
        ],
        out_specs=pl.BlockSpec((tn, d), lambda i, s: (i, 0), pipeline_mode=pl.Buffered(1)),
        scratch_shapes=[pltpu.VMEM((tn, d), BF16), pltpu.VMEM((tn, d), F32),
                        pltpu.VMEM((2, tn, nblk * eb), BF16)],
        compiler_params=_cparams(("parallel", "arbitrary"), vmem=PEER_VMEM_LIMIT),
        name="peer_experts",
    )(x, norm_g.reshape(1, d), gates, u_tab, v_tab, final_g.reshape(1, d))


def _layer(x, wts, layer, mem_k, mem_v, conv_prev, gdn_state, k_past, v_past, *, tm):
    bsz, t, d = x.shape
    n = bsz * t
    x2d = x.reshape(n, d)
    main, k_new, v_new, tail = _in_proj(x2d, wts["norm_mix_g"], wts["w_in"], tm=min(IN_TM, n))
    conv_dim = 3 * GDN_WIDTH
    conv_prev8 = jnp.concatenate(
        [jnp.zeros((bsz, SUBLANES - (GDN_CONV - 1), conv_dim), F32), conv_prev.astype(F32)], axis=1)
    o_a, s_new = _gdn(main, tail, conv_prev8, gdn_state.astype(F32), wts["gdn_conv_w"], wts["gdn_a_log"],
                      wts["gdn_dt_bias"], wts["gdn_norm_g"], bsz=bsz, t=t)
    conv_new = main.reshape(bsz, t, -1)[:, t - (GDN_CONV - 1):, :conv_dim]
    lam_init = 0.8 - 0.6 * math.exp(-0.3 * layer)
    lams = tuple(wts[k].reshape(1, -1) for k in ("diff_lambda_q1", "diff_lambda_k1", "diff_lambda_q2", "diff_lambda_k2"))
    if k_past is None:
        o_b = _diff_prompt(main, k_new, v_new, lams, wts["diff_subln_g"], bsz=bsz, t=t, lam_init=lam_init)
    else:
        o_b = _diff_sample(main, k_new, v_new, k_past, v_past, lams, wts["diff_subln_g"], bsz=bsz, t=t,
                           lam_init=lam_init)
    k_rows = k_new.reshape(bsz, t, DIFF_HEADS, HEAD_DIM)
    v_rows = v_new.reshape(bsz, t, DIFF_HEADS, HEAD_DIM)
    merged = _merge(o_a, o_b, wts["w_branch_a"], wts["w_branch_b"], tail, tm=tm)
    x1 = _matmul(merged, wts["w_out"], residual=x2d, tm=tm, tn=MM_TN, name="out_proj")
    qm = _matmul(x1, wts["w_mq"], norm_g=wts["norm_cross_g"], tm=tm, tn=MM_TN, out_dtype=BF16, name="mem_q")
    mt = mem_k.shape[1]
    oc = _cross_attend(qm, mem_k.reshape(bsz, mt, d), mem_v.reshape(bsz, mt, d), bsz=bsz, t=t, tm=min(tm, t))
    x2 = _matmul(oc, wts["w_mo"], residual=x1, tm=tm, tn=MM_TN, name="mem_o")
    qp = _matmul(x2, wts["peer_w_q"], norm_g=wts["norm_ffn_g"], tm=tm, tn=MM_TN, name="peer_q")
    tn = min(PEER_TN, n)
    gates = _peer_route(qp, wts["peer_sub_keys"], tn=LANES)
    y = _peer_experts(x2, wts["norm_ffn_g"], gates, wts["peer_u"], wts["peer_v"], wts["final_norm_g"], tn=tn)
    return y.reshape(bsz, t, d), k_rows, v_rows, s_new, conv_new


def kernel(x_prompt, x_sample, cache_diff_k, cache_diff_v, state_gdn, state_conv, cache_mem_k, cache_mem_v,
           mem_prompt, norm_mix_g, w_in, gdn_conv_w, gdn_a_log, gdn_dt_bias, gdn_norm_g,
           diff_lambda_q1, diff_lambda_k1, diff_lambda_q2, diff_lambda_k2, diff_subln_g,
           w_branch_a, w_branch_b, w_out, norm_cross_g, norm_mem_g, w_mq, w_mk, w_mv, w_mo,
           norm_ffn_g, peer_w_q, peer_sub_keys, peer_u, peer_v, final_norm_g):
    depth = w_in.shape[0]
    assert depth == 1, "final norm is fused into the last layer's PEER kernel"
    l = 0
    wts = {
        "norm_mix_g": norm_mix_g[l], "w_in": _in_proj_weights(w_in[l]), "gdn_conv_w": gdn_conv_w[l],
        "gdn_a_log": gdn_a_log[l], "gdn_dt_bias": gdn_dt_bias[l], "gdn_norm_g": gdn_norm_g[l],
        "diff_lambda_q1": diff_lambda_q1[l], "diff_lambda_k1": diff_lambda_k1[l],
        "diff_lambda_q2": diff_lambda_q2[l], "diff_lambda_k2": diff_lambda_k2[l],
        "diff_subln_g": diff_subln_g[l], "w_branch_a": w_branch_a[l].astype(BF16),
        "w_branch_b": w_branch_b[l].astype(BF16), "w_out": w_out[l].astype(BF16),
        "norm_cross_g": norm_cross_g[l], "w_mq": w_mq[l].astype(BF16), "w_mo": w_mo[l].astype(BF16),
        "norm_ffn_g": norm_ffn_g[l], "peer_w_q": peer_w_q[l].astype(BF16), "peer_sub_keys": peer_sub_keys[l],
        "peer_u": peer_u[l], "peer_v": peer_v[l], "final_norm_g": final_norm_g,
    }
    bp, tp, d = x_prompt.shape
    bs, ts, _ = x_sample.shape
    mem2d = mem_prompt.reshape(-1, d)
    tmm = min(512, mem2d.shape[0])
    mem_k = _matmul(mem2d, w_mk[l], norm_g=norm_mem_g[l], tm=tmm, tn=512, name="mem_k")
    mem_v = _matmul(mem2d, w_mv[l], norm_g=norm_mem_g[l], tm=tmm, tn=512, name="mem_v")
    mshape = mem_prompt.shape[:2] + (MEM_HEADS, d // MEM_HEADS)
    mem_k = mem_k.reshape(mshape)
    mem_v = mem_v.reshape(mshape)
    conv0 = jnp.zeros((bp, GDN_CONV - 1, 3 * GDN_WIDTH), F32)
    s0 = jnp.zeros((bp, GDN_HEADS, HEAD_DIM, HEAD_DIM), F32)
    yp, pk, pv, ps, pc = _layer(x_prompt, wts, l, mem_k, mem_v, conv0, s0, None, None, tm=min(1024, bp * tp))
    ys, sk, sv, ss, sc = _layer(x_sample, wts, l, cache_mem_k[l], cache_mem_v[l], state_conv[l], state_gdn[l],
                                cache_diff_k[l], cache_diff_v[l], tm=min(256, bs * ts))
    return (yp, ys, pk[None], pv[None], ps[None], pc[None], mem_k[None], mem_v[None],
            sk[None], sv[None], ss[None], sc[None])
```

```python
import functools
import math

import jax
import jax.numpy as jnp
from jax import lax
from jax.experimental import pallas as pl
from jax.experimental.pallas import tpu as pltpu

F32 = jnp.float32
BF16 = jnp.bfloat16
HIGHEST = lax.Precision.HIGHEST

RMS_EPS = 1e-6
CHUNK = 64
CHUNK_SHIFT = 6
GDN_HEADS = 8
HEAD_DIM = 128
GDN_WIDTH = GDN_HEADS * HEAD_DIM
GDN_CONV = 4
DIFF_HEADS = 8
DIFF_HEAD_DIM = 64
MEM_HEADS = 4
PEER_HEADS = 8
PEER_N_KEYS = 128
PEER_TOPK = 16

LANES = 128
SUBLANES = 8
V7X_VMEM_LIMIT = 52 * 1024 * 1024

PROMPT_TM = 1024
SAMPLE_TM = 256
MM_TN = 1024
DIFF_TQ = 1024
DIFF_SUB = 256
GDN_CHUNKS_PER_STEP = 4


def _cparams(sem, vmem=V7X_VMEM_LIMIT):
    return pltpu.CompilerParams(dimension_semantics=sem, vmem_limit_bytes=vmem)


def _dot(a, b, precision=None):
    return jnp.dot(a, b, preferred_element_type=F32, precision=precision)


def _dot_nt(a, b, precision=None):
    return lax.dot_general(a, b, (((1,), (1,)), ((), ())), preferred_element_type=F32, precision=precision)


def _split_bf16(x):
    hi = x.astype(BF16)
    return hi, (x - hi.astype(F32)).astype(BF16)


def _cat_parts(p, q):
    return jnp.concatenate([p[0], q[0]], axis=0), jnp.concatenate([p[1], q[1]], axis=0)


def _dot3_parts(a_parts, b_parts, dot):
    a_hi, a_lo = a_parts
    b_hi, b_lo = b_parts
    m = a_hi.shape[0]
    both = dot(jnp.concatenate([a_hi, a_lo], axis=0), b_hi)
    return both[:m] + both[m:] + dot(a_hi, b_lo)


def _dot3(a, b):
    return _dot3_parts(_split_bf16(a), _split_bf16(b), _dot)


def _rms(xf, g):
    return xf * lax.rsqrt(jnp.mean(xf * xf, axis=-1, keepdims=True) + RMS_EPS) * g


def _mm_kernel(*refs, has_norm, has_res):
    it = iter(refs)
    x_ref = next(it)
    g_ref = next(it) if has_norm else None
    w_ref = next(it)
    r_ref = next(it) if has_res else None
    o_ref = next(it)
    h_scr = next(it)

    @pl.when(pl.program_id(1) == 0)
    def _():
        xf = x_ref[...].astype(F32)
        if has_norm:
            xf = _rms(xf, g_ref[...])
        h_scr[...] = xf.astype(BF16)

    acc = _dot(h_scr[...], w_ref[...].astype(BF16))
    if has_res:
        acc = acc + r_ref[...]
    o_ref[...] = acc.astype(o_ref.dtype)


def _matmul(x, w, *, norm_g=None, residual=None, tm, tn, out_dtype=F32, name="matmul"):
    m, k = x.shape
    n = w.shape[1]
    assert m % tm == 0 and n % tn == 0, (m, n, tm, tn)
    in_specs = [pl.BlockSpec((tm, k), lambda i, j: (i, 0))]
    args = [x]
    if norm_g is not None:
        in_specs.append(pl.BlockSpec((1, k), lambda i, j: (0, 0)))
        args.append(norm_g.reshape(1, k).astype(F32))
    in_specs.append(pl.BlockSpec((k, tn), lambda i, j: (0, j)))
    args.append(w)
    if residual is not None:
        in_specs.append(pl.BlockSpec((tm, tn), lambda i, j: (i, j)))
        args.append(residual)
    return pl.pallas_call(
        functools.partial(_mm_kernel, has_norm=norm_g is not None, has_res=residual is not None),
        out_shape=jax.ShapeDtypeStruct((m, n), out_dtype),
        grid=(m // tm, n // tn),
        in_specs=in_specs,
        out_specs=pl.BlockSpec((tm, tn), lambda i, j: (i, j)),
        scratch_shapes=[pltpu.VMEM((tm, k), BF16)],
        compiler_params=_cparams(("parallel", "arbitrary")),
        name=name,
    )(*args)


IN_TN = 512
IN_TM = 1024
IN_W1_TILES = 8
IN_W2_TILES = 14
IN_MAIN_TILES = 10
IN_K_TILE0 = 10
IN_V_TILE0 = 12
IN_TAIL_TILE0 = 14
IN_TILES = IN_W1_TILES + IN_W2_TILES + 1
MAIN_COLS = IN_MAIN_TILES * IN_TN
TAIL_COLS = (IN_TILES - IN_TAIL_TILE0) * IN_TN
COL_DQ_MAIN = 4096
COL_GB_TAIL = 2048
COL_BA_TAIL = 4096


def _in_proj_kernel(x_ref, g_ref, w1_ref, w2_ref, w3_ref, main_ref, k_ref, v_ref, tail_ref, h_scr):
    j = pl.program_id(1)

    @pl.when(j == 0)
    def _():
        h_scr[...] = _rms(x_ref[...], g_ref[...]).astype(BF16)

    routes = ((0, IN_W1_TILES, w1_ref, main_ref),
              (IN_W1_TILES, IN_K_TILE0, w2_ref, main_ref),
              (IN_K_TILE0, IN_V_TILE0, w2_ref, k_ref),
              (IN_V_TILE0, IN_TAIL_TILE0, w2_ref, v_ref),
              (IN_TAIL_TILE0, IN_TILES - 1, w2_ref, tail_ref),
              (IN_TILES - 1, IN_TILES, w3_ref, tail_ref))
    for lo, hi, w_ref, dst in routes:
        @pl.when(jnp.logical_and(j >= lo, j < hi))
        def _(w_ref=w_ref, dst=dst):
            dst[...] = _dot(h_scr[...], w_ref[...])


def _in_proj_weights(w_in):
    c_z = 4 * GDN_WIDTH
    c_ba = c_z + 2 * GDN_HEADS
    assert c_z == IN_W1_TILES * IN_TN and w_in.shape[1] - c_ba == IN_W2_TILES * IN_TN
    w_bf16 = w_in.astype(BF16)
    return w_bf16, w_bf16[:, c_ba:]


def _in_proj(x, norm_g, w_pieces, *, tm):
    m, d = x.shape
    w_all, w2 = w_pieces
    w1 = w3 = w_all
    tn = IN_TN
    out = lambda cols: jax.ShapeDtypeStruct((m, cols), F32)
    return pl.pallas_call(
        _in_proj_kernel,
        out_shape=(out(MAIN_COLS), out(2 * IN_TN), out(2 * IN_TN), out(TAIL_COLS)),
        grid=(m // tm, IN_TILES),
        in_specs=[
            pl.BlockSpec((tm, d), lambda i, j: (i, 0)),
            pl.BlockSpec((1, d), lambda i, j: (0, 0)),
            pl.BlockSpec((d, tn), lambda i, j: (0, jnp.minimum(j, IN_W1_TILES - 1))),
            pl.BlockSpec((d, tn), lambda i, j: (0, jnp.clip(j - IN_W1_TILES, 0, IN_W2_TILES - 1))),
            pl.BlockSpec((d, tn), lambda i, j: (0, IN_W1_TILES)),
        ],
        out_specs=(
            pl.BlockSpec((tm, tn), lambda i, j: (i, jnp.minimum(j, IN_MAIN_TILES - 1))),
            pl.BlockSpec((tm, tn), lambda i, j: (i, jnp.clip(j - IN_K_TILE0, 0, 1))),
            pl.BlockSpec((tm, tn), lambda i, j: (i, jnp.clip(j - IN_V_TILE0, 0, 1))),
            pl.BlockSpec((tm, tn), lambda i, j: (i, jnp.clip(j - IN_TAIL_TILE0, 0, IN_TILES - IN_TAIL_TILE0 - 1))),
        ),
        scratch_shapes=[pltpu.VMEM((tm, d), BF16)],
        compiler_params=_cparams(("parallel", "arbitrary")),
        name="in_proj",
    )(x, norm_g.reshape(1, d).astype(F32), w1, w2, w3)


def _gdn_kernel(qkv_ref, z_ref, ba_ref, prev_ref, s0_ref, convw_ref, alog_ref, dtb_ref, ng_ref,
                o_ref, sfin_ref, s_scr, xbuf, *, L, nch):
    step = pl.program_id(1)
    rows = nch * L

    @pl.when(step == 0)
    def _():
        s_scr[...] = s0_ref[0]
        xbuf[0:SUBLANES, :] = prev_ref[0]

    xbuf[SUBLANES:SUBLANES + rows, :] = qkv_ref[...]
    base = SUBLANES - (GDN_CONV - 1)
    y = xbuf[base:base + rows, :] * convw_ref[0:1, :]
    for i in range(1, GDN_CONV):
        y = y + xbuf[base + i:base + i + rows, :] * convw_ref[i:i + 1, :]
    y = y * jax.nn.sigmoid(y)
    carry = xbuf[rows:rows + SUBLANES, :]
    xbuf[0:SUBLANES, :] = carry

    row = lax.broadcasted_iota(jnp.int32, (L, L), 0)
    col = lax.broadcasted_iota(jnp.int32, (L, L), 1)
    lower = row >= col
    strict = row > col
    eye = (row == col).astype(F32)
    log_l = int(math.log2(L))
    n_double = log_l - 1
    assert 2 ** log_l == L

    ba = ba_ref[...]
    beta_all = jax.nn.sigmoid(ba)
    a_in = ba + dtb_ref[...]
    softplus = jnp.maximum(a_in, 0.0) + jnp.log(1.0 + jnp.exp(-jnp.abs(a_in)))
    g_all = -jnp.exp(alog_ref[...]) * softplus
    rr = lax.broadcasted_iota(jnp.int32, (rows, rows), 0)
    cc = lax.broadcasted_iota(jnp.int32, (rows, rows), 1)
    same_chunk_lower = jnp.logical_and(rr >= cc, jnp.right_shift(rr, log_l) == jnp.right_shift(cc, log_l))
    g_cum = _dot(same_chunk_lower.astype(F32), g_all, HIGHEST)
    g_cum_t = g_cum.T

    heads = range(GDN_HEADS)
    chains = [(c, h) for c in range(nch) for h in heads]
    ids = range(len(chains))

    def tok(c):
        return slice(c * L, (c + 1) * L)

    def head_cols(part, c, h):
        return y[tok(c), part * GDN_WIDTH + h * HEAD_DIM:part * GDN_WIDTH + (h + 1) * HEAD_DIM]

    q = [head_cols(0, c, h) for c, h in chains]
    k = [head_cols(1, c, h) for c, h in chains]
    v = [head_cols(2, c, h) for c, h in chains]
    q = [a * lax.rsqrt(jnp.sum(a * a, axis=-1, keepdims=True) + 1e-6) * (HEAD_DIM ** -0.5) for a in q]
    k = [a * lax.rsqrt(jnp.sum(a * a, axis=-1, keepdims=True) + 1e-6) for a in k]
    beta = [beta_all[tok(c), h:h + 1] for c, h in chains]
    g_col = [g_cum[tok(c), GDN_HEADS + h:GDN_HEADS + h + 1] for c, h in chains]
    g_row = [g_cum_t[GDN_HEADS + h:GDN_HEADS + h + 1, tok(c)] for c, h in chains]
    decay = [jnp.where(lower, jnp.exp(jnp.where(lower, g_col[n] - g_row[n], 0.0)), 0.0) for n in ids]
    g_last = [g_col[n][L - 1:L, :] for n in ids]
    e_col = [jnp.exp(g_col[n]) for n in ids]
    qk_kk = [_dot_nt(jnp.concatenate([q[n], k[n]], axis=0).astype(BF16), k[n].astype(BF16)) for n in ids]
    x = [-jnp.where(strict, beta[n] * qk_kk[n][L:] * decay[n], 0.0) for n in ids]
    t_inv = [eye + x[n] for n in ids]
    x_parts = [_split_bf16(x[n]) for n in ids]
    xp = [_dot3_parts(x_parts[n], x_parts[n], _dot) for n in ids]
    for _ in range(n_double - 1):
        xp_parts = [_split_bf16(xp[n]) for n in ids]
        both = [_dot3_parts(_cat_parts(_split_bf16(t_inv[n]), xp_parts[n]), xp_parts[n], _dot) for n in ids]
        t_inv = [t_inv[n] + both[n][:L] for n in ids]
        xp = [both[n][L:] for n in ids]
    t_inv = [t_inv[n] + _dot3(t_inv[n], xp[n]) for n in ids]
    sol = [_dot3(t_inv[n], jnp.concatenate([v[n] * beta[n], k[n] * (beta[n] * e_col[n])], axis=-1)) for n in ids]
    qk = [jnp.where(lower, qk_kk[n][:L] * decay[n], 0.0) for n in ids]
    k_dec_t = [(k[n] * jnp.exp(g_last[n] - g_col[n])).T for n in ids]
    left_s = [jnp.concatenate([sol[n][:, HEAD_DIM:], q[n] * e_col[n]], axis=0).astype(BF16) for n in ids]
    left_v = [jnp.concatenate([qk[n], k_dec_t[n]], axis=0).astype(BF16) for n in ids]

    s = [s_scr[h] for h in heads]
    for c in range(nch):
        n0 = c * GDN_HEADS
        ws_qs = [_dot(left_s[n0 + h], s[h].astype(BF16)) for h in heads]
        v_new = [sol[n0 + h][:, :HEAD_DIM] - ws_qs[h][:L] for h in heads]
        tail = [_dot(left_v[n0 + h], v_new[h].astype(BF16)) for h in heads]
        s = [s[h] * jnp.exp(g_last[n0 + h]) + tail[h][L:] for h in heads]
        for h in heads:
            zf = z_ref[tok(c), h * HEAD_DIM:(h + 1) * HEAD_DIM]
            o = _rms(ws_qs[h][L:] + tail[h][:L], ng_ref[...]) * (zf * jax.nn.sigmoid(zf))
            o_ref[tok(c), h * HEAD_DIM:(h + 1) * HEAD_DIM] = o.astype(o_ref.dtype)
    for h in heads:
        s_scr[h] = s[h]

    @pl.when(step == pl.num_programs(1) - 1)
    def _():
        sfin_ref[0] = s_scr[...]


def _decay_lanes(p):
    return jnp.zeros((1, LANES), F32).at[0, GDN_HEADS:2 * GDN_HEADS].set(p.astype(F32))


def _gdn(main, tail, conv_prev8, s0, conv_w, a_log, dt_bias, norm_g, *, bsz, t):
    L = min(CHUNK, t)
    nch = GDN_CHUNKS_PER_STEP if (t // L) % GDN_CHUNKS_PER_STEP == 0 else 1
    nc = t // (L * nch)
    rows = L * nch
    conv_dim = 3 * GDN_WIDTH
    kern = functools.partial(_gdn_kernel, L=L, nch=nch)
    o, s_fin = pl.pallas_call(
        kern,
        out_shape=(jax.ShapeDtypeStruct((bsz * t, GDN_WIDTH), BF16),
                   jax.ShapeDtypeStruct((bsz, GDN_HEADS, HEAD_DIM, HEAD_DIM), F32)),
        grid=(bsz, nc),
        in_specs=[
            pl.BlockSpec((rows, conv_dim), lambda b, c: (b * nc + c, 0)),
            pl.BlockSpec((rows, GDN_WIDTH), lambda b, c: (b * nc + c, conv_dim // GDN_WIDTH)),
            pl.BlockSpec((rows, LANES), lambda b, c: (b * nc + c, COL_BA_TAIL // LANES)),
            pl.BlockSpec((1, SUBLANES, conv_dim), lambda b, c: (b, 0, 0)),
            pl.BlockSpec((1, GDN_HEADS, HEAD_DIM, HEAD_DIM), lambda b, c: (b, 0, 0, 0)),
            pl.BlockSpec((GDN_CONV, conv_dim), lambda b, c: (0, 0)),
            pl.BlockSpec((1, LANES), lambda b, c: (0, 0)),
            pl.BlockSpec((1, LANES), lambda b, c: (0, 0)),
            pl.BlockSpec((1, HEAD_DIM), lambda b, c: (0, 0)),
        ],
        out_specs=(pl.BlockSpec((rows, GDN_WIDTH), lambda b, c: (b * nc + c, 0)),
                   pl.BlockSpec((1, GDN_HEADS, HEAD_DIM, HEAD_DIM), lambda b, c: (b, 0, 0, 0))),
        scratch_shapes=[pltpu.VMEM((GDN_HEADS, HEAD_DIM, HEAD_DIM), F32),
                        pltpu.VMEM((SUBLANES + rows + SUBLANES, conv_dim), F32)],
        compiler_params=_cparams(("parallel", "arbitrary")),
        name="gdn",
    )(main, main, tail, conv_prev8, s0, conv_w, _decay_lanes(a_log), _decay_lanes(dt_bias),
      norm_g.reshape(1, -1))
    return o, s_fin


def _lambda_value(lam_refs, lam_init):
    lq1, lk1, lq2, lk2 = (r[...] for r in lam_refs)
    return (jnp.exp(jnp.sum(lq1 * lk1, axis=-1, keepdims=True))
            - jnp.exp(jnp.sum(lq2 * lk2, axis=-1, keepdims=True)) + lam_init)


def _softmax_update(m_scr, l_scr, acc_scr, scores, values):
    idx = range(len(scores))
    m_old = [m_scr[n] for n in idx]
    m_new = [jnp.maximum(m_old[n], jnp.max(scores[n], axis=-1, keepdims=True)) for n in idx]
    p = [jnp.exp(scores[n] - m_new[n]) for n in idx]
    alpha = [jnp.exp(m_old[n] - m_new[n]) for n in idx]
    pv = [_dot(p[n].astype(BF16), values[n]) for n in idx]
    for n in idx:
        l_scr[n] = alpha[n] * l_scr[n] + jnp.sum(p[n], axis=-1, keepdims=True)
        acc_scr[n] = alpha[n] * acc_scr[n] + pv[n]
        m_scr[n] = m_new[n]


def _diff_finalize(o1, o2, lam_refs, g_ref, lam_init):
    lam = _lambda_value(lam_refs, lam_init)
    return _rms(o1 - lam * o2, g_ref[...]) * (1.0 - lam_init)


def _diff_init(m_scr, l_scr, acc_scr):
    m_scr[...] = jnp.full(m_scr.shape, -jnp.inf, F32)
    l_scr[...] = jnp.zeros(l_scr.shape, F32)
    acc_scr[...] = jnp.zeros(acc_scr.shape, F32)


def _near_bias(slope, r, q_pos, k_pos):
    visible = jnp.right_shift(k_pos, CHUNK_SHIFT) <= jnp.right_shift(q_pos, CHUNK_SHIFT)
    return jnp.where(visible, slope * (r - jnp.abs(q_pos - k_pos)).astype(F32), -jnp.inf)


def _split_distance(slope, d):
    return (-(slope * CHUNK) * jnp.right_shift(d, CHUNK_SHIFT).astype(F32),
            -slope * jnp.bitwise_and(d, CHUNK - 1).astype(F32))


def _diff_prompt_kernel(i_tab, j_tab, slopes_ref, q_ref, k_ref, v_ref, lq1, lk1, lq2, lk2, g_ref, o_ref,
                        m_scr, l_scr, acc_scr, lhs_scr, s_even, s_odd, *, tq, lam_init):
    h = pl.program_id(1)
    s_id = pl.program_id(2)
    n_pairs = pl.num_programs(2) - 1
    score_on = s_id < n_pairs
    consume_on = s_id >= 1
    i = i_tab[s_id]
    j = j_tab[s_id]
    ci = i_tab[jnp.maximum(s_id - 1, 0)]
    cj = j_tab[jnp.maximum(s_id - 1, 0)]
    slope = slopes_ref[h]
    half = DIFF_HEAD_DIM
    lane = lax.broadcasted_iota(jnp.int32, (tq, LANES), 1)

    @pl.when(jnp.logical_and(consume_on, cj == 0))
    def _():
        _diff_init(m_scr, l_scr, acc_scr)

    @pl.when(jnp.logical_and(score_on, j == 0))
    def _():
        qs = q_ref[...] * (DIFF_HEAD_DIM ** -0.5)
        lhs_scr[0] = jnp.where(lane < half, qs, 1.0).astype(BF16)
        lhs_scr[1] = jnp.where(lane >= half, qs, 1.0).astype(BF16)

    sub = min(DIFF_SUB, tq)
    units = [(n, qb) for qb in range(tq // sub) for n in range(2)]

    def score_tile(near, dst):
        kb = k_ref[...]
        if near:
            aug0 = aug1 = jnp.zeros_like(kb)
        else:
            d = (i - j) * tq - lax.broadcasted_iota(jnp.int32, (tq, LANES), 0)
            t_hi, t_lo = _split_distance(slope, d)
            aug0 = jnp.where(lane == half, t_hi, jnp.where(lane == half + 1, t_lo, 0.0))
            aug1 = jnp.where(lane == 0, t_hi, jnp.where(lane == 1, t_lo, 0.0))
        keys = [jnp.where(lane < half, kb, aug0).astype(BF16), jnp.where(lane >= half, kb, aug1).astype(BF16)]
        bias = None
        for n, qb in units:
            cols = slice(qb * sub, (qb + 1) * sub)
            if near:
                live = (qb + 1) * sub
                if n == 0:
                    k_pos = lax.broadcasted_iota(jnp.int32, (live, sub), 0)
                    q_pos = qb * sub + lax.broadcasted_iota(jnp.int32, (live, sub), 1)
                    bias = _near_bias(slope, q_pos, q_pos, k_pos)
                dst[n, :live, cols] = _dot_nt(keys[n][:live], lhs_scr[n, cols, :]) + bias
            else:
                dst[n, :, cols] = _dot_nt(keys[n], lhs_scr[n, cols, :])

    def consume_tile(near, src):
        vt = v_ref[...].T.astype(BF16)
        idx = range(len(units))
        cols = [slice(qb * sub, (qb + 1) * sub) for _, qb in units]
        live = [(qb + 1) * sub if near else tq for _, qb in units]
        s = [src[n, :live[u], cols[u]] for u, (n, _) in enumerate(units)]
        m_old = [m_scr[n, :, cols[u]] for u, (n, _) in enumerate(units)]
        m_new = [jnp.maximum(m_old[u], jnp.max(s[u], axis=0, keepdims=True)) for u in idx]
        p = [jnp.exp(s[u] - m_new[u]) for u in idx]
        alpha = [jnp.exp(m_old[u] - m_new[u]) for u in idx]
        pv = [_dot(vt[:, :live[u]], p[u].astype(BF16)) for u in idx]
        for u, (n, _) in enumerate(units):
            l_scr[n, :, cols[u]] = alpha[u] * l_scr[n, :, cols[u]] + jnp.sum(p[u], axis=0, keepdims=True)
            acc_scr[n, :, cols[u]] = alpha[u] * acc_scr[n, :, cols[u]] + pv[u]
            m_scr[n, :, cols[u]] = m_new[u]

    def stage(cond, score_near=None, consume_near=None):
        for parity, (dst, src) in enumerate(((s_even, s_odd), (s_odd, s_even))):
            @pl.when(jnp.logical_and(cond, s_id % 2 == parity))
            def _(dst=dst, src=src):
                if score_near is not None:
                    score_tile(score_near, dst)
                if consume_near is not None:
                    consume_tile(consume_near, src)

    both = jnp.logical_and(score_on, consume_on)
    stage(jnp.logical_and(both, jnp.logical_and(j < i, cj < ci)), score_near=False, consume_near=False)
    stage(jnp.logical_and(both, jnp.logical_and(j < i, cj == ci)), score_near=False, consume_near=True)
    stage(jnp.logical_and(both, j == i), score_near=True, consume_near=False)
    stage(jnp.logical_not(consume_on), score_near=True)
    stage(jnp.logical_not(score_on), consume_near=True)

    @pl.when(jnp.logical_and(consume_on, cj == ci))
    def _():
        o_ref[...] = _diff_finalize((acc_scr[0] / l_scr[0]).T, (acc_scr[1] / l_scr[1]).T,
                                    (lq1, lk1, lq2, lk2), g_ref, lam_init).astype(o_ref.dtype)


def _alibi_slopes():
    return jnp.asarray([2.0 ** (-8.0 * (i + 1) / DIFF_HEADS) for i in range(DIFF_HEADS)], F32)


def _diff_prompt(main, k_new, v_new, lams, subln_g, *, bsz, t, lam_init):
    tq = min(DIFF_TQ, t)
    nq = t // tq
    assert tq % CHUNK == 0 and t <= CHUNK * 255
    kern = functools.partial(_diff_prompt_kernel, tq=tq, lam_init=lam_init)
    qc, kc, vc = COL_DQ_MAIN // LANES, 0, 0
    pairs = [(i, j) for i in range(nq) for j in range(i + 1)]
    pairs.append(pairs[-1])
    i_tab = jnp.asarray([p[0] for p in pairs], jnp.int32)
    j_tab = jnp.asarray([p[1] for p in pairs], jnp.int32)
    zero = lambda b, h, s, it, jt: (0, 0)

    def consumed(tab, s):
        return tab[jnp.maximum(s - 1, 0)]

    grid_spec = pltpu.PrefetchScalarGridSpec(
        num_scalar_prefetch=2,
        grid=(bsz, DIFF_HEADS, len(pairs)),
        in_specs=[
            pl.BlockSpec(memory_space=pltpu.SMEM),
            pl.BlockSpec((tq, LANES), lambda b, h, s, it, jt: (b * nq + it[s], qc + h)),
            pl.BlockSpec((tq, LANES), lambda b, h, s, it, jt: (b * nq + jt[s], kc + h)),
            pl.BlockSpec((tq, LANES), lambda b, h, s, it, jt: (b * nq + consumed(jt, s), vc + h)),
        ] + [pl.BlockSpec((1, DIFF_HEAD_DIM), zero)] * 4 + [pl.BlockSpec((1, HEAD_DIM), zero)],
        out_specs=pl.BlockSpec((tq, LANES), lambda b, h, s, it, jt: (b * nq + consumed(it, s), h)),
        scratch_shapes=[pltpu.VMEM((2, 1, tq), F32), pltpu.VMEM((2, 1, tq), F32),
                        pltpu.VMEM((2, HEAD_DIM, tq), F32), pltpu.VMEM((2, tq, LANES), BF16),
                        pltpu.VMEM((2, tq, tq), F32), pltpu.VMEM((2, tq, tq), F32)],
    )
    return pl.pallas_call(
        kern,
        out_shape=jax.ShapeDtypeStruct((bsz * t, DIFF_HEADS * HEAD_DIM), BF16),
        grid_spec=grid_spec,
        compiler_params=_cparams(("parallel", "parallel", "arbitrary")),
        name="diff_attn_prompt",
    )(i_tab, j_tab, _alibi_slopes(), main, k_new, v_new, *lams, subln_g.reshape(1, -1))


def _diff_sample_kernel(slopes_ref, q_ref, kn_ref, vn_ref, kp_ref, vp_ref, lq1, lk1, lq2, lk2, g_ref, o_ref,
                        m_scr, l_scr, acc_scr, lhs_scr, *, t, tk, past, lam_init):
    j = pl.program_id(1)
    heads = range(DIFF_HEADS)

    def cols(h):
        return slice(h * HEAD_DIM, (h + 1) * HEAD_DIM)

    @pl.when(j == 0)
    def _():
        _diff_init(m_scr, l_scr, acc_scr)
        lane = lax.broadcasted_iota(jnp.int32, (t, LANES), 1)
        r = lax.broadcasted_iota(jnp.int32, (t, t), 0)
        c = lax.broadcasted_iota(jnp.int32, (t, t), 1)
        scores = []
        for h in heads:
            qs = q_ref[:, cols(h)] * (DIFF_HEAD_DIM ** -0.5)
            lhs_scr[h, 0:t, :] = jnp.where(lane < DIFF_HEAD_DIM, qs, 0.0).astype(BF16)
            lhs_scr[h, t:2 * t, :] = jnp.where(lane >= DIFF_HEAD_DIM, qs, 0.0).astype(BF16)
            bias = _near_bias(slopes_ref[h], r, past + r, past + c)
            scores.append(_dot_nt(lhs_scr[h], kn_ref[:, cols(h)].astype(BF16)) + jnp.concatenate([bias, bias], axis=0))
        _softmax_update(m_scr, l_scr, acc_scr, scores, [vn_ref[:, cols(h)].astype(BF16) for h in heads])

    d = (past - j * tk - lax.broadcasted_iota(jnp.int32, (1, tk), 1)).astype(F32)
    kt = jnp.swapaxes(kp_ref[...], 0, 1).astype(BF16)
    vt = jnp.swapaxes(vp_ref[...], 0, 1).astype(BF16)
    scores = [_dot_nt(lhs_scr[h], kt[h]) - slopes_ref[h] * d for h in heads]
    _softmax_update(m_scr, l_scr, acc_scr, scores, [vt[h] for h in heads])

    @pl.when(j == pl.num_programs(1) - 1)
    def _():
        for h in heads:
            o = acc_scr[h] / l_scr[h]
            o_ref[:, cols(h)] = _diff_finalize(o[:t], o[t:], (lq1, lk1, lq2, lk2), g_ref, lam_init).astype(o_ref.dtype)


def _diff_sample(main, k_new, v_new, k_past, v_past, lams, subln_g, *, bsz, t, lam_init):
    past = k_past.shape[1]
    tk = min(1024, past)
    nk = past // tk
    width = DIFF_HEADS * HEAD_DIM
    kern = functools.partial(_diff_sample_kernel, t=t, tk=tk, past=past, lam_init=lam_init)
    zero = lambda b, j: (0, 0)
    cache = pl.BlockSpec((None, tk, DIFF_HEADS, HEAD_DIM), lambda b, j: (b, j, 0, 0))
    return pl.pallas_call(
        kern,
        out_shape=jax.ShapeDtypeStruct((bsz * t, width), BF16),
        grid=(bsz, nk),
        in_specs=[
            pl.BlockSpec(memory_space=pltpu.SMEM),
            pl.BlockSpec((t, width), lambda b, j: (b, COL_DQ_MAIN // width)),
            pl.BlockSpec((t, width), lambda b, j: (b, 0)),
            pl.BlockSpec((t, width), lambda b, j: (b, 0)),
            cache, cache,
        ] + [pl.BlockSpec((1, DIFF_HEAD_DIM), zero)] * 4 + [pl.BlockSpec((1, HEAD_DIM), zero)],
        out_specs=pl.BlockSpec((t, width), lambda b, j: (b, 0)),
        scratch_shapes=[pltpu.VMEM((DIFF_HEADS, 2 * t, 1), F32), pltpu.VMEM((DIFF_HEADS, 2 * t, 1), F32),
                        pltpu.VMEM((DIFF_HEADS, 2 * t, HEAD_DIM), F32), pltpu.VMEM((DIFF_HEADS, 2 * t, LANES), BF16)],
        compiler_params=_cparams(("parallel", "arbitrary")),
        name="diff_attn_sample",
    )(_alibi_slopes(), main, k_new, v_new, k_past, v_past, *lams, subln_g.reshape(1, -1))


def _merge_kernel(oa_ref, ob_ref, wa_ref, wb_ref, ga_ref, gb_ref, o_ref):
    ya = _dot(oa_ref[...], wa_ref[...])
    yb = _dot(ob_ref[...], wb_ref[...])
    o_ref[...] = (jax.nn.sigmoid(ga_ref[...]) * ya + jax.nn.sigmoid(gb_ref[...]) * yb).astype(o_ref.dtype)


def _merge(o_a, o_b, wa, wb, tail, *, tm):
    m = o_a.shape[0]
    d = wa.shape[1]
    tn = MM_TN
    ga0, gb0 = 0, COL_GB_TAIL // tn
    return pl.pallas_call(
        _merge_kernel,
        out_shape=jax.ShapeDtypeStruct((m, d), BF16),
        grid=(m // tm, d // tn),
        in_specs=[
            pl.BlockSpec((tm, o_a.shape[1]), lambda i, j: (i, 0)),
            pl.BlockSpec((tm, o_b.shape[1]), lambda i, j: (i, 0)),
            pl.BlockSpec((wa.shape[0], tn), lambda i, j: (0, j)),
            pl.BlockSpec((wb.shape[0], tn), lambda i, j: (0, j)),
            pl.BlockSpec((tm, tn), lambda i, j: (i, ga0 + j)),
            pl.BlockSpec((tm, tn), lambda i, j: (i, gb0 + j)),
        ],
        out_specs=pl.BlockSpec((tm, tn), lambda i, j: (i, j)),
        compiler_params=_cparams(("parallel", "arbitrary")),
        name="merge",
    )(o_a, o_b, wa, wb, tail, tail)


def _cross_kernel(q_ref, k_ref, v_ref, o_ref):
    dh = q_ref.shape[-1] // MEM_HEADS
    for h in range(MEM_HEADS):
        sl = slice(h * dh, (h + 1) * dh)
        s = _dot_nt(q_ref[:, sl], k_ref[0, :, sl].astype(BF16)) * (dh ** -0.5)
        s = s - jnp.max(s, axis=-1, keepdims=True)
        p = jnp.exp(s)
        p = p / jnp.sum(p, axis=-1, keepdims=True)
        o_ref[:, sl] = _dot(p.astype(BF16), v_ref[0, :, sl].astype(BF16)).astype(o_ref.dtype)


def _cross_attend(q, mem_k, mem_v, *, bsz, t, tm):
    d = q.shape[1]
    nt = t // tm
    mt = mem_k.shape[1]
    return pl.pallas_call(
        _cross_kernel,
        out_shape=jax.ShapeDtypeStruct(q.shape, BF16),
        grid=(bsz, nt),
        in_specs=[
            pl.BlockSpec((tm, d), lambda b, i: (b * nt + i, 0)),
            pl.BlockSpec((1, mt, d), lambda b, i: (b, 0, 0)),
            pl.BlockSpec((1, mt, d), lambda b, i: (b, 0, 0)),
        ],
        out_specs=pl.BlockSpec((tm, d), lambda b, i: (b * nt + i, 0)),
        compiler_params=_cparams(("parallel", "arbitrary")),
        name="cross_attn",
    )(q, mem_k, mem_v)


def _topk_rows(work, cidx, k):
    n, tn = work.shape
    rid = lax.broadcasted_iota(jnp.int32, (n, tn), 0).astype(F32)
    kid = lax.broadcasted_iota(jnp.int32, (k, tn), 0)
    vals = jnp.zeros((k, tn), F32)
    idxs = jnp.zeros((k, tn), F32)
    for t in range(k):
        m = jnp.max(work, axis=0, keepdims=True)
        pos = jnp.min(jnp.where(work == m, rid, float(n)), axis=0, keepdims=True)
        hit = rid == pos
        if cidx is None:
            picked = pos
        else:
            picked = jnp.sum(jnp.where(hit, cidx, 0.0), axis=0, keepdims=True)
        vals = jnp.where(kid == t, m, vals)
        idxs = jnp.where(kid == t, picked, idxs)
        work = jnp.where(hit, -jnp.inf, work)
    return vals, idxs


def _product_candidates(v1, i1, v2, i2, k):
    assert k == 2 * SUBLANES
    sub = lax.broadcasted_iota(jnp.int32, (SUBLANES, v1.shape[1]), 0)
    vals, idxs = [], []
    for a in range(k // 2):
        nb = k // (a + 1)
        width = k if nb > SUBLANES else SUBLANES
        cv = v1[a:a + 1, :] + v2[:width, :]
        ci = i1[a:a + 1, :] * PEER_N_KEYS + i2[:width, :]
        if nb < SUBLANES:
            cv = jnp.where(sub < nb, cv, -jnp.inf)
        vals.append(cv)
        idxs.append(ci)
    vals.append(v1[k // 2:, :] + v2[0:1, :])
    idxs.append(i1[k // 2:, :] * PEER_N_KEYS + i2[0:1, :])
    return jnp.concatenate(vals, axis=0), jnp.concatenate(idxs, axis=0)


def _peer_route_kernel(q_ref, keys_hi_ref, keys_lo_ref, e_ref, g_ref):
    k = PEER_TOPK
    for h in range(PEER_HEADS):
        sub = []
        for p in range(2):
            c0 = (h * 2 + p) * PEER_N_KEYS
            q_parts = _split_bf16(q_ref[:, c0:c0 + PEER_N_KEYS])
            s = _dot3_parts((keys_hi_ref[h, p], keys_lo_ref[h, p]), q_parts, _dot_nt)
            sub.append(_topk_rows(s, None, k))
        (v1, i1), (v2, i2) = sub
        cand, cidx = _product_candidates(v1, i1, v2, i2, k)
        best, eidx = _topk_rows(cand, cidx, k)
        ex = jnp.exp(best - jnp.max(best, axis=0, keepdims=True))
        gate = ex / jnp.sum(ex, axis=0, keepdims=True)
        e_ref[h * k:(h + 1) * k, :] = eidx.astype(jnp.int32)
        g_ref[h * k:(h + 1) * k, :] = gate


PEER_GATE_TN = 256


def _peer_gate_kernel(et_ref, gt_ref, o_ref, e_scr, g_scr):
    tg = et_ref.shape[1]
    e_scr[...] = et_ref[...].T
    g_scr[...] = gt_ref[...].T
    kid = lax.broadcasted_iota(jnp.int32, (PEER_N_KEYS, PEER_N_KEYS), 0)
    group = 2 * SUBLANES

    def body(n, carry):
        rows = pl.ds(pl.multiple_of(n * group, group), group)
        e16 = e_scr[rows, :]
        g16 = g_scr[rows, :]
        a16 = jnp.right_shift(e16, 7)
        b16 = jnp.bitwise_and(e16, PEER_N_KEYS - 1)
        toks = range(group)
        at = [jnp.where(a16[s:s + 1, :] == kid, g16[s:s + 1, :], 0.0).astype(BF16) for s in toks]
        bt = [jnp.where(b16[s:s + 1, :] == kid, 1.0, 0.0).astype(BF16) for s in toks]
        grids = [_dot_nt(at[s], bt[s]) for s in toks]
        o_ref[:, rows, :] = jnp.swapaxes(jnp.stack(grids, axis=0), 0, 1).astype(BF16)
        return carry

    lax.fori_loop(0, tg // group, body, 0, unroll=8)


def _peer_route(qp, sub_keys, *, tn):
    n = qp.shape[0]
    rows = PEER_HEADS * PEER_TOPK
    keys_hi, keys_lo = _split_bf16(sub_keys.astype(F32))
    keys_spec = pl.BlockSpec(sub_keys.shape, lambda i: (0, 0, 0, 0))
    e_t, g_t = pl.pallas_call(
        _peer_route_kernel,
        out_shape=(jax.ShapeDtypeStruct((rows, n), jnp.int32), jax.ShapeDtypeStruct((rows, n), F32)),
        grid=(n // tn,),
        in_specs=[pl.BlockSpec((tn, qp.shape[1]), lambda i: (i, 0)), keys_spec, keys_spec],
        out_specs=(pl.BlockSpec((rows, tn), lambda i: (0, i)), pl.BlockSpec((rows, tn), lambda i: (0, i))),
        compiler_params=_cparams(("parallel",)),
        name="peer_route",
    )(qp, keys_hi, keys_lo)
    tg = min(PEER_GATE_TN, n)
    return pl.pallas_call(
        _peer_gate_kernel,
        out_shape=jax.ShapeDtypeStruct((PEER_N_KEYS, n, PEER_N_KEYS), BF16),
        grid=(n // tg,),
        in_specs=[pl.BlockSpec((rows, tg), lambda i: (0, i)), pl.BlockSpec((rows, tg), lambda i: (0, i))],
        out_specs=pl.BlockSpec((PEER_N_KEYS, tg, PEER_N_KEYS), lambda i: (0, i, 0)),
        scratch_shapes=[pltpu.VMEM((tg, rows), jnp.int32), pltpu.VMEM((tg, rows), F32)],
        compiler_params=_cparams(("parallel",)),
        name="peer_gates",
    )(e_t, g_t)


def _gelu_tanh(x):
    return 0.5 * x * (1.0 + jnp.tanh(math.sqrt(2.0 / math.pi) * (x + 0.044715 * (x * x * x))))


PEER_BLOCKS_PER_STEP = 4
PEER_TN = 1024
PEER_VMEM_LIMIT = 60 * 1024 * 1024


def _peer_expert_kernel(x_ref, gn_ref, gate_ref, u_ref, v_ref, gf_ref, o_ref, h_scr, acc_scr, coef_scr):
    s = pl.program_id(1)
    last = pl.num_programs(1) - 1

    @pl.when(s == 0)
    def _():
        h_scr[...] = _rms(x_ref[...], gn_ref[...]).astype(BF16)
        acc_scr[...] = jnp.zeros(acc_scr.shape, F32)
        coef_scr[...] = jnp.zeros(coef_scr.shape, BF16)

    acc_scr[...] += _dot(coef_scr[(s + 1) % 2], v_ref[...].astype(BF16))

    gate = jnp.concatenate([gate_ref[q].astype(F32) for q in range(gate_ref.shape[0])], axis=1)
    hid = _gelu_tanh(_dot_nt(h_scr[...], u_ref[...].astype(BF16)))
    coef_scr[s % 2] = (gate * hid).astype(BF16)

    @pl.when(s == last)
    def _():
        o_ref[...] = _rms(x_ref[...] + acc_scr[...], gf_ref[...])


def _peer_experts(x, norm_g, gates, u_tab, v_tab, final_g, *, tn):
    n, d = x.shape
    eb = PEER_N_KEYS
    assert u_tab.shape[0] == eb * eb
    nblk = PEER_BLOCKS_PER_STEP
    ns = eb // nblk

    def cur(s):
        return jnp.minimum(s, ns - 1)

    def prev(s):
        return jnp.maximum(s - 1, 0)

    return pl.pallas_call(
        _peer_expert_kernel,
        out_shape=jax.ShapeDtypeStruct((n, d), F32),
        grid=(n // tn, ns + 1),
        in_specs=[
            pl.BlockSpec((tn, d), lambda i, s: (i, 0), pipeline_mode=pl.Buffered(1)),
            pl.BlockSpec((1, d), lambda i, s: (0, 0)),
            pl.BlockSpec((nblk, tn, eb), lambda i, s: (cur(s), i, 0)),
            pl.BlockSpec((nblk * eb, d), lambda i, s: (cur(s), 0)),
            pl.BlockSpec((nblk * eb, d), lambda i, s: (prev(s), 0)),
            pl.BlockSpec((1, d), lambda i, s: (0, 0)),
        ],
        out_specs=pl.BlockSpec((tn, d), lambda i, s: (i, 0), pipeline_mode=pl.Buffered(1)),
        scratch_shapes=[pltpu.VMEM((tn, d), BF16), pltpu.VMEM((tn, d), F32),
                        pltpu.VMEM((2, tn, nblk * eb), BF16)],
        compiler_params=_cparams(("parallel", "arbitrary"), vmem=PEER_VMEM_LIMIT),
        name="peer_experts",
    )(x, norm_g.reshape(1, d), gates, u_tab, v_tab, final_g.reshape(1, d))


def _layer(x, wts, layer, mem_k, mem_v, conv_prev, gdn_state, k_past, v_past, *, tm):
    bsz, t, d = x.shape
    n = bsz * t
    x2d = x.reshape(n, d)
    main, k_new, v_new, tail = _in_proj(x2d, wts["norm_mix_g"], wts["w_in"], tm=min(IN_TM, n))
    conv_dim = 3 * GDN_WIDTH
    conv_prev8 = jnp.concatenate(
        [jnp.zeros((bsz, SUBLANES - (GDN_CONV - 1), conv_dim), F32), conv_prev.astype(F32)], axis=1)
    o_a, s_new = _gdn(main, tail, conv_prev8, gdn_state.astype(F32), wts["gdn_conv_w"], wts["gdn_a_log"],
                      wts["gdn_dt_bias"], wts["gdn_norm_g"], bsz=bsz, t=t)
    conv_new = main.reshape(bsz, t, -1)[:, t - (GDN_CONV - 1):, :conv_dim]
    lam_init = 0.8 - 0.6 * math.exp(-0.3 * layer)
    lams = tuple(wts[k].reshape(1, -1) for k in ("diff_lambda_q1", "diff_lambda_k1", "diff_lambda_q2", "diff_lambda_k2"))
    if k_past is None:
        o_b = _diff_prompt(main, k_new, v_new, lams, wts["diff_subln_g"], bsz=bsz, t=t, lam_init=lam_init)
    else:
        o_b = _diff_sample(main, k_new, v_new, k_past, v_past, lams, wts["diff_subln_g"], bsz=bsz, t=t,
                           lam_init=lam_init)
    k_rows = k_new.reshape(bsz, t, DIFF_HEADS, HEAD_DIM)
    v_rows = v_new.reshape(bsz, t, DIFF_HEADS, HEAD_DIM)
    merged = _merge(o_a, o_b, wts["w_branch_a"], wts["w_branch_b"], tail, tm=tm)
    x1 = _matmul(merged, wts["w_out"], residual=x2d, tm=tm, tn=MM_TN, name="out_proj")
    qm = _matmul(x1, wts["w_mq"], norm_g=wts["norm_cross_g"], tm=tm, tn=MM_TN, out_dtype=BF16, name="mem_q")
    mt = mem_k.shape[1]
    oc = _cross_attend(qm, mem_k.reshape(bsz, mt, d), mem_v.reshape(bsz, mt, d), bsz=bsz, t=t, tm=min(tm, t))
    x2 = _matmul(oc, wts["w_mo"], residual=x1, tm=tm, tn=MM_TN, name="mem_o")
    qp = _matmul(x2, wts["peer_w_q"], norm_g=wts["norm_ffn_g"], tm=tm, tn=MM_TN, name="peer_q")
    tn = min(PEER_TN, n)
    gates = _peer_route(qp, wts["peer_sub_keys"], tn=LANES)
    y = _peer_experts(x2, wts["norm_ffn_g"], gates, wts["peer_u"], wts["peer_v"], wts["final_norm_g"], tn=tn)
    return y.reshape(bsz, t, d), k_rows, v_rows, s_new, conv_new


def kernel(x_prompt, x_sample, cache_diff_k, cache_diff_v, state_gdn, state_conv, cache_mem_k, cache_mem_v,
           mem_prompt, norm_mix_g, w_in, gdn_conv_w, gdn_a_log, gdn_dt_bias, gdn_norm_g,
           diff_lambda_q1, diff_lambda_k1, diff_lambda_q2, diff_lambda_k2, diff_subln_g,
           w_branch_a, w_branch_b, w_out, norm_cross_g, norm_mem_g, w_mq, w_mk, w_mv, w_mo,
           norm_ffn_g, peer_w_q, peer_sub_keys, peer_u, peer_v, final_norm_g):
    depth = w_in.shape[0]
    assert depth == 1, "final norm is fused into the last layer's PEER kernel"
    l = 0
    wts = {
        "norm_mix_g": norm_mix_g[l], "w_in": _in_proj_weights(w_in[l]), "gdn_conv_w": gdn_conv_w[l],
        "gdn_a_log": gdn_a_log[l], "gdn_dt_bias": gdn_dt_bias[l], "gdn_norm_g": gdn_norm_g[l],
        "diff_lambda_q1": diff_lambda_q1[l], "diff_lambda_k1": diff_lambda_k1[l],
        "diff_lambda_q2": diff_lambda_q2[l], "diff_lambda_k2": diff_lambda_k2[l],
        "diff_subln_g": diff_subln_g[l], "w_branch_a": w_branch_a[l].astype(BF16),
        "w_branch_b": w_branch_b[l].astype(BF16), "w_out": w_out[l].astype(BF16),
        "norm_cross_g": norm_cross_g[l], "w_mq": w_mq[l].astype(BF16), "w_mo": w_mo[l].astype(BF16),
        "norm_ffn_g": norm_ffn_g[l], "peer_w_q": peer_w_q[l].astype(BF16), "peer_sub_keys": peer_sub_keys[l],
        "peer_u": peer_u[l], "peer_v": peer_v[l], "final_norm_g": final_norm_g,
    }
    bp, tp, d = x_prompt.shape
    bs, ts, _ = x_sample.shape
    mem2d = mem_prompt.reshape(-1, d)
    tmm = min(512, mem2d.shape[0])
    mem_k = _matmul(mem2d, w_mk[l], norm_g=norm_mem_g[l], tm=tmm, tn=512, name="mem_k")
    mem_v = _matmul(mem2d, w_mv[l], norm_g=norm_mem_g[l], tm=tmm, tn=512, name="mem_v")
    mshape = mem_prompt.shape[:2] + (MEM_HEADS, d // MEM_HEADS)
    mem_k = mem_k.reshape(mshape)
    mem_v = mem_v.reshape(mshape)
    conv0 = jnp.zeros((bp, GDN_CONV - 1, 3 * GDN_WIDTH), F32)
    s0 = jnp.zeros((bp, GDN_HEADS, HEAD_DIM, HEAD_DIM), F32)
    yp, pk, pv, ps, pc = _layer(x_prompt, wts, l, mem_k, mem_v, conv0, s0, None, None, tm=min(PROMPT_TM, bp * tp))
    ys, sk, sv, ss, sc = _layer(x_sample, wts, l, cache_mem_k[l], cache_mem_v[l], state_conv[l], state_gdn[l],
                                cache_diff_k[l], cache_diff_v[l], tm=min(SAMPLE_TM, bs * ts))
    return (yp, ys, pk[None], pv[None], ps[None], pc[None], mem_k[None], mem_v[None],
            sk[None], sv[None], ss[None], sc[None])
```

```python
import functools
import math

import jax
import jax.numpy as jnp
from jax import lax
from jax.experimental import pallas as pl
from jax.experimental.pallas import tpu as pltpu

F32 = jnp.float32
BF16 = jnp.bfloat16
HIGHEST = lax.Precision.HIGHEST

RMS_EPS = 1e-6
CHUNK = 64
CHUNK_SHIFT = 6
GDN_HEADS = 8
HEAD_DIM = 128
GDN_WIDTH = GDN_HEADS * HEAD_DIM
GDN_CONV = 4
DIFF_HEADS = 8
DIFF_HEAD_DIM = 64
MEM_HEADS = 4
PEER_HEADS = 8
PEER_N_KEYS = 128
PEER_TOPK = 16

LANES = 128
SUBLANES = 8
V7X_VMEM_LIMIT = 52 * 1024 * 1024

PROMPT_TM = 1024
SAMPLE_TM = 256
MM_TN = 1024
DIFF_TQ = 1024
DIFF_SUB = 256
GDN_CHUNKS_PER_STEP = 8


def _cparams(sem, vmem=V7X_VMEM_LIMIT):
    return pltpu.CompilerParams(dimension_semantics=sem, vmem_limit_bytes=vmem)


def _dot(a, b, precision=None):
    return jnp.dot(a, b, preferred_element_type=F32, precision=precision)


def _dot_nt(a, b, precision=None):
    return lax.dot_general(a, b, (((1,), (1,)), ((), ())), preferred_element_type=F32, precision=precision)


def _split_bf16(x):
    hi = x.astype(BF16)
    return hi, (x - hi.astype(F32)).astype(BF16)


def _cat_parts(p, q):
    return jnp.concatenate([p[0], q[0]], axis=0), jnp.concatenate([p[1], q[1]], axis=0)


def _dot3_parts(a_parts, b_parts, dot):
    a_hi, a_lo = a_parts
    b_hi, b_lo = b_parts
    m = a_hi.shape[0]
    both = dot(jnp.concatenate([a_hi, a_lo], axis=0), b_hi)
    return both[:m] + both[m:] + dot(a_hi, b_lo)


def _dot3(a, b):
    return _dot3_parts(_split_bf16(a), _split_bf16(b), _dot)


def _rms(xf, g):
    return xf * lax.rsqrt(jnp.mean(xf * xf, axis=-1, keepdims=True) + RMS_EPS) * g


def _mm_kernel(*refs, has_norm, has_res):
    it = iter(refs)
    x_ref = next(it)
    g_ref = next(it) if has_norm else None
    w_ref = next(it)
    r_ref = next(it) if has_res else None
    o_ref = next(it)
    h_scr = next(it)

    @pl.when(pl.program_id(1) == 0)
    def _():
        xf = x_ref[...].astype(F32)
        if has_norm:
            xf = _rms(xf, g_ref[...])
        h_scr[...] = xf.astype(BF16)

    acc = _dot(h_scr[...], w_ref[...].astype(BF16))
    if has_res:
        acc = acc + r_ref[...]
    o_ref[...] = acc.astype(o_ref.dtype)


def _matmul(x, w, *, norm_g=None, residual=None, tm, tn, out_dtype=F32, name="matmul"):
    m, k = x.shape
    n = w.shape[1]
    assert m % tm == 0 and n % tn == 0, (m, n, tm, tn)
    in_specs = [pl.BlockSpec((tm, k), lambda i, j: (i, 0))]
    args = [x]
    if norm_g is not None:
        in_specs.append(pl.BlockSpec((1, k), lambda i, j: (0, 0)))
        args.append(norm_g.reshape(1, k).astype(F32))
    in_specs.append(pl.BlockSpec((k, tn), lambda i, j: (0, j)))
    args.append(w)
    if residual is not None:
        in_specs.append(pl.BlockSpec((tm, tn), lambda i, j: (i, j)))
        args.append(residual)
    return pl.pallas_call(
        functools.partial(_mm_kernel, has_norm=norm_g is not None, has_res=residual is not None),
        out_shape=jax.ShapeDtypeStruct((m, n), out_dtype),
        grid=(m // tm, n // tn),
        in_specs=in_specs,
        out_specs=pl.BlockSpec((tm, tn), lambda i, j: (i, j)),
        scratch_shapes=[pltpu.VMEM((tm, k), BF16)],
        compiler_params=_cparams(("parallel", "arbitrary")),
        name=name,
    )(*args)


IN_TN = 512
IN_TM = 1024
IN_W1_TILES = 8
IN_W2_TILES = 14
IN_MAIN_TILES = 10
IN_K_TILE0 = 10
IN_V_TILE0 = 12
IN_TAIL_TILE0 = 14
IN_TILES = IN_W1_TILES + IN_W2_TILES + 1
MAIN_COLS = IN_MAIN_TILES * IN_TN
TAIL_COLS = (IN_TILES - IN_TAIL_TILE0) * IN_TN
COL_DQ_MAIN = 4096
COL_GB_TAIL = 2048
COL_BA_TAIL = 4096


def _in_proj_kernel(x_ref, g_ref, w1_ref, w2_ref, w3_ref, main_ref, k_ref, v_ref, tail_ref, h_scr):
    j = pl.program_id(1)

    @pl.when(j == 0)
    def _():
        h_scr[...] = _rms(x_ref[...], g_ref[...]).astype(BF16)

    routes = ((0, IN_W1_TILES, w1_ref, main_ref),
              (IN_W1_TILES, IN_K_TILE0, w2_ref, main_ref),
              (IN_K_TILE0, IN_V_TILE0, w2_ref, k_ref),
              (IN_V_TILE0, IN_TAIL_TILE0, w2_ref, v_ref),
              (IN_TAIL_TILE0, IN_TILES - 1, w2_ref, tail_ref),
              (IN_TILES - 1, IN_TILES, w3_ref, tail_ref))
    for lo, hi, w_ref, dst in routes:
        @pl.when(jnp.logical_and(j >= lo, j < hi))
        def _(w_ref=w_ref, dst=dst):
            dst[...] = _dot(h_scr[...], w_ref[...])


def _in_proj_weights(w_in):
    c_z = 4 * GDN_WIDTH
    c_ba = c_z + 2 * GDN_HEADS
    assert c_z == IN_W1_TILES * IN_TN and w_in.shape[1] - c_ba == IN_W2_TILES * IN_TN
    w_bf16 = w_in.astype(BF16)
    return w_bf16, w_bf16[:, c_ba:]


def _in_proj(x, norm_g, w_pieces, *, tm):
    m, d = x.shape
    w_all, w2 = w_pieces
    w1 = w3 = w_all
    tn = IN_TN
    out = lambda cols: jax.ShapeDtypeStruct((m, cols), F32)
    return pl.pallas_call(
        _in_proj_kernel,
        out_shape=(out(MAIN_COLS), out(2 * IN_TN), out(2 * IN_TN), out(TAIL_COLS)),
        grid=(m // tm, IN_TILES),
        in_specs=[
            pl.BlockSpec((tm, d), lambda i, j: (i, 0)),
            pl.BlockSpec((1, d), lambda i, j: (0, 0)),
            pl.BlockSpec((d, tn), lambda i, j: (0, jnp.minimum(j, IN_W1_TILES - 1))),
            pl.BlockSpec((d, tn), lambda i, j: (0, jnp.clip(j - IN_W1_TILES, 0, IN_W2_TILES - 1))),
            pl.BlockSpec((d, tn), lambda i, j: (0, IN_W1_TILES)),
        ],
        out_specs=(
            pl.BlockSpec((tm, tn), lambda i, j: (i, jnp.minimum(j, IN_MAIN_TILES - 1))),
            pl.BlockSpec((tm, tn), lambda i, j: (i, jnp.clip(j - IN_K_TILE0, 0, 1))),
            pl.BlockSpec((tm, tn), lambda i, j: (i, jnp.clip(j - IN_V_TILE0, 0, 1))),
            pl.BlockSpec((tm, tn), lambda i, j: (i, jnp.clip(j - IN_TAIL_TILE0, 0, IN_TILES - IN_TAIL_TILE0 - 1))),
        ),
        scratch_shapes=[pltpu.VMEM((tm, d), BF16)],
        compiler_params=_cparams(("parallel", "arbitrary")),
        name="in_proj",
    )(x, norm_g.reshape(1, d).astype(F32), w1, w2, w3)


def _gdn_kernel(qkv_ref, z_ref, ba_ref, prev_ref, s0_ref, convw_ref, alog_ref, dtb_ref, ng_ref,
                o_ref, sfin_ref, s_scr, xbuf, *, L, nch):
    step = pl.program_id(1)
    rows = nch * L

    @pl.when(step == 0)
    def _():
        s_scr[...] = s0_ref[0]
        xbuf[0:SUBLANES, :] = prev_ref[0]

    xbuf[SUBLANES:SUBLANES + rows, :] = qkv_ref[...]
    base = SUBLANES - (GDN_CONV - 1)
    y = xbuf[base:base + rows, :] * convw_ref[0:1, :]
    for i in range(1, GDN_CONV):
        y = y + xbuf[base + i:base + i + rows, :] * convw_ref[i:i + 1, :]
    y = y * jax.nn.sigmoid(y)
    carry = xbuf[rows:rows + SUBLANES, :]
    xbuf[0:SUBLANES, :] = carry

    row = lax.broadcasted_iota(jnp.int32, (L, L), 0)
    col = lax.broadcasted_iota(jnp.int32, (L, L), 1)
    lower = row >= col
    strict = row > col
    eye = (row == col).astype(F32)
    log_l = int(math.log2(L))
    n_double = log_l - 1
    assert 2 ** log_l == L

    ba = ba_ref[...]
    beta_all = jax.nn.sigmoid(ba)
    a_in = ba + dtb_ref[...]
    softplus = jnp.maximum(a_in, 0.0) + jnp.log(1.0 + jnp.exp(-jnp.abs(a_in)))
    g_all = -jnp.exp(alog_ref[...]) * softplus
    rr = lax.broadcasted_iota(jnp.int32, (rows, rows), 0)
    cc = lax.broadcasted_iota(jnp.int32, (rows, rows), 1)
    same_chunk_lower = jnp.logical_and(rr >= cc, jnp.right_shift(rr, log_l) == jnp.right_shift(cc, log_l))
    g_cum = _dot(same_chunk_lower.astype(F32), g_all, HIGHEST)
    g_cum_t = g_cum.T

    heads = range(GDN_HEADS)
    chains = [(c, h) for c in range(nch) for h in heads]
    ids = range(len(chains))

    def tok(c):
        return slice(c * L, (c + 1) * L)

    def head_cols(part, c, h):
        return y[tok(c), part * GDN_WIDTH + h * HEAD_DIM:part * GDN_WIDTH + (h + 1) * HEAD_DIM]

    q = [head_cols(0, c, h) for c, h in chains]
    k = [head_cols(1, c, h) for c, h in chains]
    v = [head_cols(2, c, h) for c, h in chains]
    q = [a * lax.rsqrt(jnp.sum(a * a, axis=-1, keepdims=True) + 1e-6) * (HEAD_DIM ** -0.5) for a in q]
    k = [a * lax.rsqrt(jnp.sum(a * a, axis=-1, keepdims=True) + 1e-6) for a in k]
    beta = [beta_all[tok(c), h:h + 1] for c, h in chains]
    g_col = [g_cum[tok(c), GDN_HEADS + h:GDN_HEADS + h + 1] for c, h in chains]
    g_row = [g_cum_t[GDN_HEADS + h:GDN_HEADS + h + 1, tok(c)] for c, h in chains]
    decay = [jnp.where(lower, jnp.exp(jnp.where(lower, g_col[n] - g_row[n], 0.0)), 0.0) for n in ids]
    g_last = [g_col[n][L - 1:L, :] for n in ids]
    e_col = [jnp.exp(g_col[n]) for n in ids]
    qk_kk = [_dot_nt(jnp.concatenate([q[n], k[n]], axis=0).astype(BF16), k[n].astype(BF16)) for n in ids]
    x = [-jnp.where(strict, beta[n] * qk_kk[n][L:] * decay[n], 0.0) for n in ids]
    t_inv = [eye + x[n] for n in ids]
    x_parts = [_split_bf16(x[n]) for n in ids]
    xp = [_dot3_parts(x_parts[n], x_parts[n], _dot) for n in ids]
    for _ in range(n_double - 1):
        xp_parts = [_split_bf16(xp[n]) for n in ids]
        both = [_dot3_parts(_cat_parts(_split_bf16(t_inv[n]), xp_parts[n]), xp_parts[n], _dot) for n in ids]
        t_inv = [t_inv[n] + both[n][:L] for n in ids]
        xp = [both[n][L:] for n in ids]
    t_inv = [t_inv[n] + _dot3(t_inv[n], xp[n]) for n in ids]
    sol = [_dot3(t_inv[n], jnp.concatenate([v[n] * beta[n], k[n] * (beta[n] * e_col[n])], axis=-1)) for n in ids]
    qk = [jnp.where(lower, qk_kk[n][:L] * decay[n], 0.0) for n in ids]
    k_dec_t = [(k[n] * jnp.exp(g_last[n] - g_col[n])).T for n in ids]
    left_s = [jnp.concatenate([sol[n][:, HEAD_DIM:], q[n] * e_col[n]], axis=0).astype(BF16) for n in ids]
    left_v = [jnp.concatenate([qk[n], k_dec_t[n]], axis=0).astype(BF16) for n in ids]

    s = [s_scr[h] for h in heads]
    for c in range(nch):
        n0 = c * GDN_HEADS
        ws_qs = [_dot(left_s[n0 + h], s[h].astype(BF16)) for h in heads]
        v_new = [sol[n0 + h][:, :HEAD_DIM] - ws_qs[h][:L] for h in heads]
        tail = [_dot(left_v[n0 + h], v_new[h].astype(BF16)) for h in heads]
        s = [s[h] * jnp.exp(g_last[n0 + h]) + tail[h][L:] for h in heads]
        for h in heads:
            zf = z_ref[tok(c), h * HEAD_DIM:(h + 1) * HEAD_DIM]
            o = _rms(ws_qs[h][L:] + tail[h][:L], ng_ref[...]) * (zf * jax.nn.sigmoid(zf))
            o_ref[tok(c), h * HEAD_DIM:(h + 1) * HEAD_DIM] = o.astype(o_ref.dtype)
    for h in heads:
        s_scr[h] = s[h]

    @pl.when(step == pl.num_programs(1) - 1)
    def _():
        sfin_ref[0] = s_scr[...]


def _decay_lanes(p):
    return jnp.zeros((1, LANES), F32).at[0, GDN_HEADS:2 * GDN_HEADS].set(p.astype(F32))


def _gdn(main, tail, conv_prev8, s0, conv_w, a_log, dt_bias, norm_g, *, bsz, t):
    L = min(CHUNK, t)
    nch = GDN_CHUNKS_PER_STEP if (t // L) % GDN_CHUNKS_PER_STEP == 0 else 1
    nc = t // (L * nch)
    rows = L * nch
    conv_dim = 3 * GDN_WIDTH
    kern = functools.partial(_gdn_kernel, L=L, nch=nch)
    o, s_fin = pl.pallas_call(
        kern,
        out_shape=(jax.ShapeDtypeStruct((bsz * t, GDN_WIDTH), BF16),
                   jax.ShapeDtypeStruct((bsz, GDN_HEADS, HEAD_DIM, HEAD_DIM), F32)),
        grid=(bsz, nc),
        in_specs=[
            pl.BlockSpec((rows, conv_dim), lambda b, c: (b * nc + c, 0)),
            pl.BlockSpec((rows, GDN_WIDTH), lambda b, c: (b * nc + c, conv_dim // GDN_WIDTH)),
            pl.BlockSpec((rows, LANES), lambda b, c: (b * nc + c, COL_BA_TAIL // LANES)),
            pl.BlockSpec((1, SUBLANES, conv_dim), lambda b, c: (b, 0, 0)),
            pl.BlockSpec((1, GDN_HEADS, HEAD_DIM, HEAD_DIM), lambda b, c: (b, 0, 0, 0)),
            pl.BlockSpec((GDN_CONV, conv_dim), lambda b, c: (0, 0)),
            pl.BlockSpec((1, LANES), lambda b, c: (0, 0)),
            pl.BlockSpec((1, LANES), lambda b, c: (0, 0)),
            pl.BlockSpec((1, HEAD_DIM), lambda b, c: (0, 0)),
        ],
        out_specs=(pl.BlockSpec((rows, GDN_WIDTH), lambda b, c: (b * nc + c, 0)),
                   pl.BlockSpec((1, GDN_HEADS, HEAD_DIM, HEAD_DIM), lambda b, c: (b, 0, 0, 0))),
        scratch_shapes=[pltpu.VMEM((GDN_HEADS, HEAD_DIM, HEAD_DIM), F32),
                        pltpu.VMEM((SUBLANES + rows + SUBLANES, conv_dim), F32)],
        compiler_params=_cparams(("parallel", "arbitrary")),
        name="gdn",
    )(main, main, tail, conv_prev8, s0, conv_w, _decay_lanes(a_log), _decay_lanes(dt_bias),
      norm_g.reshape(1, -1))
    return o, s_fin


def _lambda_value(lam_refs, lam_init):
    lq1, lk1, lq2, lk2 = (r[...] for r in lam_refs)
    return (jnp.exp(jnp.sum(lq1 * lk1, axis=-1, keepdims=True))
            - jnp.exp(jnp.sum(lq2 * lk2, axis=-1, keepdims=True)) + lam_init)


def _softmax_update(m_scr, l_scr, acc_scr, scores, values):
    idx = range(len(scores))
    m_old = [m_scr[n] for n in idx]
    m_new = [jnp.maximum(m_old[n], jnp.max(scores[n], axis=-1, keepdims=True)) for n in idx]
    p = [jnp.exp(scores[n] - m_new[n]) for n in idx]
    alpha = [jnp.exp(m_old[n] - m_new[n]) for n in idx]
    pv = [_dot(p[n].astype(BF16), values[n]) for n in idx]
    for n in idx:
        l_scr[n] = alpha[n] * l_scr[n] + jnp.sum(p[n], axis=-1, keepdims=True)
        acc_scr[n] = alpha[n] * acc_scr[n] + pv[n]
        m_scr[n] = m_new[n]


def _diff_finalize(o1, o2, lam_refs, g_ref, lam_init):
    lam = _lambda_value(lam_refs, lam_init)
    return _rms(o1 - lam * o2, g_ref[...]) * (1.0 - lam_init)


def _diff_init(m_scr, l_scr, acc_scr):
    m_scr[...] = jnp.full(m_scr.shape, -jnp.inf, F32)
    l_scr[...] = jnp.zeros(l_scr.shape, F32)
    acc_scr[...] = jnp.zeros(acc_scr.shape, F32)


def _near_bias(slope, r, q_pos, k_pos):
    visible = jnp.right_shift(k_pos, CHUNK_SHIFT) <= jnp.right_shift(q_pos, CHUNK_SHIFT)
    return jnp.where(visible, slope * (r - jnp.abs(q_pos - k_pos)).astype(F32), -jnp.inf)


def _split_distance(slope, d):
    return (-(slope * CHUNK) * jnp.right_shift(d, CHUNK_SHIFT).astype(F32),
            -slope * jnp.bitwise_and(d, CHUNK - 1).astype(F32))


def _diff_prompt_kernel(i_tab, j_tab, slopes_ref, q_ref, k_ref, v_ref, lq1, lk1, lq2, lk2, g_ref, o_ref,
                        m_scr, l_scr, acc_scr, lhs_scr, s_even, s_odd, *, tq, lam_init):
    h = pl.program_id(1)
    s_id = pl.program_id(2)
    n_pairs = pl.num_programs(2) - 1
    score_on = s_id < n_pairs
    consume_on = s_id >= 1
    i = i_tab[s_id]
    j = j_tab[s_id]
    ci = i_tab[jnp.maximum(s_id - 1, 0)]
    cj = j_tab[jnp.maximum(s_id - 1, 0)]
    slope = slopes_ref[h]
    half = DIFF_HEAD_DIM
    lane = lax.broadcasted_iota(jnp.int32, (tq, LANES), 1)

    @pl.when(jnp.logical_and(consume_on, cj == 0))
    def _():
        _diff_init(m_scr, l_scr, acc_scr)

    @pl.when(jnp.logical_and(score_on, j == 0))
    def _():
        qs = q_ref[...] * (DIFF_HEAD_DIM ** -0.5)
        lhs_scr[0] = jnp.where(lane < half, qs, 1.0).astype(BF16)
        lhs_scr[1] = jnp.where(lane >= half, qs, 1.0).astype(BF16)

    sub = min(DIFF_SUB, tq)
    units = [(n, qb) for qb in range(tq // sub) for n in range(2)]

    def score_tile(near, dst):
        kb = k_ref[...]
        if near:
            aug0 = aug1 = jnp.zeros_like(kb)
        else:
            d = (i - j) * tq - lax.broadcasted_iota(jnp.int32, (tq, LANES), 0)
            t_hi, t_lo = _split_distance(slope, d)
            aug0 = jnp.where(lane == half, t_hi, jnp.where(lane == half + 1, t_lo, 0.0))
            aug1 = jnp.where(lane == 0, t_hi, jnp.where(lane == 1, t_lo, 0.0))
        keys = [jnp.where(lane < half, kb, aug0).astype(BF16), jnp.where(lane >= half, kb, aug1).astype(BF16)]
        bias = None
        for n, qb in units:
            cols = slice(qb * sub, (qb + 1) * sub)
            if near:
                live = (qb + 1) * sub
                if n == 0:
                    k_pos = lax.broadcasted_iota(jnp.int32, (live, sub), 0)
                    q_pos = qb * sub + lax.broadcasted_iota(jnp.int32, (live, sub), 1)
                    bias = _near_bias(slope, q_pos, q_pos, k_pos)
                dst[n, :live, cols] = _dot_nt(keys[n][:live], lhs_scr[n, cols, :]) + bias
            else:
                dst[n, :, cols] = _dot_nt(keys[n], lhs_scr[n, cols, :])

    def consume_tile(near, src):
        vt = v_ref[...].T.astype(BF16)
        idx = range(len(units))
        cols = [slice(qb * sub, (qb + 1) * sub) for _, qb in units]
        live = [(qb + 1) * sub if near else tq for _, qb in units]
        s = [src[n, :live[u], cols[u]] for u, (n, _) in enumerate(units)]
        m_old = [m_scr[n, :, cols[u]] for u, (n, _) in enumerate(units)]
        m_new = [jnp.maximum(m_old[u], jnp.max(s[u], axis=0, keepdims=True)) for u in idx]
        p = [jnp.exp(s[u] - m_new[u]) for u in idx]
        alpha = [jnp.exp(m_old[u] - m_new[u]) for u in idx]
        pv = [_dot(vt[:, :live[u]], p[u].astype(BF16)) for u in idx]
        for u, (n, _) in enumerate(units):
            l_scr[n, :, cols[u]] = alpha[u] * l_scr[n, :, cols[u]] + jnp.sum(p[u], axis=0, keepdims=True)
            acc_scr[n, :, cols[u]] = alpha[u] * acc_scr[n, :, cols[u]] + pv[u]
            m_scr[n, :, cols[u]] = m_new[u]

    def stage(cond, score_near=None, consume_near=None):
        for parity, (dst, src) in enumerate(((s_even, s_odd), (s_odd, s_even))):
            @pl.when(jnp.logical_and(cond, s_id % 2 == parity))
            def _(dst=dst, src=src):
                if score_near is not None:
                    score_tile(score_near, dst)
                if consume_near is not None:
                    consume_tile(consume_near, src)

    both = jnp.logical_and(score_on, consume_on)
    stage(jnp.logical_and(both, jnp.logical_and(j < i, cj < ci)), score_near=False, consume_near=False)
    stage(jnp.logical_and(both, jnp.logical_and(j < i, cj == ci)), score_near=False, consume_near=True)
    stage(jnp.logical_and(both, j == i), score_near=True, consume_near=False)
    stage(jnp.logical_not(consume_on), score_near=True)
    stage(jnp.logical_not(score_on), consume_near=True)

    @pl.when(jnp.logical_and(consume_on, cj == ci))
    def _():
        o_ref[...] = _diff_finalize((acc_scr[0] / l_scr[0]).T, (acc_scr[1] / l_scr[1]).T,
                                    (lq1, lk1, lq2, lk2), g_ref, lam_init).astype(o_ref.dtype)


def _alibi_slopes():
    return jnp.asarray([2.0 ** (-8.0 * (i + 1) / DIFF_HEADS) for i in range(DIFF_HEADS)], F32)


def _diff_prompt(main, k_new, v_new, lams, subln_g, *, bsz, t, lam_init):
    tq = min(DIFF_TQ, t)
    nq = t // tq
    assert tq % CHUNK == 0 and t <= CHUNK * 255
    kern = functools.partial(_diff_prompt_kernel, tq=tq, lam_init=lam_init)
    qc, kc, vc = COL_DQ_MAIN // LANES, 0, 0
    pairs = [(i, j) for i in range(nq) for j in range(i + 1)]
    pairs.append(pairs[-1])
    i_tab = jnp.asarray([p[0] for p in pairs], jnp.int32)
    j_tab = jnp.asarray([p[1] for p in pairs], jnp.int32)
    zero = lambda b, h, s, it, jt: (0, 0)

    def consumed(tab, s):
        return tab[jnp.maximum(s - 1, 0)]

    grid_spec = pltpu.PrefetchScalarGridSpec(
        num_scalar_prefetch=2,
        grid=(bsz, DIFF_HEADS, len(pairs)),
        in_specs=[
            pl.BlockSpec(memory_space=pltpu.SMEM),
            pl.BlockSpec((tq, LANES), lambda b, h, s, it, jt: (b * nq + it[s], qc + h)),
            pl.BlockSpec((tq, LANES), lambda b, h, s, it, jt: (b * nq + jt[s], kc + h)),
            pl.BlockSpec((tq, LANES), lambda b, h, s, it, jt: (b * nq + consumed(jt, s), vc + h)),
        ] + [pl.BlockSpec((1, DIFF_HEAD_DIM), zero)] * 4 + [pl.BlockSpec((1, HEAD_DIM), zero)],
        out_specs=pl.BlockSpec((tq, LANES), lambda b, h, s, it, jt: (b * nq + consumed(it, s), h)),
        scratch_shapes=[pltpu.VMEM((2, 1, tq), F32), pltpu.VMEM((2, 1, tq), F32),
                        pltpu.VMEM((2, HEAD_DIM, tq), F32), pltpu.VMEM((2, tq, LANES), BF16),
                        pltpu.VMEM((2, tq, tq), F32), pltpu.VMEM((2, tq, tq), F32)],
    )
    return pl.pallas_call(
        kern,
        out_shape=jax.ShapeDtypeStruct((bsz * t, DIFF_HEADS * HEAD_DIM), BF16),
        grid_spec=grid_spec,
        compiler_params=_cparams(("parallel", "parallel", "arbitrary")),
        name="diff_attn_prompt",
    )(i_tab, j_tab, _alibi_slopes(), main, k_new, v_new, *lams, subln_g.reshape(1, -1))


def _diff_sample_kernel(slopes_ref, q_ref, kn_ref, vn_ref, kp_ref, vp_ref, lq1, lk1, lq2, lk2, g_ref, o_ref,
                        m_scr, l_scr, acc_scr, lhs_scr, *, t, tk, past, lam_init):
    j = pl.program_id(1)
    heads = range(DIFF_HEADS)

    def cols(h):
        return slice(h * HEAD_DIM, (h + 1) * HEAD_DIM)

    @pl.when(j == 0)
    def _():
        _diff_init(m_scr, l_scr, acc_scr)
        lane = lax.broadcasted_iota(jnp.int32, (t, LANES), 1)
        r = lax.broadcasted_iota(jnp.int32, (t, t), 0)
        c = lax.broadcasted_iota(jnp.int32, (t, t), 1)
        scores = []
        for h in heads:
            qs = q_ref[:, cols(h)] * (DIFF_HEAD_DIM ** -0.5)
            lhs_scr[h, 0:t, :] = jnp.where(lane < DIFF_HEAD_DIM, qs, 0.0).astype(BF16)
            lhs_scr[h, t:2 * t, :] = jnp.where(lane >= DIFF_HEAD_DIM, qs, 0.0).astype(BF16)
            bias = _near_bias(slopes_ref[h], r, past + r, past + c)
            scores.append(_dot_nt(lhs_scr[h], kn_ref[:, cols(h)].astype(BF16)) + jnp.concatenate([bias, bias], axis=0))
        _softmax_update(m_scr, l_scr, acc_scr, scores, [vn_ref[:, cols(h)].astype(BF16) for h in heads])

    d = (past - j * tk - lax.broadcasted_iota(jnp.int32, (1, tk), 1)).astype(F32)
    kt = jnp.swapaxes(kp_ref[...], 0, 1).astype(BF16)
    vt = jnp.swapaxes(vp_ref[...], 0, 1).astype(BF16)
    scores = [_dot_nt(lhs_scr[h], kt[h]) - slopes_ref[h] * d for h in heads]
    _softmax_update(m_scr, l_scr, acc_scr, scores, [vt[h] for h in heads])

    @pl.when(j == pl.num_programs(1) - 1)
    def _():
        for h in heads:
            o = acc_scr[h] / l_scr[h]
            o_ref[:, cols(h)] = _diff_finalize(o[:t], o[t:], (lq1, lk1, lq2, lk2), g_ref, lam_init).astype(o_ref.dtype)


def _diff_sample(main, k_new, v_new, k_past, v_past, lams, subln_g, *, bsz, t, lam_init):
    past = k_past.shape[1]
    tk = min(1024, past)
    nk = past // tk
    width = DIFF_HEADS * HEAD_DIM
    kern = functools.partial(_diff_sample_kernel, t=t, tk=tk, past=past, lam_init=lam_init)
    zero = lambda b, j: (0, 0)
    cache = pl.BlockSpec((None, tk, DIFF_HEADS, HEAD_DIM), lambda b, j: (b, j, 0, 0))
    return pl.pallas_call(
        kern,
        out_shape=jax.ShapeDtypeStruct((bsz * t, width), BF16),
        grid=(bsz, nk),
        in_specs=[
            pl.BlockSpec(memory_space=pltpu.SMEM),
            pl.BlockSpec((t, width), lambda b, j: (b, COL_DQ_MAIN // width)),
            pl.BlockSpec((t, width), lambda b, j: (b, 0)),
            pl.BlockSpec((t, width), lambda b, j: (b, 0)),
            cache, cache,
        ] + [pl.BlockSpec((1, DIFF_HEAD_DIM), zero)] * 4 + [pl.BlockSpec((1, HEAD_DIM), zero)],
        out_specs=pl.BlockSpec((t, width), lambda b, j: (b, 0)),
        scratch_shapes=[pltpu.VMEM((DIFF_HEADS, 2 * t, 1), F32), pltpu.VMEM((DIFF_HEADS, 2 * t, 1), F32),
                        pltpu.VMEM((DIFF_HEADS, 2 * t, HEAD_DIM), F32), pltpu.VMEM((DIFF_HEADS, 2 * t, LANES), BF16)],
        compiler_params=_cparams(("parallel", "arbitrary")),
        name="diff_attn_sample",
    )(_alibi_slopes(), main, k_new, v_new, k_past, v_past, *lams, subln_g.reshape(1, -1))


def _merge_kernel(oa_ref, ob_ref, wa_ref, wb_ref, ga_ref, gb_ref, o_ref):
    ya = _dot(oa_ref[...], wa_ref[...])
    yb = _dot(ob_ref[...], wb_ref[...])
    o_ref[...] = (jax.nn.sigmoid(ga_ref[...]) * ya + jax.nn.sigmoid(gb_ref[...]) * yb).astype(o_ref.dtype)


def _merge(o_a, o_b, wa, wb, tail, *, tm):
    m = o_a.shape[0]
    d = wa.shape[1]
    tn = MM_TN
    ga0, gb0 = 0, COL_GB_TAIL // tn
    return pl.pallas_call(
        _merge_kernel,
        out_shape=jax.ShapeDtypeStruct((m, d), BF16),
        grid=(m // tm, d // tn),
        in_specs=[
            pl.BlockSpec((tm, o_a.shape[1]), lambda i, j: (i, 0)),
            pl.BlockSpec((tm, o_b.shape[1]), lambda i, j: (i, 0)),
            pl.BlockSpec((wa.shape[0], tn), lambda i, j: (0, j)),
            pl.BlockSpec((wb.shape[0], tn), lambda i, j: (0, j)),
            pl.BlockSpec((tm, tn), lambda i, j: (i, ga0 + j)),
            pl.BlockSpec((tm, tn), lambda i, j: (i, gb0 + j)),
        ],
        out_specs=pl.BlockSpec((tm, tn), lambda i, j: (i, j)),
        compiler_params=_cparams(("parallel", "arbitrary")),
        name="merge",
    )(o_a, o_b, wa, wb, tail, tail)


def _cross_kernel(q_ref, k_ref, v_ref, o_ref):
    dh = q_ref.shape[-1] // MEM_HEADS
    for h in range(MEM_HEADS):
        sl = slice(h * dh, (h + 1) * dh)
        s = _dot_nt(q_ref[:, sl], k_ref[0, :, sl].astype(BF16)) * (dh ** -0.5)
        s = s - jnp.max(s, axis=-1, keepdims=True)
        p = jnp.exp(s)
        p = p / jnp.sum(p, axis=-1, keepdims=True)
        o_ref[:, sl] = _dot(p.astype(BF16), v_ref[0, :, sl].astype(BF16)).astype(o_ref.dtype)


def _cross_attend(q, mem_k, mem_v, *, bsz, t, tm):
    d = q.shape[1]
    nt = t // tm
    mt = mem_k.shape[1]
    return pl.pallas_call(
        _cross_kernel,
        out_shape=jax.ShapeDtypeStruct(q.shape, BF16),
        grid=(bsz, nt),
        in_specs=[
            pl.BlockSpec((tm, d), lambda b, i: (b * nt + i, 0)),
            pl.BlockSpec((1, mt, d), lambda b, i: (b, 0, 0)),
            pl.BlockSpec((1, mt, d), lambda b, i: (b, 0, 0)),
        ],
        out_specs=pl.BlockSpec((tm, d), lambda b, i: (b * nt + i, 0)),
        compiler_params=_cparams(("parallel", "arbitrary")),
        name="cross_attn",
    )(q, mem_k, mem_v)


def _topk_rows(work, cidx, k):
    n, tn = work.shape
    rid = lax.broadcasted_iota(jnp.int32, (n, tn), 0).astype(F32)
    kid = lax.broadcasted_iota(jnp.int32, (k, tn), 0)
    vals = jnp.zeros((k, tn), F32)
    idxs = jnp.zeros((k, tn), F32)
    for t in range(k):
        m = jnp.max(work, axis=0, keepdims=True)
        pos = jnp.min(jnp.where(work == m, rid, float(n)), axis=0, keepdims=True)
        hit = rid == pos
        if cidx is None:
            picked = pos
        else:
            picked = jnp.sum(jnp.where(hit, cidx, 0.0), axis=0, keepdims=True)
        vals = jnp.where(kid == t, m, vals)
        idxs = jnp.where(kid == t, picked, idxs)
        work = jnp.where(hit, -jnp.inf, work)
    return vals, idxs


def _product_candidates(v1, i1, v2, i2, k):
    assert k == 2 * SUBLANES
    sub = lax.broadcasted_iota(jnp.int32, (SUBLANES, v1.shape[1]), 0)
    vals, idxs = [], []
    for a in range(k // 2):
        nb = k // (a + 1)
        width = k if nb > SUBLANES else SUBLANES
        cv = v1[a:a + 1, :] + v2[:width, :]
        ci = i1[a:a + 1, :] * PEER_N_KEYS + i2[:width, :]
        if nb < SUBLANES:
            cv = jnp.where(sub < nb, cv, -jnp.inf)
        vals.append(cv)
        idxs.append(ci)
    vals.append(v1[k // 2:, :] + v2[0:1, :])
    idxs.append(i1[k // 2:, :] * PEER_N_KEYS + i2[0:1, :])
    return jnp.concatenate(vals, axis=0), jnp.concatenate(idxs, axis=0)


def _peer_route_kernel(q_ref, keys_hi_ref, keys_lo_ref, e_ref, g_ref):
    k = PEER_TOPK
    for h in range(PEER_HEADS):
        sub = []
        for p in range(2):
            c0 = (h * 2 + p) * PEER_N_KEYS
            q_parts = _split_bf16(q_ref[:, c0:c0 + PEER_N_KEYS])
            s = _dot3_parts((keys_hi_ref[h, p], keys_lo_ref[h, p]), q_parts, _dot_nt)
            sub.append(_topk_rows(s, None, k))
        (v1, i1), (v2, i2) = sub
        cand, cidx = _product_candidates(v1, i1, v2, i2, k)
        best, eidx = _topk_rows(cand, cidx, k)
        ex = jnp.exp(best - jnp.max(best, axis=0, keepdims=True))
        gate = ex / jnp.sum(ex, axis=0, keepdims=True)
        e_ref[h * k:(h + 1) * k, :] = eidx.astype(jnp.int32)
        g_ref[h * k:(h + 1) * k, :] = gate


PEER_GATE_TN = 256
PEER_ROUTE_TN = 256


def _peer_gate_kernel(et_ref, gt_ref, o_ref, e_scr, g_scr):
    tg = et_ref.shape[1]
    e_scr[...] = et_ref[...].T
    g_scr[...] = gt_ref[...].T
    kid = lax.broadcasted_iota(jnp.int32, (PEER_N_KEYS, PEER_N_KEYS), 0)
    group = 2 * SUBLANES

    def body(n, carry):
        rows = pl.ds(pl.multiple_of(n * group, group), group)
        e16 = e_scr[rows, :]
        g16 = g_scr[rows, :]
        a16 = jnp.right_shift(e16, 7)
        b16 = jnp.bitwise_and(e16, PEER_N_KEYS - 1)
        toks = range(group)
        at = [jnp.where(a16[s:s + 1, :] == kid, g16[s:s + 1, :], 0.0).astype(BF16) for s in toks]
        bt = [jnp.where(b16[s:s + 1, :] == kid, 1.0, 0.0).astype(BF16) for s in toks]
        grids = [_dot_nt(at[s], bt[s]) for s in toks]
        o_ref[:, rows, :] = jnp.swapaxes(jnp.stack(grids, axis=0), 0, 1).astype(BF16)
        return carry

    lax.fori_loop(0, tg // group, body, 0, unroll=8)


def _peer_route(qp, sub_keys, *, tn):
    n = qp.shape[0]
    rows = PEER_HEADS * PEER_TOPK
    keys_hi, keys_lo = _split_bf16(sub_keys.astype(F32))
    keys_spec = pl.BlockSpec(sub_keys.shape, lambda i: (0, 0, 0, 0))
    e_t, g_t = pl.pallas_call(
        _peer_route_kernel,
        out_shape=(jax.ShapeDtypeStruct((rows, n), jnp.int32), jax.ShapeDtypeStruct((rows, n), F32)),
        grid=(n // tn,),
        in_specs=[pl.BlockSpec((tn, qp.shape[1]), lambda i: (i, 0)), keys_spec, keys_spec],
        out_specs=(pl.BlockSpec((rows, tn), lambda i: (0, i)), pl.BlockSpec((rows, tn), lambda i: (0, i))),
        compiler_params=_cparams(("parallel",)),
        name="peer_route",
    )(qp, keys_hi, keys_lo)
    tg = min(PEER_GATE_TN, n)
    return pl.pallas_call(
        _peer_gate_kernel,
        out_shape=jax.ShapeDtypeStruct((PEER_N_KEYS, n, PEER_N_KEYS), BF16),
        grid=(n // tg,),
        in_specs=[pl.BlockSpec((rows, tg), lambda i: (0, i)), pl.BlockSpec((rows, tg), lambda i: (0, i))],
        out_specs=pl.BlockSpec((PEER_N_KEYS, tg, PEER_N_KEYS), lambda i: (0, i, 0)),
        scratch_shapes=[pltpu.VMEM((tg, rows), jnp.int32), pltpu.VMEM((tg, rows), F32)],
        compiler_params=_cparams(("parallel",)),
        name="peer_gates",
    )(e_t, g_t)


def _gelu_tanh(x):
    return 0.5 * x * (1.0 + jnp.tanh(math.sqrt(2.0 / math.pi) * (x + 0.044715 * (x * x * x))))


PEER_BLOCKS_PER_STEP = 4
PEER_TN = 1024
PEER_VMEM_LIMIT = 60 * 1024 * 1024


def _peer_expert_kernel(x_ref, gn_ref, gate_ref, u_ref, v_ref, gf_ref, o_ref, h_scr, acc_scr, coef_scr):
    s = pl.program_id(1)
    last = pl.num_programs(1) - 1

    @pl.when(s == 0)
    def _():
        h_scr[...] = _rms(x_ref[...], gn_ref[...]).astype(BF16)
        acc_scr[...] = jnp.zeros(acc_scr.shape, F32)
        coef_scr[...] = jnp.zeros(coef_scr.shape, BF16)

    acc_scr[...] += _dot(coef_scr[(s + 1) % 2], v_ref[...].astype(BF16))

    gate = jnp.concatenate([gate_ref[q].astype(F32) for q in range(gate_ref.shape[0])], axis=1)
    hid = _gelu_tanh(_dot_nt(h_scr[...], u_ref[...].astype(BF16)))
    coef_scr[s % 2] = (gate * hid).astype(BF16)

    @pl.when(s == last)
    def _():
        o_ref[...] = _rms(x_ref[...] + acc_scr[...], gf_ref[...])


def _peer_experts(x, norm_g, gates, u_tab, v_tab, final_g, *, tn):
    n, d = x.shape
    eb = PEER_N_KEYS
    assert u_tab.shape[0] == eb * eb
    nblk = PEER_BLOCKS_PER_STEP
    ns = eb // nblk

    def cur(s):
        return jnp.minimum(s, ns - 1)

    def prev(s):
        return jnp.maximum(s - 1, 0)

    return pl.pallas_call(
        _peer_expert_kernel,
        out_shape=jax.ShapeDtypeStruct((n, d), F32),
        grid=(n // tn, ns + 1),
        in_specs=[
            pl.BlockSpec((tn, d), lambda i, s: (i, 0), pipeline_mode=pl.Buffered(1)),
            pl.BlockSpec((1, d), lambda i, s: (0, 0)),
            pl.BlockSpec((nblk, tn, eb), lambda i, s: (cur(s), i, 0)),
            pl.BlockSpec((nblk * eb, d), lambda i, s: (cur(s), 0)),
            pl.BlockSpec((nblk * eb, d), lambda i, s: (prev(s), 0)),
            pl.BlockSpec((1, d), lambda i, s: (0, 0)),
        ],
        out_specs=pl.BlockSpec((tn, d), lambda i, s: (i, 0), pipeline_mode=pl.Buffered(1)),
        scratch_shapes=[pltpu.VMEM((tn, d), BF16), pltpu.VMEM((tn, d), F32),
                        pltpu.VMEM((2, tn, nblk * eb), BF16)],
        compiler_params=_cparams(("parallel", "arbitrary"), vmem=PEER_VMEM_LIMIT),
        name="peer_experts",
    )(x, norm_g.reshape(1, d), gates, u_tab, v_tab, final_g.reshape(1, d))


def _layer(x, wts, layer, mem_k, mem_v, conv_prev, gdn_state, k_past, v_past, *, tm):
    bsz, t, d = x.shape
    n = bsz * t
    x2d = x.reshape(n, d)
    main, k_new, v_new, tail = _in_proj(x2d, wts["norm_mix_g"], wts["w_in"], tm=min(IN_TM, n))
    conv_dim = 3 * GDN_WIDTH
    conv_prev8 = jnp.concatenate(
        [jnp.zeros((bsz, SUBLANES - (GDN_CONV - 1), conv_dim), F32), conv_prev.astype(F32)], axis=1)
    o_a, s_new = _gdn(main, tail, conv_prev8, gdn_state.astype(F32), wts["gdn_conv_w"], wts["gdn_a_log"],
                      wts["gdn_dt_bias"], wts["gdn_norm_g"], bsz=bsz, t=t)
    conv_new = main.reshape(bsz, t, -1)[:, t - (GDN_CONV - 1):, :conv_dim]
    lam_init = 0.8 - 0.6 * math.exp(-0.3 * layer)
    lams = tuple(wts[k].reshape(1, -1) for k in ("diff_lambda_q1", "diff_lambda_k1", "diff_lambda_q2", "diff_lambda_k2"))
    if k_past is None:
        o_b = _diff_prompt(main, k_new, v_new, lams, wts["diff_subln_g"], bsz=bsz, t=t, lam_init=lam_init)
    else:
        o_b = _diff_sample(main, k_new, v_new, k_past, v_past, lams, wts["diff_subln_g"], bsz=bsz, t=t,
                           lam_init=lam_init)
    k_rows = k_new.reshape(bsz, t, DIFF_HEADS, HEAD_DIM)
    v_rows = v_new.reshape(bsz, t, DIFF_HEADS, HEAD_DIM)
    merged = _merge(o_a, o_b, wts["w_branch_a"], wts["w_branch_b"], tail, tm=tm)
    x1 = _matmul(merged, wts["w_out"], residual=x2d, tm=tm, tn=MM_TN, name="out_proj")
    qm = _matmul(x1, wts["w_mq"], norm_g=wts["norm_cross_g"], tm=tm, tn=MM_TN, out_dtype=BF16, name="mem_q")
    mt = mem_k.shape[1]
    oc = _cross_attend(qm, mem_k.reshape(bsz, mt, d), mem_v.reshape(bsz, mt, d), bsz=bsz, t=t, tm=min(tm, t))
    x2 = _matmul(oc, wts["w_mo"], residual=x1, tm=tm, tn=MM_TN, name="mem_o")
    qp = _matmul(x2, wts["peer_w_q"], norm_g=wts["norm_ffn_g"], tm=tm, tn=MM_TN, name="peer_q")
    tn = min(PEER_TN, n)
    gates = _peer_route(qp, wts["peer_sub_keys"], tn=min(PEER_ROUTE_TN, n))
    y = _peer_experts(x2, wts["norm_ffn_g"], gates, wts["peer_u"], wts["peer_v"], wts["final_norm_g"], tn=tn)
    return y.reshape(bsz, t, d), k_rows, v_rows, s_new, conv_new


def kernel(x_prompt, x_sample, cache_diff_k, cache_diff_v, state_gdn, state_conv, cache_mem_k, cache_mem_v,
           mem_prompt, norm_mix_g, w_in, gdn_conv_w, gdn_a_log, gdn_dt_bias, gdn_norm_g,
           diff_lambda_q1, diff_lambda_k1, diff_lambda_q2, diff_lambda_k2, diff_subln_g,
           w_branch_a, w_branch_b, w_out, norm_cross_g, norm_mem_g, w_mq, w_mk, w_mv, w_mo,
           norm_ffn_g, peer_w_q, peer_sub_keys, peer_u, peer_v, final_norm_g):
    depth = w_in.shape[0]
    assert depth == 1, "final norm is fused into the last layer's PEER kernel"
    l = 0
    wts = {
        "norm_mix_g": norm_mix_g[l], "w_in": _in_proj_weights(w_in[l]), "gdn_conv_w": gdn_conv_w[l],
        "gdn_a_log": gdn_a_log[l], "gdn_dt_bias": gdn_dt_bias[l], "gdn_norm_g": gdn_norm_g[l],
        "diff_lambda_q1": diff_lambda_q1[l], "diff_lambda_k1": diff_lambda_k1[l],
        "diff_lambda_q2": diff_lambda_q2[l], "diff_lambda_k2": diff_lambda_k2[l],
        "diff_subln_g": diff_subln_g[l], "w_branch_a": w_branch_a[l].astype(BF16),
        "w_branch_b": w_branch_b[l].astype(BF16), "w_out": w_out[l].astype(BF16),
        "norm_cross_g": norm_cross_g[l], "w_mq": w_mq[l].astype(BF16), "w_mo": w_mo[l].astype(BF16),
        "norm_ffn_g": norm_ffn_g[l], "peer_w_q": peer_w_q[l].astype(BF16), "peer_sub_keys": peer_sub_keys[l],
        "peer_u": peer_u[l], "peer_v": peer_v[l], "final_norm_g": final_norm_g,
    }
    bp, tp, d = x_prompt.shape
    bs, ts, _ = x_sample.shape
    mem2d = mem_prompt.reshape(-1, d)
    tmm = min(512, mem2d.shape[0])
    mem_k = _matmul(mem2d, w_mk[l], norm_g=norm_mem_g[l], tm=tmm, tn=512, name="mem_k")
    mem_v = _matmul(mem2d, w_mv[l], norm_g=norm_mem_g[l], tm=tmm, tn=512, name="mem_v")
    mshape = mem_prompt.shape[:2] + (MEM_HEADS, d // MEM_HEADS)
    mem_k = mem_k.reshape(mshape)
    mem_v = mem_v.reshape(mshape)
    conv0 = jnp.zeros((bp, GDN_CONV - 1, 3 * GDN_WIDTH), F32)
    s0 = jnp.zeros((bp, GDN_HEADS, HEAD_DIM, HEAD_DIM), F32)
    yp, pk, pv, ps, pc = _layer(x_prompt, wts, l, mem_k, mem_v, conv0, s0, None, None, tm=min(PROMPT_TM, bp * tp))
    ys, sk, sv, ss, sc = _layer(x_sample, wts, l, cache_mem_k[l], cache_mem_v[l], state_conv[l], state_gdn[l],
                                cache_diff_k[l], cache_diff_v[l], tm=min(SAMPLE_TM, bs * ts))
    return (yp, ys, pk[None], pv[None], ps[None], pc[None], mem_k[None], mem_v[None],
            sk[None], sv[None], ss[None], sc[None])
```

```python
import functools
import math

import jax
import jax.numpy as jnp
from jax import lax
from jax.experimental import pallas as pl
from jax.experimental.pallas import tpu as pltpu

F32 = jnp.float32
BF16 = jnp.bfloat16
HIGHEST = lax.Precision.HIGHEST

RMS_EPS = 1e-6
CHUNK = 64
CHUNK_SHIFT = 6
GDN_HEADS = 8
HEAD_DIM = 128
GDN_WIDTH = GDN_HEADS * HEAD_DIM
GDN_CONV = 4
DIFF_HEADS = 8
DIFF_HEAD_DIM = 64
MEM_HEADS = 4
PEER_HEADS = 8
PEER_N_KEYS = 128
PEER_TOPK = 16

LANES = 128
SUBLANES = 8
V7X_VMEM_LIMIT = 52 * 1024 * 1024

PROMPT_TM = 1024
SAMPLE_TM = 256
MM_TN = 1024
DIFF_TQ = 1024
DIFF_SUB = 256
GDN_CHUNKS_PER_STEP = 4


def _cparams(sem, vmem=V7X_VMEM_LIMIT):
    return pltpu.CompilerParams(dimension_semantics=sem, vmem_limit_bytes=vmem)


def _dot(a, b, precision=None):
    return jnp.dot(a, b, preferred_element_type=F32, precision=precision)


def _dot_nt(a, b, precision=None):
    return lax.dot_general(a, b, (((1,), (1,)), ((), ())), preferred_element_type=F32, precision=precision)


def _split_bf16(x):
    hi = x.astype(BF16)
    return hi, (x - hi.astype(F32)).astype(BF16)


def _cat_parts(p, q):
    return jnp.concatenate([p[0], q[0]], axis=0), jnp.concatenate([p[1], q[1]], axis=0)


def _dot3_parts(a_parts, b_parts, dot):
    a_hi, a_lo = a_parts
    b_hi, b_lo = b_parts
    m = a_hi.shape[0]
    both = dot(jnp.concatenate([a_hi, a_lo], axis=0), b_hi)
    return both[:m] + both[m:] + dot(a_hi, b_lo)


def _dot3(a, b):
    return _dot3_parts(_split_bf16(a), _split_bf16(b), _dot)


def _rms(xf, g):
    return xf * lax.rsqrt(jnp.mean(xf * xf, axis=-1, keepdims=True) + RMS_EPS) * g


def _mm_kernel(*refs, has_norm, has_res):
    it = iter(refs)
    x_ref = next(it)
    g_ref = next(it) if has_norm else None
    w_ref = next(it)
    r_ref = next(it) if has_res else None
    o_ref = next(it)
    h_scr = next(it)

    @pl.when(pl.program_id(1) == 0)
    def _():
        xf = x_ref[...].astype(F32)
        if has_norm:
            xf = _rms(xf, g_ref[...])
        h_scr[...] = xf.astype(BF16)

    acc = _dot(h_scr[...], w_ref[...].astype(BF16))
    if has_res:
        acc = acc + r_ref[...]
    o_ref[...] = acc.astype(o_ref.dtype)


def _matmul(x, w, *, norm_g=None, residual=None, tm, tn, out_dtype=F32, name="matmul"):
    m, k = x.shape
    n = w.shape[1]
    assert m % tm == 0 and n % tn == 0, (m, n, tm, tn)
    in_specs = [pl.BlockSpec((tm, k), lambda i, j: (i, 0))]
    args = [x]
    if norm_g is not None:
        in_specs.append(pl.BlockSpec((1, k), lambda i, j: (0, 0)))
        args.append(norm_g.reshape(1, k).astype(F32))
    in_specs.append(pl.BlockSpec((k, tn), lambda i, j: (0, j)))
    args.append(w)
    if residual is not None:
        in_specs.append(pl.BlockSpec((tm, tn), lambda i, j: (i, j)))
        args.append(residual)
    return pl.pallas_call(
        functools.partial(_mm_kernel, has_norm=norm_g is not None, has_res=residual is not None),
        out_shape=jax.ShapeDtypeStruct((m, n), out_dtype),
        grid=(m // tm, n // tn),
        in_specs=in_specs,
        out_specs=pl.BlockSpec((tm, tn), lambda i, j: (i, j)),
        scratch_shapes=[pltpu.VMEM((tm, k), BF16)],
        compiler_params=_cparams(("parallel", "arbitrary")),
        name=name,
    )(*args)


IN_TN = 512
IN_TM = 1024
IN_W1_TILES = 8
IN_W2_TILES = 14
IN_MAIN_TILES = 10
IN_K_TILE0 = 10
IN_V_TILE0 = 12
IN_TAIL_TILE0 = 14
IN_TILES = IN_W1_TILES + IN_W2_TILES + 1
MAIN_COLS = IN_MAIN_TILES * IN_TN
TAIL_COLS = (IN_TILES - IN_TAIL_TILE0) * IN_TN
COL_DQ_MAIN = 4096
COL_GB_TAIL = 2048
COL_BA_TAIL = 4096


def _in_proj_kernel(x_ref, g_ref, w1_ref, w2_ref, w3_ref, main_ref, k_ref, v_ref, tail_ref, h_scr):
    j = pl.program_id(1)

    @pl.when(j == 0)
    def _():
        h_scr[...] = _rms(x_ref[...], g_ref[...]).astype(BF16)

    routes = ((0, IN_W1_TILES, w1_ref, main_ref),
              (IN_W1_TILES, IN_K_TILE0, w2_ref, main_ref),
              (IN_K_TILE0, IN_V_TILE0, w2_ref, k_ref),
              (IN_V_TILE0, IN_TAIL_TILE0, w2_ref, v_ref),
              (IN_TAIL_TILE0, IN_TILES - 1, w2_ref, tail_ref),
              (IN_TILES - 1, IN_TILES, w3_ref, tail_ref))
    for lo, hi, w_ref, dst in routes:
        @pl.when(jnp.logical_and(j >= lo, j < hi))
        def _(w_ref=w_ref, dst=dst):
            dst[...] = _dot(h_scr[...], w_ref[...])


def _in_proj_weights(w_in):
    c_z = 4 * GDN_WIDTH
    c_ba = c_z + 2 * GDN_HEADS
    assert c_z == IN_W1_TILES * IN_TN and w_in.shape[1] - c_ba == IN_W2_TILES * IN_TN
    w_bf16 = w_in.astype(BF16)
    return w_bf16, w_bf16[:, c_ba:]


def _in_proj(x, norm_g, w_pieces, *, tm):
    m, d = x.shape
    w_all, w2 = w_pieces
    w1 = w3 = w_all
    tn = IN_TN
    out = lambda cols: jax.ShapeDtypeStruct((m, cols), F32)
    return pl.pallas_call(
        _in_proj_kernel,
        out_shape=(out(MAIN_COLS), out(2 * IN_TN), out(2 * IN_TN), out(TAIL_COLS)),
        grid=(m // tm, IN_TILES),
        in_specs=[
            pl.BlockSpec((tm, d), lambda i, j: (i, 0)),
            pl.BlockSpec((1, d), lambda i, j: (0, 0)),
            pl.BlockSpec((d, tn), lambda i, j: (0, jnp.minimum(j, IN_W1_TILES - 1))),
            pl.BlockSpec((d, tn), lambda i, j: (0, jnp.clip(j - IN_W1_TILES, 0, IN_W2_TILES - 1))),
            pl.BlockSpec((d, tn), lambda i, j: (0, IN_W1_TILES)),
        ],
        out_specs=(
            pl.BlockSpec((tm, tn), lambda i, j: (i, jnp.minimum(j, IN_MAIN_TILES - 1))),
            pl.BlockSpec((tm, tn), lambda i, j: (i, jnp.clip(j - IN_K_TILE0, 0, 1))),
            pl.BlockSpec((tm, tn), lambda i, j: (i, jnp.clip(j - IN_V_TILE0, 0, 1))),
            pl.BlockSpec((tm, tn), lambda i, j: (i, jnp.clip(j - IN_TAIL_TILE0, 0, IN_TILES - IN_TAIL_TILE0 - 1))),
        ),
        scratch_shapes=[pltpu.VMEM((tm, d), BF16)],
        compiler_params=_cparams(("parallel", "arbitrary")),
        name="in_proj",
    )(x, norm_g.reshape(1, d).astype(F32), w1, w2, w3)


def _gdn_kernel(qkv_ref, z_ref, ba_ref, prev_ref, s0_ref, convw_ref, alog_ref, dtb_ref, ng_ref,
                o_ref, sfin_ref, s_scr, xbuf, *, L, nch):
    step = pl.program_id(1)
    rows = nch * L

    @pl.when(step == 0)
    def _():
        s_scr[...] = s0_ref[0]
        xbuf[0:SUBLANES, :] = prev_ref[0]

    xbuf[SUBLANES:SUBLANES + rows, :] = qkv_ref[...]
    base = SUBLANES - (GDN_CONV - 1)
    y = xbuf[base:base + rows, :] * convw_ref[0:1, :]
    for i in range(1, GDN_CONV):
        y = y + xbuf[base + i:base + i + rows, :] * convw_ref[i:i + 1, :]
    y = y * jax.nn.sigmoid(y)
    carry = xbuf[rows:rows + SUBLANES, :]
    xbuf[0:SUBLANES, :] = carry

    row = lax.broadcasted_iota(jnp.int32, (L, L), 0)
    col = lax.broadcasted_iota(jnp.int32, (L, L), 1)
    lower = row >= col
    strict = row > col
    eye = (row == col).astype(F32)
    log_l = int(math.log2(L))
    n_double = log_l - 1
    assert 2 ** log_l == L

    ba = ba_ref[...]
    beta_all = jax.nn.sigmoid(ba)
    a_in = ba + dtb_ref[...]
    softplus = jnp.maximum(a_in, 0.0) + jnp.log(1.0 + jnp.exp(-jnp.abs(a_in)))
    g_all = -jnp.exp(alog_ref[...]) * softplus
    rr = lax.broadcasted_iota(jnp.int32, (rows, rows), 0)
    cc = lax.broadcasted_iota(jnp.int32, (rows, rows), 1)
    same_chunk_lower = jnp.logical_and(rr >= cc, jnp.right_shift(rr, log_l) == jnp.right_shift(cc, log_l))
    g_cum = _dot(same_chunk_lower.astype(F32), g_all, HIGHEST)
    g_cum_t = g_cum.T

    heads = range(GDN_HEADS)
    chains = [(c, h) for c in range(nch) for h in heads]
    ids = range(len(chains))

    def tok(c):
        return slice(c * L, (c + 1) * L)

    def head_cols(part, c, h):
        return y[tok(c), part * GDN_WIDTH + h * HEAD_DIM:part * GDN_WIDTH + (h + 1) * HEAD_DIM]

    q = [head_cols(0, c, h) for c, h in chains]
    k = [head_cols(1, c, h) for c, h in chains]
    v = [head_cols(2, c, h) for c, h in chains]
    q = [a * lax.rsqrt(jnp.sum(a * a, axis=-1, keepdims=True) + 1e-6) * (HEAD_DIM ** -0.5) for a in q]
    k = [a * lax.rsqrt(jnp.sum(a * a, axis=-1, keepdims=True) + 1e-6) for a in k]
    beta = [beta_all[tok(c), h:h + 1] for c, h in chains]
    g_col = [g_cum[tok(c), GDN_HEADS + h:GDN_HEADS + h + 1] for c, h in chains]
    g_row = [g_cum_t[GDN_HEADS + h:GDN_HEADS + h + 1, tok(c)] for c, h in chains]
    decay = [jnp.where(lower, jnp.exp(jnp.where(lower, g_col[n] - g_row[n], 0.0)), 0.0) for n in ids]
    g_last = [g_col[n][L - 1:L, :] for n in ids]
    e_col = [jnp.exp(g_col[n]) for n in ids]
    qk_kk = [_dot_nt(jnp.concatenate([q[n], k[n]], axis=0).astype(BF16), k[n].astype(BF16)) for n in ids]
    x = [-jnp.where(strict, beta[n] * qk_kk[n][L:] * decay[n], 0.0) for n in ids]
    t_inv = [eye + x[n] for n in ids]
    x_parts = [_split_bf16(x[n]) for n in ids]
    xp = [_dot3_parts(x_parts[n], x_parts[n], _dot) for n in ids]
    for _ in range(n_double - 1):
        xp_parts = [_split_bf16(xp[n]) for n in ids]
        both = [_dot3_parts(_cat_parts(_split_bf16(t_inv[n]), xp_parts[n]), xp_parts[n], _dot) for n in ids]
        t_inv = [t_inv[n] + both[n][:L] for n in ids]
        xp = [both[n][L:] for n in ids]
    t_inv = [t_inv[n] + _dot3(t_inv[n], xp[n]) for n in ids]
    sol = [_dot3(t_inv[n], jnp.concatenate([v[n] * beta[n], k[n] * (beta[n] * e_col[n])], axis=-1)) for n in ids]
    qk = [jnp.where(lower, qk_kk[n][:L] * decay[n], 0.0) for n in ids]
    k_dec_t = [(k[n] * jnp.exp(g_last[n] - g_col[n])).T for n in ids]
    left_s = [jnp.concatenate([sol[n][:, HEAD_DIM:], q[n] * e_col[n]], axis=0).astype(BF16) for n in ids]
    left_v = [jnp.concatenate([qk[n], k_dec_t[n]], axis=0).astype(BF16) for n in ids]

    s = [s_scr[h] for h in heads]
    for c in range(nch):
        n0 = c * GDN_HEADS
        ws_qs = [_dot(left_s[n0 + h], s[h].astype(BF16)) for h in heads]
        v_new = [sol[n0 + h][:, :HEAD_DIM] - ws_qs[h][:L] for h in heads]
        tail = [_dot(left_v[n0 + h], v_new[h].astype(BF16)) for h in heads]
        s = [s[h] * jnp.exp(g_last[n0 + h]) + tail[h][L:] for h in heads]
        for h in heads:
            zf = z_ref[tok(c), h * HEAD_DIM:(h + 1) * HEAD_DIM]
            o = _rms(ws_qs[h][L:] + tail[h][:L], ng_ref[...]) * (zf * jax.nn.sigmoid(zf))
            o_ref[tok(c), h * HEAD_DIM:(h + 1) * HEAD_DIM] = o.astype(o_ref.dtype)
    for h in heads:
        s_scr[h] = s[h]

    @pl.when(step == pl.num_programs(1) - 1)
    def _():
        sfin_ref[0] = s_scr[...]


def _decay_lanes(p):
    return jnp.zeros((1, LANES), F32).at[0, GDN_HEADS:2 * GDN_HEADS].set(p.astype(F32))


def _gdn(main, tail, conv_prev8, s0, conv_w, a_log, dt_bias, norm_g, *, bsz, t):
    L = min(CHUNK, t)
    nch = GDN_CHUNKS_PER_STEP if (t // L) % GDN_CHUNKS_PER_STEP == 0 else 1
    nc = t // (L * nch)
    rows = L * nch
    conv_dim = 3 * GDN_WIDTH
    kern = functools.partial(_gdn_kernel, L=L, nch=nch)
    o, s_fin = pl.pallas_call(
        kern,
        out_shape=(jax.ShapeDtypeStruct((bsz * t, GDN_WIDTH), BF16),
                   jax.ShapeDtypeStruct((bsz, GDN_HEADS, HEAD_DIM, HEAD_DIM), F32)),
        grid=(bsz, nc),
        in_specs=[
            pl.BlockSpec((rows, conv_dim), lambda b, c: (b * nc + c, 0)),
            pl.BlockSpec((rows, GDN_WIDTH), lambda b, c: (b * nc + c, conv_dim // GDN_WIDTH)),
            pl.BlockSpec((rows, LANES), lambda b, c: (b * nc + c, COL_BA_TAIL // LANES)),
            pl.BlockSpec((1, SUBLANES, conv_dim), lambda b, c: (b, 0, 0)),
            pl.BlockSpec((1, GDN_HEADS, HEAD_DIM, HEAD_DIM), lambda b, c: (b, 0, 0, 0)),
            pl.BlockSpec((GDN_CONV, conv_dim), lambda b, c: (0, 0)),
            pl.BlockSpec((1, LANES), lambda b, c: (0, 0)),
            pl.BlockSpec((1, LANES), lambda b, c: (0, 0)),
            pl.BlockSpec((1, HEAD_DIM), lambda b, c: (0, 0)),
        ],
        out_specs=(pl.BlockSpec((rows, GDN_WIDTH), lambda b, c: (b * nc + c, 0)),
                   pl.BlockSpec((1, GDN_HEADS, HEAD_DIM, HEAD_DIM), lambda b, c: (b, 0, 0, 0))),
        scratch_shapes=[pltpu.VMEM((GDN_HEADS, HEAD_DIM, HEAD_DIM), F32),
                        pltpu.VMEM((SUBLANES + rows + SUBLANES, conv_dim), F32)],
        compiler_params=_cparams(("parallel", "arbitrary")),
        name="gdn",
    )(main, main, tail, conv_prev8, s0, conv_w, _decay_lanes(a_log), _decay_lanes(dt_bias),
      norm_g.reshape(1, -1))
    return o, s_fin


def _lambda_value(lam_refs, lam_init):
    lq1, lk1, lq2, lk2 = (r[...] for r in lam_refs)
    return (jnp.exp(jnp.sum(lq1 * lk1, axis=-1, keepdims=True))
            - jnp.exp(jnp.sum(lq2 * lk2, axis=-1, keepdims=True)) + lam_init)


def _softmax_update(m_scr, l_scr, acc_scr, scores, values):
    idx = range(len(scores))
    m_old = [m_scr[n] for n in idx]
    m_new = [jnp.maximum(m_old[n], jnp.max(scores[n], axis=-1, keepdims=True)) for n in idx]
    p = [jnp.exp(scores[n] - m_new[n]) for n in idx]
    alpha = [jnp.exp(m_old[n] - m_new[n]) for n in idx]
    pv = [_dot(p[n].astype(BF16), values[n]) for n in idx]
    for n in idx:
        l_scr[n] = alpha[n] * l_scr[n] + jnp.sum(p[n], axis=-1, keepdims=True)
        acc_scr[n] = alpha[n] * acc_scr[n] + pv[n]
        m_scr[n] = m_new[n]


def _diff_finalize(o1, o2, lam_refs, g_ref, lam_init):
    lam = _lambda_value(lam_refs, lam_init)
    return _rms(o1 - lam * o2, g_ref[...]) * (1.0 - lam_init)


def _diff_init(m_scr, l_scr, acc_scr):
    m_scr[...] = jnp.full(m_scr.shape, -jnp.inf, F32)
    l_scr[...] = jnp.zeros(l_scr.shape, F32)
    acc_scr[...] = jnp.zeros(acc_scr.shape, F32)


def _near_bias(slope, r, q_pos, k_pos):
    visible = jnp.right_shift(k_pos, CHUNK_SHIFT) <= jnp.right_shift(q_pos, CHUNK_SHIFT)
    return jnp.where(visible, slope * (r - jnp.abs(q_pos - k_pos)).astype(F32), -jnp.inf)


def _split_distance(slope, d):
    return (-(slope * CHUNK) * jnp.right_shift(d, CHUNK_SHIFT).astype(F32),
            -slope * jnp.bitwise_and(d, CHUNK - 1).astype(F32))


def _diff_prompt_kernel(i_tab, j_tab, slopes_ref, q_ref, k_ref, v_ref, lq1, lk1, lq2, lk2, g_ref, o_ref,
                        m_scr, l_scr, acc_scr, lhs_scr, s_even, s_odd, *, tq, lam_init):
    h = pl.program_id(1)
    s_id = pl.program_id(2)
    n_pairs = pl.num_programs(2) - 1
    score_on = s_id < n_pairs
    consume_on = s_id >= 1
    i = i_tab[s_id]
    j = j_tab[s_id]
    ci = i_tab[jnp.maximum(s_id - 1, 0)]
    cj = j_tab[jnp.maximum(s_id - 1, 0)]
    slope = slopes_ref[h]
    half = DIFF_HEAD_DIM
    lane = lax.broadcasted_iota(jnp.int32, (tq, LANES), 1)

    @pl.when(jnp.logical_and(consume_on, cj == 0))
    def _():
        _diff_init(m_scr, l_scr, acc_scr)

    @pl.when(jnp.logical_and(score_on, j == 0))
    def _():
        qs = q_ref[...] * (DIFF_HEAD_DIM ** -0.5)
        lhs_scr[0] = jnp.where(lane < half, qs, 1.0).astype(BF16)
        lhs_scr[1] = jnp.where(lane >= half, qs, 1.0).astype(BF16)

    sub = min(DIFF_SUB, tq)
    units = [(n, qb) for qb in range(tq // sub) for n in range(2)]

    def score_tile(near, dst):
        kb = k_ref[...]
        if near:
            aug0 = aug1 = jnp.zeros_like(kb)
        else:
            d = (i - j) * tq - lax.broadcasted_iota(jnp.int32, (tq, LANES), 0)
            t_hi, t_lo = _split_distance(slope, d)
            aug0 = jnp.where(lane == half, t_hi, jnp.where(lane == half + 1, t_lo, 0.0))
            aug1 = jnp.where(lane == 0, t_hi, jnp.where(lane == 1, t_lo, 0.0))
        keys = [jnp.where(lane < half, kb, aug0).astype(BF16), jnp.where(lane >= half, kb, aug1).astype(BF16)]
        bias = None
        for n, qb in units:
            cols = slice(qb * sub, (qb + 1) * sub)
            if near:
                live = (qb + 1) * sub
                if n == 0:
                    k_pos = lax.broadcasted_iota(jnp.int32, (live, sub), 0)
                    q_pos = qb * sub + lax.broadcasted_iota(jnp.int32, (live, sub), 1)
                    bias = _near_bias(slope, q_pos, q_pos, k_pos)
                dst[n, :live, cols] = _dot_nt(keys[n][:live], lhs_scr[n, cols, :]) + bias
            else:
                dst[n, :, cols] = _dot_nt(keys[n], lhs_scr[n, cols, :])

    def consume_tile(near, src):
        vt = v_ref[...].T.astype(BF16)
        idx = range(len(units))
        cols = [slice(qb * sub, (qb + 1) * sub) for _, qb in units]
        live = [(qb + 1) * sub if near else tq for _, qb in units]
        s = [src[n, :live[u], cols[u]] for u, (n, _) in enumerate(units)]
        m_old = [m_scr[n, :, cols[u]] for u, (n, _) in enumerate(units)]
        m_new = [jnp.maximum(m_old[u], jnp.max(s[u], axis=0, keepdims=True)) for u in idx]
        p = [jnp.exp(s[u] - m_new[u]) for u in idx]
        alpha = [jnp.exp(m_old[u] - m_new[u]) for u in idx]
        pv = [_dot(vt[:, :live[u]], p[u].astype(BF16)) for u in idx]
        for u, (n, _) in enumerate(units):
            l_scr[n, :, cols[u]] = alpha[u] * l_scr[n, :, cols[u]] + jnp.sum(p[u], axis=0, keepdims=True)
            acc_scr[n, :, cols[u]] = alpha[u] * acc_scr[n, :, cols[u]] + pv[u]
            m_scr[n, :, cols[u]] = m_new[u]

    def stage(cond, score_near=None, consume_near=None):
        for parity, (dst, src) in enumerate(((s_even, s_odd), (s_odd, s_even))):
            @pl.when(jnp.logical_and(cond, s_id % 2 == parity))
            def _(dst=dst, src=src):
                if score_near is not None:
                    score_tile(score_near, dst)
                if consume_near is not None:
                    consume_tile(consume_near, src)

    both = jnp.logical_and(score_on, consume_on)
    stage(jnp.logical_and(both, jnp.logical_and(j < i, cj < ci)), score_near=False, consume_near=False)
    stage(jnp.logical_and(both, jnp.logical_and(j < i, cj == ci)), score_near=False, consume_near=True)
    stage(jnp.logical_and(both, j == i), score_near=True, consume_near=False)
    stage(jnp.logical_not(consume_on), score_near=True)
    stage(jnp.logical_not(score_on), consume_near=True)

    @pl.when(jnp.logical_and(consume_on, cj == ci))
    def _():
        o_ref[...] = _diff_finalize((acc_scr[0] / l_scr[0]).T, (acc_scr[1] / l_scr[1]).T,
                                    (lq1, lk1, lq2, lk2), g_ref, lam_init).astype(o_ref.dtype)


def _alibi_slopes():
    return jnp.asarray([2.0 ** (-8.0 * (i + 1) / DIFF_HEADS) for i in range(DIFF_HEADS)], F32)


def _diff_prompt(main, k_new, v_new, lams, subln_g, *, bsz, t, lam_init):
    tq = min(DIFF_TQ, t)
    nq = t // tq
    assert tq % CHUNK == 0 and t <= CHUNK * 255
    kern = functools.partial(_diff_prompt_kernel, tq=tq, lam_init=lam_init)
    qc, kc, vc = COL_DQ_MAIN // LANES, 0, 0
    pairs = [(i, j) for i in range(nq) for j in range(i + 1)]
    pairs.append(pairs[-1])
    i_tab = jnp.asarray([p[0] for p in pairs], jnp.int32)
    j_tab = jnp.asarray([p[1] for p in pairs], jnp.int32)
    zero = lambda b, h, s, it, jt: (0, 0)

    def consumed(tab, s):
        return tab[jnp.maximum(s - 1, 0)]

    grid_spec = pltpu.PrefetchScalarGridSpec(
        num_scalar_prefetch=2,
        grid=(bsz, DIFF_HEADS, len(pairs)),
        in_specs=[
            pl.BlockSpec(memory_space=pltpu.SMEM),
            pl.BlockSpec((tq, LANES), lambda b, h, s, it, jt: (b * nq + it[s], qc + h)),
            pl.BlockSpec((tq, LANES), lambda b, h, s, it, jt: (b * nq + jt[s], kc + h)),
            pl.BlockSpec((tq, LANES), lambda b, h, s, it, jt: (b * nq + consumed(jt, s), vc + h)),
        ] + [pl.BlockSpec((1, DIFF_HEAD_DIM), zero)] * 4 + [pl.BlockSpec((1, HEAD_DIM), zero)],
        out_specs=pl.BlockSpec((tq, LANES), lambda b, h, s, it, jt: (b * nq + consumed(it, s), h)),
        scratch_shapes=[pltpu.VMEM((2, 1, tq), F32), pltpu.VMEM((2, 1, tq), F32),
                        pltpu.VMEM((2, HEAD_DIM, tq), F32), pltpu.VMEM((2, tq, LANES), BF16),
                        pltpu.VMEM((2, tq, tq), F32), pltpu.VMEM((2, tq, tq), F32)],
    )
    return pl.pallas_call(
        kern,
        out_shape=jax.ShapeDtypeStruct((bsz * t, DIFF_HEADS * HEAD_DIM), BF16),
        grid_spec=grid_spec,
        compiler_params=_cparams(("parallel", "parallel", "arbitrary")),
        name="diff_attn_prompt",
    )(i_tab, j_tab, _alibi_slopes(), main, k_new, v_new, *lams, subln_g.reshape(1, -1))


def _diff_sample_kernel(slopes_ref, q_ref, kn_ref, vn_ref, kp_ref, vp_ref, lq1, lk1, lq2, lk2, g_ref, o_ref,
                        m_scr, l_scr, acc_scr, lhs_scr, *, t, tk, past, lam_init):
    j = pl.program_id(1)
    heads = range(DIFF_HEADS)

    def cols(h):
        return slice(h * HEAD_DIM, (h + 1) * HEAD_DIM)

    @pl.when(j == 0)
    def _():
        _diff_init(m_scr, l_scr, acc_scr)
        lane = lax.broadcasted_iota(jnp.int32, (t, LANES), 1)
        r = lax.broadcasted_iota(jnp.int32, (t, t), 0)
        c = lax.broadcasted_iota(jnp.int32, (t, t), 1)
        scores = []
        for h in heads:
            qs = q_ref[:, cols(h)] * (DIFF_HEAD_DIM ** -0.5)
            lhs_scr[h, 0:t, :] = jnp.where(lane < DIFF_HEAD_DIM, qs, 0.0).astype(BF16)
            lhs_scr[h, t:2 * t, :] = jnp.where(lane >= DIFF_HEAD_DIM, qs, 0.0).astype(BF16)
            bias = _near_bias(slopes_ref[h], r, past + r, past + c)
            scores.append(_dot_nt(lhs_scr[h], kn_ref[:, cols(h)].astype(BF16)) + jnp.concatenate([bias, bias], axis=0))
        _softmax_update(m_scr, l_scr, acc_scr, scores, [vn_ref[:, cols(h)].astype(BF16) for h in heads])

    d = (past - j * tk - lax.broadcasted_iota(jnp.int32, (1, tk), 1)).astype(F32)
    kt = jnp.swapaxes(kp_ref[...], 0, 1).astype(BF16)
    vt = jnp.swapaxes(vp_ref[...], 0, 1).astype(BF16)
    scores = [_dot_nt(lhs_scr[h], kt[h]) - slopes_ref[h] * d for h in heads]
    _softmax_update(m_scr, l_scr, acc_scr, scores, [vt[h] for h in heads])

    @pl.when(j == pl.num_programs(1) - 1)
    def _():
        for h in heads:
            o = acc_scr[h] / l_scr[h]
            o_ref[:, cols(h)] = _diff_finalize(o[:t], o[t:], (lq1, lk1, lq2, lk2), g_ref, lam_init).astype(o_ref.dtype)


def _diff_sample(main, k_new, v_new, k_past, v_past, lams, subln_g, *, bsz, t, lam_init):
    past = k_past.shape[1]
    tk = min(1024, past)
    nk = past // tk
    width = DIFF_HEADS * HEAD_DIM
    kern = functools.partial(_diff_sample_kernel, t=t, tk=tk, past=past, lam_init=lam_init)
    zero = lambda b, j: (0, 0)
    cache = pl.BlockSpec((None, tk, DIFF_HEADS, HEAD_DIM), lambda b, j: (b, j, 0, 0))
    return pl.pallas_call(
        kern,
        out_shape=jax.ShapeDtypeStruct((bsz * t, width), BF16),
        grid=(bsz, nk),
        in_specs=[
            pl.BlockSpec(memory_space=pltpu.SMEM),
            pl.BlockSpec((t, width), lambda b, j: (b, COL_DQ_MAIN // width)),
            pl.BlockSpec((t, width), lambda b, j: (b, 0)),
            pl.BlockSpec((t, width), lambda b, j: (b, 0)),
            cache, cache,
        ] + [pl.BlockSpec((1, DIFF_HEAD_DIM), zero)] * 4 + [pl.BlockSpec((1, HEAD_DIM), zero)],
        out_specs=pl.BlockSpec((t, width), lambda b, j: (b, 0)),
        scratch_shapes=[pltpu.VMEM((DIFF_HEADS, 2 * t, 1), F32), pltpu.VMEM((DIFF_HEADS, 2 * t, 1), F32),
                        pltpu.VMEM((DIFF_HEADS, 2 * t, HEAD_DIM), F32), pltpu.VMEM((DIFF_HEADS, 2 * t, LANES), BF16)],
        compiler_params=_cparams(("parallel", "arbitrary")),
        name="diff_attn_sample",
    )(_alibi_slopes(), main, k_new, v_new, k_past, v_past, *lams, subln_g.reshape(1, -1))


def _merge_kernel(oa_ref, ob_ref, wa_ref, wb_ref, ga_ref, gb_ref, o_ref):
    ya = _dot(oa_ref[...], wa_ref[...])
    yb = _dot(ob_ref[...], wb_ref[...])
    o_ref[...] = (jax.nn.sigmoid(ga_ref[...]) * ya + jax.nn.sigmoid(gb_ref[...]) * yb).astype(o_ref.dtype)


def _merge(o_a, o_b, wa, wb, tail, *, tm):
    m = o_a.shape[0]
    d = wa.shape[1]
    tn = MM_TN
    ga0, gb0 = 0, COL_GB_TAIL // tn
    return pl.pallas_call(
        _merge_kernel,
        out_shape=jax.ShapeDtypeStruct((m, d), BF16),
        grid=(m // tm, d // tn),
        in_specs=[
            pl.BlockSpec((tm, o_a.shape[1]), lambda i, j: (i, 0)),
            pl.BlockSpec((tm, o_b.shape[1]), lambda i, j: (i, 0)),
            pl.BlockSpec((wa.shape[0], tn), lambda i, j: (0, j)),
            pl.BlockSpec((wb.shape[0], tn), lambda i, j: (0, j)),
            pl.BlockSpec((tm, tn), lambda i, j: (i, ga0 + j)),
            pl.BlockSpec((tm, tn), lambda i, j: (i, gb0 + j)),
        ],
        out_specs=pl.BlockSpec((tm, tn), lambda i, j: (i, j)),
        compiler_params=_cparams(("parallel", "arbitrary")),
        name="merge",
    )(o_a, o_b, wa, wb, tail, tail)


def _cross_kernel(q_ref, k_ref, v_ref, o_ref):
    dh = q_ref.shape[-1] // MEM_HEADS
    for h in range(MEM_HEADS):
        sl = slice(h * dh, (h + 1) * dh)
        s = _dot_nt(q_ref[:, sl], k_ref[0, :, sl].astype(BF16)) * (dh ** -0.5)
        s = s - jnp.max(s, axis=-1, keepdims=True)
        p = jnp.exp(s)
        p = p / jnp.sum(p, axis=-1, keepdims=True)
        o_ref[:, sl] = _dot(p.astype(BF16), v_ref[0, :, sl].astype(BF16)).astype(o_ref.dtype)


def _cross_out_kernel(q_ref, k_ref, v_ref, w_ref, r_ref, o_ref, oc_scr):
    j = pl.program_id(2)

    @pl.when(j == 0)
    def _():
        _cross_kernel(q_ref, k_ref, v_ref, oc_scr)

    @pl.when(j >= 1)
    def _():
        o_ref[...] = r_ref[...] + _dot(oc_scr[...], w_ref[...])


def _cross_attend_project(q, mem_k, mem_v, w_out, residual, *, bsz, t, tm, tn):
    d = q.shape[1]
    nt = t // tm
    mt = mem_k.shape[1]
    col = lambda j: jnp.maximum(j - 1, 0)
    return pl.pallas_call(
        _cross_out_kernel,
        out_shape=jax.ShapeDtypeStruct(q.shape, F32),
        grid=(bsz, nt, 1 + d // tn),
        in_specs=[
            pl.BlockSpec((tm, d), lambda b, i, j: (b * nt + i, 0)),
            pl.BlockSpec((1, mt, d), lambda b, i, j: (b, 0, 0)),
            pl.BlockSpec((1, mt, d), lambda b, i, j: (b, 0, 0)),
            pl.BlockSpec((d, tn), lambda b, i, j: (0, col(j))),
            pl.BlockSpec((tm, tn), lambda b, i, j: (b * nt + i, col(j))),
        ],
        out_specs=pl.BlockSpec((tm, tn), lambda b, i, j: (b * nt + i, col(j))),
        scratch_shapes=[pltpu.VMEM((tm, d), BF16)],
        compiler_params=_cparams(("parallel", "parallel", "arbitrary")),
        name="cross_attn_out",
    )(q, mem_k, mem_v, w_out, residual)


def _cross_attend(q, mem_k, mem_v, *, bsz, t, tm):
    d = q.shape[1]
    nt = t // tm
    mt = mem_k.shape[1]
    return pl.pallas_call(
        _cross_kernel,
        out_shape=jax.ShapeDtypeStruct(q.shape, BF16),
        grid=(bsz, nt),
        in_specs=[
            pl.BlockSpec((tm, d), lambda b, i: (b * nt + i, 0)),
            pl.BlockSpec((1, mt, d), lambda b, i: (b, 0, 0)),
            pl.BlockSpec((1, mt, d), lambda b, i: (b, 0, 0)),
        ],
        out_specs=pl.BlockSpec((tm, d), lambda b, i: (b * nt + i, 0)),
        compiler_params=_cparams(("parallel", "arbitrary")),
        name="cross_attn",
    )(q, mem_k, mem_v)


def _topk_rows(work, cidx, k):
    n, tn = work.shape
    rid = lax.broadcasted_iota(jnp.int32, (n, tn), 0).astype(F32)
    kid = lax.broadcasted_iota(jnp.int32, (k, tn), 0)
    vals = jnp.zeros((k, tn), F32)
    idxs = jnp.zeros((k, tn), F32)
    for t in range(k):
        m = jnp.max(work, axis=0, keepdims=True)
        pos = jnp.min(jnp.where(work == m, rid, float(n)), axis=0, keepdims=True)
        hit = rid == pos
        if cidx is None:
            picked = pos
        else:
            picked = jnp.sum(jnp.where(hit, cidx, 0.0), axis=0, keepdims=True)
        vals = jnp.where(kid == t, m, vals)
        idxs = jnp.where(kid == t, picked, idxs)
        work = jnp.where(hit, -jnp.inf, work)
    return vals, idxs


def _product_candidates(v1, i1, v2, i2, k):
    assert k == 2 * SUBLANES
    sub = lax.broadcasted_iota(jnp.int32, (SUBLANES, v1.shape[1]), 0)
    vals, idxs = [], []
    for a in range(k // 2):
        nb = k // (a + 1)
        width = k if nb > SUBLANES else SUBLANES
        cv = v1[a:a + 1, :] + v2[:width, :]
        ci = i1[a:a + 1, :] * PEER_N_KEYS + i2[:width, :]
        if nb < SUBLANES:
            cv = jnp.where(sub < nb, cv, -jnp.inf)
        vals.append(cv)
        idxs.append(ci)
    vals.append(v1[k // 2:, :] + v2[0:1, :])
    idxs.append(i1[k // 2:, :] * PEER_N_KEYS + i2[0:1, :])
    return jnp.concatenate(vals, axis=0), jnp.concatenate(idxs, axis=0)


def _peer_route_kernel(q_ref, keys_hi_ref, keys_lo_ref, e_ref, g_ref):
    k = PEER_TOPK
    for h in range(PEER_HEADS):
        sub = []
        for p in range(2):
            c0 = (h * 2 + p) * PEER_N_KEYS
            q_parts = _split_bf16(q_ref[:, c0:c0 + PEER_N_KEYS])
            s = _dot3_parts((keys_hi_ref[h, p], keys_lo_ref[h, p]), q_parts, _dot_nt)
            sub.append(_topk_rows(s, None, k))
        (v1, i1), (v2, i2) = sub
        cand, cidx = _product_candidates(v1, i1, v2, i2, k)
        best, eidx = _topk_rows(cand, cidx, k)
        ex = jnp.exp(best - jnp.max(best, axis=0, keepdims=True))
        gate = ex / jnp.sum(ex, axis=0, keepdims=True)
        e_ref[h * k:(h + 1) * k, :] = eidx.astype(jnp.int32)
        g_ref[h * k:(h + 1) * k, :] = gate


PEER_GATE_TN = 256


def _peer_gate_kernel(et_ref, gt_ref, o_ref, e_scr, g_scr):
    tg = et_ref.shape[1]
    e_scr[...] = et_ref[...].T
    g_scr[...] = gt_ref[...].T
    kid = lax.broadcasted_iota(jnp.int32, (PEER_N_KEYS, PEER_N_KEYS), 0)
    group = 2 * SUBLANES

    def body(n, carry):
        rows = pl.ds(pl.multiple_of(n * group, group), group)
        e16 = e_scr[rows, :]
        g16 = g_scr[rows, :]
        a16 = jnp.right_shift(e16, 7)
        b16 = jnp.bitwise_and(e16, PEER_N_KEYS - 1)
        toks = range(group)
        at = [jnp.where(a16[s:s + 1, :] == kid, g16[s:s + 1, :], 0.0).astype(BF16) for s in toks]
        bt = [jnp.where(b16[s:s + 1, :] == kid, 1.0, 0.0).astype(BF16) for s in toks]
        grids = [_dot_nt(at[s], bt[s]) for s in toks]
        o_ref[:, rows, :] = jnp.swapaxes(jnp.stack(grids, axis=0), 0, 1).astype(BF16)
        return carry

    lax.fori_loop(0, tg // group, body, 0, unroll=8)


def _peer_route(qp, sub_keys, *, tn):
    n = qp.shape[0]
    rows = PEER_HEADS * PEER_TOPK
    keys_hi, keys_lo = _split_bf16(sub_keys.astype(F32))
    keys_spec = pl.BlockSpec(sub_keys.shape, lambda i: (0, 0, 0, 0))
    e_t, g_t = pl.pallas_call(
        _peer_route_kernel,
        out_shape=(jax.ShapeDtypeStruct((rows, n), jnp.int32), jax.ShapeDtypeStruct((rows, n), F32)),
        grid=(n // tn,),
        in_specs=[pl.BlockSpec((tn, qp.shape[1]), lambda i: (i, 0)), keys_spec, keys_spec],
        out_specs=(pl.BlockSpec((rows, tn), lambda i: (0, i)), pl.BlockSpec((rows, tn), lambda i: (0, i))),
        compiler_params=_cparams(("parallel",)),
        name="peer_route",
    )(qp, keys_hi, keys_lo)
    tg = min(PEER_GATE_TN, n)
    return pl.pallas_call(
        _peer_gate_kernel,
        out_shape=jax.ShapeDtypeStruct((PEER_N_KEYS, n, PEER_N_KEYS), BF16),
        grid=(n // tg,),
        in_specs=[pl.BlockSpec((rows, tg), lambda i: (0, i)), pl.BlockSpec((rows, tg), lambda i: (0, i))],
        out_specs=pl.BlockSpec((PEER_N_KEYS, tg, PEER_N_KEYS), lambda i: (0, i, 0)),
        scratch_shapes=[pltpu.VMEM((tg, rows), jnp.int32), pltpu.VMEM((tg, rows), F32)],
        compiler_params=_cparams(("parallel",)),
        name="peer_gates",
    )(e_t, g_t)


def _gelu_tanh(x):
    return 0.5 * x * (1.0 + jnp.tanh(math.sqrt(2.0 / math.pi) * (x + 0.044715 * (x * x * x))))


PEER_BLOCKS_PER_STEP = 4
PEER_TN = 1024
PEER_VMEM_LIMIT = 60 * 1024 * 1024


def _peer_expert_kernel(x_ref, gn_ref, gate_ref, u_ref, v_ref, gf_ref, o_ref, h_scr, acc_scr, coef_scr):
    s = pl.program_id(1)
    last = pl.num_programs(1) - 1

    @pl.when(s == 0)
    def _():
        h_scr[...] = _rms(x_ref[...], gn_ref[...]).astype(BF16)
        acc_scr[...] = jnp.zeros(acc_scr.shape, F32)
        coef_scr[...] = jnp.zeros(coef_scr.shape, BF16)

    acc_scr[...] += _dot(coef_scr[(s + 1) % 2], v_ref[...].astype(BF16))

    gate = jnp.concatenate([gate_ref[q].astype(F32) for q in range(gate_ref.shape[0])], axis=1)
    hid = _gelu_tanh(_dot_nt(h_scr[...], u_ref[...].astype(BF16)))
    coef_scr[s % 2] = (gate * hid).astype(BF16)

    @pl.when(s == last)
    def _():
        o_ref[...] = _rms(x_ref[...] + acc_scr[...], gf_ref[...])


def _peer_experts(x, norm_g, gates, u_tab, v_tab, final_g, *, tn):
    n, d = x.shape
    eb = PEER_N_KEYS
    assert u_tab.shape[0] == eb * eb
    nblk = PEER_BLOCKS_PER_STEP
    ns = eb // nblk

    def cur(s):
        return jnp.minimum(s, ns - 1)

    def prev(s):
        return jnp.maximum(s - 1, 0)

    return pl.pallas_call(
        _peer_expert_kernel,
        out_shape=jax.ShapeDtypeStruct((n, d), F32),
        grid=(n // tn, ns + 1),
        in_specs=[
            pl.BlockSpec((tn, d), lambda i, s: (i, 0), pipeline_mode=pl.Buffered(1)),
            pl.BlockSpec((1, d), lambda i, s: (0, 0)),
            pl.BlockSpec((nblk, tn, eb), lambda i, s: (cur(s), i, 0)),
            pl.BlockSpec((nblk * eb, d), lambda i, s: (cur(s), 0)),
            pl.BlockSpec((nblk * eb, d), lambda i, s: (prev(s), 0)),
            pl.BlockSpec((1, d), lambda i, s: (0, 0)),
        ],
        out_specs=pl.BlockSpec((tn, d), lambda i, s: (i, 0), pipeline_mode=pl.Buffered(1)),
        scratch_shapes=[pltpu.VMEM((tn, d), BF16), pltpu.VMEM((tn, d), F32),
                        pltpu.VMEM((2, tn, nblk * eb), BF16)],
        compiler_params=_cparams(("parallel", "arbitrary"), vmem=PEER_VMEM_LIMIT),
        name="peer_experts",
    )(x, norm_g.reshape(1, d), gates, u_tab, v_tab, final_g.reshape(1, d))


def _layer(x, wts, layer, mem_k, mem_v, conv_prev, gdn_state, k_past, v_past, *, tm):
    bsz, t, d = x.shape
    n = bsz * t
    x2d = x.reshape(n, d)
    main, k_new, v_new, tail = _in_proj(x2d, wts["norm_mix_g"], wts["w_in"], tm=min(IN_TM, n))
    conv_dim = 3 * GDN_WIDTH
    conv_prev8 = jnp.concatenate(
        [jnp.zeros((bsz, SUBLANES - (GDN_CONV - 1), conv_dim), F32), conv_prev.astype(F32)], axis=1)
    o_a, s_new = _gdn(main, tail, conv_prev8, gdn_state.astype(F32), wts["gdn_conv_w"], wts["gdn_a_log"],
                      wts["gdn_dt_bias"], wts["gdn_norm_g"], bsz=bsz, t=t)
    conv_new = main.reshape(bsz, t, -1)[:, t - (GDN_CONV - 1):, :conv_dim]
    lam_init = 0.8 - 0.6 * math.exp(-0.3 * layer)
    lams = tuple(wts[k].reshape(1, -1) for k in ("diff_lambda_q1", "diff_lambda_k1", "diff_lambda_q2", "diff_lambda_k2"))
    if k_past is None:
        o_b = _diff_prompt(main, k_new, v_new, lams, wts["diff_subln_g"], bsz=bsz, t=t, lam_init=lam_init)
    else:
        o_b = _diff_sample(main, k_new, v_new, k_past, v_past, lams, wts["diff_subln_g"], bsz=bsz, t=t,
                           lam_init=lam_init)
    k_rows = k_new.reshape(bsz, t, DIFF_HEADS, HEAD_DIM)
    v_rows = v_new.reshape(bsz, t, DIFF_HEADS, HEAD_DIM)
    merged = _merge(o_a, o_b, wts["w_branch_a"], wts["w_branch_b"], tail, tm=tm)
    x1 = _matmul(merged, wts["w_out"], residual=x2d, tm=tm, tn=MM_TN, name="out_proj")
    qm = _matmul(x1, wts["w_mq"], norm_g=wts["norm_cross_g"], tm=tm, tn=MM_TN, out_dtype=BF16, name="mem_q")
    mt = mem_k.shape[1]
    mem_k3, mem_v3 = mem_k.reshape(bsz, mt, d), mem_v.reshape(bsz, mt, d)
    if t >= tm:
        x2 = _cross_attend_project(qm, mem_k3, mem_v3, wts["w_mo"], x1, bsz=bsz, t=t, tm=tm, tn=MM_TN)
    else:
        oc = _cross_attend(qm, mem_k3, mem_v3, bsz=bsz, t=t, tm=t)
        x2 = _matmul(oc, wts["w_mo"], residual=x1, tm=tm, tn=MM_TN, name="mem_o")
    qp = _matmul(x2, wts["peer_w_q"], norm_g=wts["norm_ffn_g"], tm=tm, tn=MM_TN, name="peer_q")
    tn = min(PEER_TN, n)
    gates = _peer_route(qp, wts["peer_sub_keys"], tn=LANES)
    y = _peer_experts(x2, wts["norm_ffn_g"], gates, wts["peer_u"], wts["peer_v"], wts["final_norm_g"], tn=tn)
    return y.reshape(bsz, t, d), k_rows, v_rows, s_new, conv_new


def kernel(x_prompt, x_sample, cache_diff_k, cache_diff_v, state_gdn, state_conv, cache_mem_k, cache_mem_v,
           mem_prompt, norm_mix_g, w_in, gdn_conv_w, gdn_a_log, gdn_dt_bias, gdn_norm_g,
           diff_lambda_q1, diff_lambda_k1, diff_lambda_q2, diff_lambda_k2, diff_subln_g,
           w_branch_a, w_branch_b, w_out, norm_cross_g, norm_mem_g, w_mq, w_mk, w_mv, w_mo,
           norm_ffn_g, peer_w_q, peer_sub_keys, peer_u, peer_v, final_norm_g):
    depth = w_in.shape[0]
    assert depth == 1, "final norm is fused into the last layer's PEER kernel"
    l = 0
    wts = {
        "norm_mix_g": norm_mix_g[l], "w_in": _in_proj_weights(w_in[l]), "gdn_conv_w": gdn_conv_w[l],
        "gdn_a_log": gdn_a_log[l], "gdn_dt_bias": gdn_dt_bias[l], "gdn_norm_g": gdn_norm_g[l],
        "diff_lambda_q1": diff_lambda_q1[l], "diff_lambda_k1": diff_lambda_k1[l],
        "diff_lambda_q2": diff_lambda_q2[l], "diff_lambda_k2": diff_lambda_k2[l],
        "diff_subln_g": diff_subln_g[l], "w_branch_a": w_branch_a[l].astype(BF16),
        "w_branch_b": w_branch_b[l].astype(BF16), "w_out": w_out[l].astype(BF16),
        "norm_cross_g": norm_cross_g[l], "w_mq": w_mq[l].astype(BF16), "w_mo": w_mo[l].astype(BF16),
        "norm_ffn_g": norm_ffn_g[l], "peer_w_q": peer_w_q[l].astype(BF16), "peer_sub_keys": peer_sub_keys[l],
        "peer_u": peer_u[l], "peer_v": peer_v[l], "final_norm_g": final_norm_g,
    }
    bp, tp, d = x_prompt.shape
    bs, ts, _ = x_sample.shape
    mem2d = mem_prompt.reshape(-1, d)
    tmm = min(512, mem2d.shape[0])
    mem_k = _matmul(mem2d, w_mk[l], norm_g=norm_mem_g[l], tm=tmm, tn=512, name="mem_k")
    mem_v = _matmul(mem2d, w_mv[l], norm_g=norm_mem_g[l], tm=tmm, tn=512, name="mem_v")
    mshape = mem_prompt.shape[:2] + (MEM_HEADS, d // MEM_HEADS)
    mem_k = mem_k.reshape(mshape)
    mem_v = mem_v.reshape(mshape)
    conv0 = jnp.zeros((bp, GDN_CONV - 1, 3 * GDN_WIDTH), F32)
    s0 = jnp.zeros((bp, GDN_HEADS, HEAD_DIM, HEAD_DIM), F32)
    yp, pk, pv, ps, pc = _layer(x_prompt, wts, l, mem_k, mem_v, conv0, s0, None, None, tm=min(PROMPT_TM, bp * tp))
    ys, sk, sv, ss, sc = _layer(x_sample, wts, l, cache_mem_k[l], cache_mem_v[l], state_conv[l], state_gdn[l],
                                cache_diff_k[l], cache_diff_v[l], tm=min(SAMPLE_TM, bs * ts))
    return (yp, ys, pk[None], pv[None], ps[None], pc[None], mem_k[None], mem_v[None],
            sk[None], sv[None], ss[None], sc[None])
```

```python
import functools
import math

import jax
import jax.numpy as jnp
from jax import lax
from jax.experimental import pallas as pl
from jax.experimental.pallas import tpu as pltpu

F32 = jnp.float32
BF16 = jnp.bfloat16
HIGHEST = lax.Precision.HIGHEST

RMS_EPS = 1e-6
CHUNK = 64
CHUNK_SHIFT = 6
GDN_HEADS = 8
HEAD_DIM = 128
GDN_WIDTH = GDN_HEADS * HEAD_DIM
GDN_CONV = 4
DIFF_HEADS = 8
DIFF_HEAD_DIM = 64
MEM_HEADS = 4
PEER_HEADS = 8
PEER_N_KEYS = 128
PEER_TOPK = 16

LANES = 128
SUBLANES = 8
V7X_VMEM_LIMIT = 52 * 1024 * 1024

PROMPT_TM = 1024
SAMPLE_TM = 256
MM_TN = 1024
DIFF_TQ = 1024
DIFF_SUB = 256
GDN_CHUNKS_PER_STEP = 4


def _cparams(sem, vmem=V7X_VMEM_LIMIT):
    return pltpu.CompilerParams(dimension_semantics=sem, vmem_limit_bytes=vmem)


def _dot(a, b, precision=None):
    return jnp.dot(a, b, preferred_element_type=F32, precision=precision)


def _dot_nt(a, b, precision=None):
    return lax.dot_general(a, b, (((1,), (1,)), ((), ())), preferred_element_type=F32, precision=precision)


def _split_bf16(x):
    hi = x.astype(BF16)
    return hi, (x - hi.astype(F32)).astype(BF16)


def _cat_parts(p, q):
    return jnp.concatenate([p[0], q[0]], axis=0), jnp.concatenate([p[1], q[1]], axis=0)


def _dot3_parts(a_parts, b_parts, dot):
    a_hi, a_lo = a_parts
    b_hi, b_lo = b_parts
    m = a_hi.shape[0]
    both = dot(jnp.concatenate([a_hi, a_lo], axis=0), b_hi)
    return both[:m] + both[m:] + dot(a_hi, b_lo)


def _dot3(a, b):
    return _dot3_parts(_split_bf16(a), _split_bf16(b), _dot)


def _rms(xf, g):
    return xf * lax.rsqrt(jnp.mean(xf * xf, axis=-1, keepdims=True) + RMS_EPS) * g


def _mm_kernel(*refs, has_norm, has_res):
    it = iter(refs)
    x_ref = next(it)
    g_ref = next(it) if has_norm else None
    w_ref = next(it)
    r_ref = next(it) if has_res else None
    o_ref = next(it)
    h_scr = next(it)

    @pl.when(pl.program_id(1) == 0)
    def _():
        xf = x_ref[...].astype(F32)
        if has_norm:
            xf = _rms(xf, g_ref[...])
        h_scr[...] = xf.astype(BF16)

    acc = _dot(h_scr[...], w_ref[...].astype(BF16))
    if has_res:
        acc = acc + r_ref[...]
    o_ref[...] = acc.astype(o_ref.dtype)


def _matmul(x, w, *, norm_g=None, residual=None, tm, tn, out_dtype=F32, name="matmul"):
    m, k = x.shape
    n = w.shape[1]
    assert m % tm == 0 and n % tn == 0, (m, n, tm, tn)
    in_specs = [pl.BlockSpec((tm, k), lambda i, j: (i, 0))]
    args = [x]
    if norm_g is not None:
        in_specs.append(pl.BlockSpec((1, k), lambda i, j: (0, 0)))
        args.append(norm_g.reshape(1, k).astype(F32))
    in_specs.append(pl.BlockSpec((k, tn), lambda i, j: (0, j)))
    args.append(w)
    if residual is not None:
        in_specs.append(pl.BlockSpec((tm, tn), lambda i, j: (i, j)))
        args.append(residual)
    return pl.pallas_call(
        functools.partial(_mm_kernel, has_norm=norm_g is not None, has_res=residual is not None),
        out_shape=jax.ShapeDtypeStruct((m, n), out_dtype),
        grid=(m // tm, n // tn),
        in_specs=in_specs,
        out_specs=pl.BlockSpec((tm, tn), lambda i, j: (i, j)),
        scratch_shapes=[pltpu.VMEM((tm, k), BF16)],
        compiler_params=_cparams(("parallel", "arbitrary")),
        name=name,
    )(*args)


IN_TN = 512
IN_TM = 1024
IN_W1_TILES = 8
IN_W2_TILES = 14
IN_MAIN_TILES = 10
IN_K_TILE0 = 10
IN_V_TILE0 = 12
IN_TAIL_TILE0 = 14
IN_TILES = IN_W1_TILES + IN_W2_TILES + 1
MAIN_COLS = IN_MAIN_TILES * IN_TN
TAIL_COLS = (IN_TILES - IN_TAIL_TILE0) * IN_TN
COL_DQ_MAIN = 4096
COL_GB_TAIL = 2048
COL_BA_TAIL = 4096


def _in_proj_kernel(x_ref, g_ref, w1_ref, w2_ref, w3_ref, main_ref, k_ref, v_ref, tail_ref, h_scr):
    j = pl.program_id(1)

    @pl.when(j == 0)
    def _():
        h_scr[...] = _rms(x_ref[...], g_ref[...]).astype(BF16)

    routes = ((0, IN_W1_TILES, w1_ref, main_ref),
              (IN_W1_TILES, IN_K_TILE0, w2_ref, main_ref),
              (IN_K_TILE0, IN_V_TILE0, w2_ref, k_ref),
              (IN_V_TILE0, IN_TAIL_TILE0, w2_ref, v_ref),
              (IN_TAIL_TILE0, IN_TILES - 1, w2_ref, tail_ref),
              (IN_TILES - 1, IN_TILES, w3_ref, tail_ref))
    for lo, hi, w_ref, dst in routes:
        @pl.when(jnp.logical_and(j >= lo, j < hi))
        def _(w_ref=w_ref, dst=dst):
            dst[...] = _dot(h_scr[...], w_ref[...])


def _in_proj_weights(w_in):
    c_z = 4 * GDN_WIDTH
    c_ba = c_z + 2 * GDN_HEADS
    assert c_z == IN_W1_TILES * IN_TN and w_in.shape[1] - c_ba == IN_W2_TILES * IN_TN
    w_bf16 = w_in.astype(BF16)
    return w_bf16, w_bf16[:, c_ba:]


def _in_proj(x, norm_g, w_pieces, *, tm):
    m, d = x.shape
    w_all, w2 = w_pieces
    w1 = w3 = w_all
    tn = IN_TN
    out = lambda cols: jax.ShapeDtypeStruct((m, cols), F32)
    return pl.pallas_call(
        _in_proj_kernel,
        out_shape=(out(MAIN_COLS), out(2 * IN_TN), out(2 * IN_TN), out(TAIL_COLS)),
        grid=(m // tm, IN_TILES),
        in_specs=[
            pl.BlockSpec((tm, d), lambda i, j: (i, 0)),
            pl.BlockSpec((1, d), lambda i, j: (0, 0)),
            pl.BlockSpec((d, tn), lambda i, j: (0, jnp.minimum(j, IN_W1_TILES - 1))),
            pl.BlockSpec((d, tn), lambda i, j: (0, jnp.clip(j - IN_W1_TILES, 0, IN_W2_TILES - 1))),
            pl.BlockSpec((d, tn), lambda i, j: (0, IN_W1_TILES)),
        ],
        out_specs=(
            pl.BlockSpec((tm, tn), lambda i, j: (i, jnp.minimum(j, IN_MAIN_TILES - 1))),
            pl.BlockSpec((tm, tn), lambda i, j: (i, jnp.clip(j - IN_K_TILE0, 0, 1))),
            pl.BlockSpec((tm, tn), lambda i, j: (i, jnp.clip(j - IN_V_TILE0, 0, 1))),
            pl.BlockSpec((tm, tn), lambda i, j: (i, jnp.clip(j - IN_TAIL_TILE0, 0, IN_TILES - IN_TAIL_TILE0 - 1))),
        ),
        scratch_shapes=[pltpu.VMEM((tm, d), BF16)],
        compiler_params=_cparams(("parallel", "arbitrary")),
        name="in_proj",
    )(x, norm_g.reshape(1, d).astype(F32), w1, w2, w3)


def _gdn_kernel(qkv_ref, z_ref, ba_ref, prev_ref, s0_ref, convw_ref, alog_ref, dtb_ref, ng_ref,
                o_ref, sfin_ref, s_scr, xbuf, *, L, nch):
    step = pl.program_id(1)
    rows = nch * L

    @pl.when(step == 0)
    def _():
        s_scr[...] = s0_ref[0]
        xbuf[0:SUBLANES, :] = prev_ref[0]

    xbuf[SUBLANES:SUBLANES + rows, :] = qkv_ref[...]
    base = SUBLANES - (GDN_CONV - 1)
    y = xbuf[base:base + rows, :] * convw_ref[0:1, :]
    for i in range(1, GDN_CONV):
        y = y + xbuf[base + i:base + i + rows, :] * convw_ref[i:i + 1, :]
    y = y * jax.nn.sigmoid(y)
    carry = xbuf[rows:rows + SUBLANES, :]
    xbuf[0:SUBLANES, :] = carry

    row = lax.broadcasted_iota(jnp.int32, (L, L), 0)
    col = lax.broadcasted_iota(jnp.int32, (L, L), 1)
    lower = row >= col
    strict = row > col
    eye = (row == col).astype(F32)
    log_l = int(math.log2(L))
    n_double = log_l - 1
    assert 2 ** log_l == L

    ba = ba_ref[...]
    beta_all = jax.nn.sigmoid(ba)
    a_in = ba + dtb_ref[...]
    softplus = jnp.maximum(a_in, 0.0) + jnp.log(1.0 + jnp.exp(-jnp.abs(a_in)))
    g_all = -jnp.exp(alog_ref[...]) * softplus
    rr = lax.broadcasted_iota(jnp.int32, (rows, rows), 0)
    cc = lax.broadcasted_iota(jnp.int32, (rows, rows), 1)
    same_chunk_lower = jnp.logical_and(rr >= cc, jnp.right_shift(rr, log_l) == jnp.right_shift(cc, log_l))
    g_cum = _dot(same_chunk_lower.astype(F32), g_all, HIGHEST)
    g_cum_t = g_cum.T

    heads = range(GDN_HEADS)
    chains = [(c, h) for c in range(nch) for h in heads]
    ids = range(len(chains))

    def tok(c):
        return slice(c * L, (c + 1) * L)

    def head_cols(part, c, h):
        return y[tok(c), part * GDN_WIDTH + h * HEAD_DIM:part * GDN_WIDTH + (h + 1) * HEAD_DIM]

    q = [head_cols(0, c, h) for c, h in chains]
    k = [head_cols(1, c, h) for c, h in chains]
    v = [head_cols(2, c, h) for c, h in chains]
    q = [a * lax.rsqrt(jnp.sum(a * a, axis=-1, keepdims=True) + 1e-6) * (HEAD_DIM ** -0.5) for a in q]
    k = [a * lax.rsqrt(jnp.sum(a * a, axis=-1, keepdims=True) + 1e-6) for a in k]
    beta = [beta_all[tok(c), h:h + 1] for c, h in chains]
    g_col = [g_cum[tok(c), GDN_HEADS + h:GDN_HEADS + h + 1] for c, h in chains]
    g_row = [g_cum_t[GDN_HEADS + h:GDN_HEADS + h + 1, tok(c)] for c, h in chains]
    decay = [jnp.where(lower, jnp.exp(jnp.where(lower, g_col[n] - g_row[n], 0.0)), 0.0) for n in ids]
    g_last = [g_col[n][L - 1:L, :] for n in ids]
    e_col = [jnp.exp(g_col[n]) for n in ids]
    qk_kk = [_dot_nt(jnp.concatenate([q[n], k[n]], axis=0).astype(BF16), k[n].astype(BF16)) for n in ids]
    x = [-jnp.where(strict, beta[n] * qk_kk[n][L:] * decay[n], 0.0) for n in ids]
    t_inv = [eye + x[n] for n in ids]
    x_parts = [_split_bf16(x[n]) for n in ids]
    xp = [_dot3_parts(x_parts[n], x_parts[n], _dot) for n in ids]
    for _ in range(n_double - 1):
        xp_parts = [_split_bf16(xp[n]) for n in ids]
        both = [_dot3_parts(_cat_parts(_split_bf16(t_inv[n]), xp_parts[n]), xp_parts[n], _dot) for n in ids]
        t_inv = [t_inv[n] + both[n][:L] for n in ids]
        xp = [both[n][L:] for n in ids]
    t_inv = [t_inv[n] + _dot3(t_inv[n], xp[n]) for n in ids]
    sol = [_dot3(t_inv[n], jnp.concatenate([v[n] * beta[n], k[n] * (beta[n] * e_col[n])], axis=-1)) for n in ids]
    qk = [jnp.where(lower, qk_kk[n][:L] * decay[n], 0.0) for n in ids]
    k_dec_t = [(k[n] * jnp.exp(g_last[n] - g_col[n])).T for n in ids]
    left_s = [jnp.concatenate([sol[n][:, HEAD_DIM:], q[n] * e_col[n]], axis=0).astype(BF16) for n in ids]
    left_v = [jnp.concatenate([qk[n], k_dec_t[n]], axis=0).astype(BF16) for n in ids]

    s = [s_scr[h] for h in heads]
    for c in range(nch):
        n0 = c * GDN_HEADS
        ws_qs = [_dot(left_s[n0 + h], s[h].astype(BF16)) for h in heads]
        v_new = [sol[n0 + h][:, :HEAD_DIM] - ws_qs[h][:L] for h in heads]
        tail = [_dot(left_v[n0 + h], v_new[h].astype(BF16)) for h in heads]
        s = [s[h] * jnp.exp(g_last[n0 + h]) + tail[h][L:] for h in heads]
        for h in heads:
            zf = z_ref[tok(c), h * HEAD_DIM:(h + 1) * HEAD_DIM]
            o = _rms(ws_qs[h][L:] + tail[h][:L], ng_ref[...]) * (zf * jax.nn.sigmoid(zf))
            o_ref[tok(c), h * HEAD_DIM:(h + 1) * HEAD_DIM] = o.astype(o_ref.dtype)
    for h in heads:
        s_scr[h] = s[h]

    @pl.when(step == pl.num_programs(1) - 1)
    def _():
        sfin_ref[0] = s_scr[...]


def _decay_lanes(p):
    return jnp.zeros((1, LANES), F32).at[0, GDN_HEADS:2 * GDN_HEADS].set(p.astype(F32))


def _gdn(main, tail, conv_prev8, s0, conv_w, a_log, dt_bias, norm_g, *, bsz, t):
    L = min(CHUNK, t)
    nch = GDN_CHUNKS_PER_STEP if (t // L) % GDN_CHUNKS_PER_STEP == 0 else 1
    nc = t // (L * nch)
    rows = L * nch
    conv_dim = 3 * GDN_WIDTH
    kern = functools.partial(_gdn_kernel, L=L, nch=nch)
    o, s_fin = pl.pallas_call(
        kern,
        out_shape=(jax.ShapeDtypeStruct((bsz * t, GDN_WIDTH), BF16),
                   jax.ShapeDtypeStruct((bsz, GDN_HEADS, HEAD_DIM, HEAD_DIM), F32)),
        grid=(bsz, nc),
        in_specs=[
            pl.BlockSpec((rows, conv_dim), lambda b, c: (b * nc + c, 0)),
            pl.BlockSpec((rows, GDN_WIDTH), lambda b, c: (b * nc + c, conv_dim // GDN_WIDTH)),
            pl.BlockSpec((rows, LANES), lambda b, c: (b * nc + c, COL_BA_TAIL // LANES)),
            pl.BlockSpec((1, SUBLANES, conv_dim), lambda b, c: (b, 0, 0)),
            pl.BlockSpec((1, GDN_HEADS, HEAD_DIM, HEAD_DIM), lambda b, c: (b, 0, 0, 0)),
            pl.BlockSpec((GDN_CONV, conv_dim), lambda b, c: (0, 0)),
            pl.BlockSpec((1, LANES), lambda b, c: (0, 0)),
            pl.BlockSpec((1, LANES), lambda b, c: (0, 0)),
            pl.BlockSpec((1, HEAD_DIM), lambda b, c: (0, 0)),
        ],
        out_specs=(pl.BlockSpec((rows, GDN_WIDTH), lambda b, c: (b * nc + c, 0)),
                   pl.BlockSpec((1, GDN_HEADS, HEAD_DIM, HEAD_DIM), lambda b, c: (b, 0, 0, 0))),
        scratch_shapes=[pltpu.VMEM((GDN_HEADS, HEAD_DIM, HEAD_DIM), F32),
                        pltpu.VMEM((SUBLANES + rows + SUBLANES, conv_dim), F32)],
        compiler_params=_cparams(("parallel", "arbitrary")),
        name="gdn",
    )(main, main, tail, conv_prev8, s0, conv_w, _decay_lanes(a_log), _decay_lanes(dt_bias),
      norm_g.reshape(1, -1))
    return o, s_fin


def _lambda_value(lam_refs, lam_init):
    lq1, lk1, lq2, lk2 = (r[...] for r in lam_refs)
    return (jnp.exp(jnp.sum(lq1 * lk1, axis=-1, keepdims=True))
            - jnp.exp(jnp.sum(lq2 * lk2, axis=-1, keepdims=True)) + lam_init)


def _softmax_update(m_scr, l_scr, acc_scr, scores, values):
    idx = range(len(scores))
    m_old = [m_scr[n] for n in idx]
    m_new = [jnp.maximum(m_old[n], jnp.max(scores[n], axis=-1, keepdims=True)) for n in idx]
    p = [jnp.exp(scores[n] - m_new[n]) for n in idx]
    alpha = [jnp.exp(m_old[n] - m_new[n]) for n in idx]
    pv = [_dot(p[n].astype(BF16), values[n]) for n in idx]
    for n in idx:
        l_scr[n] = alpha[n] * l_scr[n] + jnp.sum(p[n], axis=-1, keepdims=True)
        acc_scr[n] = alpha[n] * acc_scr[n] + pv[n]
        m_scr[n] = m_new[n]


def _diff_finalize(o1, o2, lam_refs, g_ref, lam_init):
    lam = _lambda_value(lam_refs, lam_init)
    return _rms(o1 - lam * o2, g_ref[...]) * (1.0 - lam_init)


def _diff_init(m_scr, l_scr, acc_scr):
    m_scr[...] = jnp.full(m_scr.shape, -jnp.inf, F32)
    l_scr[...] = jnp.zeros(l_scr.shape, F32)
    acc_scr[...] = jnp.zeros(acc_scr.shape, F32)


def _near_bias(slope, r, q_pos, k_pos):
    visible = jnp.right_shift(k_pos, CHUNK_SHIFT) <= jnp.right_shift(q_pos, CHUNK_SHIFT)
    return jnp.where(visible, slope * (r - jnp.abs(q_pos - k_pos)).astype(F32), -jnp.inf)


def _split_distance(slope, d):
    return (-(slope * CHUNK) * jnp.right_shift(d, CHUNK_SHIFT).astype(F32),
            -slope * jnp.bitwise_and(d, CHUNK - 1).astype(F32))


def _diff_prompt_kernel(i_tab, j_tab, slopes_ref, q_ref, k_ref, v_ref, lq1, lk1, lq2, lk2, g_ref, o_ref,
                        m_scr, l_scr, acc_scr, lhs_scr, s_even, s_odd, *, tq, lam_init):
    h = pl.program_id(1)
    s_id = pl.program_id(2)
    n_pairs = pl.num_programs(2) - 1
    score_on = s_id < n_pairs
    consume_on = s_id >= 1
    i = i_tab[s_id]
    j = j_tab[s_id]
    ci = i_tab[jnp.maximum(s_id - 1, 0)]
    cj = j_tab[jnp.maximum(s_id - 1, 0)]
    slope = slopes_ref[h]
    half = DIFF_HEAD_DIM
    lane = lax.broadcasted_iota(jnp.int32, (tq, LANES), 1)

    @pl.when(jnp.logical_and(consume_on, cj == 0))
    def _():
        _diff_init(m_scr, l_scr, acc_scr)

    @pl.when(jnp.logical_and(score_on, j == 0))
    def _():
        qs = q_ref[...] * (DIFF_HEAD_DIM ** -0.5)
        lhs_scr[0] = jnp.where(lane < half, qs, 1.0).astype(BF16)
        lhs_scr[1] = jnp.where(lane >= half, qs, 1.0).astype(BF16)

    sub = min(DIFF_SUB, tq)
    units = [(n, qb) for qb in range(tq // sub) for n in range(2)]

    def score_tile(near, dst):
        kb = k_ref[...]
        if near:
            aug0 = aug1 = jnp.zeros_like(kb)
        else:
            d = (i - j) * tq - lax.broadcasted_iota(jnp.int32, (tq, LANES), 0)
            t_hi, t_lo = _split_distance(slope, d)
            aug0 = jnp.where(lane == half, t_hi, jnp.where(lane == half + 1, t_lo, 0.0))
            aug1 = jnp.where(lane == 0, t_hi, jnp.where(lane == 1, t_lo, 0.0))
        keys = [jnp.where(lane < half, kb, aug0).astype(BF16), jnp.where(lane >= half, kb, aug1).astype(BF16)]
        bias = None
        for n, qb in units:
            cols = slice(qb * sub, (qb + 1) * sub)
            if near:
                live = (qb + 1) * sub
                if n == 0:
                    k_pos = lax.broadcasted_iota(jnp.int32, (live, sub), 0)
                    q_pos = qb * sub + lax.broadcasted_iota(jnp.int32, (live, sub), 1)
                    bias = _near_bias(slope, q_pos, q_pos, k_pos)
                dst[n, :live, cols] = _dot_nt(keys[n][:live], lhs_scr[n, cols, :]) + bias
            else:
                dst[n, :, cols] = _dot_nt(keys[n], lhs_scr[n, cols, :])

    def consume_tile(near, src):
        vt = v_ref[...].T.astype(BF16)
        idx = range(len(units))
        cols = [slice(qb * sub, (qb + 1) * sub) for _, qb in units]
        live = [(qb + 1) * sub if near else tq for _, qb in units]
        s = [src[n, :live[u], cols[u]] for u, (n, _) in enumerate(units)]
        m_old = [m_scr[n, :, cols[u]] for u, (n, _) in enumerate(units)]
        m_new = [jnp.maximum(m_old[u], jnp.max(s[u], axis=0, keepdims=True)) for u in idx]
        p = [jnp.exp(s[u] - m_new[u]) for u in idx]
        alpha = [jnp.exp(m_old[u] - m_new[u]) for u in idx]
        pv = [_dot(vt[:, :live[u]], p[u].astype(BF16)) for u in idx]
        for u, (n, _) in enumerate(units):
            l_scr[n, :, cols[u]] = alpha[u] * l_scr[n, :, cols[u]] + jnp.sum(p[u], axis=0, keepdims=True)
            acc_scr[n, :, cols[u]] = alpha[u] * acc_scr[n, :, cols[u]] + pv[u]
            m_scr[n, :, cols[u]] = m_new[u]

    def stage(cond, score_near=None, consume_near=None):
        for parity, (dst, src) in enumerate(((s_even, s_odd), (s_odd, s_even))):
            @pl.when(jnp.logical_and(cond, s_id % 2 == parity))
            def _(dst=dst, src=src):
                if score_near is not None:
                    score_tile(score_near, dst)
                if consume_near is not None:
                    consume_tile(consume_near, src)

    both = jnp.logical_and(score_on, consume_on)
    stage(jnp.logical_and(both, jnp.logical_and(j < i, cj < ci)), score_near=False, consume_near=False)
    stage(jnp.logical_and(both, jnp.logical_and(j < i, cj == ci)), score_near=False, consume_near=True)
    stage(jnp.logical_and(both, j == i), score_near=True, consume_near=False)
    stage(jnp.logical_not(consume_on), score_near=True)
    stage(jnp.logical_not(score_on), consume_near=True)

    @pl.when(jnp.logical_and(consume_on, cj == ci))
    def _():
        o_ref[...] = _diff_finalize((acc_scr[0] / l_scr[0]).T, (acc_scr[1] / l_scr[1]).T,
                                    (lq1, lk1, lq2, lk2), g_ref, lam_init).astype(o_ref.dtype)


def _alibi_slopes():
    return jnp.asarray([2.0 ** (-8.0 * (i + 1) / DIFF_HEADS) for i in range(DIFF_HEADS)], F32)


def _diff_prompt(main, k_new, v_new, lams, subln_g, *, bsz, t, lam_init):
    tq = min(DIFF_TQ, t)
    nq = t // tq
    assert tq % CHUNK == 0 and t <= CHUNK * 255
    kern = functools.partial(_diff_prompt_kernel, tq=tq, lam_init=lam_init)
    qc, kc, vc = COL_DQ_MAIN // LANES, 0, 0
    pairs = [(i, j) for i in range(nq) for j in range(i + 1)]
    pairs.append(pairs[-1])
    i_tab = jnp.asarray([p[0] for p in pairs], jnp.int32)
    j_tab = jnp.asarray([p[1] for p in pairs], jnp.int32)
    zero = lambda b, h, s, it, jt: (0, 0)

    def consumed(tab, s):
        return tab[jnp.maximum(s - 1, 0)]

    grid_spec = pltpu.PrefetchScalarGridSpec(
        num_scalar_prefetch=2,
        grid=(bsz, DIFF_HEADS, len(pairs)),
        in_specs=[
            pl.BlockSpec(memory_space=pltpu.SMEM),
            pl.BlockSpec((tq, LANES), lambda b, h, s, it, jt: (b * nq + it[s], qc + h)),
            pl.BlockSpec((tq, LANES), lambda b, h, s, it, jt: (b * nq + jt[s], kc + h)),
            pl.BlockSpec((tq, LANES), lambda b, h, s, it, jt: (b * nq + consumed(jt, s), vc + h)),
        ] + [pl.BlockSpec((1, DIFF_HEAD_DIM), zero)] * 4 + [pl.BlockSpec((1, HEAD_DIM), zero)],
        out_specs=pl.BlockSpec((tq, LANES), lambda b, h, s, it, jt: (b * nq + consumed(it, s), h)),
        scratch_shapes=[pltpu.VMEM((2, 1, tq), F32), pltpu.VMEM((2, 1, tq), F32),
                        pltpu.VMEM((2, HEAD_DIM, tq), F32), pltpu.VMEM((2, tq, LANES), BF16),
                        pltpu.VMEM((2, tq, tq), F32), pltpu.VMEM((2, tq, tq), F32)],
    )
    return pl.pallas_call(
        kern,
        out_shape=jax.ShapeDtypeStruct((bsz * t, DIFF_HEADS * HEAD_DIM), BF16),
        grid_spec=grid_spec,
        compiler_params=_cparams(("parallel", "parallel", "arbitrary")),
        name="diff_attn_prompt",
    )(i_tab, j_tab, _alibi_slopes(), main, k_new, v_new, *lams, subln_g.reshape(1, -1))


def _diff_sample_kernel(slopes_ref, q_ref, kn_ref, vn_ref, kp_ref, vp_ref, lq1, lk1, lq2, lk2, g_ref, o_ref,
                        m_scr, l_scr, acc_scr, lhs_scr, *, t, tk, past, lam_init):
    j = pl.program_id(1)
    heads = range(DIFF_HEADS)

    def cols(h):
        return slice(h * HEAD_DIM, (h + 1) * HEAD_DIM)

    @pl.when(j == 0)
    def _():
        _diff_init(m_scr, l_scr, acc_scr)
        lane = lax.broadcasted_iota(jnp.int32, (t, LANES), 1)
        r = lax.broadcasted_iota(jnp.int32, (t, t), 0)
        c = lax.broadcasted_iota(jnp.int32, (t, t), 1)
        scores = []
        for h in heads:
            qs = q_ref[:, cols(h)] * (DIFF_HEAD_DIM ** -0.5)
            lhs_scr[h, 0:t, :] = jnp.where(lane < DIFF_HEAD_DIM, qs, 0.0).astype(BF16)
            lhs_scr[h, t:2 * t, :] = jnp.where(lane >= DIFF_HEAD_DIM, qs, 0.0).astype(BF16)
            bias = _near_bias(slopes_ref[h], r, past + r, past + c)
            scores.append(_dot_nt(lhs_scr[h], kn_ref[:, cols(h)].astype(BF16)) + jnp.concatenate([bias, bias], axis=0))
        _softmax_update(m_scr, l_scr, acc_scr, scores, [vn_ref[:, cols(h)].astype(BF16) for h in heads])

    d = (past - j * tk - lax.broadcasted_iota(jnp.int32, (1, tk), 1)).astype(F32)
    kt = jnp.swapaxes(kp_ref[...], 0, 1).astype(BF16)
    vt = jnp.swapaxes(vp_ref[...], 0, 1).astype(BF16)
    scores = [_dot_nt(lhs_scr[h], kt[h]) - slopes_ref[h] * d for h in heads]
    _softmax_update(m_scr, l_scr, acc_scr, scores, [vt[h] for h in heads])

    @pl.when(j == pl.num_programs(1) - 1)
    def _():
        for h in heads:
            o = acc_scr[h] / l_scr[h]
            o_ref[:, cols(h)] = _diff_finalize(o[:t], o[t:], (lq1, lk1, lq2, lk2), g_ref, lam_init).astype(o_ref.dtype)


def _diff_sample(main, k_new, v_new, k_past, v_past, lams, subln_g, *, bsz, t, lam_init):
    past = k_past.shape[1]
    tk = min(1024, past)
    nk = past // tk
    width = DIFF_HEADS * HEAD_DIM
    kern = functools.partial(_diff_sample_kernel, t=t, tk=tk, past=past, lam_init=lam_init)
    zero = lambda b, j: (0, 0)
    cache = pl.BlockSpec((None, tk, DIFF_HEADS, HEAD_DIM), lambda b, j: (b, j, 0, 0))
    return pl.pallas_call(
        kern,
        out_shape=jax.ShapeDtypeStruct((bsz * t, width), BF16),
        grid=(bsz, nk),
        in_specs=[
            pl.BlockSpec(memory_space=pltpu.SMEM),
            pl.BlockSpec((t, width), lambda b, j: (b, COL_DQ_MAIN // width)),
            pl.BlockSpec((t, width), lambda b, j: (b, 0)),
            pl.BlockSpec((t, width), lambda b, j: (b, 0)),
            cache, cache,
        ] + [pl.BlockSpec((1, DIFF_HEAD_DIM), zero)] * 4 + [pl.BlockSpec((1, HEAD_DIM), zero)],
        out_specs=pl.BlockSpec((t, width), lambda b, j: (b, 0)),
        scratch_shapes=[pltpu.VMEM((DIFF_HEADS, 2 * t, 1), F32), pltpu.VMEM((DIFF_HEADS, 2 * t, 1), F32),
                        pltpu.VMEM((DIFF_HEADS, 2 * t, HEAD_DIM), F32), pltpu.VMEM((DIFF_HEADS, 2 * t, LANES), BF16)],
        compiler_params=_cparams(("parallel", "arbitrary")),
        name="diff_attn_sample",
    )(_alibi_slopes(), main, k_new, v_new, k_past, v_past, *lams, subln_g.reshape(1, -1))


def _merge_kernel(oa_ref, ob_ref, wa_ref, wb_ref, ga_ref, gb_ref, o_ref):
    ya = _dot(oa_ref[...], wa_ref[...])
    yb = _dot(ob_ref[...], wb_ref[...])
    o_ref[...] = (jax.nn.sigmoid(ga_ref[...]) * ya + jax.nn.sigmoid(gb_ref[...]) * yb).astype(o_ref.dtype)


def _merge(o_a, o_b, wa, wb, tail, *, tm):
    m = o_a.shape[0]
    d = wa.shape[1]
    tn = MM_TN
    ga0, gb0 = 0, COL_GB_TAIL // tn
    return pl.pallas_call(
        _merge_kernel,
        out_shape=jax.ShapeDtypeStruct((m, d), BF16),
        grid=(m // tm, d // tn),
        in_specs=[
            pl.BlockSpec((tm, o_a.shape[1]), lambda i, j: (i, 0)),
            pl.BlockSpec((tm, o_b.shape[1]), lambda i, j: (i, 0)),
            pl.BlockSpec((wa.shape[0], tn), lambda i, j: (0, j)),
            pl.BlockSpec((wb.shape[0], tn), lambda i, j: (0, j)),
            pl.BlockSpec((tm, tn), lambda i, j: (i, ga0 + j)),
            pl.BlockSpec((tm, tn), lambda i, j: (i, gb0 + j)),
        ],
        out_specs=pl.BlockSpec((tm, tn), lambda i, j: (i, j)),
        compiler_params=_cparams(("parallel", "arbitrary")),
        name="merge",
    )(o_a, o_b, wa, wb, tail, tail)


def _cross_kernel(q_ref, k_ref, v_ref, o_ref):
    dh = q_ref.shape[-1] // MEM_HEADS
    for h in range(MEM_HEADS):
        sl = slice(h * dh, (h + 1) * dh)
        s = _dot_nt(q_ref[:, sl], k_ref[0, :, sl].astype(BF16)) * (dh ** -0.5)
        s = s - jnp.max(s, axis=-1, keepdims=True)
        p = jnp.exp(s)
        p = p / jnp.sum(p, axis=-1, keepdims=True)
        o_ref[:, sl] = _dot(p.astype(BF16), v_ref[0, :, sl].astype(BF16)).astype(o_ref.dtype)


def _cross_attend(q, mem_k, mem_v, *, bsz, t, tm):
    d = q.shape[1]
    nt = t // tm
    mt = mem_k.shape[1]
    return pl.pallas_call(
        _cross_kernel,
        out_shape=jax.ShapeDtypeStruct(q.shape, BF16),
        grid=(bsz, nt),
        in_specs=[
            pl.BlockSpec((tm, d), lambda b, i: (b * nt + i, 0)),
            pl.BlockSpec((1, mt, d), lambda b, i: (b, 0, 0)),
            pl.BlockSpec((1, mt, d), lambda b, i: (b, 0, 0)),
        ],
        out_specs=pl.BlockSpec((tm, d), lambda b, i: (b * nt + i, 0)),
        compiler_params=_cparams(("parallel", "arbitrary")),
        name="cross_attn",
    )(q, mem_k, mem_v)


def _topk_rows(work, cidx, k):
    n, tn = work.shape
    rid = lax.broadcasted_iota(jnp.int32, (n, tn), 0).astype(F32)
    kid = lax.broadcasted_iota(jnp.int32, (k, tn), 0)
    vals = jnp.zeros((k, tn), F32)
    idxs = jnp.zeros((k, tn), F32)
    for t in range(k):
        m = jnp.max(work, axis=0, keepdims=True)
        pos = jnp.min(jnp.where(work == m, rid, float(n)), axis=0, keepdims=True)
        hit = rid == pos
        if cidx is None:
            picked = pos
        else:
            picked = jnp.sum(jnp.where(hit, cidx, 0.0), axis=0, keepdims=True)
        vals = jnp.where(kid == t, m, vals)
        idxs = jnp.where(kid == t, picked, idxs)
        work = jnp.where(hit, -jnp.inf, work)
    return vals, idxs


def _product_candidates(v1, i1, v2, i2, k):
    assert k == 2 * SUBLANES
    sub = lax.broadcasted_iota(jnp.int32, (SUBLANES, v1.shape[1]), 0)
    vals, idxs = [], []
    for a in range(k // 2):
        nb = k // (a + 1)
        width = k if nb > SUBLANES else SUBLANES
        cv = v1[a:a + 1, :] + v2[:width, :]
        ci = i1[a:a + 1, :] * PEER_N_KEYS + i2[:width, :]
        if nb < SUBLANES:
            cv = jnp.where(sub < nb, cv, -jnp.inf)
        vals.append(cv)
        idxs.append(ci)
    vals.append(v1[k // 2:, :] + v2[0:1, :])
    idxs.append(i1[k // 2:, :] * PEER_N_KEYS + i2[0:1, :])
    return jnp.concatenate(vals, axis=0), jnp.concatenate(idxs, axis=0)


def _peer_route_kernel(q_ref, keys_hi_ref, keys_lo_ref, e_ref, g_ref):
    k = PEER_TOPK
    for h in range(PEER_HEADS):
        sub = []
        for p in range(2):
            c0 = (h * 2 + p) * PEER_N_KEYS
            q_parts = _split_bf16(q_ref[:, c0:c0 + PEER_N_KEYS])
            s = _dot3_parts((keys_hi_ref[h, p], keys_lo_ref[h, p]), q_parts, _dot_nt)
            sub.append(_topk_rows(s, None, k))
        (v1, i1), (v2, i2) = sub
        cand, cidx = _product_candidates(v1, i1, v2, i2, k)
        best, eidx = _topk_rows(cand, cidx, k)
        ex = jnp.exp(best - jnp.max(best, axis=0, keepdims=True))
        gate = ex / jnp.sum(ex, axis=0, keepdims=True)
        e_ref[h * k:(h + 1) * k, :] = eidx.astype(jnp.int32)
        g_ref[h * k:(h + 1) * k, :] = gate


PEER_GATE_TN = 256


def _peer_gate_kernel(et_ref, gt_ref, o_ref, e_scr, g_scr):
    tg = et_ref.shape[1]
    e_scr[...] = et_ref[...].T
    g_scr[...] = gt_ref[...].T
    kid = lax.broadcasted_iota(jnp.int32, (PEER_N_KEYS, PEER_N_KEYS), 0)
    group = 2 * SUBLANES

    def body(n, carry):
        rows = pl.ds(pl.multiple_of(n * group, group), group)
        e16 = e_scr[rows, :]
        g16 = g_scr[rows, :]
        a16 = jnp.right_shift(e16, 7)
        b16 = jnp.bitwise_and(e16, PEER_N_KEYS - 1)
        toks = range(group)
        at = [jnp.where(a16[s:s + 1, :] == kid, g16[s:s + 1, :], 0.0).astype(BF16) for s in toks]
        bt = [jnp.where(b16[s:s + 1, :] == kid, 1.0, 0.0).astype(BF16) for s in toks]
        grids = [_dot_nt(at[s], bt[s]) for s in toks]
        o_ref[:, rows, :] = jnp.swapaxes(jnp.stack(grids, axis=0), 0, 1).astype(BF16)
        return carry

    lax.fori_loop(0, tg // group, body, 0, unroll=8)


def _peer_route(qp, sub_keys, *, tn):
    n = qp.shape[0]
    rows = PEER_HEADS * PEER_TOPK
    keys_hi, keys_lo = _split_bf16(sub_keys.astype(F32))
    keys_spec = pl.BlockSpec(sub_keys.shape, lambda i: (0, 0, 0, 0))
    e_t, g_t = pl.pallas_call(
        _peer_route_kernel,
        out_shape=(jax.ShapeDtypeStruct((rows, n), jnp.int32), jax.ShapeDtypeStruct((rows, n), F32)),
        grid=(n // tn,),
        in_specs=[pl.BlockSpec((tn, qp.shape[1]), lambda i: (i, 0)), keys_spec, keys_spec],
        out_specs=(pl.BlockSpec((rows, tn), lambda i: (0, i)), pl.BlockSpec((rows, tn), lambda i: (0, i))),
        compiler_params=_cparams(("parallel",)),
        name="peer_route",
    )(qp, keys_hi, keys_lo)
    tg = min(PEER_GATE_TN, n)
    return pl.pallas_call(
        _peer_gate_kernel,
        out_shape=jax.ShapeDtypeStruct((PEER_N_KEYS, n, PEER_N_KEYS), BF16),
        grid=(n // tg,),
        in_specs=[pl.BlockSpec((rows, tg), lambda i: (0, i)), pl.BlockSpec((rows, tg), lambda i: (0, i))],
        out_specs=pl.BlockSpec((PEER_N_KEYS, tg, PEER_N_KEYS), lambda i: (0, i, 0)),
        scratch_shapes=[pltpu.VMEM((tg, rows), jnp.int32), pltpu.VMEM((tg, rows), F32)],
        compiler_params=_cparams(("parallel",)),
        name="peer_gates",
    )(e_t, g_t)


def _gelu_tanh(x):
    return 0.5 * x * (1.0 + jnp.tanh(math.sqrt(2.0 / math.pi) * (x + 0.044715 * (x * x * x))))


PEER_BLOCKS_PER_STEP = 4
PEER_TN = 1024
PEER_VMEM_LIMIT = 60 * 1024 * 1024


def _peer_expert_kernel(x_ref, gn_ref, gate_ref, u_ref, v_ref, gf_ref, o_ref, h_scr, acc_scr, coef_scr):
    s = pl.program_id(1)
    last = pl.num_programs(1) - 1

    @pl.when(s == 0)
    def _():
        h_scr[...] = _rms(x_ref[...], gn_ref[...]).astype(BF16)
        acc_scr[...] = jnp.zeros(acc_scr.shape, F32)
        coef_scr[...] = jnp.zeros(coef_scr.shape, BF16)

    def contract():
        acc_scr[...] += _dot(coef_scr[(s + 1) % 2], v_ref[...].astype(BF16))

    @pl.when(s < last)
    def _():
        contract()
        gate = jnp.concatenate([gate_ref[q].astype(F32) for q in range(gate_ref.shape[0])], axis=1)
        hid = _gelu_tanh(_dot_nt(h_scr[...], u_ref[...].astype(BF16)))
        coef_scr[s % 2] = (gate * hid).astype(BF16)

    @pl.when(s == last)
    def _():
        contract()
        o_ref[...] = _rms(x_ref[...] + acc_scr[...], gf_ref[...])


def _peer_experts(x, norm_g, gates, u_tab, v_tab, final_g, *, tn):
    n, d = x.shape
    eb = PEER_N_KEYS
    assert u_tab.shape[0] == eb * eb
    nblk = PEER_BLOCKS_PER_STEP
    ns = eb // nblk

    def cur(s):
        return jnp.minimum(s, ns - 1)

    def prev(s):
        return jnp.maximum(s - 1, 0)

    return pl.pallas_call(
        _peer_expert_kernel,
        out_shape=jax.ShapeDtypeStruct((n, d), F32),
        grid=(n // tn, ns + 1),
        in_specs=[
            pl.BlockSpec((tn, d), lambda i, s: (i, 0), pipeline_mode=pl.Buffered(1)),
            pl.BlockSpec((1, d), lambda i, s: (0, 0)),
            pl.BlockSpec((nblk, tn, eb), lambda i, s: (cur(s), i, 0)),
            pl.BlockSpec((nblk * eb, d), lambda i, s: (cur(s), 0)),
            pl.BlockSpec((nblk * eb, d), lambda i, s: (prev(s), 0)),
            pl.BlockSpec((1, d), lambda i, s: (0, 0)),
        ],
        out_specs=pl.BlockSpec((tn, d), lambda i, s: (i, 0), pipeline_mode=pl.Buffered(1)),
        scratch_shapes=[pltpu.VMEM((tn, d), BF16), pltpu.VMEM((tn, d), F32),
                        pltpu.VMEM((2, tn, nblk * eb), BF16)],
        compiler_params=_cparams(("parallel", "arbitrary"), vmem=PEER_VMEM_LIMIT),
        name="peer_experts",
    )(x, norm_g.reshape(1, d), gates, u_tab, v_tab, final_g.reshape(1, d))


def _layer(x, wts, layer, mem_k, mem_v, conv_prev, gdn_state, k_past, v_past, *, tm):
    bsz, t, d = x.shape
    n = bsz * t
    x2d = x.reshape(n, d)
    main, k_new, v_new, tail = _in_proj(x2d, wts["norm_mix_g"], wts["w_in"], tm=min(IN_TM, n))
    conv_dim = 3 * GDN_WIDTH
    conv_prev8 = jnp.concatenate(
        [jnp.zeros((bsz, SUBLANES - (GDN_CONV - 1), conv_dim), F32), conv_prev.astype(F32)], axis=1)
    o_a, s_new = _gdn(main, tail, conv_prev8, gdn_state.astype(F32), wts["gdn_conv_w"], wts["gdn_a_log"],
                      wts["gdn_dt_bias"], wts["gdn_norm_g"], bsz=bsz, t=t)
    conv_new = main.reshape(bsz, t, -1)[:, t - (GDN_CONV - 1):, :conv_dim]
    lam_init = 0.8 - 0.6 * math.exp(-0.3 * layer)
    lams = tuple(wts[k].reshape(1, -1) for k in ("diff_lambda_q1", "diff_lambda_k1", "diff_lambda_q2", "diff_lambda_k2"))
    if k_past is None:
        o_b = _diff_prompt(main, k_new, v_new, lams, wts["diff_subln_g"], bsz=bsz, t=t, lam_init=lam_init)
    else:
        o_b = _diff_sample(main, k_new, v_new, k_past, v_past, lams, wts["diff_subln_g"], bsz=bsz, t=t,
                           lam_init=lam_init)
    k_rows = k_new.reshape(bsz, t, DIFF_HEADS, HEAD_DIM)
    v_rows = v_new.reshape(bsz, t, DIFF_HEADS, HEAD_DIM)
    merged = _merge(o_a, o_b, wts["w_branch_a"], wts["w_branch_b"], tail, tm=tm)
    x1 = _matmul(merged, wts["w_out"], residual=x2d, tm=tm, tn=MM_TN, name="out_proj")
    qm = _matmul(x1, wts["w_mq"], norm_g=wts["norm_cross_g"], tm=tm, tn=MM_TN, out_dtype=BF16, name="mem_q")
    mt = mem_k.shape[1]
    oc = _cross_attend(qm, mem_k.reshape(bsz, mt, d), mem_v.reshape(bsz, mt, d), bsz=bsz, t=t, tm=min(tm, t))
    x2 = _matmul(oc, wts["w_mo"], residual=x1, tm=tm, tn=MM_TN, name="mem_o")
    qp = _matmul(x2, wts["peer_w_q"], norm_g=wts["norm_ffn_g"], tm=tm, tn=MM_TN, name="peer_q")
    tn = min(PEER_TN, n)
    gates = _peer_route(qp, wts["peer_sub_keys"], tn=LANES)
    y = _peer_experts(x2, wts["norm_ffn_g"], gates, wts["peer_u"], wts["peer_v"], wts["final_norm_g"], tn=tn)
    return y.reshape(bsz, t, d), k_rows, v_rows, s_new, conv_new


def kernel(x_prompt, x_sample, cache_diff_k, cache_diff_v, state_gdn, state_conv, cache_mem_k, cache_mem_v,
           mem_prompt, norm_mix_g, w_in, gdn_conv_w, gdn_a_log, gdn_dt_bias, gdn_norm_g,
           diff_lambda_q1, diff_lambda_k1, diff_lambda_q2, diff_lambda_k2, diff_subln_g,
           w_branch_a, w_branch_b, w_out, norm_cross_g, norm_mem_g, w_mq, w_mk, w_mv, w_mo,
           norm_ffn_g, peer_w_q, peer_sub_keys, peer_u, peer_v, final_norm_g):
    depth = w_in.shape[0]
    assert depth == 1, "final norm is fused into the last layer's PEER kernel"
    l = 0
    wts = {
        "norm_mix_g": norm_mix_g[l], "w_in": _in_proj_weights(w_in[l]), "gdn_conv_w": gdn_conv_w[l],
        "gdn_a_log": gdn_a_log[l], "gdn_dt_bias": gdn_dt_bias[l], "gdn_norm_g": gdn_norm_g[l],
        "diff_lambda_q1": diff_lambda_q1[l], "diff_lambda_k1": diff_lambda_k1[l],
        "diff_lambda_q2": diff_lambda_q2[l], "diff_lambda_k2": diff_lambda_k2[l],
        "diff_subln_g": diff_subln_g[l], "w_branch_a": w_branch_a[l].astype(BF16),
        "w_branch_b": w_branch_b[l].astype(BF16), "w_out": w_out[l].astype(BF16),
        "norm_cross_g": norm_cross_g[l], "w_mq": w_mq[l].astype(BF16), "w_mo": w_mo[l].astype(BF16),
        "norm_ffn_g": norm_ffn_g[l], "peer_w_q": peer_w_q[l].astype(BF16), "peer_sub_keys": peer_sub_keys[l],
        "peer_u": peer_u[l], "peer_v": peer_v[l], "final_norm_g": final_norm_g,
    }
    bp, tp, d = x_prompt.shape
    bs, ts, _ = x_sample.shape
    mem2d = mem_prompt.reshape(-1, d)
    tmm = min(512, mem2d.shape[0])
    mem_k = _matmul(mem2d, w_mk[l], norm_g=norm_mem_g[l], tm=tmm, tn=512, name="mem_k")
    mem_v = _matmul(mem2d, w_mv[l], norm_g=norm_mem_g[l], tm=tmm, tn=512, name="mem_v")
    mshape = mem_prompt.shape[:2] + (MEM_HEADS, d // MEM_HEADS)
    mem_k = mem_k.reshape(mshape)
    mem_v = mem_v.reshape(mshape)
    conv0 = jnp.zeros((bp, GDN_CONV - 1, 3 * GDN_WIDTH), F32)
    s0 = jnp.zeros((bp, GDN_HEADS, HEAD_DIM, HEAD_DIM), F32)
    yp, pk, pv, ps, pc = _layer(x_prompt, wts, l, mem_k, mem_v, conv0, s0, None, None, tm=min(PROMPT_TM, bp * tp))
    ys, sk, sv, ss, sc = _layer(x_sample, wts, l, cache_mem_k[l], cache_mem_v[l], state_conv[l], state_gdn[l],
                                cache_diff_k[l], cache_diff_v[l], tm=min(SAMPLE_TM, bs * ts))
    return (yp, ys, pk[None], pv[None], ps[None], pc[None], mem_k[None], mem_v[None],
            sk[None], sv[None], ss[None], sc[None])
```

```python
import functools
import math

import jax
import jax.numpy as jnp
from jax import lax
from jax.experimental import pallas as pl
from jax.experimental.pallas import tpu as pltpu

F32 = jnp.float32
BF16 = jnp.bfloat16
HIGHEST = lax.Precision.HIGHEST

RMS_EPS = 1e-6
CHUNK = 64
CHUNK_SHIFT = 6
GDN_HEADS = 8
HEAD_DIM = 128
GDN_WIDTH = GDN_HEADS * HEAD_DIM
GDN_CONV = 4
DIFF_HEADS = 8
DIFF_HEAD_DIM = 64
MEM_HEADS = 4
PEER_HEADS = 8
PEER_N_KEYS = 128
PEER_TOPK = 16

LANES = 128
SUBLANES = 8
V7X_VMEM_LIMIT = 52 * 1024 * 1024

PROMPT_TM = 1024
SAMPLE_TM = 256
MM_TN = 1024
DIFF_TQ = 1024
DIFF_SUB = 256
GDN_CHUNKS_PER_STEP = 4


def _cparams(sem, vmem=V7X_VMEM_LIMIT):
    return pltpu.CompilerParams(dimension_semantics=sem, vmem_limit_bytes=vmem)


def _dot(a, b, precision=None):
    return jnp.dot(a, b, preferred_element_type=F32, precision=precision)


def _dot_nt(a, b, precision=None):
    return lax.dot_general(a, b, (((1,), (1,)), ((), ())), preferred_element_type=F32, precision=precision)


def _split_bf16(x):
    hi = x.astype(BF16)
    return hi, (x - hi.astype(F32)).astype(BF16)


def _cat_parts(p, q):
    return jnp.concatenate([p[0], q[0]], axis=0), jnp.concatenate([p[1], q[1]], axis=0)


def _dot3_parts(a_parts, b_parts, dot):
    a_hi, a_lo = a_parts
    b_hi, b_lo = b_parts
    m = a_hi.shape[0]
    both = dot(jnp.concatenate([a_hi, a_lo], axis=0), b_hi)
    return both[:m] + both[m:] + dot(a_hi, b_lo)


def _dot3(a, b):
    return _dot3_parts(_split_bf16(a), _split_bf16(b), _dot)


def _rms(xf, g):
    return xf * lax.rsqrt(jnp.mean(xf * xf, axis=-1, keepdims=True) + RMS_EPS) * g


def _mm_kernel(*refs, has_norm, has_res):
    it = iter(refs)
    x_ref = next(it)
    g_ref = next(it) if has_norm else None
    w_ref = next(it)
    r_ref = next(it) if has_res else None
    o_ref = next(it)
    h_scr = next(it)

    @pl.when(pl.program_id(1) == 0)
    def _():
        xf = x_ref[...].astype(F32)
        if has_norm:
            xf = _rms(xf, g_ref[...])
        h_scr[...] = xf.astype(BF16)

    acc = _dot(h_scr[...], w_ref[...].astype(BF16))
    if has_res:
        acc = acc + r_ref[...]
    o_ref[...] = acc.astype(o_ref.dtype)


def _matmul(x, w, *, norm_g=None, residual=None, tm, tn, out_dtype=F32, name="matmul"):
    m, k = x.shape
    n = w.shape[1]
    assert m % tm == 0 and n % tn == 0, (m, n, tm, tn)
    in_specs = [pl.BlockSpec((tm, k), lambda i, j: (i, 0))]
    args = [x]
    if norm_g is not None:
        in_specs.append(pl.BlockSpec((1, k), lambda i, j: (0, 0)))
        args.append(norm_g.reshape(1, k).astype(F32))
    in_specs.append(pl.BlockSpec((k, tn), lambda i, j: (0, j)))
    args.append(w)
    if residual is not None:
        in_specs.append(pl.BlockSpec((tm, tn), lambda i, j: (i, j)))
        args.append(residual)
    return pl.pallas_call(
        functools.partial(_mm_kernel, has_norm=norm_g is not None, has_res=residual is not None),
        out_shape=jax.ShapeDtypeStruct((m, n), out_dtype),
        grid=(m // tm, n // tn),
        in_specs=in_specs,
        out_specs=pl.BlockSpec((tm, tn), lambda i, j: (i, j)),
        scratch_shapes=[pltpu.VMEM((tm, k), BF16)],
        compiler_params=_cparams(("parallel", "arbitrary")),
        name=name,
    )(*args)


IN_TN = 512
IN_TM = 1024
IN_W1_TILES = 8
IN_W2_TILES = 14
IN_MAIN_TILES = 10
IN_K_TILE0 = 10
IN_V_TILE0 = 12
IN_TAIL_TILE0 = 14
IN_TILES = IN_W1_TILES + IN_W2_TILES + 1
MAIN_COLS = IN_MAIN_TILES * IN_TN
TAIL_COLS = (IN_TILES - IN_TAIL_TILE0) * IN_TN
COL_DQ_MAIN = 4096
COL_GB_TAIL = 2048
COL_BA_TAIL = 4096


def _in_proj_kernel(x_ref, g_ref, w1_ref, w2_ref, w3_ref, main_ref, k_ref, v_ref, tail_ref, h_scr):
    j = pl.program_id(1)

    @pl.when(j == 0)
    def _():
        h_scr[...] = _rms(x_ref[...], g_ref[...]).astype(BF16)

    routes = ((0, IN_W1_TILES, w1_ref, main_ref),
              (IN_W1_TILES, IN_K_TILE0, w2_ref, main_ref),
              (IN_K_TILE0, IN_V_TILE0, w2_ref, k_ref),
              (IN_V_TILE0, IN_TAIL_TILE0, w2_ref, v_ref),
              (IN_TAIL_TILE0, IN_TILES - 1, w2_ref, tail_ref),
              (IN_TILES - 1, IN_TILES, w3_ref, tail_ref))
    for lo, hi, w_ref, dst in routes:
        @pl.when(jnp.logical_and(j >= lo, j < hi))
        def _(w_ref=w_ref, dst=dst):
            dst[...] = _dot(h_scr[...], w_ref[...])


def _in_proj_weights(w_in):
    c_z = 4 * GDN_WIDTH
    c_ba = c_z + 2 * GDN_HEADS
    assert c_z == IN_W1_TILES * IN_TN and w_in.shape[1] - c_ba == IN_W2_TILES * IN_TN
    w_bf16 = w_in.astype(BF16)
    return w_bf16, w_bf16[:, c_ba:]


def _in_proj(x, norm_g, w_pieces, *, tm):
    m, d = x.shape
    w_all, w2 = w_pieces
    w1 = w3 = w_all
    tn = IN_TN
    out = lambda cols: jax.ShapeDtypeStruct((m, cols), F32)
    return pl.pallas_call(
        _in_proj_kernel,
        out_shape=(out(MAIN_COLS), out(2 * IN_TN), out(2 * IN_TN), out(TAIL_COLS)),
        grid=(m // tm, IN_TILES),
        in_specs=[
            pl.BlockSpec((tm, d), lambda i, j: (i, 0)),
            pl.BlockSpec((1, d), lambda i, j: (0, 0)),
            pl.BlockSpec((d, tn), lambda i, j: (0, jnp.minimum(j, IN_W1_TILES - 1))),
            pl.BlockSpec((d, tn), lambda i, j: (0, jnp.clip(j - IN_W1_TILES, 0, IN_W2_TILES - 1))),
            pl.BlockSpec((d, tn), lambda i, j: (0, IN_W1_TILES)),
        ],
        out_specs=(
            pl.BlockSpec((tm, tn), lambda i, j: (i, jnp.minimum(j, IN_MAIN_TILES - 1))),
            pl.BlockSpec((tm, tn), lambda i, j: (i, jnp.clip(j - IN_K_TILE0, 0, 1))),
            pl.BlockSpec((tm, tn), lambda i, j: (i, jnp.clip(j - IN_V_TILE0, 0, 1))),
            pl.BlockSpec((tm, tn), lambda i, j: (i, jnp.clip(j - IN_TAIL_TILE0, 0, IN_TILES - IN_TAIL_TILE0 - 1))),
        ),
        scratch_shapes=[pltpu.VMEM((tm, d), BF16)],
        compiler_params=_cparams(("parallel", "arbitrary")),
        name="in_proj",
    )(x, norm_g.reshape(1, d).astype(F32), w1, w2, w3)


def _gdn_kernel(qkv_ref, z_ref, ba_ref, prev_ref, s0_ref, convw_ref, alog_ref, dtb_ref, ng_ref,
                o_ref, sfin_ref, s_scr, xbuf, *, L, nch):
    step = pl.program_id(1)
    rows = nch * L

    @pl.when(step == 0)
    def _():
        s_scr[...] = s0_ref[0]
        xbuf[0:SUBLANES, :] = prev_ref[0]

    xbuf[SUBLANES:SUBLANES + rows, :] = qkv_ref[...]
    base = SUBLANES - (GDN_CONV - 1)
    y = xbuf[base:base + rows, :] * convw_ref[0:1, :]
    for i in range(1, GDN_CONV):
        y = y + xbuf[base + i:base + i + rows, :] * convw_ref[i:i + 1, :]
    y = y * jax.nn.sigmoid(y)
    carry = xbuf[rows:rows + SUBLANES, :]
    xbuf[0:SUBLANES, :] = carry

    row = lax.broadcasted_iota(jnp.int32, (L, L), 0)
    col = lax.broadcasted_iota(jnp.int32, (L, L), 1)
    lower = row >= col
    strict = row > col
    eye = (row == col).astype(F32)
    log_l = int(math.log2(L))
    n_double = log_l - 1
    assert 2 ** log_l == L

    ba = ba_ref[...]
    beta_all = jax.nn.sigmoid(ba)
    a_in = ba + dtb_ref[...]
    softplus = jnp.maximum(a_in, 0.0) + jnp.log(1.0 + jnp.exp(-jnp.abs(a_in)))
    g_all = -jnp.exp(alog_ref[...]) * softplus
    rr = lax.broadcasted_iota(jnp.int32, (rows, rows), 0)
    cc = lax.broadcasted_iota(jnp.int32, (rows, rows), 1)
    same_chunk_lower = jnp.logical_and(rr >= cc, jnp.right_shift(rr, log_l) == jnp.right_shift(cc, log_l))
    g_cum = _dot(same_chunk_lower.astype(F32), g_all, HIGHEST)
    g_cum_t = g_cum.T

    heads = range(GDN_HEADS)
    chains = [(c, h) for c in range(nch) for h in heads]
    ids = range(len(chains))

    def tok(c):
        return slice(c * L, (c + 1) * L)

    def head_cols(part, c, h):
        return y[tok(c), part * GDN_WIDTH + h * HEAD_DIM:part * GDN_WIDTH + (h + 1) * HEAD_DIM]

    q = [head_cols(0, c, h) for c, h in chains]
    k = [head_cols(1, c, h) for c, h in chains]
    v = [head_cols(2, c, h) for c, h in chains]
    q = [a * lax.rsqrt(jnp.sum(a * a, axis=-1, keepdims=True) + 1e-6) * (HEAD_DIM ** -0.5) for a in q]
    k = [a * lax.rsqrt(jnp.sum(a * a, axis=-1, keepdims=True) + 1e-6) for a in k]
    beta = [beta_all[tok(c), h:h + 1] for c, h in chains]
    g_col = [g_cum[tok(c), GDN_HEADS + h:GDN_HEADS + h + 1] for c, h in chains]
    g_row = [g_cum_t[GDN_HEADS + h:GDN_HEADS + h + 1, tok(c)] for c, h in chains]
    decay = [jnp.where(lower, jnp.exp(jnp.where(lower, g_col[n] - g_row[n], 0.0)), 0.0) for n in ids]
    g_last = [g_col[n][L - 1:L, :] for n in ids]
    e_col = [jnp.exp(g_col[n]) for n in ids]
    qk_kk = [_dot_nt(jnp.concatenate([q[n], k[n]], axis=0).astype(BF16), k[n].astype(BF16)) for n in ids]
    x = [-jnp.where(strict, beta[n] * qk_kk[n][L:] * decay[n], 0.0) for n in ids]
    t_inv = [eye + x[n] for n in ids]
    x_parts = [_split_bf16(x[n]) for n in ids]
    xp = [_dot3_parts(x_parts[n], x_parts[n], _dot) for n in ids]
    for _ in range(n_double - 1):
        xp_parts = [_split_bf16(xp[n]) for n in ids]
        both = [_dot3_parts(_cat_parts(_split_bf16(t_inv[n]), xp_parts[n]), xp_parts[n], _dot) for n in ids]
        t_inv = [t_inv[n] + both[n][:L] for n in ids]
        xp = [both[n][L:] for n in ids]
    t_inv = [t_inv[n] + _dot3(t_inv[n], xp[n]) for n in ids]
    sol = [_dot3(t_inv[n], jnp.concatenate([v[n] * beta[n], k[n] * (beta[n] * e_col[n])], axis=-1)) for n in ids]
    qk = [jnp.where(lower, qk_kk[n][:L] * decay[n], 0.0) for n in ids]
    k_dec_t = [(k[n] * jnp.exp(g_last[n] - g_col[n])).T for n in ids]
    left_s = [jnp.concatenate([sol[n][:, HEAD_DIM:], q[n] * e_col[n]], axis=0).astype(BF16) for n in ids]
    left_v = [jnp.concatenate([qk[n], k_dec_t[n]], axis=0).astype(BF16) for n in ids]

    s = [s_scr[h] for h in heads]
    for c in range(nch):
        n0 = c * GDN_HEADS
        ws_qs = [_dot(left_s[n0 + h], s[h].astype(BF16)) for h in heads]
        v_new = [sol[n0 + h][:, :HEAD_DIM] - ws_qs[h][:L] for h in heads]
        tail = [_dot(left_v[n0 + h], v_new[h].astype(BF16)) for h in heads]
        s = [s[h] * jnp.exp(g_last[n0 + h]) + tail[h][L:] for h in heads]
        for h in heads:
            zf = z_ref[tok(c), h * HEAD_DIM:(h + 1) * HEAD_DIM]
            o = _rms(ws_qs[h][L:] + tail[h][:L], ng_ref[...]) * (zf * jax.nn.sigmoid(zf))
            o_ref[tok(c), h * HEAD_DIM:(h + 1) * HEAD_DIM] = o.astype(o_ref.dtype)
    for h in heads:
        s_scr[h] = s[h]

    @pl.when(step == pl.num_programs(1) - 1)
    def _():
        sfin_ref[0] = s_scr[...]


def _decay_lanes(p):
    return jnp.zeros((1, LANES), F32).at[0, GDN_HEADS:2 * GDN_HEADS].set(p.astype(F32))


def _gdn(main, tail, conv_prev8, s0, conv_w, a_log, dt_bias, norm_g, *, bsz, t):
    L = min(CHUNK, t)
    nch = GDN_CHUNKS_PER_STEP if (t // L) % GDN_CHUNKS_PER_STEP == 0 else 1
    nc = t // (L * nch)
    rows = L * nch
    conv_dim = 3 * GDN_WIDTH
    kern = functools.partial(_gdn_kernel, L=L, nch=nch)
    o, s_fin = pl.pallas_call(
        kern,
        out_shape=(jax.ShapeDtypeStruct((bsz * t, GDN_WIDTH), BF16),
                   jax.ShapeDtypeStruct((bsz, GDN_HEADS, HEAD_DIM, HEAD_DIM), F32)),
        grid=(bsz, nc),
        in_specs=[
            pl.BlockSpec((rows, conv_dim), lambda b, c: (b * nc + c, 0)),
            pl.BlockSpec((rows, GDN_WIDTH), lambda b, c: (b * nc + c, conv_dim // GDN_WIDTH)),
            pl.BlockSpec((rows, LANES), lambda b, c: (b * nc + c, COL_BA_TAIL // LANES)),
            pl.BlockSpec((1, SUBLANES, conv_dim), lambda b, c: (b, 0, 0)),
            pl.BlockSpec((1, GDN_HEADS, HEAD_DIM, HEAD_DIM), lambda b, c: (b, 0, 0, 0)),
            pl.BlockSpec((GDN_CONV, conv_dim), lambda b, c: (0, 0)),
            pl.BlockSpec((1, LANES), lambda b, c: (0, 0)),
            pl.BlockSpec((1, LANES), lambda b, c: (0, 0)),
            pl.BlockSpec((1, HEAD_DIM), lambda b, c: (0, 0)),
        ],
        out_specs=(pl.BlockSpec((rows, GDN_WIDTH), lambda b, c: (b * nc + c, 0)),
                   pl.BlockSpec((1, GDN_HEADS, HEAD_DIM, HEAD_DIM), lambda b, c: (b, 0, 0, 0))),
        scratch_shapes=[pltpu.VMEM((GDN_HEADS, HEAD_DIM, HEAD_DIM), F32),
                        pltpu.VMEM((SUBLANES + rows + SUBLANES, conv_dim), F32)],
        compiler_params=_cparams(("parallel", "arbitrary")),
        name="gdn",
    )(main, main, tail, conv_prev8, s0, conv_w, _decay_lanes(a_log), _decay_lanes(dt_bias),
      norm_g.reshape(1, -1))
    return o, s_fin


def _lambda_value(lam_refs, lam_init):
    lq1, lk1, lq2, lk2 = (r[...] for r in lam_refs)
    return (jnp.exp(jnp.sum(lq1 * lk1, axis=-1, keepdims=True))
            - jnp.exp(jnp.sum(lq2 * lk2, axis=-1, keepdims=True)) + lam_init)


def _softmax_update(m_scr, l_scr, acc_scr, scores, values):
    idx = range(len(scores))
    m_old = [m_scr[n] for n in idx]
    m_new = [jnp.maximum(m_old[n], jnp.max(scores[n], axis=-1, keepdims=True)) for n in idx]
    p = [jnp.exp(scores[n] - m_new[n]) for n in idx]
    alpha = [jnp.exp(m_old[n] - m_new[n]) for n in idx]
    pv = [_dot(p[n].astype(BF16), values[n]) for n in idx]
    for n in idx:
        l_scr[n] = alpha[n] * l_scr[n] + jnp.sum(p[n], axis=-1, keepdims=True)
        acc_scr[n] = alpha[n] * acc_scr[n] + pv[n]
        m_scr[n] = m_new[n]


def _diff_finalize(o1, o2, lam_refs, g_ref, lam_init):
    lam = _lambda_value(lam_refs, lam_init)
    return _rms(o1 - lam * o2, g_ref[...]) * (1.0 - lam_init)


def _diff_init(m_scr, l_scr, acc_scr):
    m_scr[...] = jnp.full(m_scr.shape, -jnp.inf, F32)
    l_scr[...] = jnp.zeros(l_scr.shape, F32)
    acc_scr[...] = jnp.zeros(acc_scr.shape, F32)


def _near_bias(slope, r, q_pos, k_pos):
    visible = jnp.right_shift(k_pos, CHUNK_SHIFT) <= jnp.right_shift(q_pos, CHUNK_SHIFT)
    return jnp.where(visible, slope * (r - jnp.abs(q_pos - k_pos)).astype(F32), -jnp.inf)


def _split_distance(slope, d):
    return (-(slope * CHUNK) * jnp.right_shift(d, CHUNK_SHIFT).astype(F32),
            -slope * jnp.bitwise_and(d, CHUNK - 1).astype(F32))


def _diff_prompt_kernel(i_tab, j_tab, slopes_ref, q_ref, k_ref, v_ref, lq1, lk1, lq2, lk2, g_ref, o_ref,
                        m_scr, l_scr, acc_scr, lhs_scr, s_even, s_odd, *, tq, lam_init):
    h = pl.program_id(1)
    s_id = pl.program_id(2)
    n_pairs = pl.num_programs(2) - 1
    score_on = s_id < n_pairs
    consume_on = s_id >= 1
    i = i_tab[s_id]
    j = j_tab[s_id]
    ci = i_tab[jnp.maximum(s_id - 1, 0)]
    cj = j_tab[jnp.maximum(s_id - 1, 0)]
    slope = slopes_ref[h]
    half = DIFF_HEAD_DIM
    lane = lax.broadcasted_iota(jnp.int32, (tq, LANES), 1)

    @pl.when(jnp.logical_and(consume_on, cj == 0))
    def _():
        _diff_init(m_scr, l_scr, acc_scr)

    @pl.when(jnp.logical_and(score_on, j == 0))
    def _():
        qs = q_ref[...] * (DIFF_HEAD_DIM ** -0.5)
        lhs_scr[0] = jnp.where(lane < half, qs, 1.0).astype(BF16)
        lhs_scr[1] = jnp.where(lane >= half, qs, 1.0).astype(BF16)

    sub = min(DIFF_SUB, tq)
    units = [(n, qb) for qb in range(tq // sub) for n in range(2)]

    def score_tile(near, dst):
        kb = k_ref[...]
        if near:
            aug0 = aug1 = jnp.zeros_like(kb)
        else:
            d = (i - j) * tq - lax.broadcasted_iota(jnp.int32, (tq, LANES), 0)
            t_hi, t_lo = _split_distance(slope, d)
            aug0 = jnp.where(lane == half, t_hi, jnp.where(lane == half + 1, t_lo, 0.0))
            aug1 = jnp.where(lane == 0, t_hi, jnp.where(lane == 1, t_lo, 0.0))
        keys = [jnp.where(lane < half, kb, aug0).astype(BF16), jnp.where(lane >= half, kb, aug1).astype(BF16)]
        bias = None
        for n, qb in units:
            cols = slice(qb * sub, (qb + 1) * sub)
            if near:
                live = (qb + 1) * sub
                if n == 0:
                    k_pos = lax.broadcasted_iota(jnp.int32, (live, sub), 0)
                    q_pos = qb * sub + lax.broadcasted_iota(jnp.int32, (live, sub), 1)
                    bias = _near_bias(slope, q_pos, q_pos, k_pos)
                dst[n, :live, cols] = _dot_nt(keys[n][:live], lhs_scr[n, cols, :]) + bias
            else:
                dst[n, :, cols] = _dot_nt(keys[n], lhs_scr[n, cols, :])

    def consume_tile(near, src):
        vt = v_ref[...].T.astype(BF16)
        idx = range(len(units))
        cols = [slice(qb * sub, (qb + 1) * sub) for _, qb in units]
        live = [(qb + 1) * sub if near else tq for _, qb in units]
        s = [src[n, :live[u], cols[u]] for u, (n, _) in enumerate(units)]
        m_old = [m_scr[n, :, cols[u]] for u, (n, _) in enumerate(units)]
        m_new = [jnp.maximum(m_old[u], jnp.max(s[u], axis=0, keepdims=True)) for u in idx]
        p = [jnp.exp(s[u] - m_new[u]) for u in idx]
        alpha = [jnp.exp(m_old[u] - m_new[u]) for u in idx]
        pv = [_dot(vt[:, :live[u]], p[u].astype(BF16)) for u in idx]
        for u, (n, _) in enumerate(units):
            l_scr[n, :, cols[u]] = alpha[u] * l_scr[n, :, cols[u]] + jnp.sum(p[u], axis=0, keepdims=True)
            acc_scr[n, :, cols[u]] = alpha[u] * acc_scr[n, :, cols[u]] + pv[u]
            m_scr[n, :, cols[u]] = m_new[u]

    def stage(cond, score_near=None, consume_near=None):
        for parity, (dst, src) in enumerate(((s_even, s_odd), (s_odd, s_even))):
            @pl.when(jnp.logical_and(cond, s_id % 2 == parity))
            def _(dst=dst, src=src):
                if score_near is not None:
                    score_tile(score_near, dst)
                if consume_near is not None:
                    consume_tile(consume_near, src)

    both = jnp.logical_and(score_on, consume_on)
    stage(jnp.logical_and(both, jnp.logical_and(j < i, cj < ci)), score_near=False, consume_near=False)
    stage(jnp.logical_and(both, jnp.logical_and(j < i, cj == ci)), score_near=False, consume_near=True)
    stage(jnp.logical_and(both, j == i), score_near=True, consume_near=False)
    stage(jnp.logical_not(consume_on), score_near=True)
    stage(jnp.logical_not(score_on), consume_near=True)

    @pl.when(jnp.logical_and(consume_on, cj == ci))
    def _():
        o_ref[...] = _diff_finalize((acc_scr[0] / l_scr[0]).T, (acc_scr[1] / l_scr[1]).T,
                                    (lq1, lk1, lq2, lk2), g_ref, lam_init).astype(o_ref.dtype)


def _alibi_slopes():
    return jnp.asarray([2.0 ** (-8.0 * (i + 1) / DIFF_HEADS) for i in range(DIFF_HEADS)], F32)


def _diff_prompt(main, k_new, v_new, lams, subln_g, *, bsz, t, lam_init):
    tq = min(DIFF_TQ, t)
    nq = t // tq
    assert tq % CHUNK == 0 and t <= CHUNK * 255
    kern = functools.partial(_diff_prompt_kernel, tq=tq, lam_init=lam_init)
    qc, kc, vc = COL_DQ_MAIN // LANES, 0, 0
    pairs = [(i, j) for i in range(nq) for j in range(i + 1)]
    pairs.append(pairs[-1])
    i_tab = jnp.asarray([p[0] for p in pairs], jnp.int32)
    j_tab = jnp.asarray([p[1] for p in pairs], jnp.int32)
    zero = lambda b, h, s, it, jt: (0, 0)

    def consumed(tab, s):
        return tab[jnp.maximum(s - 1, 0)]

    grid_spec = pltpu.PrefetchScalarGridSpec(
        num_scalar_prefetch=2,
        grid=(bsz, DIFF_HEADS, len(pairs)),
        in_specs=[
            pl.BlockSpec(memory_space=pltpu.SMEM),
            pl.BlockSpec((tq, LANES), lambda b, h, s, it, jt: (b * nq + it[s], qc + h)),
            pl.BlockSpec((tq, LANES), lambda b, h, s, it, jt: (b * nq + jt[s], kc + h)),
            pl.BlockSpec((tq, LANES), lambda b, h, s, it, jt: (b * nq + consumed(jt, s), vc + h)),
        ] + [pl.BlockSpec((1, DIFF_HEAD_DIM), zero)] * 4 + [pl.BlockSpec((1, HEAD_DIM), zero)],
        out_specs=pl.BlockSpec((tq, LANES), lambda b, h, s, it, jt: (b * nq + consumed(it, s), h)),
        scratch_shapes=[pltpu.VMEM((2, 1, tq), F32), pltpu.VMEM((2, 1, tq), F32),
                        pltpu.VMEM((2, HEAD_DIM, tq), F32), pltpu.VMEM((2, tq, LANES), BF16),
                        pltpu.VMEM((2, tq, tq), F32), pltpu.VMEM((2, tq, tq), F32)],
    )
    return pl.pallas_call(
        kern,
        out_shape=jax.ShapeDtypeStruct((bsz * t, DIFF_HEADS * HEAD_DIM), BF16),
        grid_spec=grid_spec,
        compiler_params=_cparams(("parallel", "parallel", "arbitrary")),
        name="diff_attn_prompt",
    )(i_tab, j_tab, _alibi_slopes(), main, k_new, v_new, *lams, subln_g.reshape(1, -1))


def _diff_sample_kernel(slopes_ref, q_ref, kn_ref, vn_ref, kp_ref, vp_ref, lq1, lk1, lq2, lk2, g_ref, o_ref,
                        m_scr, l_scr, acc_scr, lhs_scr, *, t, tk, past, lam_init):
    j = pl.program_id(1)
    heads = range(DIFF_HEADS)

    def cols(h):
        return slice(h * HEAD_DIM, (h + 1) * HEAD_DIM)

    @pl.when(j == 0)
    def _():
        _diff_init(m_scr, l_scr, acc_scr)
        lane = lax.broadcasted_iota(jnp.int32, (t, LANES), 1)
        r = lax.broadcasted_iota(jnp.int32, (t, t), 0)
        c = lax.broadcasted_iota(jnp.int32, (t, t), 1)
        scores = []
        for h in heads:
            qs = q_ref[:, cols(h)] * (DIFF_HEAD_DIM ** -0.5)
            lhs_scr[h, 0:t, :] = jnp.where(lane < DIFF_HEAD_DIM, qs, 0.0).astype(BF16)
            lhs_scr[h, t:2 * t, :] = jnp.where(lane >= DIFF_HEAD_DIM, qs, 0.0).astype(BF16)
            bias = _near_bias(slopes_ref[h], r, past + r, past + c)
            scores.append(_dot_nt(lhs_scr[h], kn_ref[:, cols(h)].astype(BF16)) + jnp.concatenate([bias, bias], axis=0))
        _softmax_update(m_scr, l_scr, acc_scr, scores, [vn_ref[:, cols(h)].astype(BF16) for h in heads])

    d = (past - j * tk - lax.broadcasted_iota(jnp.int32, (1, tk), 1)).astype(F32)
    kt = jnp.swapaxes(kp_ref[...], 0, 1).astype(BF16)
    vt = jnp.swapaxes(vp_ref[...], 0, 1).astype(BF16)
    scores = [_dot_nt(lhs_scr[h], kt[h]) - slopes_ref[h] * d for h in heads]
    _softmax_update(m_scr, l_scr, acc_scr, scores, [vt[h] for h in heads])

    @pl.when(j == pl.num_programs(1) - 1)
    def _():
        for h in heads:
            o = acc_scr[h] / l_scr[h]
            o_ref[:, cols(h)] = _diff_finalize(o[:t], o[t:], (lq1, lk1, lq2, lk2), g_ref, lam_init).astype(o_ref.dtype)


def _diff_sample(main, k_new, v_new, k_past, v_past, lams, subln_g, *, bsz, t, lam_init):
    past = k_past.shape[1]
    tk = min(1024, past)
    nk = past // tk
    width = DIFF_HEADS * HEAD_DIM
    kern = functools.partial(_diff_sample_kernel, t=t, tk=tk, past=past, lam_init=lam_init)
    zero = lambda b, j: (0, 0)
    cache = pl.BlockSpec((None, tk, DIFF_HEADS, HEAD_DIM), lambda b, j: (b, j, 0, 0))
    return pl.pallas_call(
        kern,
        out_shape=jax.ShapeDtypeStruct((bsz * t, width), BF16),
        grid=(bsz, nk),
        in_specs=[
            pl.BlockSpec(memory_space=pltpu.SMEM),
            pl.BlockSpec((t, width), lambda b, j: (b, COL_DQ_MAIN // width)),
            pl.BlockSpec((t, width), lambda b, j: (b, 0)),
            pl.BlockSpec((t, width), lambda b, j: (b, 0)),
            cache, cache,
        ] + [pl.BlockSpec((1, DIFF_HEAD_DIM), zero)] * 4 + [pl.BlockSpec((1, HEAD_DIM), zero)],
        out_specs=pl.BlockSpec((t, width), lambda b, j: (b, 0)),
        scratch_shapes=[pltpu.VMEM((DIFF_HEADS, 2 * t, 1), F32), pltpu.VMEM((DIFF_HEADS, 2 * t, 1), F32),
                        pltpu.VMEM((DIFF_HEADS, 2 * t, HEAD_DIM), F32), pltpu.VMEM((DIFF_HEADS, 2 * t, LANES), BF16)],
        compiler_params=_cparams(("parallel", "arbitrary")),
        name="diff_attn_sample",
    )(_alibi_slopes(), main, k_new, v_new, k_past, v_past, *lams, subln_g.reshape(1, -1))


def _merge_kernel(oa_ref, ob_ref, wa_ref, wb_ref, ga_ref, gb_ref, o_ref):
    ya = _dot(oa_ref[...], wa_ref[...])
    yb = _dot(ob_ref[...], wb_ref[...])
    o_ref[...] = (jax.nn.sigmoid(ga_ref[...]) * ya + jax.nn.sigmoid(gb_ref[...]) * yb).astype(o_ref.dtype)


def _merge(o_a, o_b, wa, wb, tail, *, tm):
    m = o_a.shape[0]
    d = wa.shape[1]
    tn = MM_TN
    ga0, gb0 = 0, COL_GB_TAIL // tn
    return pl.pallas_call(
        _merge_kernel,
        out_shape=jax.ShapeDtypeStruct((m, d), BF16),
        grid=(m // tm, d // tn),
        in_specs=[
            pl.BlockSpec((tm, o_a.shape[1]), lambda i, j: (i, 0)),
            pl.BlockSpec((tm, o_b.shape[1]), lambda i, j: (i, 0)),
            pl.BlockSpec((wa.shape[0], tn), lambda i, j: (0, j)),
            pl.BlockSpec((wb.shape[0], tn), lambda i, j: (0, j)),
            pl.BlockSpec((tm, tn), lambda i, j: (i, ga0 + j)),
            pl.BlockSpec((tm, tn), lambda i, j: (i, gb0 + j)),
        ],
        out_specs=pl.BlockSpec((tm, tn), lambda i, j: (i, j)),
        compiler_params=_cparams(("parallel", "arbitrary")),
        name="merge",
    )(o_a, o_b, wa, wb, tail, tail)


def _memory_kv_kernel(x_ref, g_ref, wk_ref, wv_ref, k_ref, v_ref, h_scr, *, nk):
    j = pl.program_id(0)

    @pl.when(j == 0)
    def _():
        h_scr[...] = _rms(x_ref[...], g_ref[...]).astype(BF16)

    @pl.when(j < nk)
    def _():
        k_ref[...] = _dot(h_scr[...], wk_ref[...].astype(BF16))

    @pl.when(j >= nk)
    def _():
        v_ref[...] = _dot(h_scr[...], wv_ref[...].astype(BF16))


def _memory_kv(mem, w_k, w_v, norm_g, *, tn):
    m, d = mem.shape
    nk = d // tn
    first = lambda j: (0, jnp.minimum(j, nk - 1))
    second = lambda j: (0, jnp.maximum(j - nk, 0))
    return pl.pallas_call(
        functools.partial(_memory_kv_kernel, nk=nk),
        out_shape=(jax.ShapeDtypeStruct((m, d), F32), jax.ShapeDtypeStruct((m, d), F32)),
        grid=(2 * nk,),
        in_specs=[pl.BlockSpec((m, d), lambda j: (0, 0)), pl.BlockSpec((1, d), lambda j: (0, 0)),
                  pl.BlockSpec((d, tn), first), pl.BlockSpec((d, tn), second)],
        out_specs=(pl.BlockSpec((m, tn), first), pl.BlockSpec((m, tn), second)),
        scratch_shapes=[pltpu.VMEM((m, d), BF16)],
        compiler_params=_cparams(("arbitrary",)),
        name="mem_kv",
    )(mem, norm_g.reshape(1, d).astype(F32), w_k, w_v)


def _cross_kernel(q_ref, k_ref, v_ref, o_ref):
    dh = q_ref.shape[-1] // MEM_HEADS
    for h in range(MEM_HEADS):
        sl = slice(h * dh, (h + 1) * dh)
        s = _dot_nt(q_ref[:, sl], k_ref[0, :, sl].astype(BF16)) * (dh ** -0.5)
        s = s - jnp.max(s, axis=-1, keepdims=True)
        p = jnp.exp(s)
        p = p / jnp.sum(p, axis=-1, keepdims=True)
        o_ref[:, sl] = _dot(p.astype(BF16), v_ref[0, :, sl].astype(BF16)).astype(o_ref.dtype)


def _cross_attend(q, mem_k, mem_v, *, bsz, t, tm):
    d = q.shape[1]
    nt = t // tm
    mt = mem_k.shape[1]
    return pl.pallas_call(
        _cross_kernel,
        out_shape=jax.ShapeDtypeStruct(q.shape, BF16),
        grid=(bsz, nt),
        in_specs=[
            pl.BlockSpec((tm, d), lambda b, i: (b * nt + i, 0)),
            pl.BlockSpec((1, mt, d), lambda b, i: (b, 0, 0)),
            pl.BlockSpec((1, mt, d), lambda b, i: (b, 0, 0)),
        ],
        out_specs=pl.BlockSpec((tm, d), lambda b, i: (b * nt + i, 0)),
        compiler_params=_cparams(("parallel", "arbitrary")),
        name="cross_attn",
    )(q, mem_k, mem_v)


def _topk_rows(work, cidx, k):
    n, tn = work.shape
    rid = lax.broadcasted_iota(jnp.int32, (n, tn), 0).astype(F32)
    kid = lax.broadcasted_iota(jnp.int32, (k, tn), 0)
    vals = jnp.zeros((k, tn), F32)
    idxs = jnp.zeros((k, tn), F32)
    for t in range(k):
        m = jnp.max(work, axis=0, keepdims=True)
        pos = jnp.min(jnp.where(work == m, rid, float(n)), axis=0, keepdims=True)
        hit = rid == pos
        if cidx is None:
            picked = pos
        else:
            picked = jnp.sum(jnp.where(hit, cidx, 0.0), axis=0, keepdims=True)
        vals = jnp.where(kid == t, m, vals)
        idxs = jnp.where(kid == t, picked, idxs)
        work = jnp.where(hit, -jnp.inf, work)
    return vals, idxs


def _product_candidates(v1, i1, v2, i2, k):
    assert k == 2 * SUBLANES
    sub = lax.broadcasted_iota(jnp.int32, (SUBLANES, v1.shape[1]), 0)
    vals, idxs = [], []
    for a in range(k // 2):
        nb = k // (a + 1)
        width = k if nb > SUBLANES else SUBLANES
        cv = v1[a:a + 1, :] + v2[:width, :]
        ci = i1[a:a + 1, :] * PEER_N_KEYS + i2[:width, :]
        if nb < SUBLANES:
            cv = jnp.where(sub < nb, cv, -jnp.inf)
        vals.append(cv)
        idxs.append(ci)
    vals.append(v1[k // 2:, :] + v2[0:1, :])
    idxs.append(i1[k // 2:, :] * PEER_N_KEYS + i2[0:1, :])
    return jnp.concatenate(vals, axis=0), jnp.concatenate(idxs, axis=0)


def _peer_route_kernel(q_ref, keys_hi_ref, keys_lo_ref, e_ref, g_ref):
    k = PEER_TOPK
    for h in range(PEER_HEADS):
        sub = []
        for p in range(2):
            c0 = (h * 2 + p) * PEER_N_KEYS
            q_parts = _split_bf16(q_ref[:, c0:c0 + PEER_N_KEYS])
            s = _dot3_parts((keys_hi_ref[h, p], keys_lo_ref[h, p]), q_parts, _dot_nt)
            sub.append(_topk_rows(s, None, k))
        (v1, i1), (v2, i2) = sub
        cand, cidx = _product_candidates(v1, i1, v2, i2, k)
        best, eidx = _topk_rows(cand, cidx, k)
        ex = jnp.exp(best - jnp.max(best, axis=0, keepdims=True))
        gate = ex / jnp.sum(ex, axis=0, keepdims=True)
        e_ref[h * k:(h + 1) * k, :] = eidx.astype(jnp.int32)
        g_ref[h * k:(h + 1) * k, :] = gate


PEER_GATE_TN = 256


def _peer_gate_kernel(et_ref, gt_ref, o_ref, e_scr, g_scr):
    tg = et_ref.shape[1]
    e_scr[...] = et_ref[...].T
    g_scr[...] = gt_ref[...].T
    kid = lax.broadcasted_iota(jnp.int32, (PEER_N_KEYS, PEER_N_KEYS), 0)
    group = 2 * SUBLANES

    def body(n, carry):
        rows = pl.ds(pl.multiple_of(n * group, group), group)
        e16 = e_scr[rows, :]
        g16 = g_scr[rows, :]
        a16 = jnp.right_shift(e16, 7)
        b16 = jnp.bitwise_and(e16, PEER_N_KEYS - 1)
        toks = range(group)
        at = [jnp.where(a16[s:s + 1, :] == kid, g16[s:s + 1, :], 0.0).astype(BF16) for s in toks]
        bt = [jnp.where(b16[s:s + 1, :] == kid, 1.0, 0.0).astype(BF16) for s in toks]
        grids = [_dot_nt(at[s], bt[s]) for s in toks]
        o_ref[:, rows, :] = jnp.swapaxes(jnp.stack(grids, axis=0), 0, 1).astype(BF16)
        return carry

    lax.fori_loop(0, tg // group, body, 0, unroll=8)


def _peer_route(qp, sub_keys, *, tn):
    n = qp.shape[0]
    rows = PEER_HEADS * PEER_TOPK
    keys_hi, keys_lo = _split_bf16(sub_keys.astype(F32))
    keys_spec = pl.BlockSpec(sub_keys.shape, lambda i: (0, 0, 0, 0))
    e_t, g_t = pl.pallas_call(
        _peer_route_kernel,
        out_shape=(jax.ShapeDtypeStruct((rows, n), jnp.int32), jax.ShapeDtypeStruct((rows, n), F32)),
        grid=(n // tn,),
        in_specs=[pl.BlockSpec((tn, qp.shape[1]), lambda i: (i, 0)), keys_spec, keys_spec],
        out_specs=(pl.BlockSpec((rows, tn), lambda i: (0, i)), pl.BlockSpec((rows, tn), lambda i: (0, i))),
        compiler_params=_cparams(("parallel",)),
        name="peer_route",
    )(qp, keys_hi, keys_lo)
    tg = min(PEER_GATE_TN, n)
    return pl.pallas_call(
        _peer_gate_kernel,
        out_shape=jax.ShapeDtypeStruct((PEER_N_KEYS, n, PEER_N_KEYS), BF16),
        grid=(n // tg,),
        in_specs=[pl.BlockSpec((rows, tg), lambda i: (0, i)), pl.BlockSpec((rows, tg), lambda i: (0, i))],
        out_specs=pl.BlockSpec((PEER_N_KEYS, tg, PEER_N_KEYS), lambda i: (0, i, 0)),
        scratch_shapes=[pltpu.VMEM((tg, rows), jnp.int32), pltpu.VMEM((tg, rows), F32)],
        compiler_params=_cparams(("parallel",)),
        name="peer_gates",
    )(e_t, g_t)


def _gelu_tanh(x):
    return 0.5 * x * (1.0 + jnp.tanh(math.sqrt(2.0 / math.pi) * (x + 0.044715 * (x * x * x))))


PEER_BLOCKS_PER_STEP = 4
PEER_TN = 1024
PEER_VMEM_LIMIT = 60 * 1024 * 1024


def _peer_expert_kernel(x_ref, gn_ref, gate_ref, u_ref, v_ref, gf_ref, o_ref, h_scr, acc_scr, coef_scr):
    s = pl.program_id(1)
    last = pl.num_programs(1) - 1

    @pl.when(s == 0)
    def _():
        h_scr[...] = _rms(x_ref[...], gn_ref[...]).astype(BF16)
        acc_scr[...] = jnp.zeros(acc_scr.shape, F32)
        coef_scr[...] = jnp.zeros(coef_scr.shape, BF16)

    def contract():
        acc_scr[...] += _dot(coef_scr[(s + 1) % 2], v_ref[...].astype(BF16))

    @pl.when(s < last)
    def _():
        contract()
        gate = jnp.concatenate([gate_ref[q].astype(F32) for q in range(gate_ref.shape[0])], axis=1)
        hid = _gelu_tanh(_dot_nt(h_scr[...], u_ref[...].astype(BF16)))
        coef_scr[s % 2] = (gate * hid).astype(BF16)

    @pl.when(s == last)
    def _():
        contract()
        o_ref[...] = _rms(x_ref[...] + acc_scr[...], gf_ref[...])


def _peer_experts(x, norm_g, gates, u_tab, v_tab, final_g, *, tn):
    n, d = x.shape
    eb = PEER_N_KEYS
    assert u_tab.shape[0] == eb * eb
    nblk = PEER_BLOCKS_PER_STEP
    ns = eb // nblk

    def cur(s):
        return jnp.minimum(s, ns - 1)

    def prev(s):
        return jnp.maximum(s - 1, 0)

    return pl.pallas_call(
        _peer_expert_kernel,
        out_shape=jax.ShapeDtypeStruct((n, d), F32),
        grid=(n // tn, ns + 1),
        in_specs=[
            pl.BlockSpec((tn, d), lambda i, s: (i, 0), pipeline_mode=pl.Buffered(1)),
            pl.BlockSpec((1, d), lambda i, s: (0, 0)),
            pl.BlockSpec((nblk, tn, eb), lambda i, s: (cur(s), i, 0)),
            pl.BlockSpec((nblk * eb, d), lambda i, s: (cur(s), 0)),
            pl.BlockSpec((nblk * eb, d), lambda i, s: (prev(s), 0)),
            pl.BlockSpec((1, d), lambda i, s: (0, 0)),
        ],
        out_specs=pl.BlockSpec((tn, d), lambda i, s: (i, 0), pipeline_mode=pl.Buffered(1)),
        scratch_shapes=[pltpu.VMEM((tn, d), BF16), pltpu.VMEM((tn, d), F32),
                        pltpu.VMEM((2, tn, nblk * eb), BF16)],
        compiler_params=_cparams(("parallel", "arbitrary"), vmem=PEER_VMEM_LIMIT),
        name="peer_experts",
    )(x, norm_g.reshape(1, d), gates, u_tab, v_tab, final_g.reshape(1, d))


def _layer(x, wts, layer, mem_k, mem_v, conv_prev, gdn_state, k_past, v_past, *, tm):
    bsz, t, d = x.shape
    n = bsz * t
    x2d = x.reshape(n, d)
    main, k_new, v_new, tail = _in_proj(x2d, wts["norm_mix_g"], wts["w_in"], tm=min(IN_TM, n))
    conv_dim = 3 * GDN_WIDTH
    conv_prev8 = jnp.concatenate(
        [jnp.zeros((bsz, SUBLANES - (GDN_CONV - 1), conv_dim), F32), conv_prev.astype(F32)], axis=1)
    o_a, s_new = _gdn(main, tail, conv_prev8, gdn_state.astype(F32), wts["gdn_conv_w"], wts["gdn_a_log"],
                      wts["gdn_dt_bias"], wts["gdn_norm_g"], bsz=bsz, t=t)
    conv_new = main.reshape(bsz, t, -1)[:, t - (GDN_CONV - 1):, :conv_dim]
    lam_init = 0.8 - 0.6 * math.exp(-0.3 * layer)
    lams = tuple(wts[k].reshape(1, -1) for k in ("diff_lambda_q1", "diff_lambda_k1", "diff_lambda_q2", "diff_lambda_k2"))
    if k_past is None:
        o_b = _diff_prompt(main, k_new, v_new, lams, wts["diff_subln_g"], bsz=bsz, t=t, lam_init=lam_init)
    else:
        o_b = _diff_sample(main, k_new, v_new, k_past, v_past, lams, wts["diff_subln_g"], bsz=bsz, t=t,
                           lam_init=lam_init)
    k_rows = k_new.reshape(bsz, t, DIFF_HEADS, HEAD_DIM)
    v_rows = v_new.reshape(bsz, t, DIFF_HEADS, HEAD_DIM)
    merged = _merge(o_a, o_b, wts["w_branch_a"], wts["w_branch_b"], tail, tm=tm)
    x1 = _matmul(merged, wts["w_out"], residual=x2d, tm=tm, tn=MM_TN, name="out_proj")
    qm = _matmul(x1, wts["w_mq"], norm_g=wts["norm_cross_g"], tm=tm, tn=MM_TN, out_dtype=BF16, name="mem_q")
    mt = mem_k.shape[1]
    oc = _cross_attend(qm, mem_k.reshape(bsz, mt, d), mem_v.reshape(bsz, mt, d), bsz=bsz, t=t, tm=min(tm, t))
    x2 = _matmul(oc, wts["w_mo"], residual=x1, tm=tm, tn=MM_TN, name="mem_o")
    qp = _matmul(x2, wts["peer_w_q"], norm_g=wts["norm_ffn_g"], tm=tm, tn=MM_TN, name="peer_q")
    tn = min(PEER_TN, n)
    gates = _peer_route(qp, wts["peer_sub_keys"], tn=LANES)
    y = _peer_experts(x2, wts["norm_ffn_g"], gates, wts["peer_u"], wts["peer_v"], wts["final_norm_g"], tn=tn)
    return y.reshape(bsz, t, d), k_rows, v_rows, s_new, conv_new


def kernel(x_prompt, x_sample, cache_diff_k, cache_diff_v, state_gdn, state_conv, cache_mem_k, cache_mem_v,
           mem_prompt, norm_mix_g, w_in, gdn_conv_w, gdn_a_log, gdn_dt_bias, gdn_norm_g,
           diff_lambda_q1, diff_lambda_k1, diff_lambda_q2, diff_lambda_k2, diff_subln_g,
           w_branch_a, w_branch_b, w_out, norm_cross_g, norm_mem_g, w_mq, w_mk, w_mv, w_mo,
           norm_ffn_g, peer_w_q, peer_sub_keys, peer_u, peer_v, final_norm_g):
    depth = w_in.shape[0]
    assert depth == 1, "final norm is fused into the last layer's PEER kernel"
    l = 0
    wts = {
        "norm_mix_g": norm_mix_g[l], "w_in": _in_proj_weights(w_in[l]), "gdn_conv_w": gdn_conv_w[l],
        "gdn_a_log": gdn_a_log[l], "gdn_dt_bias": gdn_dt_bias[l], "gdn_norm_g": gdn_norm_g[l],
        "diff_lambda_q1": diff_lambda_q1[l], "diff_lambda_k1": diff_lambda_k1[l],
        "diff_lambda_q2": diff_lambda_q2[l], "diff_lambda_k2": diff_lambda_k2[l],
        "diff_subln_g": diff_subln_g[l], "w_branch_a": w_branch_a[l].astype(BF16),
        "w_branch_b": w_branch_b[l].astype(BF16), "w_out": w_out[l].astype(BF16),
        "norm_cross_g": norm_cross_g[l], "w_mq": w_mq[l].astype(BF16), "w_mo": w_mo[l].astype(BF16),
        "norm_ffn_g": norm_ffn_g[l], "peer_w_q": peer_w_q[l].astype(BF16), "peer_sub_keys": peer_sub_keys[l],
        "peer_u": peer_u[l], "peer_v": peer_v[l], "final_norm_g": final_norm_g,
    }
    bp, tp, d = x_prompt.shape
    bs, ts, _ = x_sample.shape
    mem2d = mem_prompt.reshape(-1, d)
    mem_k, mem_v = _memory_kv(mem2d, w_mk[l], w_mv[l], norm_mem_g[l], tn=512)
    mshape = mem_prompt.shape[:2] + (MEM_HEADS, d // MEM_HEADS)
    mem_k = mem_k.reshape(mshape)
    mem_v = mem_v.reshape(mshape)
    conv0 = jnp.zeros((bp, GDN_CONV - 1, 3 * GDN_WIDTH), F32)
    s0 = jnp.zeros((bp, GDN_HEADS, HEAD_DIM, HEAD_DIM), F32)
    yp, pk, pv, ps, pc = _layer(x_prompt, wts, l, mem_k, mem_v, conv0, s0, None, None, tm=min(PROMPT_TM, bp * tp))
    ys, sk, sv, ss, sc = _layer(x_sample, wts, l, cache_mem_k[l], cache_mem_v[l], state_conv[l], state_gdn[l],
                                cache_diff_k[l], cache_diff_v[l], tm=min(SAMPLE_TM, bs * ts))
    return (yp, ys, pk[None], pv[None], ps[None], pc[None], mem_k[None], mem_v[None],
            sk[None], sv[None], ss[None], sc[None])
```
